```python
import math
import jax, jax.numpy as jnp
from jax import lax
import numpy as np

D_MODEL = 1024
BATCH = 8
SEQ = 16384
DEPTH = 2

N_A_LAYERS = DEPTH // 2
N_B_LAYERS = DEPTH - N_A_LAYERS

A_HEADS = 4
A_QK_DIM = D_MODEL // 2 // A_HEADS
A_V_DIM = D_MODEL // A_HEADS
A_CHUNK = 64
GATE_SOFTCAP = 15.0
A_IN_DIM = 2 * A_HEADS * A_QK_DIM + 2 * A_HEADS * A_V_DIM + 2 * A_HEADS

B_GROUPS = ((128, 1), (512, 4), (2048, 16))
N_GROUPS = len(B_GROUPS)
B_HEAD_DIM = 64
B_HEADS = D_MODEL // B_HEAD_DIM
B_BLOCK = 128
REL_BUCKETS = 32
REL_MAX_DIST = 2048

D_FF = 2816
CONV_WIDTH = 3
EPS = 1e-6

kernel_name = "yoco_mlstm_dilated_attn_convffn"


def rmsnorm(x, g):
    xf = x.astype(jnp.float32)
    y = xf * lax.rsqrt(jnp.mean(xf * xf, axis=-1, keepdims=True) + EPS)
    return (y * g.astype(jnp.float32)).astype(x.dtype)


def softcap(z):
    return GATE_SOFTCAP * jnp.tanh(z / GATE_SOFTCAP)


def mlstm_chunkwise(q, k, v, li, lf):
    Bsz, S, H, dk = q.shape
    dv = v.shape[-1]
    L = A_CHUNK
    NC = S // L

    def to_chunks(t):
        return t.reshape(Bsz, NC, L, H, t.shape[-1]).transpose(1, 0, 3, 2, 4)

    def gate_chunks(t):
        return t.reshape(Bsz, NC, L, H).transpose(1, 0, 3, 2)

    causal = jnp.tril(jnp.ones((L, L), dtype=bool))

    def step(carry, xs):
        C, n, m = carry
        qc, kc, vc, lic, lfc = xs
        b = jnp.cumsum(lfc, axis=-1)
        logD = b[..., :, None] - b[..., None, :] + lic[..., None, :]
        logD = jnp.where(causal, logD, -jnp.inf)
        m_inter = b + m[..., None]
        m_t = jnp.maximum(m_inter, jnp.max(logD, axis=-1))
        Sm = jnp.einsum('bhtd,bhsd->bhts', qc, kc) * jnp.exp(logD - m_t[..., None])
        w_inter = jnp.exp(m_inter - m_t)
        num = (jnp.einsum('bhts,bhsv->bhtv', Sm, vc)
               + w_inter[..., None] * jnp.einsum('bhtd,bhdv->bhtv', qc, C))
        den = jnp.sum(Sm, axis=-1) + w_inter * jnp.einsum('bhtd,bhd->bht', qc, n)
        h = num / jnp.maximum(jnp.abs(den), jnp.exp(-m_t))[..., None]
        bL = b[..., -1]
        g = bL[..., None] - b + lic
        m_new = jnp.maximum(bL + m, jnp.max(g, axis=-1))
        wk = jnp.exp(g - m_new[..., None])
        decay = jnp.exp(bL + m - m_new)
        C_new = decay[..., None, None] * C + jnp.einsum('bhs,bhsd,bhsv->bhdv', wk, kc, vc)
        n_new = decay[..., None] * n + jnp.einsum('bhs,bhsd->bhd', wk, kc)
        return (C_new, n_new, m_new), h

    init = (jnp.zeros((Bsz, H, dk, dv), jnp.float32),
            jnp.zeros((Bsz, H, dk), jnp.float32),
            jnp.zeros((Bsz, H), jnp.float32))
    _, hs = lax.scan(step, init, (to_chunks(q), to_chunks(k), to_chunks(v),
                                  gate_chunks(li), gate_chunks(lf)))
    return hs.transpose(1, 0, 3, 2, 4).reshape(Bsz, S, H, dv)


def mlstm_layer(x, norm_g, w_in, b_if, hnorm_g, w_out):
    Bsz, S, _ = x.shape
    nq = A_HEADS * A_QK_DIM
    nv = A_HEADS * A_V_DIM
    z = rmsnorm(x, norm_g) @ w_in
    q, k, v, o, gi, gf = jnp.split(
        z, [nq, 2 * nq, 2 * nq + nv, 2 * nq + 2 * nv, 2 * nq + 2 * nv + A_HEADS], axis=-1)
    q = q.reshape(Bsz, S, A_HEADS, A_QK_DIM).astype(jnp.float32) * (A_QK_DIM ** -0.5)
    k = k.reshape(Bsz, S, A_HEADS, A_QK_DIM).astype(jnp.float32)
    v = v.reshape(Bsz, S, A_HEADS, A_V_DIM).astype(jnp.float32)
    bf = b_if.astype(jnp.float32)
    li = softcap(gi.astype(jnp.float32) + bf[:A_HEADS])
    lf = jax.nn.log_sigmoid(softcap(gf.astype(jnp.float32) + bf[A_HEADS:]))
    h = mlstm_chunkwise(q, k, v, li, lf)
    h = h * lax.rsqrt(jnp.mean(h * h, axis=-1, keepdims=True) + EPS) * hnorm_g.astype(jnp.float32)
    h = h.astype(x.dtype) * jax.nn.sigmoid(o.reshape(Bsz, S, A_HEADS, A_V_DIM))
    return h.reshape(Bsz, S, nv) @ w_out


def conv_ffn(x, norm_g, w_up, conv_w, conv_b, w_down):
    u = rmsnorm(x, norm_g) @ w_up
    up = jnp.pad(u, ((0, 0), (CONV_WIDTH - 1, 0), (0, 0)))
    u = up[:, :-2] * conv_w[0] + up[:, 1:-1] * conv_w[1] + up[:, 2:] * conv_w[2] + conv_b
    gate, val = jnp.split(u, 2, axis=-1)
    return (jax.nn.silu(gate) * val) @ w_down


def t5_bucket(dist):
    max_exact = REL_BUCKETS // 2
    d = np.maximum(dist, 0)
    log_ratio = np.log(np.maximum(d, 1) / max_exact) / math.log(REL_MAX_DIST / max_exact)
    large = np.minimum(max_exact + (log_ratio * (REL_BUCKETS - max_exact)).astype(np.int64),
                       REL_BUCKETS - 1)
    return np.where(d < max_exact, d, large).astype(np.int32)


def dilated_group(q, k, v, bias, win, dil):
    Bsz, S_pad, H, dh = q.shape
    nb = S_pad // (dil * B_BLOCK)

    def blocks(t):
        return t.reshape(Bsz, nb, B_BLOCK, dil, H, dh)

    def with_prev(t):
        prev = jnp.pad(t[:, :-1], ((0, 0), (1, 0), (0, 0), (0, 0), (0, 0), (0, 0)))
        return jnp.concatenate([prev, t], axis=2)

    qb = blocks(q)
    kc, vc = with_prev(blocks(k)), with_prev(blocks(v))
    delta = B_BLOCK + np.arange(B_BLOCK)[:, None] - np.arange(2 * B_BLOCK)[None, :]
    band = (delta >= 0) & (delta <= win // dil)
    first = (np.arange(nb)[:, None, None] == 0) & (np.arange(2 * B_BLOCK)[None, None, :] < B_BLOCK)
    valid = jnp.asarray(band[None] & ~first)[None, :, None, None]
    s = jnp.einsum('bnqrhe,bnkrhe->bnrhqk', qb, kc).astype(jnp.float32) * (dh ** -0.5) + bias
    s = jnp.where(valid, s, -jnp.inf)
    m = jnp.max(s, axis=-1, keepdims=True)
    p = jnp.exp(s - m)
    l = jnp.sum(p, axis=-1)
    o = jnp.einsum('bnrhqk,bnkrhe->bnrhqe', p.astype(vc.dtype), vc).astype(jnp.float32) / l[..., None]
    lse = m[..., 0] + jnp.log(l)
    o = o.transpose(0, 1, 4, 2, 3, 5).reshape(Bsz, S_pad, H, dh)
    lse = lse.transpose(0, 1, 4, 2, 3).reshape(Bsz, S_pad, H)
    return o, lse


def dilated_attention_layer(x, k_sh, v_sh, norm_g, w_q, w_out, rel_bias):
    Bsz, S, _ = x.shape
    q = (rmsnorm(x, norm_g) @ w_q).reshape(Bsz, S, N_GROUPS, B_HEADS, B_HEAD_DIM)
    span = max(d for _, d in B_GROUPS) * B_BLOCK
    S_pad = -(-S // span) * span
    padw = ((0, 0), (0, S_pad - S), (0, 0), (0, 0), (0, 0))
    q, k, v = jnp.pad(q, padw), jnp.pad(k_sh, padw), jnp.pad(v_sh, padw)
    outs, lses = [], []
    for g, (win, dil) in enumerate(B_GROUPS):
        delta = B_BLOCK + np.arange(B_BLOCK)[:, None] - np.arange(2 * B_BLOCK)[None, :]
        bucket = t5_bucket(delta * dil)
        bias = rel_bias[bucket][..., g * B_HEADS:(g + 1) * B_HEADS]
        bias = bias.transpose(2, 0, 1).astype(jnp.float32)
        o, lse = dilated_group(q[:, :, g], k[:, :, g], v[:, :, g], bias, win, dil)
        outs.append(o)
        lses.append(lse)
    w = jax.nn.softmax(jnp.stack(lses), axis=0)
    out = jnp.sum(w[..., None] * jnp.stack(outs), axis=0)[:, :S]
    return out.astype(x.dtype).reshape(Bsz, S, B_HEADS * B_HEAD_DIM) @ w_out


def shared_kv(x, kv_norm_g, w_kv):
    Bsz, S, _ = x.shape
    kv = (rmsnorm(x, kv_norm_g) @ w_kv).reshape(Bsz, S, 2, N_GROUPS, B_HEADS, B_HEAD_DIM)
    return kv[:, :, 0], kv[:, :, 1]


def _fwd_setup_inputs(seed: int = 0) -> dict:
    key = jax.random.key(seed)
    ks = jax.random.split(key, 20)
    f32 = jnp.float32
    nrm = lambda k, s, sc: jax.random.normal(k, s, f32) * sc
    att_w = N_GROUPS * B_HEADS * B_HEAD_DIM
    return {
        "x": nrm(ks[0], (BATCH, SEQ, D_MODEL), 1.0),
        "a_norm_g": 1.0 + nrm(ks[1], (N_A_LAYERS, D_MODEL), 0.02),
        "a_w_in": nrm(ks[2], (N_A_LAYERS, D_MODEL, A_IN_DIM), D_MODEL ** -0.5),
        "a_b_if": jnp.concatenate([nrm(ks[3], (N_A_LAYERS, A_HEADS), 0.1),
                                   3.0 + 3.0 * jax.random.uniform(ks[4], (N_A_LAYERS, A_HEADS), f32)], axis=-1),
        "a_hnorm_g": 1.0 + nrm(ks[5], (N_A_LAYERS, A_HEADS, A_V_DIM), 0.02),
        "a_w_out": nrm(ks[6], (N_A_LAYERS, A_HEADS * A_V_DIM, D_MODEL), (A_HEADS * A_V_DIM) ** -0.5),
        "kv_norm_g": 1.0 + nrm(ks[7], (D_MODEL,), 0.02),
        "w_kv": nrm(ks[8], (D_MODEL, 2 * att_w), D_MODEL ** -0.5),
        "b_norm_g": 1.0 + nrm(ks[9], (N_B_LAYERS, D_MODEL), 0.02),
        "b_w_q": nrm(ks[10], (N_B_LAYERS, D_MODEL, att_w), D_MODEL ** -0.5),
        "b_w_out": nrm(ks[11], (N_B_LAYERS, B_HEADS * B_HEAD_DIM, D_MODEL), (B_HEADS * B_HEAD_DIM) ** -0.5),
        "rel_bias": nrm(ks[12], (REL_BUCKETS, N_GROUPS * B_HEADS), 0.5),
        "f_norm_g": 1.0 + nrm(ks[13], (DEPTH, D_MODEL), 0.02),
        "f_w_up": nrm(ks[14], (DEPTH, D_MODEL, 2 * D_FF), D_MODEL ** -0.5),
        "f_conv_w": nrm(ks[15], (DEPTH, CONV_WIDTH, 2 * D_FF), CONV_WIDTH ** -0.5),
        "f_conv_b": nrm(ks[16], (DEPTH, 2 * D_FF), 0.01),
        "f_w_down": nrm(ks[17], (DEPTH, D_FF, D_MODEL), D_FF ** -0.5),
        "final_norm_g": 1.0 + nrm(ks[18], (D_MODEL,), 0.02),
    }


def _fwd_reference(x, a_norm_g, a_w_in, a_b_if, a_hnorm_g, a_w_out, kv_norm_g, w_kv,
              b_norm_g, b_w_q, b_w_out, rel_bias, f_norm_g, f_w_up, f_conv_w,
              f_conv_b, f_w_down, final_norm_g):
    k_sh = v_sh = None
    for layer in range(DEPTH):
        if layer < N_A_LAYERS:
            x = x + mlstm_layer(x, a_norm_g[layer], a_w_in[layer], a_b_if[layer],
                                a_hnorm_g[layer], a_w_out[layer])
        else:
            j = layer - N_A_LAYERS
            if j == 0:
                k_sh, v_sh = shared_kv(x, kv_norm_g, w_kv)
            x = x + dilated_attention_layer(x, k_sh, v_sh, b_norm_g[j], b_w_q[j],
                                            b_w_out[j], rel_bias)
        x = x + conv_ffn(x, f_norm_g[layer], f_w_up[layer], f_conv_w[layer],
                         f_conv_b[layer], f_w_down[layer])
    return rmsnorm(x, final_norm_g)


import jax as _jax
import jax.numpy as _jnp

TWIN_FORMAT = 'train_step'
FWD_PARAMS = ['x', 'a_norm_g', 'a_w_in', 'a_b_if', 'a_hnorm_g', 'a_w_out', 'kv_norm_g', 'w_kv', 'b_norm_g', 'b_w_q', 'b_w_out', 'rel_bias', 'f_norm_g', 'f_w_up', 'f_conv_w', 'f_conv_b', 'f_w_down', 'final_norm_g']
TWIN_WEIGHTS = ['a_norm_g', 'a_w_in', 'a_b_if', 'a_hnorm_g', 'a_w_out', 'kv_norm_g', 'w_kv', 'b_norm_g', 'b_w_q', 'b_w_out', 'rel_bias', 'f_norm_g', 'f_w_up', 'f_conv_w', 'f_conv_b', 'f_w_down', 'final_norm_g']
TWIN_DIFF_INPUT = 'x'
TWIN_INPUTS = ['x', 'a_norm_g', 'a_w_in', 'a_b_if', 'a_hnorm_g', 'a_w_out', 'kv_norm_g', 'w_kv', 'b_norm_g', 'b_w_q', 'b_w_out', 'rel_bias', 'f_norm_g', 'f_w_up', 'f_conv_w', 'f_conv_b', 'f_w_down', 'final_norm_g', 'loss_target', 'm_a_norm_g', 'm_a_w_in', 'm_a_b_if', 'm_a_hnorm_g', 'm_a_w_out', 'm_kv_norm_g', 'm_w_kv', 'm_b_norm_g', 'm_b_w_q', 'm_b_w_out', 'm_rel_bias', 'm_f_norm_g', 'm_f_w_up', 'm_f_conv_w', 'm_f_conv_b', 'm_f_w_down', 'm_final_norm_g', 'v_a_norm_g', 'v_a_w_in', 'v_a_b_if', 'v_a_hnorm_g', 'v_a_w_out', 'v_kv_norm_g', 'v_w_kv', 'v_b_norm_g', 'v_b_w_q', 'v_b_w_out', 'v_rel_bias', 'v_f_norm_g', 'v_f_w_up', 'v_f_conv_w', 'v_f_conv_b', 'v_f_w_down', 'v_final_norm_g']
TWIN_OUTPUTS = ['loss', 'grad_x', 'grad_a_norm_g', 'grad_a_w_in', 'grad_a_b_if', 'grad_a_hnorm_g', 'grad_a_w_out', 'grad_kv_norm_g', 'grad_w_kv', 'grad_b_norm_g', 'grad_b_w_q', 'grad_b_w_out', 'grad_rel_bias', 'grad_f_norm_g', 'grad_f_w_up', 'grad_f_conv_w', 'grad_f_conv_b', 'grad_f_w_down', 'grad_final_norm_g', 'delta_a_norm_g', 'delta_a_w_in', 'delta_a_b_if', 'delta_a_hnorm_g', 'delta_a_w_out', 'delta_kv_norm_g', 'delta_w_kv', 'delta_b_norm_g', 'delta_b_w_q', 'delta_b_w_out', 'delta_rel_bias', 'delta_f_norm_g', 'delta_f_w_up', 'delta_f_conv_w', 'delta_f_conv_b', 'delta_f_w_down', 'delta_final_norm_g', 'new_m_a_norm_g', 'new_m_a_w_in', 'new_m_a_b_if', 'new_m_a_hnorm_g', 'new_m_a_w_out', 'new_m_kv_norm_g', 'new_m_w_kv', 'new_m_b_norm_g', 'new_m_b_w_q', 'new_m_b_w_out', 'new_m_rel_bias', 'new_m_f_norm_g', 'new_m_f_w_up', 'new_m_f_conv_w', 'new_m_f_conv_b', 'new_m_f_w_down', 'new_m_final_norm_g', 'new_v_a_norm_g', 'new_v_a_w_in', 'new_v_a_b_if', 'new_v_a_hnorm_g', 'new_v_a_w_out', 'new_v_kv_norm_g', 'new_v_w_kv', 'new_v_b_norm_g', 'new_v_b_w_q', 'new_v_b_w_out', 'new_v_rel_bias', 'new_v_f_norm_g', 'new_v_f_w_up', 'new_v_f_conv_w', 'new_v_f_conv_b', 'new_v_f_w_down', 'new_v_final_norm_g']
TWIN_LEAF_KINDS = {'loss': 'loss', 'grad_x': 'grad_x', 'grad_a_norm_g': 'grad_w', 'grad_a_w_in': 'grad_w', 'grad_a_b_if': 'grad_w', 'grad_a_hnorm_g': 'grad_w', 'grad_a_w_out': 'grad_w', 'grad_kv_norm_g': 'grad_w', 'grad_w_kv': 'grad_w', 'grad_b_norm_g': 'grad_w', 'grad_b_w_q': 'grad_w', 'grad_b_w_out': 'grad_w', 'grad_rel_bias': 'grad_w', 'grad_f_norm_g': 'grad_w', 'grad_f_w_up': 'grad_w', 'grad_f_conv_w': 'grad_w', 'grad_f_conv_b': 'grad_w', 'grad_f_w_down': 'grad_w', 'grad_final_norm_g': 'grad_w', 'delta_a_norm_g': 'delta_w', 'delta_a_w_in': 'delta_w', 'delta_a_b_if': 'delta_w', 'delta_a_hnorm_g': 'delta_w', 'delta_a_w_out': 'delta_w', 'delta_kv_norm_g': 'delta_w', 'delta_w_kv': 'delta_w', 'delta_b_norm_g': 'delta_w', 'delta_b_w_q': 'delta_w', 'delta_b_w_out': 'delta_w', 'delta_rel_bias': 'delta_w', 'delta_f_norm_g': 'delta_w', 'delta_f_w_up': 'delta_w', 'delta_f_conv_w': 'delta_w', 'delta_f_conv_b': 'delta_w', 'delta_f_w_down': 'delta_w', 'delta_final_norm_g': 'delta_w', 'new_m_a_norm_g': 'new_m', 'new_m_a_w_in': 'new_m', 'new_m_a_b_if': 'new_m', 'new_m_a_hnorm_g': 'new_m', 'new_m_a_w_out': 'new_m', 'new_m_kv_norm_g': 'new_m', 'new_m_w_kv': 'new_m', 'new_m_b_norm_g': 'new_m', 'new_m_b_w_q': 'new_m', 'new_m_b_w_out': 'new_m', 'new_m_rel_bias': 'new_m', 'new_m_f_norm_g': 'new_m', 'new_m_f_w_up': 'new_m', 'new_m_f_conv_w': 'new_m', 'new_m_f_conv_b': 'new_m', 'new_m_f_w_down': 'new_m', 'new_m_final_norm_g': 'new_m', 'new_v_a_norm_g': 'new_v', 'new_v_a_w_in': 'new_v', 'new_v_a_b_if': 'new_v', 'new_v_a_hnorm_g': 'new_v', 'new_v_a_w_out': 'new_v', 'new_v_kv_norm_g': 'new_v', 'new_v_w_kv': 'new_v', 'new_v_b_norm_g': 'new_v', 'new_v_b_w_q': 'new_v', 'new_v_b_w_out': 'new_v', 'new_v_rel_bias': 'new_v', 'new_v_f_norm_g': 'new_v', 'new_v_f_w_up': 'new_v', 'new_v_f_conv_w': 'new_v', 'new_v_f_conv_b': 'new_v', 'new_v_f_w_down': 'new_v', 'new_v_final_norm_g': 'new_v'}


def _forward(args):
    return _fwd_reference(*[args[k] for k in FWD_PARAMS])


def _output_shape():
    def fwd():
        inp = _fwd_setup_inputs(0)
        return _fwd_reference(*[inp[k] for k in FWD_PARAMS])
    out = _jax.eval_shape(fwd)
    return out.shape, out.dtype

N_MICROBATCH = 1
ADAM_LR = 0.001
ADAM_B1 = 0.9
ADAM_B2 = 0.999
ADAM_EPS = 1e-08
ADAM_WD = 0.01
ADAM_STEP = 10
PER_EXAMPLE_BATCH_AXIS = {'x': 0, 'loss_target': 0}
SHARED_INPUTS = []
_WEIGHT_DTYPES = {'a_norm_g': _jnp.float32, 'a_w_in': _jnp.float32, 'a_b_if': _jnp.float32, 'a_hnorm_g': _jnp.float32, 'a_w_out': _jnp.float32, 'kv_norm_g': _jnp.float32, 'w_kv': _jnp.float32, 'b_norm_g': _jnp.float32, 'b_w_q': _jnp.float32, 'b_w_out': _jnp.float32, 'rel_bias': _jnp.float32, 'f_norm_g': _jnp.float32, 'f_w_up': _jnp.float32, 'f_conv_w': _jnp.float32, 'f_conv_b': _jnp.float32, 'f_w_down': _jnp.float32, 'final_norm_g': _jnp.float32}
MOMENT_SCALE = {'a_norm_g': 7.418346e-01, 'a_w_in': 4.619644e-01, 'a_b_if': 1.018674e+00, 'a_hnorm_g': 2.143827e-01, 'a_w_out': 2.092085e-01, 'kv_norm_g': 8.136175e-02, 'w_kv': 3.184188e-02, 'b_norm_g': 5.256230e-02, 'b_w_q': 2.952804e-02, 'b_w_out': 5.895685e-02, 'rel_bias': 3.660756e-02, 'f_norm_g': 2.491378e-01, 'f_w_up': 1.039084e-01, 'f_conv_w': 1.050255e-01, 'f_conv_b': 1.014254e-01, 'f_w_down': 1.706964e-01, 'final_norm_g': 1.280896e+02}


def _to_microbatches(a, axis):
    t = _jnp.moveaxis(a, axis, 0)
    t = t.reshape((N_MICROBATCH, t.shape[0] // N_MICROBATCH) + t.shape[1:])
    return _jnp.moveaxis(t, 1, axis + 1)


def setup_inputs(seed: int = 0) -> dict:
    inp = _fwd_setup_inputs(seed)
    key = _jax.random.fold_in(_jax.random.key(seed), 7919)
    shape, _ = _output_shape()
    out = dict(inp)
    out["loss_target"] = _jax.random.normal(_jax.random.fold_in(key, 0), shape, _jnp.float32)
    for i, name in enumerate(TWIN_WEIGHTS):
        w = inp[name].astype(_jnp.float32)
        if MOMENT_SCALE is None:
            s = _jnp.sqrt(_jnp.mean(_jnp.square(w)) + 1e-30)
        else:
            s = MOMENT_SCALE[name]
        km, kv = _jax.random.split(_jax.random.fold_in(key, i + 1))
        out[name] = w
        out["m_" + name] = s * _jax.random.normal(km, w.shape, _jnp.float32)
        out["v_" + name] = (s * s) * _jax.random.uniform(kv, w.shape, _jnp.float32, 0.5, 1.5)
    if N_MICROBATCH > 1:
        for name, axis in PER_EXAMPLE_BATCH_AXIS.items():
            out[name] = _to_microbatches(out[name], axis)
    return {'x': out['x'], 'a_norm_g': out['a_norm_g'], 'a_w_in': out['a_w_in'], 'a_b_if': out['a_b_if'], 'a_hnorm_g': out['a_hnorm_g'], 'a_w_out': out['a_w_out'], 'kv_norm_g': out['kv_norm_g'], 'w_kv': out['w_kv'], 'b_norm_g': out['b_norm_g'], 'b_w_q': out['b_w_q'], 'b_w_out': out['b_w_out'], 'rel_bias': out['rel_bias'], 'f_norm_g': out['f_norm_g'], 'f_w_up': out['f_w_up'], 'f_conv_w': out['f_conv_w'], 'f_conv_b': out['f_conv_b'], 'f_w_down': out['f_w_down'], 'final_norm_g': out['final_norm_g'], 'loss_target': out['loss_target'], 'm_a_norm_g': out['m_a_norm_g'], 'm_a_w_in': out['m_a_w_in'], 'm_a_b_if': out['m_a_b_if'], 'm_a_hnorm_g': out['m_a_hnorm_g'], 'm_a_w_out': out['m_a_w_out'], 'm_kv_norm_g': out['m_kv_norm_g'], 'm_w_kv': out['m_w_kv'], 'm_b_norm_g': out['m_b_norm_g'], 'm_b_w_q': out['m_b_w_q'], 'm_b_w_out': out['m_b_w_out'], 'm_rel_bias': out['m_rel_bias'], 'm_f_norm_g': out['m_f_norm_g'], 'm_f_w_up': out['m_f_w_up'], 'm_f_conv_w': out['m_f_conv_w'], 'm_f_conv_b': out['m_f_conv_b'], 'm_f_w_down': out['m_f_w_down'], 'm_final_norm_g': out['m_final_norm_g'], 'v_a_norm_g': out['v_a_norm_g'], 'v_a_w_in': out['v_a_w_in'], 'v_a_b_if': out['v_a_b_if'], 'v_a_hnorm_g': out['v_a_hnorm_g'], 'v_a_w_out': out['v_a_w_out'], 'v_kv_norm_g': out['v_kv_norm_g'], 'v_w_kv': out['v_w_kv'], 'v_b_norm_g': out['v_b_norm_g'], 'v_b_w_q': out['v_b_w_q'], 'v_b_w_out': out['v_b_w_out'], 'v_rel_bias': out['v_rel_bias'], 'v_f_norm_g': out['v_f_norm_g'], 'v_f_w_up': out['v_f_w_up'], 'v_f_conv_w': out['v_f_conv_w'], 'v_f_conv_b': out['v_f_conv_b'], 'v_f_w_down': out['v_f_w_down'], 'v_final_norm_g': out['v_final_norm_g']}


def _loss(weights, diff, rest, loss_target):
    with _jax.named_scope("forward"):
        args = {**rest, TWIN_DIFF_INPUT: diff, **{k: w.astype(_WEIGHT_DTYPES[k]) for k, w in weights.items()}}
        y = _forward(args)
    with _jax.named_scope("loss_head"):
        err = _jnp.square(y.astype(_jnp.float32) - loss_target)
        return 0.5 * _jnp.sum(_jnp.mean(err, axis=-1)) if err.ndim else 0.5 * err


def _adamw(w, g, m, v):
    m = ADAM_B1 * m + (1.0 - ADAM_B1) * g
    v = ADAM_B2 * v + (1.0 - ADAM_B2) * _jnp.square(g)
    m_hat = m / (1.0 - ADAM_B1 ** ADAM_STEP)
    v_hat = v / (1.0 - ADAM_B2 ** ADAM_STEP)
    delta = -ADAM_LR * (m_hat / (_jnp.sqrt(v_hat) + ADAM_EPS) + ADAM_WD * w)
    return delta, m, v


def reference(x, a_norm_g, a_w_in, a_b_if, a_hnorm_g, a_w_out, kv_norm_g, w_kv, b_norm_g, b_w_q, b_w_out, rel_bias, f_norm_g, f_w_up, f_conv_w, f_conv_b, f_w_down, final_norm_g, loss_target, m_a_norm_g, m_a_w_in, m_a_b_if, m_a_hnorm_g, m_a_w_out, m_kv_norm_g, m_w_kv, m_b_norm_g, m_b_w_q, m_b_w_out, m_rel_bias, m_f_norm_g, m_f_w_up, m_f_conv_w, m_f_conv_b, m_f_w_down, m_final_norm_g, v_a_norm_g, v_a_w_in, v_a_b_if, v_a_hnorm_g, v_a_w_out, v_kv_norm_g, v_w_kv, v_b_norm_g, v_b_w_q, v_b_w_out, v_rel_bias, v_f_norm_g, v_f_w_up, v_f_conv_w, v_f_conv_b, v_f_w_down, v_final_norm_g):
    given = dict(x=x, a_norm_g=a_norm_g, a_w_in=a_w_in, a_b_if=a_b_if, a_hnorm_g=a_hnorm_g, a_w_out=a_w_out, kv_norm_g=kv_norm_g, w_kv=w_kv, b_norm_g=b_norm_g, b_w_q=b_w_q, b_w_out=b_w_out, rel_bias=rel_bias, f_norm_g=f_norm_g, f_w_up=f_w_up, f_conv_w=f_conv_w, f_conv_b=f_conv_b, f_w_down=f_w_down, final_norm_g=final_norm_g, loss_target=loss_target, m_a_norm_g=m_a_norm_g, m_a_w_in=m_a_w_in, m_a_b_if=m_a_b_if, m_a_hnorm_g=m_a_hnorm_g, m_a_w_out=m_a_w_out, m_kv_norm_g=m_kv_norm_g, m_w_kv=m_w_kv, m_b_norm_g=m_b_norm_g, m_b_w_q=m_b_w_q, m_b_w_out=m_b_w_out, m_rel_bias=m_rel_bias, m_f_norm_g=m_f_norm_g, m_f_w_up=m_f_w_up, m_f_conv_w=m_f_conv_w, m_f_conv_b=m_f_conv_b, m_f_w_down=m_f_w_down, m_final_norm_g=m_final_norm_g, v_a_norm_g=v_a_norm_g, v_a_w_in=v_a_w_in, v_a_b_if=v_a_b_if, v_a_hnorm_g=v_a_hnorm_g, v_a_w_out=v_a_w_out, v_kv_norm_g=v_kv_norm_g, v_w_kv=v_w_kv, v_b_norm_g=v_b_norm_g, v_b_w_q=v_b_w_q, v_b_w_out=v_b_w_out, v_rel_bias=v_rel_bias, v_f_norm_g=v_f_norm_g, v_f_w_up=v_f_w_up, v_f_conv_w=v_f_conv_w, v_f_conv_b=v_f_conv_b, v_f_w_down=v_f_w_down, v_final_norm_g=v_final_norm_g)
    weights = {n: given[n] for n in TWIN_WEIGHTS}
    shared = {n: given[n] for n in SHARED_INPUTS}
    per_example = {n: given[n] for n in ['x']}
    grad_fn = _jax.value_and_grad(_loss, argnums=(0, 1))

    def one_microbatch(ex, loss_target):
        ex = dict(ex)
        diff = ex.pop(TWIN_DIFF_INPUT)
        return grad_fn(weights, diff, {**shared, **ex}, loss_target)

    if N_MICROBATCH == 1:
        loss, (grad_w, grad_x) = one_microbatch(per_example, given["loss_target"])
    else:
        def body(carry, xs):
            loss_sum, grad_sum = carry
            l_k, (gw_k, gx_k) = one_microbatch(xs[0], xs[1])
            with _jax.named_scope("update"):
                return (loss_sum + l_k, _jax.tree.map(_jnp.add, grad_sum, gw_k)), gx_k

        init = (_jnp.zeros((), _jnp.float32), _jax.tree.map(_jnp.zeros_like, weights))
        (loss, grad_w), grad_x = _jax.lax.scan(body, init, (per_example, given["loss_target"]))
    with _jax.named_scope("update"):
        delta_w, new_m, new_v = {}, {}, {}
        for n in TWIN_WEIGHTS:
            delta_w[n], new_m[n], new_v[n] = _adamw(weights[n], grad_w[n], given["m_" + n], given["v_" + n])
    return (loss, grad_x, *[grad_w[n] for n in TWIN_WEIGHTS], *[delta_w[n] for n in TWIN_WEIGHTS],
            *[new_m[n] for n in TWIN_WEIGHTS], *[new_v[n] for n in TWIN_WEIGHTS])
```

```python
import functools
import math

import numpy as np
import jax
import jax.numpy as jnp
from jax import lax
from jax.experimental import pallas as pl
from jax.experimental.pallas import tpu as pltpu

F32 = jnp.float32
BF16 = jnp.bfloat16
MM_DTYPE = jnp.bfloat16
HI = lax.Precision.HIGHEST

D_MODEL = 1024
A_HEADS = 4
A_QK = 128
A_V = 256
A_CHUNK = 64
A_IN = 3080
A_IN_PAD = 3200
GATE_COL = 3072
SOFTCAP = 15.0
N_GROUPS = 3
B_HEADS = 16
B_DH = 64
B_BLOCK = 128
DILATIONS = (1, 4, 16)
WINDOWS = (128, 512, 2048)
REL_BUCKETS = 32
REL_MAX_DIST = 2048
D_FF = 2816
FF_TC = 256
EPS = 1e-6
ADAM_LR, ADAM_B1, ADAM_B2, ADAM_EPS, ADAM_WD, ADAM_STEP = 0.001, 0.9, 0.999, 1e-08, 0.01, 10

VMEM_LIMIT = 56 * 1024 * 1024
NT_DIMS = (((1,), (1,)), ((), ()))
TN_DIMS = (((0,), (0,)), ((), ()))
MESH_ID = pl.DeviceIdType.MESH


def _params(*sem):
    return pltpu.CompilerParams(dimension_semantics=sem, vmem_limit_bytes=VMEM_LIMIT)


def _tile(n, cap):
    if n <= cap:
        return n
    best = None
    for t in range(128, cap + 1, 128):
        if n % t == 0:
            best = t
    assert best is not None, (n, cap)
    return best


def _rows(n, cap):
    if n <= cap:
        return n
    for t in range(cap // 8 * 8, 7, -8):
        if n % t == 0:
            return t
    raise ValueError((n, cap))


def _dot(a, b):
    return jnp.dot(a.astype(MM_DTYPE), b.astype(MM_DTYPE), preferred_element_type=F32)


def _dot_nt(a, b):
    return lax.dot_general(a.astype(MM_DTYPE), b.astype(MM_DTYPE), NT_DIMS, preferred_element_type=F32)


def _dot_tn(a, b):
    return lax.dot_general(a.astype(MM_DTYPE), b.astype(MM_DTYPE), TN_DIMS, preferred_element_type=F32)


def _sigmoid(x):
    return 1.0 / (1.0 + jnp.exp(-x))


def mm_nn(a, b, name, res=None, out_dtype=F32, exact=False):
    M, K = a.shape
    N = b.shape[1]
    tm, tn, tk = _rows(M, 512), _tile(N, 1536), _tile(K, 1536)
    nk = K // tk

    def body(*refs):
        if res is None:
            a_ref, b_ref, o_ref, acc = refs
            r_ref = None
        else:
            a_ref, b_ref, r_ref, o_ref, acc = refs
        if exact:
            p = jnp.dot(a_ref[...], b_ref[...], precision=HI, preferred_element_type=F32)
        else:
            p = _dot(a_ref[...], b_ref[...])

        def finish(total):
            if r_ref is not None:
                total = total + r_ref[...]
            o_ref[...] = total.astype(out_dtype)

        if nk == 1:
            finish(p)
        else:
            k = pl.program_id(2)

            @pl.when(k == 0)
            def _():
                acc[...] = p

            @pl.when(jnp.logical_and(k > 0, k < nk - 1))
            def _():
                acc[...] += p

            @pl.when(k == nk - 1)
            def _():
                finish(acc[...] + p)

    in_specs = [pl.BlockSpec((tm, tk), lambda i, j, k: (i, k)),
                pl.BlockSpec((tk, tn), lambda i, j, k: (k, j))]
    args = [a, b]
    if res is not None:
        in_specs.append(pl.BlockSpec((tm, tn), lambda i, j, k: (i, j)))
        args.append(res)
    acc_shape = (tm, tn) if nk > 1 else (8, 128)
    return pl.pallas_call(
        body, name=name, grid=(M // tm, N // tn, nk),
        in_specs=in_specs, out_specs=pl.BlockSpec((tm, tn), lambda i, j, k: (i, j)),
        out_shape=jax.ShapeDtypeStruct((M, N), out_dtype),
        scratch_shapes=[pltpu.VMEM(acc_shape, F32)],
        compiler_params=_params("parallel", "parallel", "arbitrary"),
    )(*args)


def mm_tn(a, g, name):
    T, Ka = a.shape
    N = g.shape[1]
    tka, tn, tt = _tile(Ka, 1536), _tile(N, 1536), _rows(T, 512)
    nt = T // tt

    def body(a_ref, g_ref, o_ref):
        t = pl.program_id(2)
        p = _dot_tn(a_ref[...], g_ref[...])

        @pl.when(t == 0)
        def _():
            o_ref[...] = p

        @pl.when(t > 0)
        def _():
            o_ref[...] += p

    return pl.pallas_call(
        body, name=name, grid=(Ka // tka, N // tn, nt),
        in_specs=[pl.BlockSpec((tt, tka), lambda i, j, t: (t, i)),
                  pl.BlockSpec((tt, tn), lambda i, j, t: (t, j))],
        out_specs=pl.BlockSpec((tka, tn), lambda i, j, t: (i, j)),
        out_shape=jax.ShapeDtypeStruct((Ka, N), F32),
        compiler_params=_params("parallel", "parallel", "arbitrary"),
    )(a, g)


def rms_fwd(x, gains, name):
    T, D = x.shape
    tt = _rows(T, 512)
    ng = len(gains)

    def body(*refs):
        x_ref = refs[0]
        g_refs = refs[1:1 + ng]
        o_refs = refs[1 + ng:]
        xf = x_ref[...]
        y = xf * lax.rsqrt(jnp.mean(xf * xf, axis=-1, keepdims=True) + EPS)
        for g_ref, o_ref in zip(g_refs, o_refs):
            o_ref[...] = (y * g_ref[...]).astype(o_ref.dtype)

    row = pl.BlockSpec((tt, D), lambda i: (i, 0))
    gsp = pl.BlockSpec((1, D), lambda i: (0, 0))
    return pl.pallas_call(
        body, name=name, grid=(T // tt,),
        in_specs=[row] + [gsp] * ng, out_specs=[row] * ng,
        out_shape=[jax.ShapeDtypeStruct((T, D), MM_DTYPE)] * ng,
        compiler_params=_params("parallel"),
    )(x, *gains)


def rms_bwd(x, dres, branches, name):
    T, D = x.shape
    tt = _rows(T, 256)
    nb = len(branches)

    def body(*refs):
        x_ref, r_ref = refs[0], refs[1]
        dy_refs = refs[2:2 + nb]
        g_refs = refs[2 + nb:2 + 2 * nb]
        dx_ref = refs[2 + 2 * nb]
        dg_refs = refs[3 + 2 * nb:]
        i = pl.program_id(0)
        xf = x_ref[...]
        r = lax.rsqrt(jnp.mean(xf * xf, axis=-1, keepdims=True) + EPS)
        xh = xf * r
        dx = r_ref[...]
        for dy_ref, g_ref, dg_ref in zip(dy_refs, g_refs, dg_refs):
            dy = dy_ref[...].astype(F32)
            dyg = dy * g_ref[...]
            dx = dx + r * (dyg - xh * jnp.mean(dyg * xh, axis=-1, keepdims=True))
            part = jnp.sum(dy * xh, axis=0, keepdims=True)

            @pl.when(i == 0)
            def _():
                dg_ref[...] = part

            @pl.when(i > 0)
            def _():
                dg_ref[...] += part
        dx_ref[...] = dx

    row = pl.BlockSpec((tt, D), lambda i: (i, 0))
    gsp = pl.BlockSpec((1, D), lambda i: (0, 0))
    outs = pl.pallas_call(
        body, name=name, grid=(T // tt,),
        in_specs=[row, row] + [row] * nb + [gsp] * nb,
        out_specs=[row] + [gsp] * nb,
        out_shape=[jax.ShapeDtypeStruct((T, D), F32)] + [jax.ShapeDtypeStruct((1, D), F32)] * nb,
        compiler_params=_params("arbitrary"),
    )(x, dres, *[b[0] for b in branches], *[b[1] for b in branches])
    return outs[0], outs[1:]


def loss_head(x, target, gain):
    T, D = x.shape
    tt = _rows(T, 256)

    def body(x_ref, t_ref, g_ref, dx_ref, dg_ref, loss_ref):
        i = pl.program_id(0)
        xf = x_ref[...]
        g = g_ref[...]
        r = lax.rsqrt(jnp.mean(xf * xf, axis=-1, keepdims=True) + EPS)
        xh = xf * r
        e = xh * g - t_ref[...]
        lpart = 0.5 * jnp.sum(jnp.sum(e * e, axis=1, keepdims=True), axis=0, keepdims=True) / D
        dy = e / D
        dyg = dy * g
        dx_ref[...] = r * (dyg - xh * jnp.mean(dyg * xh, axis=-1, keepdims=True))
        gpart = jnp.sum(dy * xh, axis=0, keepdims=True)
        lrow = jnp.broadcast_to(lpart, (1, 128))

        @pl.when(i == 0)
        def _():
            dg_ref[...] = gpart
            loss_ref[...] = lrow

        @pl.when(i > 0)
        def _():
            dg_ref[...] += gpart
            loss_ref[...] += lrow

    row = pl.BlockSpec((tt, D), lambda i: (i, 0))
    gsp = pl.BlockSpec((1, D), lambda i: (0, 0))
    return pl.pallas_call(
        body, name="loss_head", grid=(T // tt,),
        in_specs=[row, row, gsp],
        out_specs=[row, gsp, pl.BlockSpec((1, 128), lambda i: (0, 0))],
        out_shape=[jax.ShapeDtypeStruct((T, D), F32), jax.ShapeDtypeStruct((1, D), F32),
                   jax.ShapeDtypeStruct((1, 128), F32)],
        compiler_params=_params("arbitrary"),
    )(x, target, gain)


def _shift_down(u, prev8, first, k):
    rolled = pltpu.roll(u, k, 0)
    rid = lax.broadcasted_iota(jnp.int32, u.shape, 0)
    halo = jnp.where(first, 0.0, prev8)
    out = rolled
    for j in range(k):
        out = jnp.where(rid == j, halo[8 - k + j:8 - k + j + 1, :], out)
    return out


def _conv3(u, prev8, first, w, b):
    return (_shift_down(u, prev8, first, 2) * w[0:1, :] + _shift_down(u, prev8, first, 1) * w[1:2, :]
            + u * w[2:3, :] + b)


def conv_act_fwd(u, w, b, name):
    T = u.shape[0]
    tt = _rows(T, 512)
    nj = D_FF // FF_TC

    def body(u_ref, p_ref, w_ref, b_ref, o_ref):
        first = pl.program_id(1) == 0
        c = _conv3(u_ref[...], p_ref[...], first, w_ref[...], b_ref[...])
        cg, cv = c[:, :FF_TC], c[:, FF_TC:]
        o_ref[...] = (cg * _sigmoid(cg) * cv).astype(o_ref.dtype)

    return pl.pallas_call(
        body, name=name, grid=(nj, T // tt),
        in_specs=[pl.BlockSpec((tt, 2 * FF_TC), lambda j, i: (i, j)),
                  pl.BlockSpec((8, 2 * FF_TC), lambda j, i: (jnp.maximum(i * (tt // 8) - 1, 0), j)),
                  pl.BlockSpec((3, 2 * FF_TC), lambda j, i: (0, j)),
                  pl.BlockSpec((1, 2 * FF_TC), lambda j, i: (0, j))],
        out_specs=pl.BlockSpec((tt, FF_TC), lambda j, i: (i, j)),
        out_shape=jax.ShapeDtypeStruct((T, D_FF), MM_DTYPE),
        compiler_params=_params("parallel", "parallel"),
    )(u, u, w, b)


def conv_act_bwd(u, da, w, b, name):
    T = u.shape[0]
    tt = _rows(T, 512)
    nt = T // tt
    nj = D_FF // FF_TC
    te = tt + 8

    def body(u_ref, p_ref, n_ref, da_ref, dan_ref, w_ref, b_ref, du_ref, dw_ref, db_ref):
        i = pl.program_id(1)
        first = i == 0
        last = i == nt - 1
        w = w_ref[...]
        ue = jnp.concatenate([u_ref[...], n_ref[...]], axis=0)
        dae = jnp.concatenate([da_ref[...], jnp.where(last, 0.0, dan_ref[...])], axis=0)
        um2 = _shift_down(ue, p_ref[...], first, 2)
        um1 = _shift_down(ue, p_ref[...], first, 1)
        c = um2 * w[0:1, :] + um1 * w[1:2, :] + ue * w[2:3, :] + b_ref[...]
        cg, cv = c[:, :FF_TC], c[:, FF_TC:]
        s = _sigmoid(cg)
        dcg = dae * cv * (s * (1.0 + cg * (1.0 - s)))
        dcv = dae * (cg * s)
        dc = jnp.concatenate([dcg, dcv], axis=1)
        du = (dc * w[2:3, :] + pltpu.roll(dc, te - 1, 0) * w[1:2, :] + pltpu.roll(dc, te - 2, 0) * w[0:1, :])
        du_ref[...] = du[:tt, :].astype(du_ref.dtype)
        dcm = dc[:tt, :]
        dwp = jnp.concatenate([jnp.sum(dcm * um2[:tt, :], axis=0, keepdims=True),
                               jnp.sum(dcm * um1[:tt, :], axis=0, keepdims=True),
                               jnp.sum(dcm * ue[:tt, :], axis=0, keepdims=True)], axis=0)
        dbp = jnp.sum(dcm, axis=0, keepdims=True)

        @pl.when(first)
        def _():
            dw_ref[...] = dwp
            db_ref[...] = dbp

        @pl.when(i > 0)
        def _():
            dw_ref[...] += dwp
            db_ref[...] += dbp

    nb8 = T // 8
    return pl.pallas_call(
        body, name=name, grid=(nj, nt),
        in_specs=[pl.BlockSpec((tt, 2 * FF_TC), lambda j, i: (i, j)),
                  pl.BlockSpec((8, 2 * FF_TC), lambda j, i: (jnp.maximum(i * (tt // 8) - 1, 0), j)),
                  pl.BlockSpec((8, 2 * FF_TC), lambda j, i: (jnp.minimum((i + 1) * (tt // 8), nb8 - 1), j)),
                  pl.BlockSpec((tt, FF_TC), lambda j, i: (i, j)),
                  pl.BlockSpec((8, FF_TC), lambda j, i: (jnp.minimum((i + 1) * (tt // 8), nb8 - 1), j)),
                  pl.BlockSpec((3, 2 * FF_TC), lambda j, i: (0, j)),
                  pl.BlockSpec((1, 2 * FF_TC), lambda j, i: (0, j))],
        out_specs=[pl.BlockSpec((tt, 2 * FF_TC), lambda j, i: (i, j)),
                   pl.BlockSpec((3, 2 * FF_TC), lambda j, i: (0, j)),
                   pl.BlockSpec((1, 2 * FF_TC), lambda j, i: (0, j))],
        out_shape=[jax.ShapeDtypeStruct((T, 2 * D_FF), MM_DTYPE),
                   jax.ShapeDtypeStruct((3, 2 * D_FF), F32),
                   jax.ShapeDtypeStruct((1, 2 * D_FF), F32)],
        compiler_params=_params("parallel", "arbitrary"),
    )(u, u, u, da, da, w, b)


def _interleave(a):
    lead = a.shape[:-1]
    nj = D_FF // FF_TC
    return jnp.swapaxes(a.reshape(*lead, 2, nj, FF_TC), -3, -2).reshape(*lead, 2 * D_FF)


def _deinterleave(a):
    lead = a.shape[:-1]
    nj = D_FF // FF_TC
    return jnp.swapaxes(a.reshape(*lead, nj, 2, FF_TC), -3, -2).reshape(*lead, 2 * D_FF)


A_GC = 2
A_TB = A_GC * A_CHUNK


def gate_prep(z, bias128):
    T = z.shape[0]
    tt = _rows(T, 512)

    def body(z_ref, b_ref, gc_ref, gr_ref):
        pre = z_ref[...] + b_ref[...]
        sc = SOFTCAP * jnp.tanh(pre / SOFTCAP)
        lf = jnp.minimum(sc, 0.0) - jnp.log(1.0 + jnp.exp(-jnp.abs(sc)))
        col = lax.broadcasted_iota(jnp.int32, pre.shape, 1)
        isf = jnp.logical_and(col >= A_HEADS, col < 2 * A_HEADS)
        r = lax.broadcasted_iota(jnp.int32, (tt, tt), 0)
        c = lax.broadcasted_iota(jnp.int32, (tt, tt), 1)
        tri = jnp.logical_and(jnp.right_shift(r, 6) == jnp.right_shift(c, 6), c <= r).astype(F32)
        bcum = jnp.dot(tri, jnp.where(isf, lf, 0.0), precision=HI, preferred_element_type=F32)
        g = jnp.where(col < A_HEADS, sc, jnp.where(isf, bcum, 0.0))
        gc_ref[...] = g
        for s in range(tt // 128):
            gr_ref[s] = g[s * 128:(s + 1) * 128, :].T[0:8, :]

    return pl.pallas_call(
        body, name="gate_prep", grid=(T // tt,),
        in_specs=[pl.BlockSpec((tt, 128), lambda i: (i, GATE_COL // 128)),
                  pl.BlockSpec((1, 128), lambda i: (0, 0))],
        out_specs=[pl.BlockSpec((tt, 128), lambda i: (i, 0)),
                   pl.BlockSpec((tt // 128, 8, 128), lambda i: (i, 0, 0))],
        out_shape=[jax.ShapeDtypeStruct((T, 128), F32), jax.ShapeDtypeStruct((T // 128, 8, 128), F32)],
        compiler_params=_params("parallel"),
    )(z, bias128)


def _chunk_fwd(qh, kh, vh, bc, br, lir, C, n, m, causal):
    A = _dot_nt(qh, kh)
    logD = jnp.where(causal, bc - br + lir, -jnp.inf)
    m_inter = bc + m
    m_t = jnp.maximum(m_inter, jnp.max(logD, axis=1, keepdims=True))
    E = jnp.exp(logD - m_t)
    Sm = A * E
    wi = jnp.exp(m_inter - m_t)
    num = _dot(Sm, vh) + wi * _dot(qh, C)
    qn = jnp.sum(qh.astype(F32) * n, axis=1, keepdims=True)
    den = jnp.sum(Sm, axis=1, keepdims=True) + wi * qn
    gs = jnp.maximum(jnp.abs(den), jnp.exp(-m_t))
    return E, Sm, wi, num, den, gs, m_t


def _state_weights(bc, lic, br, lir, m):
    bL = bc[A_CHUNK - 1:A_CHUNK, :]
    m_new = jnp.maximum(bL + m, jnp.max(bL - br + lir, axis=1, keepdims=True))
    wk = jnp.exp(bL - bc + lic - m_new)
    decay = jnp.exp(bL + m - m_new)
    return wk, decay, m_new


def _head_slices(h):
    return (slice(h * A_QK, (h + 1) * A_QK), slice(h * A_V, (h + 1) * A_V))


def mlstm_fwd(z, gcol, grow, hng):
    T = z.shape[0]
    NC = T // A_CHUNK
    scale = A_QK ** -0.5

    def body(q_ref, k_ref, v_ref, o_ref, gc_ref, gr_ref, hng_ref, hg_ref, Cs_ref, ns_ref, ms_ref,
             C_sc, n_sc, m_sc):
        @pl.when(pl.program_id(0) == 0)
        def _():
            C_sc[...] = jnp.zeros_like(C_sc)
            n_sc[...] = jnp.zeros_like(n_sc)
            m_sc[...] = jnp.zeros_like(m_sc)

        ri = lax.broadcasted_iota(jnp.int32, (A_CHUNK, A_CHUNK), 0)
        ci = lax.broadcasted_iota(jnp.int32, (A_CHUNK, A_CHUNK), 1)
        causal = ri >= ci
        gr = gr_ref[0]
        for c in range(A_GC):
            rows = slice(c * A_CHUNK, (c + 1) * A_CHUNK)
            gc = gc_ref[rows, :]
            grc = gr[:, c * A_CHUNK:(c + 1) * A_CHUNK]
            for h in range(A_HEADS):
                sk, sv = _head_slices(h)
                qh = (q_ref[rows, sk] * scale).astype(MM_DTYPE)
                kh = k_ref[rows, sk].astype(MM_DTYPE)
                vh = v_ref[rows, sv].astype(MM_DTYPE)
                lic, bc = gc[:, h:h + 1], gc[:, A_HEADS + h:A_HEADS + h + 1]
                lir, br = grc[h:h + 1, :], grc[A_HEADS + h:A_HEADS + h + 1, :]
                C, n, m = C_sc[h], n_sc[h], m_sc[h][:, 0:1]
                Cs_ref[c, h] = C
                ns_ref[c, h] = n
                ms_ref[c, h] = m_sc[h]
                _, _, _, num, _, gs, _ = _chunk_fwd(qh, kh, vh, bc, br, lir, C, n, m, causal)
                hh = num / gs
                hn = hh * lax.rsqrt(jnp.mean(hh * hh, axis=1, keepdims=True) + EPS) * hng_ref[:, sv]
                hg_ref[rows, sv] = (hn * _sigmoid(o_ref[rows, sv])).astype(hg_ref.dtype)
                wk, decay, m_new = _state_weights(bc, lic, br, lir, m)
                kw = kh.astype(F32) * wk
                C_sc[h] = decay * C + _dot_tn(kw, vh)
                n_sc[h] = decay * n + jnp.sum(kw, axis=0, keepdims=True)
                m_sc[h] = jnp.broadcast_to(m_new, (1, 128))

    tok = lambda w, cb: pl.BlockSpec((A_TB, w), lambda i: (i, cb))
    return pl.pallas_call(
        body, name="mlstm_fwd", grid=(NC // A_GC,),
        in_specs=[tok(512, 0), tok(512, 1), tok(1024, 1), tok(1024, 2),
                  pl.BlockSpec((A_TB, 128), lambda i: (i, 0)),
                  pl.BlockSpec((1, 8, 128), lambda i: (i, 0, 0)),
                  pl.BlockSpec((1, 1024), lambda i: (0, 0))],
        out_specs=[pl.BlockSpec((A_TB, 1024), lambda i: (i, 0)),
                   pl.BlockSpec((A_GC, A_HEADS, A_QK, A_V), lambda i: (i, 0, 0, 0)),
                   pl.BlockSpec((A_GC, A_HEADS, 1, 128), lambda i: (i, 0, 0, 0)),
                   pl.BlockSpec((A_GC, A_HEADS, 1, 128), lambda i: (i, 0, 0, 0))],
        out_shape=[jax.ShapeDtypeStruct((T, 1024), MM_DTYPE),
                   jax.ShapeDtypeStruct((NC, A_HEADS, A_QK, A_V), F32),
                   jax.ShapeDtypeStruct((NC, A_HEADS, 1, 128), F32),
                   jax.ShapeDtypeStruct((NC, A_HEADS, 1, 128), F32)],
        scratch_shapes=[pltpu.VMEM((A_HEADS, A_QK, A_V), F32), pltpu.VMEM((A_HEADS, 1, 128), F32),
                        pltpu.VMEM((A_HEADS, 1, 128), F32)],
        compiler_params=_params("arbitrary"),
    )(z, z, z, z, gcol, grow, hng)


def mlstm_bwd(z, gcol, grow, hng, bias128, Cs, ns, ms, dhg):
    T = z.shape[0]
    NC = T // A_CHUNK
    nsteps = NC // A_GC
    scale = A_QK ** -0.5

    def body(q_ref, k_ref, v_ref, o_ref, zg_ref, gc_ref, gr_ref, hng_ref, b_ref, Cs_ref, ns_ref, ms_ref,
             dhg_ref, dz_ref, dgn_ref, dbif_ref, dC_sc, dn_sc):
        @pl.when(pl.program_id(0) == 0)
        def _():
            dC_sc[...] = jnp.zeros_like(dC_sc)
            dn_sc[...] = jnp.zeros_like(dn_sc)
            dgn_ref[...] = jnp.zeros_like(dgn_ref)
            dbif_ref[...] = jnp.zeros_like(dbif_ref)

        ri = lax.broadcasted_iota(jnp.int32, (A_CHUNK, A_CHUNK), 0)
        ci = lax.broadcasted_iota(jnp.int32, (A_CHUNK, A_CHUNK), 1)
        causal = ri >= ci
        upper = (ci >= ri).astype(F32)
        rid = lax.broadcasted_iota(jnp.int32, (A_CHUNK, 1), 0)
        col = lax.broadcasted_iota(jnp.int32, (A_CHUNK, 128), 1)
        gr = gr_ref[0]
        for c in reversed(range(A_GC)):
            rows = slice(c * A_CHUNK, (c + 1) * A_CHUNK)
            gc = gc_ref[rows, :]
            grc = gr[:, c * A_CHUNK:(c + 1) * A_CHUNK]
            dG = jnp.zeros((A_CHUNK, 128), F32)
            for h in range(A_HEADS):
                sk, sv = _head_slices(h)
                qh = (q_ref[rows, sk] * scale).astype(MM_DTYPE)
                kh = k_ref[rows, sk].astype(MM_DTYPE)
                vh = v_ref[rows, sv].astype(MM_DTYPE)
                qf, kf = qh.astype(F32), kh.astype(F32)
                lic, bc = gc[:, h:h + 1], gc[:, A_HEADS + h:A_HEADS + h + 1]
                lir, br = grc[h:h + 1, :], grc[A_HEADS + h:A_HEADS + h + 1, :]
                C, n, m = Cs_ref[c, h], ns_ref[c, h], ms_ref[c, h][:, 0:1]
                dC, dn = dC_sc[h], dn_sc[h]
                E, Sm, wi, num, den, gs, m_t = _chunk_fwd(qh, kh, vh, bc, br, lir, C, n, m, causal)
                wk, decay, _ = _state_weights(bc, lic, br, lir, m)
                hh = num / gs
                r = lax.rsqrt(jnp.mean(hh * hh, axis=1, keepdims=True) + EPS)
                gn = hng_ref[:, sv]
                o = o_ref[rows, sv]
                s = _sigmoid(o)
                dhg_h = dhg_ref[rows, sv]
                dhn = dhg_h * s
                dz_ref[rows, 2048 + h * A_V:2048 + (h + 1) * A_V] = dhg_h * (hh * r * gn) * s * (1.0 - s)
                dgn_ref[:, sv] += jnp.sum(dhn * hh * r, axis=0, keepdims=True)
                dyg = dhn * gn
                dh = r * dyg - hh * (r * r * r) * jnp.mean(dyg * hh, axis=1, keepdims=True)
                dnum = dh / gs
                live = (jnp.abs(den) > jnp.exp(-m_t)).astype(F32)
                dden = -jnp.sum(dh * hh, axis=1, keepdims=True) / gs * jnp.sign(den) * live
                dSE = jnp.where(causal, _dot_nt(dnum, vh) + dden, 0.0) * E
                dq = _dot(dSE, kh) + wi * (_dot_nt(dnum, C) + dden * n)
                dk_inter = wk * (_dot_nt(vh, dC) + dn)
                dk = _dot_tn(dSE, qh) + dk_inter
                dv = _dot_tn(Sm, dnum) + wk * _dot(kh, dC)
                dz_ref[rows, sk] = dq * scale
                dz_ref[rows, 512 + h * A_QK:512 + (h + 1) * A_QK] = dk
                dz_ref[rows, 1024 + h * A_V:1024 + (h + 1) * A_V] = dv
                dli = jnp.sum(kf * dk, axis=1, keepdims=True)
                db = jnp.sum(qf * dq, axis=1, keepdims=True) - dli
                usum = jnp.sum(jnp.sum(kf * dk_inter, axis=1, keepdims=True), axis=0, keepdims=True)
                ddecay = (jnp.sum(jnp.sum(dC * C, axis=1, keepdims=True), axis=0, keepdims=True)
                          + jnp.sum(dn * n, axis=1, keepdims=True))
                db = db + jnp.where(rid == A_CHUNK - 1, usum + ddecay * decay, 0.0)
                dG = dG + jnp.where(col == h, dli, 0.0) + jnp.where(col == A_HEADS + h, db, 0.0)
                dC_sc[h] = decay * dC + _dot_tn(qf * wi, dnum)
                dn_sc[h] = decay * dn + jnp.sum(qf * (wi * dden), axis=0, keepdims=True)
            dlf = jnp.dot(upper, dG, precision=HI, preferred_element_type=F32)
            pre = zg_ref[rows, :] + b_ref[...]
            th = jnp.tanh(pre / SOFTCAP)
            dcap = 1.0 - th * th
            dpre = jnp.where(col < A_HEADS, dG * dcap,
                             jnp.where(col < 2 * A_HEADS, dlf * _sigmoid(-SOFTCAP * th) * dcap, 0.0))
            dz_ref[rows, GATE_COL:GATE_COL + 128] = dpre
            dbif_ref[...] += jnp.sum(dpre, axis=0, keepdims=True)

    rev = lambda i: nsteps - 1 - i
    tok = lambda w, cb: pl.BlockSpec((A_TB, w), lambda i: (rev(i), cb))
    st = lambda a, b: pl.BlockSpec((A_GC, A_HEADS, a, b), lambda i: (rev(i), 0, 0, 0))
    return pl.pallas_call(
        body, name="mlstm_bwd", grid=(nsteps,),
        in_specs=[tok(512, 0), tok(512, 1), tok(1024, 1), tok(1024, 2), tok(128, GATE_COL // 128),
                  pl.BlockSpec((A_TB, 128), lambda i: (rev(i), 0)),
                  pl.BlockSpec((1, 8, 128), lambda i: (rev(i), 0, 0)),
                  pl.BlockSpec((1, 1024), lambda i: (0, 0)),
                  pl.BlockSpec((1, 128), lambda i: (0, 0)),
                  st(A_QK, A_V), st(1, 128), st(1, 128),
                  pl.BlockSpec((A_TB, 1024), lambda i: (rev(i), 0))],
        out_specs=[pl.BlockSpec((A_TB, A_IN_PAD), lambda i: (rev(i), 0)),
                   pl.BlockSpec((1, 1024), lambda i: (0, 0)),
                   pl.BlockSpec((1, 128), lambda i: (0, 0))],
        out_shape=[jax.ShapeDtypeStruct((T, A_IN_PAD), F32), jax.ShapeDtypeStruct((1, 1024), F32),
                   jax.ShapeDtypeStruct((1, 128), F32)],
        scratch_shapes=[pltpu.VMEM((A_HEADS, A_QK, A_V), F32), pltpu.VMEM((A_HEADS, 1, 128), F32)],
        compiler_params=_params("arbitrary"),
    )(z, z, z, z, z, gcol, grow, hng, bias128, Cs, ns, ms, dhg)


def _t5_bucket(dist):
    max_exact = REL_BUCKETS // 2
    d = np.maximum(dist, 0)
    log_ratio = np.log(np.maximum(d, 1) / max_exact) / math.log(REL_MAX_DIST / max_exact)
    large = np.minimum(max_exact + (log_ratio * (REL_BUCKETS - max_exact)).astype(np.int64), REL_BUCKETS - 1)
    return np.where(d < max_exact, d, large).astype(np.int32)


def _group_bucket(g):
    delta = B_BLOCK + np.arange(B_BLOCK)[:, None] - np.arange(2 * B_BLOCK)[None, :]
    return _t5_bucket(delta * DILATIONS[g])


def _band_masks(n):
    ri = lax.broadcasted_iota(jnp.int32, (B_BLOCK, B_BLOCK), 0)
    ci = lax.broadcasted_iota(jnp.int32, (B_BLOCK, B_BLOCK), 1)
    return jnp.logical_and(ci >= ri, n > 0), ci <= ri


def _scores(qh, kp, kc, bias_h, mp, mc):
    sp = _dot_nt(qh, kp) * (B_DH ** -0.5) + bias_h[:, :B_BLOCK]
    sc = _dot_nt(qh, kc) * (B_DH ** -0.5) + bias_h[:, B_BLOCK:]
    return jnp.where(mp, sp, -jnp.inf), jnp.where(mc, sc, -jnp.inf)


def _attn_specs(g, dil):
    qs = pl.BlockSpec((B_BLOCK, 1024), lambda r, n: (n, r * 3 + g))
    kc = pl.BlockSpec((B_BLOCK, 1024), lambda r, n: (n, r * 6 + g))
    kp = pl.BlockSpec((B_BLOCK, 1024), lambda r, n: (jnp.maximum(n - 1, 0), r * 6 + g))
    vc = pl.BlockSpec((B_BLOCK, 1024), lambda r, n: (n, r * 6 + 3 + g))
    vp = pl.BlockSpec((B_BLOCK, 1024), lambda r, n: (jnp.maximum(n - 1, 0), r * 6 + 3 + g))
    bias = pl.BlockSpec((B_HEADS, B_BLOCK, 2 * B_BLOCK), lambda r, n: (0, 0, 0))
    wide = pl.BlockSpec((B_BLOCK, 1024), lambda r, n: (n, r))
    narrow = pl.BlockSpec((B_BLOCK, 128), lambda r, n: (n, r))
    return qs, kp, kc, vp, vc, bias, wide, narrow


def attn_fwd(qd, kv, bias, g):
    T = qd.shape[0]
    dil = DILATIONS[g]
    Tv = T // dil
    nb = Tv // B_BLOCK
    qs, kp, kc, vp, vc, bsp, wide, narrow = _attn_specs(g, dil)

    def body(q_ref, kp_ref, kc_ref, vp_ref, vc_ref, b_ref, o_ref, lse_ref):
        mp, mc = _band_masks(pl.program_id(1))
        lse_ref[...] = jnp.zeros_like(lse_ref)
        for h in range(B_HEADS):
            sl = slice(h * B_DH, (h + 1) * B_DH)
            sp, sc = _scores(q_ref[:, sl], kp_ref[:, sl], kc_ref[:, sl], b_ref[h], mp, mc)
            m = jnp.maximum(jnp.max(sp, axis=1, keepdims=True), jnp.max(sc, axis=1, keepdims=True))
            pp, pc = jnp.exp(sp - m), jnp.exp(sc - m)
            l = jnp.sum(pp, axis=1, keepdims=True) + jnp.sum(pc, axis=1, keepdims=True)
            o_ref[:, sl] = (_dot(pp, vp_ref[:, sl]) + _dot(pc, vc_ref[:, sl])) / l
            lse_ref[:, h:h + 1] = m + jnp.log(l)

    o, lse = pl.pallas_call(
        body, name=f"attn_fwd_g{g}", grid=(dil, nb),
        in_specs=[qs, kp, kc, vp, vc, bsp], out_specs=[wide, narrow],
        out_shape=[jax.ShapeDtypeStruct((Tv, dil * 1024), F32), jax.ShapeDtypeStruct((Tv, dil * 128), F32)],
        compiler_params=_params("parallel", "parallel"),
    )(qd.reshape(Tv, dil * 3072), *([kv.reshape(Tv, dil * 6144)] * 4), bias)
    return o.reshape(T, 1024), lse.reshape(T, 128)


def attn_bwd(qd, kv, bias, datt, lse, delta, g):
    T = qd.shape[0]
    dil = DILATIONS[g]
    Tv = T // dil
    nb = Tv // B_BLOCK
    qs, kp, kc, vp, vc, bsp, wide, narrow = _attn_specs(g, dil)

    def body(q_ref, kp_ref, kc_ref, vp_ref, vc_ref, b_ref, do_ref, lse_ref, dl_ref,
             dq_ref, dkc_ref, dkp_ref, dvc_ref, dvp_ref, db_ref):
        @pl.when(jnp.logical_and(pl.program_id(0) == 0, pl.program_id(1) == 0))
        def _():
            db_ref[...] = jnp.zeros_like(db_ref)

        mp, mc = _band_masks(pl.program_id(1))
        for h in range(B_HEADS):
            sl = slice(h * B_DH, (h + 1) * B_DH)
            qh, doh = q_ref[:, sl], do_ref[:, sl].astype(MM_DTYPE)
            sp, sc = _scores(qh, kp_ref[:, sl], kc_ref[:, sl], b_ref[h], mp, mc)
            L, D = lse_ref[:, h:h + 1], dl_ref[:, h:h + 1]
            pp, pc = jnp.exp(sp - L), jnp.exp(sc - L)
            dsp = pp * (_dot_nt(doh, vp_ref[:, sl]) - D)
            dsc = pc * (_dot_nt(doh, vc_ref[:, sl]) - D)
            db_ref[h, :, :B_BLOCK] += dsp
            db_ref[h, :, B_BLOCK:] += dsc
            dsp, dsc = dsp * (B_DH ** -0.5), dsc * (B_DH ** -0.5)
            dq_ref[:, sl] = _dot(dsp, kp_ref[:, sl]) + _dot(dsc, kc_ref[:, sl])
            dkp_ref[:, sl] = _dot_tn(dsp, qh)
            dkc_ref[:, sl] = _dot_tn(dsc, qh)
            dvp_ref[:, sl] = _dot_tn(pp, doh)
            dvc_ref[:, sl] = _dot_tn(pc, doh)

    big = jax.ShapeDtypeStruct((Tv, dil * 1024), F32)
    outs = pl.pallas_call(
        body, name=f"attn_bwd_g{g}", grid=(dil, nb),
        in_specs=[qs, kp, kc, vp, vc, bsp, wide, narrow, narrow],
        out_specs=[wide] * 5 + [bsp],
        out_shape=[big] * 5 + [jax.ShapeDtypeStruct((B_HEADS, B_BLOCK, 2 * B_BLOCK), F32)],
        compiler_params=_params("arbitrary", "arbitrary"),
    )(qd.reshape(Tv, dil * 3072), *([kv.reshape(Tv, dil * 6144)] * 4), bias,
      datt.reshape(Tv, dil * 1024), lse.reshape(Tv, dil * 128), delta.reshape(Tv, dil * 128))
    return [o.reshape(T, 1024) for o in outs[:5]] + [outs[5]]


def _head_expand():
    e = np.zeros((128, 1024), np.float32)
    for h in range(B_HEADS):
        e[h, h * B_DH:(h + 1) * B_DH] = 1.0
    return e


def attn_merge(os_, lses):
    T = os_[0].shape[0]
    tt = _rows(T, 256)
    expand = jnp.asarray(_head_expand())

    def body(o0, o1, o2, l0, l1, l2, e_ref, ob_ref, of_ref, lse_ref):
        ls = [l0[...], l1[...], l2[...]]
        m = jnp.maximum(jnp.maximum(ls[0], ls[1]), ls[2])
        ex = [jnp.exp(l - m) for l in ls]
        tot = ex[0] + ex[1] + ex[2]
        lse_ref[...] = m + jnp.log(tot)
        out = jnp.zeros((tt, 1024), F32)
        for e, o in zip(ex, (o0, o1, o2)):
            w = jnp.dot(e / tot, e_ref[...], precision=HI, preferred_element_type=F32)
            out = out + w * o[...]
        of_ref[...] = out
        ob_ref[...] = out.astype(ob_ref.dtype)

    wide = pl.BlockSpec((tt, 1024), lambda i: (i, 0))
    narrow = pl.BlockSpec((tt, 128), lambda i: (i, 0))
    return pl.pallas_call(
        body, name="attn_merge", grid=(T // tt,),
        in_specs=[wide] * 3 + [narrow] * 3 + [pl.BlockSpec((128, 1024), lambda i: (0, 0))],
        out_specs=[wide, wide, narrow],
        out_shape=[jax.ShapeDtypeStruct((T, 1024), MM_DTYPE), jax.ShapeDtypeStruct((T, 1024), F32),
                   jax.ShapeDtypeStruct((T, 128), F32)],
        compiler_params=_params("parallel"),
    )(*os_, *lses, expand)


def attn_delta(datt, out):
    T = datt.shape[0]
    tt = _rows(T, 512)
    expand_t = jnp.asarray(_head_expand().T.copy())

    def body(d_ref, o_ref, e_ref, dl_ref):
        dl_ref[...] = jnp.dot(d_ref[...] * o_ref[...], e_ref[...], precision=HI, preferred_element_type=F32)

    wide = pl.BlockSpec((tt, 1024), lambda i: (i, 0))
    return pl.pallas_call(
        body, name="attn_delta", grid=(T // tt,),
        in_specs=[wide, wide, pl.BlockSpec((1024, 128), lambda i: (0, 0))],
        out_specs=pl.BlockSpec((tt, 128), lambda i: (i, 0)),
        out_shape=jax.ShapeDtypeStruct((T, 128), F32),
        compiler_params=_params("parallel"),
    )(datt, out, expand_t)


def attn_combine(dqs, dkc, dkp, dvc, dvp):
    T = dqs[0].shape[0]
    tt = B_BLOCK
    nt = T // tt

    def body(*refs):
        i = pl.program_id(0)
        dq_refs, rest = refs[:3], refs[3:]
        c_refs, p_refs = rest[:6], rest[6:12]
        dq_ref, dkv_ref = rest[12], rest[13]
        for g in range(N_GROUPS):
            dq_ref[:, g * 1024:(g + 1) * 1024] = dq_refs[g][...].astype(dq_ref.dtype)
        for j in range(6):
            live = i + DILATIONS[j % 3] < nt
            tot = c_refs[j][...] + jnp.where(live, p_refs[j][...], 0.0)
            dkv_ref[:, j * 1024:(j + 1) * 1024] = tot.astype(dkv_ref.dtype)

    cur = pl.BlockSpec((tt, 1024), lambda i: (i, 0))

    def nxt(d):
        return pl.BlockSpec((tt, 1024), lambda i: (jnp.minimum(i + d, nt - 1), 0))

    return pl.pallas_call(
        body, name="attn_combine", grid=(nt,),
        in_specs=[cur] * 3 + [cur] * 6 + [nxt(DILATIONS[j % 3]) for j in range(6)],
        out_specs=[pl.BlockSpec((tt, 3072), lambda i: (i, 0)), pl.BlockSpec((tt, 6144), lambda i: (i, 0))],
        out_shape=[jax.ShapeDtypeStruct((T, 3072), MM_DTYPE), jax.ShapeDtypeStruct((T, 6144), MM_DTYPE)],
        compiler_params=_params("parallel"),
    )(*dqs, *dkc, *dvc, *dkp, *dvp)


def adamw(w, g, m, v, name):
    R, C = w.shape
    tr = R if R * C * 4 <= (1 << 20) else _rows(R, max(8, ((1 << 20) // (C * 4)) // 8 * 8))

    def body(w_ref, g_ref, m_ref, v_ref, d_ref, nm_ref, nv_ref):
        gg = g_ref[...]
        nm = ADAM_B1 * m_ref[...] + (1.0 - ADAM_B1) * gg
        nv = ADAM_B2 * v_ref[...] + (1.0 - ADAM_B2) * (gg * gg)
        m_hat = nm / (1.0 - ADAM_B1 ** ADAM_STEP)
        v_hat = nv / (1.0 - ADAM_B2 ** ADAM_STEP)
        d_ref[...] = -ADAM_LR * (m_hat / (jnp.sqrt(v_hat) + ADAM_EPS) + ADAM_WD * w_ref[...])
        nm_ref[...] = nm
        nv_ref[...] = nv

    blk = pl.BlockSpec((tr, C), lambda i: (i, 0))
    sds = jax.ShapeDtypeStruct((R, C), F32)
    return pl.pallas_call(
        body, name=name, grid=(R // tr,), in_specs=[blk] * 4, out_specs=[blk] * 3, out_shape=[sds] * 3,
        compiler_params=_params("parallel"),
    )(w, g, m, v)


def sum8(x, name):
    _, R, C = x.shape
    tr = _rows(R, 256)

    def body(x_ref, o_ref):
        acc = x_ref[0]
        for s in range(1, 8):
            acc = acc + x_ref[s]
        o_ref[...] = acc

    return pl.pallas_call(
        body, name=name, grid=(R // tr,),
        in_specs=[pl.BlockSpec((8, tr, C), lambda i: (0, i, 0))],
        out_specs=pl.BlockSpec((tr, C), lambda i: (i, 0)),
        out_shape=jax.ShapeDtypeStruct((R, C), x.dtype),
        compiler_params=_params("parallel"),
    )(x)


_FLIPS = [(0, 0, 1), (0, 1, 0), (0, 1, 1), (1, 0, 0), (1, 0, 1), (1, 1, 0), (1, 1, 1)]
_ANY = pl.BlockSpec(memory_space=pl.ANY)


def _me():
    return lax.axis_index("x"), lax.axis_index("y"), lax.axis_index("c")


def _peer(me, flip):
    return tuple(1 - a if f else a for a, f in zip(me, flip))


def _lid(dev):
    return 4 * dev[0] + 2 * dev[1] + dev[2]


def all_gather8(x, name):
    R, C = x.shape

    def body(x_ref, o_ref, send_sems, recv_sems, local_sem):
        me = _me()
        mine = pltpu.make_async_copy(x_ref, o_ref.at[_lid(me)], local_sem)
        mine.start()
        copies = []
        for k, flip in enumerate(_FLIPS):
            cp = pltpu.make_async_remote_copy(
                src_ref=x_ref, dst_ref=o_ref.at[_lid(me)], send_sem=send_sems.at[k], recv_sem=recv_sems.at[k],
                device_id=_peer(me, flip), device_id_type=MESH_ID)
            cp.start()
            copies.append(cp)
        for k, flip in enumerate(_FLIPS):
            pltpu.make_async_remote_copy(
                src_ref=x_ref, dst_ref=o_ref.at[_lid(_peer(me, flip))], send_sem=send_sems.at[k],
                recv_sem=recv_sems.at[k], device_id=_peer(me, flip), device_id_type=MESH_ID).wait_recv()
        for cp in copies:
            cp.wait_send()
        mine.wait()

    return pl.pallas_call(
        body, name=name, in_specs=[_ANY], out_specs=_ANY,
        out_shape=jax.ShapeDtypeStruct((8, R, C), x.dtype),
        scratch_shapes=[pltpu.SemaphoreType.DMA((7,)), pltpu.SemaphoreType.DMA((7,)), pltpu.SemaphoreType.DMA],
    )(x)


def all_to_all8(x, name):
    _, R, C = x.shape

    def body(x_ref, o_ref, send_sems, recv_sems, local_sem):
        me = _me()
        mine = pltpu.make_async_copy(x_ref.at[_lid(me)], o_ref.at[_lid(me)], local_sem)
        mine.start()
        copies = []
        for k, flip in enumerate(_FLIPS):
            peer = _peer(me, flip)
            cp = pltpu.make_async_remote_copy(
                src_ref=x_ref.at[_lid(peer)], dst_ref=o_ref.at[_lid(me)], send_sem=send_sems.at[k],
                recv_sem=recv_sems.at[k], device_id=peer, device_id_type=MESH_ID)
            cp.start()
            copies.append(cp)
        for k, flip in enumerate(_FLIPS):
            peer = _peer(me, flip)
            pltpu.make_async_remote_copy(
                src_ref=x_ref.at[_lid(me)], dst_ref=o_ref.at[_lid(peer)], send_sem=send_sems.at[k],
                recv_sem=recv_sems.at[k], device_id=peer, device_id_type=MESH_ID).wait_recv()
        for cp in copies:
            cp.wait_send()
        mine.wait()

    return pl.pallas_call(
        body, name=name, in_specs=[_ANY], out_specs=_ANY,
        out_shape=jax.ShapeDtypeStruct((8, R, C), x.dtype),
        scratch_shapes=[pltpu.SemaphoreType.DMA((7,)), pltpu.SemaphoreType.DMA((7,)), pltpu.SemaphoreType.DMA],
    )(x)


def sibling_gather(x, name):
    R, C = x.shape

    def body(x_ref, o_ref, send_sem, recv_sem, local_sem):
        me = _me()
        sib = (me[0], me[1], 1 - me[2])
        mine = pltpu.make_async_copy(x_ref, o_ref.at[me[2]], local_sem)
        mine.start()
        cp = pltpu.make_async_remote_copy(
            src_ref=x_ref, dst_ref=o_ref.at[me[2]], send_sem=send_sem, recv_sem=recv_sem,
            device_id=sib, device_id_type=MESH_ID)
        cp.start()
        pltpu.make_async_remote_copy(
            src_ref=x_ref, dst_ref=o_ref.at[1 - me[2]], send_sem=send_sem, recv_sem=recv_sem,
            device_id=sib, device_id_type=MESH_ID).wait_recv()
        cp.wait_send()
        mine.wait()

    return pl.pallas_call(
        body, name=name, in_specs=[_ANY], out_specs=_ANY,
        out_shape=jax.ShapeDtypeStruct((2, R, C), x.dtype),
        scratch_shapes=[pltpu.SemaphoreType.DMA, pltpu.SemaphoreType.DMA, pltpu.SemaphoreType.DMA],
    )(x)


WEIGHTS = ['a_norm_g', 'a_w_in', 'a_b_if', 'a_hnorm_g', 'a_w_out', 'kv_norm_g', 'w_kv', 'b_norm_g', 'b_w_q',
           'b_w_out', 'rel_bias', 'f_norm_g', 'f_w_up', 'f_conv_w', 'f_conv_b', 'f_w_down', 'final_norm_g']
SHARD_AXIS = {'a_norm_g': 1, 'a_w_in': 2, 'a_b_if': None, 'a_hnorm_g': 2, 'a_w_out': 1, 'kv_norm_g': None,
              'w_kv': 1, 'b_norm_g': None, 'b_w_q': 2, 'b_w_out': 1, 'rel_bias': None, 'f_norm_g': None,
              'f_w_up': 2, 'f_conv_w': 2, 'f_conv_b': None, 'f_w_down': 1, 'final_norm_g': None}
BIG = ['a_w_in', 'a_w_out', 'w_kv', 'b_w_q', 'b_w_out', 'f_w_up', 'f_w_down']
SMALL = [n for n in WEIGHTS if n not in BIG]
LANES = 1024


def _pad_rows(flat, mult):
    n = flat.shape[0]
    per = LANES * mult
    tot = -(-n // per) * per
    return jnp.pad(flat, (0, tot - n)).reshape(tot // LANES, LANES)


def _full_from_shards(sh, axis):
    return jnp.concatenate([sh[j] for j in range(4)], axis=axis)


def _shards_from_full(full, axis):
    return jnp.stack(jnp.split(full, 4, axis=axis))


def _local_step(x, target, W):
    T = x.shape[0]
    row = lambda a: a.reshape(1, -1).astype(F32)
    w_in = jnp.pad(W['a_w_in'][0], ((0, 0), (0, A_IN_PAD - A_IN)))
    bias128 = jnp.pad(row(W['a_b_if'][0]), ((0, 0), (0, 120)))
    hng = row(W['a_hnorm_g'][0])
    w_up = [_interleave(W['f_w_up'][l]) for l in range(2)]
    cw = [_interleave(W['f_conv_w'][l].astype(F32)) for l in range(2)]
    cb = [_interleave(row(W['f_conv_b'][l])) for l in range(2)]
    onehots = [(jnp.asarray(_group_bucket(g).reshape(-1, 1)) == jnp.arange(128)[None, :]).astype(F32)
               for g in range(N_GROUPS)]
    rb_t = jnp.pad(W['rel_bias'].astype(F32).T, ((0, 0), (0, 128 - REL_BUCKETS)))
    biases = [mm_nn(rb_t[g * B_HEADS:(g + 1) * B_HEADS], onehots[g].T, f"rel_bias_table_g{g}", exact=True)
              .reshape(B_HEADS, B_BLOCK, 2 * B_BLOCK) for g in range(N_GROUPS)]
    G = {}

    def ffn_fwd(xin, l):
        xn, = rms_fwd(xin, [row(W['f_norm_g'][l])], f"ffn{l}_norm")
        u = mm_nn(xn, w_up[l], f"ffn{l}_up")
        act = conv_act_fwd(u, cw[l], cb[l], f"ffn{l}_act")
        return mm_nn(act, W['f_w_down'][l], f"ffn{l}_down", res=xin), (xn, u, act)

    def ffn_bwd(xin, saved, dout, l):
        xn, u, act = saved
        dact = mm_nn(dout, W['f_w_down'][l].T, f"ffn{l}_ddown")
        gd = mm_tn(act, dout, f"ffn{l}_gdown")
        du, gcw, gcb = conv_act_bwd(u, dact, cw[l], cb[l], f"ffn{l}_dact")
        dxn = mm_nn(du, w_up[l].T, f"ffn{l}_dup")
        gu = _deinterleave(mm_tn(xn, du, f"ffn{l}_gup"))
        dxin, (gn,) = rms_bwd(xin, dout, [(dxn, row(W['f_norm_g'][l]))], f"ffn{l}_dnorm")
        return dxin, gd, gu, _deinterleave(gcw), _deinterleave(gcb), gn

    xn_a, = rms_fwd(x, [row(W['a_norm_g'][0])], "a_norm")
    z = mm_nn(xn_a, w_in, "a_in")
    gcol, grow = gate_prep(z, bias128)
    hg, Cs, ns, ms = mlstm_fwd(z, gcol, grow, hng)
    x1 = mm_nn(hg, W['a_w_out'][0], "a_out", res=x)
    x2, ffn0 = ffn_fwd(x1, 0)
    xn_kv, xn_b = rms_fwd(x2, [row(W['kv_norm_g']), row(W['b_norm_g'][0])], "b_norms")
    kv = mm_nn(xn_kv, W['w_kv'], "kv_proj", out_dtype=MM_DTYPE)
    qd = mm_nn(xn_b, W['b_w_q'][0], "q_proj", out_dtype=MM_DTYPE)
    os_, lses = zip(*[attn_fwd(qd, kv, biases[g], g) for g in range(N_GROUPS)])
    att, att_f, lse = attn_merge(os_, lses)
    x3 = mm_nn(att, W['b_w_out'][0], "b_out", res=x2)
    x4, ffn1 = ffn_fwd(x3, 1)
    dx4, g_final, loss = loss_head(x4, target, row(W['final_norm_g']))
    G['final_norm_g'] = g_final.reshape(-1)

    dx3, gd1, gu1, gcw1, gcb1, gn1 = ffn_bwd(x3, ffn1, dx4, 1)
    datt = mm_nn(dx3, W['b_w_out'][0].T, "b_dout")
    G['b_w_out'] = mm_tn(att, dx3, "b_gout")[None]
    delta = attn_delta(datt, att_f)
    parts = [attn_bwd(qd, kv, biases[g], datt, lse, delta, g) for g in range(N_GROUPS)]
    dq_all, dkv = attn_combine([p[0] for p in parts], [p[1] for p in parts], [p[2] for p in parts],
                               [p[3] for p in parts], [p[4] for p in parts])
    grb = []
    for g in range(N_GROUPS):
        gb = mm_nn(parts[g][5].reshape(B_HEADS, -1), onehots[g], f"rel_bias_g{g}", exact=True)
        grb.append(gb[:, :REL_BUCKETS].T)
    G['rel_bias'] = jnp.concatenate(grb, axis=1)
    dxn_b = mm_nn(dq_all, W['b_w_q'][0].T, "q_dproj")
    G['b_w_q'] = mm_tn(xn_b, dq_all, "q_gproj")[None]
    dxn_kv = mm_nn(dkv, W['w_kv'].T, "kv_dproj")
    G['w_kv'] = mm_tn(xn_kv, dkv, "kv_gproj")
    dx2, (g_kvn, g_bn) = rms_bwd(x2, dx3, [(dxn_kv, row(W['kv_norm_g'])), (dxn_b, row(W['b_norm_g'][0]))],
                                 "b_dnorms")
    G['kv_norm_g'] = g_kvn.reshape(-1)
    G['b_norm_g'] = g_bn
    dx1, gd0, gu0, gcw0, gcb0, gn0 = ffn_bwd(x1, ffn0, dx2, 0)
    G['f_w_down'] = jnp.stack([gd0, gd1])
    G['f_w_up'] = jnp.stack([gu0, gu1])
    G['f_conv_w'] = jnp.stack([gcw0, gcw1])
    G['f_conv_b'] = jnp.concatenate([gcb0, gcb1], axis=0)
    G['f_norm_g'] = jnp.concatenate([gn0, gn1], axis=0)
    dhg = mm_nn(dx1, W['a_w_out'][0].T, "a_dout")
    G['a_w_out'] = mm_tn(hg, dx1, "a_gout")[None]
    dz, g_hn, g_bif = mlstm_bwd(z, gcol, grow, hng, bias128, Cs, ns, ms, dhg)
    G['a_hnorm_g'] = g_hn.reshape(1, A_HEADS, A_V)
    G['a_b_if'] = g_bif[:, :2 * A_HEADS]
    dxn_a = mm_nn(dz, w_in.T, "a_din")
    G['a_w_in'] = mm_tn(xn_a, dz, "a_gin")[:, :A_IN][None]
    grad_x, (g_an,) = rms_bwd(x, dx1, [(dxn_a, row(W['a_norm_g'][0]))], "a_dnorm")
    G['a_norm_g'] = g_an
    return loss, grad_x, G


def kernel(x, a_norm_g, a_w_in, a_b_if, a_hnorm_g, a_w_out, kv_norm_g, w_kv, b_norm_g, b_w_q, b_w_out, rel_bias, f_norm_g, f_w_up, f_conv_w, f_conv_b, f_w_down, final_norm_g, loss_target, m_a_norm_g, m_a_w_in, m_a_b_if, m_a_hnorm_g, m_a_w_out, m_kv_norm_g, m_w_kv, m_b_norm_g, m_b_w_q, m_b_w_out, m_rel_bias, m_f_norm_g, m_f_w_up, m_f_conv_w, m_f_conv_b, m_f_w_down, m_final_norm_g, v_a_norm_g, v_a_w_in, v_a_b_if, v_a_hnorm_g, v_a_w_out, v_kv_norm_g, v_w_kv, v_b_norm_g, v_b_w_q, v_b_w_out, v_rel_bias, v_f_norm_g, v_f_w_up, v_f_conv_w, v_f_conv_b, v_f_w_down, v_final_norm_g):
    given = dict(locals())
    shard = {n: given[n] for n in WEIGHTS}
    mom = {n: given["m_" + n] for n in WEIGHTS}
    var = {n: given["v_" + n] for n in WEIGHTS}
    cx, cy, cc = _me()
    chip = 2 * cx + cy

    halves = [shard[n].astype(MM_DTYPE).reshape(2, -1) for n in BIG]
    sizes = [h.shape[1] for h in halves]
    mine = lax.dynamic_index_in_dim(jnp.concatenate(halves, axis=1), cc, axis=0, keepdims=False)
    gathered = all_gather8(_pad_rows(mine, 16), "gather_weights").reshape(8, -1)
    W = {}
    off = 0
    for n, sz in zip(BIG, sizes):
        sh = gathered[:, off:off + sz].reshape((4,) + shard[n].shape)
        W[n] = _full_from_shards(sh, SHARD_AXIS[n])
        off += sz
    sharded_small = [n for n in SMALL if SHARD_AXIS[n] is not None]
    ssz = [shard[n].size for n in sharded_small]
    sflat = jnp.concatenate([shard[n].reshape(-1) for n in sharded_small])
    sg = all_gather8(_pad_rows(sflat, 8), "gather_small").reshape(8, -1)[0::2]
    off = 0
    for n, sz in zip(sharded_small, ssz):
        W[n] = _full_from_shards(sg[:, off:off + sz].reshape((4,) + shard[n].shape), SHARD_AXIS[n])
        off += sz
    for n in SMALL:
        if SHARD_AXIS[n] is None:
            W[n] = shard[n]

    loss_row, grad_x, G = _local_step(x[0], loss_target[0], W)

    slots = jnp.concatenate([_shards_from_full(G[n], SHARD_AXIS[n]).reshape(8, -1) for n in BIG], axis=1)
    per = slots.shape[1]
    slots = jax.vmap(lambda f: _pad_rows(f, 16))(slots)
    reduced = sum8(all_to_all8(slots, "scatter_grads"), "sum_grads")
    both = sibling_gather(reduced, "join_halves").reshape(2, -1)[:, :per]
    gsh = {}
    off = 0
    for n, sz in zip(BIG, sizes):
        gsh[n] = both[:, off:off + sz].reshape(shard[n].shape)
        off += sz
    small_parts = [loss_row[0, 0:1]] + [G[n].reshape(-1) for n in SMALL]
    small_sz = [p.shape[0] for p in small_parts]
    small = sum8(all_gather8(_pad_rows(jnp.concatenate(small_parts), 8), "gather_small_grads"),
                 "sum_small_grads").reshape(-1)
    loss = small[0]
    off = 1
    for n, sz in zip(SMALL, small_sz[1:]):
        full = small[off:off + sz].reshape(W[n].shape)
        off += sz
        if SHARD_AXIS[n] is None:
            gsh[n] = full
        else:
            gsh[n] = lax.dynamic_index_in_dim(_shards_from_full(full, SHARD_AXIS[n]), chip, 0, keepdims=False)

    delta, new_m, new_v = {}, {}, {}
    for n in WEIGHTS:
        shp = shard[n].shape
        two = lambda a: a.reshape(-1, shp[-1])
        d, nm, nv = adamw(two(shard[n]), two(gsh[n]), two(mom[n]), two(var[n]), f"adamw_{n}")
        delta[n], new_m[n], new_v[n] = d.reshape(shp), nm.reshape(shp), nv.reshape(shp)
    return (loss, grad_x[None], *[gsh[n] for n in WEIGHTS], *[delta[n] for n in WEIGHTS],
            *[new_m[n] for n in WEIGHTS], *[new_v[n] for n in WEIGHTS])
```

```python
import functools
import math

import numpy as np
import jax
import jax.numpy as jnp
from jax import lax
from jax.experimental import pallas as pl
from jax.experimental.pallas import tpu as pltpu

F32 = jnp.float32
BF16 = jnp.bfloat16
MM_DTYPE = jnp.bfloat16
GRAD_WIRE_DTYPE = jnp.bfloat16
HI = lax.Precision.HIGHEST

D_MODEL = 1024
A_HEADS = 4
A_QK = 128
A_V = 256
A_CHUNK = 64
A_IN = 3080
A_IN_PAD = 3200
GATE_COL = 3072
SOFTCAP = 15.0
N_GROUPS = 3
B_HEADS = 16
B_DH = 64
B_BLOCK = 128
DILATIONS = (1, 4, 16)
WINDOWS = (128, 512, 2048)
REL_BUCKETS = 32
REL_MAX_DIST = 2048
D_FF = 2816
FF_TC = 256
EPS = 1e-6
ADAM_LR, ADAM_B1, ADAM_B2, ADAM_EPS, ADAM_WD, ADAM_STEP = 0.001, 0.9, 0.999, 1e-08, 0.01, 10

VMEM_LIMIT = 56 * 1024 * 1024
NT_DIMS = (((1,), (1,)), ((), ()))
TN_DIMS = (((0,), (0,)), ((), ()))
MESH_ID = pl.DeviceIdType.MESH


def _params(*sem):
    return pltpu.CompilerParams(dimension_semantics=sem, vmem_limit_bytes=VMEM_LIMIT)


def _tile(n, cap):
    if n <= cap:
        return n
    best = None
    for t in range(128, cap + 1, 128):
        if n % t == 0:
            best = t
    assert best is not None, (n, cap)
    return best


def _rows(n, cap):
    if n <= cap:
        return n
    for t in range(cap // 8 * 8, 7, -8):
        if n % t == 0:
            return t
    raise ValueError((n, cap))


def _dot(a, b):
    return jnp.dot(a.astype(MM_DTYPE), b.astype(MM_DTYPE), preferred_element_type=F32)


def _dot_nt(a, b):
    return lax.dot_general(a.astype(MM_DTYPE), b.astype(MM_DTYPE), NT_DIMS, preferred_element_type=F32)


def _dot_tn(a, b):
    return lax.dot_general(a.astype(MM_DTYPE), b.astype(MM_DTYPE), TN_DIMS, preferred_element_type=F32)


def _sigmoid(x):
    return 1.0 / (1.0 + jnp.exp(-x))


def _sigmoid_tanh(x):
    return 0.5 * jnp.tanh(0.5 * x) + 0.5


def mm_nn(a, b, name, res=None, out_dtype=F32, exact=False):
    M, K = a.shape
    N = b.shape[1]
    tm, tn, tk = _rows(M, 512), _tile(N, 1536), _tile(K, 1536)
    nk = K // tk

    def body(*refs):
        if res is None:
            a_ref, b_ref, o_ref, acc = refs
            r_ref = None
        else:
            a_ref, b_ref, r_ref, o_ref, acc = refs
        if exact:
            p = jnp.dot(a_ref[...], b_ref[...], precision=HI, preferred_element_type=F32)
        else:
            p = _dot(a_ref[...], b_ref[...])

        def finish(total):
            if r_ref is not None:
                total = total + r_ref[...]
            o_ref[...] = total.astype(out_dtype)

        if nk == 1:
            finish(p)
        else:
            k = pl.program_id(2)

            @pl.when(k == 0)
            def _():
                acc[...] = p

            @pl.when(jnp.logical_and(k > 0, k < nk - 1))
            def _():
                acc[...] += p

            @pl.when(k == nk - 1)
            def _():
                finish(acc[...] + p)

    in_specs = [pl.BlockSpec((tm, tk), lambda i, j, k: (i, k)),
                pl.BlockSpec((tk, tn), lambda i, j, k: (k, j))]
    args = [a, b]
    if res is not None:
        in_specs.append(pl.BlockSpec((tm, tn), lambda i, j, k: (i, j)))
        args.append(res)
    acc_shape = (tm, tn) if nk > 1 else (8, 128)
    return pl.pallas_call(
        body, name=name, grid=(M // tm, N // tn, nk),
        in_specs=in_specs, out_specs=pl.BlockSpec((tm, tn), lambda i, j, k: (i, j)),
        out_shape=jax.ShapeDtypeStruct((M, N), out_dtype),
        scratch_shapes=[pltpu.VMEM(acc_shape, F32)],
        compiler_params=_params("parallel", "parallel", "arbitrary"),
    )(*args)


def mm_tn(a, g, name):
    T, Ka = a.shape
    N = g.shape[1]
    tka, tn, tt = _tile(Ka, 1536), _tile(N, 1536), _rows(T, 512)
    nt = T // tt

    def body(a_ref, g_ref, o_ref):
        t = pl.program_id(2)
        p = _dot_tn(a_ref[...], g_ref[...])

        @pl.when(t == 0)
        def _():
            o_ref[...] = p

        @pl.when(t > 0)
        def _():
            o_ref[...] += p

    return pl.pallas_call(
        body, name=name, grid=(Ka // tka, N // tn, nt),
        in_specs=[pl.BlockSpec((tt, tka), lambda i, j, t: (t, i)),
                  pl.BlockSpec((tt, tn), lambda i, j, t: (t, j))],
        out_specs=pl.BlockSpec((tka, tn), lambda i, j, t: (i, j)),
        out_shape=jax.ShapeDtypeStruct((Ka, N), F32),
        compiler_params=_params("parallel", "parallel", "arbitrary"),
    )(a, g)


def rms_fwd(x, gains, name):
    T, D = x.shape
    tt = _rows(T, 512)
    ng = len(gains)

    def body(*refs):
        x_ref = refs[0]
        g_refs = refs[1:1 + ng]
        o_refs = refs[1 + ng:]
        xf = x_ref[...]
        y = xf * lax.rsqrt(jnp.mean(xf * xf, axis=-1, keepdims=True) + EPS)
        for g_ref, o_ref in zip(g_refs, o_refs):
            o_ref[...] = (y * g_ref[...]).astype(o_ref.dtype)

    row = pl.BlockSpec((tt, D), lambda i: (i, 0))
    gsp = pl.BlockSpec((1, D), lambda i: (0, 0))
    return pl.pallas_call(
        body, name=name, grid=(T // tt,),
        in_specs=[row] + [gsp] * ng, out_specs=[row] * ng,
        out_shape=[jax.ShapeDtypeStruct((T, D), MM_DTYPE)] * ng,
        compiler_params=_params("parallel"),
    )(x, *gains)


def rms_bwd(x, dres, branches, name):
    T, D = x.shape
    tt = _rows(T, 256)
    nb = len(branches)

    def body(*refs):
        x_ref, r_ref = refs[0], refs[1]
        dy_refs = refs[2:2 + nb]
        g_refs = refs[2 + nb:2 + 2 * nb]
        dx_ref = refs[2 + 2 * nb]
        dg_refs = refs[3 + 2 * nb:]
        i = pl.program_id(0)
        xf = x_ref[...]
        r = lax.rsqrt(jnp.mean(xf * xf, axis=-1, keepdims=True) + EPS)
        xh = xf * r
        dx = r_ref[...]
        for dy_ref, g_ref, dg_ref in zip(dy_refs, g_refs, dg_refs):
            dy = dy_ref[...].astype(F32)
            dyg = dy * g_ref[...]
            dx = dx + r * (dyg - xh * jnp.mean(dyg * xh, axis=-1, keepdims=True))
            part = jnp.sum(dy * xh, axis=0, keepdims=True)

            @pl.when(i == 0)
            def _():
                dg_ref[...] = part

            @pl.when(i > 0)
            def _():
                dg_ref[...] += part
        dx_ref[...] = dx

    row = pl.BlockSpec((tt, D), lambda i: (i, 0))
    gsp = pl.BlockSpec((1, D), lambda i: (0, 0))
    outs = pl.pallas_call(
        body, name=name, grid=(T // tt,),
        in_specs=[row, row] + [row] * nb + [gsp] * nb,
        out_specs=[row] + [gsp] * nb,
        out_shape=[jax.ShapeDtypeStruct((T, D), F32)] + [jax.ShapeDtypeStruct((1, D), F32)] * nb,
        compiler_params=_params("arbitrary"),
    )(x, dres, *[b[0] for b in branches], *[b[1] for b in branches])
    return outs[0], outs[1:]


def loss_head(x, target, gain):
    T, D = x.shape
    tt = _rows(T, 256)

    def body(x_ref, t_ref, g_ref, dx_ref, dg_ref, loss_ref):
        i = pl.program_id(0)
        xf = x_ref[...]
        g = g_ref[...]
        r = lax.rsqrt(jnp.mean(xf * xf, axis=-1, keepdims=True) + EPS)
        xh = xf * r
        e = xh * g - t_ref[...]
        lpart = 0.5 * jnp.sum(jnp.sum(e * e, axis=1, keepdims=True), axis=0, keepdims=True) / D
        dy = e / D
        dyg = dy * g
        dx_ref[...] = r * (dyg - xh * jnp.mean(dyg * xh, axis=-1, keepdims=True))
        gpart = jnp.sum(dy * xh, axis=0, keepdims=True)
        lrow = jnp.broadcast_to(lpart, (1, 128))

        @pl.when(i == 0)
        def _():
            dg_ref[...] = gpart
            loss_ref[...] = lrow

        @pl.when(i > 0)
        def _():
            dg_ref[...] += gpart
            loss_ref[...] += lrow

    row = pl.BlockSpec((tt, D), lambda i: (i, 0))
    gsp = pl.BlockSpec((1, D), lambda i: (0, 0))
    return pl.pallas_call(
        body, name="loss_head", grid=(T // tt,),
        in_specs=[row, row, gsp],
        out_specs=[row, gsp, pl.BlockSpec((1, 128), lambda i: (0, 0))],
        out_shape=[jax.ShapeDtypeStruct((T, D), F32), jax.ShapeDtypeStruct((1, D), F32),
                   jax.ShapeDtypeStruct((1, 128), F32)],
        compiler_params=_params("arbitrary"),
    )(x, target, gain)


def _shift_down(u, prev8, first, k):
    rolled = pltpu.roll(u, k, 0)
    rid = lax.broadcasted_iota(jnp.int32, u.shape, 0)
    halo = jnp.where(first, 0.0, prev8)
    out = rolled
    for j in range(k):
        out = jnp.where(rid == j, halo[8 - k + j:8 - k + j + 1, :], out)
    return out


def _conv3(u, prev8, first, w, b):
    return (_shift_down(u, prev8, first, 2) * w[0:1, :] + _shift_down(u, prev8, first, 1) * w[1:2, :]
            + u * w[2:3, :] + b)


def conv_act_fwd(u, w, b, name):
    T = u.shape[0]
    tt = _rows(T, 512)
    nj = D_FF // FF_TC

    def body(u_ref, p_ref, w_ref, b_ref, o_ref):
        first = pl.program_id(1) == 0
        c = _conv3(u_ref[...], p_ref[...], first, w_ref[...], b_ref[...])
        cg, cv = c[:, :FF_TC], c[:, FF_TC:]
        o_ref[...] = (cg * _sigmoid_tanh(cg) * cv).astype(o_ref.dtype)

    return pl.pallas_call(
        body, name=name, grid=(nj, T // tt),
        in_specs=[pl.BlockSpec((tt, 2 * FF_TC), lambda j, i: (i, j)),
                  pl.BlockSpec((8, 2 * FF_TC), lambda j, i: (jnp.maximum(i * (tt // 8) - 1, 0), j)),
                  pl.BlockSpec((3, 2 * FF_TC), lambda j, i: (0, j)),
                  pl.BlockSpec((1, 2 * FF_TC), lambda j, i: (0, j))],
        out_specs=pl.BlockSpec((tt, FF_TC), lambda j, i: (i, j)),
        out_shape=jax.ShapeDtypeStruct((T, D_FF), MM_DTYPE),
        compiler_params=_params("parallel", "parallel"),
    )(u, u, w, b)


def conv_act_bwd(u, da, w, b, name):
    T = u.shape[0]
    tt = _rows(T, 512)
    nt = T // tt
    nj = D_FF // FF_TC
    te = tt + 8

    def body(u_ref, p_ref, n_ref, da_ref, dan_ref, w_ref, b_ref, du_ref, dw_ref, db_ref):
        i = pl.program_id(1)
        first = i == 0
        last = i == nt - 1
        w = w_ref[...]
        ue = jnp.concatenate([u_ref[...], n_ref[...]], axis=0)
        dae = jnp.concatenate([da_ref[...], jnp.where(last, 0.0, dan_ref[...])], axis=0)
        um2 = _shift_down(ue, p_ref[...], first, 2)
        um1 = _shift_down(ue, p_ref[...], first, 1)
        c = um2 * w[0:1, :] + um1 * w[1:2, :] + ue * w[2:3, :] + b_ref[...]
        cg, cv = c[:, :FF_TC], c[:, FF_TC:]
        s = _sigmoid_tanh(cg)
        dcg = dae * cv * (s * (1.0 + cg * (1.0 - s)))
        dcv = dae * (cg * s)
        dc = jnp.concatenate([dcg, dcv], axis=1)
        du = (dc * w[2:3, :] + pltpu.roll(dc, te - 1, 0) * w[1:2, :] + pltpu.roll(dc, te - 2, 0) * w[0:1, :])
        du_ref[...] = du[:tt, :].astype(du_ref.dtype)
        dcm = dc[:tt, :]
        dwp = jnp.concatenate([jnp.sum(dcm * um2[:tt, :], axis=0, keepdims=True),
                               jnp.sum(dcm * um1[:tt, :], axis=0, keepdims=True),
                               jnp.sum(dcm * ue[:tt, :], axis=0, keepdims=True)], axis=0)
        dbp = jnp.sum(dcm, axis=0, keepdims=True)

        @pl.when(first)
        def _():
            dw_ref[...] = dwp
            db_ref[...] = dbp

        @pl.when(i > 0)
        def _():
            dw_ref[...] += dwp
            db_ref[...] += dbp

    nb8 = T // 8
    return pl.pallas_call(
        body, name=name, grid=(nj, nt),
        in_specs=[pl.BlockSpec((tt, 2 * FF_TC), lambda j, i: (i, j)),
                  pl.BlockSpec((8, 2 * FF_TC), lambda j, i: (jnp.maximum(i * (tt // 8) - 1, 0), j)),
                  pl.BlockSpec((8, 2 * FF_TC), lambda j, i: (jnp.minimum((i + 1) * (tt // 8), nb8 - 1), j)),
                  pl.BlockSpec((tt, FF_TC), lambda j, i: (i, j)),
                  pl.BlockSpec((8, FF_TC), lambda j, i: (jnp.minimum((i + 1) * (tt // 8), nb8 - 1), j)),
                  pl.BlockSpec((3, 2 * FF_TC), lambda j, i: (0, j)),
                  pl.BlockSpec((1, 2 * FF_TC), lambda j, i: (0, j))],
        out_specs=[pl.BlockSpec((tt, 2 * FF_TC), lambda j, i: (i, j)),
                   pl.BlockSpec((3, 2 * FF_TC), lambda j, i: (0, j)),
                   pl.BlockSpec((1, 2 * FF_TC), lambda j, i: (0, j))],
        out_shape=[jax.ShapeDtypeStruct((T, 2 * D_FF), MM_DTYPE),
                   jax.ShapeDtypeStruct((3, 2 * D_FF), F32),
                   jax.ShapeDtypeStruct((1, 2 * D_FF), F32)],
        compiler_params=_params("parallel", "arbitrary"),
    )(u, u, u, da, da, w, b)


def _interleave(a):
    lead = a.shape[:-1]
    nj = D_FF // FF_TC
    return jnp.swapaxes(a.reshape(*lead, 2, nj, FF_TC), -3, -2).reshape(*lead, 2 * D_FF)


def _deinterleave(a):
    lead = a.shape[:-1]
    nj = D_FF // FF_TC
    return jnp.swapaxes(a.reshape(*lead, nj, 2, FF_TC), -3, -2).reshape(*lead, 2 * D_FF)


A_GC = 2
A_TB = A_GC * A_CHUNK


def gate_prep(z, bias128):
    T = z.shape[0]
    tt = _rows(T, 512)

    def body(z_ref, b_ref, gc_ref, gr_ref):
        pre = z_ref[...] + b_ref[...]
        sc = SOFTCAP * jnp.tanh(pre / SOFTCAP)
        lf = jnp.minimum(sc, 0.0) - jnp.log(1.0 + jnp.exp(-jnp.abs(sc)))
        col = lax.broadcasted_iota(jnp.int32, pre.shape, 1)
        isf = jnp.logical_and(col >= A_HEADS, col < 2 * A_HEADS)
        r = lax.broadcasted_iota(jnp.int32, (tt, tt), 0)
        c = lax.broadcasted_iota(jnp.int32, (tt, tt), 1)
        tri = jnp.logical_and(jnp.right_shift(r, 6) == jnp.right_shift(c, 6), c <= r).astype(F32)
        bcum = jnp.dot(tri, jnp.where(isf, lf, 0.0), precision=HI, preferred_element_type=F32)
        g = jnp.where(col < A_HEADS, sc, jnp.where(isf, bcum, 0.0))
        gc_ref[...] = g
        for s in range(tt // 128):
            gr_ref[s] = g[s * 128:(s + 1) * 128, :].T[0:8, :]

    return pl.pallas_call(
        body, name="gate_prep", grid=(T // tt,),
        in_specs=[pl.BlockSpec((tt, 128), lambda i: (i, GATE_COL // 128)),
                  pl.BlockSpec((1, 128), lambda i: (0, 0))],
        out_specs=[pl.BlockSpec((tt, 128), lambda i: (i, 0)),
                   pl.BlockSpec((tt // 128, 8, 128), lambda i: (i, 0, 0))],
        out_shape=[jax.ShapeDtypeStruct((T, 128), F32), jax.ShapeDtypeStruct((T // 128, 8, 128), F32)],
        compiler_params=_params("parallel"),
    )(z, bias128)


def _chunk_fwd(qh, kh, vh, bc, br, lir, C, n, m, causal):
    A = _dot_nt(qh, kh)
    logD = jnp.where(causal, bc - br + lir, -jnp.inf)
    m_inter = bc + m
    m_t = jnp.maximum(m_inter, jnp.max(logD, axis=1, keepdims=True))
    E = jnp.exp(logD - m_t)
    Sm = A * E
    wi = jnp.exp(m_inter - m_t)
    num = _dot(Sm, vh) + wi * _dot(qh, C)
    qn = jnp.sum(qh.astype(F32) * n, axis=1, keepdims=True)
    den = jnp.sum(Sm, axis=1, keepdims=True) + wi * qn
    gs = jnp.maximum(jnp.abs(den), jnp.exp(-m_t))
    return E, Sm, wi, num, den, gs, m_t


def _state_weights(bc, lic, br, lir, m):
    bL = bc[A_CHUNK - 1:A_CHUNK, :]
    m_new = jnp.maximum(bL + m, jnp.max(bL - br + lir, axis=1, keepdims=True))
    wk = jnp.exp(bL - bc + lic - m_new)
    decay = jnp.exp(bL + m - m_new)
    return wk, decay, m_new


def _head_slices(h):
    return (slice(h * A_QK, (h + 1) * A_QK), slice(h * A_V, (h + 1) * A_V))


def mlstm_fwd(z, gcol, grow, hng):
    T = z.shape[0]
    NC = T // A_CHUNK
    scale = A_QK ** -0.5

    def body(q_ref, k_ref, v_ref, o_ref, gc_ref, gr_ref, hng_ref, hg_ref, Cs_ref, ns_ref, ms_ref,
             C_sc, n_sc, m_sc):
        @pl.when(pl.program_id(0) == 0)
        def _():
            C_sc[...] = jnp.zeros_like(C_sc)
            n_sc[...] = jnp.zeros_like(n_sc)
            m_sc[...] = jnp.zeros_like(m_sc)

        ri = lax.broadcasted_iota(jnp.int32, (A_CHUNK, A_CHUNK), 0)
        ci = lax.broadcasted_iota(jnp.int32, (A_CHUNK, A_CHUNK), 1)
        causal = ri >= ci
        gr = gr_ref[0]
        for c in range(A_GC):
            rows = slice(c * A_CHUNK, (c + 1) * A_CHUNK)
            gc = gc_ref[rows, :]
            grc = gr[:, c * A_CHUNK:(c + 1) * A_CHUNK]
            for h in range(A_HEADS):
                sk, sv = _head_slices(h)
                qh = (q_ref[rows, sk] * scale).astype(MM_DTYPE)
                kh = k_ref[rows, sk].astype(MM_DTYPE)
                vh = v_ref[rows, sv].astype(MM_DTYPE)
                lic, bc = gc[:, h:h + 1], gc[:, A_HEADS + h:A_HEADS + h + 1]
                lir, br = grc[h:h + 1, :], grc[A_HEADS + h:A_HEADS + h + 1, :]
                C, n, m = C_sc[h], n_sc[h], m_sc[h][:, 0:1]
                Cs_ref[c, h] = C
                ns_ref[c, h] = n
                ms_ref[c, h] = m_sc[h]
                _, _, _, num, _, gs, _ = _chunk_fwd(qh, kh, vh, bc, br, lir, C, n, m, causal)
                hh = num / gs
                hn = hh * lax.rsqrt(jnp.mean(hh * hh, axis=1, keepdims=True) + EPS) * hng_ref[:, sv]
                hg_ref[rows, sv] = (hn * _sigmoid(o_ref[rows, sv])).astype(hg_ref.dtype)
                wk, decay, m_new = _state_weights(bc, lic, br, lir, m)
                kw = kh.astype(F32) * wk
                C_sc[h] = decay * C + _dot_tn(kw, vh)
                n_sc[h] = decay * n + jnp.sum(kw, axis=0, keepdims=True)
                m_sc[h] = jnp.broadcast_to(m_new, (1, 128))

    tok = lambda w, cb: pl.BlockSpec((A_TB, w), lambda i: (i, cb))
    return pl.pallas_call(
        body, name="mlstm_fwd", grid=(NC // A_GC,),
        in_specs=[tok(512, 0), tok(512, 1), tok(1024, 1), tok(1024, 2),
                  pl.BlockSpec((A_TB, 128), lambda i: (i, 0)),
                  pl.BlockSpec((1, 8, 128), lambda i: (i, 0, 0)),
                  pl.BlockSpec((1, 1024), lambda i: (0, 0))],
        out_specs=[pl.BlockSpec((A_TB, 1024), lambda i: (i, 0)),
                   pl.BlockSpec((A_GC, A_HEADS, A_QK, A_V), lambda i: (i, 0, 0, 0)),
                   pl.BlockSpec((A_GC, A_HEADS, 1, 128), lambda i: (i, 0, 0, 0)),
                   pl.BlockSpec((A_GC, A_HEADS, 1, 128), lambda i: (i, 0, 0, 0))],
        out_shape=[jax.ShapeDtypeStruct((T, 1024), MM_DTYPE),
                   jax.ShapeDtypeStruct((NC, A_HEADS, A_QK, A_V), F32),
                   jax.ShapeDtypeStruct((NC, A_HEADS, 1, 128), F32),
                   jax.ShapeDtypeStruct((NC, A_HEADS, 1, 128), F32)],
        scratch_shapes=[pltpu.VMEM((A_HEADS, A_QK, A_V), F32), pltpu.VMEM((A_HEADS, 1, 128), F32),
                        pltpu.VMEM((A_HEADS, 1, 128), F32)],
        compiler_params=_params("arbitrary"),
    )(z, z, z, z, gcol, grow, hng)


def mlstm_bwd(z, gcol, grow, hng, bias128, Cs, ns, ms, dhg):
    T = z.shape[0]
    NC = T // A_CHUNK
    nsteps = NC // A_GC
    scale = A_QK ** -0.5

    def body(q_ref, k_ref, v_ref, o_ref, zg_ref, gc_ref, gr_ref, hng_ref, b_ref, Cs_ref, ns_ref, ms_ref,
             dhg_ref, dz_ref, dgn_ref, dbif_ref, dC_sc, dn_sc):
        @pl.when(pl.program_id(0) == 0)
        def _():
            dC_sc[...] = jnp.zeros_like(dC_sc)
            dn_sc[...] = jnp.zeros_like(dn_sc)
            dgn_ref[...] = jnp.zeros_like(dgn_ref)
            dbif_ref[...] = jnp.zeros_like(dbif_ref)

        ri = lax.broadcasted_iota(jnp.int32, (A_CHUNK, A_CHUNK), 0)
        ci = lax.broadcasted_iota(jnp.int32, (A_CHUNK, A_CHUNK), 1)
        causal = ri >= ci
        upper = (ci >= ri).astype(F32)
        rid = lax.broadcasted_iota(jnp.int32, (A_CHUNK, 1), 0)
        col = lax.broadcasted_iota(jnp.int32, (A_CHUNK, 128), 1)
        gr = gr_ref[0]
        for c in reversed(range(A_GC)):
            rows = slice(c * A_CHUNK, (c + 1) * A_CHUNK)
            gc = gc_ref[rows, :]
            grc = gr[:, c * A_CHUNK:(c + 1) * A_CHUNK]
            dG = jnp.zeros((A_CHUNK, 128), F32)
            for h in range(A_HEADS):
                sk, sv = _head_slices(h)
                qh = (q_ref[rows, sk] * scale).astype(MM_DTYPE)
                kh = k_ref[rows, sk].astype(MM_DTYPE)
                vh = v_ref[rows, sv].astype(MM_DTYPE)
                qf, kf = qh.astype(F32), kh.astype(F32)
                lic, bc = gc[:, h:h + 1], gc[:, A_HEADS + h:A_HEADS + h + 1]
                lir, br = grc[h:h + 1, :], grc[A_HEADS + h:A_HEADS + h + 1, :]
                C, n, m = Cs_ref[c, h], ns_ref[c, h], ms_ref[c, h][:, 0:1]
                dC, dn = dC_sc[h], dn_sc[h]
                E, Sm, wi, num, den, gs, m_t = _chunk_fwd(qh, kh, vh, bc, br, lir, C, n, m, causal)
                wk, decay, _ = _state_weights(bc, lic, br, lir, m)
                hh = num / gs
                r = lax.rsqrt(jnp.mean(hh * hh, axis=1, keepdims=True) + EPS)
                gn = hng_ref[:, sv]
                o = o_ref[rows, sv]
                s = _sigmoid(o)
                dhg_h = dhg_ref[rows, sv]
                dhn = dhg_h * s
                dz_ref[rows, 2048 + h * A_V:2048 + (h + 1) * A_V] = dhg_h * (hh * r * gn) * s * (1.0 - s)
                dgn_ref[:, sv] += jnp.sum(dhn * hh * r, axis=0, keepdims=True)
                dyg = dhn * gn
                dh = r * dyg - hh * (r * r * r) * jnp.mean(dyg * hh, axis=1, keepdims=True)
                dnum = dh / gs
                live = (jnp.abs(den) > jnp.exp(-m_t)).astype(F32)
                dden = -jnp.sum(dh * hh, axis=1, keepdims=True) / gs * jnp.sign(den) * live
                dSE = jnp.where(causal, _dot_nt(dnum, vh) + dden, 0.0) * E
                dq = _dot(dSE, kh) + wi * (_dot_nt(dnum, C) + dden * n)
                dk_inter = wk * (_dot_nt(vh, dC) + dn)
                dk = _dot_tn(dSE, qh) + dk_inter
                dv = _dot_tn(Sm, dnum) + wk * _dot(kh, dC)
                dz_ref[rows, sk] = dq * scale
                dz_ref[rows, 512 + h * A_QK:512 + (h + 1) * A_QK] = dk
                dz_ref[rows, 1024 + h * A_V:1024 + (h + 1) * A_V] = dv
                dli = jnp.sum(kf * dk, axis=1, keepdims=True)
                db = jnp.sum(qf * dq, axis=1, keepdims=True) - dli
                usum = jnp.sum(jnp.sum(kf * dk_inter, axis=1, keepdims=True), axis=0, keepdims=True)
                ddecay = (jnp.sum(jnp.sum(dC * C, axis=1, keepdims=True), axis=0, keepdims=True)
                          + jnp.sum(dn * n, axis=1, keepdims=True))
                db = db + jnp.where(rid == A_CHUNK - 1, usum + ddecay * decay, 0.0)
                dG = dG + jnp.where(col == h, dli, 0.0) + jnp.where(col == A_HEADS + h, db, 0.0)
                dC_sc[h] = decay * dC + _dot_tn(qf * wi, dnum)
                dn_sc[h] = decay * dn + jnp.sum(qf * (wi * dden), axis=0, keepdims=True)
            dlf = jnp.dot(upper, dG, precision=HI, preferred_element_type=F32)
            pre = zg_ref[rows, :] + b_ref[...]
            th = jnp.tanh(pre / SOFTCAP)
            dcap = 1.0 - th * th
            dpre = jnp.where(col < A_HEADS, dG * dcap,
                             jnp.where(col < 2 * A_HEADS, dlf * _sigmoid(-SOFTCAP * th) * dcap, 0.0))
            dz_ref[rows, GATE_COL:GATE_COL + 128] = dpre
            dbif_ref[...] += jnp.sum(dpre, axis=0, keepdims=True)

    rev = lambda i: nsteps - 1 - i
    tok = lambda w, cb: pl.BlockSpec((A_TB, w), lambda i: (rev(i), cb))
    st = lambda a, b: pl.BlockSpec((A_GC, A_HEADS, a, b), lambda i: (rev(i), 0, 0, 0))
    return pl.pallas_call(
        body, name="mlstm_bwd", grid=(nsteps,),
        in_specs=[tok(512, 0), tok(512, 1), tok(1024, 1), tok(1024, 2), tok(128, GATE_COL // 128),
                  pl.BlockSpec((A_TB, 128), lambda i: (rev(i), 0)),
                  pl.BlockSpec((1, 8, 128), lambda i: (rev(i), 0, 0)),
                  pl.BlockSpec((1, 1024), lambda i: (0, 0)),
                  pl.BlockSpec((1, 128), lambda i: (0, 0)),
                  st(A_QK, A_V), st(1, 128), st(1, 128),
                  pl.BlockSpec((A_TB, 1024), lambda i: (rev(i), 0))],
        out_specs=[pl.BlockSpec((A_TB, A_IN_PAD), lambda i: (rev(i), 0)),
                   pl.BlockSpec((1, 1024), lambda i: (0, 0)),
                   pl.BlockSpec((1, 128), lambda i: (0, 0))],
        out_shape=[jax.ShapeDtypeStruct((T, A_IN_PAD), F32), jax.ShapeDtypeStruct((1, 1024), F32),
                   jax.ShapeDtypeStruct((1, 128), F32)],
        scratch_shapes=[pltpu.VMEM((A_HEADS, A_QK, A_V), F32), pltpu.VMEM((A_HEADS, 1, 128), F32)],
        compiler_params=_params("arbitrary"),
    )(z, z, z, z, z, gcol, grow, hng, bias128, Cs, ns, ms, dhg)


def _t5_bucket(dist):
    max_exact = REL_BUCKETS // 2
    d = np.maximum(dist, 0)
    log_ratio = np.log(np.maximum(d, 1) / max_exact) / math.log(REL_MAX_DIST / max_exact)
    large = np.minimum(max_exact + (log_ratio * (REL_BUCKETS - max_exact)).astype(np.int64), REL_BUCKETS - 1)
    return np.where(d < max_exact, d, large).astype(np.int32)


def _group_bucket(g):
    delta = B_BLOCK + np.arange(B_BLOCK)[:, None] - np.arange(2 * B_BLOCK)[None, :]
    return _t5_bucket(delta * DILATIONS[g])


def _band_mask(n):
    ri = lax.broadcasted_iota(jnp.int32, (B_BLOCK, 2 * B_BLOCK), 0)
    ci = lax.broadcasted_iota(jnp.int32, (B_BLOCK, 2 * B_BLOCK), 1)
    band = jnp.logical_and(ci >= ri, ci <= ri + B_BLOCK)
    return jnp.logical_and(band, jnp.logical_or(ci >= B_BLOCK, n > 0))


def _both(p_ref, c_ref, sl):
    return jnp.concatenate([p_ref[:, sl], c_ref[:, sl]], axis=0)


def _scores(qh, kh, bias_h, valid):
    return jnp.where(valid, _dot_nt(qh, kh) * (B_DH ** -0.5) + bias_h, -jnp.inf)


def _attn_specs(g, dil):
    qs = pl.BlockSpec((B_BLOCK, 1024), lambda r, n: (n, r * 3 + g))
    kc = pl.BlockSpec((B_BLOCK, 1024), lambda r, n: (n, r * 6 + g))
    kp = pl.BlockSpec((B_BLOCK, 1024), lambda r, n: (jnp.maximum(n - 1, 0), r * 6 + g))
    vc = pl.BlockSpec((B_BLOCK, 1024), lambda r, n: (n, r * 6 + 3 + g))
    vp = pl.BlockSpec((B_BLOCK, 1024), lambda r, n: (jnp.maximum(n - 1, 0), r * 6 + 3 + g))
    bias = pl.BlockSpec((B_HEADS, B_BLOCK, 2 * B_BLOCK), lambda r, n: (0, 0, 0))
    wide = pl.BlockSpec((B_BLOCK, 1024), lambda r, n: (n, r))
    narrow = pl.BlockSpec((B_BLOCK, 128), lambda r, n: (n, r))
    return qs, kp, kc, vp, vc, bias, wide, narrow


def attn_fwd(qd, kv, bias, g):
    T = qd.shape[0]
    dil = DILATIONS[g]
    Tv = T // dil
    nb = Tv // B_BLOCK
    qs, kp, kc, vp, vc, bsp, wide, narrow = _attn_specs(g, dil)

    def body(q_ref, kp_ref, kc_ref, vp_ref, vc_ref, b_ref, o_ref, lse_ref):
        valid = _band_mask(pl.program_id(1))
        lse_ref[...] = jnp.zeros_like(lse_ref)
        heads = [slice(h * B_DH, (h + 1) * B_DH) for h in range(B_HEADS)]
        S = [_scores(q_ref[:, sl], _both(kp_ref, kc_ref, sl), b_ref[h], valid) for h, sl in enumerate(heads)]
        P, L = [], []
        for h in range(B_HEADS):
            m = jnp.max(S[h], axis=1, keepdims=True)
            p = jnp.exp(S[h] - m)
            l = jnp.sum(p, axis=1, keepdims=True)
            lse_ref[:, h:h + 1] = m + jnp.log(l)
            P.append(p.astype(MM_DTYPE))
            L.append(l)
        for h, sl in enumerate(heads):
            o_ref[:, sl] = _dot(P[h], _both(vp_ref, vc_ref, sl)) / L[h]

    o, lse = pl.pallas_call(
        body, name=f"attn_fwd_g{g}", grid=(dil, nb),
        in_specs=[qs, kp, kc, vp, vc, bsp], out_specs=[wide, narrow],
        out_shape=[jax.ShapeDtypeStruct((Tv, dil * 1024), F32), jax.ShapeDtypeStruct((Tv, dil * 128), F32)],
        compiler_params=_params("parallel", "parallel"),
    )(qd.reshape(Tv, dil * 3072), *([kv.reshape(Tv, dil * 6144)] * 4), bias)
    return o.reshape(T, 1024), lse.reshape(T, 128)


def attn_bwd(qd, kv, bias, datt, lse, delta, g):
    T = qd.shape[0]
    dil = DILATIONS[g]
    Tv = T // dil
    nb = Tv // B_BLOCK
    qs, kp, kc, vp, vc, bsp, wide, narrow = _attn_specs(g, dil)

    def body(q_ref, kp_ref, kc_ref, vp_ref, vc_ref, b_ref, do_ref, lse_ref, dl_ref,
             dq_ref, dkc_ref, dkp_ref, dvc_ref, dvp_ref, db_ref):
        @pl.when(jnp.logical_and(pl.program_id(0) == 0, pl.program_id(1) == 0))
        def _():
            db_ref[...] = jnp.zeros_like(db_ref)

        valid = _band_mask(pl.program_id(1))
        heads = [slice(h * B_DH, (h + 1) * B_DH) for h in range(B_HEADS)]
        P, DS = [], []
        for h, sl in enumerate(heads):
            doh = do_ref[:, sl].astype(MM_DTYPE)
            s = _scores(q_ref[:, sl], _both(kp_ref, kc_ref, sl), b_ref[h], valid)
            p = jnp.exp(s - lse_ref[:, h:h + 1])
            ds = p * (_dot_nt(doh, _both(vp_ref, vc_ref, sl)) - dl_ref[:, h:h + 1])
            db_ref[h] += ds
            P.append(p.astype(MM_DTYPE))
            DS.append((ds * (B_DH ** -0.5)).astype(MM_DTYPE))
        for h, sl in enumerate(heads):
            qh, doh = q_ref[:, sl], do_ref[:, sl].astype(MM_DTYPE)
            dq_ref[:, sl] = _dot(DS[h], _both(kp_ref, kc_ref, sl))
            dk = _dot_tn(DS[h], qh)
            dv = _dot_tn(P[h], doh)
            dkp_ref[:, sl], dkc_ref[:, sl] = dk[:B_BLOCK], dk[B_BLOCK:]
            dvp_ref[:, sl], dvc_ref[:, sl] = dv[:B_BLOCK], dv[B_BLOCK:]

    big = jax.ShapeDtypeStruct((Tv, dil * 1024), F32)
    outs = pl.pallas_call(
        body, name=f"attn_bwd_g{g}", grid=(dil, nb),
        in_specs=[qs, kp, kc, vp, vc, bsp, wide, narrow, narrow],
        out_specs=[wide] * 5 + [bsp],
        out_shape=[big] * 5 + [jax.ShapeDtypeStruct((B_HEADS, B_BLOCK, 2 * B_BLOCK), F32)],
        compiler_params=_params("arbitrary", "arbitrary"),
    )(qd.reshape(Tv, dil * 3072), *([kv.reshape(Tv, dil * 6144)] * 4), bias,
      datt.reshape(Tv, dil * 1024), lse.reshape(Tv, dil * 128), delta.reshape(Tv, dil * 128))
    return [o.reshape(T, 1024) for o in outs[:5]] + [outs[5]]


def _head_expand():
    e = np.zeros((128, 1024), np.float32)
    for h in range(B_HEADS):
        e[h, h * B_DH:(h + 1) * B_DH] = 1.0
    return e


def attn_merge(os_, lses):
    T = os_[0].shape[0]
    tt = _rows(T, 256)
    expand = jnp.asarray(_head_expand())

    def body(o0, o1, o2, l0, l1, l2, e_ref, ob_ref, of_ref, lse_ref):
        ls = [l0[...], l1[...], l2[...]]
        m = jnp.maximum(jnp.maximum(ls[0], ls[1]), ls[2])
        ex = [jnp.exp(l - m) for l in ls]
        tot = ex[0] + ex[1] + ex[2]
        lse_ref[...] = m + jnp.log(tot)
        out = jnp.zeros((tt, 1024), F32)
        for e, o in zip(ex, (o0, o1, o2)):
            w = jnp.dot(e / tot, e_ref[...], precision=HI, preferred_element_type=F32)
            out = out + w * o[...]
        of_ref[...] = out
        ob_ref[...] = out.astype(ob_ref.dtype)

    wide = pl.BlockSpec((tt, 1024), lambda i: (i, 0))
    narrow = pl.BlockSpec((tt, 128), lambda i: (i, 0))
    return pl.pallas_call(
        body, name="attn_merge", grid=(T // tt,),
        in_specs=[wide] * 3 + [narrow] * 3 + [pl.BlockSpec((128, 1024), lambda i: (0, 0))],
        out_specs=[wide, wide, narrow],
        out_shape=[jax.ShapeDtypeStruct((T, 1024), MM_DTYPE), jax.ShapeDtypeStruct((T, 1024), F32),
                   jax.ShapeDtypeStruct((T, 128), F32)],
        compiler_params=_params("parallel"),
    )(*os_, *lses, expand)


def attn_delta(datt, out):
    T = datt.shape[0]
    tt = _rows(T, 512)
    expand_t = jnp.asarray(_head_expand().T.copy())

    def body(d_ref, o_ref, e_ref, dl_ref):
        dl_ref[...] = jnp.dot(d_ref[...] * o_ref[...], e_ref[...], precision=HI, preferred_element_type=F32)

    wide = pl.BlockSpec((tt, 1024), lambda i: (i, 0))
    return pl.pallas_call(
        body, name="attn_delta", grid=(T // tt,),
        in_specs=[wide, wide, pl.BlockSpec((1024, 128), lambda i: (0, 0))],
        out_specs=pl.BlockSpec((tt, 128), lambda i: (i, 0)),
        out_shape=jax.ShapeDtypeStruct((T, 128), F32),
        compiler_params=_params("parallel"),
    )(datt, out, expand_t)


def attn_combine(dqs, dkc, dkp, dvc, dvp):
    T = dqs[0].shape[0]
    tt = B_BLOCK
    nt = T // tt

    def body(*refs):
        i = pl.program_id(0)
        dq_refs, rest = refs[:3], refs[3:]
        c_refs, p_refs = rest[:6], rest[6:12]
        dq_ref, dkv_ref = rest[12], rest[13]
        for g in range(N_GROUPS):
            dq_ref[:, g * 1024:(g + 1) * 1024] = dq_refs[g][...].astype(dq_ref.dtype)
        for j in range(6):
            live = i + DILATIONS[j % 3] < nt
            tot = c_refs[j][...] + jnp.where(live, p_refs[j][...], 0.0)
            dkv_ref[:, j * 1024:(j + 1) * 1024] = tot.astype(dkv_ref.dtype)

    cur = pl.BlockSpec((tt, 1024), lambda i: (i, 0))

    def nxt(d):
        return pl.BlockSpec((tt, 1024), lambda i: (jnp.minimum(i + d, nt - 1), 0))

    return pl.pallas_call(
        body, name="attn_combine", grid=(nt,),
        in_specs=[cur] * 3 + [cur] * 6 + [nxt(DILATIONS[j % 3]) for j in range(6)],
        out_specs=[pl.BlockSpec((tt, 3072), lambda i: (i, 0)), pl.BlockSpec((tt, 6144), lambda i: (i, 0))],
        out_shape=[jax.ShapeDtypeStruct((T, 3072), MM_DTYPE), jax.ShapeDtypeStruct((T, 6144), MM_DTYPE)],
        compiler_params=_params("parallel"),
    )(*dqs, *dkc, *dvc, *dkp, *dvp)


def adamw(w, g, m, v, name):
    R, C = w.shape
    tr = R if R * C * 4 <= (1 << 20) else _rows(R, max(8, ((1 << 20) // (C * 4)) // 8 * 8))

    def body(w_ref, g_ref, m_ref, v_ref, d_ref, nm_ref, nv_ref):
        gg = g_ref[...]
        nm = ADAM_B1 * m_ref[...] + (1.0 - ADAM_B1) * gg
        nv = ADAM_B2 * v_ref[...] + (1.0 - ADAM_B2) * (gg * gg)
        m_hat = nm / (1.0 - ADAM_B1 ** ADAM_STEP)
        v_hat = nv / (1.0 - ADAM_B2 ** ADAM_STEP)
        d_ref[...] = -ADAM_LR * (m_hat / (jnp.sqrt(v_hat) + ADAM_EPS) + ADAM_WD * w_ref[...])
        nm_ref[...] = nm
        nv_ref[...] = nv

    blk = pl.BlockSpec((tr, C), lambda i: (i, 0))
    sds = jax.ShapeDtypeStruct((R, C), F32)
    return pl.pallas_call(
        body, name=name, grid=(R // tr,), in_specs=[blk] * 4, out_specs=[blk] * 3, out_shape=[sds] * 3,
        compiler_params=_params("parallel"),
    )(w, g, m, v)


def sum_slots(x, name):
    n, R, C = x.shape
    tr = _rows(R, 256)

    def body(x_ref, o_ref):
        acc = x_ref[0].astype(F32)
        for s in range(1, n):
            acc = acc + x_ref[s].astype(F32)
        o_ref[...] = acc

    return pl.pallas_call(
        body, name=name, grid=(R // tr,),
        in_specs=[pl.BlockSpec((n, tr, C), lambda i: (0, i, 0))],
        out_specs=pl.BlockSpec((tr, C), lambda i: (i, 0)),
        out_shape=jax.ShapeDtypeStruct((R, C), F32),
        compiler_params=_params("parallel"),
    )(x)


def add_pair(a, b, name, out_dtype):
    R, C = a.shape
    tr = _rows(R, 512)

    def body(a_ref, b_ref, o_ref):
        o_ref[...] = (a_ref[...].astype(F32) + b_ref[...].astype(F32)).astype(out_dtype)

    blk = pl.BlockSpec((tr, C), lambda i: (i, 0))
    return pl.pallas_call(
        body, name=name, grid=(R // tr,), in_specs=[blk, blk], out_specs=blk,
        out_shape=jax.ShapeDtypeStruct((R, C), out_dtype), compiler_params=_params("parallel"),
    )(a, b)


_ANY = pl.BlockSpec(memory_space=pl.ANY)
GROUP_ALL = ([(0, 0, 1), (0, 1, 0), (0, 1, 1), (1, 0, 0), (1, 0, 1), (1, 1, 0), (1, 1, 1)],
             lambda d: 4 * d[0] + 2 * d[1] + d[2])
GROUP_CHIPS = ([(0, 1, 0), (1, 0, 0), (1, 1, 0)], lambda d: 2 * d[0] + d[1])
GROUP_SIBLING = ([(0, 0, 1)], lambda d: d[2])


def _me():
    return lax.axis_index("x"), lax.axis_index("y"), lax.axis_index("c")


def _peer(me, flip):
    return tuple(1 - a if f else a for a, f in zip(me, flip))


def _group_exchange(x, name, group, scatter):
    flips, slot = group
    n = len(flips) + 1
    R, C = x.shape[-2:]

    def body(x_ref, o_ref, send_sems, recv_sems, local_sem):
        me = _me()
        mine = pltpu.make_async_copy(x_ref.at[slot(me)] if scatter else x_ref, o_ref.at[slot(me)], local_sem)
        mine.start()
        sends = []
        for k, flip in enumerate(flips):
            peer = _peer(me, flip)
            cp = pltpu.make_async_remote_copy(
                src_ref=x_ref.at[slot(peer)] if scatter else x_ref, dst_ref=o_ref.at[slot(me)],
                send_sem=send_sems.at[k], recv_sem=recv_sems.at[k], device_id=peer, device_id_type=MESH_ID)
            cp.start()
            sends.append(cp)
        for k, flip in enumerate(flips):
            peer = _peer(me, flip)
            pltpu.make_async_remote_copy(
                src_ref=o_ref.at[slot(me)], dst_ref=o_ref.at[slot(peer)], send_sem=send_sems.at[k],
                recv_sem=recv_sems.at[k], device_id=peer, device_id_type=MESH_ID).wait_recv()
        for cp in sends:
            cp.wait_send()
        mine.wait()

    return pl.pallas_call(
        body, name=name, in_specs=[_ANY], out_specs=_ANY,
        out_shape=jax.ShapeDtypeStruct((n, R, C), x.dtype),
        scratch_shapes=[pltpu.SemaphoreType.DMA((n - 1,)), pltpu.SemaphoreType.DMA((n - 1,)),
                        pltpu.SemaphoreType.DMA],
    )(x)


def group_gather(x, name, group):
    return _group_exchange(x, name, group, scatter=False)


def group_scatter(x, name, group):
    return _group_exchange(x, name, group, scatter=True)


WEIGHTS = ['a_norm_g', 'a_w_in', 'a_b_if', 'a_hnorm_g', 'a_w_out', 'kv_norm_g', 'w_kv', 'b_norm_g', 'b_w_q',
           'b_w_out', 'rel_bias', 'f_norm_g', 'f_w_up', 'f_conv_w', 'f_conv_b', 'f_w_down', 'final_norm_g']
SHARD_AXIS = {'a_norm_g': 1, 'a_w_in': 2, 'a_b_if': None, 'a_hnorm_g': 2, 'a_w_out': 1, 'kv_norm_g': None,
              'w_kv': 1, 'b_norm_g': None, 'b_w_q': 2, 'b_w_out': 1, 'rel_bias': None, 'f_norm_g': None,
              'f_w_up': 2, 'f_conv_w': 2, 'f_conv_b': None, 'f_w_down': 1, 'final_norm_g': None}
BIG = ['a_w_in', 'a_w_out', 'w_kv', 'b_w_q', 'b_w_out', 'f_w_up', 'f_w_down']
SMALL = [n for n in WEIGHTS if n not in BIG]
LANES = 1024


def _pad_rows(flat, mult):
    n = flat.shape[0]
    per = LANES * mult
    tot = -(-n // per) * per
    return jnp.pad(flat, (0, tot - n)).reshape(tot // LANES, LANES)


def _full_from_shards(sh, axis):
    return jnp.concatenate([sh[j] for j in range(4)], axis=axis)


def _shards_from_full(full, axis):
    return jnp.stack(jnp.split(full, 4, axis=axis))


def _local_step(x, target, W):
    T = x.shape[0]
    row = lambda a: a.reshape(1, -1).astype(F32)
    w_in = jnp.pad(W['a_w_in'][0], ((0, 0), (0, A_IN_PAD - A_IN)))
    bias128 = jnp.pad(row(W['a_b_if'][0]), ((0, 0), (0, 120)))
    hng = row(W['a_hnorm_g'][0])
    w_up = [_interleave(W['f_w_up'][l]) for l in range(2)]
    cw = [_interleave(W['f_conv_w'][l].astype(F32)) for l in range(2)]
    cb = [_interleave(row(W['f_conv_b'][l])) for l in range(2)]
    onehots = [(jnp.asarray(_group_bucket(g).reshape(-1, 1)) == jnp.arange(128)[None, :]).astype(F32)
               for g in range(N_GROUPS)]
    rb_t = jnp.pad(W['rel_bias'].astype(F32).T, ((0, 0), (0, 128 - REL_BUCKETS)))
    biases = [mm_nn(rb_t[g * B_HEADS:(g + 1) * B_HEADS], onehots[g].T, f"rel_bias_table_g{g}", exact=True)
              .reshape(B_HEADS, B_BLOCK, 2 * B_BLOCK) for g in range(N_GROUPS)]
    G = {}

    def ffn_fwd(xin, l):
        xn, = rms_fwd(xin, [row(W['f_norm_g'][l])], f"ffn{l}_norm")
        u = mm_nn(xn, w_up[l], f"ffn{l}_up")
        act = conv_act_fwd(u, cw[l], cb[l], f"ffn{l}_act")
        return mm_nn(act, W['f_w_down'][l], f"ffn{l}_down", res=xin), (xn, u, act)

    def ffn_bwd(xin, saved, dout, l):
        xn, u, act = saved
        dact = mm_nn(dout, W['f_w_down'][l].T, f"ffn{l}_ddown")
        gd = mm_tn(act, dout, f"ffn{l}_gdown")
        du, gcw, gcb = conv_act_bwd(u, dact, cw[l], cb[l], f"ffn{l}_dact")
        dxn = mm_nn(du, w_up[l].T, f"ffn{l}_dup")
        gu = _deinterleave(mm_tn(xn, du, f"ffn{l}_gup"))
        dxin, (gn,) = rms_bwd(xin, dout, [(dxn, row(W['f_norm_g'][l]))], f"ffn{l}_dnorm")
        return dxin, gd, gu, _deinterleave(gcw), _deinterleave(gcb), gn

    xn_a, = rms_fwd(x, [row(W['a_norm_g'][0])], "a_norm")
    z = mm_nn(xn_a, w_in, "a_in")
    gcol, grow = gate_prep(z, bias128)
    hg, Cs, ns, ms = mlstm_fwd(z, gcol, grow, hng)
    x1 = mm_nn(hg, W['a_w_out'][0], "a_out", res=x)
    x2, ffn0 = ffn_fwd(x1, 0)
    xn_kv, xn_b = rms_fwd(x2, [row(W['kv_norm_g']), row(W['b_norm_g'][0])], "b_norms")
    kv = mm_nn(xn_kv, W['w_kv'], "kv_proj", out_dtype=MM_DTYPE)
    qd = mm_nn(xn_b, W['b_w_q'][0], "q_proj", out_dtype=MM_DTYPE)
    os_, lses = zip(*[attn_fwd(qd, kv, biases[g], g) for g in range(N_GROUPS)])
    att, att_f, lse = attn_merge(os_, lses)
    x3 = mm_nn(att, W['b_w_out'][0], "b_out", res=x2)
    x4, ffn1 = ffn_fwd(x3, 1)
    dx4, g_final, loss = loss_head(x4, target, row(W['final_norm_g']))
    G['final_norm_g'] = g_final.reshape(-1)

    dx3, gd1, gu1, gcw1, gcb1, gn1 = ffn_bwd(x3, ffn1, dx4, 1)
    datt = mm_nn(dx3, W['b_w_out'][0].T, "b_dout")
    G['b_w_out'] = mm_tn(att, dx3, "b_gout")[None]
    delta = attn_delta(datt, att_f)
    parts = [attn_bwd(qd, kv, biases[g], datt, lse, delta, g) for g in range(N_GROUPS)]
    dq_all, dkv = attn_combine([p[0] for p in parts], [p[1] for p in parts], [p[2] for p in parts],
                               [p[3] for p in parts], [p[4] for p in parts])
    grb = []
    for g in range(N_GROUPS):
        gb = mm_nn(parts[g][5].reshape(B_HEADS, -1), onehots[g], f"rel_bias_g{g}", exact=True)
        grb.append(gb[:, :REL_BUCKETS].T)
    G['rel_bias'] = jnp.concatenate(grb, axis=1)
    dxn_b = mm_nn(dq_all, W['b_w_q'][0].T, "q_dproj")
    G['b_w_q'] = mm_tn(xn_b, dq_all, "q_gproj")[None]
    dxn_kv = mm_nn(dkv, W['w_kv'].T, "kv_dproj")
    G['w_kv'] = mm_tn(xn_kv, dkv, "kv_gproj")
    dx2, (g_kvn, g_bn) = rms_bwd(x2, dx3, [(dxn_kv, row(W['kv_norm_g'])), (dxn_b, row(W['b_norm_g'][0]))],
                                 "b_dnorms")
    G['kv_norm_g'] = g_kvn.reshape(-1)
    G['b_norm_g'] = g_bn
    dx1, gd0, gu0, gcw0, gcb0, gn0 = ffn_bwd(x1, ffn0, dx2, 0)
    G['f_w_down'] = jnp.stack([gd0, gd1])
    G['f_w_up'] = jnp.stack([gu0, gu1])
    G['f_conv_w'] = jnp.stack([gcw0, gcw1])
    G['f_conv_b'] = jnp.concatenate([gcb0, gcb1], axis=0)
    G['f_norm_g'] = jnp.concatenate([gn0, gn1], axis=0)
    dhg = mm_nn(dx1, W['a_w_out'][0].T, "a_dout")
    G['a_w_out'] = mm_tn(hg, dx1, "a_gout")[None]
    dz, g_hn, g_bif = mlstm_bwd(z, gcol, grow, hng, bias128, Cs, ns, ms, dhg)
    G['a_hnorm_g'] = g_hn.reshape(1, A_HEADS, A_V)
    G['a_b_if'] = g_bif[:, :2 * A_HEADS]
    dxn_a = mm_nn(dz, w_in.T, "a_din")
    G['a_w_in'] = mm_tn(xn_a, dz, "a_gin")[:, :A_IN][None]
    grad_x, (g_an,) = rms_bwd(x, dx1, [(dxn_a, row(W['a_norm_g'][0]))], "a_dnorm")
    G['a_norm_g'] = g_an
    return loss, grad_x, G


def kernel(x, a_norm_g, a_w_in, a_b_if, a_hnorm_g, a_w_out, kv_norm_g, w_kv, b_norm_g, b_w_q, b_w_out, rel_bias, f_norm_g, f_w_up, f_conv_w, f_conv_b, f_w_down, final_norm_g, loss_target, m_a_norm_g, m_a_w_in, m_a_b_if, m_a_hnorm_g, m_a_w_out, m_kv_norm_g, m_w_kv, m_b_norm_g, m_b_w_q, m_b_w_out, m_rel_bias, m_f_norm_g, m_f_w_up, m_f_conv_w, m_f_conv_b, m_f_w_down, m_final_norm_g, v_a_norm_g, v_a_w_in, v_a_b_if, v_a_hnorm_g, v_a_w_out, v_kv_norm_g, v_w_kv, v_b_norm_g, v_b_w_q, v_b_w_out, v_rel_bias, v_f_norm_g, v_f_w_up, v_f_conv_w, v_f_conv_b, v_f_w_down, v_final_norm_g):
    given = dict(locals())
    shard = {n: given[n] for n in WEIGHTS}
    mom = {n: given["m_" + n] for n in WEIGHTS}
    var = {n: given["v_" + n] for n in WEIGHTS}
    cx, cy, cc = _me()
    chip = 2 * cx + cy

    halves = [shard[n].astype(MM_DTYPE).reshape(2, -1) for n in BIG]
    sizes = [h.shape[1] for h in halves]
    mine = lax.dynamic_index_in_dim(jnp.concatenate(halves, axis=1), cc, axis=0, keepdims=False)
    mine = _pad_rows(mine, 16)
    rows = mine.shape[0]
    by_chip = group_gather(mine, "gather_weights_chips", GROUP_CHIPS)
    gathered = group_gather(by_chip.reshape(4 * rows, LANES), "gather_weights_sibling",
                            GROUP_SIBLING).reshape(2, 4, -1)
    W = {}
    off = 0
    for n, sz in zip(BIG, sizes):
        sh = jnp.swapaxes(gathered[:, :, off:off + sz], 0, 1).reshape((4,) + shard[n].shape)
        W[n] = _full_from_shards(sh, SHARD_AXIS[n])
        off += sz
    sharded_small = [n for n in SMALL if SHARD_AXIS[n] is not None]
    ssz = [shard[n].size for n in sharded_small]
    sflat = jnp.concatenate([shard[n].reshape(-1) for n in sharded_small])
    sg = group_gather(_pad_rows(sflat, 8), "gather_small", GROUP_CHIPS).reshape(4, -1)
    off = 0
    for n, sz in zip(sharded_small, ssz):
        W[n] = _full_from_shards(sg[:, off:off + sz].reshape((4,) + shard[n].shape), SHARD_AXIS[n])
        off += sz
    for n in SMALL:
        if SHARD_AXIS[n] is None:
            W[n] = shard[n]

    loss_row, grad_x, G = _local_step(x[0], loss_target[0], W)

    slots = jnp.concatenate([_shards_from_full(G[n], SHARD_AXIS[n]).reshape(8, -1) for n in BIG], axis=1)
    per = slots.shape[1]
    slots = jax.vmap(lambda f: _pad_rows(f, 16))(slots).astype(GRAD_WIRE_DTYPE)
    rows = slots.shape[1]
    by_half = jnp.swapaxes(slots.reshape(4, 2, rows, LANES), 0, 1).reshape(2, 4 * rows, LANES)
    keep = lax.dynamic_index_in_dim(by_half, cc, axis=0, keepdims=False)
    give = lax.dynamic_index_in_dim(by_half, 1 - cc, axis=0, keepdims=False)
    got = lax.dynamic_index_in_dim(group_gather(give, "pair_grads", GROUP_SIBLING), 1 - cc, axis=0, keepdims=False)
    chip_sum = add_pair(keep, got, "sum_pair_grads", GRAD_WIRE_DTYPE).reshape(4, rows, LANES)
    reduced = sum_slots(group_scatter(chip_sum, "scatter_grads", GROUP_CHIPS), "sum_grads")
    both = group_gather(reduced, "join_halves", GROUP_SIBLING).reshape(2, -1)[:, :per]
    gsh = {}
    off = 0
    for n, sz in zip(BIG, sizes):
        gsh[n] = both[:, off:off + sz].reshape(shard[n].shape)
        off += sz
    small_parts = [loss_row[0, 0:1]] + [G[n].reshape(-1) for n in SMALL]
    small_sz = [p.shape[0] for p in small_parts]
    small = sum_slots(group_gather(_pad_rows(jnp.concatenate(small_parts), 8), "gather_small_grads", GROUP_ALL),
                      "sum_small_grads").reshape(-1)
    loss = small[0]
    off = 1
    for n, sz in zip(SMALL, small_sz[1:]):
        full = small[off:off + sz].reshape(W[n].shape)
        off += sz
        if SHARD_AXIS[n] is None:
            gsh[n] = full
        else:
            gsh[n] = lax.dynamic_index_in_dim(_shards_from_full(full, SHARD_AXIS[n]), chip, 0, keepdims=False)

    delta, new_m, new_v = {}, {}, {}
    for n in WEIGHTS:
        shp = shard[n].shape
        two = lambda a: a.reshape(-1, shp[-1])
        d, nm, nv = adamw(two(shard[n]), two(gsh[n]), two(mom[n]), two(var[n]), f"adamw_{n}")
        delta[n], new_m[n], new_v[n] = d.reshape(shp), nm.reshape(shp), nv.reshape(shp)
    return (loss, grad_x[None], *[gsh[n] for n in WEIGHTS], *[delta[n] for n in WEIGHTS],
            *[new_m[n] for n in WEIGHTS], *[new_v[n] for n in WEIGHTS])
```

```python
import functools
import math

import numpy as np
import jax
import jax.numpy as jnp
from jax import lax
from jax.experimental import pallas as pl
from jax.experimental.pallas import tpu as pltpu

F32 = jnp.float32
BF16 = jnp.bfloat16
MM_DTYPE = jnp.bfloat16
GRAD_WIRE_DTYPE = jnp.bfloat16
HI = lax.Precision.HIGHEST

D_MODEL = 1024
A_HEADS = 4
A_QK = 128
A_V = 256
A_CHUNK = 64
A_IN = 3080
A_IN_PAD = 3200
GATE_COL = 3072
SOFTCAP = 15.0
N_GROUPS = 3
B_HEADS = 16
B_DH = 64
B_BLOCK = 128
DILATIONS = (1, 4, 16)
WINDOWS = (128, 512, 2048)
REL_BUCKETS = 32
REL_MAX_DIST = 2048
D_FF = 2816
FF_TC = 256
EPS = 1e-6
ADAM_LR, ADAM_B1, ADAM_B2, ADAM_EPS, ADAM_WD, ADAM_STEP = 0.001, 0.9, 0.999, 1e-08, 0.01, 10

VMEM_LIMIT = 56 * 1024 * 1024
NT_DIMS = (((1,), (1,)), ((), ()))
TN_DIMS = (((0,), (0,)), ((), ()))
MESH_ID = pl.DeviceIdType.MESH


def _params(*sem):
    return pltpu.CompilerParams(dimension_semantics=sem, vmem_limit_bytes=VMEM_LIMIT)


def _tile(n, cap):
    if n <= cap:
        return n
    best = None
    for t in range(128, cap + 1, 128):
        if n % t == 0:
            best = t
    assert best is not None, (n, cap)
    return best


def _rows(n, cap):
    if n <= cap:
        return n
    for t in range(cap // 8 * 8, 7, -8):
        if n % t == 0:
            return t
    raise ValueError((n, cap))


def _dot(a, b):
    return jnp.dot(a.astype(MM_DTYPE), b.astype(MM_DTYPE), preferred_element_type=F32)


def _dot_nt(a, b):
    return lax.dot_general(a.astype(MM_DTYPE), b.astype(MM_DTYPE), NT_DIMS, preferred_element_type=F32)


def _dot_tn(a, b):
    return lax.dot_general(a.astype(MM_DTYPE), b.astype(MM_DTYPE), TN_DIMS, preferred_element_type=F32)


def _sigmoid(x):
    return 1.0 / (1.0 + jnp.exp(-x))


def _sigmoid_tanh(x):
    return 0.5 * jnp.tanh(0.5 * x) + 0.5


def mm_nn(a, b, name, res=None, out_dtype=F32, exact=False):
    M, K = a.shape
    N = b.shape[1]
    tm, tn, tk = _rows(M, 512), _tile(N, 1536), _tile(K, 1536)
    nk = K // tk

    def body(*refs):
        if res is None:
            a_ref, b_ref, o_ref, acc = refs
            r_ref = None
        else:
            a_ref, b_ref, r_ref, o_ref, acc = refs
        if exact:
            p = jnp.dot(a_ref[...], b_ref[...], precision=HI, preferred_element_type=F32)
        else:
            p = _dot(a_ref[...], b_ref[...])

        def finish(total):
            if r_ref is not None:
                total = total + r_ref[...]
            o_ref[...] = total.astype(out_dtype)

        if nk == 1:
            finish(p)
        else:
            k = pl.program_id(2)

            @pl.when(k == 0)
            def _():
                acc[...] = p

            @pl.when(jnp.logical_and(k > 0, k < nk - 1))
            def _():
                acc[...] += p

            @pl.when(k == nk - 1)
            def _():
                finish(acc[...] + p)

    in_specs = [pl.BlockSpec((tm, tk), lambda i, j, k: (i, k)),
                pl.BlockSpec((tk, tn), lambda i, j, k: (k, j))]
    args = [a, b]
    if res is not None:
        in_specs.append(pl.BlockSpec((tm, tn), lambda i, j, k: (i, j)))
        args.append(res)
    acc_shape = (tm, tn) if nk > 1 else (8, 128)
    return pl.pallas_call(
        body, name=name, grid=(M // tm, N // tn, nk),
        in_specs=in_specs, out_specs=pl.BlockSpec((tm, tn), lambda i, j, k: (i, j)),
        out_shape=jax.ShapeDtypeStruct((M, N), out_dtype),
        scratch_shapes=[pltpu.VMEM(acc_shape, F32)],
        compiler_params=_params("parallel", "parallel", "arbitrary"),
    )(*args)


def mm_tn(a, g, name):
    T, Ka = a.shape
    N = g.shape[1]
    tka, tn, tt = _tile(Ka, 1536), _tile(N, 1536), _rows(T, 512)
    nt = T // tt

    def body(a_ref, g_ref, o_ref):
        t = pl.program_id(2)
        p = _dot_tn(a_ref[...], g_ref[...])

        @pl.when(t == 0)
        def _():
            o_ref[...] = p

        @pl.when(t > 0)
        def _():
            o_ref[...] += p

    return pl.pallas_call(
        body, name=name, grid=(Ka // tka, N // tn, nt),
        in_specs=[pl.BlockSpec((tt, tka), lambda i, j, t: (t, i)),
                  pl.BlockSpec((tt, tn), lambda i, j, t: (t, j))],
        out_specs=pl.BlockSpec((tka, tn), lambda i, j, t: (i, j)),
        out_shape=jax.ShapeDtypeStruct((Ka, N), F32),
        compiler_params=_params("parallel", "parallel", "arbitrary"),
    )(a, g)


def rms_fwd(x, gains, name):
    T, D = x.shape
    tt = _rows(T, 512)
    ng = len(gains)

    def body(*refs):
        x_ref = refs[0]
        g_refs = refs[1:1 + ng]
        o_refs = refs[1 + ng:]
        xf = x_ref[...]
        y = xf * lax.rsqrt(jnp.mean(xf * xf, axis=-1, keepdims=True) + EPS)
        for g_ref, o_ref in zip(g_refs, o_refs):
            o_ref[...] = (y * g_ref[...]).astype(o_ref.dtype)

    row = pl.BlockSpec((tt, D), lambda i: (i, 0))
    gsp = pl.BlockSpec((1, D), lambda i: (0, 0))
    return pl.pallas_call(
        body, name=name, grid=(T // tt,),
        in_specs=[row] + [gsp] * ng, out_specs=[row] * ng,
        out_shape=[jax.ShapeDtypeStruct((T, D), MM_DTYPE)] * ng,
        compiler_params=_params("parallel"),
    )(x, *gains)


def rms_bwd(x, dres, branches, name):
    T, D = x.shape
    tt = _rows(T, 256)
    nb = len(branches)

    def body(*refs):
        x_ref, r_ref = refs[0], refs[1]
        dy_refs = refs[2:2 + nb]
        g_refs = refs[2 + nb:2 + 2 * nb]
        dx_ref = refs[2 + 2 * nb]
        dg_refs = refs[3 + 2 * nb:]
        i = pl.program_id(0)
        xf = x_ref[...]
        r = lax.rsqrt(jnp.mean(xf * xf, axis=-1, keepdims=True) + EPS)
        xh = xf * r
        dx = r_ref[...]
        for dy_ref, g_ref, dg_ref in zip(dy_refs, g_refs, dg_refs):
            dy = dy_ref[...].astype(F32)
            dyg = dy * g_ref[...]
            dx = dx + r * (dyg - xh * jnp.mean(dyg * xh, axis=-1, keepdims=True))
            part = jnp.sum(dy * xh, axis=0, keepdims=True)

            @pl.when(i == 0)
            def _():
                dg_ref[...] = part

            @pl.when(i > 0)
            def _():
                dg_ref[...] += part
        dx_ref[...] = dx

    row = pl.BlockSpec((tt, D), lambda i: (i, 0))
    gsp = pl.BlockSpec((1, D), lambda i: (0, 0))
    outs = pl.pallas_call(
        body, name=name, grid=(T // tt,),
        in_specs=[row, row] + [row] * nb + [gsp] * nb,
        out_specs=[row] + [gsp] * nb,
        out_shape=[jax.ShapeDtypeStruct((T, D), F32)] + [jax.ShapeDtypeStruct((1, D), F32)] * nb,
        compiler_params=_params("arbitrary"),
    )(x, dres, *[b[0] for b in branches], *[b[1] for b in branches])
    return outs[0], outs[1:]


def loss_head(x, target, gain):
    T, D = x.shape
    tt = _rows(T, 256)

    def body(x_ref, t_ref, g_ref, dx_ref, dg_ref, loss_ref):
        i = pl.program_id(0)
        xf = x_ref[...]
        g = g_ref[...]
        r = lax.rsqrt(jnp.mean(xf * xf, axis=-1, keepdims=True) + EPS)
        xh = xf * r
        e = xh * g - t_ref[...]
        lpart = 0.5 * jnp.sum(jnp.sum(e * e, axis=1, keepdims=True), axis=0, keepdims=True) / D
        dy = e / D
        dyg = dy * g
        dx_ref[...] = r * (dyg - xh * jnp.mean(dyg * xh, axis=-1, keepdims=True))
        gpart = jnp.sum(dy * xh, axis=0, keepdims=True)
        lrow = jnp.broadcast_to(lpart, (1, 128))

        @pl.when(i == 0)
        def _():
            dg_ref[...] = gpart
            loss_ref[...] = lrow

        @pl.when(i > 0)
        def _():
            dg_ref[...] += gpart
            loss_ref[...] += lrow

    row = pl.BlockSpec((tt, D), lambda i: (i, 0))
    gsp = pl.BlockSpec((1, D), lambda i: (0, 0))
    return pl.pallas_call(
        body, name="loss_head", grid=(T // tt,),
        in_specs=[row, row, gsp],
        out_specs=[row, gsp, pl.BlockSpec((1, 128), lambda i: (0, 0))],
        out_shape=[jax.ShapeDtypeStruct((T, D), F32), jax.ShapeDtypeStruct((1, D), F32),
                   jax.ShapeDtypeStruct((1, 128), F32)],
        compiler_params=_params("arbitrary"),
    )(x, target, gain)


def _shift_down(u, prev8, first, k):
    rolled = pltpu.roll(u, k, 0)
    rid = lax.broadcasted_iota(jnp.int32, u.shape, 0)
    halo = jnp.where(first, 0.0, prev8)
    out = rolled
    for j in range(k):
        out = jnp.where(rid == j, halo[8 - k + j:8 - k + j + 1, :], out)
    return out


def _conv3(u, prev8, first, w, b):
    return (_shift_down(u, prev8, first, 2) * w[0:1, :] + _shift_down(u, prev8, first, 1) * w[1:2, :]
            + u * w[2:3, :] + b)


def conv_act_fwd(u, w, b, name):
    T = u.shape[0]
    tt = _rows(T, 512)
    nj = D_FF // FF_TC

    def body(u_ref, p_ref, w_ref, b_ref, o_ref):
        first = pl.program_id(1) == 0
        c = _conv3(u_ref[...], p_ref[...], first, w_ref[...], b_ref[...])
        cg, cv = c[:, :FF_TC], c[:, FF_TC:]
        o_ref[...] = (cg * _sigmoid_tanh(cg) * cv).astype(o_ref.dtype)

    return pl.pallas_call(
        body, name=name, grid=(nj, T // tt),
        in_specs=[pl.BlockSpec((tt, 2 * FF_TC), lambda j, i: (i, j)),
                  pl.BlockSpec((8, 2 * FF_TC), lambda j, i: (jnp.maximum(i * (tt // 8) - 1, 0), j)),
                  pl.BlockSpec((3, 2 * FF_TC), lambda j, i: (0, j)),
                  pl.BlockSpec((1, 2 * FF_TC), lambda j, i: (0, j))],
        out_specs=pl.BlockSpec((tt, FF_TC), lambda j, i: (i, j)),
        out_shape=jax.ShapeDtypeStruct((T, D_FF), MM_DTYPE),
        compiler_params=_params("parallel", "parallel"),
    )(u, u, w, b)


def conv_act_bwd(u, da, w, b, name):
    T = u.shape[0]
    tt = _rows(T, 512)
    nt = T // tt
    nj = D_FF // FF_TC
    te = tt + 8

    def body(u_ref, p_ref, n_ref, da_ref, dan_ref, w_ref, b_ref, du_ref, dw_ref, db_ref):
        i = pl.program_id(1)
        first = i == 0
        last = i == nt - 1
        w = w_ref[...]
        ue = jnp.concatenate([u_ref[...], n_ref[...]], axis=0)
        dae = jnp.concatenate([da_ref[...], jnp.where(last, 0.0, dan_ref[...])], axis=0)
        um2 = _shift_down(ue, p_ref[...], first, 2)
        um1 = _shift_down(ue, p_ref[...], first, 1)
        c = um2 * w[0:1, :] + um1 * w[1:2, :] + ue * w[2:3, :] + b_ref[...]
        cg, cv = c[:, :FF_TC], c[:, FF_TC:]
        s = _sigmoid_tanh(cg)
        dcg = dae * cv * (s * (1.0 + cg * (1.0 - s)))
        dcv = dae * (cg * s)
        dc = jnp.concatenate([dcg, dcv], axis=1)
        du = (dc * w[2:3, :] + pltpu.roll(dc, te - 1, 0) * w[1:2, :] + pltpu.roll(dc, te - 2, 0) * w[0:1, :])
        du_ref[...] = du[:tt, :].astype(du_ref.dtype)
        dcm = dc[:tt, :]
        dwp = jnp.concatenate([jnp.sum(dcm * um2[:tt, :], axis=0, keepdims=True),
                               jnp.sum(dcm * um1[:tt, :], axis=0, keepdims=True),
                               jnp.sum(dcm * ue[:tt, :], axis=0, keepdims=True)], axis=0)
        dbp = jnp.sum(dcm, axis=0, keepdims=True)

        @pl.when(first)
        def _():
            dw_ref[...] = dwp
            db_ref[...] = dbp

        @pl.when(i > 0)
        def _():
            dw_ref[...] += dwp
            db_ref[...] += dbp

    nb8 = T // 8
    return pl.pallas_call(
        body, name=name, grid=(nj, nt),
        in_specs=[pl.BlockSpec((tt, 2 * FF_TC), lambda j, i: (i, j)),
                  pl.BlockSpec((8, 2 * FF_TC), lambda j, i: (jnp.maximum(i * (tt // 8) - 1, 0), j)),
                  pl.BlockSpec((8, 2 * FF_TC), lambda j, i: (jnp.minimum((i + 1) * (tt // 8), nb8 - 1), j)),
                  pl.BlockSpec((tt, FF_TC), lambda j, i: (i, j)),
                  pl.BlockSpec((8, FF_TC), lambda j, i: (jnp.minimum((i + 1) * (tt // 8), nb8 - 1), j)),
                  pl.BlockSpec((3, 2 * FF_TC), lambda j, i: (0, j)),
                  pl.BlockSpec((1, 2 * FF_TC), lambda j, i: (0, j))],
        out_specs=[pl.BlockSpec((tt, 2 * FF_TC), lambda j, i: (i, j)),
                   pl.BlockSpec((3, 2 * FF_TC), lambda j, i: (0, j)),
                   pl.BlockSpec((1, 2 * FF_TC), lambda j, i: (0, j))],
        out_shape=[jax.ShapeDtypeStruct((T, 2 * D_FF), MM_DTYPE),
                   jax.ShapeDtypeStruct((3, 2 * D_FF), F32),
                   jax.ShapeDtypeStruct((1, 2 * D_FF), F32)],
        compiler_params=_params("parallel", "arbitrary"),
    )(u, u, u, da, da, w, b)


def _interleave(a):
    lead = a.shape[:-1]
    nj = D_FF // FF_TC
    return jnp.swapaxes(a.reshape(*lead, 2, nj, FF_TC), -3, -2).reshape(*lead, 2 * D_FF)


def _deinterleave(a):
    lead = a.shape[:-1]
    nj = D_FF // FF_TC
    return jnp.swapaxes(a.reshape(*lead, nj, 2, FF_TC), -3, -2).reshape(*lead, 2 * D_FF)


A_GC = 2
A_TB = A_GC * A_CHUNK


def gate_prep(z, bias128):
    T = z.shape[0]
    tt = _rows(T, 512)

    def body(z_ref, b_ref, gc_ref, gr_ref):
        pre = z_ref[...] + b_ref[...]
        sc = SOFTCAP * jnp.tanh(pre / SOFTCAP)
        lf = jnp.minimum(sc, 0.0) - jnp.log(1.0 + jnp.exp(-jnp.abs(sc)))
        col = lax.broadcasted_iota(jnp.int32, pre.shape, 1)
        isf = jnp.logical_and(col >= A_HEADS, col < 2 * A_HEADS)
        r = lax.broadcasted_iota(jnp.int32, (tt, tt), 0)
        c = lax.broadcasted_iota(jnp.int32, (tt, tt), 1)
        tri = jnp.logical_and(jnp.right_shift(r, 6) == jnp.right_shift(c, 6), c <= r).astype(F32)
        bcum = jnp.dot(tri, jnp.where(isf, lf, 0.0), precision=HI, preferred_element_type=F32)
        g = jnp.where(col < A_HEADS, sc, jnp.where(isf, bcum, 0.0))
        gc_ref[...] = g
        for s in range(tt // 128):
            gr_ref[s] = g[s * 128:(s + 1) * 128, :].T[0:8, :]

    return pl.pallas_call(
        body, name="gate_prep", grid=(T // tt,),
        in_specs=[pl.BlockSpec((tt, 128), lambda i: (i, GATE_COL // 128)),
                  pl.BlockSpec((1, 128), lambda i: (0, 0))],
        out_specs=[pl.BlockSpec((tt, 128), lambda i: (i, 0)),
                   pl.BlockSpec((tt // 128, 8, 128), lambda i: (i, 0, 0))],
        out_shape=[jax.ShapeDtypeStruct((T, 128), F32), jax.ShapeDtypeStruct((T // 128, 8, 128), F32)],
        compiler_params=_params("parallel"),
    )(z, bias128)


def _chunk_fwd(qh, kh, vh, bc, br, lir, C, n, m, causal):
    A = _dot_nt(qh, kh)
    logD = jnp.where(causal, bc - br + lir, -jnp.inf)
    m_inter = bc + m
    m_t = jnp.maximum(m_inter, jnp.max(logD, axis=1, keepdims=True))
    E = jnp.exp(logD - m_t)
    Sm = A * E
    wi = jnp.exp(m_inter - m_t)
    num = _dot(Sm, vh) + wi * _dot(qh, C)
    qn = jnp.sum(qh.astype(F32) * n, axis=1, keepdims=True)
    den = jnp.sum(Sm, axis=1, keepdims=True) + wi * qn
    gs = jnp.maximum(jnp.abs(den), jnp.exp(-m_t))
    return E, Sm, wi, num, den, gs, m_t


def _state_weights(bc, lic, br, lir, m):
    bL = bc[A_CHUNK - 1:A_CHUNK, :]
    m_new = jnp.maximum(bL + m, jnp.max(bL - br + lir, axis=1, keepdims=True))
    wk = jnp.exp(bL - bc + lic - m_new)
    decay = jnp.exp(bL + m - m_new)
    return wk, decay, m_new


def _head_slices(h):
    return (slice(h * A_QK, (h + 1) * A_QK), slice(h * A_V, (h + 1) * A_V))


def mlstm_fwd(z, gcol, grow, hng):
    T = z.shape[0]
    NC = T // A_CHUNK
    scale = A_QK ** -0.5

    def body(q_ref, k_ref, v_ref, o_ref, gc_ref, gr_ref, hng_ref, hg_ref, Cs_ref, ns_ref, ms_ref,
             C_sc, n_sc, m_sc):
        @pl.when(pl.program_id(0) == 0)
        def _():
            C_sc[...] = jnp.zeros_like(C_sc)
            n_sc[...] = jnp.zeros_like(n_sc)
            m_sc[...] = jnp.zeros_like(m_sc)

        ri = lax.broadcasted_iota(jnp.int32, (A_CHUNK, A_CHUNK), 0)
        ci = lax.broadcasted_iota(jnp.int32, (A_CHUNK, A_CHUNK), 1)
        causal = ri >= ci
        gr = gr_ref[0]
        for c in range(A_GC):
            rows = slice(c * A_CHUNK, (c + 1) * A_CHUNK)
            gc = gc_ref[rows, :]
            grc = gr[:, c * A_CHUNK:(c + 1) * A_CHUNK]
            for h in range(A_HEADS):
                sk, sv = _head_slices(h)
                qh = (q_ref[rows, sk] * scale).astype(MM_DTYPE)
                kh = k_ref[rows, sk].astype(MM_DTYPE)
                vh = v_ref[rows, sv].astype(MM_DTYPE)
                lic, bc = gc[:, h:h + 1], gc[:, A_HEADS + h:A_HEADS + h + 1]
                lir, br = grc[h:h + 1, :], grc[A_HEADS + h:A_HEADS + h + 1, :]
                C, n, m = C_sc[h], n_sc[h], m_sc[h][:, 0:1]
                Cs_ref[c, h] = C
                ns_ref[c, h] = n
                ms_ref[c, h] = m_sc[h]
                _, _, _, num, _, gs, _ = _chunk_fwd(qh, kh, vh, bc, br, lir, C, n, m, causal)
                hh = num / gs
                hn = hh * lax.rsqrt(jnp.mean(hh * hh, axis=1, keepdims=True) + EPS) * hng_ref[:, sv]
                hg_ref[rows, sv] = (hn * _sigmoid(o_ref[rows, sv])).astype(hg_ref.dtype)
                wk, decay, m_new = _state_weights(bc, lic, br, lir, m)
                kw = kh.astype(F32) * wk
                C_sc[h] = decay * C + _dot_tn(kw, vh)
                n_sc[h] = decay * n + jnp.sum(kw, axis=0, keepdims=True)
                m_sc[h] = jnp.broadcast_to(m_new, (1, 128))

    tok = lambda w, cb: pl.BlockSpec((A_TB, w), lambda i: (i, cb))
    return pl.pallas_call(
        body, name="mlstm_fwd", grid=(NC // A_GC,),
        in_specs=[tok(512, 0), tok(512, 1), tok(1024, 1), tok(1024, 2),
                  pl.BlockSpec((A_TB, 128), lambda i: (i, 0)),
                  pl.BlockSpec((1, 8, 128), lambda i: (i, 0, 0)),
                  pl.BlockSpec((1, 1024), lambda i: (0, 0))],
        out_specs=[pl.BlockSpec((A_TB, 1024), lambda i: (i, 0)),
                   pl.BlockSpec((A_GC, A_HEADS, A_QK, A_V), lambda i: (i, 0, 0, 0)),
                   pl.BlockSpec((A_GC, A_HEADS, 1, 128), lambda i: (i, 0, 0, 0)),
                   pl.BlockSpec((A_GC, A_HEADS, 1, 128), lambda i: (i, 0, 0, 0))],
        out_shape=[jax.ShapeDtypeStruct((T, 1024), MM_DTYPE),
                   jax.ShapeDtypeStruct((NC, A_HEADS, A_QK, A_V), F32),
                   jax.ShapeDtypeStruct((NC, A_HEADS, 1, 128), F32),
                   jax.ShapeDtypeStruct((NC, A_HEADS, 1, 128), F32)],
        scratch_shapes=[pltpu.VMEM((A_HEADS, A_QK, A_V), F32), pltpu.VMEM((A_HEADS, 1, 128), F32),
                        pltpu.VMEM((A_HEADS, 1, 128), F32)],
        compiler_params=_params("arbitrary"),
    )(z, z, z, z, gcol, grow, hng)


def mlstm_bwd(z, gcol, grow, hng, bias128, Cs, ns, ms, dhg):
    T = z.shape[0]
    NC = T // A_CHUNK
    nsteps = NC // A_GC
    scale = A_QK ** -0.5

    def body(q_ref, k_ref, v_ref, o_ref, zg_ref, gc_ref, gr_ref, hng_ref, b_ref, Cs_ref, ns_ref, ms_ref,
             dhg_ref, dz_ref, dgn_ref, dbif_ref, dC_sc, dn_sc):
        @pl.when(pl.program_id(0) == 0)
        def _():
            dC_sc[...] = jnp.zeros_like(dC_sc)
            dn_sc[...] = jnp.zeros_like(dn_sc)
            dgn_ref[...] = jnp.zeros_like(dgn_ref)
            dbif_ref[...] = jnp.zeros_like(dbif_ref)

        ri = lax.broadcasted_iota(jnp.int32, (A_CHUNK, A_CHUNK), 0)
        ci = lax.broadcasted_iota(jnp.int32, (A_CHUNK, A_CHUNK), 1)
        causal = ri >= ci
        upper = (ci >= ri).astype(F32)
        rid = lax.broadcasted_iota(jnp.int32, (A_CHUNK, 1), 0)
        col = lax.broadcasted_iota(jnp.int32, (A_CHUNK, 128), 1)
        gr = gr_ref[0]
        for c in reversed(range(A_GC)):
            rows = slice(c * A_CHUNK, (c + 1) * A_CHUNK)
            gc = gc_ref[rows, :]
            grc = gr[:, c * A_CHUNK:(c + 1) * A_CHUNK]
            dG = jnp.zeros((A_CHUNK, 128), F32)
            for h in range(A_HEADS):
                sk, sv = _head_slices(h)
                qh = (q_ref[rows, sk] * scale).astype(MM_DTYPE)
                kh = k_ref[rows, sk].astype(MM_DTYPE)
                vh = v_ref[rows, sv].astype(MM_DTYPE)
                qf, kf = qh.astype(F32), kh.astype(F32)
                lic, bc = gc[:, h:h + 1], gc[:, A_HEADS + h:A_HEADS + h + 1]
                lir, br = grc[h:h + 1, :], grc[A_HEADS + h:A_HEADS + h + 1, :]
                C, n, m = Cs_ref[c, h], ns_ref[c, h], ms_ref[c, h][:, 0:1]
                dC, dn = dC_sc[h], dn_sc[h]
                E, Sm, wi, num, den, gs, m_t = _chunk_fwd(qh, kh, vh, bc, br, lir, C, n, m, causal)
                wk, decay, _ = _state_weights(bc, lic, br, lir, m)
                hh = num / gs
                r = lax.rsqrt(jnp.mean(hh * hh, axis=1, keepdims=True) + EPS)
                gn = hng_ref[:, sv]
                o = o_ref[rows, sv]
                s = _sigmoid(o)
                dhg_h = dhg_ref[rows, sv]
                dhn = dhg_h * s
                dz_ref[rows, 2048 + h * A_V:2048 + (h + 1) * A_V] = dhg_h * (hh * r * gn) * s * (1.0 - s)
                dgn_ref[:, sv] += jnp.sum(dhn * hh * r, axis=0, keepdims=True)
                dyg = dhn * gn
                dh = r * dyg - hh * (r * r * r) * jnp.mean(dyg * hh, axis=1, keepdims=True)
                dnum = dh / gs
                live = (jnp.abs(den) > jnp.exp(-m_t)).astype(F32)
                dden = -jnp.sum(dh * hh, axis=1, keepdims=True) / gs * jnp.sign(den) * live
                dSE = jnp.where(causal, _dot_nt(dnum, vh) + dden, 0.0) * E
                dq = _dot(dSE, kh) + wi * (_dot_nt(dnum, C) + dden * n)
                dk_inter = wk * (_dot_nt(vh, dC) + dn)
                dk = _dot_tn(dSE, qh) + dk_inter
                dv = _dot_tn(Sm, dnum) + wk * _dot(kh, dC)
                dz_ref[rows, sk] = dq * scale
                dz_ref[rows, 512 + h * A_QK:512 + (h + 1) * A_QK] = dk
                dz_ref[rows, 1024 + h * A_V:1024 + (h + 1) * A_V] = dv
                dli = jnp.sum(kf * dk, axis=1, keepdims=True)
                db = jnp.sum(qf * dq, axis=1, keepdims=True) - dli
                usum = jnp.sum(jnp.sum(kf * dk_inter, axis=1, keepdims=True), axis=0, keepdims=True)
                ddecay = (jnp.sum(jnp.sum(dC * C, axis=1, keepdims=True), axis=0, keepdims=True)
                          + jnp.sum(dn * n, axis=1, keepdims=True))
                db = db + jnp.where(rid == A_CHUNK - 1, usum + ddecay * decay, 0.0)
                dG = dG + jnp.where(col == h, dli, 0.0) + jnp.where(col == A_HEADS + h, db, 0.0)
                dC_sc[h] = decay * dC + _dot_tn(qf * wi, dnum)
                dn_sc[h] = decay * dn + jnp.sum(qf * (wi * dden), axis=0, keepdims=True)
            dlf = jnp.dot(upper, dG, precision=HI, preferred_element_type=F32)
            pre = zg_ref[rows, :] + b_ref[...]
            th = jnp.tanh(pre / SOFTCAP)
            dcap = 1.0 - th * th
            dpre = jnp.where(col < A_HEADS, dG * dcap,
                             jnp.where(col < 2 * A_HEADS, dlf * _sigmoid(-SOFTCAP * th) * dcap, 0.0))
            dz_ref[rows, GATE_COL:GATE_COL + 128] = dpre
            dbif_ref[...] += jnp.sum(dpre, axis=0, keepdims=True)

    rev = lambda i: nsteps - 1 - i
    tok = lambda w, cb: pl.BlockSpec((A_TB, w), lambda i: (rev(i), cb))
    st = lambda a, b: pl.BlockSpec((A_GC, A_HEADS, a, b), lambda i: (rev(i), 0, 0, 0))
    return pl.pallas_call(
        body, name="mlstm_bwd", grid=(nsteps,),
        in_specs=[tok(512, 0), tok(512, 1), tok(1024, 1), tok(1024, 2), tok(128, GATE_COL // 128),
                  pl.BlockSpec((A_TB, 128), lambda i: (rev(i), 0)),
                  pl.BlockSpec((1, 8, 128), lambda i: (rev(i), 0, 0)),
                  pl.BlockSpec((1, 1024), lambda i: (0, 0)),
                  pl.BlockSpec((1, 128), lambda i: (0, 0)),
                  st(A_QK, A_V), st(1, 128), st(1, 128),
                  pl.BlockSpec((A_TB, 1024), lambda i: (rev(i), 0))],
        out_specs=[pl.BlockSpec((A_TB, A_IN_PAD), lambda i: (rev(i), 0)),
                   pl.BlockSpec((1, 1024), lambda i: (0, 0)),
                   pl.BlockSpec((1, 128), lambda i: (0, 0))],
        out_shape=[jax.ShapeDtypeStruct((T, A_IN_PAD), F32), jax.ShapeDtypeStruct((1, 1024), F32),
                   jax.ShapeDtypeStruct((1, 128), F32)],
        scratch_shapes=[pltpu.VMEM((A_HEADS, A_QK, A_V), F32), pltpu.VMEM((A_HEADS, 1, 128), F32)],
        compiler_params=_params("arbitrary"),
    )(z, z, z, z, z, gcol, grow, hng, bias128, Cs, ns, ms, dhg)


def _t5_bucket(dist):
    max_exact = REL_BUCKETS // 2
    d = np.maximum(dist, 0)
    log_ratio = np.log(np.maximum(d, 1) / max_exact) / math.log(REL_MAX_DIST / max_exact)
    large = np.minimum(max_exact + (log_ratio * (REL_BUCKETS - max_exact)).astype(np.int64), REL_BUCKETS - 1)
    return np.where(d < max_exact, d, large).astype(np.int32)


def _group_bucket(g):
    delta = B_BLOCK + np.arange(B_BLOCK)[:, None] - np.arange(2 * B_BLOCK)[None, :]
    return _t5_bucket(delta * DILATIONS[g])


def _band_mask(n):
    ri = lax.broadcasted_iota(jnp.int32, (B_BLOCK, 2 * B_BLOCK), 0)
    ci = lax.broadcasted_iota(jnp.int32, (B_BLOCK, 2 * B_BLOCK), 1)
    band = jnp.logical_and(ci >= ri, ci <= ri + B_BLOCK)
    return jnp.logical_and(band, jnp.logical_or(ci >= B_BLOCK, n > 0))


def _both(p_ref, c_ref, sl):
    return jnp.concatenate([p_ref[:, sl], c_ref[:, sl]], axis=0)


def _scores(qh, kh, bias_h, valid):
    return jnp.where(valid, _dot_nt(qh, kh) * (B_DH ** -0.5) + bias_h, -jnp.inf)


def _attn_specs(g, dil):
    qs = pl.BlockSpec((B_BLOCK, 1024), lambda r, n: (n, r * 3 + g))
    kc = pl.BlockSpec((B_BLOCK, 1024), lambda r, n: (n, r * 6 + g))
    kp = pl.BlockSpec((B_BLOCK, 1024), lambda r, n: (jnp.maximum(n - 1, 0), r * 6 + g))
    vc = pl.BlockSpec((B_BLOCK, 1024), lambda r, n: (n, r * 6 + 3 + g))
    vp = pl.BlockSpec((B_BLOCK, 1024), lambda r, n: (jnp.maximum(n - 1, 0), r * 6 + 3 + g))
    bias = pl.BlockSpec((B_HEADS, B_BLOCK, 2 * B_BLOCK), lambda r, n: (0, 0, 0))
    wide = pl.BlockSpec((B_BLOCK, 1024), lambda r, n: (n, r))
    narrow = pl.BlockSpec((B_BLOCK, 128), lambda r, n: (n, r))
    return qs, kp, kc, vp, vc, bias, wide, narrow


def attn_fwd(qd, kv, bias, g):
    T = qd.shape[0]
    dil = DILATIONS[g]
    Tv = T // dil
    nb = Tv // B_BLOCK
    qs, kp, kc, vp, vc, bsp, wide, narrow = _attn_specs(g, dil)

    def body(q_ref, kp_ref, kc_ref, vp_ref, vc_ref, b_ref, o_ref, lse_ref):
        valid = _band_mask(pl.program_id(1))
        lse_ref[...] = jnp.zeros_like(lse_ref)
        heads = [slice(h * B_DH, (h + 1) * B_DH) for h in range(B_HEADS)]
        S = [_scores(q_ref[:, sl], _both(kp_ref, kc_ref, sl), b_ref[h], valid) for h, sl in enumerate(heads)]
        P, L = [], []
        for h in range(B_HEADS):
            m = jnp.max(S[h], axis=1, keepdims=True)
            p = jnp.exp(S[h] - m)
            l = jnp.sum(p, axis=1, keepdims=True)
            lse_ref[:, h:h + 1] = m + jnp.log(l)
            P.append(p.astype(MM_DTYPE))
            L.append(l)
        for h, sl in enumerate(heads):
            o_ref[:, sl] = _dot(P[h], _both(vp_ref, vc_ref, sl)) / L[h]

    o, lse = pl.pallas_call(
        body, name=f"attn_fwd_g{g}", grid=(dil, nb),
        in_specs=[qs, kp, kc, vp, vc, bsp], out_specs=[wide, narrow],
        out_shape=[jax.ShapeDtypeStruct((Tv, dil * 1024), F32), jax.ShapeDtypeStruct((Tv, dil * 128), F32)],
        compiler_params=_params("parallel", "parallel"),
    )(qd.reshape(Tv, dil * 3072), *([kv.reshape(Tv, dil * 6144)] * 4), bias)
    return o.reshape(T, 1024), lse.reshape(T, 128)


def attn_bwd(qd, kv, bias, datt, lse, delta, g):
    T = qd.shape[0]
    dil = DILATIONS[g]
    Tv = T // dil
    nb = Tv // B_BLOCK
    qs, kp, kc, vp, vc, bsp, wide, narrow = _attn_specs(g, dil)

    def body(q_ref, kp_ref, kc_ref, vp_ref, vc_ref, b_ref, bt_ref, do_ref, lse_ref, dl_ref,
             dq_ref, dkc_ref, dkp_ref, dvc_ref, dvp_ref, db_ref):
        @pl.when(jnp.logical_and(pl.program_id(0) == 0, pl.program_id(1) == 0))
        def _():
            db_ref[...] = jnp.zeros_like(db_ref)

        n = pl.program_id(1)
        valid = _band_mask(n)
        ki = lax.broadcasted_iota(jnp.int32, (2 * B_BLOCK, B_BLOCK), 0)
        qi = lax.broadcasted_iota(jnp.int32, (2 * B_BLOCK, B_BLOCK), 1)
        valid_t = jnp.logical_and(jnp.logical_and(ki >= qi, ki <= qi + B_BLOCK), jnp.logical_or(ki >= B_BLOCK, n > 0))
        lse_t, dl_t = lse_ref[...].T, dl_ref[...].T
        heads = [slice(h * B_DH, (h + 1) * B_DH) for h in range(B_HEADS)]
        scale = B_DH ** -0.5
        PT, DS, DST = [], [], []
        for h, sl in enumerate(heads):
            qh, doh = q_ref[:, sl], do_ref[:, sl].astype(MM_DTYPE)
            kh, vh = _both(kp_ref, kc_ref, sl), _both(vp_ref, vc_ref, sl)
            p = jnp.exp(_scores(qh, kh, b_ref[h], valid) - lse_ref[:, h:h + 1])
            ds = p * (_dot_nt(doh, vh) - dl_ref[:, h:h + 1])
            db_ref[h] += ds
            DS.append((ds * scale).astype(MM_DTYPE))
            pt = jnp.exp(_scores(kh, qh, bt_ref[h], valid_t) - lse_t[h:h + 1, :])
            PT.append(pt.astype(MM_DTYPE))
            DST.append((pt * (_dot_nt(vh, doh) - dl_t[h:h + 1, :]) * scale).astype(MM_DTYPE))
        for h, sl in enumerate(heads):
            qh, doh = q_ref[:, sl], do_ref[:, sl].astype(MM_DTYPE)
            dq_ref[:, sl] = _dot(DS[h], _both(kp_ref, kc_ref, sl))
            dk = _dot(DST[h], qh)
            dv = _dot(PT[h], doh)
            dkp_ref[:, sl], dkc_ref[:, sl] = dk[:B_BLOCK], dk[B_BLOCK:]
            dvp_ref[:, sl], dvc_ref[:, sl] = dv[:B_BLOCK], dv[B_BLOCK:]

    big = jax.ShapeDtypeStruct((Tv, dil * 1024), F32)
    bsp_t = pl.BlockSpec((B_HEADS, 2 * B_BLOCK, B_BLOCK), lambda r, n: (0, 0, 0))
    outs = pl.pallas_call(
        body, name=f"attn_bwd_g{g}", grid=(dil, nb),
        in_specs=[qs, kp, kc, vp, vc, bsp, bsp_t, wide, narrow, narrow],
        out_specs=[wide] * 5 + [bsp],
        out_shape=[big] * 5 + [jax.ShapeDtypeStruct((B_HEADS, B_BLOCK, 2 * B_BLOCK), F32)],
        compiler_params=_params("arbitrary", "arbitrary"),
    )(qd.reshape(Tv, dil * 3072), *([kv.reshape(Tv, dil * 6144)] * 4), bias, jnp.swapaxes(bias, 1, 2),
      datt.reshape(Tv, dil * 1024), lse.reshape(Tv, dil * 128), delta.reshape(Tv, dil * 128))
    return [o.reshape(T, 1024) for o in outs[:5]] + [outs[5]]


def _head_expand():
    e = np.zeros((128, 1024), np.float32)
    for h in range(B_HEADS):
        e[h, h * B_DH:(h + 1) * B_DH] = 1.0
    return e


def attn_merge(os_, lses):
    T = os_[0].shape[0]
    tt = _rows(T, 256)
    expand = jnp.asarray(_head_expand())

    def body(o0, o1, o2, l0, l1, l2, e_ref, ob_ref, of_ref, lse_ref):
        ls = [l0[...], l1[...], l2[...]]
        m = jnp.maximum(jnp.maximum(ls[0], ls[1]), ls[2])
        ex = [jnp.exp(l - m) for l in ls]
        tot = ex[0] + ex[1] + ex[2]
        lse_ref[...] = m + jnp.log(tot)
        out = jnp.zeros((tt, 1024), F32)
        for e, o in zip(ex, (o0, o1, o2)):
            w = jnp.dot(e / tot, e_ref[...], precision=HI, preferred_element_type=F32)
            out = out + w * o[...]
        of_ref[...] = out
        ob_ref[...] = out.astype(ob_ref.dtype)

    wide = pl.BlockSpec((tt, 1024), lambda i: (i, 0))
    narrow = pl.BlockSpec((tt, 128), lambda i: (i, 0))
    return pl.pallas_call(
        body, name="attn_merge", grid=(T // tt,),
        in_specs=[wide] * 3 + [narrow] * 3 + [pl.BlockSpec((128, 1024), lambda i: (0, 0))],
        out_specs=[wide, wide, narrow],
        out_shape=[jax.ShapeDtypeStruct((T, 1024), MM_DTYPE), jax.ShapeDtypeStruct((T, 1024), F32),
                   jax.ShapeDtypeStruct((T, 128), F32)],
        compiler_params=_params("parallel"),
    )(*os_, *lses, expand)


def attn_delta(datt, out):
    T = datt.shape[0]
    tt = _rows(T, 512)
    expand_t = jnp.asarray(_head_expand().T.copy())

    def body(d_ref, o_ref, e_ref, dl_ref):
        dl_ref[...] = jnp.dot(d_ref[...] * o_ref[...], e_ref[...], precision=HI, preferred_element_type=F32)

    wide = pl.BlockSpec((tt, 1024), lambda i: (i, 0))
    return pl.pallas_call(
        body, name="attn_delta", grid=(T // tt,),
        in_specs=[wide, wide, pl.BlockSpec((1024, 128), lambda i: (0, 0))],
        out_specs=pl.BlockSpec((tt, 128), lambda i: (i, 0)),
        out_shape=jax.ShapeDtypeStruct((T, 128), F32),
        compiler_params=_params("parallel"),
    )(datt, out, expand_t)


def attn_combine(dqs, dkc, dkp, dvc, dvp):
    T = dqs[0].shape[0]
    tt = B_BLOCK
    nt = T // tt

    def body(*refs):
        i = pl.program_id(0)
        dq_refs, rest = refs[:3], refs[3:]
        c_refs, p_refs = rest[:6], rest[6:12]
        dq_ref, dkv_ref = rest[12], rest[13]
        for g in range(N_GROUPS):
            dq_ref[:, g * 1024:(g + 1) * 1024] = dq_refs[g][...].astype(dq_ref.dtype)
        for j in range(6):
            live = i + DILATIONS[j % 3] < nt
            tot = c_refs[j][...] + jnp.where(live, p_refs[j][...], 0.0)
            dkv_ref[:, j * 1024:(j + 1) * 1024] = tot.astype(dkv_ref.dtype)

    cur = pl.BlockSpec((tt, 1024), lambda i: (i, 0))

    def nxt(d):
        return pl.BlockSpec((tt, 1024), lambda i: (jnp.minimum(i + d, nt - 1), 0))

    return pl.pallas_call(
        body, name="attn_combine", grid=(nt,),
        in_specs=[cur] * 3 + [cur] * 6 + [nxt(DILATIONS[j % 3]) for j in range(6)],
        out_specs=[pl.BlockSpec((tt, 3072), lambda i: (i, 0)), pl.BlockSpec((tt, 6144), lambda i: (i, 0))],
        out_shape=[jax.ShapeDtypeStruct((T, 3072), MM_DTYPE), jax.ShapeDtypeStruct((T, 6144), MM_DTYPE)],
        compiler_params=_params("parallel"),
    )(*dqs, *dkc, *dvc, *dkp, *dvp)


def adamw(w, g, m, v, name):
    R, C = w.shape
    tr = R if R * C * 4 <= (1 << 20) else _rows(R, max(8, ((1 << 20) // (C * 4)) // 8 * 8))

    def body(w_ref, g_ref, m_ref, v_ref, d_ref, nm_ref, nv_ref):
        gg = g_ref[...]
        nm = ADAM_B1 * m_ref[...] + (1.0 - ADAM_B1) * gg
        nv = ADAM_B2 * v_ref[...] + (1.0 - ADAM_B2) * (gg * gg)
        m_hat = nm / (1.0 - ADAM_B1 ** ADAM_STEP)
        v_hat = nv / (1.0 - ADAM_B2 ** ADAM_STEP)
        d_ref[...] = -ADAM_LR * (m_hat / (jnp.sqrt(v_hat) + ADAM_EPS) + ADAM_WD * w_ref[...])
        nm_ref[...] = nm
        nv_ref[...] = nv

    blk = pl.BlockSpec((tr, C), lambda i: (i, 0))
    sds = jax.ShapeDtypeStruct((R, C), F32)
    return pl.pallas_call(
        body, name=name, grid=(R // tr,), in_specs=[blk] * 4, out_specs=[blk] * 3, out_shape=[sds] * 3,
        compiler_params=_params("parallel"),
    )(w, g, m, v)


def sum_slots(x, name):
    n, R, C = x.shape
    tr = _rows(R, 256)

    def body(x_ref, o_ref):
        acc = x_ref[0].astype(F32)
        for s in range(1, n):
            acc = acc + x_ref[s].astype(F32)
        o_ref[...] = acc

    return pl.pallas_call(
        body, name=name, grid=(R // tr,),
        in_specs=[pl.BlockSpec((n, tr, C), lambda i: (0, i, 0))],
        out_specs=pl.BlockSpec((tr, C), lambda i: (i, 0)),
        out_shape=jax.ShapeDtypeStruct((R, C), F32),
        compiler_params=_params("parallel"),
    )(x)


def add_pair(a, b, name, out_dtype):
    R, C = a.shape
    tr = _rows(R, 512)

    def body(a_ref, b_ref, o_ref):
        o_ref[...] = (a_ref[...].astype(F32) + b_ref[...].astype(F32)).astype(out_dtype)

    blk = pl.BlockSpec((tr, C), lambda i: (i, 0))
    return pl.pallas_call(
        body, name=name, grid=(R // tr,), in_specs=[blk, blk], out_specs=blk,
        out_shape=jax.ShapeDtypeStruct((R, C), out_dtype), compiler_params=_params("parallel"),
    )(a, b)


_ANY = pl.BlockSpec(memory_space=pl.ANY)
GROUP_ALL = ([(0, 0, 1), (0, 1, 0), (0, 1, 1), (1, 0, 0), (1, 0, 1), (1, 1, 0), (1, 1, 1)],
             lambda d: 4 * d[0] + 2 * d[1] + d[2])
GROUP_CHIPS = ([(0, 1, 0), (1, 0, 0), (1, 1, 0)], lambda d: 2 * d[0] + d[1])
GROUP_SIBLING = ([(0, 0, 1)], lambda d: d[2])


def _me():
    return lax.axis_index("x"), lax.axis_index("y"), lax.axis_index("c")


def _peer(me, flip):
    return tuple(1 - a if f else a for a, f in zip(me, flip))


def _group_exchange(x, name, group, scatter, chunks=1):
    flips, slot = group
    n = len(flips) + 1
    R, C = x.shape[-2:]
    nc = max([k for k in range(1, chunks + 1) if R % (16 * k) == 0] or [1])
    rc = R // nc

    def body(x_ref, o_ref, send_sems, recv_sems, local_sem):
        me = _me()
        mine = pltpu.make_async_copy(x_ref.at[slot(me)] if scatter else x_ref, o_ref.at[slot(me)], local_sem)
        mine.start()
        sends = []
        for k, flip in enumerate(flips):
            peer = _peer(me, flip)
            src = x_ref.at[slot(peer)] if scatter else x_ref
            for j in range(nc):
                part = pl.ds(j * rc, rc)
                cp = pltpu.make_async_remote_copy(
                    src_ref=src.at[part], dst_ref=o_ref.at[slot(me), part], send_sem=send_sems.at[k * nc + j],
                    recv_sem=recv_sems.at[k * nc + j], device_id=peer, device_id_type=MESH_ID)
                cp.start()
                sends.append(cp)
        for k, flip in enumerate(flips):
            peer = _peer(me, flip)
            for j in range(nc):
                part = pl.ds(j * rc, rc)
                pltpu.make_async_remote_copy(
                    src_ref=o_ref.at[slot(me), part], dst_ref=o_ref.at[slot(peer), part],
                    send_sem=send_sems.at[k * nc + j], recv_sem=recv_sems.at[k * nc + j],
                    device_id=peer, device_id_type=MESH_ID).wait_recv()
        for cp in sends:
            cp.wait_send()
        mine.wait()

    return pl.pallas_call(
        body, name=name, in_specs=[_ANY], out_specs=_ANY,
        out_shape=jax.ShapeDtypeStruct((n, R, C), x.dtype),
        scratch_shapes=[pltpu.SemaphoreType.DMA(((n - 1) * nc,)), pltpu.SemaphoreType.DMA(((n - 1) * nc,)),
                        pltpu.SemaphoreType.DMA],
    )(x)


def group_gather(x, name, group, chunks=1):
    return _group_exchange(x, name, group, scatter=False, chunks=chunks)


def group_scatter(x, name, group, chunks=1):
    return _group_exchange(x, name, group, scatter=True, chunks=chunks)


WEIGHTS = ['a_norm_g', 'a_w_in', 'a_b_if', 'a_hnorm_g', 'a_w_out', 'kv_norm_g', 'w_kv', 'b_norm_g', 'b_w_q',
           'b_w_out', 'rel_bias', 'f_norm_g', 'f_w_up', 'f_conv_w', 'f_conv_b', 'f_w_down', 'final_norm_g']
SHARD_AXIS = {'a_norm_g': 1, 'a_w_in': 2, 'a_b_if': None, 'a_hnorm_g': 2, 'a_w_out': 1, 'kv_norm_g': None,
              'w_kv': 1, 'b_norm_g': None, 'b_w_q': 2, 'b_w_out': 1, 'rel_bias': None, 'f_norm_g': None,
              'f_w_up': 2, 'f_conv_w': 2, 'f_conv_b': None, 'f_w_down': 1, 'final_norm_g': None}
BIG = ['a_w_in', 'a_w_out', 'w_kv', 'b_w_q', 'b_w_out', 'f_w_up', 'f_w_down']
SMALL = [n for n in WEIGHTS if n not in BIG]
LANES = 1024


def _pad_rows(flat, mult):
    n = flat.shape[0]
    per = LANES * mult
    tot = -(-n // per) * per
    return jnp.pad(flat, (0, tot - n)).reshape(tot // LANES, LANES)


def _full_from_shards(sh, axis):
    return jnp.concatenate([sh[j] for j in range(4)], axis=axis)


def _shards_from_full(full, axis):
    return jnp.stack(jnp.split(full, 4, axis=axis))


def _local_step(x, target, W):
    T = x.shape[0]
    row = lambda a: a.reshape(1, -1).astype(F32)
    w_in = jnp.pad(W['a_w_in'][0], ((0, 0), (0, A_IN_PAD - A_IN)))
    bias128 = jnp.pad(row(W['a_b_if'][0]), ((0, 0), (0, 120)))
    hng = row(W['a_hnorm_g'][0])
    w_up = [_interleave(W['f_w_up'][l]) for l in range(2)]
    cw = [_interleave(W['f_conv_w'][l].astype(F32)) for l in range(2)]
    cb = [_interleave(row(W['f_conv_b'][l])) for l in range(2)]
    onehots = [(jnp.asarray(_group_bucket(g).reshape(-1, 1)) == jnp.arange(128)[None, :]).astype(F32)
               for g in range(N_GROUPS)]
    rb_t = jnp.pad(W['rel_bias'].astype(F32).T, ((0, 0), (0, 128 - REL_BUCKETS)))
    biases = [mm_nn(rb_t[g * B_HEADS:(g + 1) * B_HEADS], onehots[g].T, f"rel_bias_table_g{g}", exact=True)
              .reshape(B_HEADS, B_BLOCK, 2 * B_BLOCK) for g in range(N_GROUPS)]
    G = {}

    def ffn_fwd(xin, l):
        xn, = rms_fwd(xin, [row(W['f_norm_g'][l])], f"ffn{l}_norm")
        u = mm_nn(xn, w_up[l], f"ffn{l}_up")
        act = conv_act_fwd(u, cw[l], cb[l], f"ffn{l}_act")
        return mm_nn(act, W['f_w_down'][l], f"ffn{l}_down", res=xin), (xn, u, act)

    def ffn_bwd(xin, saved, dout, l):
        xn, u, act = saved
        dact = mm_nn(dout, W['f_w_down'][l].T, f"ffn{l}_ddown")
        gd = mm_tn(act, dout, f"ffn{l}_gdown")
        du, gcw, gcb = conv_act_bwd(u, dact, cw[l], cb[l], f"ffn{l}_dact")
        dxn = mm_nn(du, w_up[l].T, f"ffn{l}_dup")
        gu = _deinterleave(mm_tn(xn, du, f"ffn{l}_gup"))
        dxin, (gn,) = rms_bwd(xin, dout, [(dxn, row(W['f_norm_g'][l]))], f"ffn{l}_dnorm")
        return dxin, gd, gu, _deinterleave(gcw), _deinterleave(gcb), gn

    xn_a, = rms_fwd(x, [row(W['a_norm_g'][0])], "a_norm")
    z = mm_nn(xn_a, w_in, "a_in")
    gcol, grow = gate_prep(z, bias128)
    hg, Cs, ns, ms = mlstm_fwd(z, gcol, grow, hng)
    x1 = mm_nn(hg, W['a_w_out'][0], "a_out", res=x)
    x2, ffn0 = ffn_fwd(x1, 0)
    xn_kv, xn_b = rms_fwd(x2, [row(W['kv_norm_g']), row(W['b_norm_g'][0])], "b_norms")
    kv = mm_nn(xn_kv, W['w_kv'], "kv_proj", out_dtype=MM_DTYPE)
    qd = mm_nn(xn_b, W['b_w_q'][0], "q_proj", out_dtype=MM_DTYPE)
    os_, lses = zip(*[attn_fwd(qd, kv, biases[g], g) for g in range(N_GROUPS)])
    att, att_f, lse = attn_merge(os_, lses)
    x3 = mm_nn(att, W['b_w_out'][0], "b_out", res=x2)
    x4, ffn1 = ffn_fwd(x3, 1)
    dx4, g_final, loss = loss_head(x4, target, row(W['final_norm_g']))
    G['final_norm_g'] = g_final.reshape(-1)

    dx3, gd1, gu1, gcw1, gcb1, gn1 = ffn_bwd(x3, ffn1, dx4, 1)
    datt = mm_nn(dx3, W['b_w_out'][0].T, "b_dout")
    G['b_w_out'] = mm_tn(att, dx3, "b_gout")[None]
    delta = attn_delta(datt, att_f)
    parts = [attn_bwd(qd, kv, biases[g], datt, lse, delta, g) for g in range(N_GROUPS)]
    dq_all, dkv = attn_combine([p[0] for p in parts], [p[1] for p in parts], [p[2] for p in parts],
                               [p[3] for p in parts], [p[4] for p in parts])
    grb = []
    for g in range(N_GROUPS):
        gb = mm_nn(parts[g][5].reshape(B_HEADS, -1), onehots[g], f"rel_bias_g{g}", exact=True)
        grb.append(gb[:, :REL_BUCKETS].T)
    G['rel_bias'] = jnp.concatenate(grb, axis=1)
    dxn_b = mm_nn(dq_all, W['b_w_q'][0].T, "q_dproj")
    G['b_w_q'] = mm_tn(xn_b, dq_all, "q_gproj")[None]
    dxn_kv = mm_nn(dkv, W['w_kv'].T, "kv_dproj")
    G['w_kv'] = mm_tn(xn_kv, dkv, "kv_gproj")
    dx2, (g_kvn, g_bn) = rms_bwd(x2, dx3, [(dxn_kv, row(W['kv_norm_g'])), (dxn_b, row(W['b_norm_g'][0]))],
                                 "b_dnorms")
    G['kv_norm_g'] = g_kvn.reshape(-1)
    G['b_norm_g'] = g_bn
    dx1, gd0, gu0, gcw0, gcb0, gn0 = ffn_bwd(x1, ffn0, dx2, 0)
    G['f_w_down'] = jnp.stack([gd0, gd1])
    G['f_w_up'] = jnp.stack([gu0, gu1])
    G['f_conv_w'] = jnp.stack([gcw0, gcw1])
    G['f_conv_b'] = jnp.concatenate([gcb0, gcb1], axis=0)
    G['f_norm_g'] = jnp.concatenate([gn0, gn1], axis=0)
    dhg = mm_nn(dx1, W['a_w_out'][0].T, "a_dout")
    G['a_w_out'] = mm_tn(hg, dx1, "a_gout")[None]
    dz, g_hn, g_bif = mlstm_bwd(z, gcol, grow, hng, bias128, Cs, ns, ms, dhg)
    G['a_hnorm_g'] = g_hn.reshape(1, A_HEADS, A_V)
    G['a_b_if'] = g_bif[:, :2 * A_HEADS]
    dxn_a = mm_nn(dz, w_in.T, "a_din")
    G['a_w_in'] = mm_tn(xn_a, dz, "a_gin")[:, :A_IN][None]
    grad_x, (g_an,) = rms_bwd(x, dx1, [(dxn_a, row(W['a_norm_g'][0]))], "a_dnorm")
    G['a_norm_g'] = g_an
    return loss, grad_x, G


def kernel(x, a_norm_g, a_w_in, a_b_if, a_hnorm_g, a_w_out, kv_norm_g, w_kv, b_norm_g, b_w_q, b_w_out, rel_bias, f_norm_g, f_w_up, f_conv_w, f_conv_b, f_w_down, final_norm_g, loss_target, m_a_norm_g, m_a_w_in, m_a_b_if, m_a_hnorm_g, m_a_w_out, m_kv_norm_g, m_w_kv, m_b_norm_g, m_b_w_q, m_b_w_out, m_rel_bias, m_f_norm_g, m_f_w_up, m_f_conv_w, m_f_conv_b, m_f_w_down, m_final_norm_g, v_a_norm_g, v_a_w_in, v_a_b_if, v_a_hnorm_g, v_a_w_out, v_kv_norm_g, v_w_kv, v_b_norm_g, v_b_w_q, v_b_w_out, v_rel_bias, v_f_norm_g, v_f_w_up, v_f_conv_w, v_f_conv_b, v_f_w_down, v_final_norm_g):
    given = dict(locals())
    shard = {n: given[n] for n in WEIGHTS}
    mom = {n: given["m_" + n] for n in WEIGHTS}
    var = {n: given["v_" + n] for n in WEIGHTS}
    cx, cy, cc = _me()
    chip = 2 * cx + cy

    halves = [shard[n].astype(MM_DTYPE).reshape(2, -1, LANES) for n in BIG]
    sizes = [h.shape[1] for h in halves]
    mine = lax.dynamic_index_in_dim(jnp.concatenate(halves, axis=1), cc, axis=0, keepdims=False)
    fill = -mine.shape[0] % 16
    mine = jnp.pad(mine, ((0, fill), (0, 0)))
    rows = mine.shape[0]
    by_chip = group_gather(mine, "gather_weights_chips", GROUP_CHIPS)
    gathered = group_gather(by_chip.reshape(4 * rows, LANES), "gather_weights_sibling",
                            GROUP_SIBLING, chunks=10).reshape(2, 4, rows, LANES)
    W = {}
    off = 0
    for n, sz in zip(BIG, sizes):
        sh = jnp.swapaxes(gathered[:, :, off:off + sz], 0, 1).reshape((4,) + shard[n].shape)
        W[n] = _full_from_shards(sh, SHARD_AXIS[n])
        off += sz
    sharded_small = [n for n in SMALL if SHARD_AXIS[n] is not None]
    ssz = [shard[n].size for n in sharded_small]
    sflat = jnp.concatenate([shard[n].reshape(-1) for n in sharded_small])
    sg = group_gather(_pad_rows(sflat, 8), "gather_small", GROUP_CHIPS).reshape(4, -1)
    off = 0
    for n, sz in zip(sharded_small, ssz):
        W[n] = _full_from_shards(sg[:, off:off + sz].reshape((4,) + shard[n].shape), SHARD_AXIS[n])
        off += sz
    for n in SMALL:
        if SHARD_AXIS[n] is None:
            W[n] = shard[n]

    loss_row, grad_x, G = _local_step(x[0], loss_target[0], W)

    by_half = jnp.concatenate(
        [jnp.swapaxes(_shards_from_full(G[n], SHARD_AXIS[n]).reshape(4, 2, -1, LANES), 0, 1).astype(GRAD_WIRE_DTYPE)
         for n in BIG] + [jnp.zeros((2, 4, fill, LANES), GRAD_WIRE_DTYPE)], axis=2).reshape(2, 4 * rows, LANES)
    keep = lax.dynamic_index_in_dim(by_half, cc, axis=0, keepdims=False)
    give = lax.dynamic_index_in_dim(by_half, 1 - cc, axis=0, keepdims=False)
    got = lax.dynamic_index_in_dim(group_gather(give, "pair_grads", GROUP_SIBLING, chunks=10), 1 - cc, axis=0,
                                   keepdims=False)
    chip_sum = add_pair(keep, got, "sum_pair_grads", GRAD_WIRE_DTYPE).reshape(4, rows, LANES)
    reduced = sum_slots(group_scatter(chip_sum, "scatter_grads", GROUP_CHIPS), "sum_grads")
    both = group_gather(reduced, "join_halves", GROUP_SIBLING, chunks=10)
    gsh = {}
    off = 0
    for n, sz in zip(BIG, sizes):
        gsh[n] = both[:, off:off + sz].reshape(shard[n].shape)
        off += sz
    small_parts = [loss_row[0, 0:1]] + [G[n].reshape(-1) for n in SMALL]
    small_sz = [p.shape[0] for p in small_parts]
    small = sum_slots(group_gather(_pad_rows(jnp.concatenate(small_parts), 8), "gather_small_grads", GROUP_ALL),
                      "sum_small_grads").reshape(-1)
    loss = small[0]
    off = 1
    for n, sz in zip(SMALL, small_sz[1:]):
        full = small[off:off + sz].reshape(W[n].shape)
        off += sz
        if SHARD_AXIS[n] is None:
            gsh[n] = full
        else:
            gsh[n] = lax.dynamic_index_in_dim(_shards_from_full(full, SHARD_AXIS[n]), chip, 0, keepdims=False)

    delta, new_m, new_v = {}, {}, {}
    for n in WEIGHTS:
        shp = shard[n].shape
        two = lambda a: a.reshape(-1, shp[-1])
        d, nm, nv = adamw(two(shard[n]), two(gsh[n]), two(mom[n]), two(var[n]), f"adamw_{n}")
        delta[n], new_m[n], new_v[n] = d.reshape(shp), nm.reshape(shp), nv.reshape(shp)
    return (loss, grad_x[None], *[gsh[n] for n in WEIGHTS], *[delta[n] for n in WEIGHTS],
            *[new_m[n] for n in WEIGHTS], *[new_v[n] for n in WEIGHTS])
```

```python
import functools
import math

import numpy as np
import jax
import jax.numpy as jnp
from jax import lax
from jax.experimental import pallas as pl
from jax.experimental.pallas import tpu as pltpu

F32 = jnp.float32
BF16 = jnp.bfloat16
MM_DTYPE = jnp.bfloat16
GRAD_WIRE_DTYPE = jnp.bfloat16
HI = lax.Precision.HIGHEST

D_MODEL = 1024
A_HEADS = 4
A_QK = 128
A_V = 256
A_CHUNK = 64
A_IN = 3080
A_IN_PAD = 3200
GATE_COL = 3072
SOFTCAP = 15.0
N_GROUPS = 3
B_HEADS = 16
B_DH = 64
B_BLOCK = 128
DILATIONS = (1, 4, 16)
WINDOWS = (128, 512, 2048)
REL_BUCKETS = 32
REL_MAX_DIST = 2048
D_FF = 2816
FF_TC = 256
EPS = 1e-6
ADAM_LR, ADAM_B1, ADAM_B2, ADAM_EPS, ADAM_WD, ADAM_STEP = 0.001, 0.9, 0.999, 1e-08, 0.01, 10

VMEM_LIMIT = 56 * 1024 * 1024
NT_DIMS = (((1,), (1,)), ((), ()))
TN_DIMS = (((0,), (0,)), ((), ()))
MESH_ID = pl.DeviceIdType.MESH


def _params(*sem):
    return pltpu.CompilerParams(dimension_semantics=sem, vmem_limit_bytes=VMEM_LIMIT)


def _tile(n, cap):
    if n <= cap:
        return n
    best = None
    for t in range(128, cap + 1, 128):
        if n % t == 0:
            best = t
    assert best is not None, (n, cap)
    return best


def _rows(n, cap):
    if n <= cap:
        return n
    for t in range(cap // 8 * 8, 7, -8):
        if n % t == 0:
            return t
    raise ValueError((n, cap))


def _dot(a, b):
    return jnp.dot(a.astype(MM_DTYPE), b.astype(MM_DTYPE), preferred_element_type=F32)


def _dot_nt(a, b):
    return lax.dot_general(a.astype(MM_DTYPE), b.astype(MM_DTYPE), NT_DIMS, preferred_element_type=F32)


def _dot_tn(a, b):
    return lax.dot_general(a.astype(MM_DTYPE), b.astype(MM_DTYPE), TN_DIMS, preferred_element_type=F32)


def _sigmoid(x):
    return 1.0 / (1.0 + jnp.exp(-x))


def _sigmoid_tanh(x):
    return 0.5 * jnp.tanh(0.5 * x) + 0.5


def mm_nn(a, b, name, res=None, out_dtype=F32, exact=False):
    M, K = a.shape
    N = b.shape[1]
    tm, tn, tk = _rows(M, 512), _tile(N, 1536), _tile(K, 1536)
    nk = K // tk

    def body(*refs):
        if res is None:
            a_ref, b_ref, o_ref, acc = refs
            r_ref = None
        else:
            a_ref, b_ref, r_ref, o_ref, acc = refs
        if exact:
            p = jnp.dot(a_ref[...], b_ref[...], precision=HI, preferred_element_type=F32)
        else:
            p = _dot(a_ref[...], b_ref[...])

        def finish(total):
            if r_ref is not None:
                total = total + r_ref[...]
            o_ref[...] = total.astype(out_dtype)

        if nk == 1:
            finish(p)
        else:
            k = pl.program_id(2)

            @pl.when(k == 0)
            def _():
                acc[...] = p

            @pl.when(jnp.logical_and(k > 0, k < nk - 1))
            def _():
                acc[...] += p

            @pl.when(k == nk - 1)
            def _():
                finish(acc[...] + p)

    in_specs = [pl.BlockSpec((tm, tk), lambda j, i, k: (i, k)),
                pl.BlockSpec((tk, tn), lambda j, i, k: (k, j))]
    args = [a, b]
    if res is not None:
        in_specs.append(pl.BlockSpec((tm, tn), lambda j, i, k: (i, j)))
        args.append(res)
    acc_shape = (tm, tn) if nk > 1 else (8, 128)
    return pl.pallas_call(
        body, name=name, grid=(N // tn, M // tm, nk),
        in_specs=in_specs, out_specs=pl.BlockSpec((tm, tn), lambda j, i, k: (i, j)),
        out_shape=jax.ShapeDtypeStruct((M, N), out_dtype),
        scratch_shapes=[pltpu.VMEM(acc_shape, F32)],
        compiler_params=_params("parallel", "parallel", "arbitrary"),
    )(*args)


def mm_tn(a, g, name):
    T, Ka = a.shape
    N = g.shape[1]
    tka, tn, tt = _tile(Ka, 1536), _tile(N, 1536), _rows(T, 512)
    nt = T // tt

    def body(a_ref, g_ref, o_ref):
        t = pl.program_id(2)
        p = _dot_tn(a_ref[...], g_ref[...])

        @pl.when(t == 0)
        def _():
            o_ref[...] = p

        @pl.when(t > 0)
        def _():
            o_ref[...] += p

    return pl.pallas_call(
        body, name=name, grid=(Ka // tka, N // tn, nt),
        in_specs=[pl.BlockSpec((tt, tka), lambda i, j, t: (t, i)),
                  pl.BlockSpec((tt, tn), lambda i, j, t: (t, j))],
        out_specs=pl.BlockSpec((tka, tn), lambda i, j, t: (i, j)),
        out_shape=jax.ShapeDtypeStruct((Ka, N), F32),
        compiler_params=_params("parallel", "parallel", "arbitrary"),
    )(a, g)


def rms_fwd(x, gains, name):
    T, D = x.shape
    tt = _rows(T, 512)
    ng = len(gains)

    def body(*refs):
        x_ref = refs[0]
        g_refs = refs[1:1 + ng]
        o_refs = refs[1 + ng:]
        xf = x_ref[...]
        y = xf * lax.rsqrt(jnp.mean(xf * xf, axis=-1, keepdims=True) + EPS)
        for g_ref, o_ref in zip(g_refs, o_refs):
            o_ref[...] = (y * g_ref[...]).astype(o_ref.dtype)

    row = pl.BlockSpec((tt, D), lambda i: (i, 0))
    gsp = pl.BlockSpec((1, D), lambda i: (0, 0))
    return pl.pallas_call(
        body, name=name, grid=(T // tt,),
        in_specs=[row] + [gsp] * ng, out_specs=[row] * ng,
        out_shape=[jax.ShapeDtypeStruct((T, D), MM_DTYPE)] * ng,
        compiler_params=_params("parallel"),
    )(x, *gains)


def rms_bwd(x, dres, branches, name):
    T, D = x.shape
    tt = _rows(T, 256)
    nb = len(branches)

    def body(*refs):
        x_ref, r_ref = refs[0], refs[1]
        dy_refs = refs[2:2 + nb]
        g_refs = refs[2 + nb:2 + 2 * nb]
        dx_ref = refs[2 + 2 * nb]
        dg_refs = refs[3 + 2 * nb:]
        i = pl.program_id(0)
        xf = x_ref[...]
        r = lax.rsqrt(jnp.mean(xf * xf, axis=-1, keepdims=True) + EPS)
        xh = xf * r
        dx = r_ref[...]
        for dy_ref, g_ref, dg_ref in zip(dy_refs, g_refs, dg_refs):
            dy = dy_ref[...].astype(F32)
            dyg = dy * g_ref[...]
            dx = dx + r * (dyg - xh * jnp.mean(dyg * xh, axis=-1, keepdims=True))
            part = jnp.sum(dy * xh, axis=0, keepdims=True)

            @pl.when(i == 0)
            def _():
                dg_ref[...] = part

            @pl.when(i > 0)
            def _():
                dg_ref[...] += part
        dx_ref[...] = dx

    row = pl.BlockSpec((tt, D), lambda i: (i, 0))
    gsp = pl.BlockSpec((1, D), lambda i: (0, 0))
    outs = pl.pallas_call(
        body, name=name, grid=(T // tt,),
        in_specs=[row, row] + [row] * nb + [gsp] * nb,
        out_specs=[row] + [gsp] * nb,
        out_shape=[jax.ShapeDtypeStruct((T, D), F32)] + [jax.ShapeDtypeStruct((1, D), F32)] * nb,
        compiler_params=_params("arbitrary"),
    )(x, dres, *[b[0] for b in branches], *[b[1] for b in branches])
    return outs[0], outs[1:]


def loss_head(x, target, gain):
    T, D = x.shape
    tt = _rows(T, 256)

    def body(x_ref, t_ref, g_ref, dx_ref, dg_ref, loss_ref):
        i = pl.program_id(0)
        xf = x_ref[...]
        g = g_ref[...]
        r = lax.rsqrt(jnp.mean(xf * xf, axis=-1, keepdims=True) + EPS)
        xh = xf * r
        e = xh * g - t_ref[...]
        lpart = 0.5 * jnp.sum(jnp.sum(e * e, axis=1, keepdims=True), axis=0, keepdims=True) / D
        dy = e / D
        dyg = dy * g
        dx_ref[...] = r * (dyg - xh * jnp.mean(dyg * xh, axis=-1, keepdims=True))
        gpart = jnp.sum(dy * xh, axis=0, keepdims=True)
        lrow = jnp.broadcast_to(lpart, (1, 128))

        @pl.when(i == 0)
        def _():
            dg_ref[...] = gpart
            loss_ref[...] = lrow

        @pl.when(i > 0)
        def _():
            dg_ref[...] += gpart
            loss_ref[...] += lrow

    row = pl.BlockSpec((tt, D), lambda i: (i, 0))
    gsp = pl.BlockSpec((1, D), lambda i: (0, 0))
    return pl.pallas_call(
        body, name="loss_head", grid=(T // tt,),
        in_specs=[row, row, gsp],
        out_specs=[row, gsp, pl.BlockSpec((1, 128), lambda i: (0, 0))],
        out_shape=[jax.ShapeDtypeStruct((T, D), F32), jax.ShapeDtypeStruct((1, D), F32),
                   jax.ShapeDtypeStruct((1, 128), F32)],
        compiler_params=_params("arbitrary"),
    )(x, target, gain)


def _shift_down(u, prev8, first, k):
    rolled = pltpu.roll(u, k, 0)
    rid = lax.broadcasted_iota(jnp.int32, u.shape, 0)
    halo = jnp.where(first, 0.0, prev8)
    out = rolled
    for j in range(k):
        out = jnp.where(rid == j, halo[8 - k + j:8 - k + j + 1, :], out)
    return out


def _conv3(u, prev8, first, w, b):
    return (_shift_down(u, prev8, first, 2) * w[0:1, :] + _shift_down(u, prev8, first, 1) * w[1:2, :]
            + u * w[2:3, :] + b)


def conv_act_fwd(u, w, b, name):
    T = u.shape[0]
    tt = _rows(T, 512)
    nj = D_FF // FF_TC

    def body(u_ref, p_ref, w_ref, b_ref, o_ref):
        first = pl.program_id(1) == 0
        c = _conv3(u_ref[...], p_ref[...], first, w_ref[...], b_ref[...])
        cg, cv = c[:, :FF_TC], c[:, FF_TC:]
        o_ref[...] = (cg * _sigmoid_tanh(cg) * cv).astype(o_ref.dtype)

    return pl.pallas_call(
        body, name=name, grid=(nj, T // tt),
        in_specs=[pl.BlockSpec((tt, 2 * FF_TC), lambda j, i: (i, j)),
                  pl.BlockSpec((8, 2 * FF_TC), lambda j, i: (jnp.maximum(i * (tt // 8) - 1, 0), j)),
                  pl.BlockSpec((3, 2 * FF_TC), lambda j, i: (0, j)),
                  pl.BlockSpec((1, 2 * FF_TC), lambda j, i: (0, j))],
        out_specs=pl.BlockSpec((tt, FF_TC), lambda j, i: (i, j)),
        out_shape=jax.ShapeDtypeStruct((T, D_FF), MM_DTYPE),
        compiler_params=_params("parallel", "parallel"),
    )(u, u, w, b)


def conv_act_bwd(u, da, w, b, name):
    T = u.shape[0]
    tt = _rows(T, 512)
    nt = T // tt
    nj = D_FF // FF_TC
    te = tt + 8

    def body(u_ref, p_ref, n_ref, da_ref, dan_ref, w_ref, b_ref, du_ref, dw_ref, db_ref):
        i = pl.program_id(1)
        first = i == 0
        last = i == nt - 1
        w = w_ref[...]
        ue = jnp.concatenate([u_ref[...], n_ref[...]], axis=0)
        dae = jnp.concatenate([da_ref[...], jnp.where(last, 0.0, dan_ref[...])], axis=0)
        um2 = _shift_down(ue, p_ref[...], first, 2)
        um1 = _shift_down(ue, p_ref[...], first, 1)
        c = um2 * w[0:1, :] + um1 * w[1:2, :] + ue * w[2:3, :] + b_ref[...]
        cg, cv = c[:, :FF_TC], c[:, FF_TC:]
        s = _sigmoid_tanh(cg)
        dcg = dae * cv * (s * (1.0 + cg * (1.0 - s)))
        dcv = dae * (cg * s)
        dc = jnp.concatenate([dcg, dcv], axis=1)
        du = (dc * w[2:3, :] + pltpu.roll(dc, te - 1, 0) * w[1:2, :] + pltpu.roll(dc, te - 2, 0) * w[0:1, :])
        du_ref[...] = du[:tt, :].astype(du_ref.dtype)
        dcm = dc[:tt, :]
        dwp = jnp.concatenate([jnp.sum(dcm * um2[:tt, :], axis=0, keepdims=True),
                               jnp.sum(dcm * um1[:tt, :], axis=0, keepdims=True),
                               jnp.sum(dcm * ue[:tt, :], axis=0, keepdims=True)], axis=0)
        dbp = jnp.sum(dcm, axis=0, keepdims=True)

        @pl.when(first)
        def _():
            dw_ref[...] = dwp
            db_ref[...] = dbp

        @pl.when(i > 0)
        def _():
            dw_ref[...] += dwp
            db_ref[...] += dbp

    nb8 = T // 8
    return pl.pallas_call(
        body, name=name, grid=(nj, nt),
        in_specs=[pl.BlockSpec((tt, 2 * FF_TC), lambda j, i: (i, j)),
                  pl.BlockSpec((8, 2 * FF_TC), lambda j, i: (jnp.maximum(i * (tt // 8) - 1, 0), j)),
                  pl.BlockSpec((8, 2 * FF_TC), lambda j, i: (jnp.minimum((i + 1) * (tt // 8), nb8 - 1), j)),
                  pl.BlockSpec((tt, FF_TC), lambda j, i: (i, j)),
                  pl.BlockSpec((8, FF_TC), lambda j, i: (jnp.minimum((i + 1) * (tt // 8), nb8 - 1), j)),
                  pl.BlockSpec((3, 2 * FF_TC), lambda j, i: (0, j)),
                  pl.BlockSpec((1, 2 * FF_TC), lambda j, i: (0, j))],
        out_specs=[pl.BlockSpec((tt, 2 * FF_TC), lambda j, i: (i, j)),
                   pl.BlockSpec((3, 2 * FF_TC), lambda j, i: (0, j)),
                   pl.BlockSpec((1, 2 * FF_TC), lambda j, i: (0, j))],
        out_shape=[jax.ShapeDtypeStruct((T, 2 * D_FF), MM_DTYPE),
                   jax.ShapeDtypeStruct((3, 2 * D_FF), F32),
                   jax.ShapeDtypeStruct((1, 2 * D_FF), F32)],
        compiler_params=_params("parallel", "arbitrary"),
    )(u, u, u, da, da, w, b)


def _interleave(a):
    lead = a.shape[:-1]
    nj = D_FF // FF_TC
    return jnp.swapaxes(a.reshape(*lead, 2, nj, FF_TC), -3, -2).reshape(*lead, 2 * D_FF)


def _deinterleave(a):
    lead = a.shape[:-1]
    nj = D_FF // FF_TC
    return jnp.swapaxes(a.reshape(*lead, nj, 2, FF_TC), -3, -2).reshape(*lead, 2 * D_FF)


A_GC = 2
A_TB = A_GC * A_CHUNK


def gate_prep(z, bias128):
    T = z.shape[0]
    tt = _rows(T, 512)

    def body(z_ref, b_ref, gc_ref, gr_ref):
        pre = z_ref[...] + b_ref[...]
        sc = SOFTCAP * jnp.tanh(pre / SOFTCAP)
        lf = jnp.minimum(sc, 0.0) - jnp.log(1.0 + jnp.exp(-jnp.abs(sc)))
        col = lax.broadcasted_iota(jnp.int32, pre.shape, 1)
        isf = jnp.logical_and(col >= A_HEADS, col < 2 * A_HEADS)
        r = lax.broadcasted_iota(jnp.int32, (tt, tt), 0)
        c = lax.broadcasted_iota(jnp.int32, (tt, tt), 1)
        tri = jnp.logical_and(jnp.right_shift(r, 6) == jnp.right_shift(c, 6), c <= r).astype(F32)
        bcum = jnp.dot(tri, jnp.where(isf, lf, 0.0), precision=HI, preferred_element_type=F32)
        g = jnp.where(col < A_HEADS, sc, jnp.where(isf, bcum, 0.0))
        gc_ref[...] = g
        for s in range(tt // 128):
            gr_ref[s] = g[s * 128:(s + 1) * 128, :].T[0:8, :]

    return pl.pallas_call(
        body, name="gate_prep", grid=(T // tt,),
        in_specs=[pl.BlockSpec((tt, 128), lambda i: (i, GATE_COL // 128)),
                  pl.BlockSpec((1, 128), lambda i: (0, 0))],
        out_specs=[pl.BlockSpec((tt, 128), lambda i: (i, 0)),
                   pl.BlockSpec((tt // 128, 8, 128), lambda i: (i, 0, 0))],
        out_shape=[jax.ShapeDtypeStruct((T, 128), F32), jax.ShapeDtypeStruct((T // 128, 8, 128), F32)],
        compiler_params=_params("parallel"),
    )(z, bias128)


def _chunk_fwd(qh, kh, vh, bc, br, lir, C, n, m, causal):
    A = _dot_nt(qh, kh)
    logD = jnp.where(causal, bc - br + lir, -jnp.inf)
    m_inter = bc + m
    m_t = jnp.maximum(m_inter, jnp.max(logD, axis=1, keepdims=True))
    E = jnp.exp(logD - m_t)
    Sm = A * E
    wi = jnp.exp(m_inter - m_t)
    num = _dot(Sm, vh) + wi * _dot(qh, C)
    qn = jnp.sum(qh.astype(F32) * n, axis=1, keepdims=True)
    den = jnp.sum(Sm, axis=1, keepdims=True) + wi * qn
    gs = jnp.maximum(jnp.abs(den), jnp.exp(-m_t))
    return E, Sm, wi, num, den, gs, m_t


def _state_weights(bc, lic, br, lir, m):
    bL = bc[A_CHUNK - 1:A_CHUNK, :]
    m_new = jnp.maximum(bL + m, jnp.max(bL - br + lir, axis=1, keepdims=True))
    wk = jnp.exp(bL - bc + lic - m_new)
    decay = jnp.exp(bL + m - m_new)
    return wk, decay, m_new


def _head_slices(h):
    return (slice(h * A_QK, (h + 1) * A_QK), slice(h * A_V, (h + 1) * A_V))


def mlstm_fwd(z, gcol, grow, hng):
    T = z.shape[0]
    NC = T // A_CHUNK
    scale = A_QK ** -0.5

    def body(q_ref, k_ref, v_ref, o_ref, gc_ref, gr_ref, hng_ref, hg_ref, Cs_ref, ns_ref, ms_ref,
             C_sc, n_sc, m_sc):
        @pl.when(pl.program_id(0) == 0)
        def _():
            C_sc[...] = jnp.zeros_like(C_sc)
            n_sc[...] = jnp.zeros_like(n_sc)
            m_sc[...] = jnp.zeros_like(m_sc)

        ri = lax.broadcasted_iota(jnp.int32, (A_CHUNK, A_CHUNK), 0)
        ci = lax.broadcasted_iota(jnp.int32, (A_CHUNK, A_CHUNK), 1)
        causal = ri >= ci
        gr = gr_ref[0]
        for c in range(A_GC):
            rows = slice(c * A_CHUNK, (c + 1) * A_CHUNK)
            gc = gc_ref[rows, :]
            grc = gr[:, c * A_CHUNK:(c + 1) * A_CHUNK]
            for h in range(A_HEADS):
                sk, sv = _head_slices(h)
                qh = (q_ref[rows, sk] * scale).astype(MM_DTYPE)
                kh = k_ref[rows, sk].astype(MM_DTYPE)
                vh = v_ref[rows, sv].astype(MM_DTYPE)
                lic, bc = gc[:, h:h + 1], gc[:, A_HEADS + h:A_HEADS + h + 1]
                lir, br = grc[h:h + 1, :], grc[A_HEADS + h:A_HEADS + h + 1, :]
                C, n, m = C_sc[h], n_sc[h], m_sc[h][:, 0:1]
                Cs_ref[c, h] = C
                ns_ref[c, h] = n
                ms_ref[c, h] = m_sc[h]
                _, _, _, num, _, gs, _ = _chunk_fwd(qh, kh, vh, bc, br, lir, C, n, m, causal)
                hh = num / gs
                hn = hh * lax.rsqrt(jnp.mean(hh * hh, axis=1, keepdims=True) + EPS) * hng_ref[:, sv]
                hg_ref[rows, sv] = (hn * _sigmoid(o_ref[rows, sv])).astype(hg_ref.dtype)
                wk, decay, m_new = _state_weights(bc, lic, br, lir, m)
                kw = kh.astype(F32) * wk
                C_sc[h] = decay * C + _dot_tn(kw, vh)
                n_sc[h] = decay * n + jnp.sum(kw, axis=0, keepdims=True)
                m_sc[h] = jnp.broadcast_to(m_new, (1, 128))

    tok = lambda w, cb: pl.BlockSpec((A_TB, w), lambda i: (i, cb))
    return pl.pallas_call(
        body, name="mlstm_fwd", grid=(NC // A_GC,),
        in_specs=[tok(512, 0), tok(512, 1), tok(1024, 1), tok(1024, 2),
                  pl.BlockSpec((A_TB, 128), lambda i: (i, 0)),
                  pl.BlockSpec((1, 8, 128), lambda i: (i, 0, 0)),
                  pl.BlockSpec((1, 1024), lambda i: (0, 0))],
        out_specs=[pl.BlockSpec((A_TB, 1024), lambda i: (i, 0)),
                   pl.BlockSpec((A_GC, A_HEADS, A_QK, A_V), lambda i: (i, 0, 0, 0)),
                   pl.BlockSpec((A_GC, A_HEADS, 1, 128), lambda i: (i, 0, 0, 0)),
                   pl.BlockSpec((A_GC, A_HEADS, 1, 128), lambda i: (i, 0, 0, 0))],
        out_shape=[jax.ShapeDtypeStruct((T, 1024), MM_DTYPE),
                   jax.ShapeDtypeStruct((NC, A_HEADS, A_QK, A_V), F32),
                   jax.ShapeDtypeStruct((NC, A_HEADS, 1, 128), F32),
                   jax.ShapeDtypeStruct((NC, A_HEADS, 1, 128), F32)],
        scratch_shapes=[pltpu.VMEM((A_HEADS, A_QK, A_V), F32), pltpu.VMEM((A_HEADS, 1, 128), F32),
                        pltpu.VMEM((A_HEADS, 1, 128), F32)],
        compiler_params=_params("arbitrary"),
    )(z, z, z, z, gcol, grow, hng)


def mlstm_bwd(z, gcol, grow, hng, bias128, Cs, ns, ms, dhg):
    T = z.shape[0]
    NC = T // A_CHUNK
    nsteps = NC // A_GC
    scale = A_QK ** -0.5

    def body(q_ref, k_ref, v_ref, o_ref, zg_ref, gc_ref, gr_ref, hng_ref, b_ref, Cs_ref, ns_ref, ms_ref,
             dhg_ref, dz_ref, dgn_ref, dbif_ref, dC_sc, dn_sc):
        @pl.when(pl.program_id(0) == 0)
        def _():
            dC_sc[...] = jnp.zeros_like(dC_sc)
            dn_sc[...] = jnp.zeros_like(dn_sc)
            dgn_ref[...] = jnp.zeros_like(dgn_ref)
            dbif_ref[...] = jnp.zeros_like(dbif_ref)

        ri = lax.broadcasted_iota(jnp.int32, (A_CHUNK, A_CHUNK), 0)
        ci = lax.broadcasted_iota(jnp.int32, (A_CHUNK, A_CHUNK), 1)
        causal = ri >= ci
        upper = (ci >= ri).astype(F32)
        rid = lax.broadcasted_iota(jnp.int32, (A_CHUNK, 1), 0)
        col = lax.broadcasted_iota(jnp.int32, (A_CHUNK, 128), 1)
        gr = gr_ref[0]
        for c in reversed(range(A_GC)):
            rows = slice(c * A_CHUNK, (c + 1) * A_CHUNK)
            gc = gc_ref[rows, :]
            grc = gr[:, c * A_CHUNK:(c + 1) * A_CHUNK]
            dG = jnp.zeros((A_CHUNK, 128), F32)
            for h in range(A_HEADS):
                sk, sv = _head_slices(h)
                qh = (q_ref[rows, sk] * scale).astype(MM_DTYPE)
                kh = k_ref[rows, sk].astype(MM_DTYPE)
                vh = v_ref[rows, sv].astype(MM_DTYPE)
                qf, kf = qh.astype(F32), kh.astype(F32)
                lic, bc = gc[:, h:h + 1], gc[:, A_HEADS + h:A_HEADS + h + 1]
                lir, br = grc[h:h + 1, :], grc[A_HEADS + h:A_HEADS + h + 1, :]
                C, n, m = Cs_ref[c, h], ns_ref[c, h], ms_ref[c, h][:, 0:1]
                dC, dn = dC_sc[h], dn_sc[h]
                E, Sm, wi, num, den, gs, m_t = _chunk_fwd(qh, kh, vh, bc, br, lir, C, n, m, causal)
                wk, decay, _ = _state_weights(bc, lic, br, lir, m)
                hh = num / gs
                r = lax.rsqrt(jnp.mean(hh * hh, axis=1, keepdims=True) + EPS)
                gn = hng_ref[:, sv]
                o = o_ref[rows, sv]
                s = _sigmoid(o)
                dhg_h = dhg_ref[rows, sv]
                dhn = dhg_h * s
                dz_ref[rows, 2048 + h * A_V:2048 + (h + 1) * A_V] = dhg_h * (hh * r * gn) * s * (1.0 - s)
                dgn_ref[:, sv] += jnp.sum(dhn * hh * r, axis=0, keepdims=True)
                dyg = dhn * gn
                dh = r * dyg - hh * (r * r * r) * jnp.mean(dyg * hh, axis=1, keepdims=True)
                dnum = dh / gs
                live = (jnp.abs(den) > jnp.exp(-m_t)).astype(F32)
                dden = -jnp.sum(dh * hh, axis=1, keepdims=True) / gs * jnp.sign(den) * live
                dSE = jnp.where(causal, _dot_nt(dnum, vh) + dden, 0.0) * E
                dq = _dot(dSE, kh) + wi * (_dot_nt(dnum, C) + dden * n)
                dk_inter = wk * (_dot_nt(vh, dC) + dn)
                dk = _dot_tn(dSE, qh) + dk_inter
                dv = _dot_tn(Sm, dnum) + wk * _dot(kh, dC)
                dz_ref[rows, sk] = dq * scale
                dz_ref[rows, 512 + h * A_QK:512 + (h + 1) * A_QK] = dk
                dz_ref[rows, 1024 + h * A_V:1024 + (h + 1) * A_V] = dv
                dli = jnp.sum(kf * dk, axis=1, keepdims=True)
                db = jnp.sum(qf * dq, axis=1, keepdims=True) - dli
                usum = jnp.sum(jnp.sum(kf * dk_inter, axis=1, keepdims=True), axis=0, keepdims=True)
                ddecay = (jnp.sum(jnp.sum(dC * C, axis=1, keepdims=True), axis=0, keepdims=True)
                          + jnp.sum(dn * n, axis=1, keepdims=True))
                db = db + jnp.where(rid == A_CHUNK - 1, usum + ddecay * decay, 0.0)
                dG = dG + jnp.where(col == h, dli, 0.0) + jnp.where(col == A_HEADS + h, db, 0.0)
                dC_sc[h] = decay * dC + _dot_tn(qf * wi, dnum)
                dn_sc[h] = decay * dn + jnp.sum(qf * (wi * dden), axis=0, keepdims=True)
            dlf = jnp.dot(upper, dG, precision=HI, preferred_element_type=F32)
            pre = zg_ref[rows, :] + b_ref[...]
            th = jnp.tanh(pre / SOFTCAP)
            dcap = 1.0 - th * th
            dpre = jnp.where(col < A_HEADS, dG * dcap,
                             jnp.where(col < 2 * A_HEADS, dlf * _sigmoid(-SOFTCAP * th) * dcap, 0.0))
            dz_ref[rows, GATE_COL:GATE_COL + 128] = dpre
            dbif_ref[...] += jnp.sum(dpre, axis=0, keepdims=True)

    rev = lambda i: nsteps - 1 - i
    tok = lambda w, cb: pl.BlockSpec((A_TB, w), lambda i: (rev(i), cb))
    st = lambda a, b: pl.BlockSpec((A_GC, A_HEADS, a, b), lambda i: (rev(i), 0, 0, 0))
    return pl.pallas_call(
        body, name="mlstm_bwd", grid=(nsteps,),
        in_specs=[tok(512, 0), tok(512, 1), tok(1024, 1), tok(1024, 2), tok(128, GATE_COL // 128),
                  pl.BlockSpec((A_TB, 128), lambda i: (rev(i), 0)),
                  pl.BlockSpec((1, 8, 128), lambda i: (rev(i), 0, 0)),
                  pl.BlockSpec((1, 1024), lambda i: (0, 0)),
                  pl.BlockSpec((1, 128), lambda i: (0, 0)),
                  st(A_QK, A_V), st(1, 128), st(1, 128),
                  pl.BlockSpec((A_TB, 1024), lambda i: (rev(i), 0))],
        out_specs=[pl.BlockSpec((A_TB, A_IN_PAD), lambda i: (rev(i), 0)),
                   pl.BlockSpec((1, 1024), lambda i: (0, 0)),
                   pl.BlockSpec((1, 128), lambda i: (0, 0))],
        out_shape=[jax.ShapeDtypeStruct((T, A_IN_PAD), F32), jax.ShapeDtypeStruct((1, 1024), F32),
                   jax.ShapeDtypeStruct((1, 128), F32)],
        scratch_shapes=[pltpu.VMEM((A_HEADS, A_QK, A_V), F32), pltpu.VMEM((A_HEADS, 1, 128), F32)],
        compiler_params=_params("arbitrary"),
    )(z, z, z, z, z, gcol, grow, hng, bias128, Cs, ns, ms, dhg)


def _t5_bucket(dist):
    max_exact = REL_BUCKETS // 2
    d = np.maximum(dist, 0)
    log_ratio = np.log(np.maximum(d, 1) / max_exact) / math.log(REL_MAX_DIST / max_exact)
    large = np.minimum(max_exact + (log_ratio * (REL_BUCKETS - max_exact)).astype(np.int64), REL_BUCKETS - 1)
    return np.where(d < max_exact, d, large).astype(np.int32)


def _group_bucket(g):
    delta = B_BLOCK + np.arange(B_BLOCK)[:, None] - np.arange(2 * B_BLOCK)[None, :]
    return _t5_bucket(delta * DILATIONS[g])


def _band_mask(n):
    ri = lax.broadcasted_iota(jnp.int32, (B_BLOCK, 2 * B_BLOCK), 0)
    ci = lax.broadcasted_iota(jnp.int32, (B_BLOCK, 2 * B_BLOCK), 1)
    band = jnp.logical_and(ci >= ri, ci <= ri + B_BLOCK)
    return jnp.logical_and(band, jnp.logical_or(ci >= B_BLOCK, n > 0))


def _both(p_ref, c_ref, sl):
    return jnp.concatenate([p_ref[:, sl], c_ref[:, sl]], axis=0)


def _scores(qh, kh, bias_h, valid):
    return jnp.where(valid, _dot_nt(qh, kh) * (B_DH ** -0.5) + bias_h, -jnp.inf)


def _group_views(qd, kv, g):
    if g == 0:
        return (qd, kv, kv), ((3, 0), (6, 0), (6, 3))
    dil = DILATIONS[g]
    view = lambda a, c: a[:, c * 1024:(c + 1) * 1024].reshape(a.shape[0] // dil, dil * 1024)
    return (view(qd, g), view(kv, g), view(kv, 3 + g)), ((1, 0), (1, 0), (1, 0))


def _attn_specs(cols):
    (qst, qof), (kst, kof), (vst, vof) = cols
    qs = pl.BlockSpec((B_BLOCK, 1024), lambda r, n: (n, r * qst + qof))
    kc = pl.BlockSpec((B_BLOCK, 1024), lambda r, n: (n, r * kst + kof))
    kp = pl.BlockSpec((B_BLOCK, 1024), lambda r, n: (jnp.maximum(n - 1, 0), r * kst + kof))
    vc = pl.BlockSpec((B_BLOCK, 1024), lambda r, n: (n, r * vst + vof))
    vp = pl.BlockSpec((B_BLOCK, 1024), lambda r, n: (jnp.maximum(n - 1, 0), r * vst + vof))
    bias = pl.BlockSpec((B_HEADS, B_BLOCK, 2 * B_BLOCK), lambda r, n: (0, 0, 0))
    wide = pl.BlockSpec((B_BLOCK, 1024), lambda r, n: (n, r))
    narrow = pl.BlockSpec((B_BLOCK, 128), lambda r, n: (n, r))
    return qs, kp, kc, vp, vc, bias, wide, narrow


def attn_fwd(qd, kv, bias, g):
    T = qd.shape[0]
    dil = DILATIONS[g]
    Tv = T // dil
    nb = Tv // B_BLOCK
    (qv, kvw, vvw), cols = _group_views(qd, kv, g)
    qs, kp, kc, vp, vc, bsp, wide, narrow = _attn_specs(cols)

    def body(q_ref, kp_ref, kc_ref, vp_ref, vc_ref, b_ref, o_ref, lse_ref):
        valid = _band_mask(pl.program_id(1))
        lse_ref[...] = jnp.zeros_like(lse_ref)
        heads = [slice(h * B_DH, (h + 1) * B_DH) for h in range(B_HEADS)]
        S = [_scores(q_ref[:, sl], _both(kp_ref, kc_ref, sl), b_ref[h], valid) for h, sl in enumerate(heads)]
        P, L = [], []
        for h in range(B_HEADS):
            m = jnp.max(S[h], axis=1, keepdims=True)
            p = jnp.exp(S[h] - m)
            l = jnp.sum(p, axis=1, keepdims=True)
            lse_ref[:, h:h + 1] = m + jnp.log(l)
            P.append(p.astype(MM_DTYPE))
            L.append(l)
        for h, sl in enumerate(heads):
            o_ref[:, sl] = _dot(P[h], _both(vp_ref, vc_ref, sl)) / L[h]

    o, lse = pl.pallas_call(
        body, name=f"attn_fwd_g{g}", grid=(dil, nb),
        in_specs=[qs, kp, kc, vp, vc, bsp], out_specs=[wide, narrow],
        out_shape=[jax.ShapeDtypeStruct((Tv, dil * 1024), F32), jax.ShapeDtypeStruct((Tv, dil * 128), F32)],
        compiler_params=_params("parallel", "parallel"),
    )(qv, kvw, kvw, vvw, vvw, bias)
    return o.reshape(T, 1024), lse.reshape(T, 128)


def attn_bwd(qd, kv, bias, datt, lse, delta, g):
    T = qd.shape[0]
    dil = DILATIONS[g]
    Tv = T // dil
    nb = Tv // B_BLOCK
    (qv, kvw, vvw), cols = _group_views(qd, kv, g)
    qs, kp, kc, vp, vc, bsp, wide, narrow = _attn_specs(cols)

    def body(q_ref, kp_ref, kc_ref, vp_ref, vc_ref, b_ref, bt_ref, do_ref, lse_ref, dl_ref,
             dq_ref, dkc_ref, dkp_ref, dvc_ref, dvp_ref, db_ref):
        @pl.when(jnp.logical_and(pl.program_id(0) == 0, pl.program_id(1) == 0))
        def _():
            db_ref[...] = jnp.zeros_like(db_ref)

        n = pl.program_id(1)
        valid = _band_mask(n)
        ki = lax.broadcasted_iota(jnp.int32, (2 * B_BLOCK, B_BLOCK), 0)
        qi = lax.broadcasted_iota(jnp.int32, (2 * B_BLOCK, B_BLOCK), 1)
        valid_t = jnp.logical_and(jnp.logical_and(ki >= qi, ki <= qi + B_BLOCK), jnp.logical_or(ki >= B_BLOCK, n > 0))
        lse_t, dl_t = lse_ref[...].T, dl_ref[...].T
        heads = [slice(h * B_DH, (h + 1) * B_DH) for h in range(B_HEADS)]
        scale = B_DH ** -0.5
        PT, DS, DST = [], [], []
        for h, sl in enumerate(heads):
            qh, doh = q_ref[:, sl], do_ref[:, sl].astype(MM_DTYPE)
            kh, vh = _both(kp_ref, kc_ref, sl), _both(vp_ref, vc_ref, sl)
            p = jnp.exp(_scores(qh, kh, b_ref[h], valid) - lse_ref[:, h:h + 1])
            ds = p * (_dot_nt(doh, vh) - dl_ref[:, h:h + 1])
            db_ref[h] += ds
            DS.append((ds * scale).astype(MM_DTYPE))
            pt = jnp.exp(_scores(kh, qh, bt_ref[h], valid_t) - lse_t[h:h + 1, :])
            PT.append(pt.astype(MM_DTYPE))
            DST.append((pt * (_dot_nt(vh, doh) - dl_t[h:h + 1, :]) * scale).astype(MM_DTYPE))
        for h, sl in enumerate(heads):
            qh, doh = q_ref[:, sl], do_ref[:, sl].astype(MM_DTYPE)
            dq_ref[:, sl] = _dot(DS[h], _both(kp_ref, kc_ref, sl)).astype(MM_DTYPE)
            dk = _dot(DST[h], qh).astype(MM_DTYPE)
            dv = _dot(PT[h], doh).astype(MM_DTYPE)
            dkp_ref[:, sl], dkc_ref[:, sl] = dk[:B_BLOCK], dk[B_BLOCK:]
            dvp_ref[:, sl], dvc_ref[:, sl] = dv[:B_BLOCK], dv[B_BLOCK:]

    big = jax.ShapeDtypeStruct((Tv, dil * 1024), MM_DTYPE)
    bsp_t = pl.BlockSpec((B_HEADS, 2 * B_BLOCK, B_BLOCK), lambda r, n: (0, 0, 0))
    outs = pl.pallas_call(
        body, name=f"attn_bwd_g{g}", grid=(dil, nb),
        in_specs=[qs, kp, kc, vp, vc, bsp, bsp_t, wide, narrow, narrow],
        out_specs=[wide] * 5 + [bsp],
        out_shape=[big] * 5 + [jax.ShapeDtypeStruct((B_HEADS, B_BLOCK, 2 * B_BLOCK), F32)],
        compiler_params=_params("arbitrary", "arbitrary"),
    )(qv, kvw, kvw, vvw, vvw, bias, jnp.swapaxes(bias, 1, 2),
      datt.reshape(Tv, dil * 1024), lse.reshape(Tv, dil * 128), delta.reshape(Tv, dil * 128))
    return [o.reshape(T, 1024) for o in outs[:5]] + [outs[5]]


def _head_expand():
    e = np.zeros((128, 1024), np.float32)
    for h in range(B_HEADS):
        e[h, h * B_DH:(h + 1) * B_DH] = 1.0
    return e


def attn_merge(os_, lses):
    T = os_[0].shape[0]
    tt = _rows(T, 256)
    expand = jnp.asarray(_head_expand())

    def body(o0, o1, o2, l0, l1, l2, e_ref, ob_ref, of_ref, lse_ref):
        ls = [l0[...], l1[...], l2[...]]
        m = jnp.maximum(jnp.maximum(ls[0], ls[1]), ls[2])
        ex = [jnp.exp(l - m) for l in ls]
        tot = ex[0] + ex[1] + ex[2]
        lse_ref[...] = m + jnp.log(tot)
        out = jnp.zeros((tt, 1024), F32)
        for e, o in zip(ex, (o0, o1, o2)):
            w = jnp.dot(e / tot, e_ref[...], precision=HI, preferred_element_type=F32)
            out = out + w * o[...]
        of_ref[...] = out
        ob_ref[...] = out.astype(ob_ref.dtype)

    wide = pl.BlockSpec((tt, 1024), lambda i: (i, 0))
    narrow = pl.BlockSpec((tt, 128), lambda i: (i, 0))
    return pl.pallas_call(
        body, name="attn_merge", grid=(T // tt,),
        in_specs=[wide] * 3 + [narrow] * 3 + [pl.BlockSpec((128, 1024), lambda i: (0, 0))],
        out_specs=[wide, wide, narrow],
        out_shape=[jax.ShapeDtypeStruct((T, 1024), MM_DTYPE), jax.ShapeDtypeStruct((T, 1024), F32),
                   jax.ShapeDtypeStruct((T, 128), F32)],
        compiler_params=_params("parallel"),
    )(*os_, *lses, expand)


def attn_delta(datt, out):
    T = datt.shape[0]
    tt = _rows(T, 512)
    expand_t = jnp.asarray(_head_expand().T.copy())

    def body(d_ref, o_ref, e_ref, dl_ref):
        dl_ref[...] = jnp.dot(d_ref[...] * o_ref[...], e_ref[...], precision=HI, preferred_element_type=F32)

    wide = pl.BlockSpec((tt, 1024), lambda i: (i, 0))
    return pl.pallas_call(
        body, name="attn_delta", grid=(T // tt,),
        in_specs=[wide, wide, pl.BlockSpec((1024, 128), lambda i: (0, 0))],
        out_specs=pl.BlockSpec((tt, 128), lambda i: (i, 0)),
        out_shape=jax.ShapeDtypeStruct((T, 128), F32),
        compiler_params=_params("parallel"),
    )(datt, out, expand_t)


def attn_combine(dqs, dkc, dkp, dvc, dvp):
    T = dqs[0].shape[0]
    tt = B_BLOCK
    nt = T // tt

    def body(*refs):
        i = pl.program_id(0)
        dq_refs, rest = refs[:3], refs[3:]
        c_refs, p_refs = rest[:6], rest[6:12]
        dq_ref, dkv_ref = rest[12], rest[13]
        for g in range(N_GROUPS):
            dq_ref[:, g * 1024:(g + 1) * 1024] = dq_refs[g][...].astype(dq_ref.dtype)
        for j in range(6):
            live = i + DILATIONS[j % 3] < nt
            tot = c_refs[j][...].astype(F32) + jnp.where(live, p_refs[j][...].astype(F32), 0.0)
            dkv_ref[:, j * 1024:(j + 1) * 1024] = tot.astype(dkv_ref.dtype)

    cur = pl.BlockSpec((tt, 1024), lambda i: (i, 0))

    def nxt(d):
        return pl.BlockSpec((tt, 1024), lambda i: (jnp.minimum(i + d, nt - 1), 0))

    return pl.pallas_call(
        body, name="attn_combine", grid=(nt,),
        in_specs=[cur] * 3 + [cur] * 6 + [nxt(DILATIONS[j % 3]) for j in range(6)],
        out_specs=[pl.BlockSpec((tt, 3072), lambda i: (i, 0)), pl.BlockSpec((tt, 6144), lambda i: (i, 0))],
        out_shape=[jax.ShapeDtypeStruct((T, 3072), MM_DTYPE), jax.ShapeDtypeStruct((T, 6144), MM_DTYPE)],
        compiler_params=_params("parallel"),
    )(*dqs, *dkc, *dvc, *dkp, *dvp)


def adamw(w, g, m, v, name):
    R, C = w.shape
    tr = R if R * C * 4 <= (1 << 20) else _rows(R, max(8, ((1 << 20) // (C * 4)) // 8 * 8))

    def body(w_ref, g_ref, m_ref, v_ref, d_ref, nm_ref, nv_ref):
        gg = g_ref[...]
        nm = ADAM_B1 * m_ref[...] + (1.0 - ADAM_B1) * gg
        nv = ADAM_B2 * v_ref[...] + (1.0 - ADAM_B2) * (gg * gg)
        m_hat = nm / (1.0 - ADAM_B1 ** ADAM_STEP)
        v_hat = nv / (1.0 - ADAM_B2 ** ADAM_STEP)
        d_ref[...] = -ADAM_LR * (m_hat / (jnp.sqrt(v_hat) + ADAM_EPS) + ADAM_WD * w_ref[...])
        nm_ref[...] = nm
        nv_ref[...] = nv

    blk = pl.BlockSpec((tr, C), lambda i: (i, 0))
    sds = jax.ShapeDtypeStruct((R, C), F32)
    return pl.pallas_call(
        body, name=name, grid=(R // tr,), in_specs=[blk] * 4, out_specs=[blk] * 3, out_shape=[sds] * 3,
        compiler_params=_params("parallel"),
    )(w, g, m, v)


def sum_slots(x, name, out_dtype=F32):
    n, R, C = x.shape
    tr = _rows(R, 256)

    def body(x_ref, o_ref):
        acc = x_ref[0].astype(F32)
        for s in range(1, n):
            acc = acc + x_ref[s].astype(F32)
        o_ref[...] = acc.astype(out_dtype)

    return pl.pallas_call(
        body, name=name, grid=(R // tr,),
        in_specs=[pl.BlockSpec((n, tr, C), lambda i: (0, i, 0))],
        out_specs=pl.BlockSpec((tr, C), lambda i: (i, 0)),
        out_shape=jax.ShapeDtypeStruct((R, C), out_dtype),
        compiler_params=_params("parallel"),
    )(x)


_ANY = pl.BlockSpec(memory_space=pl.ANY)
GROUP_ALL = ([(0, 0, 1), (0, 1, 0), (0, 1, 1), (1, 0, 0), (1, 0, 1), (1, 1, 0), (1, 1, 1)],
             lambda d: 4 * d[0] + 2 * d[1] + d[2])
GROUP_CHIPS = ([(0, 1, 0), (1, 0, 0), (1, 1, 0)], lambda d: 2 * d[0] + d[1])
GROUP_SIBLING = ([(0, 0, 1)], lambda d: d[2])


def _me():
    return lax.axis_index("x"), lax.axis_index("y"), lax.axis_index("c")


def _peer(me, flip):
    return tuple(1 - a if f else a for a, f in zip(me, flip))


def _group_exchange(x, name, group, scatter, chunks=1):
    flips, slot = group
    n = len(flips) + 1
    R, C = x.shape[-2:]
    nc = max([k for k in range(1, chunks + 1) if R % (16 * k) == 0] or [1])
    rc = R // nc

    def body(x_ref, o_ref, send_sems, recv_sems, local_sem):
        me = _me()
        mine = pltpu.make_async_copy(x_ref.at[slot(me)] if scatter else x_ref, o_ref.at[slot(me)], local_sem)
        mine.start()
        sends = []
        for k, flip in enumerate(flips):
            peer = _peer(me, flip)
            src = x_ref.at[slot(peer)] if scatter else x_ref
            for j in range(nc):
                part = pl.ds(j * rc, rc)
                cp = pltpu.make_async_remote_copy(
                    src_ref=src.at[part], dst_ref=o_ref.at[slot(me), part], send_sem=send_sems.at[k * nc + j],
                    recv_sem=recv_sems.at[k * nc + j], device_id=peer, device_id_type=MESH_ID)
                cp.start()
                sends.append(cp)
        for k, flip in enumerate(flips):
            peer = _peer(me, flip)
            for j in range(nc):
                part = pl.ds(j * rc, rc)
                pltpu.make_async_remote_copy(
                    src_ref=o_ref.at[slot(me), part], dst_ref=o_ref.at[slot(peer), part],
                    send_sem=send_sems.at[k * nc + j], recv_sem=recv_sems.at[k * nc + j],
                    device_id=peer, device_id_type=MESH_ID).wait_recv()
        for cp in sends:
            cp.wait_send()
        mine.wait()

    return pl.pallas_call(
        body, name=name, in_specs=[_ANY], out_specs=_ANY,
        out_shape=jax.ShapeDtypeStruct((n, R, C), x.dtype),
        scratch_shapes=[pltpu.SemaphoreType.DMA(((n - 1) * nc,)), pltpu.SemaphoreType.DMA(((n - 1) * nc,)),
                        pltpu.SemaphoreType.DMA],
    )(x)


def group_gather(x, name, group, chunks=1):
    return _group_exchange(x, name, group, scatter=False, chunks=chunks)


def group_scatter(x, name, group, chunks=1):
    return _group_exchange(x, name, group, scatter=True, chunks=chunks)


WEIGHTS = ['a_norm_g', 'a_w_in', 'a_b_if', 'a_hnorm_g', 'a_w_out', 'kv_norm_g', 'w_kv', 'b_norm_g', 'b_w_q',
           'b_w_out', 'rel_bias', 'f_norm_g', 'f_w_up', 'f_conv_w', 'f_conv_b', 'f_w_down', 'final_norm_g']
SHARD_AXIS = {'a_norm_g': 1, 'a_w_in': 2, 'a_b_if': None, 'a_hnorm_g': 2, 'a_w_out': 1, 'kv_norm_g': None,
              'w_kv': 1, 'b_norm_g': None, 'b_w_q': 2, 'b_w_out': 1, 'rel_bias': None, 'f_norm_g': None,
              'f_w_up': 2, 'f_conv_w': 2, 'f_conv_b': None, 'f_w_down': 1, 'final_norm_g': None}
BIG = ['a_w_in', 'a_w_out', 'w_kv', 'b_w_q', 'b_w_out', 'f_w_up', 'f_w_down']
SMALL = [n for n in WEIGHTS if n not in BIG]
LANES = 1024


def _pad_rows(flat, mult):
    n = flat.shape[0]
    per = LANES * mult
    tot = -(-n // per) * per
    return jnp.pad(flat, (0, tot - n)).reshape(tot // LANES, LANES)


def _full_from_shards(sh, axis):
    return jnp.concatenate([sh[j] for j in range(4)], axis=axis)


def _shards_from_full(full, axis):
    return jnp.stack(jnp.split(full, 4, axis=axis))


def _local_step(x, target, W):
    T = x.shape[0]
    row = lambda a: a.reshape(1, -1).astype(F32)
    w_in = jnp.pad(W['a_w_in'][0], ((0, 0), (0, A_IN_PAD - A_IN)))
    bias128 = jnp.pad(row(W['a_b_if'][0]), ((0, 0), (0, 120)))
    hng = row(W['a_hnorm_g'][0])
    w_up = [_interleave(W['f_w_up'][l]) for l in range(2)]
    cw = [_interleave(W['f_conv_w'][l].astype(F32)) for l in range(2)]
    cb = [_interleave(row(W['f_conv_b'][l])) for l in range(2)]
    onehots = [(jnp.asarray(_group_bucket(g).reshape(-1, 1)) == jnp.arange(128)[None, :]).astype(F32)
               for g in range(N_GROUPS)]
    rb_t = jnp.pad(W['rel_bias'].astype(F32).T, ((0, 0), (0, 128 - REL_BUCKETS)))
    biases = [mm_nn(rb_t[g * B_HEADS:(g + 1) * B_HEADS], onehots[g].T, f"rel_bias_table_g{g}", exact=True)
              .reshape(B_HEADS, B_BLOCK, 2 * B_BLOCK) for g in range(N_GROUPS)]
    G = {}

    def ffn_fwd(xin, l):
        xn, = rms_fwd(xin, [row(W['f_norm_g'][l])], f"ffn{l}_norm")
        u = mm_nn(xn, w_up[l], f"ffn{l}_up")
        act = conv_act_fwd(u, cw[l], cb[l], f"ffn{l}_act")
        return mm_nn(act, W['f_w_down'][l], f"ffn{l}_down", res=xin), (xn, u, act)

    def ffn_bwd(xin, saved, dout, l):
        xn, u, act = saved
        dact = mm_nn(dout, W['f_w_down'][l].T, f"ffn{l}_ddown")
        gd = mm_tn(act, dout, f"ffn{l}_gdown")
        du, gcw, gcb = conv_act_bwd(u, dact, cw[l], cb[l], f"ffn{l}_dact")
        dxn = mm_nn(du, w_up[l].T, f"ffn{l}_dup")
        gu = _deinterleave(mm_tn(xn, du, f"ffn{l}_gup"))
        dxin, (gn,) = rms_bwd(xin, dout, [(dxn, row(W['f_norm_g'][l]))], f"ffn{l}_dnorm")
        return dxin, gd, gu, _deinterleave(gcw), _deinterleave(gcb), gn

    xn_a, = rms_fwd(x, [row(W['a_norm_g'][0])], "a_norm")
    z = mm_nn(xn_a, w_in, "a_in")
    gcol, grow = gate_prep(z, bias128)
    hg, Cs, ns, ms = mlstm_fwd(z, gcol, grow, hng)
    x1 = mm_nn(hg, W['a_w_out'][0], "a_out", res=x)
    x2, ffn0 = ffn_fwd(x1, 0)
    xn_kv, xn_b = rms_fwd(x2, [row(W['kv_norm_g']), row(W['b_norm_g'][0])], "b_norms")
    kv = mm_nn(xn_kv, W['w_kv'], "kv_proj", out_dtype=MM_DTYPE)
    qd = mm_nn(xn_b, W['b_w_q'][0], "q_proj", out_dtype=MM_DTYPE)
    os_, lses = zip(*[attn_fwd(qd, kv, biases[g], g) for g in range(N_GROUPS)])
    att, att_f, lse = attn_merge(os_, lses)
    x3 = mm_nn(att, W['b_w_out'][0], "b_out", res=x2)
    x4, ffn1 = ffn_fwd(x3, 1)
    dx4, g_final, loss = loss_head(x4, target, row(W['final_norm_g']))
    G['final_norm_g'] = g_final.reshape(-1)

    dx3, gd1, gu1, gcw1, gcb1, gn1 = ffn_bwd(x3, ffn1, dx4, 1)
    datt = mm_nn(dx3, W['b_w_out'][0].T, "b_dout")
    G['b_w_out'] = mm_tn(att, dx3, "b_gout")[None]
    delta = attn_delta(datt, att_f)
    parts = [attn_bwd(qd, kv, biases[g], datt, lse, delta, g) for g in range(N_GROUPS)]
    dq_all, dkv = attn_combine([p[0] for p in parts], [p[1] for p in parts], [p[2] for p in parts],
                               [p[3] for p in parts], [p[4] for p in parts])
    grb = []
    for g in range(N_GROUPS):
        gb = mm_nn(parts[g][5].reshape(B_HEADS, -1), onehots[g], f"rel_bias_g{g}", exact=True)
        grb.append(gb[:, :REL_BUCKETS].T)
    G['rel_bias'] = jnp.concatenate(grb, axis=1)
    dxn_b = mm_nn(dq_all, W['b_w_q'][0].T, "q_dproj")
    G['b_w_q'] = mm_tn(xn_b, dq_all, "q_gproj")[None]
    dxn_kv = mm_nn(dkv, W['w_kv'].T, "kv_dproj")
    G['w_kv'] = mm_tn(xn_kv, dkv, "kv_gproj")
    dx2, (g_kvn, g_bn) = rms_bwd(x2, dx3, [(dxn_kv, row(W['kv_norm_g'])), (dxn_b, row(W['b_norm_g'][0]))],
                                 "b_dnorms")
    G['kv_norm_g'] = g_kvn.reshape(-1)
    G['b_norm_g'] = g_bn
    dx1, gd0, gu0, gcw0, gcb0, gn0 = ffn_bwd(x1, ffn0, dx2, 0)
    G['f_w_down'] = jnp.stack([gd0, gd1])
    G['f_w_up'] = jnp.stack([gu0, gu1])
    G['f_conv_w'] = jnp.stack([gcw0, gcw1])
    G['f_conv_b'] = jnp.concatenate([gcb0, gcb1], axis=0)
    G['f_norm_g'] = jnp.concatenate([gn0, gn1], axis=0)
    dhg = mm_nn(dx1, W['a_w_out'][0].T, "a_dout")
    G['a_w_out'] = mm_tn(hg, dx1, "a_gout")[None]
    dz, g_hn, g_bif = mlstm_bwd(z, gcol, grow, hng, bias128, Cs, ns, ms, dhg)
    G['a_hnorm_g'] = g_hn.reshape(1, A_HEADS, A_V)
    G['a_b_if'] = g_bif[:, :2 * A_HEADS]
    dxn_a = mm_nn(dz, w_in.T, "a_din")
    G['a_w_in'] = mm_tn(xn_a, dz, "a_gin")[:, :A_IN][None]
    grad_x, (g_an,) = rms_bwd(x, dx1, [(dxn_a, row(W['a_norm_g'][0]))], "a_dnorm")
    G['a_norm_g'] = g_an
    return loss, grad_x, G


def kernel(x, a_norm_g, a_w_in, a_b_if, a_hnorm_g, a_w_out, kv_norm_g, w_kv, b_norm_g, b_w_q, b_w_out, rel_bias, f_norm_g, f_w_up, f_conv_w, f_conv_b, f_w_down, final_norm_g, loss_target, m_a_norm_g, m_a_w_in, m_a_b_if, m_a_hnorm_g, m_a_w_out, m_kv_norm_g, m_w_kv, m_b_norm_g, m_b_w_q, m_b_w_out, m_rel_bias, m_f_norm_g, m_f_w_up, m_f_conv_w, m_f_conv_b, m_f_w_down, m_final_norm_g, v_a_norm_g, v_a_w_in, v_a_b_if, v_a_hnorm_g, v_a_w_out, v_kv_norm_g, v_w_kv, v_b_norm_g, v_b_w_q, v_b_w_out, v_rel_bias, v_f_norm_g, v_f_w_up, v_f_conv_w, v_f_conv_b, v_f_w_down, v_final_norm_g):
    given = dict(locals())
    shard = {n: given[n] for n in WEIGHTS}
    mom = {n: given["m_" + n] for n in WEIGHTS}
    var = {n: given["v_" + n] for n in WEIGHTS}
    cx, cy, cc = _me()
    chip = 2 * cx + cy

    halves = [shard[n].astype(MM_DTYPE).reshape(2, -1, LANES) for n in BIG]
    sizes = [h.shape[1] for h in halves]
    mine = lax.dynamic_index_in_dim(jnp.concatenate(halves, axis=1), cc, axis=0, keepdims=False)
    fill = -mine.shape[0] % 16
    mine = jnp.pad(mine, ((0, fill), (0, 0)))
    rows = mine.shape[0]
    gathered = group_gather(mine, "gather_weights", GROUP_ALL).reshape(4, 2, rows, LANES)
    W = {}
    off = 0
    for n, sz in zip(BIG, sizes):
        W[n] = _full_from_shards(gathered[:, :, off:off + sz].reshape((4,) + shard[n].shape), SHARD_AXIS[n])
        off += sz
    sharded_small = [n for n in SMALL if SHARD_AXIS[n] is not None]
    ssz = [shard[n].size for n in sharded_small]
    sflat = jnp.concatenate([shard[n].reshape(-1) for n in sharded_small])
    sg = group_gather(_pad_rows(sflat, 8), "gather_small", GROUP_CHIPS).reshape(4, -1)
    off = 0
    for n, sz in zip(sharded_small, ssz):
        W[n] = _full_from_shards(sg[:, off:off + sz].reshape((4,) + shard[n].shape), SHARD_AXIS[n])
        off += sz
    for n in SMALL:
        if SHARD_AXIS[n] is None:
            W[n] = shard[n]

    loss_row, grad_x, G = _local_step(x[0], loss_target[0], W)

    slots = jnp.concatenate(
        [_shards_from_full(G[n], SHARD_AXIS[n]).reshape(4, 2, -1, LANES).astype(GRAD_WIRE_DTYPE) for n in BIG]
        + [jnp.zeros((4, 2, fill, LANES), GRAD_WIRE_DTYPE)], axis=2).reshape(8, rows, LANES)
    reduced = sum_slots(group_scatter(slots, "scatter_grads", GROUP_ALL), "sum_grads", GRAD_WIRE_DTYPE)
    both = group_gather(reduced, "join_halves", GROUP_SIBLING)
    gsh = {}
    off = 0
    for n, sz in zip(BIG, sizes):
        gsh[n] = both[:, off:off + sz].reshape(shard[n].shape).astype(F32)
        off += sz
    small_parts = [loss_row[0, 0:1]] + [G[n].reshape(-1) for n in SMALL]
    small_sz = [p.shape[0] for p in small_parts]
    small = sum_slots(group_gather(_pad_rows(jnp.concatenate(small_parts), 8), "gather_small_grads", GROUP_ALL),
                      "sum_small_grads").reshape(-1)
    loss = small[0]
    off = 1
    for n, sz in zip(SMALL, small_sz[1:]):
        full = small[off:off + sz].reshape(W[n].shape)
        off += sz
        if SHARD_AXIS[n] is None:
            gsh[n] = full
        else:
            gsh[n] = lax.dynamic_index_in_dim(_shards_from_full(full, SHARD_AXIS[n]), chip, 0, keepdims=False)

    delta, new_m, new_v = {}, {}, {}
    for n in WEIGHTS:
        shp = shard[n].shape
        two = lambda a: a.reshape(-1, shp[-1])
        d, nm, nv = adamw(two(shard[n]), two(gsh[n]), two(mom[n]), two(var[n]), f"adamw_{n}")
        delta[n], new_m[n], new_v[n] = d.reshape(shp), nm.reshape(shp), nv.reshape(shp)
    return (loss, grad_x[None], *[gsh[n] for n in WEIGHTS], *[delta[n] for n in WEIGHTS],
            *[new_m[n] for n in WEIGHTS], *[new_v[n] for n in WEIGHTS])
```

```python
import functools
import math

import numpy as np
import jax
import jax.numpy as jnp
from jax import lax
from jax.experimental import pallas as pl
from jax.experimental.pallas import tpu as pltpu

F32 = jnp.float32
BF16 = jnp.bfloat16
MM_DTYPE = jnp.bfloat16
GRAD_WIRE_DTYPE = jnp.bfloat16
HI = lax.Precision.HIGHEST

D_MODEL = 1024
A_HEADS = 4
A_QK = 128
A_V = 256
A_CHUNK = 64
A_IN = 3080
A_IN_PAD = 3200
GATE_COL = 3072
SOFTCAP = 15.0
N_GROUPS = 3
B_HEADS = 16
B_DH = 64
B_BLOCK = 128
DILATIONS = (1, 4, 16)
WINDOWS = (128, 512, 2048)
REL_BUCKETS = 32
REL_MAX_DIST = 2048
D_FF = 2816
FF_TC = 256
EPS = 1e-6
ADAM_LR, ADAM_B1, ADAM_B2, ADAM_EPS, ADAM_WD, ADAM_STEP = 0.001, 0.9, 0.999, 1e-08, 0.01, 10

VMEM_LIMIT = 56 * 1024 * 1024
NT_DIMS = (((1,), (1,)), ((), ()))
TN_DIMS = (((0,), (0,)), ((), ()))
MESH_ID = pl.DeviceIdType.MESH


def _params(*sem):
    return pltpu.CompilerParams(dimension_semantics=sem, vmem_limit_bytes=VMEM_LIMIT)


def _tile(n, cap):
    if n <= cap:
        return n
    best = None
    for t in range(128, cap + 1, 128):
        if n % t == 0:
            best = t
    assert best is not None, (n, cap)
    return best


def _rows(n, cap):
    if n <= cap:
        return n
    for t in range(cap // 8 * 8, 7, -8):
        if n % t == 0:
            return t
    raise ValueError((n, cap))


def _dot(a, b):
    return jnp.dot(a.astype(MM_DTYPE), b.astype(MM_DTYPE), preferred_element_type=F32)


def _dot_nt(a, b):
    return lax.dot_general(a.astype(MM_DTYPE), b.astype(MM_DTYPE), NT_DIMS, preferred_element_type=F32)


def _dot_tn(a, b):
    return lax.dot_general(a.astype(MM_DTYPE), b.astype(MM_DTYPE), TN_DIMS, preferred_element_type=F32)


def _sigmoid(x):
    return 1.0 / (1.0 + jnp.exp(-x))


def _sigmoid_tanh(x):
    return 0.5 * jnp.tanh(0.5 * x) + 0.5


def mm_nn(a, b, name, res=None, out_dtype=F32, exact=False):
    M, K = a.shape
    N = b.shape[1]
    tm, tn = _rows(M, 512), _tile(N, 1536)
    whole_k = (2 * (tm * K * a.dtype.itemsize + K * tn * b.dtype.itemsize)
               + 2 * tm * tn * 4 * (1 if res is None else 2))
    tk = K if whole_k <= 46 * 1024 * 1024 else _tile(K, 1536)
    nk = K // tk

    def body(*refs):
        if res is None:
            a_ref, b_ref, o_ref, acc = refs
            r_ref = None
        else:
            a_ref, b_ref, r_ref, o_ref, acc = refs
        if exact:
            p = jnp.dot(a_ref[...], b_ref[...], precision=HI, preferred_element_type=F32)
        else:
            p = _dot(a_ref[...], b_ref[...])

        def finish(total):
            if r_ref is not None:
                total = total + r_ref[...]
            o_ref[...] = total.astype(out_dtype)

        if nk == 1:
            finish(p)
        else:
            k = pl.program_id(2)

            @pl.when(k == 0)
            def _():
                acc[...] = p

            @pl.when(jnp.logical_and(k > 0, k < nk - 1))
            def _():
                acc[...] += p

            @pl.when(k == nk - 1)
            def _():
                finish(acc[...] + p)

    in_specs = [pl.BlockSpec((tm, tk), lambda j, i, k: (i, k)),
                pl.BlockSpec((tk, tn), lambda j, i, k: (k, j))]
    args = [a, b]
    if res is not None:
        in_specs.append(pl.BlockSpec((tm, tn), lambda j, i, k: (i, j)))
        args.append(res)
    acc_shape = (tm, tn) if nk > 1 else (8, 128)
    return pl.pallas_call(
        body, name=name, grid=(N // tn, M // tm, nk),
        in_specs=in_specs, out_specs=pl.BlockSpec((tm, tn), lambda j, i, k: (i, j)),
        out_shape=jax.ShapeDtypeStruct((M, N), out_dtype),
        scratch_shapes=[pltpu.VMEM(acc_shape, F32)],
        compiler_params=_params("parallel", "parallel", "arbitrary"),
    )(*args)


def mm_view(a, b, name, dil):
    T, K = a.shape
    tm = 512

    def body(a_ref, b_ref, o_ref, sc):
        p = _dot(a_ref[...], b_ref[...])
        if dil == 1:
            o_ref[...] = p.astype(o_ref.dtype)
        else:
            _to_view(lambda c: p[:, c * 128:(c + 1) * 128], sc, o_ref, dil, 8, tm)

    return pl.pallas_call(
        body, name=name, grid=(T // tm,),
        in_specs=[pl.BlockSpec((tm, K), lambda i: (i, 0)), pl.BlockSpec((K, 1024), lambda i: (0, 0))],
        out_specs=pl.BlockSpec((tm // dil, dil * 1024), lambda i: (i, 0)),
        out_shape=jax.ShapeDtypeStruct((T // dil, dil * 1024), MM_DTYPE),
        scratch_shapes=[pltpu.VMEM((8, tm, 128), F32)],
        compiler_params=_params("parallel"),
    )(a, b)


def mm_tn(a, g, name):
    T, Ka = a.shape
    N = g.shape[1]
    tka, tn, tt = _tile(Ka, 1536), _tile(N, 1536), _rows(T, 1024)
    nt = T // tt

    def body(a_ref, g_ref, o_ref):
        t = pl.program_id(2)
        p = _dot_tn(a_ref[...], g_ref[...])

        @pl.when(t == 0)
        def _():
            o_ref[...] = p

        @pl.when(t > 0)
        def _():
            o_ref[...] += p

    return pl.pallas_call(
        body, name=name, grid=(Ka // tka, N // tn, nt),
        in_specs=[pl.BlockSpec((tt, tka), lambda i, j, t: (t, i)),
                  pl.BlockSpec((tt, tn), lambda i, j, t: (t, j))],
        out_specs=pl.BlockSpec((tka, tn), lambda i, j, t: (i, j)),
        out_shape=jax.ShapeDtypeStruct((Ka, N), F32),
        compiler_params=_params("parallel", "parallel", "arbitrary"),
    )(a, g)


def rms_fwd(x, gains, name):
    T, D = x.shape
    tt = _rows(T, 512)
    ng = len(gains)

    def body(*refs):
        x_ref = refs[0]
        g_refs = refs[1:1 + ng]
        o_refs = refs[1 + ng:]
        xf = x_ref[...]
        y = xf * lax.rsqrt(jnp.mean(xf * xf, axis=-1, keepdims=True) + EPS)
        for g_ref, o_ref in zip(g_refs, o_refs):
            o_ref[...] = (y * g_ref[...]).astype(o_ref.dtype)

    row = pl.BlockSpec((tt, D), lambda i: (i, 0))
    gsp = pl.BlockSpec((1, D), lambda i: (0, 0))
    return pl.pallas_call(
        body, name=name, grid=(T // tt,),
        in_specs=[row] + [gsp] * ng, out_specs=[row] * ng,
        out_shape=[jax.ShapeDtypeStruct((T, D), MM_DTYPE)] * ng,
        compiler_params=_params("parallel"),
    )(x, *gains)


def rms_bwd(x, dres, branches, name):
    T, D = x.shape
    tt = _rows(T, 256)
    nb = len(branches)

    def body(*refs):
        x_ref, r_ref = refs[0], refs[1]
        dy_refs = refs[2:2 + nb]
        g_refs = refs[2 + nb:2 + 2 * nb]
        dx_ref = refs[2 + 2 * nb]
        dg_refs = refs[3 + 2 * nb:]
        i = pl.program_id(0)
        xf = x_ref[...]
        r = lax.rsqrt(jnp.mean(xf * xf, axis=-1, keepdims=True) + EPS)
        xh = xf * r
        dx = r_ref[...]
        for dy_ref, g_ref, dg_ref in zip(dy_refs, g_refs, dg_refs):
            dy = dy_ref[...].astype(F32)
            dyg = dy * g_ref[...]
            dx = dx + r * (dyg - xh * jnp.mean(dyg * xh, axis=-1, keepdims=True))
            part = jnp.sum(dy * xh, axis=0, keepdims=True)

            @pl.when(i == 0)
            def _():
                dg_ref[...] = part

            @pl.when(i > 0)
            def _():
                dg_ref[...] += part
        dx_ref[...] = dx

    row = pl.BlockSpec((tt, D), lambda i: (i, 0))
    gsp = pl.BlockSpec((1, D), lambda i: (0, 0))
    outs = pl.pallas_call(
        body, name=name, grid=(T // tt,),
        in_specs=[row, row] + [row] * nb + [gsp] * nb,
        out_specs=[row] + [gsp] * nb,
        out_shape=[jax.ShapeDtypeStruct((T, D), F32)] + [jax.ShapeDtypeStruct((1, D), F32)] * nb,
        compiler_params=_params("arbitrary"),
    )(x, dres, *[b[0] for b in branches], *[b[1] for b in branches])
    return outs[0], outs[1:]


def loss_head(x, target, gain):
    T, D = x.shape
    tt = _rows(T, 256)

    def body(x_ref, t_ref, g_ref, dx_ref, dg_ref, loss_ref):
        i = pl.program_id(0)
        xf = x_ref[...]
        g = g_ref[...]
        r = lax.rsqrt(jnp.mean(xf * xf, axis=-1, keepdims=True) + EPS)
        xh = xf * r
        e = xh * g - t_ref[...]
        lpart = 0.5 * jnp.sum(jnp.sum(e * e, axis=1, keepdims=True), axis=0, keepdims=True) / D
        dy = e / D
        dyg = dy * g
        dx_ref[...] = r * (dyg - xh * jnp.mean(dyg * xh, axis=-1, keepdims=True))
        gpart = jnp.sum(dy * xh, axis=0, keepdims=True)
        lrow = jnp.broadcast_to(lpart, (1, 128))

        @pl.when(i == 0)
        def _():
            dg_ref[...] = gpart
            loss_ref[...] = lrow

        @pl.when(i > 0)
        def _():
            dg_ref[...] += gpart
            loss_ref[...] += lrow

    row = pl.BlockSpec((tt, D), lambda i: (i, 0))
    gsp = pl.BlockSpec((1, D), lambda i: (0, 0))
    return pl.pallas_call(
        body, name="loss_head", grid=(T // tt,),
        in_specs=[row, row, gsp],
        out_specs=[row, gsp, pl.BlockSpec((1, 128), lambda i: (0, 0))],
        out_shape=[jax.ShapeDtypeStruct((T, D), F32), jax.ShapeDtypeStruct((1, D), F32),
                   jax.ShapeDtypeStruct((1, 128), F32)],
        compiler_params=_params("arbitrary"),
    )(x, target, gain)


def _shift_down(u, prev8, first, k):
    rolled = pltpu.roll(u, k, 0)
    rid = lax.broadcasted_iota(jnp.int32, u.shape, 0)
    halo = jnp.where(first, 0.0, prev8)
    out = rolled
    for j in range(k):
        out = jnp.where(rid == j, halo[8 - k + j:8 - k + j + 1, :], out)
    return out


def _conv3(u, prev8, first, w, b):
    return (_shift_down(u, prev8, first, 2) * w[0:1, :] + _shift_down(u, prev8, first, 1) * w[1:2, :]
            + u * w[2:3, :] + b)


def conv_act_fwd(u, w, b, name):
    T = u.shape[0]
    tt = _rows(T, 512)
    nj = D_FF // FF_TC

    def body(u_ref, p_ref, w_ref, b_ref, o_ref):
        first = pl.program_id(1) == 0
        c = _conv3(u_ref[...], p_ref[...], first, w_ref[...], b_ref[...])
        cg, cv = c[:, :FF_TC], c[:, FF_TC:]
        o_ref[...] = (cg * _sigmoid_tanh(cg) * cv).astype(o_ref.dtype)

    return pl.pallas_call(
        body, name=name, grid=(nj, T // tt),
        in_specs=[pl.BlockSpec((tt, 2 * FF_TC), lambda j, i: (i, j)),
                  pl.BlockSpec((8, 2 * FF_TC), lambda j, i: (jnp.maximum(i * (tt // 8) - 1, 0), j)),
                  pl.BlockSpec((3, 2 * FF_TC), lambda j, i: (0, j)),
                  pl.BlockSpec((1, 2 * FF_TC), lambda j, i: (0, j))],
        out_specs=pl.BlockSpec((tt, FF_TC), lambda j, i: (i, j)),
        out_shape=jax.ShapeDtypeStruct((T, D_FF), MM_DTYPE),
        compiler_params=_params("parallel", "parallel"),
    )(u, u, w, b)


def conv_act_bwd(u, da, w, b, name):
    T = u.shape[0]
    tt = _rows(T, 512)
    nt = T // tt
    nj = D_FF // FF_TC
    te = tt + 8

    def body(u_ref, p_ref, n_ref, da_ref, dan_ref, w_ref, b_ref, du_ref, dw_ref, db_ref):
        i = pl.program_id(1)
        first = i == 0
        last = i == nt - 1
        w = w_ref[...]
        ue = jnp.concatenate([u_ref[...], n_ref[...]], axis=0)
        dae = jnp.concatenate([da_ref[...], jnp.where(last, 0.0, dan_ref[...])], axis=0)
        um2 = _shift_down(ue, p_ref[...], first, 2)
        um1 = _shift_down(ue, p_ref[...], first, 1)
        c = um2 * w[0:1, :] + um1 * w[1:2, :] + ue * w[2:3, :] + b_ref[...]
        cg, cv = c[:, :FF_TC], c[:, FF_TC:]
        s = _sigmoid_tanh(cg)
        dcg = dae * cv * (s * (1.0 + cg * (1.0 - s)))
        dcv = dae * (cg * s)
        dc = jnp.concatenate([dcg, dcv], axis=1)
        du = (dc * w[2:3, :] + pltpu.roll(dc, te - 1, 0) * w[1:2, :] + pltpu.roll(dc, te - 2, 0) * w[0:1, :])
        du_ref[...] = du[:tt, :].astype(du_ref.dtype)
        dcm = dc[:tt, :]
        dwp = jnp.concatenate([jnp.sum(dcm * um2[:tt, :], axis=0, keepdims=True),
                               jnp.sum(dcm * um1[:tt, :], axis=0, keepdims=True),
                               jnp.sum(dcm * ue[:tt, :], axis=0, keepdims=True)], axis=0)
        dbp = jnp.sum(dcm, axis=0, keepdims=True)

        @pl.when(first)
        def _():
            dw_ref[...] = dwp
            db_ref[...] = dbp

        @pl.when(i > 0)
        def _():
            dw_ref[...] += dwp
            db_ref[...] += dbp

    nb8 = T // 8
    return pl.pallas_call(
        body, name=name, grid=(nj, nt),
        in_specs=[pl.BlockSpec((tt, 2 * FF_TC), lambda j, i: (i, j)),
                  pl.BlockSpec((8, 2 * FF_TC), lambda j, i: (jnp.maximum(i * (tt // 8) - 1, 0), j)),
                  pl.BlockSpec((8, 2 * FF_TC), lambda j, i: (jnp.minimum((i + 1) * (tt // 8), nb8 - 1), j)),
                  pl.BlockSpec((tt, FF_TC), lambda j, i: (i, j)),
                  pl.BlockSpec((8, FF_TC), lambda j, i: (jnp.minimum((i + 1) * (tt // 8), nb8 - 1), j)),
                  pl.BlockSpec((3, 2 * FF_TC), lambda j, i: (0, j)),
                  pl.BlockSpec((1, 2 * FF_TC), lambda j, i: (0, j))],
        out_specs=[pl.BlockSpec((tt, 2 * FF_TC), lambda j, i: (i, j)),
                   pl.BlockSpec((3, 2 * FF_TC), lambda j, i: (0, j)),
                   pl.BlockSpec((1, 2 * FF_TC), lambda j, i: (0, j))],
        out_shape=[jax.ShapeDtypeStruct((T, 2 * D_FF), MM_DTYPE),
                   jax.ShapeDtypeStruct((3, 2 * D_FF), F32),
                   jax.ShapeDtypeStruct((1, 2 * D_FF), F32)],
        compiler_params=_params("parallel", "arbitrary"),
    )(u, u, u, da, da, w, b)


def _interleave(a):
    lead = a.shape[:-1]
    nj = D_FF // FF_TC
    return jnp.swapaxes(a.reshape(*lead, 2, nj, FF_TC), -3, -2).reshape(*lead, 2 * D_FF)


def _deinterleave(a):
    lead = a.shape[:-1]
    nj = D_FF // FF_TC
    return jnp.swapaxes(a.reshape(*lead, nj, 2, FF_TC), -3, -2).reshape(*lead, 2 * D_FF)


A_GC = 2
A_TB = A_GC * A_CHUNK


def gate_prep(z, bias128):
    T = z.shape[0]
    tt = _rows(T, 512)

    def body(z_ref, b_ref, gc_ref, gr_ref):
        pre = z_ref[...] + b_ref[...]
        sc = SOFTCAP * jnp.tanh(pre / SOFTCAP)
        lf = jnp.minimum(sc, 0.0) - jnp.log(1.0 + jnp.exp(-jnp.abs(sc)))
        col = lax.broadcasted_iota(jnp.int32, pre.shape, 1)
        isf = jnp.logical_and(col >= A_HEADS, col < 2 * A_HEADS)
        r = lax.broadcasted_iota(jnp.int32, (tt, tt), 0)
        c = lax.broadcasted_iota(jnp.int32, (tt, tt), 1)
        tri = jnp.logical_and(jnp.right_shift(r, 6) == jnp.right_shift(c, 6), c <= r).astype(F32)
        bcum = jnp.dot(tri, jnp.where(isf, lf, 0.0), precision=HI, preferred_element_type=F32)
        g = jnp.where(col < A_HEADS, sc, jnp.where(isf, bcum, 0.0))
        gc_ref[...] = g
        for s in range(tt // 128):
            gr_ref[s] = g[s * 128:(s + 1) * 128, :].T[0:8, :]

    return pl.pallas_call(
        body, name="gate_prep", grid=(T // tt,),
        in_specs=[pl.BlockSpec((tt, 128), lambda i: (i, GATE_COL // 128)),
                  pl.BlockSpec((1, 128), lambda i: (0, 0))],
        out_specs=[pl.BlockSpec((tt, 128), lambda i: (i, 0)),
                   pl.BlockSpec((tt // 128, 8, 128), lambda i: (i, 0, 0))],
        out_shape=[jax.ShapeDtypeStruct((T, 128), F32), jax.ShapeDtypeStruct((T // 128, 8, 128), F32)],
        compiler_params=_params("parallel"),
    )(z, bias128)


def _chunk_fwd(qh, kh, vh, bc, br, lir, C, n, m, causal):
    A = _dot_nt(qh, kh)
    logD = jnp.where(causal, bc - br + lir, -jnp.inf)
    m_inter = bc + m
    m_t = jnp.maximum(m_inter, jnp.max(logD, axis=1, keepdims=True))
    E = jnp.exp(logD - m_t)
    Sm = A * E
    wi = jnp.exp(m_inter - m_t)
    num = _dot(Sm, vh) + wi * _dot(qh, C)
    qn = jnp.sum(qh.astype(F32) * n, axis=1, keepdims=True)
    den = jnp.sum(Sm, axis=1, keepdims=True) + wi * qn
    gs = jnp.maximum(jnp.abs(den), jnp.exp(-m_t))
    return E, Sm, wi, num, den, gs, m_t


def _state_weights(bc, lic, br, lir, m):
    bL = bc[A_CHUNK - 1:A_CHUNK, :]
    m_new = jnp.maximum(bL + m, jnp.max(bL - br + lir, axis=1, keepdims=True))
    wk = jnp.exp(bL - bc + lic - m_new)
    decay = jnp.exp(bL + m - m_new)
    return wk, decay, m_new


def _head_slices(h):
    return (slice(h * A_QK, (h + 1) * A_QK), slice(h * A_V, (h + 1) * A_V))


def mlstm_fwd(z, gcol, grow, hng):
    T = z.shape[0]
    NC = T // A_CHUNK
    scale = A_QK ** -0.5

    def body(q_ref, k_ref, v_ref, o_ref, gc_ref, gr_ref, hng_ref, hg_ref, Cs_ref, ns_ref, ms_ref,
             C_sc, n_sc, m_sc):
        @pl.when(pl.program_id(0) == 0)
        def _():
            C_sc[...] = jnp.zeros_like(C_sc)
            n_sc[...] = jnp.zeros_like(n_sc)
            m_sc[...] = jnp.zeros_like(m_sc)

        ri = lax.broadcasted_iota(jnp.int32, (A_CHUNK, A_CHUNK), 0)
        ci = lax.broadcasted_iota(jnp.int32, (A_CHUNK, A_CHUNK), 1)
        causal = ri >= ci
        gr = gr_ref[0]
        for c in range(A_GC):
            rows = slice(c * A_CHUNK, (c + 1) * A_CHUNK)
            gc = gc_ref[rows, :]
            grc = gr[:, c * A_CHUNK:(c + 1) * A_CHUNK]
            for h in range(A_HEADS):
                sk, sv = _head_slices(h)
                qh = (q_ref[rows, sk] * scale).astype(MM_DTYPE)
                kh = k_ref[rows, sk].astype(MM_DTYPE)
                vh = v_ref[rows, sv].astype(MM_DTYPE)
                lic, bc = gc[:, h:h + 1], gc[:, A_HEADS + h:A_HEADS + h + 1]
                lir, br = grc[h:h + 1, :], grc[A_HEADS + h:A_HEADS + h + 1, :]
                C, n, m = C_sc[h], n_sc[h], m_sc[h][:, 0:1]
                Cs_ref[c, h] = C
                ns_ref[c, h] = n
                ms_ref[c, h] = m_sc[h]
                _, _, _, num, _, gs, _ = _chunk_fwd(qh, kh, vh, bc, br, lir, C, n, m, causal)
                hh = num / gs
                hn = hh * lax.rsqrt(jnp.mean(hh * hh, axis=1, keepdims=True) + EPS) * hng_ref[:, sv]
                hg_ref[rows, sv] = (hn * _sigmoid(o_ref[rows, sv])).astype(hg_ref.dtype)
                wk, decay, m_new = _state_weights(bc, lic, br, lir, m)
                kw = kh.astype(F32) * wk
                C_sc[h] = decay * C + _dot_tn(kw, vh)
                n_sc[h] = decay * n + jnp.sum(kw, axis=0, keepdims=True)
                m_sc[h] = jnp.broadcast_to(m_new, (1, 128))

    tok = lambda w, cb: pl.BlockSpec((A_TB, w), lambda i: (i, cb))
    return pl.pallas_call(
        body, name="mlstm_fwd", grid=(NC // A_GC,),
        in_specs=[tok(512, 0), tok(512, 1), tok(1024, 1), tok(1024, 2),
                  pl.BlockSpec((A_TB, 128), lambda i: (i, 0)),
                  pl.BlockSpec((1, 8, 128), lambda i: (i, 0, 0)),
                  pl.BlockSpec((1, 1024), lambda i: (0, 0))],
        out_specs=[pl.BlockSpec((A_TB, 1024), lambda i: (i, 0)),
                   pl.BlockSpec((A_GC, A_HEADS, A_QK, A_V), lambda i: (i, 0, 0, 0)),
                   pl.BlockSpec((A_GC, A_HEADS, 1, 128), lambda i: (i, 0, 0, 0)),
                   pl.BlockSpec((A_GC, A_HEADS, 1, 128), lambda i: (i, 0, 0, 0))],
        out_shape=[jax.ShapeDtypeStruct((T, 1024), MM_DTYPE),
                   jax.ShapeDtypeStruct((NC, A_HEADS, A_QK, A_V), F32),
                   jax.ShapeDtypeStruct((NC, A_HEADS, 1, 128), F32),
                   jax.ShapeDtypeStruct((NC, A_HEADS, 1, 128), F32)],
        scratch_shapes=[pltpu.VMEM((A_HEADS, A_QK, A_V), F32), pltpu.VMEM((A_HEADS, 1, 128), F32),
                        pltpu.VMEM((A_HEADS, 1, 128), F32)],
        compiler_params=_params("arbitrary"),
    )(z, z, z, z, gcol, grow, hng)


def mlstm_bwd(z, gcol, grow, hng, bias128, Cs, ns, ms, dhg):
    T = z.shape[0]
    NC = T // A_CHUNK
    nsteps = NC // A_GC
    scale = A_QK ** -0.5

    def body(q_ref, k_ref, v_ref, o_ref, zg_ref, gc_ref, gr_ref, hng_ref, b_ref, Cs_ref, ns_ref, ms_ref,
             dhg_ref, dz_ref, dgn_ref, dbif_ref, dC_sc, dn_sc):
        @pl.when(pl.program_id(0) == 0)
        def _():
            dC_sc[...] = jnp.zeros_like(dC_sc)
            dn_sc[...] = jnp.zeros_like(dn_sc)
            dgn_ref[...] = jnp.zeros_like(dgn_ref)
            dbif_ref[...] = jnp.zeros_like(dbif_ref)

        ri = lax.broadcasted_iota(jnp.int32, (A_CHUNK, A_CHUNK), 0)
        ci = lax.broadcasted_iota(jnp.int32, (A_CHUNK, A_CHUNK), 1)
        causal = ri >= ci
        upper = (ci >= ri).astype(F32)
        rid = lax.broadcasted_iota(jnp.int32, (A_CHUNK, 1), 0)
        col = lax.broadcasted_iota(jnp.int32, (A_CHUNK, 128), 1)
        gr = gr_ref[0]
        for c in reversed(range(A_GC)):
            rows = slice(c * A_CHUNK, (c + 1) * A_CHUNK)
            gc = gc_ref[rows, :]
            grc = gr[:, c * A_CHUNK:(c + 1) * A_CHUNK]
            dG = jnp.zeros((A_CHUNK, 128), F32)
            for h in range(A_HEADS):
                sk, sv = _head_slices(h)
                qh = (q_ref[rows, sk] * scale).astype(MM_DTYPE)
                kh = k_ref[rows, sk].astype(MM_DTYPE)
                vh = v_ref[rows, sv].astype(MM_DTYPE)
                qf, kf = qh.astype(F32), kh.astype(F32)
                lic, bc = gc[:, h:h + 1], gc[:, A_HEADS + h:A_HEADS + h + 1]
                lir, br = grc[h:h + 1, :], grc[A_HEADS + h:A_HEADS + h + 1, :]
                C, n, m = Cs_ref[c, h], ns_ref[c, h], ms_ref[c, h][:, 0:1]
                dC, dn = dC_sc[h], dn_sc[h]
                E, Sm, wi, num, den, gs, m_t = _chunk_fwd(qh, kh, vh, bc, br, lir, C, n, m, causal)
                wk, decay, _ = _state_weights(bc, lic, br, lir, m)
                hh = num / gs
                r = lax.rsqrt(jnp.mean(hh * hh, axis=1, keepdims=True) + EPS)
                gn = hng_ref[:, sv]
                o = o_ref[rows, sv]
                s = _sigmoid(o)
                dhg_h = dhg_ref[rows, sv]
                dhn = dhg_h * s
                dz_ref[rows, 2048 + h * A_V:2048 + (h + 1) * A_V] = dhg_h * (hh * r * gn) * s * (1.0 - s)
                dgn_ref[:, sv] += jnp.sum(dhn * hh * r, axis=0, keepdims=True)
                dyg = dhn * gn
                dh = r * dyg - hh * (r * r * r) * jnp.mean(dyg * hh, axis=1, keepdims=True)
                dnum = dh / gs
                live = (jnp.abs(den) > jnp.exp(-m_t)).astype(F32)
                dden = -jnp.sum(dh * hh, axis=1, keepdims=True) / gs * jnp.sign(den) * live
                dSE = jnp.where(causal, _dot_nt(dnum, vh) + dden, 0.0) * E
                dq = _dot(dSE, kh) + wi * (_dot_nt(dnum, C) + dden * n)
                dk_inter = wk * (_dot_nt(vh, dC) + dn)
                dk = _dot_tn(dSE, qh) + dk_inter
                dv = _dot_tn(Sm, dnum) + wk * _dot(kh, dC)
                dz_ref[rows, sk] = dq * scale
                dz_ref[rows, 512 + h * A_QK:512 + (h + 1) * A_QK] = dk
                dz_ref[rows, 1024 + h * A_V:1024 + (h + 1) * A_V] = dv
                dli = jnp.sum(kf * dk, axis=1, keepdims=True)
                db = jnp.sum(qf * dq, axis=1, keepdims=True) - dli
                usum = jnp.sum(jnp.sum(kf * dk_inter, axis=1, keepdims=True), axis=0, keepdims=True)
                ddecay = (jnp.sum(jnp.sum(dC * C, axis=1, keepdims=True), axis=0, keepdims=True)
                          + jnp.sum(dn * n, axis=1, keepdims=True))
                db = db + jnp.where(rid == A_CHUNK - 1, usum + ddecay * decay, 0.0)
                dG = dG + jnp.where(col == h, dli, 0.0) + jnp.where(col == A_HEADS + h, db, 0.0)
                dC_sc[h] = decay * dC + _dot_tn(qf * wi, dnum)
                dn_sc[h] = decay * dn + jnp.sum(qf * (wi * dden), axis=0, keepdims=True)
            dlf = jnp.dot(upper, dG, precision=HI, preferred_element_type=F32)
            pre = zg_ref[rows, :] + b_ref[...]
            th = jnp.tanh(pre / SOFTCAP)
            dcap = 1.0 - th * th
            dpre = jnp.where(col < A_HEADS, dG * dcap,
                             jnp.where(col < 2 * A_HEADS, dlf * _sigmoid(-SOFTCAP * th) * dcap, 0.0))
            dz_ref[rows, GATE_COL:GATE_COL + 128] = dpre
            dbif_ref[...] += jnp.sum(dpre, axis=0, keepdims=True)

    rev = lambda i: nsteps - 1 - i
    tok = lambda w, cb: pl.BlockSpec((A_TB, w), lambda i: (rev(i), cb))
    st = lambda a, b: pl.BlockSpec((A_GC, A_HEADS, a, b), lambda i: (rev(i), 0, 0, 0))
    return pl.pallas_call(
        body, name="mlstm_bwd", grid=(nsteps,),
        in_specs=[tok(512, 0), tok(512, 1), tok(1024, 1), tok(1024, 2), tok(128, GATE_COL // 128),
                  pl.BlockSpec((A_TB, 128), lambda i: (rev(i), 0)),
                  pl.BlockSpec((1, 8, 128), lambda i: (rev(i), 0, 0)),
                  pl.BlockSpec((1, 1024), lambda i: (0, 0)),
                  pl.BlockSpec((1, 128), lambda i: (0, 0)),
                  st(A_QK, A_V), st(1, 128), st(1, 128),
                  pl.BlockSpec((A_TB, 1024), lambda i: (rev(i), 0))],
        out_specs=[pl.BlockSpec((A_TB, A_IN_PAD), lambda i: (rev(i), 0)),
                   pl.BlockSpec((1, 1024), lambda i: (0, 0)),
                   pl.BlockSpec((1, 128), lambda i: (0, 0))],
        out_shape=[jax.ShapeDtypeStruct((T, A_IN_PAD), F32), jax.ShapeDtypeStruct((1, 1024), F32),
                   jax.ShapeDtypeStruct((1, 128), F32)],
        scratch_shapes=[pltpu.VMEM((A_HEADS, A_QK, A_V), F32), pltpu.VMEM((A_HEADS, 1, 128), F32)],
        compiler_params=_params("arbitrary"),
    )(z, z, z, z, z, gcol, grow, hng, bias128, Cs, ns, ms, dhg)


def _t5_bucket(dist):
    max_exact = REL_BUCKETS // 2
    d = np.maximum(dist, 0)
    log_ratio = np.log(np.maximum(d, 1) / max_exact) / math.log(REL_MAX_DIST / max_exact)
    large = np.minimum(max_exact + (log_ratio * (REL_BUCKETS - max_exact)).astype(np.int64), REL_BUCKETS - 1)
    return np.where(d < max_exact, d, large).astype(np.int32)


def _group_bucket(g):
    delta = B_BLOCK + np.arange(B_BLOCK)[:, None] - np.arange(2 * B_BLOCK)[None, :]
    return _t5_bucket(delta * DILATIONS[g])


def _band_mask(n):
    ri = lax.broadcasted_iota(jnp.int32, (B_BLOCK, 2 * B_BLOCK), 0)
    ci = lax.broadcasted_iota(jnp.int32, (B_BLOCK, 2 * B_BLOCK), 1)
    band = jnp.logical_and(ci >= ri, ci <= ri + B_BLOCK)
    return jnp.logical_and(band, jnp.logical_or(ci >= B_BLOCK, n > 0))


def _both(p_ref, c_ref, sl):
    return jnp.concatenate([p_ref[:, sl], c_ref[:, sl]], axis=0)


def _scores(qh, kh, bias_h, valid):
    return jnp.where(valid, _dot_nt(qh, kh) * (B_DH ** -0.5) + bias_h, -jnp.inf)


def _attn_specs():
    wide = pl.BlockSpec((B_BLOCK, 1024), lambda r, n: (n, r))
    prev = pl.BlockSpec((B_BLOCK, 1024), lambda r, n: (jnp.maximum(n - 1, 0), r))
    narrow = pl.BlockSpec((B_BLOCK, 128), lambda r, n: (n, r))
    bias = pl.BlockSpec((B_HEADS, B_BLOCK, 2 * B_BLOCK), lambda r, n: (0, 0, 0))
    return wide, prev, narrow, bias


def _to_view(read_chunk, sc, o_ref, dil, nc, tt):
    for c in range(nc):
        sc[c] = read_chunk(c)
    for r in range(dil):
        for c in range(nc):
            lo = (r * nc + c) * 128
            o_ref[:, lo:lo + 128] = sc[c, pl.ds(r, tt // dil, stride=dil), :].astype(o_ref.dtype)


def _from_view(read_view, sc, dil, nc, tt):
    for r in range(dil):
        for c in range(nc):
            sc[c, pl.ds(r, tt // dil, stride=dil), :] = read_view((r * nc + c) * 128).astype(F32)


def attn_fwd(qv, kvw, vvw, bias, g):
    dil = DILATIONS[g]
    Tv = qv.shape[0]
    nb = Tv // B_BLOCK
    wide, prev, narrow, bsp = _attn_specs()

    def body(q_ref, kp_ref, kc_ref, vp_ref, vc_ref, b_ref, o_ref, lse_ref):
        valid = _band_mask(pl.program_id(1))
        lse_ref[...] = jnp.zeros_like(lse_ref)
        heads = [slice(h * B_DH, (h + 1) * B_DH) for h in range(B_HEADS)]
        S = [_scores(q_ref[:, sl], _both(kp_ref, kc_ref, sl), b_ref[h], valid) for h, sl in enumerate(heads)]
        P, L = [], []
        for h in range(B_HEADS):
            m = jnp.max(S[h], axis=1, keepdims=True)
            p = jnp.exp(S[h] - m)
            l = jnp.sum(p, axis=1, keepdims=True)
            lse_ref[:, h:h + 1] = m + jnp.log(l)
            P.append(p.astype(MM_DTYPE))
            L.append(l)
        for h, sl in enumerate(heads):
            o_ref[:, sl] = _dot(P[h], _both(vp_ref, vc_ref, sl)) / L[h]

    return pl.pallas_call(
        body, name=f"attn_fwd_g{g}", grid=(dil, nb),
        in_specs=[wide, prev, wide, prev, wide, bsp], out_specs=[wide, narrow],
        out_shape=[jax.ShapeDtypeStruct((Tv, dil * 1024), F32), jax.ShapeDtypeStruct((Tv, dil * 128), F32)],
        compiler_params=_params("parallel", "parallel"),
    )(qv, kvw, kvw, vvw, vvw, bias)


def attn_bwd(qv, kvw, vvw, bias, do_v, lse_v, dl_v, g):
    dil = DILATIONS[g]
    Tv = qv.shape[0]
    nb = Tv // B_BLOCK
    wide, prev, narrow, bsp = _attn_specs()

    def body(q_ref, kp_ref, kc_ref, vp_ref, vc_ref, b_ref, bt_ref, do_ref, lse_ref, dl_ref,
             dq_ref, dkc_ref, dkp_ref, dvc_ref, dvp_ref, db_ref):
        @pl.when(jnp.logical_and(pl.program_id(0) == 0, pl.program_id(1) == 0))
        def _():
            db_ref[...] = jnp.zeros_like(db_ref)

        n = pl.program_id(1)
        valid = _band_mask(n)
        ki = lax.broadcasted_iota(jnp.int32, (2 * B_BLOCK, B_BLOCK), 0)
        qi = lax.broadcasted_iota(jnp.int32, (2 * B_BLOCK, B_BLOCK), 1)
        valid_t = jnp.logical_and(jnp.logical_and(ki >= qi, ki <= qi + B_BLOCK), jnp.logical_or(ki >= B_BLOCK, n > 0))
        lse_t, dl_t = lse_ref[...].T, dl_ref[...].T
        heads = [slice(h * B_DH, (h + 1) * B_DH) for h in range(B_HEADS)]
        scale = B_DH ** -0.5
        PT, DS, DST = [], [], []
        for h, sl in enumerate(heads):
            qh, doh = q_ref[:, sl], do_ref[:, sl].astype(MM_DTYPE)
            kh, vh = _both(kp_ref, kc_ref, sl), _both(vp_ref, vc_ref, sl)
            p = jnp.exp(_scores(qh, kh, b_ref[h], valid) - lse_ref[:, h:h + 1])
            ds = p * (_dot_nt(doh, vh) - dl_ref[:, h:h + 1])
            db_ref[h] += ds
            DS.append((ds * scale).astype(MM_DTYPE))
            pt = jnp.exp(_scores(kh, qh, bt_ref[h], valid_t) - lse_t[h:h + 1, :])
            PT.append(pt.astype(MM_DTYPE))
            DST.append((pt * (_dot_nt(vh, doh) - dl_t[h:h + 1, :]) * scale).astype(MM_DTYPE))
        for h, sl in enumerate(heads):
            qh, doh = q_ref[:, sl], do_ref[:, sl].astype(MM_DTYPE)
            dq_ref[:, sl] = _dot(DS[h], _both(kp_ref, kc_ref, sl)).astype(MM_DTYPE)
            dk = _dot(DST[h], qh).astype(MM_DTYPE)
            dv = _dot(PT[h], doh).astype(MM_DTYPE)
            dkp_ref[:, sl], dkc_ref[:, sl] = dk[:B_BLOCK], dk[B_BLOCK:]
            dvp_ref[:, sl], dvc_ref[:, sl] = dv[:B_BLOCK], dv[B_BLOCK:]

    big = jax.ShapeDtypeStruct((Tv, dil * 1024), MM_DTYPE)
    bsp_t = pl.BlockSpec((B_HEADS, 2 * B_BLOCK, B_BLOCK), lambda r, n: (0, 0, 0))
    return pl.pallas_call(
        body, name=f"attn_bwd_g{g}", grid=(dil, nb),
        in_specs=[wide, prev, wide, prev, wide, bsp, bsp_t, wide, narrow, narrow],
        out_specs=[wide] * 5 + [bsp],
        out_shape=[big] * 5 + [jax.ShapeDtypeStruct((B_HEADS, B_BLOCK, 2 * B_BLOCK), F32)],
        compiler_params=_params("arbitrary", "arbitrary"),
    )(qv, kvw, kvw, vvw, vvw, bias, jnp.swapaxes(bias, 1, 2), do_v, lse_v, dl_v)


def _head_expand():
    e = np.zeros((128, 1024), np.float32)
    for h in range(B_HEADS):
        e[h, h * B_DH:(h + 1) * B_DH] = 1.0
    return e


A_TT = 256


def _view_spec(dil, width):
    return pl.BlockSpec((A_TT // dil, dil * width), lambda i: (i, 0))


def attn_merge(os_v, lses_v):
    T = os_v[0].shape[0]
    tt = A_TT
    expand = jnp.asarray(_head_expand())

    def body(o0, o1, o2, l0, l1, l2, e_ref, ob_ref, of_ref, lse0_ref, lse1_ref, lse2_ref, sc_o, sc_l):
        for gi, (o_ref, l_ref) in enumerate(((o1, l1), (o2, l2))):
            dil = DILATIONS[gi + 1]
            _from_view(lambda lo: o_ref[:, lo:lo + 128], sc_o.at[gi], dil, 8, tt)
            _from_view(lambda lo: l_ref[:, lo:lo + 128], sc_l.at[gi], dil, 1, tt)
        ls = [l0[...], sc_l[0, 0], sc_l[1, 0]]
        m = jnp.maximum(jnp.maximum(ls[0], ls[1]), ls[2])
        ex = [jnp.exp(l - m) for l in ls]
        tot = ex[0] + ex[1] + ex[2]
        lse = m + jnp.log(tot)
        lse0_ref[...] = lse
        _to_view(lambda c: lse, sc_l.at[2], lse1_ref, DILATIONS[1], 1, tt)
        _to_view(lambda c: lse, sc_l.at[2], lse2_ref, DILATIONS[2], 1, tt)
        ws = [e / tot for e in ex]
        for c in range(8):
            cols = slice(c * 128, (c + 1) * 128)
            ecol = e_ref[:, cols]
            spread = [jnp.dot(w, ecol, precision=HI, preferred_element_type=F32) for w in ws]
            out = spread[0] * o0[:, cols] + spread[1] * sc_o[0, c] + spread[2] * sc_o[1, c]
            of_ref[:, cols] = out
            ob_ref[:, cols] = out.astype(ob_ref.dtype)

    wide = pl.BlockSpec((tt, 1024), lambda i: (i, 0))
    return pl.pallas_call(
        body, name="attn_merge", grid=(T // tt,),
        in_specs=[_view_spec(d, 1024) for d in DILATIONS] + [_view_spec(d, 128) for d in DILATIONS]
        + [pl.BlockSpec((128, 1024), lambda i: (0, 0))],
        out_specs=[wide, wide] + [_view_spec(d, 128) for d in DILATIONS],
        out_shape=[jax.ShapeDtypeStruct((T, 1024), MM_DTYPE), jax.ShapeDtypeStruct((T, 1024), F32)]
        + [jax.ShapeDtypeStruct((T // d, d * 128), F32) for d in DILATIONS],
        scratch_shapes=[pltpu.VMEM((2, 8, tt, 128), F32), pltpu.VMEM((3, 1, tt, 128), F32)],
        compiler_params=_params("parallel"),
    )(*os_v, *lses_v, expand)


def attn_prep(datt, out):
    T = datt.shape[0]
    tt = A_TT
    expand_t = jnp.asarray(_head_expand().T.copy())

    def body(d_ref, o_ref, e_ref, do0, do1, do2, dl0, dl1, dl2, sc_d, sc_l):
        delta = jnp.dot(d_ref[...] * o_ref[...], e_ref[...], precision=HI, preferred_element_type=F32)
        do0[...] = d_ref[...].astype(do0.dtype)
        dl0[...] = delta
        for do_ref, dl_ref, dil in ((do1, dl1, DILATIONS[1]), (do2, dl2, DILATIONS[2])):
            _to_view(lambda c: d_ref[:, c * 128:(c + 1) * 128], sc_d, do_ref, dil, 8, tt)
            _to_view(lambda c: delta, sc_l, dl_ref, dil, 1, tt)

    wide = pl.BlockSpec((tt, 1024), lambda i: (i, 0))
    return pl.pallas_call(
        body, name="attn_prep", grid=(T // tt,),
        in_specs=[wide, wide, pl.BlockSpec((1024, 128), lambda i: (0, 0))],
        out_specs=[_view_spec(d, 1024) for d in DILATIONS] + [_view_spec(d, 128) for d in DILATIONS],
        out_shape=[jax.ShapeDtypeStruct((T // d, d * 1024), MM_DTYPE) for d in DILATIONS]
        + [jax.ShapeDtypeStruct((T // d, d * 128), F32) for d in DILATIONS],
        scratch_shapes=[pltpu.VMEM((8, tt, 128), F32), pltpu.VMEM((1, tt, 128), F32)],
        compiler_params=_params("parallel"),
    )(datt, out, expand_t)


def attn_combine(parts):
    T = parts[0][0].shape[0]
    tt = A_TT
    nt = T // tt
    shift = [None] + [B_BLOCK * d // tt for d in DILATIONS[1:]]

    def body(dq0, kc0, vc0, kpa0, kpb0, vpa0, vpb0, dq1, kc1, kp1, vc1, vp1, dq2, kc2, kp2, vc2, vp2,
             dq_ref, dkv_ref, sc):
        i = pl.program_id(0)
        dq_ref[:, 0:1024] = dq0[...].astype(dq_ref.dtype)
        for col, c_ref, pa_ref, pb_ref in ((0, kc0, kpa0, kpb0), (3, vc0, vpa0, vpb0)):
            nxt = jnp.where(i + 1 < nt, pb_ref[:tt // 2, :].astype(F32), 0.0)
            later = jnp.concatenate([pa_ref[tt // 2:, :].astype(F32), nxt], axis=0)
            dkv_ref[:, col * 1024:(col + 1) * 1024] = (c_ref[...].astype(F32) + later).astype(dkv_ref.dtype)
        for g, (dq, kc, kp, vc, vp) in ((1, (dq1, kc1, kp1, vc1, vp1)), (2, (dq2, kc2, kp2, vc2, vp2))):
            dil = DILATIONS[g]
            live = i + shift[g] < nt
            _from_view(lambda lo: dq[:, lo:lo + 128], sc, dil, 8, tt)
            for c in range(8):
                dq_ref[:, g * 1024 + c * 128:g * 1024 + (c + 1) * 128] = sc[c].astype(dq_ref.dtype)
            for col, c_ref, p_ref in ((g, kc, kp), (3 + g, vc, vp)):
                _from_view(lambda lo: c_ref[:, lo:lo + 128].astype(F32)
                           + jnp.where(live, p_ref[:, lo:lo + 128].astype(F32), 0.0), sc, dil, 8, tt)
                for c in range(8):
                    dkv_ref[:, col * 1024 + c * 128:col * 1024 + (c + 1) * 128] = sc[c].astype(dkv_ref.dtype)

    def later_spec(dil, blocks):
        return pl.BlockSpec((tt // dil, dil * 1024), lambda i: (jnp.minimum(i + blocks, nt - 1), 0))

    cur = [_view_spec(d, 1024) for d in DILATIONS]
    in_specs = [cur[0], cur[0], cur[0], cur[0], later_spec(1, 1), cur[0], later_spec(1, 1)]
    args = [parts[0][0], parts[0][1], parts[0][3], parts[0][2], parts[0][2], parts[0][4], parts[0][4]]
    for g in (1, 2):
        in_specs += [cur[g], cur[g], later_spec(DILATIONS[g], shift[g]), cur[g], later_spec(DILATIONS[g], shift[g])]
        args += list(parts[g][:5])
    return pl.pallas_call(
        body, name="attn_combine", grid=(nt,), in_specs=in_specs,
        out_specs=[pl.BlockSpec((tt, 3072), lambda i: (i, 0)), pl.BlockSpec((tt, 6144), lambda i: (i, 0))],
        out_shape=[jax.ShapeDtypeStruct((T, 3072), MM_DTYPE), jax.ShapeDtypeStruct((T, 6144), MM_DTYPE)],
        scratch_shapes=[pltpu.VMEM((8, tt, 128), F32)],
        compiler_params=_params("parallel"),
    )(*args)


def adamw(w, g, m, v, name):
    R, C = w.shape
    tr = R if R * C * 4 <= (1 << 20) else _rows(R, max(8, ((1 << 20) // (C * 4)) // 8 * 8))

    def body(w_ref, g_ref, m_ref, v_ref, d_ref, nm_ref, nv_ref):
        gg = g_ref[...]
        nm = ADAM_B1 * m_ref[...] + (1.0 - ADAM_B1) * gg
        nv = ADAM_B2 * v_ref[...] + (1.0 - ADAM_B2) * (gg * gg)
        m_hat = nm / (1.0 - ADAM_B1 ** ADAM_STEP)
        v_hat = nv / (1.0 - ADAM_B2 ** ADAM_STEP)
        d_ref[...] = -ADAM_LR * (m_hat / (jnp.sqrt(v_hat) + ADAM_EPS) + ADAM_WD * w_ref[...])
        nm_ref[...] = nm
        nv_ref[...] = nv

    blk = pl.BlockSpec((tr, C), lambda i: (i, 0))
    sds = jax.ShapeDtypeStruct((R, C), F32)
    return pl.pallas_call(
        body, name=name, grid=(R // tr,), in_specs=[blk] * 4, out_specs=[blk] * 3, out_shape=[sds] * 3,
        compiler_params=_params("parallel"),
    )(w, g, m, v)


def sum_slots(x, name, out_dtype=F32):
    n, R, C = x.shape
    tr = _rows(R, 256)

    def body(x_ref, o_ref):
        acc = x_ref[0].astype(F32)
        for s in range(1, n):
            acc = acc + x_ref[s].astype(F32)
        o_ref[...] = acc.astype(out_dtype)

    return pl.pallas_call(
        body, name=name, grid=(R // tr,),
        in_specs=[pl.BlockSpec((n, tr, C), lambda i: (0, i, 0))],
        out_specs=pl.BlockSpec((tr, C), lambda i: (i, 0)),
        out_shape=jax.ShapeDtypeStruct((R, C), out_dtype),
        compiler_params=_params("parallel"),
    )(x)


_ANY = pl.BlockSpec(memory_space=pl.ANY)
GROUP_ALL = ([(0, 0, 1), (0, 1, 0), (0, 1, 1), (1, 0, 0), (1, 0, 1), (1, 1, 0), (1, 1, 1)],
             lambda d: 4 * d[0] + 2 * d[1] + d[2])
GROUP_CHIPS = ([(0, 1, 0), (1, 0, 0), (1, 1, 0)], lambda d: 2 * d[0] + d[1])
GROUP_SIBLING = ([(0, 0, 1)], lambda d: d[2])


def _me():
    return lax.axis_index("x"), lax.axis_index("y"), lax.axis_index("c")


def _peer(me, flip):
    return tuple(1 - a if f else a for a, f in zip(me, flip))


def _group_exchange(x, name, group, scatter, chunks=1):
    flips, slot = group
    n = len(flips) + 1
    R, C = x.shape[-2:]
    nc = max([k for k in range(1, chunks + 1) if R % (16 * k) == 0] or [1])
    rc = R // nc

    def body(x_ref, o_ref, send_sems, recv_sems, local_sem):
        me = _me()
        mine = pltpu.make_async_copy(x_ref.at[slot(me)] if scatter else x_ref, o_ref.at[slot(me)], local_sem)
        mine.start()
        sends = []
        for k, flip in enumerate(flips):
            peer = _peer(me, flip)
            src = x_ref.at[slot(peer)] if scatter else x_ref
            for j in range(nc):
                part = pl.ds(j * rc, rc)
                cp = pltpu.make_async_remote_copy(
                    src_ref=src.at[part], dst_ref=o_ref.at[slot(me), part], send_sem=send_sems.at[k * nc + j],
                    recv_sem=recv_sems.at[k * nc + j], device_id=peer, device_id_type=MESH_ID)
                cp.start()
                sends.append(cp)
        for k, flip in enumerate(flips):
            peer = _peer(me, flip)
            for j in range(nc):
                part = pl.ds(j * rc, rc)
                pltpu.make_async_remote_copy(
                    src_ref=o_ref.at[slot(me), part], dst_ref=o_ref.at[slot(peer), part],
                    send_sem=send_sems.at[k * nc + j], recv_sem=recv_sems.at[k * nc + j],
                    device_id=peer, device_id_type=MESH_ID).wait_recv()
        for cp in sends:
            cp.wait_send()
        mine.wait()

    return pl.pallas_call(
        body, name=name, in_specs=[_ANY], out_specs=_ANY,
        out_shape=jax.ShapeDtypeStruct((n, R, C), x.dtype),
        scratch_shapes=[pltpu.SemaphoreType.DMA(((n - 1) * nc,)), pltpu.SemaphoreType.DMA(((n - 1) * nc,)),
                        pltpu.SemaphoreType.DMA],
    )(x)


def group_gather(x, name, group, chunks=1):
    return _group_exchange(x, name, group, scatter=False, chunks=chunks)


def group_scatter(x, name, group, chunks=1):
    return _group_exchange(x, name, group, scatter=True, chunks=chunks)


WEIGHTS = ['a_norm_g', 'a_w_in', 'a_b_if', 'a_hnorm_g', 'a_w_out', 'kv_norm_g', 'w_kv', 'b_norm_g', 'b_w_q',
           'b_w_out', 'rel_bias', 'f_norm_g', 'f_w_up', 'f_conv_w', 'f_conv_b', 'f_w_down', 'final_norm_g']
SHARD_AXIS = {'a_norm_g': 1, 'a_w_in': 2, 'a_b_if': None, 'a_hnorm_g': 2, 'a_w_out': 1, 'kv_norm_g': None,
              'w_kv': 1, 'b_norm_g': None, 'b_w_q': 2, 'b_w_out': 1, 'rel_bias': None, 'f_norm_g': None,
              'f_w_up': 2, 'f_conv_w': 2, 'f_conv_b': None, 'f_w_down': 1, 'final_norm_g': None}
BIG = ['a_w_in', 'a_w_out', 'w_kv', 'b_w_q', 'b_w_out', 'f_w_up', 'f_w_down']
SMALL = [n for n in WEIGHTS if n not in BIG]
LANES = 1024


def _pad_rows(flat, mult):
    n = flat.shape[0]
    per = LANES * mult
    tot = -(-n // per) * per
    return jnp.pad(flat, (0, tot - n)).reshape(tot // LANES, LANES)


def _full_from_shards(sh, axis):
    return jnp.concatenate([sh[j] for j in range(4)], axis=axis)


def _shards_from_full(full, axis):
    return jnp.stack(jnp.split(full, 4, axis=axis))


def _local_step(x, target, W):
    T = x.shape[0]
    row = lambda a: a.reshape(1, -1).astype(F32)
    w_in = jnp.pad(W['a_w_in'][0], ((0, 0), (0, A_IN_PAD - A_IN)))
    bias128 = jnp.pad(row(W['a_b_if'][0]), ((0, 0), (0, 120)))
    hng = row(W['a_hnorm_g'][0])
    w_up = [_interleave(W['f_w_up'][l]) for l in range(2)]
    cw = [_interleave(W['f_conv_w'][l].astype(F32)) for l in range(2)]
    cb = [_interleave(row(W['f_conv_b'][l])) for l in range(2)]
    onehots = [(jnp.asarray(_group_bucket(g).reshape(-1, 1)) == jnp.arange(128)[None, :]).astype(F32)
               for g in range(N_GROUPS)]
    rb_t = jnp.pad(W['rel_bias'].astype(F32).T, ((0, 0), (0, 128 - REL_BUCKETS)))
    biases = [mm_nn(rb_t[g * B_HEADS:(g + 1) * B_HEADS], onehots[g].T, f"rel_bias_table_g{g}", exact=True)
              .reshape(B_HEADS, B_BLOCK, 2 * B_BLOCK) for g in range(N_GROUPS)]
    G = {}

    def ffn_fwd(xin, l):
        xn, = rms_fwd(xin, [row(W['f_norm_g'][l])], f"ffn{l}_norm")
        u = mm_nn(xn, w_up[l], f"ffn{l}_up")
        act = conv_act_fwd(u, cw[l], cb[l], f"ffn{l}_act")
        return mm_nn(act, W['f_w_down'][l], f"ffn{l}_down", res=xin), (xn, u, act)

    def ffn_bwd(xin, saved, dout, l):
        xn, u, act = saved
        dact = mm_nn(dout, W['f_w_down'][l].T, f"ffn{l}_ddown")
        gd = mm_tn(act, dout, f"ffn{l}_gdown")
        du, gcw, gcb = conv_act_bwd(u, dact, cw[l], cb[l], f"ffn{l}_dact")
        dxn = mm_nn(du, w_up[l].T, f"ffn{l}_dup")
        gu = _deinterleave(mm_tn(xn, du, f"ffn{l}_gup"))
        dxin, (gn,) = rms_bwd(xin, dout, [(dxn, row(W['f_norm_g'][l]))], f"ffn{l}_dnorm")
        return dxin, gd, gu, _deinterleave(gcw), _deinterleave(gcb), gn

    xn_a, = rms_fwd(x, [row(W['a_norm_g'][0])], "a_norm")
    z = mm_nn(xn_a, w_in, "a_in")
    gcol, grow = gate_prep(z, bias128)
    hg, Cs, ns, ms = mlstm_fwd(z, gcol, grow, hng)
    x1 = mm_nn(hg, W['a_w_out'][0], "a_out", res=x)
    x2, ffn0 = ffn_fwd(x1, 0)
    xn_kv, xn_b = rms_fwd(x2, [row(W['kv_norm_g']), row(W['b_norm_g'][0])], "b_norms")
    gcols = lambda w, c: w[:, c * 1024:(c + 1) * 1024]
    qv = [mm_view(xn_b, gcols(W['b_w_q'][0], g), f"q_proj_g{g}", DILATIONS[g]) for g in range(N_GROUPS)]
    kvw = [mm_view(xn_kv, gcols(W['w_kv'], g), f"k_proj_g{g}", DILATIONS[g]) for g in range(N_GROUPS)]
    vvw = [mm_view(xn_kv, gcols(W['w_kv'], 3 + g), f"v_proj_g{g}", DILATIONS[g]) for g in range(N_GROUPS)]
    os_, lses = zip(*[attn_fwd(qv[g], kvw[g], vvw[g], biases[g], g) for g in range(N_GROUPS)])
    att, att_f, *lse_v = attn_merge(os_, lses)
    x3 = mm_nn(att, W['b_w_out'][0], "b_out", res=x2)
    x4, ffn1 = ffn_fwd(x3, 1)
    dx4, g_final, loss = loss_head(x4, target, row(W['final_norm_g']))
    G['final_norm_g'] = g_final.reshape(-1)

    dx3, gd1, gu1, gcw1, gcb1, gn1 = ffn_bwd(x3, ffn1, dx4, 1)
    datt = mm_nn(dx3, W['b_w_out'][0].T, "b_dout")
    G['b_w_out'] = mm_tn(att, dx3, "b_gout")[None]
    prep = attn_prep(datt, att_f)
    do_v, dl_v = prep[:3], prep[3:]
    parts = [attn_bwd(qv[g], kvw[g], vvw[g], biases[g], do_v[g], lse_v[g], dl_v[g], g) for g in range(N_GROUPS)]
    dq_all, dkv = attn_combine(parts)
    grb = []
    for g in range(N_GROUPS):
        gb = mm_nn(parts[g][5].reshape(B_HEADS, -1), onehots[g], f"rel_bias_g{g}", exact=True)
        grb.append(gb[:, :REL_BUCKETS].T)
    G['rel_bias'] = jnp.concatenate(grb, axis=1)
    dxn_b = mm_nn(dq_all, W['b_w_q'][0].T, "q_dproj")
    G['b_w_q'] = mm_tn(xn_b, dq_all, "q_gproj")[None]
    dxn_kv = mm_nn(dkv, W['w_kv'].T, "kv_dproj")
    G['w_kv'] = mm_tn(xn_kv, dkv, "kv_gproj")
    dx2, (g_kvn, g_bn) = rms_bwd(x2, dx3, [(dxn_kv, row(W['kv_norm_g'])), (dxn_b, row(W['b_norm_g'][0]))],
                                 "b_dnorms")
    G['kv_norm_g'] = g_kvn.reshape(-1)
    G['b_norm_g'] = g_bn
    dx1, gd0, gu0, gcw0, gcb0, gn0 = ffn_bwd(x1, ffn0, dx2, 0)
    G['f_w_down'] = jnp.stack([gd0, gd1])
    G['f_w_up'] = jnp.stack([gu0, gu1])
    G['f_conv_w'] = jnp.stack([gcw0, gcw1])
    G['f_conv_b'] = jnp.concatenate([gcb0, gcb1], axis=0)
    G['f_norm_g'] = jnp.concatenate([gn0, gn1], axis=0)
    dhg = mm_nn(dx1, W['a_w_out'][0].T, "a_dout")
    G['a_w_out'] = mm_tn(hg, dx1, "a_gout")[None]
    dz, g_hn, g_bif = mlstm_bwd(z, gcol, grow, hng, bias128, Cs, ns, ms, dhg)
    G['a_hnorm_g'] = g_hn.reshape(1, A_HEADS, A_V)
    G['a_b_if'] = g_bif[:, :2 * A_HEADS]
    dxn_a = mm_nn(dz, w_in.T, "a_din")
    G['a_w_in'] = mm_tn(xn_a, dz, "a_gin")[:, :A_IN][None]
    grad_x, (g_an,) = rms_bwd(x, dx1, [(dxn_a, row(W['a_norm_g'][0]))], "a_dnorm")
    G['a_norm_g'] = g_an
    return loss, grad_x, G


def kernel(x, a_norm_g, a_w_in, a_b_if, a_hnorm_g, a_w_out, kv_norm_g, w_kv, b_norm_g, b_w_q, b_w_out, rel_bias, f_norm_g, f_w_up, f_conv_w, f_conv_b, f_w_down, final_norm_g, loss_target, m_a_norm_g, m_a_w_in, m_a_b_if, m_a_hnorm_g, m_a_w_out, m_kv_norm_g, m_w_kv, m_b_norm_g, m_b_w_q, m_b_w_out, m_rel_bias, m_f_norm_g, m_f_w_up, m_f_conv_w, m_f_conv_b, m_f_w_down, m_final_norm_g, v_a_norm_g, v_a_w_in, v_a_b_if, v_a_hnorm_g, v_a_w_out, v_kv_norm_g, v_w_kv, v_b_norm_g, v_b_w_q, v_b_w_out, v_rel_bias, v_f_norm_g, v_f_w_up, v_f_conv_w, v_f_conv_b, v_f_w_down, v_final_norm_g):
    given = dict(locals())
    shard = {n: given[n] for n in WEIGHTS}
    mom = {n: given["m_" + n] for n in WEIGHTS}
    var = {n: given["v_" + n] for n in WEIGHTS}
    cx, cy, cc = _me()
    chip = 2 * cx + cy

    halves = [shard[n].astype(MM_DTYPE).reshape(2, -1, LANES) for n in BIG]
    sizes = [h.shape[1] for h in halves]
    mine = lax.dynamic_index_in_dim(jnp.concatenate(halves, axis=1), cc, axis=0, keepdims=False)
    fill = -mine.shape[0] % 16
    mine = jnp.pad(mine, ((0, fill), (0, 0)))
    rows = mine.shape[0]
    gathered = group_gather(mine, "gather_weights", GROUP_ALL).reshape(4, 2, rows, LANES)
    W = {}
    off = 0
    for n, sz in zip(BIG, sizes):
        W[n] = _full_from_shards(gathered[:, :, off:off + sz].reshape((4,) + shard[n].shape), SHARD_AXIS[n])
        off += sz
    sharded_small = [n for n in SMALL if SHARD_AXIS[n] is not None]
    ssz = [shard[n].size for n in sharded_small]
    sflat = jnp.concatenate([shard[n].reshape(-1) for n in sharded_small])
    sg = group_gather(_pad_rows(sflat, 8), "gather_small", GROUP_CHIPS).reshape(4, -1)
    off = 0
    for n, sz in zip(sharded_small, ssz):
        W[n] = _full_from_shards(sg[:, off:off + sz].reshape((4,) + shard[n].shape), SHARD_AXIS[n])
        off += sz
    for n in SMALL:
        if SHARD_AXIS[n] is None:
            W[n] = shard[n]

    loss_row, grad_x, G = _local_step(x[0], loss_target[0], W)

    slots = jnp.concatenate(
        [_shards_from_full(G[n], SHARD_AXIS[n]).reshape(4, 2, -1, LANES).astype(GRAD_WIRE_DTYPE) for n in BIG]
        + [jnp.zeros((4, 2, fill, LANES), GRAD_WIRE_DTYPE)], axis=2).reshape(8, rows, LANES)
    reduced = sum_slots(group_scatter(slots, "scatter_grads", GROUP_ALL), "sum_grads", GRAD_WIRE_DTYPE)
    both = group_gather(reduced, "join_halves", GROUP_SIBLING)
    gsh = {}
    off = 0
    for n, sz in zip(BIG, sizes):
        gsh[n] = both[:, off:off + sz].reshape(shard[n].shape).astype(F32)
        off += sz
    small_parts = [loss_row[0, 0:1]] + [G[n].reshape(-1) for n in SMALL]
    small_sz = [p.shape[0] for p in small_parts]
    small = sum_slots(group_gather(_pad_rows(jnp.concatenate(small_parts), 8), "gather_small_grads", GROUP_ALL),
                      "sum_small_grads").reshape(-1)
    loss = small[0]
    off = 1
    for n, sz in zip(SMALL, small_sz[1:]):
        full = small[off:off + sz].reshape(W[n].shape)
        off += sz
        if SHARD_AXIS[n] is None:
            gsh[n] = full
        else:
            gsh[n] = lax.dynamic_index_in_dim(_shards_from_full(full, SHARD_AXIS[n]), chip, 0, keepdims=False)

    delta, new_m, new_v = {}, {}, {}
    for n in WEIGHTS:
        shp = shard[n].shape
        two = lambda a: a.reshape(-1, shp[-1])
        d, nm, nv = adamw(two(shard[n]), two(gsh[n]), two(mom[n]), two(var[n]), f"adamw_{n}")
        delta[n], new_m[n], new_v[n] = d.reshape(shp), nm.reshape(shp), nv.reshape(shp)
    return (loss, grad_x[None], *[gsh[n] for n in WEIGHTS], *[delta[n] for n in WEIGHTS],
            *[new_m[n] for n in WEIGHTS], *[new_v[n] for n in WEIGHTS])
```

```python
import functools
import math

import numpy as np
import jax
import jax.numpy as jnp
from jax import lax
from jax.experimental import pallas as pl
from jax.experimental.pallas import tpu as pltpu

F32 = jnp.float32
BF16 = jnp.bfloat16
MM_DTYPE = jnp.bfloat16
GRAD_WIRE_DTYPE = jnp.bfloat16
HI = lax.Precision.HIGHEST

D_MODEL = 1024
A_HEADS = 4
A_QK = 128
A_V = 256
A_CHUNK = 64
A_IN = 3080
A_IN_PAD = 3200
GATE_COL = 3072
SOFTCAP = 15.0
N_GROUPS = 3
B_HEADS = 16
B_DH = 64
B_BLOCK = 128
DILATIONS = (1, 4, 16)
WINDOWS = (128, 512, 2048)
REL_BUCKETS = 32
REL_MAX_DIST = 2048
D_FF = 2816
FF_TC = 256
EPS = 1e-6
ADAM_LR, ADAM_B1, ADAM_B2, ADAM_EPS, ADAM_WD, ADAM_STEP = 0.001, 0.9, 0.999, 1e-08, 0.01, 10

VMEM_LIMIT = 56 * 1024 * 1024
NT_DIMS = (((1,), (1,)), ((), ()))
TN_DIMS = (((0,), (0,)), ((), ()))
MESH_ID = pl.DeviceIdType.MESH


def _params(*sem):
    return pltpu.CompilerParams(dimension_semantics=sem, vmem_limit_bytes=VMEM_LIMIT)


def _tile(n, cap):
    if n <= cap:
        return n
    best = None
    for t in range(128, cap + 1, 128):
        if n % t == 0:
            best = t
    assert best is not None, (n, cap)
    return best


def _rows(n, cap):
    if n <= cap:
        return n
    for t in range(cap // 8 * 8, 7, -8):
        if n % t == 0:
            return t
    raise ValueError((n, cap))


def _dot(a, b):
    return jnp.dot(a.astype(MM_DTYPE), b.astype(MM_DTYPE), preferred_element_type=F32)


def _dot_nt(a, b):
    return lax.dot_general(a.astype(MM_DTYPE), b.astype(MM_DTYPE), NT_DIMS, preferred_element_type=F32)


def _dot_tn(a, b):
    return lax.dot_general(a.astype(MM_DTYPE), b.astype(MM_DTYPE), TN_DIMS, preferred_element_type=F32)


def _sigmoid(x):
    return 1.0 / (1.0 + jnp.exp(-x))


def _sigmoid_tanh(x):
    return 0.5 * jnp.tanh(0.5 * x) + 0.5


def mm_nn(a, b, name, res=None, out_dtype=F32, exact=False):
    M, K = a.shape
    N = b.shape[1]
    tm, tn = _rows(M, 512), _tile(N, 1536)
    whole_k = (2 * (tm * K * a.dtype.itemsize + K * tn * b.dtype.itemsize)
               + 2 * tm * tn * 4 * (1 if res is None else 2))
    tk = K if whole_k <= 46 * 1024 * 1024 else _tile(K, 1536)
    nk = K // tk

    def body(*refs):
        if res is None:
            a_ref, b_ref, o_ref, acc = refs
            r_ref = None
        else:
            a_ref, b_ref, r_ref, o_ref, acc = refs
        if exact:
            p = jnp.dot(a_ref[...], b_ref[...], precision=HI, preferred_element_type=F32)
        else:
            p = _dot(a_ref[...], b_ref[...])

        def finish(total):
            if r_ref is not None:
                total = total + r_ref[...]
            o_ref[...] = total.astype(out_dtype)

        if nk == 1:
            finish(p)
        else:
            k = pl.program_id(2)

            @pl.when(k == 0)
            def _():
                acc[...] = p

            @pl.when(jnp.logical_and(k > 0, k < nk - 1))
            def _():
                acc[...] += p

            @pl.when(k == nk - 1)
            def _():
                finish(acc[...] + p)

    in_specs = [pl.BlockSpec((tm, tk), lambda j, i, k: (i, k)),
                pl.BlockSpec((tk, tn), lambda j, i, k: (k, j))]
    args = [a, b]
    if res is not None:
        in_specs.append(pl.BlockSpec((tm, tn), lambda j, i, k: (i, j)))
        args.append(res)
    acc_shape = (tm, tn) if nk > 1 else (8, 128)
    return pl.pallas_call(
        body, name=name, grid=(N // tn, M // tm, nk),
        in_specs=in_specs, out_specs=pl.BlockSpec((tm, tn), lambda j, i, k: (i, j)),
        out_shape=jax.ShapeDtypeStruct((M, N), out_dtype),
        scratch_shapes=[pltpu.VMEM(acc_shape, F32)],
        compiler_params=_params("parallel", "parallel", "arbitrary"),
    )(*args)


def mm_view(a, b, name, dil):
    T, K = a.shape
    tm = 512

    def body(a_ref, b_ref, o_ref, sc):
        p = _dot(a_ref[...], b_ref[...])
        if dil == 1:
            o_ref[...] = p.astype(o_ref.dtype)
        else:
            _to_view(lambda c: p[:, c * 128:(c + 1) * 128], sc, o_ref, dil, 8, tm)

    return pl.pallas_call(
        body, name=name, grid=(T // tm,),
        in_specs=[pl.BlockSpec((tm, K), lambda i: (i, 0)), pl.BlockSpec((K, 1024), lambda i: (0, 0))],
        out_specs=pl.BlockSpec((tm // dil, dil * 1024), lambda i: (i, 0)),
        out_shape=jax.ShapeDtypeStruct((T // dil, dil * 1024), MM_DTYPE),
        scratch_shapes=[pltpu.VMEM((8, tm, 128), F32)],
        compiler_params=_params("parallel"),
    )(a, b)


def mm_tn(a, g, name):
    T, Ka = a.shape
    N = g.shape[1]
    tka, tn, tt = _tile(Ka, 1536), _tile(N, 1536), _rows(T, 1024)
    nt = T // tt

    def body(a_ref, g_ref, o_ref):
        t = pl.program_id(2)
        p = _dot_tn(a_ref[...], g_ref[...])

        @pl.when(t == 0)
        def _():
            o_ref[...] = p

        @pl.when(t > 0)
        def _():
            o_ref[...] += p

    return pl.pallas_call(
        body, name=name, grid=(Ka // tka, N // tn, nt),
        in_specs=[pl.BlockSpec((tt, tka), lambda i, j, t: (t, i)),
                  pl.BlockSpec((tt, tn), lambda i, j, t: (t, j))],
        out_specs=pl.BlockSpec((tka, tn), lambda i, j, t: (i, j)),
        out_shape=jax.ShapeDtypeStruct((Ka, N), F32),
        compiler_params=_params("parallel", "parallel", "arbitrary"),
    )(a, g)


def rms_fwd(x, gains, name):
    T, D = x.shape
    tt = _rows(T, 512)
    ng = len(gains)

    def body(*refs):
        x_ref = refs[0]
        g_refs = refs[1:1 + ng]
        o_refs = refs[1 + ng:]
        xf = x_ref[...]
        y = xf * lax.rsqrt(jnp.mean(xf * xf, axis=-1, keepdims=True) + EPS)
        for g_ref, o_ref in zip(g_refs, o_refs):
            o_ref[...] = (y * g_ref[...]).astype(o_ref.dtype)

    row = pl.BlockSpec((tt, D), lambda i: (i, 0))
    gsp = pl.BlockSpec((1, D), lambda i: (0, 0))
    return pl.pallas_call(
        body, name=name, grid=(T // tt,),
        in_specs=[row] + [gsp] * ng, out_specs=[row] * ng,
        out_shape=[jax.ShapeDtypeStruct((T, D), MM_DTYPE)] * ng,
        compiler_params=_params("parallel"),
    )(x, *gains)


def rms_bwd(x, dres, branches, name):
    T, D = x.shape
    tt = _rows(T, 256)
    nb = len(branches)

    def body(*refs):
        x_ref, r_ref = refs[0], refs[1]
        dy_refs = refs[2:2 + nb]
        g_refs = refs[2 + nb:2 + 2 * nb]
        dx_ref = refs[2 + 2 * nb]
        dg_refs = refs[3 + 2 * nb:]
        i = pl.program_id(0)
        xf = x_ref[...]
        r = lax.rsqrt(jnp.mean(xf * xf, axis=-1, keepdims=True) + EPS)
        xh = xf * r
        dx = r_ref[...]
        for dy_ref, g_ref, dg_ref in zip(dy_refs, g_refs, dg_refs):
            dy = dy_ref[...].astype(F32)
            dyg = dy * g_ref[...]
            dx = dx + r * (dyg - xh * jnp.mean(dyg * xh, axis=-1, keepdims=True))
            part = jnp.sum(dy * xh, axis=0, keepdims=True)

            @pl.when(i == 0)
            def _():
                dg_ref[...] = part

            @pl.when(i > 0)
            def _():
                dg_ref[...] += part
        dx_ref[...] = dx

    row = pl.BlockSpec((tt, D), lambda i: (i, 0))
    gsp = pl.BlockSpec((1, D), lambda i: (0, 0))
    outs = pl.pallas_call(
        body, name=name, grid=(T // tt,),
        in_specs=[row, row] + [row] * nb + [gsp] * nb,
        out_specs=[row] + [gsp] * nb,
        out_shape=[jax.ShapeDtypeStruct((T, D), F32)] + [jax.ShapeDtypeStruct((1, D), F32)] * nb,
        compiler_params=_params("arbitrary"),
    )(x, dres, *[b[0] for b in branches], *[b[1] for b in branches])
    return outs[0], outs[1:]


def loss_head(x, target, gain):
    T, D = x.shape
    tt = _rows(T, 256)

    def body(x_ref, t_ref, g_ref, dx_ref, dg_ref, loss_ref):
        i = pl.program_id(0)
        xf = x_ref[...]
        g = g_ref[...]
        r = lax.rsqrt(jnp.mean(xf * xf, axis=-1, keepdims=True) + EPS)
        xh = xf * r
        e = xh * g - t_ref[...]
        lpart = 0.5 * jnp.sum(jnp.sum(e * e, axis=1, keepdims=True), axis=0, keepdims=True) / D
        dy = e / D
        dyg = dy * g
        dx_ref[...] = r * (dyg - xh * jnp.mean(dyg * xh, axis=-1, keepdims=True))
        gpart = jnp.sum(dy * xh, axis=0, keepdims=True)
        lrow = jnp.broadcast_to(lpart, (1, 128))

        @pl.when(i == 0)
        def _():
            dg_ref[...] = gpart
            loss_ref[...] = lrow

        @pl.when(i > 0)
        def _():
            dg_ref[...] += gpart
            loss_ref[...] += lrow

    row = pl.BlockSpec((tt, D), lambda i: (i, 0))
    gsp = pl.BlockSpec((1, D), lambda i: (0, 0))
    return pl.pallas_call(
        body, name="loss_head", grid=(T // tt,),
        in_specs=[row, row, gsp],
        out_specs=[row, gsp, pl.BlockSpec((1, 128), lambda i: (0, 0))],
        out_shape=[jax.ShapeDtypeStruct((T, D), F32), jax.ShapeDtypeStruct((1, D), F32),
                   jax.ShapeDtypeStruct((1, 128), F32)],
        compiler_params=_params("arbitrary"),
    )(x, target, gain)


def _shift_down(u, prev8, first, k):
    rolled = pltpu.roll(u, k, 0)
    rid = lax.broadcasted_iota(jnp.int32, u.shape, 0)
    halo = jnp.where(first, 0.0, prev8)
    out = rolled
    for j in range(k):
        out = jnp.where(rid == j, halo[8 - k + j:8 - k + j + 1, :], out)
    return out


def _conv3(u, prev8, first, w, b):
    return (_shift_down(u, prev8, first, 2) * w[0:1, :] + _shift_down(u, prev8, first, 1) * w[1:2, :]
            + u * w[2:3, :] + b)


def ffn_up_act(xn, w_up, w, b, name):
    T, K = xn.shape
    tt = _rows(T, 512)
    nj = D_FF // FF_TC

    def body(x_ref, wu_ref, w_ref, b_ref, u_ref, o_ref, tail):
        first = pl.program_id(1) == 0
        u = _dot(x_ref[...], wu_ref[...])
        u_ref[...] = u
        c = _conv3(u, tail[...], first, w_ref[...], b_ref[...])
        tail[...] = u[tt - 8:, :]
        cg, cv = c[:, :FF_TC], c[:, FF_TC:]
        o_ref[...] = (cg * _sigmoid_tanh(cg) * cv).astype(o_ref.dtype)

    return pl.pallas_call(
        body, name=name, grid=(nj, T // tt),
        in_specs=[pl.BlockSpec((tt, K), lambda j, i: (i, 0)),
                  pl.BlockSpec((K, 2 * FF_TC), lambda j, i: (0, j)),
                  pl.BlockSpec((3, 2 * FF_TC), lambda j, i: (0, j)),
                  pl.BlockSpec((1, 2 * FF_TC), lambda j, i: (0, j))],
        out_specs=[pl.BlockSpec((tt, 2 * FF_TC), lambda j, i: (i, j)),
                   pl.BlockSpec((tt, FF_TC), lambda j, i: (i, j))],
        out_shape=[jax.ShapeDtypeStruct((T, 2 * D_FF), F32), jax.ShapeDtypeStruct((T, D_FF), MM_DTYPE)],
        scratch_shapes=[pltpu.VMEM((8, 2 * FF_TC), F32)],
        compiler_params=_params("parallel", "arbitrary"),
    )(xn, w_up, w, b)


def conv_act_bwd(u, dy, wd_t, w, b, name):
    T = u.shape[0]
    D = dy.shape[1]
    tt = _rows(T, 512)
    nt = T // tt
    nj = D_FF // FF_TC
    te = tt + 8

    def body(u_ref, p_ref, n_ref, dy_ref, dyn_ref, wd_ref, w_ref, b_ref, du_ref, dw_ref, db_ref):
        i = pl.program_id(1)
        first = i == 0
        last = i == nt - 1
        w = w_ref[...]
        ue = jnp.concatenate([u_ref[...], n_ref[...]], axis=0)
        da_next = jnp.where(last, 0.0, _dot(dyn_ref[...], wd_ref[...]))
        dae = jnp.concatenate([_dot(dy_ref[...], wd_ref[...]), da_next], axis=0)
        um2 = _shift_down(ue, p_ref[...], first, 2)
        um1 = _shift_down(ue, p_ref[...], first, 1)
        c = um2 * w[0:1, :] + um1 * w[1:2, :] + ue * w[2:3, :] + b_ref[...]
        cg, cv = c[:, :FF_TC], c[:, FF_TC:]
        s = _sigmoid_tanh(cg)
        dcg = dae * cv * (s * (1.0 + cg * (1.0 - s)))
        dcv = dae * (cg * s)
        dc = jnp.concatenate([dcg, dcv], axis=1)
        du = (dc * w[2:3, :] + pltpu.roll(dc, te - 1, 0) * w[1:2, :] + pltpu.roll(dc, te - 2, 0) * w[0:1, :])
        du_ref[...] = du[:tt, :].astype(du_ref.dtype)
        dcm = dc[:tt, :]
        dwp = jnp.concatenate([jnp.sum(dcm * um2[:tt, :], axis=0, keepdims=True),
                               jnp.sum(dcm * um1[:tt, :], axis=0, keepdims=True),
                               jnp.sum(dcm * ue[:tt, :], axis=0, keepdims=True)], axis=0)
        dbp = jnp.sum(dcm, axis=0, keepdims=True)

        @pl.when(first)
        def _():
            dw_ref[...] = dwp
            db_ref[...] = dbp

        @pl.when(i > 0)
        def _():
            dw_ref[...] += dwp
            db_ref[...] += dbp

    nb8 = T // 8
    return pl.pallas_call(
        body, name=name, grid=(nj, nt),
        in_specs=[pl.BlockSpec((tt, 2 * FF_TC), lambda j, i: (i, j)),
                  pl.BlockSpec((8, 2 * FF_TC), lambda j, i: (jnp.maximum(i * (tt // 8) - 1, 0), j)),
                  pl.BlockSpec((8, 2 * FF_TC), lambda j, i: (jnp.minimum((i + 1) * (tt // 8), nb8 - 1), j)),
                  pl.BlockSpec((tt, D), lambda j, i: (i, 0)),
                  pl.BlockSpec((8, D), lambda j, i: (jnp.minimum((i + 1) * (tt // 8), nb8 - 1), 0)),
                  pl.BlockSpec((D, FF_TC), lambda j, i: (0, j)),
                  pl.BlockSpec((3, 2 * FF_TC), lambda j, i: (0, j)),
                  pl.BlockSpec((1, 2 * FF_TC), lambda j, i: (0, j))],
        out_specs=[pl.BlockSpec((tt, 2 * FF_TC), lambda j, i: (i, j)),
                   pl.BlockSpec((3, 2 * FF_TC), lambda j, i: (0, j)),
                   pl.BlockSpec((1, 2 * FF_TC), lambda j, i: (0, j))],
        out_shape=[jax.ShapeDtypeStruct((T, 2 * D_FF), MM_DTYPE),
                   jax.ShapeDtypeStruct((3, 2 * D_FF), F32),
                   jax.ShapeDtypeStruct((1, 2 * D_FF), F32)],
        compiler_params=_params("parallel", "arbitrary"),
    )(u, u, u, dy, dy, wd_t, w, b)


def _interleave(a):
    lead = a.shape[:-1]
    nj = D_FF // FF_TC
    return jnp.swapaxes(a.reshape(*lead, 2, nj, FF_TC), -3, -2).reshape(*lead, 2 * D_FF)


def _deinterleave(a):
    lead = a.shape[:-1]
    nj = D_FF // FF_TC
    return jnp.swapaxes(a.reshape(*lead, nj, 2, FF_TC), -3, -2).reshape(*lead, 2 * D_FF)


A_GC = 2
A_TB = A_GC * A_CHUNK


def gate_prep(z, bias128):
    T = z.shape[0]
    tt = _rows(T, 512)

    def body(z_ref, b_ref, gc_ref, gr_ref):
        pre = z_ref[...] + b_ref[...]
        sc = SOFTCAP * jnp.tanh(pre / SOFTCAP)
        lf = jnp.minimum(sc, 0.0) - jnp.log(1.0 + jnp.exp(-jnp.abs(sc)))
        col = lax.broadcasted_iota(jnp.int32, pre.shape, 1)
        isf = jnp.logical_and(col >= A_HEADS, col < 2 * A_HEADS)
        r = lax.broadcasted_iota(jnp.int32, (tt, tt), 0)
        c = lax.broadcasted_iota(jnp.int32, (tt, tt), 1)
        tri = jnp.logical_and(jnp.right_shift(r, 6) == jnp.right_shift(c, 6), c <= r).astype(F32)
        bcum = jnp.dot(tri, jnp.where(isf, lf, 0.0), precision=HI, preferred_element_type=F32)
        g = jnp.where(col < A_HEADS, sc, jnp.where(isf, bcum, 0.0))
        gc_ref[...] = g
        for s in range(tt // 128):
            gr_ref[s] = g[s * 128:(s + 1) * 128, :].T[0:8, :]

    return pl.pallas_call(
        body, name="gate_prep", grid=(T // tt,),
        in_specs=[pl.BlockSpec((tt, 128), lambda i: (i, GATE_COL // 128)),
                  pl.BlockSpec((1, 128), lambda i: (0, 0))],
        out_specs=[pl.BlockSpec((tt, 128), lambda i: (i, 0)),
                   pl.BlockSpec((tt // 128, 8, 128), lambda i: (i, 0, 0))],
        out_shape=[jax.ShapeDtypeStruct((T, 128), F32), jax.ShapeDtypeStruct((T // 128, 8, 128), F32)],
        compiler_params=_params("parallel"),
    )(z, bias128)


def _chunk_decay(A, qh, bc, br, lir, n, m, causal):
    logD = jnp.where(causal, bc - br + lir, -jnp.inf)
    m_inter = bc + m
    m_t = jnp.maximum(m_inter, jnp.max(logD, axis=1, keepdims=True))
    E = jnp.exp(logD - m_t)
    Sm = A * E
    wi = jnp.exp(m_inter - m_t)
    qn = jnp.sum(qh.astype(F32) * n, axis=1, keepdims=True)
    den = jnp.sum(Sm, axis=1, keepdims=True) + wi * qn
    gs = jnp.maximum(jnp.abs(den), jnp.exp(-m_t))
    return E, Sm, wi, den, gs, m_t


def _state_weights(bc, lic, br, lir, m):
    bL = bc[A_CHUNK - 1:A_CHUNK, :]
    m_new = jnp.maximum(bL + m, jnp.max(bL - br + lir, axis=1, keepdims=True))
    wk = jnp.exp(bL - bc + lic - m_new)
    decay = jnp.exp(bL + m - m_new)
    return wk, decay, m_new


def _head_slices(h):
    return (slice(h * A_QK, (h + 1) * A_QK), slice(h * A_V, (h + 1) * A_V))


def mlstm_fwd(z, gcol, grow, hng):
    T = z.shape[0]
    NC = T // A_CHUNK
    scale = A_QK ** -0.5

    def body(q_ref, k_ref, v_ref, o_ref, gc_ref, gr_ref, hng_ref, hg_ref, Cs_ref, ns_ref, ms_ref,
             C_sc, n_sc, m_sc):
        @pl.when(pl.program_id(0) == 0)
        def _():
            C_sc[...] = jnp.zeros_like(C_sc)
            n_sc[...] = jnp.zeros_like(n_sc)
            m_sc[...] = jnp.zeros_like(m_sc)

        ri = lax.broadcasted_iota(jnp.int32, (A_CHUNK, A_CHUNK), 0)
        ci = lax.broadcasted_iota(jnp.int32, (A_CHUNK, A_CHUNK), 1)
        causal = ri >= ci
        gr = gr_ref[0]
        for c in range(A_GC):
            rows = slice(c * A_CHUNK, (c + 1) * A_CHUNK)
            gc = gc_ref[rows, :]
            grc = gr[:, c * A_CHUNK:(c + 1) * A_CHUNK]
            hs = []
            for h in range(A_HEADS):
                sk, sv = _head_slices(h)
                s = dict(sv=sv, qh=(q_ref[rows, sk] * scale).astype(MM_DTYPE), kh=k_ref[rows, sk].astype(MM_DTYPE),
                         vh=v_ref[rows, sv].astype(MM_DTYPE), lic=gc[:, h:h + 1], bc=gc[:, A_HEADS + h:A_HEADS + h + 1],
                         lir=grc[h:h + 1, :], br=grc[A_HEADS + h:A_HEADS + h + 1, :],
                         C=C_sc[h], n=n_sc[h], m=m_sc[h][:, 0:1])
                Cs_ref[c, h] = s['C']
                ns_ref[c, h] = s['n']
                ms_ref[c, h] = m_sc[h]
                s['wk'], s['decay'], s['m_new'] = _state_weights(s['bc'], s['lic'], s['br'], s['lir'], s['m'])
                s['kw'] = s['kh'].astype(F32) * s['wk']
                hs.append(s)
            for s in hs:
                s['A'] = _dot_nt(s['qh'], s['kh'])
                s['qC'] = _dot(s['qh'], s['C'])
                s['kv'] = _dot_tn(s['kw'], s['vh'])
            for s in hs:
                _, s['Sm'], s['wi'], _, s['gs'], _ = _chunk_decay(s['A'], s['qh'], s['bc'], s['br'], s['lir'],
                                                                  s['n'], s['m'], causal)
            for s in hs:
                s['num'] = _dot(s['Sm'], s['vh']) + s['wi'] * s['qC']
            for h, s in enumerate(hs):
                sv = s['sv']
                hh = s['num'] / s['gs']
                hn = hh * lax.rsqrt(jnp.mean(hh * hh, axis=1, keepdims=True) + EPS) * hng_ref[:, sv]
                hg_ref[rows, sv] = (hn * _sigmoid(o_ref[rows, sv])).astype(hg_ref.dtype)
                C_sc[h] = s['decay'] * s['C'] + s['kv']
                n_sc[h] = s['decay'] * s['n'] + jnp.sum(s['kw'], axis=0, keepdims=True)
                m_sc[h] = jnp.broadcast_to(s['m_new'], (1, 128))

    tok = lambda w, cb: pl.BlockSpec((A_TB, w), lambda i: (i, cb))
    return pl.pallas_call(
        body, name="mlstm_fwd", grid=(NC // A_GC,),
        in_specs=[tok(512, 0), tok(512, 1), tok(1024, 1), tok(1024, 2),
                  pl.BlockSpec((A_TB, 128), lambda i: (i, 0)),
                  pl.BlockSpec((1, 8, 128), lambda i: (i, 0, 0)),
                  pl.BlockSpec((1, 1024), lambda i: (0, 0))],
        out_specs=[pl.BlockSpec((A_TB, 1024), lambda i: (i, 0)),
                   pl.BlockSpec((A_GC, A_HEADS, A_QK, A_V), lambda i: (i, 0, 0, 0)),
                   pl.BlockSpec((A_GC, A_HEADS, 1, 128), lambda i: (i, 0, 0, 0)),
                   pl.BlockSpec((A_GC, A_HEADS, 1, 128), lambda i: (i, 0, 0, 0))],
        out_shape=[jax.ShapeDtypeStruct((T, 1024), MM_DTYPE),
                   jax.ShapeDtypeStruct((NC, A_HEADS, A_QK, A_V), F32),
                   jax.ShapeDtypeStruct((NC, A_HEADS, 1, 128), F32),
                   jax.ShapeDtypeStruct((NC, A_HEADS, 1, 128), F32)],
        scratch_shapes=[pltpu.VMEM((A_HEADS, A_QK, A_V), F32), pltpu.VMEM((A_HEADS, 1, 128), F32),
                        pltpu.VMEM((A_HEADS, 1, 128), F32)],
        compiler_params=_params("arbitrary"),
    )(z, z, z, z, gcol, grow, hng)


def mlstm_bwd(z, gcol, grow, hng, bias128, Cs, ns, ms, dhg):
    T = z.shape[0]
    NC = T // A_CHUNK
    nsteps = NC // A_GC
    scale = A_QK ** -0.5

    def body(q_ref, k_ref, v_ref, o_ref, zg_ref, gc_ref, gr_ref, hng_ref, b_ref, Cs_ref, ns_ref, ms_ref,
             dhg_ref, dz_ref, dgn_ref, dbif_ref, dC_sc, dn_sc):
        @pl.when(pl.program_id(0) == 0)
        def _():
            dC_sc[...] = jnp.zeros_like(dC_sc)
            dn_sc[...] = jnp.zeros_like(dn_sc)
            dgn_ref[...] = jnp.zeros_like(dgn_ref)
            dbif_ref[...] = jnp.zeros_like(dbif_ref)

        ri = lax.broadcasted_iota(jnp.int32, (A_CHUNK, A_CHUNK), 0)
        ci = lax.broadcasted_iota(jnp.int32, (A_CHUNK, A_CHUNK), 1)
        causal = ri >= ci
        upper = (ci >= ri).astype(F32)
        rid = lax.broadcasted_iota(jnp.int32, (A_CHUNK, 1), 0)
        col = lax.broadcasted_iota(jnp.int32, (A_CHUNK, 128), 1)
        gr = gr_ref[0]
        for c in reversed(range(A_GC)):
            rows = slice(c * A_CHUNK, (c + 1) * A_CHUNK)
            gc = gc_ref[rows, :]
            grc = gr[:, c * A_CHUNK:(c + 1) * A_CHUNK]
            dG = jnp.zeros((A_CHUNK, 128), F32)
            hs = []
            for h in range(A_HEADS):
                sk, sv = _head_slices(h)
                s = dict(sk=sk, sv=sv, qh=(q_ref[rows, sk] * scale).astype(MM_DTYPE),
                         kh=k_ref[rows, sk].astype(MM_DTYPE), vh=v_ref[rows, sv].astype(MM_DTYPE),
                         lic=gc[:, h:h + 1], bc=gc[:, A_HEADS + h:A_HEADS + h + 1],
                         lir=grc[h:h + 1, :], br=grc[A_HEADS + h:A_HEADS + h + 1, :],
                         C=Cs_ref[c, h], n=ns_ref[c, h], m=ms_ref[c, h][:, 0:1], dC=dC_sc[h], dn=dn_sc[h])
                s['qf'], s['kf'] = s['qh'].astype(F32), s['kh'].astype(F32)
                s['wk'], s['decay'], _ = _state_weights(s['bc'], s['lic'], s['br'], s['lir'], s['m'])
                hs.append(s)
            for s in hs:
                s['A'] = _dot_nt(s['qh'], s['kh'])
                s['qC'] = _dot(s['qh'], s['C'])
                s['vdC'] = _dot_nt(s['vh'], s['dC'])
                s['kdC'] = _dot(s['kh'], s['dC'])
            for s in hs:
                s['E'], s['Sm'], s['wi'], s['den'], s['gs'], s['m_t'] = _chunk_decay(
                    s['A'], s['qh'], s['bc'], s['br'], s['lir'], s['n'], s['m'], causal)
            for s in hs:
                s['num'] = _dot(s['Sm'], s['vh']) + s['wi'] * s['qC']
            for h, s in enumerate(hs):
                sv, gs = s['sv'], s['gs']
                hh = s['num'] / gs
                r = lax.rsqrt(jnp.mean(hh * hh, axis=1, keepdims=True) + EPS)
                gn = hng_ref[:, sv]
                sg = _sigmoid(o_ref[rows, sv])
                dhg_h = dhg_ref[rows, sv]
                dhn = dhg_h * sg
                dz_ref[rows, 2048 + h * A_V:2048 + (h + 1) * A_V] = dhg_h * (hh * r * gn) * sg * (1.0 - sg)
                dgn_ref[:, sv] += jnp.sum(dhn * hh * r, axis=0, keepdims=True)
                dyg = dhn * gn
                dh = r * dyg - hh * (r * r * r) * jnp.mean(dyg * hh, axis=1, keepdims=True)
                s['dnum'] = dh / gs
                live = (jnp.abs(s['den']) > jnp.exp(-s['m_t'])).astype(F32)
                s['dden'] = -jnp.sum(dh * hh, axis=1, keepdims=True) / gs * jnp.sign(s['den']) * live
            for s in hs:
                s['dnv'] = _dot_nt(s['dnum'], s['vh'])
                s['dnC'] = _dot_nt(s['dnum'], s['C'])
            for s in hs:
                s['dSE'] = jnp.where(causal, s['dnv'] + s['dden'], 0.0) * s['E']
            for s in hs:
                s['dq'] = _dot(s['dSE'], s['kh']) + s['wi'] * (s['dnC'] + s['dden'] * s['n'])
                s['dk_inter'] = s['wk'] * (s['vdC'] + s['dn'])
                s['dk'] = _dot_tn(s['dSE'], s['qh']) + s['dk_inter']
                s['dv'] = _dot_tn(s['Sm'], s['dnum']) + s['wk'] * s['kdC']
                s['dCq'] = _dot_tn(s['qf'] * s['wi'], s['dnum'])
            for h, s in enumerate(hs):
                dq, dk, qf, kf, dC, dn = s['dq'], s['dk'], s['qf'], s['kf'], s['dC'], s['dn']
                dz_ref[rows, s['sk']] = dq * scale
                dz_ref[rows, 512 + h * A_QK:512 + (h + 1) * A_QK] = dk
                dz_ref[rows, 1024 + h * A_V:1024 + (h + 1) * A_V] = s['dv']
                dli = jnp.sum(kf * dk, axis=1, keepdims=True)
                db = jnp.sum(qf * dq, axis=1, keepdims=True) - dli
                usum = jnp.sum(jnp.sum(kf * s['dk_inter'], axis=1, keepdims=True), axis=0, keepdims=True)
                ddecay = (jnp.sum(jnp.sum(dC * s['C'], axis=1, keepdims=True), axis=0, keepdims=True)
                          + jnp.sum(dn * s['n'], axis=1, keepdims=True))
                db = db + jnp.where(rid == A_CHUNK - 1, usum + ddecay * s['decay'], 0.0)
                dG = dG + jnp.where(col == h, dli, 0.0) + jnp.where(col == A_HEADS + h, db, 0.0)
                dC_sc[h] = s['decay'] * dC + s['dCq']
                dn_sc[h] = s['decay'] * dn + jnp.sum(qf * (s['wi'] * s['dden']), axis=0, keepdims=True)
            dlf = jnp.dot(upper, dG, precision=HI, preferred_element_type=F32)
            pre = zg_ref[rows, :] + b_ref[...]
            th = jnp.tanh(pre / SOFTCAP)
            dcap = 1.0 - th * th
            dpre = jnp.where(col < A_HEADS, dG * dcap,
                             jnp.where(col < 2 * A_HEADS, dlf * _sigmoid(-SOFTCAP * th) * dcap, 0.0))
            dz_ref[rows, GATE_COL:GATE_COL + 128] = dpre
            dbif_ref[...] += jnp.sum(dpre, axis=0, keepdims=True)

    rev = lambda i: nsteps - 1 - i
    tok = lambda w, cb: pl.BlockSpec((A_TB, w), lambda i: (rev(i), cb))
    st = lambda a, b: pl.BlockSpec((A_GC, A_HEADS, a, b), lambda i: (rev(i), 0, 0, 0))
    return pl.pallas_call(
        body, name="mlstm_bwd", grid=(nsteps,),
        in_specs=[tok(512, 0), tok(512, 1), tok(1024, 1), tok(1024, 2), tok(128, GATE_COL // 128),
                  pl.BlockSpec((A_TB, 128), lambda i: (rev(i), 0)),
                  pl.BlockSpec((1, 8, 128), lambda i: (rev(i), 0, 0)),
                  pl.BlockSpec((1, 1024), lambda i: (0, 0)),
                  pl.BlockSpec((1, 128), lambda i: (0, 0)),
                  st(A_QK, A_V), st(1, 128), st(1, 128),
                  pl.BlockSpec((A_TB, 1024), lambda i: (rev(i), 0))],
        out_specs=[pl.BlockSpec((A_TB, A_IN_PAD), lambda i: (rev(i), 0)),
                   pl.BlockSpec((1, 1024), lambda i: (0, 0)),
                   pl.BlockSpec((1, 128), lambda i: (0, 0))],
        out_shape=[jax.ShapeDtypeStruct((T, A_IN_PAD), F32), jax.ShapeDtypeStruct((1, 1024), F32),
                   jax.ShapeDtypeStruct((1, 128), F32)],
        scratch_shapes=[pltpu.VMEM((A_HEADS, A_QK, A_V), F32), pltpu.VMEM((A_HEADS, 1, 128), F32)],
        compiler_params=_params("arbitrary"),
    )(z, z, z, z, z, gcol, grow, hng, bias128, Cs, ns, ms, dhg)


def _t5_bucket(dist):
    max_exact = REL_BUCKETS // 2
    d = np.maximum(dist, 0)
    log_ratio = np.log(np.maximum(d, 1) / max_exact) / math.log(REL_MAX_DIST / max_exact)
    large = np.minimum(max_exact + (log_ratio * (REL_BUCKETS - max_exact)).astype(np.int64), REL_BUCKETS - 1)
    return np.where(d < max_exact, d, large).astype(np.int32)


def _group_bucket(g):
    delta = B_BLOCK + np.arange(B_BLOCK)[:, None] - np.arange(2 * B_BLOCK)[None, :]
    return _t5_bucket(delta * DILATIONS[g])


def _band_mask(n):
    ri = lax.broadcasted_iota(jnp.int32, (B_BLOCK, 2 * B_BLOCK), 0)
    ci = lax.broadcasted_iota(jnp.int32, (B_BLOCK, 2 * B_BLOCK), 1)
    band = jnp.logical_and(ci >= ri, ci <= ri + B_BLOCK)
    return jnp.logical_and(band, jnp.logical_or(ci >= B_BLOCK, n > 0))


def _both(p_ref, c_ref, sl):
    return jnp.concatenate([p_ref[:, sl], c_ref[:, sl]], axis=0)


def _scores(qh, kh, bias_h, valid):
    return jnp.where(valid, _dot_nt(qh, kh) * (B_DH ** -0.5) + bias_h, -jnp.inf)


def _attn_specs():
    wide = pl.BlockSpec((B_BLOCK, 1024), lambda r, n: (n, r))
    prev = pl.BlockSpec((B_BLOCK, 1024), lambda r, n: (jnp.maximum(n - 1, 0), r))
    narrow = pl.BlockSpec((B_BLOCK, 128), lambda r, n: (n, r))
    bias = pl.BlockSpec((B_HEADS, B_BLOCK, 2 * B_BLOCK), lambda r, n: (0, 0, 0))
    return wide, prev, narrow, bias


def _to_view(read_chunk, sc, o_ref, dil, nc, tt):
    for c in range(nc):
        sc[c] = read_chunk(c)
    for r in range(dil):
        for c in range(nc):
            lo = (r * nc + c) * 128
            o_ref[:, lo:lo + 128] = sc[c, pl.ds(r, tt // dil, stride=dil), :].astype(o_ref.dtype)


def _from_view(read_view, sc, dil, nc, tt):
    for r in range(dil):
        for c in range(nc):
            sc[c, pl.ds(r, tt // dil, stride=dil), :] = read_view((r * nc + c) * 128).astype(F32)


def attn_fwd(qv, kvw, vvw, bias, g):
    dil = DILATIONS[g]
    Tv = qv.shape[0]
    nb = Tv // B_BLOCK
    wide, prev, narrow, bsp = _attn_specs()

    def body(q_ref, kp_ref, kc_ref, vp_ref, vc_ref, b_ref, o_ref, lse_ref):
        valid = _band_mask(pl.program_id(1))
        lse_ref[...] = jnp.zeros_like(lse_ref)
        heads = [slice(h * B_DH, (h + 1) * B_DH) for h in range(B_HEADS)]
        S = [_scores(q_ref[:, sl], _both(kp_ref, kc_ref, sl), b_ref[h], valid) for h, sl in enumerate(heads)]
        P, L = [], []
        for h in range(B_HEADS):
            m = jnp.max(S[h], axis=1, keepdims=True)
            p = jnp.exp(S[h] - m)
            l = jnp.sum(p, axis=1, keepdims=True)
            lse_ref[:, h:h + 1] = m + jnp.log(l)
            P.append(p.astype(MM_DTYPE))
            L.append(l)
        for h, sl in enumerate(heads):
            o_ref[:, sl] = _dot(P[h], _both(vp_ref, vc_ref, sl)) / L[h]

    return pl.pallas_call(
        body, name=f"attn_fwd_g{g}", grid=(dil, nb),
        in_specs=[wide, prev, wide, prev, wide, bsp], out_specs=[wide, narrow],
        out_shape=[jax.ShapeDtypeStruct((Tv, dil * 1024), F32), jax.ShapeDtypeStruct((Tv, dil * 128), F32)],
        compiler_params=_params("parallel", "parallel"),
    )(qv, kvw, kvw, vvw, vvw, bias)


def attn_bwd(qv, kvw, vvw, bias, do_v, lse_v, dl_v, g):
    dil = DILATIONS[g]
    Tv = qv.shape[0]
    nb = Tv // B_BLOCK
    wide, prev, narrow, bsp = _attn_specs()

    def body(q_ref, kp_ref, kc_ref, vp_ref, vc_ref, b_ref, bt_ref, do_ref, lse_ref, dl_ref,
             dq_ref, dkc_ref, dkp_ref, dvc_ref, dvp_ref, db_ref):
        @pl.when(jnp.logical_and(pl.program_id(0) == 0, pl.program_id(1) == 0))
        def _():
            db_ref[...] = jnp.zeros_like(db_ref)

        n = pl.program_id(1)
        valid = _band_mask(n)
        ki = lax.broadcasted_iota(jnp.int32, (2 * B_BLOCK, B_BLOCK), 0)
        qi = lax.broadcasted_iota(jnp.int32, (2 * B_BLOCK, B_BLOCK), 1)
        valid_t = jnp.logical_and(jnp.logical_and(ki >= qi, ki <= qi + B_BLOCK), jnp.logical_or(ki >= B_BLOCK, n > 0))
        lse_t, dl_t = lse_ref[...].T, dl_ref[...].T
        heads = [slice(h * B_DH, (h + 1) * B_DH) for h in range(B_HEADS)]
        scale = B_DH ** -0.5
        PT, DS, DST = [], [], []
        for h, sl in enumerate(heads):
            qh, doh = q_ref[:, sl], do_ref[:, sl].astype(MM_DTYPE)
            kh, vh = _both(kp_ref, kc_ref, sl), _both(vp_ref, vc_ref, sl)
            p = jnp.exp(_scores(qh, kh, b_ref[h], valid) - lse_ref[:, h:h + 1])
            ds = p * (_dot_nt(doh, vh) - dl_ref[:, h:h + 1])
            db_ref[h] += ds
            DS.append((ds * scale).astype(MM_DTYPE))
            pt = jnp.exp(_scores(kh, qh, bt_ref[h], valid_t) - lse_t[h:h + 1, :])
            PT.append(pt.astype(MM_DTYPE))
            DST.append((pt * (_dot_nt(vh, doh) - dl_t[h:h + 1, :]) * scale).astype(MM_DTYPE))
        for h, sl in enumerate(heads):
            qh, doh = q_ref[:, sl], do_ref[:, sl].astype(MM_DTYPE)
            dq_ref[:, sl] = _dot(DS[h], _both(kp_ref, kc_ref, sl)).astype(MM_DTYPE)
            dk = _dot(DST[h], qh).astype(MM_DTYPE)
            dv = _dot(PT[h], doh).astype(MM_DTYPE)
            dkp_ref[:, sl], dkc_ref[:, sl] = dk[:B_BLOCK], dk[B_BLOCK:]
            dvp_ref[:, sl], dvc_ref[:, sl] = dv[:B_BLOCK], dv[B_BLOCK:]

    big = jax.ShapeDtypeStruct((Tv, dil * 1024), MM_DTYPE)
    bsp_t = pl.BlockSpec((B_HEADS, 2 * B_BLOCK, B_BLOCK), lambda r, n: (0, 0, 0))
    return pl.pallas_call(
        body, name=f"attn_bwd_g{g}", grid=(dil, nb),
        in_specs=[wide, prev, wide, prev, wide, bsp, bsp_t, wide, narrow, narrow],
        out_specs=[wide] * 5 + [bsp],
        out_shape=[big] * 5 + [jax.ShapeDtypeStruct((B_HEADS, B_BLOCK, 2 * B_BLOCK), F32)],
        compiler_params=_params("arbitrary", "arbitrary"),
    )(qv, kvw, kvw, vvw, vvw, bias, jnp.swapaxes(bias, 1, 2), do_v, lse_v, dl_v)


def _head_expand():
    e = np.zeros((128, 1024), np.float32)
    for h in range(B_HEADS):
        e[h, h * B_DH:(h + 1) * B_DH] = 1.0
    return e


A_TT = 256


def _view_spec(dil, width):
    return pl.BlockSpec((A_TT // dil, dil * width), lambda i: (i, 0))


def attn_merge(os_v, lses_v):
    T = os_v[0].shape[0]
    tt = A_TT
    expand = jnp.asarray(_head_expand())

    def body(o0, o1, o2, l0, l1, l2, e_ref, ob_ref, of_ref, lse0_ref, lse1_ref, lse2_ref, sc_o, sc_l):
        for gi, (o_ref, l_ref) in enumerate(((o1, l1), (o2, l2))):
            dil = DILATIONS[gi + 1]
            _from_view(lambda lo: o_ref[:, lo:lo + 128], sc_o.at[gi], dil, 8, tt)
            _from_view(lambda lo: l_ref[:, lo:lo + 128], sc_l.at[gi], dil, 1, tt)
        ls = [l0[...], sc_l[0, 0], sc_l[1, 0]]
        m = jnp.maximum(jnp.maximum(ls[0], ls[1]), ls[2])
        ex = [jnp.exp(l - m) for l in ls]
        tot = ex[0] + ex[1] + ex[2]
        lse = m + jnp.log(tot)
        lse0_ref[...] = lse
        _to_view(lambda c: lse, sc_l.at[2], lse1_ref, DILATIONS[1], 1, tt)
        _to_view(lambda c: lse, sc_l.at[2], lse2_ref, DILATIONS[2], 1, tt)
        ws = [e / tot for e in ex]
        for c in range(8):
            cols = slice(c * 128, (c + 1) * 128)
            ecol = e_ref[:, cols]
            spread = [jnp.dot(w, ecol, precision=HI, preferred_element_type=F32) for w in ws]
            out = spread[0] * o0[:, cols] + spread[1] * sc_o[0, c] + spread[2] * sc_o[1, c]
            of_ref[:, cols] = out
            ob_ref[:, cols] = out.astype(ob_ref.dtype)

    wide = pl.BlockSpec((tt, 1024), lambda i: (i, 0))
    return pl.pallas_call(
        body, name="attn_merge", grid=(T // tt,),
        in_specs=[_view_spec(d, 1024) for d in DILATIONS] + [_view_spec(d, 128) for d in DILATIONS]
        + [pl.BlockSpec((128, 1024), lambda i: (0, 0))],
        out_specs=[wide, wide] + [_view_spec(d, 128) for d in DILATIONS],
        out_shape=[jax.ShapeDtypeStruct((T, 1024), MM_DTYPE), jax.ShapeDtypeStruct((T, 1024), F32)]
        + [jax.ShapeDtypeStruct((T // d, d * 128), F32) for d in DILATIONS],
        scratch_shapes=[pltpu.VMEM((2, 8, tt, 128), F32), pltpu.VMEM((3, 1, tt, 128), F32)],
        compiler_params=_params("parallel"),
    )(*os_v, *lses_v, expand)


def attn_prep(datt, out):
    T = datt.shape[0]
    tt = A_TT
    expand_t = jnp.asarray(_head_expand().T.copy())

    def body(d_ref, o_ref, e_ref, do0, do1, do2, dl0, dl1, dl2, sc_d, sc_l):
        delta = jnp.dot(d_ref[...] * o_ref[...], e_ref[...], precision=HI, preferred_element_type=F32)
        do0[...] = d_ref[...].astype(do0.dtype)
        dl0[...] = delta
        for do_ref, dl_ref, dil in ((do1, dl1, DILATIONS[1]), (do2, dl2, DILATIONS[2])):
            _to_view(lambda c: d_ref[:, c * 128:(c + 1) * 128], sc_d, do_ref, dil, 8, tt)
            _to_view(lambda c: delta, sc_l, dl_ref, dil, 1, tt)

    wide = pl.BlockSpec((tt, 1024), lambda i: (i, 0))
    return pl.pallas_call(
        body, name="attn_prep", grid=(T // tt,),
        in_specs=[wide, wide, pl.BlockSpec((1024, 128), lambda i: (0, 0))],
        out_specs=[_view_spec(d, 1024) for d in DILATIONS] + [_view_spec(d, 128) for d in DILATIONS],
        out_shape=[jax.ShapeDtypeStruct((T // d, d * 1024), MM_DTYPE) for d in DILATIONS]
        + [jax.ShapeDtypeStruct((T // d, d * 128), F32) for d in DILATIONS],
        scratch_shapes=[pltpu.VMEM((8, tt, 128), F32), pltpu.VMEM((1, tt, 128), F32)],
        compiler_params=_params("parallel"),
    )(datt, out, expand_t)


def attn_combine(parts):
    T = parts[0][0].shape[0]
    tt = A_TT
    nt = T // tt
    shift = [None] + [B_BLOCK * d // tt for d in DILATIONS[1:]]

    def body(dq0, kc0, vc0, kpa0, kpb0, vpa0, vpb0, dq1, kc1, kp1, vc1, vp1, dq2, kc2, kp2, vc2, vp2,
             dq_ref, dkv_ref, sc):
        i = pl.program_id(0)
        dq_ref[:, 0:1024] = dq0[...].astype(dq_ref.dtype)
        for col, c_ref, pa_ref, pb_ref in ((0, kc0, kpa0, kpb0), (3, vc0, vpa0, vpb0)):
            nxt = jnp.where(i + 1 < nt, pb_ref[:tt // 2, :].astype(F32), 0.0)
            later = jnp.concatenate([pa_ref[tt // 2:, :].astype(F32), nxt], axis=0)
            dkv_ref[:, col * 1024:(col + 1) * 1024] = (c_ref[...].astype(F32) + later).astype(dkv_ref.dtype)
        for g, (dq, kc, kp, vc, vp) in ((1, (dq1, kc1, kp1, vc1, vp1)), (2, (dq2, kc2, kp2, vc2, vp2))):
            dil = DILATIONS[g]
            live = i + shift[g] < nt
            _from_view(lambda lo: dq[:, lo:lo + 128], sc, dil, 8, tt)
            for c in range(8):
                dq_ref[:, g * 1024 + c * 128:g * 1024 + (c + 1) * 128] = sc[c].astype(dq_ref.dtype)
            for col, c_ref, p_ref in ((g, kc, kp), (3 + g, vc, vp)):
                _from_view(lambda lo: c_ref[:, lo:lo + 128].astype(F32)
                           + jnp.where(live, p_ref[:, lo:lo + 128].astype(F32), 0.0), sc, dil, 8, tt)
                for c in range(8):
                    dkv_ref[:, col * 1024 + c * 128:col * 1024 + (c + 1) * 128] = sc[c].astype(dkv_ref.dtype)

    def later_spec(dil, blocks):
        return pl.BlockSpec((tt // dil, dil * 1024), lambda i: (jnp.minimum(i + blocks, nt - 1), 0))

    cur = [_view_spec(d, 1024) for d in DILATIONS]
    in_specs = [cur[0], cur[0], cur[0], cur[0], later_spec(1, 1), cur[0], later_spec(1, 1)]
    args = [parts[0][0], parts[0][1], parts[0][3], parts[0][2], parts[0][2], parts[0][4], parts[0][4]]
    for g in (1, 2):
        in_specs += [cur[g], cur[g], later_spec(DILATIONS[g], shift[g]), cur[g], later_spec(DILATIONS[g], shift[g])]
        args += list(parts[g][:5])
    return pl.pallas_call(
        body, name="attn_combine", grid=(nt,), in_specs=in_specs,
        out_specs=[pl.BlockSpec((tt, 3072), lambda i: (i, 0)), pl.BlockSpec((tt, 6144), lambda i: (i, 0))],
        out_shape=[jax.ShapeDtypeStruct((T, 3072), MM_DTYPE), jax.ShapeDtypeStruct((T, 6144), MM_DTYPE)],
        scratch_shapes=[pltpu.VMEM((8, tt, 128), F32)],
        compiler_params=_params("parallel"),
    )(*args)


def adamw(w, g, m, v, name):
    R, C = w.shape
    tr = R if R * C * 4 <= (1 << 20) else _rows(R, max(8, ((1 << 20) // (C * 4)) // 8 * 8))

    def body(w_ref, g_ref, m_ref, v_ref, d_ref, nm_ref, nv_ref):
        gg = g_ref[...]
        nm = ADAM_B1 * m_ref[...] + (1.0 - ADAM_B1) * gg
        nv = ADAM_B2 * v_ref[...] + (1.0 - ADAM_B2) * (gg * gg)
        m_hat = nm / (1.0 - ADAM_B1 ** ADAM_STEP)
        v_hat = nv / (1.0 - ADAM_B2 ** ADAM_STEP)
        d_ref[...] = -ADAM_LR * (m_hat / (jnp.sqrt(v_hat) + ADAM_EPS) + ADAM_WD * w_ref[...])
        nm_ref[...] = nm
        nv_ref[...] = nv

    blk = pl.BlockSpec((tr, C), lambda i: (i, 0))
    sds = jax.ShapeDtypeStruct((R, C), F32)
    return pl.pallas_call(
        body, name=name, grid=(R // tr,), in_specs=[blk] * 4, out_specs=[blk] * 3, out_shape=[sds] * 3,
        compiler_params=_params("parallel"),
    )(w, g, m, v)


def sum_slots(x, name, out_dtype=F32):
    n, R, C = x.shape
    tr = _rows(R, 256)

    def body(x_ref, o_ref):
        acc = x_ref[0].astype(F32)
        for s in range(1, n):
            acc = acc + x_ref[s].astype(F32)
        o_ref[...] = acc.astype(out_dtype)

    return pl.pallas_call(
        body, name=name, grid=(R // tr,),
        in_specs=[pl.BlockSpec((n, tr, C), lambda i: (0, i, 0))],
        out_specs=pl.BlockSpec((tr, C), lambda i: (i, 0)),
        out_shape=jax.ShapeDtypeStruct((R, C), out_dtype),
        compiler_params=_params("parallel"),
    )(x)


_ANY = pl.BlockSpec(memory_space=pl.ANY)
GROUP_ALL = ([(0, 0, 1), (0, 1, 0), (0, 1, 1), (1, 0, 0), (1, 0, 1), (1, 1, 0), (1, 1, 1)],
             lambda d: 4 * d[0] + 2 * d[1] + d[2])
GROUP_CHIPS = ([(0, 1, 0), (1, 0, 0), (1, 1, 0)], lambda d: 2 * d[0] + d[1])
GROUP_SIBLING = ([(0, 0, 1)], lambda d: d[2])


def _me():
    return lax.axis_index("x"), lax.axis_index("y"), lax.axis_index("c")


def _peer(me, flip):
    return tuple(1 - a if f else a for a, f in zip(me, flip))


def _group_exchange(x, name, group, scatter, chunks=1):
    flips, slot = group
    n = len(flips) + 1
    R, C = x.shape[-2:]
    nc = max([k for k in range(1, chunks + 1) if R % (16 * k) == 0] or [1])
    rc = R // nc

    def body(x_ref, o_ref, send_sems, recv_sems, local_sem):
        me = _me()
        mine = pltpu.make_async_copy(x_ref.at[slot(me)] if scatter else x_ref, o_ref.at[slot(me)], local_sem)
        mine.start()
        sends = []
        for k, flip in enumerate(flips):
            peer = _peer(me, flip)
            src = x_ref.at[slot(peer)] if scatter else x_ref
            for j in range(nc):
                part = pl.ds(j * rc, rc)
                cp = pltpu.make_async_remote_copy(
                    src_ref=src.at[part], dst_ref=o_ref.at[slot(me), part], send_sem=send_sems.at[k * nc + j],
                    recv_sem=recv_sems.at[k * nc + j], device_id=peer, device_id_type=MESH_ID)
                cp.start()
                sends.append(cp)
        for k, flip in enumerate(flips):
            peer = _peer(me, flip)
            for j in range(nc):
                part = pl.ds(j * rc, rc)
                pltpu.make_async_remote_copy(
                    src_ref=o_ref.at[slot(me), part], dst_ref=o_ref.at[slot(peer), part],
                    send_sem=send_sems.at[k * nc + j], recv_sem=recv_sems.at[k * nc + j],
                    device_id=peer, device_id_type=MESH_ID).wait_recv()
        for cp in sends:
            cp.wait_send()
        mine.wait()

    return pl.pallas_call(
        body, name=name, in_specs=[_ANY], out_specs=_ANY,
        out_shape=jax.ShapeDtypeStruct((n, R, C), x.dtype),
        scratch_shapes=[pltpu.SemaphoreType.DMA(((n - 1) * nc,)), pltpu.SemaphoreType.DMA(((n - 1) * nc,)),
                        pltpu.SemaphoreType.DMA],
    )(x)


def group_gather(x, name, group, chunks=1):
    return _group_exchange(x, name, group, scatter=False, chunks=chunks)


def group_scatter(x, name, group, chunks=1):
    return _group_exchange(x, name, group, scatter=True, chunks=chunks)


WEIGHTS = ['a_norm_g', 'a_w_in', 'a_b_if', 'a_hnorm_g', 'a_w_out', 'kv_norm_g', 'w_kv', 'b_norm_g', 'b_w_q',
           'b_w_out', 'rel_bias', 'f_norm_g', 'f_w_up', 'f_conv_w', 'f_conv_b', 'f_w_down', 'final_norm_g']
SHARD_AXIS = {'a_norm_g': 1, 'a_w_in': 2, 'a_b_if': None, 'a_hnorm_g': 2, 'a_w_out': 1, 'kv_norm_g': None,
              'w_kv': 1, 'b_norm_g': None, 'b_w_q': 2, 'b_w_out': 1, 'rel_bias': None, 'f_norm_g': None,
              'f_w_up': 2, 'f_conv_w': 2, 'f_conv_b': None, 'f_w_down': 1, 'final_norm_g': None}
BIG = ['a_w_in', 'a_w_out', 'w_kv', 'b_w_q', 'b_w_out', 'f_w_up', 'f_w_down']
SMALL = [n for n in WEIGHTS if n not in BIG]
LANES = 1024


def _pad_rows(flat, mult):
    n = flat.shape[0]
    per = LANES * mult
    tot = -(-n // per) * per
    return jnp.pad(flat, (0, tot - n)).reshape(tot // LANES, LANES)


def _full_from_shards(sh, axis):
    return jnp.concatenate([sh[j] for j in range(4)], axis=axis)


def _shards_from_full(full, axis):
    return jnp.stack(jnp.split(full, 4, axis=axis))


def _local_step(x, target, W):
    T = x.shape[0]
    row = lambda a: a.reshape(1, -1).astype(F32)
    w_in = jnp.pad(W['a_w_in'][0], ((0, 0), (0, A_IN_PAD - A_IN)))
    bias128 = jnp.pad(row(W['a_b_if'][0]), ((0, 0), (0, 120)))
    hng = row(W['a_hnorm_g'][0])
    w_up = [_interleave(W['f_w_up'][l]) for l in range(2)]
    cw = [_interleave(W['f_conv_w'][l].astype(F32)) for l in range(2)]
    cb = [_interleave(row(W['f_conv_b'][l])) for l in range(2)]
    onehots = [(jnp.asarray(_group_bucket(g).reshape(-1, 1)) == jnp.arange(128)[None, :]).astype(F32)
               for g in range(N_GROUPS)]
    rb_t = jnp.pad(W['rel_bias'].astype(F32).T, ((0, 0), (0, 128 - REL_BUCKETS)))
    biases = [mm_nn(rb_t[g * B_HEADS:(g + 1) * B_HEADS], onehots[g].T, f"rel_bias_table_g{g}", exact=True)
              .reshape(B_HEADS, B_BLOCK, 2 * B_BLOCK) for g in range(N_GROUPS)]
    G = {}

    def ffn_fwd(xin, l):
        xn, = rms_fwd(xin, [row(W['f_norm_g'][l])], f"ffn{l}_norm")
        u, act = ffn_up_act(xn, w_up[l], cw[l], cb[l], f"ffn{l}_up_act")
        return mm_nn(act, W['f_w_down'][l], f"ffn{l}_down", res=xin), (xn, u, act)

    def ffn_bwd(xin, saved, dout, l):
        xn, u, act = saved
        gd = mm_tn(act, dout, f"ffn{l}_gdown")
        du, gcw, gcb = conv_act_bwd(u, dout, W['f_w_down'][l].T, cw[l], cb[l], f"ffn{l}_dact")
        dxn = mm_nn(du, w_up[l].T, f"ffn{l}_dup")
        gu = _deinterleave(mm_tn(xn, du, f"ffn{l}_gup"))
        dxin, (gn,) = rms_bwd(xin, dout, [(dxn, row(W['f_norm_g'][l]))], f"ffn{l}_dnorm")
        return dxin, gd, gu, _deinterleave(gcw), _deinterleave(gcb), gn

    xn_a, = rms_fwd(x, [row(W['a_norm_g'][0])], "a_norm")
    z = mm_nn(xn_a, w_in, "a_in")
    gcol, grow = gate_prep(z, bias128)
    hg, Cs, ns, ms = mlstm_fwd(z, gcol, grow, hng)
    x1 = mm_nn(hg, W['a_w_out'][0], "a_out", res=x)
    x2, ffn0 = ffn_fwd(x1, 0)
    xn_kv, xn_b = rms_fwd(x2, [row(W['kv_norm_g']), row(W['b_norm_g'][0])], "b_norms")
    gcols = lambda w, c: w[:, c * 1024:(c + 1) * 1024]
    qv = [mm_view(xn_b, gcols(W['b_w_q'][0], g), f"q_proj_g{g}", DILATIONS[g]) for g in range(N_GROUPS)]
    kvw = [mm_view(xn_kv, gcols(W['w_kv'], g), f"k_proj_g{g}", DILATIONS[g]) for g in range(N_GROUPS)]
    vvw = [mm_view(xn_kv, gcols(W['w_kv'], 3 + g), f"v_proj_g{g}", DILATIONS[g]) for g in range(N_GROUPS)]
    os_, lses = zip(*[attn_fwd(qv[g], kvw[g], vvw[g], biases[g], g) for g in range(N_GROUPS)])
    att, att_f, *lse_v = attn_merge(os_, lses)
    x3 = mm_nn(att, W['b_w_out'][0], "b_out", res=x2)
    x4, ffn1 = ffn_fwd(x3, 1)
    dx4, g_final, loss = loss_head(x4, target, row(W['final_norm_g']))
    G['final_norm_g'] = g_final.reshape(-1)

    dx3, gd1, gu1, gcw1, gcb1, gn1 = ffn_bwd(x3, ffn1, dx4, 1)
    datt = mm_nn(dx3, W['b_w_out'][0].T, "b_dout")
    G['b_w_out'] = mm_tn(att, dx3, "b_gout")[None]
    prep = attn_prep(datt, att_f)
    do_v, dl_v = prep[:3], prep[3:]
    parts = [attn_bwd(qv[g], kvw[g], vvw[g], biases[g], do_v[g], lse_v[g], dl_v[g], g) for g in range(N_GROUPS)]
    dq_all, dkv = attn_combine(parts)
    grb = []
    for g in range(N_GROUPS):
        gb = mm_nn(parts[g][5].reshape(B_HEADS, -1), onehots[g], f"rel_bias_g{g}", exact=True)
        grb.append(gb[:, :REL_BUCKETS].T)
    G['rel_bias'] = jnp.concatenate(grb, axis=1)
    dxn_b = mm_nn(dq_all, W['b_w_q'][0].T, "q_dproj")
    G['b_w_q'] = mm_tn(xn_b, dq_all, "q_gproj")[None]
    dxn_kv = mm_nn(dkv, W['w_kv'].T, "kv_dproj")
    G['w_kv'] = mm_tn(xn_kv, dkv, "kv_gproj")
    dx2, (g_kvn, g_bn) = rms_bwd(x2, dx3, [(dxn_kv, row(W['kv_norm_g'])), (dxn_b, row(W['b_norm_g'][0]))],
                                 "b_dnorms")
    G['kv_norm_g'] = g_kvn.reshape(-1)
    G['b_norm_g'] = g_bn
    dx1, gd0, gu0, gcw0, gcb0, gn0 = ffn_bwd(x1, ffn0, dx2, 0)
    G['f_w_down'] = jnp.stack([gd0, gd1])
    G['f_w_up'] = jnp.stack([gu0, gu1])
    G['f_conv_w'] = jnp.stack([gcw0, gcw1])
    G['f_conv_b'] = jnp.concatenate([gcb0, gcb1], axis=0)
    G['f_norm_g'] = jnp.concatenate([gn0, gn1], axis=0)
    dhg = mm_nn(dx1, W['a_w_out'][0].T, "a_dout")
    G['a_w_out'] = mm_tn(hg, dx1, "a_gout")[None]
    dz, g_hn, g_bif = mlstm_bwd(z, gcol, grow, hng, bias128, Cs, ns, ms, dhg)
    G['a_hnorm_g'] = g_hn.reshape(1, A_HEADS, A_V)
    G['a_b_if'] = g_bif[:, :2 * A_HEADS]
    dxn_a = mm_nn(dz, w_in.T, "a_din")
    G['a_w_in'] = mm_tn(xn_a, dz, "a_gin")[:, :A_IN][None]
    grad_x, (g_an,) = rms_bwd(x, dx1, [(dxn_a, row(W['a_norm_g'][0]))], "a_dnorm")
    G['a_norm_g'] = g_an
    return loss, grad_x, G


def kernel(x, a_norm_g, a_w_in, a_b_if, a_hnorm_g, a_w_out, kv_norm_g, w_kv, b_norm_g, b_w_q, b_w_out, rel_bias, f_norm_g, f_w_up, f_conv_w, f_conv_b, f_w_down, final_norm_g, loss_target, m_a_norm_g, m_a_w_in, m_a_b_if, m_a_hnorm_g, m_a_w_out, m_kv_norm_g, m_w_kv, m_b_norm_g, m_b_w_q, m_b_w_out, m_rel_bias, m_f_norm_g, m_f_w_up, m_f_conv_w, m_f_conv_b, m_f_w_down, m_final_norm_g, v_a_norm_g, v_a_w_in, v_a_b_if, v_a_hnorm_g, v_a_w_out, v_kv_norm_g, v_w_kv, v_b_norm_g, v_b_w_q, v_b_w_out, v_rel_bias, v_f_norm_g, v_f_w_up, v_f_conv_w, v_f_conv_b, v_f_w_down, v_final_norm_g):
    given = dict(locals())
    shard = {n: given[n] for n in WEIGHTS}
    mom = {n: given["m_" + n] for n in WEIGHTS}
    var = {n: given["v_" + n] for n in WEIGHTS}
    cx, cy, cc = _me()
    chip = 2 * cx + cy

    halves = [shard[n].astype(MM_DTYPE).reshape(2, -1, LANES) for n in BIG]
    sizes = [h.shape[1] for h in halves]
    mine = lax.dynamic_index_in_dim(jnp.concatenate(halves, axis=1), cc, axis=0, keepdims=False)
    fill = -mine.shape[0] % 16
    mine = jnp.pad(mine, ((0, fill), (0, 0)))
    rows = mine.shape[0]
    gathered = group_gather(mine, "gather_weights", GROUP_ALL).reshape(4, 2, rows, LANES)
    W = {}
    off = 0
    for n, sz in zip(BIG, sizes):
        W[n] = _full_from_shards(gathered[:, :, off:off + sz].reshape((4,) + shard[n].shape), SHARD_AXIS[n])
        off += sz
    sharded_small = [n for n in SMALL if SHARD_AXIS[n] is not None]
    ssz = [shard[n].size for n in sharded_small]
    sflat = jnp.concatenate([shard[n].reshape(-1) for n in sharded_small])
    sg = group_gather(_pad_rows(sflat, 8), "gather_small", GROUP_CHIPS).reshape(4, -1)
    off = 0
    for n, sz in zip(sharded_small, ssz):
        W[n] = _full_from_shards(sg[:, off:off + sz].reshape((4,) + shard[n].shape), SHARD_AXIS[n])
        off += sz
    for n in SMALL:
        if SHARD_AXIS[n] is None:
            W[n] = shard[n]

    loss_row, grad_x, G = _local_step(x[0], loss_target[0], W)

    slots = jnp.concatenate(
        [_shards_from_full(G[n], SHARD_AXIS[n]).reshape(4, 2, -1, LANES).astype(GRAD_WIRE_DTYPE) for n in BIG]
        + [jnp.zeros((4, 2, fill, LANES), GRAD_WIRE_DTYPE)], axis=2).reshape(8, rows, LANES)
    reduced = sum_slots(group_scatter(slots, "scatter_grads", GROUP_ALL), "sum_grads", GRAD_WIRE_DTYPE)
    both = group_gather(reduced, "join_halves", GROUP_SIBLING)
    gsh = {}
    off = 0
    for n, sz in zip(BIG, sizes):
        gsh[n] = both[:, off:off + sz].reshape(shard[n].shape).astype(F32)
        off += sz
    small_parts = [loss_row[0, 0:1]] + [G[n].reshape(-1) for n in SMALL]
    small_sz = [p.shape[0] for p in small_parts]
    small = sum_slots(group_gather(_pad_rows(jnp.concatenate(small_parts), 8), "gather_small_grads", GROUP_ALL),
                      "sum_small_grads").reshape(-1)
    loss = small[0]
    off = 1
    for n, sz in zip(SMALL, small_sz[1:]):
        full = small[off:off + sz].reshape(W[n].shape)
        off += sz
        if SHARD_AXIS[n] is None:
            gsh[n] = full
        else:
            gsh[n] = lax.dynamic_index_in_dim(_shards_from_full(full, SHARD_AXIS[n]), chip, 0, keepdims=False)

    delta, new_m, new_v = {}, {}, {}
    for n in WEIGHTS:
        shp = shard[n].shape
        two = lambda a: a.reshape(-1, shp[-1])
        d, nm, nv = adamw(two(shard[n]), two(gsh[n]), two(mom[n]), two(var[n]), f"adamw_{n}")
        delta[n], new_m[n], new_v[n] = d.reshape(shp), nm.reshape(shp), nv.reshape(shp)
    return (loss, grad_x[None], *[gsh[n] for n in WEIGHTS], *[delta[n] for n in WEIGHTS],
            *[new_m[n] for n in WEIGHTS], *[new_v[n] for n in WEIGHTS])
```

```python
import functools
import math

import numpy as np
import jax
import jax.numpy as jnp
from jax import lax
from jax.experimental import pallas as pl
from jax.experimental.pallas import tpu as pltpu

F32 = jnp.float32
BF16 = jnp.bfloat16
MM_DTYPE = jnp.bfloat16
GRAD_WIRE_DTYPE = jnp.bfloat16
HI = lax.Precision.HIGHEST

D_MODEL = 1024
A_HEADS = 4
A_QK = 128
A_V = 256
A_CHUNK = 64
A_IN = 3080
A_IN_PAD = 3200
GATE_COL = 3072
SOFTCAP = 15.0
N_GROUPS = 3
B_HEADS = 16
B_DH = 64
B_BLOCK = 128
DILATIONS = (1, 4, 16)
WINDOWS = (128, 512, 2048)
REL_BUCKETS = 32
REL_MAX_DIST = 2048
D_FF = 2816
FF_TC = 256
EPS = 1e-6
ADAM_LR, ADAM_B1, ADAM_B2, ADAM_EPS, ADAM_WD, ADAM_STEP = 0.001, 0.9, 0.999, 1e-08, 0.01, 10

VMEM_LIMIT = 56 * 1024 * 1024
NT_DIMS = (((1,), (1,)), ((), ()))
TN_DIMS = (((0,), (0,)), ((), ()))
MESH_ID = pl.DeviceIdType.MESH


def _params(*sem):
    return pltpu.CompilerParams(dimension_semantics=sem, vmem_limit_bytes=VMEM_LIMIT)


def _tile(n, cap):
    if n <= cap:
        return n
    best = None
    for t in range(128, cap + 1, 128):
        if n % t == 0:
            best = t
    assert best is not None, (n, cap)
    return best


def _rows(n, cap):
    if n <= cap:
        return n
    for t in range(cap // 8 * 8, 7, -8):
        if n % t == 0:
            return t
    raise ValueError((n, cap))


def _dot(a, b):
    return jnp.dot(a.astype(MM_DTYPE), b.astype(MM_DTYPE), preferred_element_type=F32)


def _dot_nt(a, b):
    return lax.dot_general(a.astype(MM_DTYPE), b.astype(MM_DTYPE), NT_DIMS, preferred_element_type=F32)


def _dot_tn(a, b):
    return lax.dot_general(a.astype(MM_DTYPE), b.astype(MM_DTYPE), TN_DIMS, preferred_element_type=F32)


def _sigmoid(x):
    return 1.0 / (1.0 + jnp.exp(-x))


def _sigmoid_tanh(x):
    return 0.5 * jnp.tanh(0.5 * x) + 0.5


def mm_nn(a, b, name, res=None, out_dtype=F32, exact=False, ride=None):
    M, K = a.shape
    N = b.shape[1]
    tm, tn = _rows(M, 512), _tile(N, 1536)
    whole_k = (2 * (tm * K * a.dtype.itemsize + K * tn * b.dtype.itemsize)
               + 2 * tm * tn * 4 * (1 if res is None else 2))
    tk = K if whole_k <= 46 * 1024 * 1024 else _tile(K, 1536)
    nk = K // tk

    def body(*refs):
        if res is None:
            a_ref, b_ref, o_ref, acc = refs
            r_ref = None
        else:
            a_ref, b_ref, r_ref, o_ref, acc = refs
        if exact:
            p = jnp.dot(a_ref[...], b_ref[...], precision=HI, preferred_element_type=F32)
        else:
            p = _dot(a_ref[...], b_ref[...])

        def finish(total):
            if r_ref is not None:
                total = total + r_ref[...]
            o_ref[...] = total.astype(out_dtype)

        if nk == 1:
            finish(p)
        else:
            k = pl.program_id(2)

            @pl.when(k == 0)
            def _():
                acc[...] = p

            @pl.when(jnp.logical_and(k > 0, k < nk - 1))
            def _():
                acc[...] += p

            @pl.when(k == nk - 1)
            def _():
                finish(acc[...] + p)

    in_specs = [pl.BlockSpec((tm, tk), lambda j, i, k: (i, k)),
                pl.BlockSpec((tk, tn), lambda j, i, k: (k, j))]
    args = [a, b]
    if res is not None:
        in_specs.append(pl.BlockSpec((tm, tn), lambda j, i, k: (i, j)))
        args.append(res)
    acc_shape = (tm, tn) if nk > 1 else (8, 128)
    outs = _call(body, name, (N // tn, M // tm, nk), in_specs, [pl.BlockSpec((tm, tn), lambda j, i, k: (i, j))],
                 [jax.ShapeDtypeStruct((M, N), out_dtype)], [pltpu.VMEM(acc_shape, F32)],
                 ("parallel", "parallel", "arbitrary"), args, ride)
    return outs[0] if ride is None else outs


def mm_view(a, b, name, dil):
    T, K = a.shape
    tm = 512

    def body(a_ref, b_ref, o_ref, sc):
        p = _dot(a_ref[...], b_ref[...])
        if dil == 1:
            o_ref[...] = p.astype(o_ref.dtype)
        else:
            _to_view(lambda c: p[:, c * 128:(c + 1) * 128], sc, o_ref, dil, 8, tm)

    return pl.pallas_call(
        body, name=name, grid=(T // tm,),
        in_specs=[pl.BlockSpec((tm, K), lambda i: (i, 0)), pl.BlockSpec((K, 1024), lambda i: (0, 0))],
        out_specs=pl.BlockSpec((tm // dil, dil * 1024), lambda i: (i, 0)),
        out_shape=jax.ShapeDtypeStruct((T // dil, dil * 1024), MM_DTYPE),
        scratch_shapes=[pltpu.VMEM((8, tm, 128), F32)],
        compiler_params=_params("parallel"),
    )(a, b)


def mm_tn(a, g, name):
    T, Ka = a.shape
    N = g.shape[1]
    tka, tn, tt = _tile(Ka, 1536), _tile(N, 1536), _rows(T, 1024)
    nt = T // tt

    def body(a_ref, g_ref, o_ref):
        t = pl.program_id(2)
        p = _dot_tn(a_ref[...], g_ref[...])

        @pl.when(t == 0)
        def _():
            o_ref[...] = p

        @pl.when(t > 0)
        def _():
            o_ref[...] += p

    return pl.pallas_call(
        body, name=name, grid=(Ka // tka, N // tn, nt),
        in_specs=[pl.BlockSpec((tt, tka), lambda i, j, t: (t, i)),
                  pl.BlockSpec((tt, tn), lambda i, j, t: (t, j))],
        out_specs=pl.BlockSpec((tka, tn), lambda i, j, t: (i, j)),
        out_shape=jax.ShapeDtypeStruct((Ka, N), F32),
        compiler_params=_params("parallel", "parallel", "arbitrary"),
    )(a, g)


def rms_fwd(x, gains, name):
    T, D = x.shape
    tt = _rows(T, 512)
    ng = len(gains)

    def body(*refs):
        x_ref = refs[0]
        g_refs = refs[1:1 + ng]
        o_refs = refs[1 + ng:]
        xf = x_ref[...]
        y = xf * lax.rsqrt(jnp.mean(xf * xf, axis=-1, keepdims=True) + EPS)
        for g_ref, o_ref in zip(g_refs, o_refs):
            o_ref[...] = (y * g_ref[...]).astype(o_ref.dtype)

    row = pl.BlockSpec((tt, D), lambda i: (i, 0))
    gsp = pl.BlockSpec((1, D), lambda i: (0, 0))
    return pl.pallas_call(
        body, name=name, grid=(T // tt,),
        in_specs=[row] + [gsp] * ng, out_specs=[row] * ng,
        out_shape=[jax.ShapeDtypeStruct((T, D), MM_DTYPE)] * ng,
        compiler_params=_params("parallel"),
    )(x, *gains)


def rms_bwd(x, dres, branches, name):
    T, D = x.shape
    tt = _rows(T, 256)
    nb = len(branches)

    def body(*refs):
        x_ref, r_ref = refs[0], refs[1]
        dy_refs = refs[2:2 + nb]
        g_refs = refs[2 + nb:2 + 2 * nb]
        dx_ref = refs[2 + 2 * nb]
        dg_refs = refs[3 + 2 * nb:]
        i = pl.program_id(0)
        xf = x_ref[...]
        r = lax.rsqrt(jnp.mean(xf * xf, axis=-1, keepdims=True) + EPS)
        xh = xf * r
        dx = r_ref[...]
        for dy_ref, g_ref, dg_ref in zip(dy_refs, g_refs, dg_refs):
            dy = dy_ref[...].astype(F32)
            dyg = dy * g_ref[...]
            dx = dx + r * (dyg - xh * jnp.mean(dyg * xh, axis=-1, keepdims=True))
            part = jnp.sum(dy * xh, axis=0, keepdims=True)

            @pl.when(i == 0)
            def _():
                dg_ref[...] = part

            @pl.when(i > 0)
            def _():
                dg_ref[...] += part
        dx_ref[...] = dx

    row = pl.BlockSpec((tt, D), lambda i: (i, 0))
    gsp = pl.BlockSpec((1, D), lambda i: (0, 0))
    outs = pl.pallas_call(
        body, name=name, grid=(T // tt,),
        in_specs=[row, row] + [row] * nb + [gsp] * nb,
        out_specs=[row] + [gsp] * nb,
        out_shape=[jax.ShapeDtypeStruct((T, D), F32)] + [jax.ShapeDtypeStruct((1, D), F32)] * nb,
        compiler_params=_params("arbitrary"),
    )(x, dres, *[b[0] for b in branches], *[b[1] for b in branches])
    return outs[0], outs[1:]


def loss_head(x, target, gain):
    T, D = x.shape
    tt = _rows(T, 256)

    def body(x_ref, t_ref, g_ref, dx_ref, dg_ref, loss_ref):
        i = pl.program_id(0)
        xf = x_ref[...]
        g = g_ref[...]
        r = lax.rsqrt(jnp.mean(xf * xf, axis=-1, keepdims=True) + EPS)
        xh = xf * r
        e = xh * g - t_ref[...]
        lpart = 0.5 * jnp.sum(jnp.sum(e * e, axis=1, keepdims=True), axis=0, keepdims=True) / D
        dy = e / D
        dyg = dy * g
        dx_ref[...] = r * (dyg - xh * jnp.mean(dyg * xh, axis=-1, keepdims=True))
        gpart = jnp.sum(dy * xh, axis=0, keepdims=True)
        lrow = jnp.broadcast_to(lpart, (1, 128))

        @pl.when(i == 0)
        def _():
            dg_ref[...] = gpart
            loss_ref[...] = lrow

        @pl.when(i > 0)
        def _():
            dg_ref[...] += gpart
            loss_ref[...] += lrow

    row = pl.BlockSpec((tt, D), lambda i: (i, 0))
    gsp = pl.BlockSpec((1, D), lambda i: (0, 0))
    return pl.pallas_call(
        body, name="loss_head", grid=(T // tt,),
        in_specs=[row, row, gsp],
        out_specs=[row, gsp, pl.BlockSpec((1, 128), lambda i: (0, 0))],
        out_shape=[jax.ShapeDtypeStruct((T, D), F32), jax.ShapeDtypeStruct((1, D), F32),
                   jax.ShapeDtypeStruct((1, 128), F32)],
        compiler_params=_params("arbitrary"),
    )(x, target, gain)


def _shift_down(u, prev8, first, k):
    rolled = pltpu.roll(u, k, 0)
    rid = lax.broadcasted_iota(jnp.int32, u.shape, 0)
    halo = jnp.where(first, 0.0, prev8)
    out = rolled
    for j in range(k):
        out = jnp.where(rid == j, halo[8 - k + j:8 - k + j + 1, :], out)
    return out


def _conv3(u, prev8, first, w, b):
    return (_shift_down(u, prev8, first, 2) * w[0:1, :] + _shift_down(u, prev8, first, 1) * w[1:2, :]
            + u * w[2:3, :] + b)


def ffn_up_act(xn, w_up, w, b, name, ride=None):
    T, K = xn.shape
    tt = _rows(T, 512)
    nj = D_FF // FF_TC

    def body(x_ref, wu_ref, w_ref, b_ref, u_ref, o_ref, tail):
        first = pl.program_id(1) == 0
        u = _dot(x_ref[...], wu_ref[...])
        u_ref[...] = u
        c = _conv3(u, tail[...], first, w_ref[...], b_ref[...])
        tail[...] = u[tt - 8:, :]
        cg, cv = c[:, :FF_TC], c[:, FF_TC:]
        o_ref[...] = (cg * _sigmoid_tanh(cg) * cv).astype(o_ref.dtype)

    return _call(
        body, name, (nj, T // tt),
        [pl.BlockSpec((tt, K), lambda j, i: (i, 0)),
         pl.BlockSpec((K, 2 * FF_TC), lambda j, i: (0, j)),
         pl.BlockSpec((3, 2 * FF_TC), lambda j, i: (0, j)),
         pl.BlockSpec((1, 2 * FF_TC), lambda j, i: (0, j))],
        [pl.BlockSpec((tt, 2 * FF_TC), lambda j, i: (i, j)), pl.BlockSpec((tt, FF_TC), lambda j, i: (i, j))],
        [jax.ShapeDtypeStruct((T, 2 * D_FF), F32), jax.ShapeDtypeStruct((T, D_FF), MM_DTYPE)],
        [pltpu.VMEM((8, 2 * FF_TC), F32)], ("parallel", "arbitrary"), (xn, w_up, w, b), ride)


def conv_act_bwd(u, da, w, b, name, ride=None):
    T = u.shape[0]
    tt = _rows(T, 512)
    nt = T // tt
    nj = D_FF // FF_TC
    te = tt + 8

    def body(u_ref, p_ref, n_ref, da_ref, dan_ref, w_ref, b_ref, du_ref, dw_ref, db_ref):
        i = pl.program_id(1)
        first = i == 0
        last = i == nt - 1
        w = w_ref[...]
        ue = jnp.concatenate([u_ref[...], n_ref[...]], axis=0)
        dae = jnp.concatenate([da_ref[...], jnp.where(last, 0.0, dan_ref[...])], axis=0)
        um2 = _shift_down(ue, p_ref[...], first, 2)
        um1 = _shift_down(ue, p_ref[...], first, 1)
        c = um2 * w[0:1, :] + um1 * w[1:2, :] + ue * w[2:3, :] + b_ref[...]
        cg, cv = c[:, :FF_TC], c[:, FF_TC:]
        s = _sigmoid_tanh(cg)
        dcg = dae * cv * (s * (1.0 + cg * (1.0 - s)))
        dcv = dae * (cg * s)
        dc = jnp.concatenate([dcg, dcv], axis=1)
        du = (dc * w[2:3, :] + pltpu.roll(dc, te - 1, 0) * w[1:2, :] + pltpu.roll(dc, te - 2, 0) * w[0:1, :])
        du_ref[...] = du[:tt, :].astype(du_ref.dtype)
        dcm = dc[:tt, :]
        dwp = jnp.concatenate([jnp.sum(dcm * um2[:tt, :], axis=0, keepdims=True),
                               jnp.sum(dcm * um1[:tt, :], axis=0, keepdims=True),
                               jnp.sum(dcm * ue[:tt, :], axis=0, keepdims=True)], axis=0)
        dbp = jnp.sum(dcm, axis=0, keepdims=True)

        @pl.when(first)
        def _():
            dw_ref[...] = dwp
            db_ref[...] = dbp

        @pl.when(i > 0)
        def _():
            dw_ref[...] += dwp
            db_ref[...] += dbp

    nb8 = T // 8
    return _call(
        body, name, (nj, nt),
        [pl.BlockSpec((tt, 2 * FF_TC), lambda j, i: (i, j)),
         pl.BlockSpec((8, 2 * FF_TC), lambda j, i: (jnp.maximum(i * (tt // 8) - 1, 0), j)),
         pl.BlockSpec((8, 2 * FF_TC), lambda j, i: (jnp.minimum((i + 1) * (tt // 8), nb8 - 1), j)),
         pl.BlockSpec((tt, FF_TC), lambda j, i: (i, j)),
         pl.BlockSpec((8, FF_TC), lambda j, i: (jnp.minimum((i + 1) * (tt // 8), nb8 - 1), j)),
         pl.BlockSpec((3, 2 * FF_TC), lambda j, i: (0, j)),
         pl.BlockSpec((1, 2 * FF_TC), lambda j, i: (0, j))],
        [pl.BlockSpec((tt, 2 * FF_TC), lambda j, i: (i, j)),
         pl.BlockSpec((3, 2 * FF_TC), lambda j, i: (0, j)),
         pl.BlockSpec((1, 2 * FF_TC), lambda j, i: (0, j))],
        [jax.ShapeDtypeStruct((T, 2 * D_FF), MM_DTYPE), jax.ShapeDtypeStruct((3, 2 * D_FF), F32),
         jax.ShapeDtypeStruct((1, 2 * D_FF), F32)],
        [], ("parallel", "arbitrary"), (u, u, u, da, da, w, b), ride)


def _interleave(a):
    lead = a.shape[:-1]
    nj = D_FF // FF_TC
    return jnp.swapaxes(a.reshape(*lead, 2, nj, FF_TC), -3, -2).reshape(*lead, 2 * D_FF)


def _deinterleave(a):
    lead = a.shape[:-1]
    nj = D_FF // FF_TC
    return jnp.swapaxes(a.reshape(*lead, nj, 2, FF_TC), -3, -2).reshape(*lead, 2 * D_FF)


A_GC = 2
A_TB = A_GC * A_CHUNK


def gate_prep(z, bias128):
    T = z.shape[0]
    tt = _rows(T, 512)

    def body(z_ref, b_ref, gc_ref, gr_ref):
        pre = z_ref[...] + b_ref[...]
        sc = SOFTCAP * jnp.tanh(pre / SOFTCAP)
        lf = jnp.minimum(sc, 0.0) - jnp.log(1.0 + jnp.exp(-jnp.abs(sc)))
        col = lax.broadcasted_iota(jnp.int32, pre.shape, 1)
        isf = jnp.logical_and(col >= A_HEADS, col < 2 * A_HEADS)
        r = lax.broadcasted_iota(jnp.int32, (tt, tt), 0)
        c = lax.broadcasted_iota(jnp.int32, (tt, tt), 1)
        tri = jnp.logical_and(jnp.right_shift(r, 6) == jnp.right_shift(c, 6), c <= r).astype(F32)
        bcum = jnp.dot(tri, jnp.where(isf, lf, 0.0), precision=HI, preferred_element_type=F32)
        g = jnp.where(col < A_HEADS, sc, jnp.where(isf, bcum, 0.0))
        gc_ref[...] = g
        for s in range(tt // 128):
            gr_ref[s] = g[s * 128:(s + 1) * 128, :].T[0:8, :]

    return pl.pallas_call(
        body, name="gate_prep", grid=(T // tt,),
        in_specs=[pl.BlockSpec((tt, 128), lambda i: (i, GATE_COL // 128)),
                  pl.BlockSpec((1, 128), lambda i: (0, 0))],
        out_specs=[pl.BlockSpec((tt, 128), lambda i: (i, 0)),
                   pl.BlockSpec((tt // 128, 8, 128), lambda i: (i, 0, 0))],
        out_shape=[jax.ShapeDtypeStruct((T, 128), F32), jax.ShapeDtypeStruct((T // 128, 8, 128), F32)],
        compiler_params=_params("parallel"),
    )(z, bias128)


def _chunk_decay(A, qh, bc, br, lir, n, m, causal):
    logD = jnp.where(causal, bc - br + lir, -jnp.inf)
    m_inter = bc + m
    m_t = jnp.maximum(m_inter, jnp.max(logD, axis=1, keepdims=True))
    E = jnp.exp(logD - m_t)
    Sm = A * E
    wi = jnp.exp(m_inter - m_t)
    qn = jnp.sum(qh.astype(F32) * n, axis=1, keepdims=True)
    den = jnp.sum(Sm, axis=1, keepdims=True) + wi * qn
    gs = jnp.maximum(jnp.abs(den), jnp.exp(-m_t))
    return E, Sm, wi, den, gs, m_t


def _state_weights(bc, lic, br, lir, m):
    bL = bc[A_CHUNK - 1:A_CHUNK, :]
    m_new = jnp.maximum(bL + m, jnp.max(bL - br + lir, axis=1, keepdims=True))
    wk = jnp.exp(bL - bc + lic - m_new)
    decay = jnp.exp(bL + m - m_new)
    return wk, decay, m_new


def _head_slices(h):
    return (slice(h * A_QK, (h + 1) * A_QK), slice(h * A_V, (h + 1) * A_V))


def mlstm_fwd(z, gcol, grow, hng):
    T = z.shape[0]
    NC = T // A_CHUNK
    scale = A_QK ** -0.5

    def body(q_ref, k_ref, v_ref, o_ref, gc_ref, gr_ref, hng_ref, hg_ref, Cs_ref, ns_ref, ms_ref,
             C_sc, n_sc, m_sc):
        @pl.when(pl.program_id(0) == 0)
        def _():
            C_sc[...] = jnp.zeros_like(C_sc)
            n_sc[...] = jnp.zeros_like(n_sc)
            m_sc[...] = jnp.zeros_like(m_sc)

        ri = lax.broadcasted_iota(jnp.int32, (A_CHUNK, A_CHUNK), 0)
        ci = lax.broadcasted_iota(jnp.int32, (A_CHUNK, A_CHUNK), 1)
        causal = ri >= ci
        gr = gr_ref[0]
        for c in range(A_GC):
            rows = slice(c * A_CHUNK, (c + 1) * A_CHUNK)
            gc = gc_ref[rows, :]
            grc = gr[:, c * A_CHUNK:(c + 1) * A_CHUNK]
            for h in range(A_HEADS):
                sk, sv = _head_slices(h)
                qh = (q_ref[rows, sk] * scale).astype(MM_DTYPE)
                kh = k_ref[rows, sk].astype(MM_DTYPE)
                vh = v_ref[rows, sv].astype(MM_DTYPE)
                lic, bc = gc[:, h:h + 1], gc[:, A_HEADS + h:A_HEADS + h + 1]
                lir, br = grc[h:h + 1, :], grc[A_HEADS + h:A_HEADS + h + 1, :]
                C, n, m = C_sc[h], n_sc[h], m_sc[h][:, 0:1]
                Cs_ref[c, h] = C
                ns_ref[c, h] = n
                ms_ref[c, h] = m_sc[h]
                _, Sm, wi, _, gs, _ = _chunk_decay(_dot_nt(qh, kh), qh, bc, br, lir, n, m, causal)
                hh = (_dot(Sm, vh) + wi * _dot(qh, C)) / gs
                hn = hh * lax.rsqrt(jnp.mean(hh * hh, axis=1, keepdims=True) + EPS) * hng_ref[:, sv]
                hg_ref[rows, sv] = (hn * _sigmoid(o_ref[rows, sv])).astype(hg_ref.dtype)
                wk, decay, m_new = _state_weights(bc, lic, br, lir, m)
                kw = kh.astype(F32) * wk
                C_sc[h] = decay * C + _dot_tn(kw, vh)
                n_sc[h] = decay * n + jnp.sum(kw, axis=0, keepdims=True)
                m_sc[h] = jnp.broadcast_to(m_new, (1, 128))

    tok = lambda w, cb: pl.BlockSpec((A_TB, w), lambda i: (i, cb))
    return pl.pallas_call(
        body, name="mlstm_fwd", grid=(NC // A_GC,),
        in_specs=[tok(512, 0), tok(512, 1), tok(1024, 1), tok(1024, 2),
                  pl.BlockSpec((A_TB, 128), lambda i: (i, 0)),
                  pl.BlockSpec((1, 8, 128), lambda i: (i, 0, 0)),
                  pl.BlockSpec((1, 1024), lambda i: (0, 0))],
        out_specs=[pl.BlockSpec((A_TB, 1024), lambda i: (i, 0)),
                   pl.BlockSpec((A_GC, A_HEADS, A_QK, A_V), lambda i: (i, 0, 0, 0)),
                   pl.BlockSpec((A_GC, A_HEADS, 1, 128), lambda i: (i, 0, 0, 0)),
                   pl.BlockSpec((A_GC, A_HEADS, 1, 128), lambda i: (i, 0, 0, 0))],
        out_shape=[jax.ShapeDtypeStruct((T, 1024), MM_DTYPE),
                   jax.ShapeDtypeStruct((NC, A_HEADS, A_QK, A_V), F32),
                   jax.ShapeDtypeStruct((NC, A_HEADS, 1, 128), F32),
                   jax.ShapeDtypeStruct((NC, A_HEADS, 1, 128), F32)],
        scratch_shapes=[pltpu.VMEM((A_HEADS, A_QK, A_V), F32), pltpu.VMEM((A_HEADS, 1, 128), F32),
                        pltpu.VMEM((A_HEADS, 1, 128), F32)],
        compiler_params=_params("arbitrary"),
    )(z, z, z, z, gcol, grow, hng)


def mlstm_bwd(z, gcol, grow, hng, bias128, Cs, ns, ms, dhg):
    T = z.shape[0]
    NC = T // A_CHUNK
    nsteps = NC // A_GC
    scale = A_QK ** -0.5

    def body(q_ref, k_ref, v_ref, o_ref, zg_ref, gc_ref, gr_ref, hng_ref, b_ref, Cs_ref, ns_ref, ms_ref,
             dhg_ref, dz_ref, dgn_ref, dbif_ref, dC_sc, dn_sc):
        @pl.when(pl.program_id(0) == 0)
        def _():
            dC_sc[...] = jnp.zeros_like(dC_sc)
            dn_sc[...] = jnp.zeros_like(dn_sc)
            dgn_ref[...] = jnp.zeros_like(dgn_ref)
            dbif_ref[...] = jnp.zeros_like(dbif_ref)

        ri = lax.broadcasted_iota(jnp.int32, (A_CHUNK, A_CHUNK), 0)
        ci = lax.broadcasted_iota(jnp.int32, (A_CHUNK, A_CHUNK), 1)
        causal = ri >= ci
        upper = (ci >= ri).astype(F32)
        rid = lax.broadcasted_iota(jnp.int32, (A_CHUNK, 1), 0)
        col = lax.broadcasted_iota(jnp.int32, (A_CHUNK, 128), 1)
        gr = gr_ref[0]
        for c in reversed(range(A_GC)):
            rows = slice(c * A_CHUNK, (c + 1) * A_CHUNK)
            gc = gc_ref[rows, :]
            grc = gr[:, c * A_CHUNK:(c + 1) * A_CHUNK]
            dG = jnp.zeros((A_CHUNK, 128), F32)
            hs = []
            for h in range(A_HEADS):
                sk, sv = _head_slices(h)
                s = dict(sk=sk, sv=sv, qh=(q_ref[rows, sk] * scale).astype(MM_DTYPE),
                         kh=k_ref[rows, sk].astype(MM_DTYPE), vh=v_ref[rows, sv].astype(MM_DTYPE),
                         lic=gc[:, h:h + 1], bc=gc[:, A_HEADS + h:A_HEADS + h + 1],
                         lir=grc[h:h + 1, :], br=grc[A_HEADS + h:A_HEADS + h + 1, :],
                         C=Cs_ref[c, h], n=ns_ref[c, h], m=ms_ref[c, h][:, 0:1], dC=dC_sc[h], dn=dn_sc[h])
                s['qf'], s['kf'] = s['qh'].astype(F32), s['kh'].astype(F32)
                s['wk'], s['decay'], _ = _state_weights(s['bc'], s['lic'], s['br'], s['lir'], s['m'])
                hs.append(s)
            for s in hs:
                s['A'] = _dot_nt(s['qh'], s['kh'])
                s['qC'] = _dot(s['qh'], s['C'])
                s['vdC'] = _dot_nt(s['vh'], s['dC'])
                s['kdC'] = _dot(s['kh'], s['dC'])
            for s in hs:
                s['E'], s['Sm'], s['wi'], s['den'], s['gs'], s['m_t'] = _chunk_decay(
                    s['A'], s['qh'], s['bc'], s['br'], s['lir'], s['n'], s['m'], causal)
            for s in hs:
                s['num'] = _dot(s['Sm'], s['vh']) + s['wi'] * s['qC']
            for h, s in enumerate(hs):
                sv, gs = s['sv'], s['gs']
                hh = s['num'] / gs
                r = lax.rsqrt(jnp.mean(hh * hh, axis=1, keepdims=True) + EPS)
                gn = hng_ref[:, sv]
                sg = _sigmoid(o_ref[rows, sv])
                dhg_h = dhg_ref[rows, sv]
                dhn = dhg_h * sg
                dz_ref[rows, 2048 + h * A_V:2048 + (h + 1) * A_V] = dhg_h * (hh * r * gn) * sg * (1.0 - sg)
                dgn_ref[:, sv] += jnp.sum(dhn * hh * r, axis=0, keepdims=True)
                dyg = dhn * gn
                dh = r * dyg - hh * (r * r * r) * jnp.mean(dyg * hh, axis=1, keepdims=True)
                s['dnum'] = dh / gs
                live = (jnp.abs(s['den']) > jnp.exp(-s['m_t'])).astype(F32)
                s['dden'] = -jnp.sum(dh * hh, axis=1, keepdims=True) / gs * jnp.sign(s['den']) * live
            for s in hs:
                s['dnv'] = _dot_nt(s['dnum'], s['vh'])
                s['dnC'] = _dot_nt(s['dnum'], s['C'])
            for s in hs:
                s['dSE'] = jnp.where(causal, s['dnv'] + s['dden'], 0.0) * s['E']
            for s in hs:
                s['dq'] = _dot(s['dSE'], s['kh']) + s['wi'] * (s['dnC'] + s['dden'] * s['n'])
                s['dk_inter'] = s['wk'] * (s['vdC'] + s['dn'])
                s['dk'] = _dot_tn(s['dSE'], s['qh']) + s['dk_inter']
                s['dv'] = _dot_tn(s['Sm'], s['dnum']) + s['wk'] * s['kdC']
                s['dCq'] = _dot_tn(s['qf'] * s['wi'], s['dnum'])
            for h, s in enumerate(hs):
                dq, dk, qf, kf, dC, dn = s['dq'], s['dk'], s['qf'], s['kf'], s['dC'], s['dn']
                dz_ref[rows, s['sk']] = dq * scale
                dz_ref[rows, 512 + h * A_QK:512 + (h + 1) * A_QK] = dk
                dz_ref[rows, 1024 + h * A_V:1024 + (h + 1) * A_V] = s['dv']
                dli = jnp.sum(kf * dk, axis=1, keepdims=True)
                db = jnp.sum(qf * dq, axis=1, keepdims=True) - dli
                usum = jnp.sum(jnp.sum(kf * s['dk_inter'], axis=1, keepdims=True), axis=0, keepdims=True)
                ddecay = (jnp.sum(jnp.sum(dC * s['C'], axis=1, keepdims=True), axis=0, keepdims=True)
                          + jnp.sum(dn * s['n'], axis=1, keepdims=True))
                db = db + jnp.where(rid == A_CHUNK - 1, usum + ddecay * s['decay'], 0.0)
                dG = dG + jnp.where(col == h, dli, 0.0) + jnp.where(col == A_HEADS + h, db, 0.0)
                dC_sc[h] = s['decay'] * dC + s['dCq']
                dn_sc[h] = s['decay'] * dn + jnp.sum(qf * (s['wi'] * s['dden']), axis=0, keepdims=True)
            dlf = jnp.dot(upper, dG, precision=HI, preferred_element_type=F32)
            pre = zg_ref[rows, :] + b_ref[...]
            th = jnp.tanh(pre / SOFTCAP)
            dcap = 1.0 - th * th
            dpre = jnp.where(col < A_HEADS, dG * dcap,
                             jnp.where(col < 2 * A_HEADS, dlf * _sigmoid(-SOFTCAP * th) * dcap, 0.0))
            dz_ref[rows, GATE_COL:GATE_COL + 128] = dpre
            dbif_ref[...] += jnp.sum(dpre, axis=0, keepdims=True)

    rev = lambda i: nsteps - 1 - i
    tok = lambda w, cb: pl.BlockSpec((A_TB, w), lambda i: (rev(i), cb))
    st = lambda a, b: pl.BlockSpec((A_GC, A_HEADS, a, b), lambda i: (rev(i), 0, 0, 0))
    return pl.pallas_call(
        body, name="mlstm_bwd", grid=(nsteps,),
        in_specs=[tok(512, 0), tok(512, 1), tok(1024, 1), tok(1024, 2), tok(128, GATE_COL // 128),
                  pl.BlockSpec((A_TB, 128), lambda i: (rev(i), 0)),
                  pl.BlockSpec((1, 8, 128), lambda i: (rev(i), 0, 0)),
                  pl.BlockSpec((1, 1024), lambda i: (0, 0)),
                  pl.BlockSpec((1, 128), lambda i: (0, 0)),
                  st(A_QK, A_V), st(1, 128), st(1, 128),
                  pl.BlockSpec((A_TB, 1024), lambda i: (rev(i), 0))],
        out_specs=[pl.BlockSpec((A_TB, A_IN_PAD), lambda i: (rev(i), 0)),
                   pl.BlockSpec((1, 1024), lambda i: (0, 0)),
                   pl.BlockSpec((1, 128), lambda i: (0, 0))],
        out_shape=[jax.ShapeDtypeStruct((T, A_IN_PAD), F32), jax.ShapeDtypeStruct((1, 1024), F32),
                   jax.ShapeDtypeStruct((1, 128), F32)],
        scratch_shapes=[pltpu.VMEM((A_HEADS, A_QK, A_V), F32), pltpu.VMEM((A_HEADS, 1, 128), F32)],
        compiler_params=_params("arbitrary"),
    )(z, z, z, z, z, gcol, grow, hng, bias128, Cs, ns, ms, dhg)


def _t5_bucket(dist):
    max_exact = REL_BUCKETS // 2
    d = np.maximum(dist, 0)
    log_ratio = np.log(np.maximum(d, 1) / max_exact) / math.log(REL_MAX_DIST / max_exact)
    large = np.minimum(max_exact + (log_ratio * (REL_BUCKETS - max_exact)).astype(np.int64), REL_BUCKETS - 1)
    return np.where(d < max_exact, d, large).astype(np.int32)


def _group_bucket(g):
    delta = B_BLOCK + np.arange(B_BLOCK)[:, None] - np.arange(2 * B_BLOCK)[None, :]
    return _t5_bucket(delta * DILATIONS[g])


def _band_mask(n):
    ri = lax.broadcasted_iota(jnp.int32, (B_BLOCK, 2 * B_BLOCK), 0)
    ci = lax.broadcasted_iota(jnp.int32, (B_BLOCK, 2 * B_BLOCK), 1)
    band = jnp.logical_and(ci >= ri, ci <= ri + B_BLOCK)
    return jnp.logical_and(band, jnp.logical_or(ci >= B_BLOCK, n > 0))


def _both(p_ref, c_ref, sl):
    return jnp.concatenate([p_ref[:, sl], c_ref[:, sl]], axis=0)


def _scores(qh, kh, bias_h, valid):
    return jnp.where(valid, _dot_nt(qh, kh) * (B_DH ** -0.5) + bias_h, -jnp.inf)


def _attn_specs():
    wide = pl.BlockSpec((B_BLOCK, 1024), lambda r, n: (n, r))
    prev = pl.BlockSpec((B_BLOCK, 1024), lambda r, n: (jnp.maximum(n - 1, 0), r))
    narrow = pl.BlockSpec((B_BLOCK, 128), lambda r, n: (n, r))
    bias = pl.BlockSpec((B_HEADS, B_BLOCK, 2 * B_BLOCK), lambda r, n: (0, 0, 0))
    return wide, prev, narrow, bias


def _to_view(read_chunk, sc, o_ref, dil, nc, tt):
    for c in range(nc):
        sc[c] = read_chunk(c)
    for r in range(dil):
        for c in range(nc):
            lo = (r * nc + c) * 128
            o_ref[:, lo:lo + 128] = sc[c, pl.ds(r, tt // dil, stride=dil), :].astype(o_ref.dtype)


def _from_view(read_view, sc, dil, nc, tt):
    for r in range(dil):
        for c in range(nc):
            sc[c, pl.ds(r, tt // dil, stride=dil), :] = read_view((r * nc + c) * 128).astype(F32)


def attn_fwd(qv, kvw, vvw, bias, g):
    dil = DILATIONS[g]
    Tv = qv.shape[0]
    nb = Tv // B_BLOCK
    wide, prev, narrow, bsp = _attn_specs()

    def body(q_ref, kp_ref, kc_ref, vp_ref, vc_ref, b_ref, o_ref, lse_ref):
        valid = _band_mask(pl.program_id(1))
        lse_ref[...] = jnp.zeros_like(lse_ref)
        heads = [slice(h * B_DH, (h + 1) * B_DH) for h in range(B_HEADS)]
        S = [_scores(q_ref[:, sl], _both(kp_ref, kc_ref, sl), b_ref[h], valid) for h, sl in enumerate(heads)]
        P, L = [], []
        for h in range(B_HEADS):
            m = jnp.max(S[h], axis=1, keepdims=True)
            p = jnp.exp(S[h] - m)
            l = jnp.sum(p, axis=1, keepdims=True)
            lse_ref[:, h:h + 1] = m + jnp.log(l)
            P.append(p.astype(MM_DTYPE))
            L.append(l)
        for h, sl in enumerate(heads):
            o_ref[:, sl] = _dot(P[h], _both(vp_ref, vc_ref, sl)) / L[h]

    return pl.pallas_call(
        body, name=f"attn_fwd_g{g}", grid=(dil, nb),
        in_specs=[wide, prev, wide, prev, wide, bsp], out_specs=[wide, narrow],
        out_shape=[jax.ShapeDtypeStruct((Tv, dil * 1024), F32), jax.ShapeDtypeStruct((Tv, dil * 128), F32)],
        compiler_params=_params("parallel", "parallel"),
    )(qv, kvw, kvw, vvw, vvw, bias)


def attn_bwd(qv, kvw, vvw, bias, do_v, lse_v, dl_v, g):
    dil = DILATIONS[g]
    Tv = qv.shape[0]
    nb = Tv // B_BLOCK
    wide, prev, narrow, bsp = _attn_specs()

    def body(q_ref, kp_ref, kc_ref, vp_ref, vc_ref, b_ref, bt_ref, do_ref, lse_ref, dl_ref,
             dq_ref, dkc_ref, dkp_ref, dvc_ref, dvp_ref, db_ref):
        @pl.when(jnp.logical_and(pl.program_id(0) == 0, pl.program_id(1) == 0))
        def _():
            db_ref[...] = jnp.zeros_like(db_ref)

        n = pl.program_id(1)
        valid = _band_mask(n)
        ki = lax.broadcasted_iota(jnp.int32, (2 * B_BLOCK, B_BLOCK), 0)
        qi = lax.broadcasted_iota(jnp.int32, (2 * B_BLOCK, B_BLOCK), 1)
        valid_t = jnp.logical_and(jnp.logical_and(ki >= qi, ki <= qi + B_BLOCK), jnp.logical_or(ki >= B_BLOCK, n > 0))
        lse_t, dl_t = lse_ref[...].T, dl_ref[...].T
        heads = [slice(h * B_DH, (h + 1) * B_DH) for h in range(B_HEADS)]
        scale = B_DH ** -0.5
        PT, DS, DST = [], [], []
        for h, sl in enumerate(heads):
            qh, doh = q_ref[:, sl], do_ref[:, sl].astype(MM_DTYPE)
            kh, vh = _both(kp_ref, kc_ref, sl), _both(vp_ref, vc_ref, sl)
            p = jnp.exp(_scores(qh, kh, b_ref[h], valid) - lse_ref[:, h:h + 1])
            ds = p * (_dot_nt(doh, vh) - dl_ref[:, h:h + 1])
            db_ref[h] += ds
            DS.append((ds * scale).astype(MM_DTYPE))
            pt = jnp.exp(_scores(kh, qh, bt_ref[h], valid_t) - lse_t[h:h + 1, :])
            PT.append(pt.astype(MM_DTYPE))
            DST.append((pt * (_dot_nt(vh, doh) - dl_t[h:h + 1, :]) * scale).astype(MM_DTYPE))
        for h, sl in enumerate(heads):
            qh, doh = q_ref[:, sl], do_ref[:, sl].astype(MM_DTYPE)
            dq_ref[:, sl] = _dot(DS[h], _both(kp_ref, kc_ref, sl)).astype(MM_DTYPE)
            dk = _dot(DST[h], qh).astype(MM_DTYPE)
            dv = _dot(PT[h], doh).astype(MM_DTYPE)
            dkp_ref[:, sl], dkc_ref[:, sl] = dk[:B_BLOCK], dk[B_BLOCK:]
            dvp_ref[:, sl], dvc_ref[:, sl] = dv[:B_BLOCK], dv[B_BLOCK:]

    big = jax.ShapeDtypeStruct((Tv, dil * 1024), MM_DTYPE)
    bsp_t = pl.BlockSpec((B_HEADS, 2 * B_BLOCK, B_BLOCK), lambda r, n: (0, 0, 0))
    return pl.pallas_call(
        body, name=f"attn_bwd_g{g}", grid=(dil, nb),
        in_specs=[wide, prev, wide, prev, wide, bsp, bsp_t, wide, narrow, narrow],
        out_specs=[wide] * 5 + [bsp],
        out_shape=[big] * 5 + [jax.ShapeDtypeStruct((B_HEADS, B_BLOCK, 2 * B_BLOCK), F32)],
        compiler_params=_params("arbitrary", "arbitrary"),
    )(qv, kvw, kvw, vvw, vvw, bias, jnp.swapaxes(bias, 1, 2), do_v, lse_v, dl_v)


def _head_expand():
    e = np.zeros((128, 1024), np.float32)
    for h in range(B_HEADS):
        e[h, h * B_DH:(h + 1) * B_DH] = 1.0
    return e


A_TT = 256


def _view_spec(dil, width):
    return pl.BlockSpec((A_TT // dil, dil * width), lambda i: (i, 0))


def attn_merge(os_v, lses_v):
    T = os_v[0].shape[0]
    tt = A_TT
    expand = jnp.asarray(_head_expand())

    def body(o0, o1, o2, l0, l1, l2, e_ref, ob_ref, of_ref, lse0_ref, lse1_ref, lse2_ref, sc_o, sc_l):
        for gi, (o_ref, l_ref) in enumerate(((o1, l1), (o2, l2))):
            dil = DILATIONS[gi + 1]
            _from_view(lambda lo: o_ref[:, lo:lo + 128], sc_o.at[gi], dil, 8, tt)
            _from_view(lambda lo: l_ref[:, lo:lo + 128], sc_l.at[gi], dil, 1, tt)
        ls = [l0[...], sc_l[0, 0], sc_l[1, 0]]
        m = jnp.maximum(jnp.maximum(ls[0], ls[1]), ls[2])
        ex = [jnp.exp(l - m) for l in ls]
        tot = ex[0] + ex[1] + ex[2]
        lse = m + jnp.log(tot)
        lse0_ref[...] = lse
        _to_view(lambda c: lse, sc_l.at[2], lse1_ref, DILATIONS[1], 1, tt)
        _to_view(lambda c: lse, sc_l.at[2], lse2_ref, DILATIONS[2], 1, tt)
        ws = [e / tot for e in ex]
        for c in range(8):
            cols = slice(c * 128, (c + 1) * 128)
            ecol = e_ref[:, cols]
            spread = [jnp.dot(w, ecol, precision=HI, preferred_element_type=F32) for w in ws]
            out = spread[0] * o0[:, cols] + spread[1] * sc_o[0, c] + spread[2] * sc_o[1, c]
            of_ref[:, cols] = out
            ob_ref[:, cols] = out.astype(ob_ref.dtype)

    wide = pl.BlockSpec((tt, 1024), lambda i: (i, 0))
    return pl.pallas_call(
        body, name="attn_merge", grid=(T // tt,),
        in_specs=[_view_spec(d, 1024) for d in DILATIONS] + [_view_spec(d, 128) for d in DILATIONS]
        + [pl.BlockSpec((128, 1024), lambda i: (0, 0))],
        out_specs=[wide, wide] + [_view_spec(d, 128) for d in DILATIONS],
        out_shape=[jax.ShapeDtypeStruct((T, 1024), MM_DTYPE), jax.ShapeDtypeStruct((T, 1024), F32)]
        + [jax.ShapeDtypeStruct((T // d, d * 128), F32) for d in DILATIONS],
        scratch_shapes=[pltpu.VMEM((2, 8, tt, 128), F32), pltpu.VMEM((3, 1, tt, 128), F32)],
        compiler_params=_params("parallel"),
    )(*os_v, *lses_v, expand)


def attn_prep(datt, out):
    T = datt.shape[0]
    tt = A_TT
    expand_t = jnp.asarray(_head_expand().T.copy())

    def body(d_ref, o_ref, e_ref, do0, do1, do2, dl0, dl1, dl2, sc_d, sc_l):
        delta = jnp.dot(d_ref[...] * o_ref[...], e_ref[...], precision=HI, preferred_element_type=F32)
        do0[...] = d_ref[...].astype(do0.dtype)
        dl0[...] = delta
        for do_ref, dl_ref, dil in ((do1, dl1, DILATIONS[1]), (do2, dl2, DILATIONS[2])):
            _to_view(lambda c: d_ref[:, c * 128:(c + 1) * 128], sc_d, do_ref, dil, 8, tt)
            _to_view(lambda c: delta, sc_l, dl_ref, dil, 1, tt)

    wide = pl.BlockSpec((tt, 1024), lambda i: (i, 0))
    return pl.pallas_call(
        body, name="attn_prep", grid=(T // tt,),
        in_specs=[wide, wide, pl.BlockSpec((1024, 128), lambda i: (0, 0))],
        out_specs=[_view_spec(d, 1024) for d in DILATIONS] + [_view_spec(d, 128) for d in DILATIONS],
        out_shape=[jax.ShapeDtypeStruct((T // d, d * 1024), MM_DTYPE) for d in DILATIONS]
        + [jax.ShapeDtypeStruct((T // d, d * 128), F32) for d in DILATIONS],
        scratch_shapes=[pltpu.VMEM((8, tt, 128), F32), pltpu.VMEM((1, tt, 128), F32)],
        compiler_params=_params("parallel"),
    )(datt, out, expand_t)


def attn_combine(parts):
    T = parts[0][0].shape[0]
    tt = A_TT
    nt = T // tt
    shift = [None] + [B_BLOCK * d // tt for d in DILATIONS[1:]]

    def body(dq0, kc0, vc0, kpa0, kpb0, vpa0, vpb0, dq1, kc1, kp1, vc1, vp1, dq2, kc2, kp2, vc2, vp2,
             dq_ref, dkv_ref, sc):
        i = pl.program_id(0)
        dq_ref[:, 0:1024] = dq0[...].astype(dq_ref.dtype)
        for col, c_ref, pa_ref, pb_ref in ((0, kc0, kpa0, kpb0), (3, vc0, vpa0, vpb0)):
            nxt = jnp.where(i + 1 < nt, pb_ref[:tt // 2, :].astype(F32), 0.0)
            later = jnp.concatenate([pa_ref[tt // 2:, :].astype(F32), nxt], axis=0)
            dkv_ref[:, col * 1024:(col + 1) * 1024] = (c_ref[...].astype(F32) + later).astype(dkv_ref.dtype)
        for g, (dq, kc, kp, vc, vp) in ((1, (dq1, kc1, kp1, vc1, vp1)), (2, (dq2, kc2, kp2, vc2, vp2))):
            dil = DILATIONS[g]
            live = i + shift[g] < nt
            _from_view(lambda lo: dq[:, lo:lo + 128], sc, dil, 8, tt)
            for c in range(8):
                dq_ref[:, g * 1024 + c * 128:g * 1024 + (c + 1) * 128] = sc[c].astype(dq_ref.dtype)
            for col, c_ref, p_ref in ((g, kc, kp), (3 + g, vc, vp)):
                _from_view(lambda lo: c_ref[:, lo:lo + 128].astype(F32)
                           + jnp.where(live, p_ref[:, lo:lo + 128].astype(F32), 0.0), sc, dil, 8, tt)
                for c in range(8):
                    dkv_ref[:, col * 1024 + c * 128:col * 1024 + (c + 1) * 128] = sc[c].astype(dkv_ref.dtype)

    def later_spec(dil, blocks):
        return pl.BlockSpec((tt // dil, dil * 1024), lambda i: (jnp.minimum(i + blocks, nt - 1), 0))

    cur = [_view_spec(d, 1024) for d in DILATIONS]
    in_specs = [cur[0], cur[0], cur[0], cur[0], later_spec(1, 1), cur[0], later_spec(1, 1)]
    args = [parts[0][0], parts[0][1], parts[0][3], parts[0][2], parts[0][2], parts[0][4], parts[0][4]]
    for g in (1, 2):
        in_specs += [cur[g], cur[g], later_spec(DILATIONS[g], shift[g]), cur[g], later_spec(DILATIONS[g], shift[g])]
        args += list(parts[g][:5])
    return pl.pallas_call(
        body, name="attn_combine", grid=(nt,), in_specs=in_specs,
        out_specs=[pl.BlockSpec((tt, 3072), lambda i: (i, 0)), pl.BlockSpec((tt, 6144), lambda i: (i, 0))],
        out_shape=[jax.ShapeDtypeStruct((T, 3072), MM_DTYPE), jax.ShapeDtypeStruct((T, 6144), MM_DTYPE)],
        scratch_shapes=[pltpu.VMEM((8, tt, 128), F32)],
        compiler_params=_params("parallel"),
    )(*args)


def adamw(w, g, m, v, name):
    R, C = w.shape
    tr = R if R * C * 4 <= (1 << 20) else _rows(R, max(8, ((1 << 20) // (C * 4)) // 8 * 8))

    def body(w_ref, g_ref, m_ref, v_ref, d_ref, nm_ref, nv_ref):
        gg = g_ref[...]
        nm = ADAM_B1 * m_ref[...] + (1.0 - ADAM_B1) * gg
        nv = ADAM_B2 * v_ref[...] + (1.0 - ADAM_B2) * (gg * gg)
        m_hat = nm / (1.0 - ADAM_B1 ** ADAM_STEP)
        v_hat = nv / (1.0 - ADAM_B2 ** ADAM_STEP)
        d_ref[...] = -ADAM_LR * (m_hat / (jnp.sqrt(v_hat) + ADAM_EPS) + ADAM_WD * w_ref[...])
        nm_ref[...] = nm
        nv_ref[...] = nv

    blk = pl.BlockSpec((tr, C), lambda i: (i, 0))
    sds = jax.ShapeDtypeStruct((R, C), F32)
    return pl.pallas_call(
        body, name=name, grid=(R // tr,), in_specs=[blk] * 4, out_specs=[blk] * 3, out_shape=[sds] * 3,
        compiler_params=_params("parallel"),
    )(w, g, m, v)


def sum_slots(x, name, out_dtype=F32):
    n, R, C = x.shape
    tr = _rows(R, 256)

    def body(x_ref, o_ref):
        acc = x_ref[0].astype(F32)
        for s in range(1, n):
            acc = acc + x_ref[s].astype(F32)
        o_ref[...] = acc.astype(out_dtype)

    return pl.pallas_call(
        body, name=name, grid=(R // tr,),
        in_specs=[pl.BlockSpec((n, tr, C), lambda i: (0, i, 0))],
        out_specs=pl.BlockSpec((tr, C), lambda i: (i, 0)),
        out_shape=jax.ShapeDtypeStruct((R, C), out_dtype),
        compiler_params=_params("parallel"),
    )(x)


_ANY = pl.BlockSpec(memory_space=pl.ANY)
GROUP_ALL = ([(0, 0, 1), (0, 1, 0), (0, 1, 1), (1, 0, 0), (1, 0, 1), (1, 1, 0), (1, 1, 1)],
             lambda d: 4 * d[0] + 2 * d[1] + d[2])
GROUP_CHIPS = ([(0, 1, 0), (1, 0, 0), (1, 1, 0)], lambda d: 2 * d[0] + d[1])
GROUP_SIBLING = ([(0, 0, 1)], lambda d: d[2])


def _me():
    return lax.axis_index("x"), lax.axis_index("y"), lax.axis_index("c")


def _peer(me, flip):
    return tuple(1 - a if f else a for a, f in zip(me, flip))


class Exchange:
    def __init__(self, x, group, scatter):
        self.flips, self.slot = group
        self.scatter = scatter
        self.n = len(self.flips) + 1
        self.out_shape = jax.ShapeDtypeStruct((self.n,) + x.shape[-2:], x.dtype)
        self.scratch = [pltpu.SemaphoreType.DMA((self.n - 1,)), pltpu.SemaphoreType.DMA((self.n - 1,)),
                        pltpu.SemaphoreType.DMA]

    def _copies(self, x_ref, o_ref, send_sems, recv_sems, local_sem, arrivals):
        me = _me()
        slot = self.slot
        mine = pltpu.make_async_copy(x_ref.at[slot(me)] if self.scatter else x_ref, o_ref.at[slot(me)], local_sem)
        sends, landed = [], []
        for k, flip in enumerate(self.flips):
            peer = _peer(me, flip)
            sends.append(pltpu.make_async_remote_copy(
                src_ref=x_ref.at[slot(peer)] if self.scatter else x_ref, dst_ref=o_ref.at[slot(me)],
                send_sem=send_sems.at[k], recv_sem=recv_sems.at[k], device_id=peer, device_id_type=MESH_ID))
            if arrivals:
                landed.append(pltpu.make_async_remote_copy(
                    src_ref=o_ref.at[slot(me)], dst_ref=o_ref.at[slot(peer)], send_sem=send_sems.at[k],
                    recv_sem=recv_sems.at[k], device_id=peer, device_id_type=MESH_ID))
        return mine, sends, landed

    def start(self, *refs):
        mine, sends, _ = self._copies(*refs, arrivals=False)
        mine.start()
        for cp in sends:
            cp.start()

    def wait(self, *refs):
        mine, sends, arrivals = self._copies(*refs, arrivals=True)
        for cp in arrivals:
            cp.wait_recv()
        for cp in sends:
            cp.wait_send()
        mine.wait()

    def __call__(self, x, name):
        def body(*refs):
            self.start(*refs)
            self.wait(*refs)

        return pl.pallas_call(body, name=name, in_specs=[_ANY], out_specs=_ANY, out_shape=self.out_shape,
                              scratch_shapes=self.scratch)(x)


def group_gather(x, name, group):
    return Exchange(x, group, scatter=False)(x, name)


def group_scatter(x, name, group):
    return Exchange(x, group, scatter=True)(x, name)


def _call(body, name, grid, in_specs, out_specs, out_shape, scratch, semantics, args, ride=None):
    if ride is None:
        return pl.pallas_call(body, name=name, grid=grid, in_specs=in_specs, out_specs=out_specs,
                              out_shape=out_shape, scratch_shapes=scratch,
                              compiler_params=_params(*semantics))(*args)
    x, exch = ride
    n_in, n_out, n_scr = len(in_specs), len(out_specs), len(scratch)

    def at_step(pick):
        hit = None
        for axis, size in enumerate(grid):
            here = pl.program_id(axis) == pick(size)
            hit = here if hit is None else jnp.logical_and(hit, here)
        return hit

    def riding(*refs):
        ins, x_ref = refs[:n_in], refs[n_in]
        outs, o_ref = refs[n_in + 1:n_in + 1 + n_out], refs[n_in + 1 + n_out]
        scr, sems = refs[n_in + 2 + n_out:n_in + 2 + n_out + n_scr], refs[n_in + 2 + n_out + n_scr:]

        @pl.when(at_step(lambda size: 0))
        def _():
            exch.start(x_ref, o_ref, *sems)

        body(*ins, *outs, *scr)

        @pl.when(at_step(lambda size: size - 1))
        def _():
            exch.wait(x_ref, o_ref, *sems)

    return pl.pallas_call(
        riding, name=name, grid=grid, in_specs=list(in_specs) + [_ANY], out_specs=list(out_specs) + [_ANY],
        out_shape=list(out_shape) + [exch.out_shape], scratch_shapes=list(scratch) + exch.scratch,
        compiler_params=_params(*(["arbitrary"] * len(grid))))(*args, x)


WEIGHTS = ['a_norm_g', 'a_w_in', 'a_b_if', 'a_hnorm_g', 'a_w_out', 'kv_norm_g', 'w_kv', 'b_norm_g', 'b_w_q',
           'b_w_out', 'rel_bias', 'f_norm_g', 'f_w_up', 'f_conv_w', 'f_conv_b', 'f_w_down', 'final_norm_g']
SHARD_AXIS = {'a_norm_g': 1, 'a_w_in': 2, 'a_b_if': None, 'a_hnorm_g': 2, 'a_w_out': 1, 'kv_norm_g': None,
              'w_kv': 1, 'b_norm_g': None, 'b_w_q': 2, 'b_w_out': 1, 'rel_bias': None, 'f_norm_g': None,
              'f_w_up': 2, 'f_conv_w': 2, 'f_conv_b': None, 'f_w_down': 1, 'final_norm_g': None}
BIG = ['a_w_in', 'a_w_out', 'w_kv', 'b_w_q', 'b_w_out', 'f_w_up', 'f_w_down']
SMALL = [n for n in WEIGHTS if n not in BIG]
LANES = 1024
PIECES = {'a_w_in': ('a_w_in', None, 2), 'a_w_out': ('a_w_out', None, 1), 'f_w_up0': ('f_w_up', 0, 1),
          'f_w_down0': ('f_w_down', 0, 0), 'w_kv': ('w_kv', None, 1), 'b_w_q': ('b_w_q', None, 2),
          'b_w_out': ('b_w_out', None, 1), 'f_w_up1': ('f_w_up', 1, 1), 'f_w_down1': ('f_w_down', 1, 0)}
EARLY = ['a_w_in', 'a_w_out', 'f_w_up0', 'f_w_down0']
LATE = ['w_kv', 'b_w_q', 'b_w_out', 'f_w_up1', 'f_w_down1']


def _piece(arrays, p):
    leaf, layer, _ = PIECES[p]
    return arrays[leaf] if layer is None else arrays[leaf][layer]


class Packer:
    def __init__(self, pieces, shard):
        self.pieces = pieces
        self.shapes = [_piece(shard, p).shape for p in pieces]
        self.sizes = [math.prod(s) // (2 * LANES) for s in self.shapes]
        self.fill = -sum(self.sizes) % 16
        self.rows = sum(self.sizes) + self.fill

    def my_half(self, shard, half):
        both = jnp.concatenate([_piece(shard, p).astype(MM_DTYPE).reshape(2, -1, LANES) for p in self.pieces], axis=1)
        return jnp.pad(lax.dynamic_index_in_dim(both, half, axis=0, keepdims=False), ((0, self.fill), (0, 0)))

    def full_weights(self, gathered):
        g = gathered.reshape(4, 2, self.rows, LANES)
        out, off = {}, 0
        for p, shp, sz in zip(self.pieces, self.shapes, self.sizes):
            out[p] = _full_from_shards(g[:, :, off:off + sz].reshape((4,) + shp), PIECES[p][2])
            off += sz
        return out

    def grad_slots(self, grads):
        parts = [_shards_from_full(grads[p], PIECES[p][2]).reshape(4, 2, -1, LANES).astype(GRAD_WIRE_DTYPE)
                 for p in self.pieces]
        parts.append(jnp.zeros((4, 2, self.fill, LANES), GRAD_WIRE_DTYPE))
        return jnp.concatenate(parts, axis=2).reshape(8, self.rows, LANES)

    def shard_grads(self, both):
        out, off = {}, 0
        for p, shp, sz in zip(self.pieces, self.shapes, self.sizes):
            out[p] = both[:, off:off + sz].reshape(shp).astype(F32)
            off += sz
        return out


class Overlap:
    def __init__(self, packer, shard, half):
        self.packer, self.shard, self.half = packer, shard, half
        self.shard_grads = None

    def gather_ride(self):
        mine = self.packer.my_half(self.shard, self.half)
        return mine, Exchange(mine, GROUP_ALL, scatter=False)

    def scatter_ride(self, grads):
        slots = self.packer.grad_slots(grads)
        return slots, Exchange(slots, GROUP_ALL, scatter=True)

    def join_ride(self, received):
        reduced = sum_slots(received, "sum_grads_late", GRAD_WIRE_DTYPE)
        return reduced, Exchange(reduced, GROUP_SIBLING, scatter=False)

    def joined(self, both):
        self.shard_grads = self.packer.shard_grads(both)


def _pad_rows(flat, mult):
    n = flat.shape[0]
    per = LANES * mult
    tot = -(-n // per) * per
    return jnp.pad(flat, (0, tot - n)).reshape(tot // LANES, LANES)


def _full_from_shards(sh, axis):
    return jnp.concatenate([sh[j] for j in range(4)], axis=axis)


def _shards_from_full(full, axis):
    return jnp.stack(jnp.split(full, 4, axis=axis))


def _local_step(x, target, W, overlap=None):
    T = x.shape[0]
    W = dict(W)
    row = lambda a: a.reshape(1, -1).astype(F32)
    w_in = jnp.pad(W['a_w_in'][0], ((0, 0), (0, A_IN_PAD - A_IN)))
    bias128 = jnp.pad(row(W['a_b_if'][0]), ((0, 0), (0, 120)))
    hng = row(W['a_hnorm_g'][0])
    w_up = lambda l: _interleave(W[f'f_w_up{l}'])
    cw = [_interleave(W['f_conv_w'][l].astype(F32)) for l in range(2)]
    cb = [_interleave(row(W['f_conv_b'][l])) for l in range(2)]
    onehots = [(jnp.asarray(_group_bucket(g).reshape(-1, 1)) == jnp.arange(128)[None, :]).astype(F32)
               for g in range(N_GROUPS)]
    rb_t = jnp.pad(W['rel_bias'].astype(F32).T, ((0, 0), (0, 128 - REL_BUCKETS)))
    biases = [mm_nn(rb_t[g * B_HEADS:(g + 1) * B_HEADS], onehots[g].T, f"rel_bias_table_g{g}", exact=True)
              .reshape(B_HEADS, B_BLOCK, 2 * B_BLOCK) for g in range(N_GROUPS)]
    G = {}

    def ffn_fwd(xin, l, ride=None):
        xn, = rms_fwd(xin, [row(W['f_norm_g'][l])], f"ffn{l}_norm")
        u, act, *rode = ffn_up_act(xn, w_up(l), cw[l], cb[l], f"ffn{l}_up_act", ride)
        return mm_nn(act, W[f'f_w_down{l}'], f"ffn{l}_down", res=xin), (xn, u, act), rode

    def ffn_bwd(xin, saved, dout, l, ride=None):
        xn, u, act = saved
        dact = mm_nn(dout, W[f'f_w_down{l}'].T, f"ffn{l}_ddown")
        G[f'f_w_down{l}'] = mm_tn(act, dout, f"ffn{l}_gdown")
        du, gcw, gcb, *rode = conv_act_bwd(u, dact, cw[l], cb[l], f"ffn{l}_dact", ride)
        dxn = mm_nn(du, w_up(l).T, f"ffn{l}_dup")
        G[f'f_w_up{l}'] = _deinterleave(mm_tn(xn, du, f"ffn{l}_gup"))
        dxin, (gn,) = rms_bwd(xin, dout, [(dxn, row(W['f_norm_g'][l]))], f"ffn{l}_dnorm")
        return dxin, _deinterleave(gcw), _deinterleave(gcb), gn, rode

    xn_a, = rms_fwd(x, [row(W['a_norm_g'][0])], "a_norm")
    z = mm_nn(xn_a, w_in, "a_in")
    gcol, grow = gate_prep(z, bias128)
    hg, Cs, ns, ms = mlstm_fwd(z, gcol, grow, hng)
    x1 = mm_nn(hg, W['a_w_out'][0], "a_out", res=x)
    x2, ffn0, rode = ffn_fwd(x1, 0, overlap.gather_ride() if overlap else None)
    if overlap:
        W.update(overlap.packer.full_weights(rode[0]))
    xn_kv, xn_b = rms_fwd(x2, [row(W['kv_norm_g']), row(W['b_norm_g'][0])], "b_norms")
    gcols = lambda w, c: w[:, c * 1024:(c + 1) * 1024]
    qv = [mm_view(xn_b, gcols(W['b_w_q'][0], g), f"q_proj_g{g}", DILATIONS[g]) for g in range(N_GROUPS)]
    kvw = [mm_view(xn_kv, gcols(W['w_kv'], g), f"k_proj_g{g}", DILATIONS[g]) for g in range(N_GROUPS)]
    vvw = [mm_view(xn_kv, gcols(W['w_kv'], 3 + g), f"v_proj_g{g}", DILATIONS[g]) for g in range(N_GROUPS)]
    os_, lses = zip(*[attn_fwd(qv[g], kvw[g], vvw[g], biases[g], g) for g in range(N_GROUPS)])
    att, att_f, *lse_v = attn_merge(os_, lses)
    x3 = mm_nn(att, W['b_w_out'][0], "b_out", res=x2)
    x4, ffn1, _ = ffn_fwd(x3, 1)
    dx4, g_final, loss = loss_head(x4, target, row(W['final_norm_g']))
    G['final_norm_g'] = g_final.reshape(-1)

    dx3, gcw1, gcb1, gn1, _ = ffn_bwd(x3, ffn1, dx4, 1)
    datt = mm_nn(dx3, W['b_w_out'][0].T, "b_dout")
    G['b_w_out'] = mm_tn(att, dx3, "b_gout")[None]
    prep = attn_prep(datt, att_f)
    do_v, dl_v = prep[:3], prep[3:]
    parts = [attn_bwd(qv[g], kvw[g], vvw[g], biases[g], do_v[g], lse_v[g], dl_v[g], g) for g in range(N_GROUPS)]
    dq_all, dkv = attn_combine(parts)
    grb = []
    for g in range(N_GROUPS):
        gb = mm_nn(parts[g][5].reshape(B_HEADS, -1), onehots[g], f"rel_bias_g{g}", exact=True)
        grb.append(gb[:, :REL_BUCKETS].T)
    G['rel_bias'] = jnp.concatenate(grb, axis=1)
    dxn_b = mm_nn(dq_all, W['b_w_q'][0].T, "q_dproj")
    G['b_w_q'] = mm_tn(xn_b, dq_all, "q_gproj")[None]
    dxn_kv = mm_nn(dkv, W['w_kv'].T, "kv_dproj")
    G['w_kv'] = mm_tn(xn_kv, dkv, "kv_gproj")
    dx2, (g_kvn, g_bn) = rms_bwd(x2, dx3, [(dxn_kv, row(W['kv_norm_g'])), (dxn_b, row(W['b_norm_g'][0]))],
                                 "b_dnorms")
    G['kv_norm_g'] = g_kvn.reshape(-1)
    G['b_norm_g'] = g_bn
    ride = overlap.scatter_ride({p: G.pop(p) for p in LATE}) if overlap else None
    dx1, gcw0, gcb0, gn0, rode = ffn_bwd(x1, ffn0, dx2, 0, ride)
    G['f_conv_w'] = jnp.stack([gcw0, gcw1])
    G['f_conv_b'] = jnp.concatenate([gcb0, gcb1], axis=0)
    G['f_norm_g'] = jnp.concatenate([gn0, gn1], axis=0)
    dhg = mm_nn(dx1, W['a_w_out'][0].T, "a_dout")
    G['a_w_out'] = mm_tn(hg, dx1, "a_gout")[None]
    dz, g_hn, g_bif = mlstm_bwd(z, gcol, grow, hng, bias128, Cs, ns, ms, dhg)
    G['a_hnorm_g'] = g_hn.reshape(1, A_HEADS, A_V)
    G['a_b_if'] = g_bif[:, :2 * A_HEADS]
    if overlap:
        dxn_a, both = mm_nn(dz, w_in.T, "a_din", ride=overlap.join_ride(rode[0]))
        overlap.joined(both)
    else:
        dxn_a = mm_nn(dz, w_in.T, "a_din")
    G['a_w_in'] = mm_tn(xn_a, dz, "a_gin")[:, :A_IN][None]
    grad_x, (g_an,) = rms_bwd(x, dx1, [(dxn_a, row(W['a_norm_g'][0]))], "a_dnorm")
    G['a_norm_g'] = g_an
    return loss, grad_x, G


def kernel(x, a_norm_g, a_w_in, a_b_if, a_hnorm_g, a_w_out, kv_norm_g, w_kv, b_norm_g, b_w_q, b_w_out, rel_bias, f_norm_g, f_w_up, f_conv_w, f_conv_b, f_w_down, final_norm_g, loss_target, m_a_norm_g, m_a_w_in, m_a_b_if, m_a_hnorm_g, m_a_w_out, m_kv_norm_g, m_w_kv, m_b_norm_g, m_b_w_q, m_b_w_out, m_rel_bias, m_f_norm_g, m_f_w_up, m_f_conv_w, m_f_conv_b, m_f_w_down, m_final_norm_g, v_a_norm_g, v_a_w_in, v_a_b_if, v_a_hnorm_g, v_a_w_out, v_kv_norm_g, v_w_kv, v_b_norm_g, v_b_w_q, v_b_w_out, v_rel_bias, v_f_norm_g, v_f_w_up, v_f_conv_w, v_f_conv_b, v_f_w_down, v_final_norm_g):
    given = dict(locals())
    shard = {n: given[n] for n in WEIGHTS}
    mom = {n: given["m_" + n] for n in WEIGHTS}
    var = {n: given["v_" + n] for n in WEIGHTS}
    cx, cy, cc = _me()
    chip = 2 * cx + cy

    early, late = Packer(EARLY, shard), Packer(LATE, shard)
    W = early.full_weights(group_gather(early.my_half(shard, cc), "gather_weights", GROUP_ALL))
    sharded_small = [n for n in SMALL if SHARD_AXIS[n] is not None]
    ssz = [shard[n].size for n in sharded_small]
    sflat = jnp.concatenate([shard[n].reshape(-1) for n in sharded_small])
    sg = group_gather(_pad_rows(sflat, 8), "gather_small", GROUP_CHIPS).reshape(4, -1)
    off = 0
    for n, sz in zip(sharded_small, ssz):
        W[n] = _full_from_shards(sg[:, off:off + sz].reshape((4,) + shard[n].shape), SHARD_AXIS[n])
        off += sz
    for n in SMALL:
        if SHARD_AXIS[n] is None:
            W[n] = shard[n]

    overlap = Overlap(late, shard, cc)
    loss_row, grad_x, G = _local_step(x[0], loss_target[0], W, overlap)

    reduced = sum_slots(group_scatter(early.grad_slots(G), "scatter_grads", GROUP_ALL), "sum_grads", GRAD_WIRE_DTYPE)
    by_piece = {**early.shard_grads(group_gather(reduced, "join_halves", GROUP_SIBLING)), **overlap.shard_grads}
    gsh = {}
    for n in BIG:
        layers = [p for p in PIECES if PIECES[p][0] == n]
        gsh[n] = by_piece[n] if layers == [n] else jnp.stack([by_piece[p] for p in layers])
    small_parts = [loss_row[0, 0:1]] + [G[n].reshape(-1) for n in SMALL]
    small_sz = [p.shape[0] for p in small_parts]
    small = sum_slots(group_gather(_pad_rows(jnp.concatenate(small_parts), 8), "gather_small_grads", GROUP_ALL),
                      "sum_small_grads").reshape(-1)
    loss = small[0]
    off = 1
    for n, sz in zip(SMALL, small_sz[1:]):
        full = small[off:off + sz].reshape(W[n].shape)
        off += sz
        if SHARD_AXIS[n] is None:
            gsh[n] = full
        else:
            gsh[n] = lax.dynamic_index_in_dim(_shards_from_full(full, SHARD_AXIS[n]), chip, 0, keepdims=False)

    delta, new_m, new_v = {}, {}, {}
    for n in WEIGHTS:
        shp = shard[n].shape
        two = lambda a: a.reshape(-1, shp[-1])
        d, nm, nv = adamw(two(shard[n]), two(gsh[n]), two(mom[n]), two(var[n]), f"adamw_{n}")
        delta[n], new_m[n], new_v[n] = d.reshape(shp), nm.reshape(shp), nv.reshape(shp)
    return (loss, grad_x[None], *[gsh[n] for n in WEIGHTS], *[delta[n] for n in WEIGHTS],
            *[new_m[n] for n in WEIGHTS], *[new_v[n] for n in WEIGHTS])
```

```python
import functools
import math

import numpy as np
import jax
import jax.numpy as jnp
from jax import lax
from jax.experimental import pallas as pl
from jax.experimental.pallas import tpu as pltpu

F32 = jnp.float32
BF16 = jnp.bfloat16
MM_DTYPE = jnp.bfloat16
GRAD_WIRE_DTYPE = jnp.bfloat16
HI = lax.Precision.HIGHEST

D_MODEL = 1024
A_HEADS = 4
A_QK = 128
A_V = 256
A_CHUNK = 64
A_IN = 3080
A_IN_PAD = 3200
GATE_COL = 3072
SOFTCAP = 15.0
N_GROUPS = 3
B_HEADS = 16
B_DH = 64
B_BLOCK = 128
DILATIONS = (1, 4, 16)
WINDOWS = (128, 512, 2048)
REL_BUCKETS = 32
REL_MAX_DIST = 2048
D_FF = 2816
FF_TC = 256
EPS = 1e-6
ADAM_LR, ADAM_B1, ADAM_B2, ADAM_EPS, ADAM_WD, ADAM_STEP = 0.001, 0.9, 0.999, 1e-08, 0.01, 10

VMEM_LIMIT = 56 * 1024 * 1024
NT_DIMS = (((1,), (1,)), ((), ()))
TN_DIMS = (((0,), (0,)), ((), ()))
MESH_ID = pl.DeviceIdType.MESH


def _params(*sem):
    return pltpu.CompilerParams(dimension_semantics=sem, vmem_limit_bytes=VMEM_LIMIT)


def _tile(n, cap):
    if n <= cap:
        return n
    best = None
    for t in range(128, cap + 1, 128):
        if n % t == 0:
            best = t
    assert best is not None, (n, cap)
    return best


def _rows(n, cap):
    if n <= cap:
        return n
    for t in range(cap // 8 * 8, 7, -8):
        if n % t == 0:
            return t
    raise ValueError((n, cap))


def _dot(a, b):
    return jnp.dot(a.astype(MM_DTYPE), b.astype(MM_DTYPE), preferred_element_type=F32)


def _dot_nt(a, b):
    return lax.dot_general(a.astype(MM_DTYPE), b.astype(MM_DTYPE), NT_DIMS, preferred_element_type=F32)


def _dot_tn(a, b):
    return lax.dot_general(a.astype(MM_DTYPE), b.astype(MM_DTYPE), TN_DIMS, preferred_element_type=F32)


def _sigmoid(x):
    return 1.0 / (1.0 + jnp.exp(-x))


def _sigmoid_tanh(x):
    return 0.5 * jnp.tanh(0.5 * x) + 0.5


def mm_nn(a, b, name, res=None, out_dtype=F32, exact=False, ride=None):
    M, K = a.shape
    N = b.shape[1]
    tm, tn = _rows(M, 512), _tile(N, 1536)
    whole_k = (2 * (tm * K * a.dtype.itemsize + K * tn * b.dtype.itemsize)
               + 2 * tm * tn * 4 * (1 if res is None else 2))
    tk = K if whole_k <= 46 * 1024 * 1024 else _tile(K, 1536)
    nk = K // tk

    def body(*refs):
        if res is None:
            a_ref, b_ref, o_ref, acc = refs
            r_ref = None
        else:
            a_ref, b_ref, r_ref, o_ref, acc = refs
        if exact:
            p = jnp.dot(a_ref[...], b_ref[...], precision=HI, preferred_element_type=F32)
        else:
            p = _dot(a_ref[...], b_ref[...])

        def finish(total):
            if r_ref is not None:
                total = total + r_ref[...]
            o_ref[...] = total.astype(out_dtype)

        if nk == 1:
            finish(p)
        else:
            k = pl.program_id(2)

            @pl.when(k == 0)
            def _():
                acc[...] = p

            @pl.when(jnp.logical_and(k > 0, k < nk - 1))
            def _():
                acc[...] += p

            @pl.when(k == nk - 1)
            def _():
                finish(acc[...] + p)

    in_specs = [pl.BlockSpec((tm, tk), lambda j, i, k: (i, k)),
                pl.BlockSpec((tk, tn), lambda j, i, k: (k, j))]
    args = [a, b]
    if res is not None:
        in_specs.append(pl.BlockSpec((tm, tn), lambda j, i, k: (i, j)))
        args.append(res)
    acc_shape = (tm, tn) if nk > 1 else (8, 128)
    outs = _call(body, name, (N // tn, M // tm, nk), in_specs, [pl.BlockSpec((tm, tn), lambda j, i, k: (i, j))],
                 [jax.ShapeDtypeStruct((M, N), out_dtype)], [pltpu.VMEM(acc_shape, F32)],
                 ("parallel", "parallel", "arbitrary"), args, ride)
    return outs[0] if ride is None else outs


def mm_view(a, b, name, dil):
    T, K = a.shape
    tm = 512

    def body(a_ref, b_ref, o_ref, sc):
        p = _dot(a_ref[...], b_ref[...])
        if dil == 1:
            o_ref[...] = p.astype(o_ref.dtype)
        else:
            _to_view(lambda c: p[:, c * 128:(c + 1) * 128], sc, o_ref, dil, 8, tm)

    return pl.pallas_call(
        body, name=name, grid=(T // tm,),
        in_specs=[pl.BlockSpec((tm, K), lambda i: (i, 0)), pl.BlockSpec((K, 1024), lambda i: (0, 0))],
        out_specs=pl.BlockSpec((tm // dil, dil * 1024), lambda i: (i, 0)),
        out_shape=jax.ShapeDtypeStruct((T // dil, dil * 1024), MM_DTYPE),
        scratch_shapes=[pltpu.VMEM((8, tm, 128), F32)],
        compiler_params=_params("parallel"),
    )(a, b)


def mm_tn(a, g, name, ride=None):
    T, Ka = a.shape
    N = g.shape[1]
    tka, tn, tt = _tile(Ka, 1536), _tile(N, 1536), _rows(T, 1024)
    nt = T // tt

    def body(a_ref, g_ref, o_ref):
        t = pl.program_id(2)
        p = _dot_tn(a_ref[...], g_ref[...])

        @pl.when(t == 0)
        def _():
            o_ref[...] = p

        @pl.when(t > 0)
        def _():
            o_ref[...] += p

    outs = _call(body, name, (Ka // tka, N // tn, nt),
                 [pl.BlockSpec((tt, tka), lambda i, j, t: (t, i)), pl.BlockSpec((tt, tn), lambda i, j, t: (t, j))],
                 [pl.BlockSpec((tka, tn), lambda i, j, t: (i, j))], [jax.ShapeDtypeStruct((Ka, N), F32)], [],
                 ("parallel", "parallel", "arbitrary"), (a, g), ride)
    return outs[0] if ride is None else outs


def rms_fwd(x, gains, name):
    T, D = x.shape
    tt = _rows(T, 512)
    ng = len(gains)

    def body(*refs):
        x_ref = refs[0]
        g_refs = refs[1:1 + ng]
        o_refs = refs[1 + ng:]
        xf = x_ref[...]
        y = xf * lax.rsqrt(jnp.mean(xf * xf, axis=-1, keepdims=True) + EPS)
        for g_ref, o_ref in zip(g_refs, o_refs):
            o_ref[...] = (y * g_ref[...]).astype(o_ref.dtype)

    row = pl.BlockSpec((tt, D), lambda i: (i, 0))
    gsp = pl.BlockSpec((1, D), lambda i: (0, 0))
    return pl.pallas_call(
        body, name=name, grid=(T // tt,),
        in_specs=[row] + [gsp] * ng, out_specs=[row] * ng,
        out_shape=[jax.ShapeDtypeStruct((T, D), MM_DTYPE)] * ng,
        compiler_params=_params("parallel"),
    )(x, *gains)


def rms_bwd(x, dres, branches, name):
    T, D = x.shape
    tt = _rows(T, 256)
    nb = len(branches)

    def body(*refs):
        x_ref, r_ref = refs[0], refs[1]
        dy_refs = refs[2:2 + nb]
        g_refs = refs[2 + nb:2 + 2 * nb]
        dx_ref = refs[2 + 2 * nb]
        dg_refs = refs[3 + 2 * nb:]
        i = pl.program_id(0)
        xf = x_ref[...]
        r = lax.rsqrt(jnp.mean(xf * xf, axis=-1, keepdims=True) + EPS)
        xh = xf * r
        dx = r_ref[...]
        for dy_ref, g_ref, dg_ref in zip(dy_refs, g_refs, dg_refs):
            dy = dy_ref[...].astype(F32)
            dyg = dy * g_ref[...]
            dx = dx + r * (dyg - xh * jnp.mean(dyg * xh, axis=-1, keepdims=True))
            part = jnp.sum(dy * xh, axis=0, keepdims=True)

            @pl.when(i == 0)
            def _():
                dg_ref[...] = part

            @pl.when(i > 0)
            def _():
                dg_ref[...] += part
        dx_ref[...] = dx

    row = pl.BlockSpec((tt, D), lambda i: (i, 0))
    gsp = pl.BlockSpec((1, D), lambda i: (0, 0))
    outs = pl.pallas_call(
        body, name=name, grid=(T // tt,),
        in_specs=[row, row] + [row] * nb + [gsp] * nb,
        out_specs=[row] + [gsp] * nb,
        out_shape=[jax.ShapeDtypeStruct((T, D), F32)] + [jax.ShapeDtypeStruct((1, D), F32)] * nb,
        compiler_params=_params("arbitrary"),
    )(x, dres, *[b[0] for b in branches], *[b[1] for b in branches])
    return outs[0], outs[1:]


def loss_head(x, target, gain):
    T, D = x.shape
    tt = _rows(T, 256)

    def body(x_ref, t_ref, g_ref, dx_ref, dg_ref, loss_ref):
        i = pl.program_id(0)
        xf = x_ref[...]
        g = g_ref[...]
        r = lax.rsqrt(jnp.mean(xf * xf, axis=-1, keepdims=True) + EPS)
        xh = xf * r
        e = xh * g - t_ref[...]
        lpart = 0.5 * jnp.sum(jnp.sum(e * e, axis=1, keepdims=True), axis=0, keepdims=True) / D
        dy = e / D
        dyg = dy * g
        dx_ref[...] = r * (dyg - xh * jnp.mean(dyg * xh, axis=-1, keepdims=True))
        gpart = jnp.sum(dy * xh, axis=0, keepdims=True)
        lrow = jnp.broadcast_to(lpart, (1, 128))

        @pl.when(i == 0)
        def _():
            dg_ref[...] = gpart
            loss_ref[...] = lrow

        @pl.when(i > 0)
        def _():
            dg_ref[...] += gpart
            loss_ref[...] += lrow

    row = pl.BlockSpec((tt, D), lambda i: (i, 0))
    gsp = pl.BlockSpec((1, D), lambda i: (0, 0))
    return pl.pallas_call(
        body, name="loss_head", grid=(T // tt,),
        in_specs=[row, row, gsp],
        out_specs=[row, gsp, pl.BlockSpec((1, 128), lambda i: (0, 0))],
        out_shape=[jax.ShapeDtypeStruct((T, D), F32), jax.ShapeDtypeStruct((1, D), F32),
                   jax.ShapeDtypeStruct((1, 128), F32)],
        compiler_params=_params("arbitrary"),
    )(x, target, gain)


def _shift_down(u, prev8, first, k):
    rolled = pltpu.roll(u, k, 0)
    rid = lax.broadcasted_iota(jnp.int32, u.shape, 0)
    halo = jnp.where(first, 0.0, prev8)
    out = rolled
    for j in range(k):
        out = jnp.where(rid == j, halo[8 - k + j:8 - k + j + 1, :], out)
    return out


def _conv3(u, prev8, first, w, b):
    return (_shift_down(u, prev8, first, 2) * w[0:1, :] + _shift_down(u, prev8, first, 1) * w[1:2, :]
            + u * w[2:3, :] + b)


def ffn_up_act(xn, w_up, w, b, name, ride=None):
    T, K = xn.shape
    tt = _rows(T, 512)
    nj = D_FF // FF_TC

    def body(x_ref, wu_ref, w_ref, b_ref, u_ref, o_ref, tail):
        first = pl.program_id(1) == 0
        u = _dot(x_ref[...], wu_ref[...])
        u_ref[...] = u
        c = _conv3(u, tail[...], first, w_ref[...], b_ref[...])
        tail[...] = u[tt - 8:, :]
        cg, cv = c[:, :FF_TC], c[:, FF_TC:]
        o_ref[...] = (cg * _sigmoid_tanh(cg) * cv).astype(o_ref.dtype)

    return _call(
        body, name, (nj, T // tt),
        [pl.BlockSpec((tt, K), lambda j, i: (i, 0)),
         pl.BlockSpec((K, 2 * FF_TC), lambda j, i: (0, j)),
         pl.BlockSpec((3, 2 * FF_TC), lambda j, i: (0, j)),
         pl.BlockSpec((1, 2 * FF_TC), lambda j, i: (0, j))],
        [pl.BlockSpec((tt, 2 * FF_TC), lambda j, i: (i, j)), pl.BlockSpec((tt, FF_TC), lambda j, i: (i, j))],
        [jax.ShapeDtypeStruct((T, 2 * D_FF), F32), jax.ShapeDtypeStruct((T, D_FF), MM_DTYPE)],
        [pltpu.VMEM((8, 2 * FF_TC), F32)], ("parallel", "arbitrary"), (xn, w_up, w, b), ride)


def conv_act_bwd(u, da, w, b, name, ride=None):
    T = u.shape[0]
    tt = _rows(T, 512)
    nt = T // tt
    nj = D_FF // FF_TC
    te = tt + 8

    def body(u_ref, p_ref, n_ref, da_ref, dan_ref, w_ref, b_ref, du_ref, dw_ref, db_ref):
        i = pl.program_id(1)
        first = i == 0
        last = i == nt - 1
        w = w_ref[...]
        ue = jnp.concatenate([u_ref[...], n_ref[...]], axis=0)
        dae = jnp.concatenate([da_ref[...], jnp.where(last, 0.0, dan_ref[...])], axis=0)
        um2 = _shift_down(ue, p_ref[...], first, 2)
        um1 = _shift_down(ue, p_ref[...], first, 1)
        c = um2 * w[0:1, :] + um1 * w[1:2, :] + ue * w[2:3, :] + b_ref[...]
        cg, cv = c[:, :FF_TC], c[:, FF_TC:]
        s = _sigmoid_tanh(cg)
        dcg = dae * cv * (s * (1.0 + cg * (1.0 - s)))
        dcv = dae * (cg * s)
        dc = jnp.concatenate([dcg, dcv], axis=1)
        du = (dc * w[2:3, :] + pltpu.roll(dc, te - 1, 0) * w[1:2, :] + pltpu.roll(dc, te - 2, 0) * w[0:1, :])
        du_ref[...] = du[:tt, :].astype(du_ref.dtype)
        dcm = dc[:tt, :]
        dwp = jnp.concatenate([jnp.sum(dcm * um2[:tt, :], axis=0, keepdims=True),
                               jnp.sum(dcm * um1[:tt, :], axis=0, keepdims=True),
                               jnp.sum(dcm * ue[:tt, :], axis=0, keepdims=True)], axis=0)
        dbp = jnp.sum(dcm, axis=0, keepdims=True)

        @pl.when(first)
        def _():
            dw_ref[...] = dwp
            db_ref[...] = dbp

        @pl.when(i > 0)
        def _():
            dw_ref[...] += dwp
            db_ref[...] += dbp

    nb8 = T // 8
    return _call(
        body, name, (nj, nt),
        [pl.BlockSpec((tt, 2 * FF_TC), lambda j, i: (i, j)),
         pl.BlockSpec((8, 2 * FF_TC), lambda j, i: (jnp.maximum(i * (tt // 8) - 1, 0), j)),
         pl.BlockSpec((8, 2 * FF_TC), lambda j, i: (jnp.minimum((i + 1) * (tt // 8), nb8 - 1), j)),
         pl.BlockSpec((tt, FF_TC), lambda j, i: (i, j)),
         pl.BlockSpec((8, FF_TC), lambda j, i: (jnp.minimum((i + 1) * (tt // 8), nb8 - 1), j)),
         pl.BlockSpec((3, 2 * FF_TC), lambda j, i: (0, j)),
         pl.BlockSpec((1, 2 * FF_TC), lambda j, i: (0, j))],
        [pl.BlockSpec((tt, 2 * FF_TC), lambda j, i: (i, j)),
         pl.BlockSpec((3, 2 * FF_TC), lambda j, i: (0, j)),
         pl.BlockSpec((1, 2 * FF_TC), lambda j, i: (0, j))],
        [jax.ShapeDtypeStruct((T, 2 * D_FF), MM_DTYPE), jax.ShapeDtypeStruct((3, 2 * D_FF), F32),
         jax.ShapeDtypeStruct((1, 2 * D_FF), F32)],
        [], ("parallel", "arbitrary"), (u, u, u, da, da, w, b), ride)


def _interleave(a):
    lead = a.shape[:-1]
    nj = D_FF // FF_TC
    return jnp.swapaxes(a.reshape(*lead, 2, nj, FF_TC), -3, -2).reshape(*lead, 2 * D_FF)


def _deinterleave(a):
    lead = a.shape[:-1]
    nj = D_FF // FF_TC
    return jnp.swapaxes(a.reshape(*lead, nj, 2, FF_TC), -3, -2).reshape(*lead, 2 * D_FF)


A_GC = 2
A_TB = A_GC * A_CHUNK


def gate_prep(z, bias128):
    T = z.shape[0]
    tt = _rows(T, 512)

    def body(z_ref, b_ref, gc_ref, gr_ref):
        pre = z_ref[...] + b_ref[...]
        sc = SOFTCAP * jnp.tanh(pre / SOFTCAP)
        lf = jnp.minimum(sc, 0.0) - jnp.log(1.0 + jnp.exp(-jnp.abs(sc)))
        col = lax.broadcasted_iota(jnp.int32, pre.shape, 1)
        isf = jnp.logical_and(col >= A_HEADS, col < 2 * A_HEADS)
        r = lax.broadcasted_iota(jnp.int32, (tt, tt), 0)
        c = lax.broadcasted_iota(jnp.int32, (tt, tt), 1)
        tri = jnp.logical_and(jnp.right_shift(r, 6) == jnp.right_shift(c, 6), c <= r).astype(F32)
        bcum = jnp.dot(tri, jnp.where(isf, lf, 0.0), precision=HI, preferred_element_type=F32)
        g = jnp.where(col < A_HEADS, sc, jnp.where(isf, bcum, 0.0))
        gc_ref[...] = g
        for s in range(tt // 128):
            gr_ref[s] = g[s * 128:(s + 1) * 128, :].T[0:8, :]

    return pl.pallas_call(
        body, name="gate_prep", grid=(T // tt,),
        in_specs=[pl.BlockSpec((tt, 128), lambda i: (i, GATE_COL // 128)),
                  pl.BlockSpec((1, 128), lambda i: (0, 0))],
        out_specs=[pl.BlockSpec((tt, 128), lambda i: (i, 0)),
                   pl.BlockSpec((tt // 128, 8, 128), lambda i: (i, 0, 0))],
        out_shape=[jax.ShapeDtypeStruct((T, 128), F32), jax.ShapeDtypeStruct((T // 128, 8, 128), F32)],
        compiler_params=_params("parallel"),
    )(z, bias128)


def _chunk_decay(A, qh, bc, br, lir, n, m, causal):
    logD = jnp.where(causal, bc - br + lir, -jnp.inf)
    m_inter = bc + m
    m_t = jnp.maximum(m_inter, jnp.max(logD, axis=1, keepdims=True))
    E = jnp.exp(logD - m_t)
    Sm = A * E
    wi = jnp.exp(m_inter - m_t)
    qn = jnp.sum(qh.astype(F32) * n, axis=1, keepdims=True)
    den = jnp.sum(Sm, axis=1, keepdims=True) + wi * qn
    gs = jnp.maximum(jnp.abs(den), jnp.exp(-m_t))
    return E, Sm, wi, den, gs, m_t


def _state_weights(bc, lic, br, lir, m):
    bL = bc[A_CHUNK - 1:A_CHUNK, :]
    m_new = jnp.maximum(bL + m, jnp.max(bL - br + lir, axis=1, keepdims=True))
    wk = jnp.exp(bL - bc + lic - m_new)
    decay = jnp.exp(bL + m - m_new)
    return wk, decay, m_new


def _head_slices(h):
    return (slice(h * A_QK, (h + 1) * A_QK), slice(h * A_V, (h + 1) * A_V))


def mlstm_fwd(z, gcol, grow, hng, ride=None):
    T = z.shape[0]
    NC = T // A_CHUNK
    scale = A_QK ** -0.5

    def body(q_ref, k_ref, v_ref, o_ref, gc_ref, gr_ref, hng_ref, hg_ref, Cs_ref, ns_ref, ms_ref,
             C_sc, n_sc, m_sc):
        @pl.when(pl.program_id(0) == 0)
        def _():
            C_sc[...] = jnp.zeros_like(C_sc)
            n_sc[...] = jnp.zeros_like(n_sc)
            m_sc[...] = jnp.zeros_like(m_sc)

        ri = lax.broadcasted_iota(jnp.int32, (A_CHUNK, A_CHUNK), 0)
        ci = lax.broadcasted_iota(jnp.int32, (A_CHUNK, A_CHUNK), 1)
        causal = ri >= ci
        gr = gr_ref[0]
        for c in range(A_GC):
            rows = slice(c * A_CHUNK, (c + 1) * A_CHUNK)
            gc = gc_ref[rows, :]
            grc = gr[:, c * A_CHUNK:(c + 1) * A_CHUNK]
            for h in range(A_HEADS):
                sk, sv = _head_slices(h)
                qh = (q_ref[rows, sk] * scale).astype(MM_DTYPE)
                kh = k_ref[rows, sk].astype(MM_DTYPE)
                vh = v_ref[rows, sv].astype(MM_DTYPE)
                lic, bc = gc[:, h:h + 1], gc[:, A_HEADS + h:A_HEADS + h + 1]
                lir, br = grc[h:h + 1, :], grc[A_HEADS + h:A_HEADS + h + 1, :]
                C, n, m = C_sc[h], n_sc[h], m_sc[h][:, 0:1]
                Cs_ref[c, h] = C
                ns_ref[c, h] = n
                ms_ref[c, h] = m_sc[h]
                _, Sm, wi, _, gs, _ = _chunk_decay(_dot_nt(qh, kh), qh, bc, br, lir, n, m, causal)
                hh = (_dot(Sm, vh) + wi * _dot(qh, C)) / gs
                hn = hh * lax.rsqrt(jnp.mean(hh * hh, axis=1, keepdims=True) + EPS) * hng_ref[:, sv]
                hg_ref[rows, sv] = (hn * _sigmoid(o_ref[rows, sv])).astype(hg_ref.dtype)
                wk, decay, m_new = _state_weights(bc, lic, br, lir, m)
                kw = kh.astype(F32) * wk
                C_sc[h] = decay * C + _dot_tn(kw, vh)
                n_sc[h] = decay * n + jnp.sum(kw, axis=0, keepdims=True)
                m_sc[h] = jnp.broadcast_to(m_new, (1, 128))

    tok = lambda w, cb: pl.BlockSpec((A_TB, w), lambda i: (i, cb))
    return _call(
        body, "mlstm_fwd", (NC // A_GC,),
        [tok(512, 0), tok(512, 1), tok(1024, 1), tok(1024, 2),
         pl.BlockSpec((A_TB, 128), lambda i: (i, 0)),
         pl.BlockSpec((1, 8, 128), lambda i: (i, 0, 0)),
         pl.BlockSpec((1, 1024), lambda i: (0, 0))],
        [pl.BlockSpec((A_TB, 1024), lambda i: (i, 0)),
         pl.BlockSpec((A_GC, A_HEADS, A_QK, A_V), lambda i: (i, 0, 0, 0)),
         pl.BlockSpec((A_GC, A_HEADS, 1, 128), lambda i: (i, 0, 0, 0)),
         pl.BlockSpec((A_GC, A_HEADS, 1, 128), lambda i: (i, 0, 0, 0))],
        [jax.ShapeDtypeStruct((T, 1024), MM_DTYPE),
         jax.ShapeDtypeStruct((NC, A_HEADS, A_QK, A_V), F32),
         jax.ShapeDtypeStruct((NC, A_HEADS, 1, 128), F32),
         jax.ShapeDtypeStruct((NC, A_HEADS, 1, 128), F32)],
        [pltpu.VMEM((A_HEADS, A_QK, A_V), F32), pltpu.VMEM((A_HEADS, 1, 128), F32),
         pltpu.VMEM((A_HEADS, 1, 128), F32)],
        ("arbitrary",), (z, z, z, z, gcol, grow, hng), ride)


def mlstm_bwd(z, gcol, grow, hng, bias128, Cs, ns, ms, dhg, ride=None):
    T = z.shape[0]
    NC = T // A_CHUNK
    nsteps = NC // A_GC
    scale = A_QK ** -0.5

    def body(q_ref, k_ref, v_ref, o_ref, zg_ref, gc_ref, gr_ref, hng_ref, b_ref, Cs_ref, ns_ref, ms_ref,
             dhg_ref, dz_ref, dgn_ref, dbif_ref, dC_sc, dn_sc):
        @pl.when(pl.program_id(0) == 0)
        def _():
            dC_sc[...] = jnp.zeros_like(dC_sc)
            dn_sc[...] = jnp.zeros_like(dn_sc)
            dgn_ref[...] = jnp.zeros_like(dgn_ref)
            dbif_ref[...] = jnp.zeros_like(dbif_ref)

        ri = lax.broadcasted_iota(jnp.int32, (A_CHUNK, A_CHUNK), 0)
        ci = lax.broadcasted_iota(jnp.int32, (A_CHUNK, A_CHUNK), 1)
        causal = ri >= ci
        upper = (ci >= ri).astype(F32)
        rid = lax.broadcasted_iota(jnp.int32, (A_CHUNK, 1), 0)
        col = lax.broadcasted_iota(jnp.int32, (A_CHUNK, 128), 1)
        gr = gr_ref[0]
        for c in reversed(range(A_GC)):
            rows = slice(c * A_CHUNK, (c + 1) * A_CHUNK)
            gc = gc_ref[rows, :]
            grc = gr[:, c * A_CHUNK:(c + 1) * A_CHUNK]
            dG = jnp.zeros((A_CHUNK, 128), F32)
            hs = []
            for h in range(A_HEADS):
                sk, sv = _head_slices(h)
                s = dict(sk=sk, sv=sv, qh=(q_ref[rows, sk] * scale).astype(MM_DTYPE),
                         kh=k_ref[rows, sk].astype(MM_DTYPE), vh=v_ref[rows, sv].astype(MM_DTYPE),
                         lic=gc[:, h:h + 1], bc=gc[:, A_HEADS + h:A_HEADS + h + 1],
                         lir=grc[h:h + 1, :], br=grc[A_HEADS + h:A_HEADS + h + 1, :],
                         C=Cs_ref[c, h], n=ns_ref[c, h], m=ms_ref[c, h][:, 0:1], dC=dC_sc[h], dn=dn_sc[h])
                s['qf'], s['kf'] = s['qh'].astype(F32), s['kh'].astype(F32)
                s['wk'], s['decay'], _ = _state_weights(s['bc'], s['lic'], s['br'], s['lir'], s['m'])
                hs.append(s)
            for s in hs:
                s['A'] = _dot_nt(s['qh'], s['kh'])
                s['qC'] = _dot(s['qh'], s['C'])
                s['vdC'] = _dot_nt(s['vh'], s['dC'])
                s['kdC'] = _dot(s['kh'], s['dC'])
            for s in hs:
                s['E'], s['Sm'], s['wi'], s['den'], s['gs'], s['m_t'] = _chunk_decay(
                    s['A'], s['qh'], s['bc'], s['br'], s['lir'], s['n'], s['m'], causal)
            for s in hs:
                s['num'] = _dot(s['Sm'], s['vh']) + s['wi'] * s['qC']
            for h, s in enumerate(hs):
                sv, gs = s['sv'], s['gs']
                hh = s['num'] / gs
                r = lax.rsqrt(jnp.mean(hh * hh, axis=1, keepdims=True) + EPS)
                gn = hng_ref[:, sv]
                sg = _sigmoid(o_ref[rows, sv])
                dhg_h = dhg_ref[rows, sv]
                dhn = dhg_h * sg
                dz_ref[rows, 2048 + h * A_V:2048 + (h + 1) * A_V] = dhg_h * (hh * r * gn) * sg * (1.0 - sg)
                dgn_ref[:, sv] += jnp.sum(dhn * hh * r, axis=0, keepdims=True)
                dyg = dhn * gn
                dh = r * dyg - hh * (r * r * r) * jnp.mean(dyg * hh, axis=1, keepdims=True)
                s['dnum'] = dh / gs
                live = (jnp.abs(s['den']) > jnp.exp(-s['m_t'])).astype(F32)
                s['dden'] = -jnp.sum(dh * hh, axis=1, keepdims=True) / gs * jnp.sign(s['den']) * live
            for s in hs:
                s['dnv'] = _dot_nt(s['dnum'], s['vh'])
                s['dnC'] = _dot_nt(s['dnum'], s['C'])
            for s in hs:
                s['dSE'] = jnp.where(causal, s['dnv'] + s['dden'], 0.0) * s['E']
            for s in hs:
                s['dq'] = _dot(s['dSE'], s['kh']) + s['wi'] * (s['dnC'] + s['dden'] * s['n'])
                s['dk_inter'] = s['wk'] * (s['vdC'] + s['dn'])
                s['dk'] = _dot_tn(s['dSE'], s['qh']) + s['dk_inter']
                s['dv'] = _dot_tn(s['Sm'], s['dnum']) + s['wk'] * s['kdC']
                s['dCq'] = _dot_tn(s['qf'] * s['wi'], s['dnum'])
            for h, s in enumerate(hs):
                dq, dk, qf, kf, dC, dn = s['dq'], s['dk'], s['qf'], s['kf'], s['dC'], s['dn']
                dz_ref[rows, s['sk']] = dq * scale
                dz_ref[rows, 512 + h * A_QK:512 + (h + 1) * A_QK] = dk
                dz_ref[rows, 1024 + h * A_V:1024 + (h + 1) * A_V] = s['dv']
                dli = jnp.sum(kf * dk, axis=1, keepdims=True)
                db = jnp.sum(qf * dq, axis=1, keepdims=True) - dli
                usum = jnp.sum(jnp.sum(kf * s['dk_inter'], axis=1, keepdims=True), axis=0, keepdims=True)
                ddecay = (jnp.sum(jnp.sum(dC * s['C'], axis=1, keepdims=True), axis=0, keepdims=True)
                          + jnp.sum(dn * s['n'], axis=1, keepdims=True))
                db = db + jnp.where(rid == A_CHUNK - 1, usum + ddecay * s['decay'], 0.0)
                dG = dG + jnp.where(col == h, dli, 0.0) + jnp.where(col == A_HEADS + h, db, 0.0)
                dC_sc[h] = s['decay'] * dC + s['dCq']
                dn_sc[h] = s['decay'] * dn + jnp.sum(qf * (s['wi'] * s['dden']), axis=0, keepdims=True)
            dlf = jnp.dot(upper, dG, precision=HI, preferred_element_type=F32)
            pre = zg_ref[rows, :] + b_ref[...]
            th = jnp.tanh(pre / SOFTCAP)
            dcap = 1.0 - th * th
            dpre = jnp.where(col < A_HEADS, dG * dcap,
                             jnp.where(col < 2 * A_HEADS, dlf * _sigmoid(-SOFTCAP * th) * dcap, 0.0))
            dz_ref[rows, GATE_COL:GATE_COL + 128] = dpre
            dbif_ref[...] += jnp.sum(dpre, axis=0, keepdims=True)

    rev = lambda i: nsteps - 1 - i
    tok = lambda w, cb: pl.BlockSpec((A_TB, w), lambda i: (rev(i), cb))
    st = lambda a, b: pl.BlockSpec((A_GC, A_HEADS, a, b), lambda i: (rev(i), 0, 0, 0))
    return _call(
        body, "mlstm_bwd", (nsteps,),
        [tok(512, 0), tok(512, 1), tok(1024, 1), tok(1024, 2), tok(128, GATE_COL // 128),
         pl.BlockSpec((A_TB, 128), lambda i: (rev(i), 0)),
         pl.BlockSpec((1, 8, 128), lambda i: (rev(i), 0, 0)),
         pl.BlockSpec((1, 1024), lambda i: (0, 0)),
         pl.BlockSpec((1, 128), lambda i: (0, 0)),
         st(A_QK, A_V), st(1, 128), st(1, 128),
         pl.BlockSpec((A_TB, 1024), lambda i: (rev(i), 0))],
        [pl.BlockSpec((A_TB, A_IN_PAD), lambda i: (rev(i), 0)),
         pl.BlockSpec((1, 1024), lambda i: (0, 0)),
         pl.BlockSpec((1, 128), lambda i: (0, 0))],
        [jax.ShapeDtypeStruct((T, A_IN_PAD), F32), jax.ShapeDtypeStruct((1, 1024), F32),
         jax.ShapeDtypeStruct((1, 128), F32)],
        [pltpu.VMEM((A_HEADS, A_QK, A_V), F32), pltpu.VMEM((A_HEADS, 1, 128), F32)],
        ("arbitrary",), (z, z, z, z, z, gcol, grow, hng, bias128, Cs, ns, ms, dhg), ride)


def _t5_bucket(dist):
    max_exact = REL_BUCKETS // 2
    d = np.maximum(dist, 0)
    log_ratio = np.log(np.maximum(d, 1) / max_exact) / math.log(REL_MAX_DIST / max_exact)
    large = np.minimum(max_exact + (log_ratio * (REL_BUCKETS - max_exact)).astype(np.int64), REL_BUCKETS - 1)
    return np.where(d < max_exact, d, large).astype(np.int32)


def _group_bucket(g):
    delta = B_BLOCK + np.arange(B_BLOCK)[:, None] - np.arange(2 * B_BLOCK)[None, :]
    return _t5_bucket(delta * DILATIONS[g])


def _band_mask(n):
    ri = lax.broadcasted_iota(jnp.int32, (B_BLOCK, 2 * B_BLOCK), 0)
    ci = lax.broadcasted_iota(jnp.int32, (B_BLOCK, 2 * B_BLOCK), 1)
    band = jnp.logical_and(ci >= ri, ci <= ri + B_BLOCK)
    return jnp.logical_and(band, jnp.logical_or(ci >= B_BLOCK, n > 0))


def _both(p_ref, c_ref, sl):
    return jnp.concatenate([p_ref[:, sl], c_ref[:, sl]], axis=0)


def _scores(qh, kh, bias_h, valid):
    return jnp.where(valid, _dot_nt(qh, kh) * (B_DH ** -0.5) + bias_h, -jnp.inf)


def _attn_specs():
    wide = pl.BlockSpec((B_BLOCK, 1024), lambda r, n: (n, r))
    prev = pl.BlockSpec((B_BLOCK, 1024), lambda r, n: (jnp.maximum(n - 1, 0), r))
    narrow = pl.BlockSpec((B_BLOCK, 128), lambda r, n: (n, r))
    bias = pl.BlockSpec((B_HEADS, B_BLOCK, 2 * B_BLOCK), lambda r, n: (0, 0, 0))
    return wide, prev, narrow, bias


def _to_view(read_chunk, sc, o_ref, dil, nc, tt):
    for c in range(nc):
        sc[c] = read_chunk(c)
    for r in range(dil):
        for c in range(nc):
            lo = (r * nc + c) * 128
            o_ref[:, lo:lo + 128] = sc[c, pl.ds(r, tt // dil, stride=dil), :].astype(o_ref.dtype)


def _from_view(read_view, sc, dil, nc, tt):
    for r in range(dil):
        for c in range(nc):
            sc[c, pl.ds(r, tt // dil, stride=dil), :] = read_view((r * nc + c) * 128).astype(F32)


def attn_fwd(qv, kvw, vvw, bias, g):
    dil = DILATIONS[g]
    Tv = qv.shape[0]
    nb = Tv // B_BLOCK
    wide, prev, narrow, bsp = _attn_specs()

    def body(q_ref, kp_ref, kc_ref, vp_ref, vc_ref, b_ref, o_ref, lse_ref):
        valid = _band_mask(pl.program_id(1))
        lse_ref[...] = jnp.zeros_like(lse_ref)
        heads = [slice(h * B_DH, (h + 1) * B_DH) for h in range(B_HEADS)]
        S = [_scores(q_ref[:, sl], _both(kp_ref, kc_ref, sl), b_ref[h], valid) for h, sl in enumerate(heads)]
        P, L = [], []
        for h in range(B_HEADS):
            m = jnp.max(S[h], axis=1, keepdims=True)
            p = jnp.exp(S[h] - m)
            l = jnp.sum(p, axis=1, keepdims=True)
            lse_ref[:, h:h + 1] = m + jnp.log(l)
            P.append(p.astype(MM_DTYPE))
            L.append(l)
        for h, sl in enumerate(heads):
            o_ref[:, sl] = _dot(P[h], _both(vp_ref, vc_ref, sl)) / L[h]

    return pl.pallas_call(
        body, name=f"attn_fwd_g{g}", grid=(dil, nb),
        in_specs=[wide, prev, wide, prev, wide, bsp], out_specs=[wide, narrow],
        out_shape=[jax.ShapeDtypeStruct((Tv, dil * 1024), F32), jax.ShapeDtypeStruct((Tv, dil * 128), F32)],
        compiler_params=_params("parallel", "parallel"),
    )(qv, kvw, kvw, vvw, vvw, bias)


def attn_bwd(qv, kvw, vvw, bias, do_v, lse_v, dl_v, g):
    dil = DILATIONS[g]
    Tv = qv.shape[0]
    nb = Tv // B_BLOCK
    wide, prev, narrow, bsp = _attn_specs()

    def body(q_ref, kp_ref, kc_ref, vp_ref, vc_ref, b_ref, bt_ref, do_ref, lse_ref, dl_ref,
             dq_ref, dkc_ref, dkp_ref, dvc_ref, dvp_ref, db_ref):
        @pl.when(jnp.logical_and(pl.program_id(0) == 0, pl.program_id(1) == 0))
        def _():
            db_ref[...] = jnp.zeros_like(db_ref)

        n = pl.program_id(1)
        valid = _band_mask(n)
        ki = lax.broadcasted_iota(jnp.int32, (2 * B_BLOCK, B_BLOCK), 0)
        qi = lax.broadcasted_iota(jnp.int32, (2 * B_BLOCK, B_BLOCK), 1)
        valid_t = jnp.logical_and(jnp.logical_and(ki >= qi, ki <= qi + B_BLOCK), jnp.logical_or(ki >= B_BLOCK, n > 0))
        lse_t, dl_t = lse_ref[...].T, dl_ref[...].T
        heads = [slice(h * B_DH, (h + 1) * B_DH) for h in range(B_HEADS)]
        scale = B_DH ** -0.5
        PT, DS, DST = [], [], []
        for h, sl in enumerate(heads):
            qh, doh = q_ref[:, sl], do_ref[:, sl].astype(MM_DTYPE)
            kh, vh = _both(kp_ref, kc_ref, sl), _both(vp_ref, vc_ref, sl)
            p = jnp.exp(_scores(qh, kh, b_ref[h], valid) - lse_ref[:, h:h + 1])
            ds = p * (_dot_nt(doh, vh) - dl_ref[:, h:h + 1])
            db_ref[h] += ds
            DS.append((ds * scale).astype(MM_DTYPE))
            pt = jnp.exp(_scores(kh, qh, bt_ref[h], valid_t) - lse_t[h:h + 1, :])
            PT.append(pt.astype(MM_DTYPE))
            DST.append((pt * (_dot_nt(vh, doh) - dl_t[h:h + 1, :]) * scale).astype(MM_DTYPE))
        for h, sl in enumerate(heads):
            qh, doh = q_ref[:, sl], do_ref[:, sl].astype(MM_DTYPE)
            dq_ref[:, sl] = _dot(DS[h], _both(kp_ref, kc_ref, sl)).astype(MM_DTYPE)
            dk = _dot(DST[h], qh).astype(MM_DTYPE)
            dv = _dot(PT[h], doh).astype(MM_DTYPE)
            dkp_ref[:, sl], dkc_ref[:, sl] = dk[:B_BLOCK], dk[B_BLOCK:]
            dvp_ref[:, sl], dvc_ref[:, sl] = dv[:B_BLOCK], dv[B_BLOCK:]

    big = jax.ShapeDtypeStruct((Tv, dil * 1024), MM_DTYPE)
    bsp_t = pl.BlockSpec((B_HEADS, 2 * B_BLOCK, B_BLOCK), lambda r, n: (0, 0, 0))
    return pl.pallas_call(
        body, name=f"attn_bwd_g{g}", grid=(dil, nb),
        in_specs=[wide, prev, wide, prev, wide, bsp, bsp_t, wide, narrow, narrow],
        out_specs=[wide] * 5 + [bsp],
        out_shape=[big] * 5 + [jax.ShapeDtypeStruct((B_HEADS, B_BLOCK, 2 * B_BLOCK), F32)],
        compiler_params=_params("arbitrary", "arbitrary"),
    )(qv, kvw, kvw, vvw, vvw, bias, jnp.swapaxes(bias, 1, 2), do_v, lse_v, dl_v)


def _head_expand():
    e = np.zeros((128, 1024), np.float32)
    for h in range(B_HEADS):
        e[h, h * B_DH:(h + 1) * B_DH] = 1.0
    return e


A_TT = 256


def _view_spec(dil, width):
    return pl.BlockSpec((A_TT // dil, dil * width), lambda i: (i, 0))


def attn_merge(os_v, lses_v):
    T = os_v[0].shape[0]
    tt = A_TT
    expand = jnp.asarray(_head_expand())

    def body(o0, o1, o2, l0, l1, l2, e_ref, ob_ref, of_ref, lse0_ref, lse1_ref, lse2_ref, sc_o, sc_l):
        for gi, (o_ref, l_ref) in enumerate(((o1, l1), (o2, l2))):
            dil = DILATIONS[gi + 1]
            _from_view(lambda lo: o_ref[:, lo:lo + 128], sc_o.at[gi], dil, 8, tt)
            _from_view(lambda lo: l_ref[:, lo:lo + 128], sc_l.at[gi], dil, 1, tt)
        ls = [l0[...], sc_l[0, 0], sc_l[1, 0]]
        m = jnp.maximum(jnp.maximum(ls[0], ls[1]), ls[2])
        ex = [jnp.exp(l - m) for l in ls]
        tot = ex[0] + ex[1] + ex[2]
        lse = m + jnp.log(tot)
        lse0_ref[...] = lse
        _to_view(lambda c: lse, sc_l.at[2], lse1_ref, DILATIONS[1], 1, tt)
        _to_view(lambda c: lse, sc_l.at[2], lse2_ref, DILATIONS[2], 1, tt)
        ws = [e / tot for e in ex]
        for c in range(8):
            cols = slice(c * 128, (c + 1) * 128)
            ecol = e_ref[:, cols]
            spread = [jnp.dot(w, ecol, precision=HI, preferred_element_type=F32) for w in ws]
            out = spread[0] * o0[:, cols] + spread[1] * sc_o[0, c] + spread[2] * sc_o[1, c]
            of_ref[:, cols] = out
            ob_ref[:, cols] = out.astype(ob_ref.dtype)

    wide = pl.BlockSpec((tt, 1024), lambda i: (i, 0))
    return pl.pallas_call(
        body, name="attn_merge", grid=(T // tt,),
        in_specs=[_view_spec(d, 1024) for d in DILATIONS] + [_view_spec(d, 128) for d in DILATIONS]
        + [pl.BlockSpec((128, 1024), lambda i: (0, 0))],
        out_specs=[wide, wide] + [_view_spec(d, 128) for d in DILATIONS],
        out_shape=[jax.ShapeDtypeStruct((T, 1024), MM_DTYPE), jax.ShapeDtypeStruct((T, 1024), F32)]
        + [jax.ShapeDtypeStruct((T // d, d * 128), F32) for d in DILATIONS],
        scratch_shapes=[pltpu.VMEM((2, 8, tt, 128), F32), pltpu.VMEM((3, 1, tt, 128), F32)],
        compiler_params=_params("parallel"),
    )(*os_v, *lses_v, expand)


def attn_prep(datt, out):
    T = datt.shape[0]
    tt = A_TT
    expand_t = jnp.asarray(_head_expand().T.copy())

    def body(d_ref, o_ref, e_ref, do0, do1, do2, dl0, dl1, dl2, sc_d, sc_l):
        delta = jnp.dot(d_ref[...] * o_ref[...], e_ref[...], precision=HI, preferred_element_type=F32)
        do0[...] = d_ref[...].astype(do0.dtype)
        dl0[...] = delta
        for do_ref, dl_ref, dil in ((do1, dl1, DILATIONS[1]), (do2, dl2, DILATIONS[2])):
            _to_view(lambda c: d_ref[:, c * 128:(c + 1) * 128], sc_d, do_ref, dil, 8, tt)
            _to_view(lambda c: delta, sc_l, dl_ref, dil, 1, tt)

    wide = pl.BlockSpec((tt, 1024), lambda i: (i, 0))
    return pl.pallas_call(
        body, name="attn_prep", grid=(T // tt,),
        in_specs=[wide, wide, pl.BlockSpec((1024, 128), lambda i: (0, 0))],
        out_specs=[_view_spec(d, 1024) for d in DILATIONS] + [_view_spec(d, 128) for d in DILATIONS],
        out_shape=[jax.ShapeDtypeStruct((T // d, d * 1024), MM_DTYPE) for d in DILATIONS]
        + [jax.ShapeDtypeStruct((T // d, d * 128), F32) for d in DILATIONS],
        scratch_shapes=[pltpu.VMEM((8, tt, 128), F32), pltpu.VMEM((1, tt, 128), F32)],
        compiler_params=_params("parallel"),
    )(datt, out, expand_t)


def attn_combine(parts):
    T = parts[0][0].shape[0]
    tt = A_TT
    nt = T // tt
    shift = [None] + [B_BLOCK * d // tt for d in DILATIONS[1:]]

    def body(dq0, kc0, vc0, kpa0, kpb0, vpa0, vpb0, dq1, kc1, kp1, vc1, vp1, dq2, kc2, kp2, vc2, vp2,
             dq_ref, dkv_ref, sc):
        i = pl.program_id(0)
        dq_ref[:, 0:1024] = dq0[...].astype(dq_ref.dtype)
        for col, c_ref, pa_ref, pb_ref in ((0, kc0, kpa0, kpb0), (3, vc0, vpa0, vpb0)):
            nxt = jnp.where(i + 1 < nt, pb_ref[:tt // 2, :].astype(F32), 0.0)
            later = jnp.concatenate([pa_ref[tt // 2:, :].astype(F32), nxt], axis=0)
            dkv_ref[:, col * 1024:(col + 1) * 1024] = (c_ref[...].astype(F32) + later).astype(dkv_ref.dtype)
        for g, (dq, kc, kp, vc, vp) in ((1, (dq1, kc1, kp1, vc1, vp1)), (2, (dq2, kc2, kp2, vc2, vp2))):
            dil = DILATIONS[g]
            live = i + shift[g] < nt
            _from_view(lambda lo: dq[:, lo:lo + 128], sc, dil, 8, tt)
            for c in range(8):
                dq_ref[:, g * 1024 + c * 128:g * 1024 + (c + 1) * 128] = sc[c].astype(dq_ref.dtype)
            for col, c_ref, p_ref in ((g, kc, kp), (3 + g, vc, vp)):
                _from_view(lambda lo: c_ref[:, lo:lo + 128].astype(F32)
                           + jnp.where(live, p_ref[:, lo:lo + 128].astype(F32), 0.0), sc, dil, 8, tt)
                for c in range(8):
                    dkv_ref[:, col * 1024 + c * 128:col * 1024 + (c + 1) * 128] = sc[c].astype(dkv_ref.dtype)

    def later_spec(dil, blocks):
        return pl.BlockSpec((tt // dil, dil * 1024), lambda i: (jnp.minimum(i + blocks, nt - 1), 0))

    cur = [_view_spec(d, 1024) for d in DILATIONS]
    in_specs = [cur[0], cur[0], cur[0], cur[0], later_spec(1, 1), cur[0], later_spec(1, 1)]
    args = [parts[0][0], parts[0][1], parts[0][3], parts[0][2], parts[0][2], parts[0][4], parts[0][4]]
    for g in (1, 2):
        in_specs += [cur[g], cur[g], later_spec(DILATIONS[g], shift[g]), cur[g], later_spec(DILATIONS[g], shift[g])]
        args += list(parts[g][:5])
    return pl.pallas_call(
        body, name="attn_combine", grid=(nt,), in_specs=in_specs,
        out_specs=[pl.BlockSpec((tt, 3072), lambda i: (i, 0)), pl.BlockSpec((tt, 6144), lambda i: (i, 0))],
        out_shape=[jax.ShapeDtypeStruct((T, 3072), MM_DTYPE), jax.ShapeDtypeStruct((T, 6144), MM_DTYPE)],
        scratch_shapes=[pltpu.VMEM((8, tt, 128), F32)],
        compiler_params=_params("parallel"),
    )(*args)


def adamw(w, g, m, v, name):
    R, C = w.shape
    tr = R if R * C * 4 <= (1 << 20) else _rows(R, max(8, ((1 << 20) // (C * 4)) // 8 * 8))

    def body(w_ref, g_ref, m_ref, v_ref, d_ref, nm_ref, nv_ref):
        gg = g_ref[...]
        nm = ADAM_B1 * m_ref[...] + (1.0 - ADAM_B1) * gg
        nv = ADAM_B2 * v_ref[...] + (1.0 - ADAM_B2) * (gg * gg)
        m_hat = nm / (1.0 - ADAM_B1 ** ADAM_STEP)
        v_hat = nv / (1.0 - ADAM_B2 ** ADAM_STEP)
        d_ref[...] = -ADAM_LR * (m_hat / (jnp.sqrt(v_hat) + ADAM_EPS) + ADAM_WD * w_ref[...])
        nm_ref[...] = nm
        nv_ref[...] = nv

    blk = pl.BlockSpec((tr, C), lambda i: (i, 0))
    sds = jax.ShapeDtypeStruct((R, C), F32)
    return pl.pallas_call(
        body, name=name, grid=(R // tr,), in_specs=[blk] * 4, out_specs=[blk] * 3, out_shape=[sds] * 3,
        compiler_params=_params("parallel"),
    )(w, g, m, v)


def sum_slots(x, name, out_dtype=F32):
    n, R, C = x.shape
    tr = _rows(R, 256)

    def body(x_ref, o_ref):
        acc = x_ref[0].astype(F32)
        for s in range(1, n):
            acc = acc + x_ref[s].astype(F32)
        o_ref[...] = acc.astype(out_dtype)

    return pl.pallas_call(
        body, name=name, grid=(R // tr,),
        in_specs=[pl.BlockSpec((n, tr, C), lambda i: (0, i, 0))],
        out_specs=pl.BlockSpec((tr, C), lambda i: (i, 0)),
        out_shape=jax.ShapeDtypeStruct((R, C), out_dtype),
        compiler_params=_params("parallel"),
    )(x)


_ANY = pl.BlockSpec(memory_space=pl.ANY)
GROUP_ALL = ([(0, 0, 1), (0, 1, 0), (0, 1, 1), (1, 0, 0), (1, 0, 1), (1, 1, 0), (1, 1, 1)],
             lambda d: 4 * d[0] + 2 * d[1] + d[2])
GROUP_CHIPS = ([(0, 1, 0), (1, 0, 0), (1, 1, 0)], lambda d: 2 * d[0] + d[1])
GROUP_SIBLING = ([(0, 0, 1)], lambda d: d[2])


def _me():
    return lax.axis_index("x"), lax.axis_index("y"), lax.axis_index("c")


def _peer(me, flip):
    return tuple(1 - a if f else a for a, f in zip(me, flip))


class Exchange:
    def __init__(self, x, group, scatter):
        self.flips, self.slot = group
        self.scatter = scatter
        self.n = len(self.flips) + 1
        self.out_shape = jax.ShapeDtypeStruct((self.n,) + x.shape[-2:], x.dtype)
        self.scratch = [pltpu.SemaphoreType.DMA((self.n - 1,)), pltpu.SemaphoreType.DMA((self.n - 1,)),
                        pltpu.SemaphoreType.DMA]

    def _copies(self, x_ref, o_ref, send_sems, recv_sems, local_sem, arrivals):
        me = _me()
        slot = self.slot
        mine = pltpu.make_async_copy(x_ref.at[slot(me)] if self.scatter else x_ref, o_ref.at[slot(me)], local_sem)
        sends, landed = [], []
        for k, flip in enumerate(self.flips):
            peer = _peer(me, flip)
            sends.append(pltpu.make_async_remote_copy(
                src_ref=x_ref.at[slot(peer)] if self.scatter else x_ref, dst_ref=o_ref.at[slot(me)],
                send_sem=send_sems.at[k], recv_sem=recv_sems.at[k], device_id=peer, device_id_type=MESH_ID))
            if arrivals:
                landed.append(pltpu.make_async_remote_copy(
                    src_ref=o_ref.at[slot(me)], dst_ref=o_ref.at[slot(peer)], send_sem=send_sems.at[k],
                    recv_sem=recv_sems.at[k], device_id=peer, device_id_type=MESH_ID))
        return mine, sends, landed

    def start(self, *refs):
        mine, sends, _ = self._copies(*refs, arrivals=False)
        mine.start()
        for cp in sends:
            cp.start()

    def wait(self, *refs):
        mine, sends, arrivals = self._copies(*refs, arrivals=True)
        for cp in arrivals:
            cp.wait_recv()
        for cp in sends:
            cp.wait_send()
        mine.wait()

    def __call__(self, x, name):
        def body(*refs):
            self.start(*refs)
            self.wait(*refs)

        return pl.pallas_call(body, name=name, in_specs=[_ANY], out_specs=_ANY, out_shape=self.out_shape,
                              scratch_shapes=self.scratch)(x)


def group_gather(x, name, group):
    return Exchange(x, group, scatter=False)(x, name)


def group_scatter(x, name, group):
    return Exchange(x, group, scatter=True)(x, name)


def _call(body, name, grid, in_specs, out_specs, out_shape, scratch, semantics, args, ride=None):
    if ride is None:
        return pl.pallas_call(body, name=name, grid=grid, in_specs=in_specs, out_specs=out_specs,
                              out_shape=out_shape, scratch_shapes=scratch,
                              compiler_params=_params(*semantics))(*args)
    x, exch = ride
    n_in, n_out, n_scr = len(in_specs), len(out_specs), len(scratch)

    def at_step(pick):
        hit = None
        for axis, size in enumerate(grid):
            here = pl.program_id(axis) == pick(size)
            hit = here if hit is None else jnp.logical_and(hit, here)
        return hit

    def riding(*refs):
        ins, x_ref = refs[:n_in], refs[n_in]
        outs, o_ref = refs[n_in + 1:n_in + 1 + n_out], refs[n_in + 1 + n_out]
        scr, sems = refs[n_in + 2 + n_out:n_in + 2 + n_out + n_scr], refs[n_in + 2 + n_out + n_scr:]

        @pl.when(at_step(lambda size: 0))
        def _():
            exch.start(x_ref, o_ref, *sems)

        body(*ins, *outs, *scr)

        @pl.when(at_step(lambda size: size - 1))
        def _():
            exch.wait(x_ref, o_ref, *sems)

    return pl.pallas_call(
        riding, name=name, grid=grid, in_specs=list(in_specs) + [_ANY], out_specs=list(out_specs) + [_ANY],
        out_shape=list(out_shape) + [exch.out_shape], scratch_shapes=list(scratch) + exch.scratch,
        compiler_params=_params(*(["arbitrary"] * len(grid))))(*args, x)


WEIGHTS = ['a_norm_g', 'a_w_in', 'a_b_if', 'a_hnorm_g', 'a_w_out', 'kv_norm_g', 'w_kv', 'b_norm_g', 'b_w_q',
           'b_w_out', 'rel_bias', 'f_norm_g', 'f_w_up', 'f_conv_w', 'f_conv_b', 'f_w_down', 'final_norm_g']
SHARD_AXIS = {'a_norm_g': 1, 'a_w_in': 2, 'a_b_if': None, 'a_hnorm_g': 2, 'a_w_out': 1, 'kv_norm_g': None,
              'w_kv': 1, 'b_norm_g': None, 'b_w_q': 2, 'b_w_out': 1, 'rel_bias': None, 'f_norm_g': None,
              'f_w_up': 2, 'f_conv_w': 2, 'f_conv_b': None, 'f_w_down': 1, 'final_norm_g': None}
BIG = ['a_w_in', 'a_w_out', 'w_kv', 'b_w_q', 'b_w_out', 'f_w_up', 'f_w_down']
SMALL = [n for n in WEIGHTS if n not in BIG]
LANES = 1024
PIECES = {'a_w_in': ('a_w_in', None, 2), 'a_w_out': ('a_w_out', None, 1), 'f_w_up0': ('f_w_up', 0, 1),
          'f_w_down0': ('f_w_down', 0, 0), 'w_kv': ('w_kv', None, 1), 'b_w_q': ('b_w_q', None, 2),
          'b_w_out': ('b_w_out', None, 1), 'f_w_up1': ('f_w_up', 1, 1), 'f_w_down1': ('f_w_down', 1, 0)}
LATE = ['w_kv', 'b_w_q', 'b_w_out', 'f_w_up1', 'f_w_down1']
WEIGHT_WAVES = {'first': ['a_w_in', 'a_w_out'], 'ffn0': ['f_w_up0', 'f_w_down0'], 'late': LATE}
GRAD_WAVES = {'late': LATE, 'layer0': ['f_w_up0', 'f_w_down0', 'a_w_out'], 'last': ['a_w_in']}


def _piece(arrays, p):
    leaf, layer, _ = PIECES[p]
    return arrays[leaf] if layer is None else arrays[leaf][layer]


class Packer:
    def __init__(self, pieces, shard):
        self.pieces = pieces
        self.shapes = [_piece(shard, p).shape for p in pieces]
        self.sizes = [math.prod(s) // (2 * LANES) for s in self.shapes]
        self.fill = -sum(self.sizes) % 16
        self.rows = sum(self.sizes) + self.fill

    def my_half(self, shard, half):
        both = jnp.concatenate([_piece(shard, p).astype(MM_DTYPE).reshape(2, -1, LANES) for p in self.pieces], axis=1)
        return jnp.pad(lax.dynamic_index_in_dim(both, half, axis=0, keepdims=False), ((0, self.fill), (0, 0)))

    def full_weights(self, gathered):
        g = gathered.reshape(4, 2, self.rows, LANES)
        out, off = {}, 0
        for p, shp, sz in zip(self.pieces, self.shapes, self.sizes):
            out[p] = _full_from_shards(g[:, :, off:off + sz].reshape((4,) + shp), PIECES[p][2])
            off += sz
        return out

    def grad_slots(self, grads):
        parts = [_shards_from_full(grads[p], PIECES[p][2]).reshape(4, 2, -1, LANES).astype(GRAD_WIRE_DTYPE)
                 for p in self.pieces]
        parts.append(jnp.zeros((4, 2, self.fill, LANES), GRAD_WIRE_DTYPE))
        return jnp.concatenate(parts, axis=2).reshape(8, self.rows, LANES)

    def shard_grads(self, both):
        out, off = {}, 0
        for p, shp, sz in zip(self.pieces, self.shapes, self.sizes):
            out[p] = both[:, off:off + sz].reshape(shp).astype(F32)
            off += sz
        return out


class Overlap:
    def __init__(self, shard, half):
        self.shard, self.half = shard, half
        self.weights = {w: Packer(p, shard) for w, p in WEIGHT_WAVES.items()}
        self.grads = {w: Packer(p, shard) for w, p in GRAD_WAVES.items()}
        self.shard_grads = {}

    def gather_ride(self, wave):
        mine = self.weights[wave].my_half(self.shard, self.half)
        return mine, Exchange(mine, GROUP_ALL, scatter=False)

    def gathered(self, wave, slots):
        return self.weights[wave].full_weights(slots)

    def scatter_ride(self, wave, grads):
        slots = self.grads[wave].grad_slots({p: grads.pop(p) for p in GRAD_WAVES[wave]})
        return slots, Exchange(slots, GROUP_ALL, scatter=True)

    def join_ride(self, wave, received):
        reduced = sum_slots(received, f"sum_grads_{wave}", GRAD_WIRE_DTYPE)
        return reduced, Exchange(reduced, GROUP_SIBLING, scatter=False)

    def joined(self, wave, both):
        self.shard_grads.update(self.grads[wave].shard_grads(both))


def _pad_rows(flat, mult):
    n = flat.shape[0]
    per = LANES * mult
    tot = -(-n // per) * per
    return jnp.pad(flat, (0, tot - n)).reshape(tot // LANES, LANES)


def _full_from_shards(sh, axis):
    return jnp.concatenate([sh[j] for j in range(4)], axis=axis)


def _shards_from_full(full, axis):
    return jnp.stack(jnp.split(full, 4, axis=axis))


def _local_step(x, target, W, overlap=None):
    T = x.shape[0]
    W = dict(W)
    row = lambda a: a.reshape(1, -1).astype(F32)
    w_in = jnp.pad(W['a_w_in'][0], ((0, 0), (0, A_IN_PAD - A_IN)))
    bias128 = jnp.pad(row(W['a_b_if'][0]), ((0, 0), (0, 120)))
    hng = row(W['a_hnorm_g'][0])
    w_up = lambda l: _interleave(W[f'f_w_up{l}'])
    cw = [_interleave(W['f_conv_w'][l].astype(F32)) for l in range(2)]
    cb = [_interleave(row(W['f_conv_b'][l])) for l in range(2)]
    onehots = [(jnp.asarray(_group_bucket(g).reshape(-1, 1)) == jnp.arange(128)[None, :]).astype(F32)
               for g in range(N_GROUPS)]
    rb_t = jnp.pad(W['rel_bias'].astype(F32).T, ((0, 0), (0, 128 - REL_BUCKETS)))
    biases = [mm_nn(rb_t[g * B_HEADS:(g + 1) * B_HEADS], onehots[g].T, f"rel_bias_table_g{g}", exact=True)
              .reshape(B_HEADS, B_BLOCK, 2 * B_BLOCK) for g in range(N_GROUPS)]
    G = {}

    def ffn_fwd(xin, l, ride=None):
        xn, = rms_fwd(xin, [row(W['f_norm_g'][l])], f"ffn{l}_norm")
        u, act, *rode = ffn_up_act(xn, w_up(l), cw[l], cb[l], f"ffn{l}_up_act", ride)
        return mm_nn(act, W[f'f_w_down{l}'], f"ffn{l}_down", res=xin), (xn, u, act), rode

    def ffn_bwd(xin, saved, dout, l, ride=None):
        xn, u, act = saved
        dact = mm_nn(dout, W[f'f_w_down{l}'].T, f"ffn{l}_ddown")
        G[f'f_w_down{l}'] = mm_tn(act, dout, f"ffn{l}_gdown")
        du, gcw, gcb, *rode = conv_act_bwd(u, dact, cw[l], cb[l], f"ffn{l}_dact", ride)
        dxn = mm_nn(du, w_up(l).T, f"ffn{l}_dup")
        G[f'f_w_up{l}'] = _deinterleave(mm_tn(xn, du, f"ffn{l}_gup"))
        dxin, (gn,) = rms_bwd(xin, dout, [(dxn, row(W['f_norm_g'][l]))], f"ffn{l}_dnorm")
        return dxin, _deinterleave(gcw), _deinterleave(gcb), gn, rode

    xn_a, = rms_fwd(x, [row(W['a_norm_g'][0])], "a_norm")
    z = mm_nn(xn_a, w_in, "a_in")
    gcol, grow = gate_prep(z, bias128)
    hg, Cs, ns, ms, *rode = mlstm_fwd(z, gcol, grow, hng, overlap.gather_ride('ffn0') if overlap else None)
    if overlap:
        W.update(overlap.gathered('ffn0', rode[0]))
    x1 = mm_nn(hg, W['a_w_out'][0], "a_out", res=x)
    x2, ffn0, rode = ffn_fwd(x1, 0, overlap.gather_ride('late') if overlap else None)
    if overlap:
        W.update(overlap.gathered('late', rode[0]))
    xn_kv, xn_b = rms_fwd(x2, [row(W['kv_norm_g']), row(W['b_norm_g'][0])], "b_norms")
    gcols = lambda w, c: w[:, c * 1024:(c + 1) * 1024]
    qv = [mm_view(xn_b, gcols(W['b_w_q'][0], g), f"q_proj_g{g}", DILATIONS[g]) for g in range(N_GROUPS)]
    kvw = [mm_view(xn_kv, gcols(W['w_kv'], g), f"k_proj_g{g}", DILATIONS[g]) for g in range(N_GROUPS)]
    vvw = [mm_view(xn_kv, gcols(W['w_kv'], 3 + g), f"v_proj_g{g}", DILATIONS[g]) for g in range(N_GROUPS)]
    os_, lses = zip(*[attn_fwd(qv[g], kvw[g], vvw[g], biases[g], g) for g in range(N_GROUPS)])
    att, att_f, *lse_v = attn_merge(os_, lses)
    x3 = mm_nn(att, W['b_w_out'][0], "b_out", res=x2)
    x4, ffn1, _ = ffn_fwd(x3, 1)
    dx4, g_final, loss = loss_head(x4, target, row(W['final_norm_g']))
    G['final_norm_g'] = g_final.reshape(-1)

    dx3, gcw1, gcb1, gn1, _ = ffn_bwd(x3, ffn1, dx4, 1)
    datt = mm_nn(dx3, W['b_w_out'][0].T, "b_dout")
    G['b_w_out'] = mm_tn(att, dx3, "b_gout")[None]
    prep = attn_prep(datt, att_f)
    do_v, dl_v = prep[:3], prep[3:]
    parts = [attn_bwd(qv[g], kvw[g], vvw[g], biases[g], do_v[g], lse_v[g], dl_v[g], g) for g in range(N_GROUPS)]
    dq_all, dkv = attn_combine(parts)
    grb = []
    for g in range(N_GROUPS):
        gb = mm_nn(parts[g][5].reshape(B_HEADS, -1), onehots[g], f"rel_bias_g{g}", exact=True)
        grb.append(gb[:, :REL_BUCKETS].T)
    G['rel_bias'] = jnp.concatenate(grb, axis=1)
    dxn_b = mm_nn(dq_all, W['b_w_q'][0].T, "q_dproj")
    G['b_w_q'] = mm_tn(xn_b, dq_all, "q_gproj")[None]
    dxn_kv = mm_nn(dkv, W['w_kv'].T, "kv_dproj")
    G['w_kv'] = mm_tn(xn_kv, dkv, "kv_gproj")
    dx2, (g_kvn, g_bn) = rms_bwd(x2, dx3, [(dxn_kv, row(W['kv_norm_g'])), (dxn_b, row(W['b_norm_g'][0]))],
                                 "b_dnorms")
    G['kv_norm_g'] = g_kvn.reshape(-1)
    G['b_norm_g'] = g_bn
    dx1, gcw0, gcb0, gn0, late_slots = ffn_bwd(x1, ffn0, dx2, 0, overlap.scatter_ride('late', G) if overlap else None)
    G['f_conv_w'] = jnp.stack([gcw0, gcw1])
    G['f_conv_b'] = jnp.concatenate([gcb0, gcb1], axis=0)
    G['f_norm_g'] = jnp.concatenate([gn0, gn1], axis=0)
    dhg = mm_nn(dx1, W['a_w_out'][0].T, "a_dout")
    G['a_w_out'] = mm_tn(hg, dx1, "a_gout")[None]
    dz, g_hn, g_bif, *layer0_slots = mlstm_bwd(z, gcol, grow, hng, bias128, Cs, ns, ms, dhg,
                                               overlap.scatter_ride('layer0', G) if overlap else None)
    G['a_hnorm_g'] = g_hn.reshape(1, A_HEADS, A_V)
    G['a_b_if'] = g_bif[:, :2 * A_HEADS]
    if overlap:
        dxn_a, both = mm_nn(dz, w_in.T, "a_din", ride=overlap.join_ride('late', late_slots[0]))
        overlap.joined('late', both)
        g_in, both = mm_tn(xn_a, dz, "a_gin", ride=overlap.join_ride('layer0', layer0_slots[0]))
        overlap.joined('layer0', both)
    else:
        dxn_a = mm_nn(dz, w_in.T, "a_din")
        g_in = mm_tn(xn_a, dz, "a_gin")
    G['a_w_in'] = g_in[:, :A_IN][None]
    grad_x, (g_an,) = rms_bwd(x, dx1, [(dxn_a, row(W['a_norm_g'][0]))], "a_dnorm")
    G['a_norm_g'] = g_an
    return loss, grad_x, G


def kernel(x, a_norm_g, a_w_in, a_b_if, a_hnorm_g, a_w_out, kv_norm_g, w_kv, b_norm_g, b_w_q, b_w_out, rel_bias, f_norm_g, f_w_up, f_conv_w, f_conv_b, f_w_down, final_norm_g, loss_target, m_a_norm_g, m_a_w_in, m_a_b_if, m_a_hnorm_g, m_a_w_out, m_kv_norm_g, m_w_kv, m_b_norm_g, m_b_w_q, m_b_w_out, m_rel_bias, m_f_norm_g, m_f_w_up, m_f_conv_w, m_f_conv_b, m_f_w_down, m_final_norm_g, v_a_norm_g, v_a_w_in, v_a_b_if, v_a_hnorm_g, v_a_w_out, v_kv_norm_g, v_w_kv, v_b_norm_g, v_b_w_q, v_b_w_out, v_rel_bias, v_f_norm_g, v_f_w_up, v_f_conv_w, v_f_conv_b, v_f_w_down, v_final_norm_g):
    given = dict(locals())
    shard = {n: given[n] for n in WEIGHTS}
    mom = {n: given["m_" + n] for n in WEIGHTS}
    var = {n: given["v_" + n] for n in WEIGHTS}
    cx, cy, cc = _me()
    chip = 2 * cx + cy

    overlap = Overlap(shard, cc)
    mine, gather = overlap.gather_ride('first')
    W = overlap.gathered('first', gather(mine, "gather_weights"))
    sharded_small = [n for n in SMALL if SHARD_AXIS[n] is not None]
    ssz = [shard[n].size for n in sharded_small]
    sflat = jnp.concatenate([shard[n].reshape(-1) for n in sharded_small])
    sg = group_gather(_pad_rows(sflat, 8), "gather_small", GROUP_CHIPS).reshape(4, -1)
    off = 0
    for n, sz in zip(sharded_small, ssz):
        W[n] = _full_from_shards(sg[:, off:off + sz].reshape((4,) + shard[n].shape), SHARD_AXIS[n])
        off += sz
    for n in SMALL:
        if SHARD_AXIS[n] is None:
            W[n] = shard[n]

    loss_row, grad_x, G = _local_step(x[0], loss_target[0], W, overlap)

    slots, scatter = overlap.scatter_ride('last', G)
    reduced, join = overlap.join_ride('last', scatter(slots, "scatter_grads"))
    overlap.joined('last', join(reduced, "join_halves"))
    by_piece = overlap.shard_grads
    gsh = {}
    for n in BIG:
        layers = [p for p in PIECES if PIECES[p][0] == n]
        gsh[n] = by_piece[n] if layers == [n] else jnp.stack([by_piece[p] for p in layers])
    small_parts = [loss_row[0, 0:1]] + [G[n].reshape(-1) for n in SMALL]
    small_sz = [p.shape[0] for p in small_parts]
    small = sum_slots(group_gather(_pad_rows(jnp.concatenate(small_parts), 8), "gather_small_grads", GROUP_ALL),
                      "sum_small_grads").reshape(-1)
    loss = small[0]
    off = 1
    for n, sz in zip(SMALL, small_sz[1:]):
        full = small[off:off + sz].reshape(W[n].shape)
        off += sz
        if SHARD_AXIS[n] is None:
            gsh[n] = full
        else:
            gsh[n] = lax.dynamic_index_in_dim(_shards_from_full(full, SHARD_AXIS[n]), chip, 0, keepdims=False)

    delta, new_m, new_v = {}, {}, {}
    for n in WEIGHTS:
        shp = shard[n].shape
        two = lambda a: a.reshape(-1, shp[-1])
        d, nm, nv = adamw(two(shard[n]), two(gsh[n]), two(mom[n]), two(var[n]), f"adamw_{n}")
        delta[n], new_m[n], new_v[n] = d.reshape(shp), nm.reshape(shp), nv.reshape(shp)
    return (loss, grad_x[None], *[gsh[n] for n in WEIGHTS], *[delta[n] for n in WEIGHTS],
            *[new_m[n] for n in WEIGHTS], *[new_v[n] for n in WEIGHTS])
```

```python
import functools
import math

import numpy as np
import jax
import jax.numpy as jnp
from jax import lax
from jax.experimental import pallas as pl
from jax.experimental.pallas import tpu as pltpu

F32 = jnp.float32
BF16 = jnp.bfloat16
MM_DTYPE = jnp.bfloat16
GRAD_WIRE_DTYPE = jnp.bfloat16
HI = lax.Precision.HIGHEST

D_MODEL = 1024
A_HEADS = 4
A_QK = 128
A_V = 256
A_CHUNK = 64
A_IN = 3080
A_IN_PAD = 3200
GATE_COL = 3072
SOFTCAP = 15.0
N_GROUPS = 3
B_HEADS = 16
B_DH = 64
B_BLOCK = 128
DILATIONS = (1, 4, 16)
WINDOWS = (128, 512, 2048)
REL_BUCKETS = 32
REL_MAX_DIST = 2048
D_FF = 2816
FF_TC = 256
EPS = 1e-6
ADAM_LR, ADAM_B1, ADAM_B2, ADAM_EPS, ADAM_WD, ADAM_STEP = 0.001, 0.9, 0.999, 1e-08, 0.01, 10

VMEM_LIMIT = 56 * 1024 * 1024
NT_DIMS = (((1,), (1,)), ((), ()))
TN_DIMS = (((0,), (0,)), ((), ()))
MESH_ID = pl.DeviceIdType.MESH


def _params(*sem):
    return pltpu.CompilerParams(dimension_semantics=sem, vmem_limit_bytes=VMEM_LIMIT)


def _tile(n, cap):
    if n <= cap:
        return n
    best = None
    for t in range(128, cap + 1, 128):
        if n % t == 0:
            best = t
    assert best is not None, (n, cap)
    return best


def _rows(n, cap):
    if n <= cap:
        return n
    for t in range(cap // 8 * 8, 7, -8):
        if n % t == 0:
            return t
    raise ValueError((n, cap))


def _dot(a, b):
    return jnp.dot(a.astype(MM_DTYPE), b.astype(MM_DTYPE), preferred_element_type=F32)


def _dot_nt(a, b):
    return lax.dot_general(a.astype(MM_DTYPE), b.astype(MM_DTYPE), NT_DIMS, preferred_element_type=F32)


def _dot_tn(a, b):
    return lax.dot_general(a.astype(MM_DTYPE), b.astype(MM_DTYPE), TN_DIMS, preferred_element_type=F32)


def _sigmoid(x):
    return 1.0 / (1.0 + jnp.exp(-x))


def _sigmoid_tanh(x):
    return 0.5 * jnp.tanh(0.5 * x) + 0.5


def mm_nn(a, b, name, res=None, out_dtype=F32, exact=False, ride=None):
    M, K = a.shape
    N = b.shape[1]
    tm = _rows(M, 512)

    def footprint(tn):
        return 2 * (tm * K * a.dtype.itemsize + K * tn * b.dtype.itemsize) + 2 * tm * tn * 4 * (1 if res is None else 2)

    budget = 46 * 1024 * 1024
    tn = N if N <= 3328 and footprint(N) <= budget else _tile(N, 1536)
    tk = K if footprint(tn) <= budget else _tile(K, 1536)
    nk = K // tk

    def body(*refs):
        if res is None:
            a_ref, b_ref, o_ref, acc = refs
            r_ref = None
        else:
            a_ref, b_ref, r_ref, o_ref, acc = refs
        if exact:
            p = jnp.dot(a_ref[...], b_ref[...], precision=HI, preferred_element_type=F32)
        else:
            p = _dot(a_ref[...], b_ref[...])

        def finish(total):
            if r_ref is not None:
                total = total + r_ref[...]
            o_ref[...] = total.astype(out_dtype)

        if nk == 1:
            finish(p)
        else:
            k = pl.program_id(2)

            @pl.when(k == 0)
            def _():
                acc[...] = p

            @pl.when(jnp.logical_and(k > 0, k < nk - 1))
            def _():
                acc[...] += p

            @pl.when(k == nk - 1)
            def _():
                finish(acc[...] + p)

    in_specs = [pl.BlockSpec((tm, tk), lambda j, i, k: (i, k)),
                pl.BlockSpec((tk, tn), lambda j, i, k: (k, j))]
    args = [a, b]
    if res is not None:
        in_specs.append(pl.BlockSpec((tm, tn), lambda j, i, k: (i, j)))
        args.append(res)
    acc_shape = (tm, tn) if nk > 1 else (8, 128)
    outs = _call(body, name, (N // tn, M // tm, nk), in_specs, [pl.BlockSpec((tm, tn), lambda j, i, k: (i, j))],
                 [jax.ShapeDtypeStruct((M, N), out_dtype)], [pltpu.VMEM(acc_shape, F32)],
                 ("parallel", "parallel", "arbitrary"), args, ride)
    return outs[0] if ride is None else outs


def mm_view(a, b, name, dil):
    T, K = a.shape
    tm = 512

    def body(a_ref, b_ref, o_ref, sc):
        p = _dot(a_ref[...], b_ref[...])
        if dil == 1:
            o_ref[...] = p.astype(o_ref.dtype)
        else:
            _to_view(lambda c: p[:, c * 128:(c + 1) * 128], sc, o_ref, dil, 8, tm)

    return pl.pallas_call(
        body, name=name, grid=(T // tm,),
        in_specs=[pl.BlockSpec((tm, K), lambda i: (i, 0)), pl.BlockSpec((K, 1024), lambda i: (0, 0))],
        out_specs=pl.BlockSpec((tm // dil, dil * 1024), lambda i: (i, 0)),
        out_shape=jax.ShapeDtypeStruct((T // dil, dil * 1024), MM_DTYPE),
        scratch_shapes=[pltpu.VMEM((8, tm, 128), F32)],
        compiler_params=_params("parallel"),
    )(a, b)


def mm_tn(a, g, name, ride=None):
    T, Ka = a.shape
    N = g.shape[1]
    tka, tn, tt = _tile(Ka, 1536), _tile(N, 1536), _rows(T, 1024)
    nt = T // tt

    def body(a_ref, g_ref, o_ref):
        t = pl.program_id(2)
        p = _dot_tn(a_ref[...], g_ref[...])

        @pl.when(t == 0)
        def _():
            o_ref[...] = p

        @pl.when(t > 0)
        def _():
            o_ref[...] += p

    outs = _call(body, name, (Ka // tka, N // tn, nt),
                 [pl.BlockSpec((tt, tka), lambda i, j, t: (t, i)), pl.BlockSpec((tt, tn), lambda i, j, t: (t, j))],
                 [pl.BlockSpec((tka, tn), lambda i, j, t: (i, j))], [jax.ShapeDtypeStruct((Ka, N), F32)], [],
                 ("parallel", "parallel", "arbitrary"), (a, g), ride)
    return outs[0] if ride is None else outs


def rms_fwd(x, gains, name):
    T, D = x.shape
    tt = _rows(T, 512)
    ng = len(gains)

    def body(*refs):
        x_ref = refs[0]
        g_refs = refs[1:1 + ng]
        o_refs = refs[1 + ng:]
        xf = x_ref[...]
        y = xf * lax.rsqrt(jnp.mean(xf * xf, axis=-1, keepdims=True) + EPS)
        for g_ref, o_ref in zip(g_refs, o_refs):
            o_ref[...] = (y * g_ref[...]).astype(o_ref.dtype)

    row = pl.BlockSpec((tt, D), lambda i: (i, 0))
    gsp = pl.BlockSpec((1, D), lambda i: (0, 0))
    return pl.pallas_call(
        body, name=name, grid=(T // tt,),
        in_specs=[row] + [gsp] * ng, out_specs=[row] * ng,
        out_shape=[jax.ShapeDtypeStruct((T, D), MM_DTYPE)] * ng,
        compiler_params=_params("parallel"),
    )(x, *gains)


def rms_bwd(x, dres, branches, name):
    T, D = x.shape
    tt = _rows(T, 256)
    nb = len(branches)

    def body(*refs):
        x_ref, r_ref = refs[0], refs[1]
        dy_refs = refs[2:2 + nb]
        g_refs = refs[2 + nb:2 + 2 * nb]
        dx_ref = refs[2 + 2 * nb]
        dg_refs = refs[3 + 2 * nb:]
        i = pl.program_id(0)
        xf = x_ref[...]
        r = lax.rsqrt(jnp.mean(xf * xf, axis=-1, keepdims=True) + EPS)
        xh = xf * r
        dx = r_ref[...]
        for dy_ref, g_ref, dg_ref in zip(dy_refs, g_refs, dg_refs):
            dy = dy_ref[...].astype(F32)
            dyg = dy * g_ref[...]
            dx = dx + r * (dyg - xh * jnp.mean(dyg * xh, axis=-1, keepdims=True))
            part = jnp.sum(dy * xh, axis=0, keepdims=True)

            @pl.when(i == 0)
            def _():
                dg_ref[...] = part

            @pl.when(i > 0)
            def _():
                dg_ref[...] += part
        dx_ref[...] = dx

    row = pl.BlockSpec((tt, D), lambda i: (i, 0))
    gsp = pl.BlockSpec((1, D), lambda i: (0, 0))
    outs = pl.pallas_call(
        body, name=name, grid=(T // tt,),
        in_specs=[row, row] + [row] * nb + [gsp] * nb,
        out_specs=[row] + [gsp] * nb,
        out_shape=[jax.ShapeDtypeStruct((T, D), F32)] + [jax.ShapeDtypeStruct((1, D), F32)] * nb,
        compiler_params=_params("arbitrary"),
    )(x, dres, *[b[0] for b in branches], *[b[1] for b in branches])
    return outs[0], outs[1:]


def loss_head(x, target, gain):
    T, D = x.shape
    tt = _rows(T, 256)

    def body(x_ref, t_ref, g_ref, dx_ref, dg_ref, loss_ref):
        i = pl.program_id(0)
        xf = x_ref[...]
        g = g_ref[...]
        r = lax.rsqrt(jnp.mean(xf * xf, axis=-1, keepdims=True) + EPS)
        xh = xf * r
        e = xh * g - t_ref[...]
        lpart = 0.5 * jnp.sum(jnp.sum(e * e, axis=1, keepdims=True), axis=0, keepdims=True) / D
        dy = e / D
        dyg = dy * g
        dx_ref[...] = r * (dyg - xh * jnp.mean(dyg * xh, axis=-1, keepdims=True))
        gpart = jnp.sum(dy * xh, axis=0, keepdims=True)
        lrow = jnp.broadcast_to(lpart, (1, 128))

        @pl.when(i == 0)
        def _():
            dg_ref[...] = gpart
            loss_ref[...] = lrow

        @pl.when(i > 0)
        def _():
            dg_ref[...] += gpart
            loss_ref[...] += lrow

    row = pl.BlockSpec((tt, D), lambda i: (i, 0))
    gsp = pl.BlockSpec((1, D), lambda i: (0, 0))
    return pl.pallas_call(
        body, name="loss_head", grid=(T // tt,),
        in_specs=[row, row, gsp],
        out_specs=[row, gsp, pl.BlockSpec((1, 128), lambda i: (0, 0))],
        out_shape=[jax.ShapeDtypeStruct((T, D), F32), jax.ShapeDtypeStruct((1, D), F32),
                   jax.ShapeDtypeStruct((1, 128), F32)],
        compiler_params=_params("arbitrary"),
    )(x, target, gain)


def _conv3(u, prev8, first, w, b):
    ext = jnp.concatenate([jnp.where(first, 0.0, prev8), u], axis=0)
    um1 = pltpu.roll(ext, 1, 0)[8:, :]
    um2 = pltpu.roll(ext, 2, 0)[8:, :]
    return um2 * w[0:1, :] + um1 * w[1:2, :] + u * w[2:3, :] + b


def ffn_up_act(xn, w_up, w, b, name, ride=None):
    T, K = xn.shape
    tt = _rows(T, 512)
    nj = D_FF // FF_TC

    def body(x_ref, wu_ref, w_ref, b_ref, u_ref, c_ref, o_ref, tail):
        first = pl.program_id(1) == 0
        u = _dot(x_ref[...], wu_ref[...])
        u_ref[...] = u
        c = _conv3(u, tail[...], first, w_ref[...], b_ref[...])
        c_ref[...] = c
        tail[...] = u[tt - 8:, :]
        cg, cv = c[:, :FF_TC], c[:, FF_TC:]
        o_ref[...] = (cg * _sigmoid_tanh(cg) * cv).astype(o_ref.dtype)

    return _call(
        body, name, (nj, T // tt),
        [pl.BlockSpec((tt, K), lambda j, i: (i, 0)),
         pl.BlockSpec((K, 2 * FF_TC), lambda j, i: (0, j)),
         pl.BlockSpec((3, 2 * FF_TC), lambda j, i: (0, j)),
         pl.BlockSpec((1, 2 * FF_TC), lambda j, i: (0, j))],
        [pl.BlockSpec((tt, 2 * FF_TC), lambda j, i: (i, j)), pl.BlockSpec((tt, 2 * FF_TC), lambda j, i: (i, j)),
         pl.BlockSpec((tt, FF_TC), lambda j, i: (i, j))],
        [jax.ShapeDtypeStruct((T, 2 * D_FF), F32), jax.ShapeDtypeStruct((T, 2 * D_FF), F32),
         jax.ShapeDtypeStruct((T, D_FF), MM_DTYPE)],
        [pltpu.VMEM((8, 2 * FF_TC), F32)], ("parallel", "arbitrary"), (xn, w_up, w, b), ride)


def conv_act_bwd(u, c, da, w, name, ride=None):
    T = u.shape[0]
    tt = _rows(T, 512)
    nt = T // tt
    nj = D_FF // FF_TC
    te = tt + 8

    def body(u_ref, c_ref, cn_ref, da_ref, dan_ref, w_ref, du_ref, dw_ref, db_ref):
        i = pl.program_id(1)
        first = i == 0
        last = i == nt - 1
        w = w_ref[...]
        ce = jnp.concatenate([c_ref[...], cn_ref[...]], axis=0)
        dae = jnp.concatenate([da_ref[...], jnp.where(last, 0.0, dan_ref[...])], axis=0)
        cg, cv = ce[:, :FF_TC], ce[:, FF_TC:]
        s = _sigmoid_tanh(cg)
        dcg = dae * cv * (s * (1.0 + cg * (1.0 - s)))
        dcv = dae * (cg * s)
        dc = jnp.concatenate([dcg, dcv], axis=1)
        dc1 = pltpu.roll(dc, te - 1, 0)[:tt, :]
        dc2 = pltpu.roll(dc, te - 2, 0)[:tt, :]
        dcm = dc[:tt, :]
        du_ref[...] = (dcm * w[2:3, :] + dc1 * w[1:2, :] + dc2 * w[0:1, :]).astype(du_ref.dtype)
        um = u_ref[...]
        dwp = jnp.concatenate([jnp.sum(dc2 * um, axis=0, keepdims=True),
                               jnp.sum(dc1 * um, axis=0, keepdims=True),
                               jnp.sum(dcm * um, axis=0, keepdims=True)], axis=0)
        dbp = jnp.sum(dcm, axis=0, keepdims=True)

        @pl.when(first)
        def _():
            dw_ref[...] = dwp
            db_ref[...] = dbp

        @pl.when(i > 0)
        def _():
            dw_ref[...] += dwp
            db_ref[...] += dbp

    nb8 = T // 8
    return _call(
        body, name, (nj, nt),
        [pl.BlockSpec((tt, 2 * FF_TC), lambda j, i: (i, j)),
         pl.BlockSpec((tt, 2 * FF_TC), lambda j, i: (i, j)),
         pl.BlockSpec((8, 2 * FF_TC), lambda j, i: (jnp.minimum((i + 1) * (tt // 8), nb8 - 1), j)),
         pl.BlockSpec((tt, FF_TC), lambda j, i: (i, j)),
         pl.BlockSpec((8, FF_TC), lambda j, i: (jnp.minimum((i + 1) * (tt // 8), nb8 - 1), j)),
         pl.BlockSpec((3, 2 * FF_TC), lambda j, i: (0, j))],
        [pl.BlockSpec((tt, 2 * FF_TC), lambda j, i: (i, j)),
         pl.BlockSpec((3, 2 * FF_TC), lambda j, i: (0, j)),
         pl.BlockSpec((1, 2 * FF_TC), lambda j, i: (0, j))],
        [jax.ShapeDtypeStruct((T, 2 * D_FF), MM_DTYPE), jax.ShapeDtypeStruct((3, 2 * D_FF), F32),
         jax.ShapeDtypeStruct((1, 2 * D_FF), F32)],
        [], ("parallel", "arbitrary"), (u, c, c, da, da, w), ride)


def _interleave(a):
    lead = a.shape[:-1]
    nj = D_FF // FF_TC
    return jnp.swapaxes(a.reshape(*lead, 2, nj, FF_TC), -3, -2).reshape(*lead, 2 * D_FF)


def _deinterleave(a):
    lead = a.shape[:-1]
    nj = D_FF // FF_TC
    return jnp.swapaxes(a.reshape(*lead, nj, 2, FF_TC), -3, -2).reshape(*lead, 2 * D_FF)


A_GC = 2
A_TB = A_GC * A_CHUNK


def gate_prep(z, bias128):
    T = z.shape[0]
    tt = _rows(T, 512)

    def body(z_ref, b_ref, gc_ref, gr_ref):
        pre = z_ref[...] + b_ref[...]
        sc = SOFTCAP * jnp.tanh(pre / SOFTCAP)
        lf = jnp.minimum(sc, 0.0) - jnp.log(1.0 + jnp.exp(-jnp.abs(sc)))
        col = lax.broadcasted_iota(jnp.int32, pre.shape, 1)
        isf = jnp.logical_and(col >= A_HEADS, col < 2 * A_HEADS)
        r = lax.broadcasted_iota(jnp.int32, (tt, tt), 0)
        c = lax.broadcasted_iota(jnp.int32, (tt, tt), 1)
        tri = jnp.logical_and(jnp.right_shift(r, 6) == jnp.right_shift(c, 6), c <= r).astype(F32)
        bcum = jnp.dot(tri, jnp.where(isf, lf, 0.0), precision=HI, preferred_element_type=F32)
        g = jnp.where(col < A_HEADS, sc, jnp.where(isf, bcum, 0.0))
        gc_ref[...] = g
        for s in range(tt // 128):
            gr_ref[s] = g[s * 128:(s + 1) * 128, :].T[0:8, :]

    return pl.pallas_call(
        body, name="gate_prep", grid=(T // tt,),
        in_specs=[pl.BlockSpec((tt, 128), lambda i: (i, GATE_COL // 128)),
                  pl.BlockSpec((1, 128), lambda i: (0, 0))],
        out_specs=[pl.BlockSpec((tt, 128), lambda i: (i, 0)),
                   pl.BlockSpec((tt // 128, 8, 128), lambda i: (i, 0, 0))],
        out_shape=[jax.ShapeDtypeStruct((T, 128), F32), jax.ShapeDtypeStruct((T // 128, 8, 128), F32)],
        compiler_params=_params("parallel"),
    )(z, bias128)


def _chunk_decay(A, qh, bc, br, lir, n, m, causal):
    logD = jnp.where(causal, bc - br + lir, -jnp.inf)
    m_inter = bc + m
    m_t = jnp.maximum(m_inter, jnp.max(logD, axis=1, keepdims=True))
    E = jnp.exp(logD - m_t)
    Sm = A * E
    wi = jnp.exp(m_inter - m_t)
    qn = jnp.sum(qh.astype(F32) * n, axis=1, keepdims=True)
    den = jnp.sum(Sm, axis=1, keepdims=True) + wi * qn
    gs = jnp.maximum(jnp.abs(den), jnp.exp(-m_t))
    return E, Sm, wi, den, gs, m_t


def _state_weights(bc, lic, br, lir, m):
    bL = bc[A_CHUNK - 1:A_CHUNK, :]
    m_new = jnp.maximum(bL + m, jnp.max(bL - br + lir, axis=1, keepdims=True))
    wk = jnp.exp(bL - bc + lic - m_new)
    decay = jnp.exp(bL + m - m_new)
    return wk, decay, m_new


def _head_slices(h):
    return (slice(h * A_QK, (h + 1) * A_QK), slice(h * A_V, (h + 1) * A_V))


def mlstm_fwd(z, gcol, grow, hng, ride=None):
    T = z.shape[0]
    NC = T // A_CHUNK
    scale = A_QK ** -0.5

    def body(q_ref, k_ref, v_ref, o_ref, gc_ref, gr_ref, hng_ref, hg_ref, Cs_ref, ns_ref, ms_ref,
             C_sc, n_sc, m_sc):
        @pl.when(pl.program_id(0) == 0)
        def _():
            C_sc[...] = jnp.zeros_like(C_sc)
            n_sc[...] = jnp.zeros_like(n_sc)
            m_sc[...] = jnp.zeros_like(m_sc)

        ri = lax.broadcasted_iota(jnp.int32, (A_CHUNK, A_CHUNK), 0)
        ci = lax.broadcasted_iota(jnp.int32, (A_CHUNK, A_CHUNK), 1)
        causal = ri >= ci
        gr = gr_ref[0]
        for c in range(A_GC):
            rows = slice(c * A_CHUNK, (c + 1) * A_CHUNK)
            gc = gc_ref[rows, :]
            grc = gr[:, c * A_CHUNK:(c + 1) * A_CHUNK]
            for h in range(A_HEADS):
                sk, sv = _head_slices(h)
                qh = (q_ref[rows, sk] * scale).astype(MM_DTYPE)
                kh = k_ref[rows, sk].astype(MM_DTYPE)
                vh = v_ref[rows, sv].astype(MM_DTYPE)
                lic, bc = gc[:, h:h + 1], gc[:, A_HEADS + h:A_HEADS + h + 1]
                lir, br = grc[h:h + 1, :], grc[A_HEADS + h:A_HEADS + h + 1, :]
                C, n, m = C_sc[h], n_sc[h], m_sc[h][:, 0:1]
                Cs_ref[c, h] = C
                ns_ref[c, h] = n
                ms_ref[c, h] = m_sc[h]
                _, Sm, wi, _, gs, _ = _chunk_decay(_dot_nt(qh, kh), qh, bc, br, lir, n, m, causal)
                hh = (_dot(Sm, vh) + wi * _dot(qh, C)) / gs
                hn = hh * lax.rsqrt(jnp.mean(hh * hh, axis=1, keepdims=True) + EPS) * hng_ref[:, sv]
                hg_ref[rows, sv] = (hn * _sigmoid(o_ref[rows, sv])).astype(hg_ref.dtype)
                wk, decay, m_new = _state_weights(bc, lic, br, lir, m)
                kw = kh.astype(F32) * wk
                C_sc[h] = decay * C + _dot_tn(kw, vh)
                n_sc[h] = decay * n + jnp.sum(kw, axis=0, keepdims=True)
                m_sc[h] = jnp.broadcast_to(m_new, (1, 128))

    tok = lambda w, cb: pl.BlockSpec((A_TB, w), lambda i: (i, cb))
    return _call(
        body, "mlstm_fwd", (NC // A_GC,),
        [tok(512, 0), tok(512, 1), tok(1024, 1), tok(1024, 2),
         pl.BlockSpec((A_TB, 128), lambda i: (i, 0)),
         pl.BlockSpec((1, 8, 128), lambda i: (i, 0, 0)),
         pl.BlockSpec((1, 1024), lambda i: (0, 0))],
        [pl.BlockSpec((A_TB, 1024), lambda i: (i, 0)),
         pl.BlockSpec((A_GC, A_HEADS, A_QK, A_V), lambda i: (i, 0, 0, 0)),
         pl.BlockSpec((A_GC, A_HEADS, 1, 128), lambda i: (i, 0, 0, 0)),
         pl.BlockSpec((A_GC, A_HEADS, 1, 128), lambda i: (i, 0, 0, 0))],
        [jax.ShapeDtypeStruct((T, 1024), MM_DTYPE),
         jax.ShapeDtypeStruct((NC, A_HEADS, A_QK, A_V), F32),
         jax.ShapeDtypeStruct((NC, A_HEADS, 1, 128), F32),
         jax.ShapeDtypeStruct((NC, A_HEADS, 1, 128), F32)],
        [pltpu.VMEM((A_HEADS, A_QK, A_V), F32), pltpu.VMEM((A_HEADS, 1, 128), F32),
         pltpu.VMEM((A_HEADS, 1, 128), F32)],
        ("arbitrary",), (z, z, z, z, gcol, grow, hng), ride)


def mlstm_bwd(z, gcol, grow, hng, bias128, Cs, ns, ms, dhg, ride=None):
    T = z.shape[0]
    NC = T // A_CHUNK
    nsteps = NC // A_GC
    scale = A_QK ** -0.5

    def body(q_ref, k_ref, v_ref, o_ref, zg_ref, gc_ref, gr_ref, hng_ref, b_ref, Cs_ref, ns_ref, ms_ref,
             dhg_ref, dz_ref, dgn_ref, dbif_ref, dC_sc, dn_sc):
        @pl.when(pl.program_id(0) == 0)
        def _():
            dC_sc[...] = jnp.zeros_like(dC_sc)
            dn_sc[...] = jnp.zeros_like(dn_sc)
            dgn_ref[...] = jnp.zeros_like(dgn_ref)
            dbif_ref[...] = jnp.zeros_like(dbif_ref)

        ri = lax.broadcasted_iota(jnp.int32, (A_CHUNK, A_CHUNK), 0)
        ci = lax.broadcasted_iota(jnp.int32, (A_CHUNK, A_CHUNK), 1)
        causal = ri >= ci
        upper = (ci >= ri).astype(F32)
        rid = lax.broadcasted_iota(jnp.int32, (A_CHUNK, 1), 0)
        col = lax.broadcasted_iota(jnp.int32, (A_CHUNK, 128), 1)
        gr = gr_ref[0]
        for c in reversed(range(A_GC)):
            rows = slice(c * A_CHUNK, (c + 1) * A_CHUNK)
            gc = gc_ref[rows, :]
            grc = gr[:, c * A_CHUNK:(c + 1) * A_CHUNK]
            dG = jnp.zeros((A_CHUNK, 128), F32)
            hs = []
            for h in range(A_HEADS):
                sk, sv = _head_slices(h)
                s = dict(sk=sk, sv=sv, qh=(q_ref[rows, sk] * scale).astype(MM_DTYPE),
                         kh=k_ref[rows, sk].astype(MM_DTYPE), vh=v_ref[rows, sv].astype(MM_DTYPE),
                         lic=gc[:, h:h + 1], bc=gc[:, A_HEADS + h:A_HEADS + h + 1],
                         lir=grc[h:h + 1, :], br=grc[A_HEADS + h:A_HEADS + h + 1, :],
                         C=Cs_ref[c, h], n=ns_ref[c, h], m=ms_ref[c, h][:, 0:1], dC=dC_sc[h], dn=dn_sc[h])
                s['qf'], s['kf'] = s['qh'].astype(F32), s['kh'].astype(F32)
                s['wk'], s['decay'], _ = _state_weights(s['bc'], s['lic'], s['br'], s['lir'], s['m'])
                hs.append(s)
            for s in hs:
                s['A'] = _dot_nt(s['qh'], s['kh'])
                s['qC'] = _dot(s['qh'], s['C'])
                s['vdC'] = _dot_nt(s['vh'], s['dC'])
                s['kdC'] = _dot(s['kh'], s['dC'])
            for s in hs:
                s['E'], s['Sm'], s['wi'], s['den'], s['gs'], s['m_t'] = _chunk_decay(
                    s['A'], s['qh'], s['bc'], s['br'], s['lir'], s['n'], s['m'], causal)
            for s in hs:
                s['num'] = _dot(s['Sm'], s['vh']) + s['wi'] * s['qC']
            for h, s in enumerate(hs):
                sv, gs = s['sv'], s['gs']
                hh = s['num'] / gs
                r = lax.rsqrt(jnp.mean(hh * hh, axis=1, keepdims=True) + EPS)
                gn = hng_ref[:, sv]
                sg = _sigmoid(o_ref[rows, sv])
                dhg_h = dhg_ref[rows, sv]
                dhn = dhg_h * sg
                dz_ref[rows, 2048 + h * A_V:2048 + (h + 1) * A_V] = dhg_h * (hh * r * gn) * sg * (1.0 - sg)
                dgn_ref[:, sv] += jnp.sum(dhn * hh * r, axis=0, keepdims=True)
                dyg = dhn * gn
                dh = r * dyg - hh * (r * r * r) * jnp.mean(dyg * hh, axis=1, keepdims=True)
                s['dnum'] = dh / gs
                live = (jnp.abs(s['den']) > jnp.exp(-s['m_t'])).astype(F32)
                s['dden'] = -jnp.sum(dh * hh, axis=1, keepdims=True) / gs * jnp.sign(s['den']) * live
            for s in hs:
                s['dnv'] = _dot_nt(s['dnum'], s['vh'])
                s['dnC'] = _dot_nt(s['dnum'], s['C'])
            for s in hs:
                s['dSE'] = jnp.where(causal, s['dnv'] + s['dden'], 0.0) * s['E']
            for s in hs:
                s['dq'] = _dot(s['dSE'], s['kh']) + s['wi'] * (s['dnC'] + s['dden'] * s['n'])
                s['dk_inter'] = s['wk'] * (s['vdC'] + s['dn'])
                s['dk'] = _dot_tn(s['dSE'], s['qh']) + s['dk_inter']
                s['dv'] = _dot_tn(s['Sm'], s['dnum']) + s['wk'] * s['kdC']
                s['dCq'] = _dot_tn(s['qf'] * s['wi'], s['dnum'])
            for h, s in enumerate(hs):
                dq, dk, qf, kf, dC, dn = s['dq'], s['dk'], s['qf'], s['kf'], s['dC'], s['dn']
                dz_ref[rows, s['sk']] = dq * scale
                dz_ref[rows, 512 + h * A_QK:512 + (h + 1) * A_QK] = dk
                dz_ref[rows, 1024 + h * A_V:1024 + (h + 1) * A_V] = s['dv']
                dli = jnp.sum(kf * dk, axis=1, keepdims=True)
                db = jnp.sum(qf * dq, axis=1, keepdims=True) - dli
                usum = jnp.sum(jnp.sum(kf * s['dk_inter'], axis=1, keepdims=True), axis=0, keepdims=True)
                ddecay = (jnp.sum(jnp.sum(dC * s['C'], axis=1, keepdims=True), axis=0, keepdims=True)
                          + jnp.sum(dn * s['n'], axis=1, keepdims=True))
                db = db + jnp.where(rid == A_CHUNK - 1, usum + ddecay * s['decay'], 0.0)
                dG = dG + jnp.where(col == h, dli, 0.0) + jnp.where(col == A_HEADS + h, db, 0.0)
                dC_sc[h] = s['decay'] * dC + s['dCq']
                dn_sc[h] = s['decay'] * dn + jnp.sum(qf * (s['wi'] * s['dden']), axis=0, keepdims=True)
            dlf = jnp.dot(upper, dG, precision=HI, preferred_element_type=F32)
            pre = zg_ref[rows, :] + b_ref[...]
            th = jnp.tanh(pre / SOFTCAP)
            dcap = 1.0 - th * th
            dpre = jnp.where(col < A_HEADS, dG * dcap,
                             jnp.where(col < 2 * A_HEADS, dlf * _sigmoid(-SOFTCAP * th) * dcap, 0.0))
            dz_ref[rows, GATE_COL:GATE_COL + 128] = dpre
            dbif_ref[...] += jnp.sum(dpre, axis=0, keepdims=True)

    rev = lambda i: nsteps - 1 - i
    tok = lambda w, cb: pl.BlockSpec((A_TB, w), lambda i: (rev(i), cb))
    st = lambda a, b: pl.BlockSpec((A_GC, A_HEADS, a, b), lambda i: (rev(i), 0, 0, 0))
    return _call(
        body, "mlstm_bwd", (nsteps,),
        [tok(512, 0), tok(512, 1), tok(1024, 1), tok(1024, 2), tok(128, GATE_COL // 128),
         pl.BlockSpec((A_TB, 128), lambda i: (rev(i), 0)),
         pl.BlockSpec((1, 8, 128), lambda i: (rev(i), 0, 0)),
         pl.BlockSpec((1, 1024), lambda i: (0, 0)),
         pl.BlockSpec((1, 128), lambda i: (0, 0)),
         st(A_QK, A_V), st(1, 128), st(1, 128),
         pl.BlockSpec((A_TB, 1024), lambda i: (rev(i), 0))],
        [pl.BlockSpec((A_TB, A_IN_PAD), lambda i: (rev(i), 0)),
         pl.BlockSpec((1, 1024), lambda i: (0, 0)),
         pl.BlockSpec((1, 128), lambda i: (0, 0))],
        [jax.ShapeDtypeStruct((T, A_IN_PAD), F32), jax.ShapeDtypeStruct((1, 1024), F32),
         jax.ShapeDtypeStruct((1, 128), F32)],
        [pltpu.VMEM((A_HEADS, A_QK, A_V), F32), pltpu.VMEM((A_HEADS, 1, 128), F32)],
        ("arbitrary",), (z, z, z, z, z, gcol, grow, hng, bias128, Cs, ns, ms, dhg), ride)


def _t5_bucket(dist):
    max_exact = REL_BUCKETS // 2
    d = np.maximum(dist, 0)
    log_ratio = np.log(np.maximum(d, 1) / max_exact) / math.log(REL_MAX_DIST / max_exact)
    large = np.minimum(max_exact + (log_ratio * (REL_BUCKETS - max_exact)).astype(np.int64), REL_BUCKETS - 1)
    return np.where(d < max_exact, d, large).astype(np.int32)


def _group_bucket(g):
    delta = B_BLOCK + np.arange(B_BLOCK)[:, None] - np.arange(2 * B_BLOCK)[None, :]
    return _t5_bucket(delta * DILATIONS[g])


def _band_mask(n):
    ri = lax.broadcasted_iota(jnp.int32, (B_BLOCK, 2 * B_BLOCK), 0)
    ci = lax.broadcasted_iota(jnp.int32, (B_BLOCK, 2 * B_BLOCK), 1)
    band = jnp.logical_and(ci >= ri, ci <= ri + B_BLOCK)
    return jnp.logical_and(band, jnp.logical_or(ci >= B_BLOCK, n > 0))


def _both(p_ref, c_ref, sl):
    return jnp.concatenate([p_ref[:, sl], c_ref[:, sl]], axis=0)


def _scores(qh, kh, bias_h, valid):
    return jnp.where(valid, _dot_nt(qh, kh) * (B_DH ** -0.5) + bias_h, -jnp.inf)


def _attn_specs():
    wide = pl.BlockSpec((B_BLOCK, 1024), lambda r, n: (n, r))
    prev = pl.BlockSpec((B_BLOCK, 1024), lambda r, n: (jnp.maximum(n - 1, 0), r))
    narrow = pl.BlockSpec((B_BLOCK, 128), lambda r, n: (n, r))
    bias = pl.BlockSpec((B_HEADS, B_BLOCK, 2 * B_BLOCK), lambda r, n: (0, 0, 0))
    return wide, prev, narrow, bias


def _to_view(read_chunk, sc, o_ref, dil, nc, tt):
    for c in range(nc):
        sc[c] = read_chunk(c)
    for r in range(dil):
        for c in range(nc):
            lo = (r * nc + c) * 128
            o_ref[:, lo:lo + 128] = sc[c, pl.ds(r, tt // dil, stride=dil), :].astype(o_ref.dtype)


def _from_view(read_view, sc, dil, nc, tt):
    for r in range(dil):
        for c in range(nc):
            sc[c, pl.ds(r, tt // dil, stride=dil), :] = read_view((r * nc + c) * 128).astype(F32)


def attn_fwd(qv, kvw, vvw, bias, g):
    dil = DILATIONS[g]
    Tv = qv.shape[0]
    nb = Tv // B_BLOCK
    wide, prev, narrow, bsp = _attn_specs()

    def body(q_ref, kp_ref, kc_ref, vp_ref, vc_ref, b_ref, o_ref, lse_ref):
        valid = _band_mask(pl.program_id(1))
        lse_ref[...] = jnp.zeros_like(lse_ref)
        heads = [slice(h * B_DH, (h + 1) * B_DH) for h in range(B_HEADS)]
        S = [_scores(q_ref[:, sl], _both(kp_ref, kc_ref, sl), b_ref[h], valid) for h, sl in enumerate(heads)]
        P, L = [], []
        for h in range(B_HEADS):
            m = jnp.max(S[h], axis=1, keepdims=True)
            p = jnp.exp(S[h] - m)
            l = jnp.sum(p, axis=1, keepdims=True)
            lse_ref[:, h:h + 1] = m + jnp.log(l)
            P.append(p.astype(MM_DTYPE))
            L.append(l)
        for h, sl in enumerate(heads):
            o_ref[:, sl] = _dot(P[h], _both(vp_ref, vc_ref, sl)) / L[h]

    return pl.pallas_call(
        body, name=f"attn_fwd_g{g}", grid=(dil, nb),
        in_specs=[wide, prev, wide, prev, wide, bsp], out_specs=[wide, narrow],
        out_shape=[jax.ShapeDtypeStruct((Tv, dil * 1024), F32), jax.ShapeDtypeStruct((Tv, dil * 128), F32)],
        compiler_params=_params("parallel", "parallel"),
    )(qv, kvw, kvw, vvw, vvw, bias)


def attn_bwd(qv, kvw, vvw, bias, do_v, lse_v, dl_v, g):
    dil = DILATIONS[g]
    Tv = qv.shape[0]
    nb = Tv // B_BLOCK
    wide, prev, narrow, bsp = _attn_specs()

    def body(q_ref, kp_ref, kc_ref, vp_ref, vc_ref, b_ref, bt_ref, do_ref, lse_ref, dl_ref,
             dq_ref, dkc_ref, dkp_ref, dvc_ref, dvp_ref, db_ref):
        @pl.when(jnp.logical_and(pl.program_id(0) == 0, pl.program_id(1) == 0))
        def _():
            db_ref[...] = jnp.zeros_like(db_ref)

        n = pl.program_id(1)
        valid = _band_mask(n)
        ki = lax.broadcasted_iota(jnp.int32, (2 * B_BLOCK, B_BLOCK), 0)
        qi = lax.broadcasted_iota(jnp.int32, (2 * B_BLOCK, B_BLOCK), 1)
        valid_t = jnp.logical_and(jnp.logical_and(ki >= qi, ki <= qi + B_BLOCK), jnp.logical_or(ki >= B_BLOCK, n > 0))
        lse_t, dl_t = lse_ref[...].T, dl_ref[...].T
        heads = [slice(h * B_DH, (h + 1) * B_DH) for h in range(B_HEADS)]
        scale = B_DH ** -0.5
        PT, DS, DST = [], [], []
        for h, sl in enumerate(heads):
            qh, doh = q_ref[:, sl], do_ref[:, sl].astype(MM_DTYPE)
            kh, vh = _both(kp_ref, kc_ref, sl), _both(vp_ref, vc_ref, sl)
            p = jnp.exp(_scores(qh, kh, b_ref[h], valid) - lse_ref[:, h:h + 1])
            ds = p * (_dot_nt(doh, vh) - dl_ref[:, h:h + 1])
            db_ref[h] += ds
            DS.append((ds * scale).astype(MM_DTYPE))
            pt = jnp.exp(_scores(kh, qh, bt_ref[h], valid_t) - lse_t[h:h + 1, :])
            PT.append(pt.astype(MM_DTYPE))
            DST.append((pt * (_dot_nt(vh, doh) - dl_t[h:h + 1, :]) * scale).astype(MM_DTYPE))
        for h, sl in enumerate(heads):
            qh, doh = q_ref[:, sl], do_ref[:, sl].astype(MM_DTYPE)
            dq_ref[:, sl] = _dot(DS[h], _both(kp_ref, kc_ref, sl)).astype(MM_DTYPE)
            dk = _dot(DST[h], qh).astype(MM_DTYPE)
            dv = _dot(PT[h], doh).astype(MM_DTYPE)
            dkp_ref[:, sl], dkc_ref[:, sl] = dk[:B_BLOCK], dk[B_BLOCK:]
            dvp_ref[:, sl], dvc_ref[:, sl] = dv[:B_BLOCK], dv[B_BLOCK:]

    big = jax.ShapeDtypeStruct((Tv, dil * 1024), MM_DTYPE)
    bsp_t = pl.BlockSpec((B_HEADS, 2 * B_BLOCK, B_BLOCK), lambda r, n: (0, 0, 0))
    return pl.pallas_call(
        body, name=f"attn_bwd_g{g}", grid=(dil, nb),
        in_specs=[wide, prev, wide, prev, wide, bsp, bsp_t, wide, narrow, narrow],
        out_specs=[wide] * 5 + [bsp],
        out_shape=[big] * 5 + [jax.ShapeDtypeStruct((B_HEADS, B_BLOCK, 2 * B_BLOCK), F32)],
        compiler_params=_params("arbitrary", "arbitrary"),
    )(qv, kvw, kvw, vvw, vvw, bias, jnp.swapaxes(bias, 1, 2), do_v, lse_v, dl_v)


def _head_expand():
    e = np.zeros((128, 1024), np.float32)
    for h in range(B_HEADS):
        e[h, h * B_DH:(h + 1) * B_DH] = 1.0
    return e


A_TT = 256


def _view_spec(dil, width):
    return pl.BlockSpec((A_TT // dil, dil * width), lambda i: (i, 0))


def attn_merge(os_v, lses_v):
    T = os_v[0].shape[0]
    tt = A_TT
    expand = jnp.asarray(_head_expand())

    def body(o0, o1, o2, l0, l1, l2, e_ref, ob_ref, of_ref, lse0_ref, lse1_ref, lse2_ref, sc_o, sc_l):
        for gi, (o_ref, l_ref) in enumerate(((o1, l1), (o2, l2))):
            dil = DILATIONS[gi + 1]
            _from_view(lambda lo: o_ref[:, lo:lo + 128], sc_o.at[gi], dil, 8, tt)
            _from_view(lambda lo: l_ref[:, lo:lo + 128], sc_l.at[gi], dil, 1, tt)
        ls = [l0[...], sc_l[0, 0], sc_l[1, 0]]
        m = jnp.maximum(jnp.maximum(ls[0], ls[1]), ls[2])
        ex = [jnp.exp(l - m) for l in ls]
        tot = ex[0] + ex[1] + ex[2]
        lse = m + jnp.log(tot)
        lse0_ref[...] = lse
        _to_view(lambda c: lse, sc_l.at[2], lse1_ref, DILATIONS[1], 1, tt)
        _to_view(lambda c: lse, sc_l.at[2], lse2_ref, DILATIONS[2], 1, tt)
        ws = [e / tot for e in ex]
        for c in range(8):
            cols = slice(c * 128, (c + 1) * 128)
            ecol = e_ref[:, cols]
            spread = [jnp.dot(w, ecol, precision=HI, preferred_element_type=F32) for w in ws]
            out = spread[0] * o0[:, cols] + spread[1] * sc_o[0, c] + spread[2] * sc_o[1, c]
            of_ref[:, cols] = out
            ob_ref[:, cols] = out.astype(ob_ref.dtype)

    wide = pl.BlockSpec((tt, 1024), lambda i: (i, 0))
    return pl.pallas_call(
        body, name="attn_merge", grid=(T // tt,),
        in_specs=[_view_spec(d, 1024) for d in DILATIONS] + [_view_spec(d, 128) for d in DILATIONS]
        + [pl.BlockSpec((128, 1024), lambda i: (0, 0))],
        out_specs=[wide, wide] + [_view_spec(d, 128) for d in DILATIONS],
        out_shape=[jax.ShapeDtypeStruct((T, 1024), MM_DTYPE), jax.ShapeDtypeStruct((T, 1024), F32)]
        + [jax.ShapeDtypeStruct((T // d, d * 128), F32) for d in DILATIONS],
        scratch_shapes=[pltpu.VMEM((2, 8, tt, 128), F32), pltpu.VMEM((3, 1, tt, 128), F32)],
        compiler_params=_params("parallel"),
    )(*os_v, *lses_v, expand)


def attn_prep(datt, out):
    T = datt.shape[0]
    tt = A_TT
    expand_t = jnp.asarray(_head_expand().T.copy())

    def body(d_ref, o_ref, e_ref, do0, do1, do2, dl0, dl1, dl2, sc_d, sc_l):
        delta = jnp.dot(d_ref[...] * o_ref[...], e_ref[...], precision=HI, preferred_element_type=F32)
        do0[...] = d_ref[...].astype(do0.dtype)
        dl0[...] = delta
        for do_ref, dl_ref, dil in ((do1, dl1, DILATIONS[1]), (do2, dl2, DILATIONS[2])):
            _to_view(lambda c: d_ref[:, c * 128:(c + 1) * 128], sc_d, do_ref, dil, 8, tt)
            _to_view(lambda c: delta, sc_l, dl_ref, dil, 1, tt)

    wide = pl.BlockSpec((tt, 1024), lambda i: (i, 0))
    return pl.pallas_call(
        body, name="attn_prep", grid=(T // tt,),
        in_specs=[wide, wide, pl.BlockSpec((1024, 128), lambda i: (0, 0))],
        out_specs=[_view_spec(d, 1024) for d in DILATIONS] + [_view_spec(d, 128) for d in DILATIONS],
        out_shape=[jax.ShapeDtypeStruct((T // d, d * 1024), MM_DTYPE) for d in DILATIONS]
        + [jax.ShapeDtypeStruct((T // d, d * 128), F32) for d in DILATIONS],
        scratch_shapes=[pltpu.VMEM((8, tt, 128), F32), pltpu.VMEM((1, tt, 128), F32)],
        compiler_params=_params("parallel"),
    )(datt, out, expand_t)


def attn_combine(parts):
    T = parts[0][0].shape[0]
    tt = A_TT
    nt = T // tt
    shift = [None] + [B_BLOCK * d // tt for d in DILATIONS[1:]]

    def body(dq0, kc0, vc0, kpa0, kpb0, vpa0, vpb0, dq1, kc1, kp1, vc1, vp1, dq2, kc2, kp2, vc2, vp2,
             dq_ref, dkv_ref, sc):
        i = pl.program_id(0)
        dq_ref[:, 0:1024] = dq0[...].astype(dq_ref.dtype)
        for col, c_ref, pa_ref, pb_ref in ((0, kc0, kpa0, kpb0), (3, vc0, vpa0, vpb0)):
            nxt = jnp.where(i + 1 < nt, pb_ref[:tt // 2, :].astype(F32), 0.0)
            later = jnp.concatenate([pa_ref[tt // 2:, :].astype(F32), nxt], axis=0)
            dkv_ref[:, col * 1024:(col + 1) * 1024] = (c_ref[...].astype(F32) + later).astype(dkv_ref.dtype)
        for g, (dq, kc, kp, vc, vp) in ((1, (dq1, kc1, kp1, vc1, vp1)), (2, (dq2, kc2, kp2, vc2, vp2))):
            dil = DILATIONS[g]
            live = i + shift[g] < nt
            _from_view(lambda lo: dq[:, lo:lo + 128], sc, dil, 8, tt)
            for c in range(8):
                dq_ref[:, g * 1024 + c * 128:g * 1024 + (c + 1) * 128] = sc[c].astype(dq_ref.dtype)
            for col, c_ref, p_ref in ((g, kc, kp), (3 + g, vc, vp)):
                _from_view(lambda lo: c_ref[:, lo:lo + 128].astype(F32)
                           + jnp.where(live, p_ref[:, lo:lo + 128].astype(F32), 0.0), sc, dil, 8, tt)
                for c in range(8):
                    dkv_ref[:, col * 1024 + c * 128:col * 1024 + (c + 1) * 128] = sc[c].astype(dkv_ref.dtype)

    def later_spec(dil, blocks):
        return pl.BlockSpec((tt // dil, dil * 1024), lambda i: (jnp.minimum(i + blocks, nt - 1), 0))

    cur = [_view_spec(d, 1024) for d in DILATIONS]
    in_specs = [cur[0], cur[0], cur[0], cur[0], later_spec(1, 1), cur[0], later_spec(1, 1)]
    args = [parts[0][0], parts[0][1], parts[0][3], parts[0][2], parts[0][2], parts[0][4], parts[0][4]]
    for g in (1, 2):
        in_specs += [cur[g], cur[g], later_spec(DILATIONS[g], shift[g]), cur[g], later_spec(DILATIONS[g], shift[g])]
        args += list(parts[g][:5])
    return pl.pallas_call(
        body, name="attn_combine", grid=(nt,), in_specs=in_specs,
        out_specs=[pl.BlockSpec((tt, 3072), lambda i: (i, 0)), pl.BlockSpec((tt, 6144), lambda i: (i, 0))],
        out_shape=[jax.ShapeDtypeStruct((T, 3072), MM_DTYPE), jax.ShapeDtypeStruct((T, 6144), MM_DTYPE)],
        scratch_shapes=[pltpu.VMEM((8, tt, 128), F32)],
        compiler_params=_params("parallel"),
    )(*args)


def adamw(w, g, m, v, name):
    R, C = w.shape
    tr = R if R * C * 4 <= (1 << 20) else _rows(R, max(8, ((1 << 20) // (C * 4)) // 8 * 8))

    def body(w_ref, g_ref, m_ref, v_ref, d_ref, nm_ref, nv_ref):
        gg = g_ref[...]
        nm = ADAM_B1 * m_ref[...] + (1.0 - ADAM_B1) * gg
        nv = ADAM_B2 * v_ref[...] + (1.0 - ADAM_B2) * (gg * gg)
        m_hat = nm / (1.0 - ADAM_B1 ** ADAM_STEP)
        v_hat = nv / (1.0 - ADAM_B2 ** ADAM_STEP)
        d_ref[...] = -ADAM_LR * (m_hat / (jnp.sqrt(v_hat) + ADAM_EPS) + ADAM_WD * w_ref[...])
        nm_ref[...] = nm
        nv_ref[...] = nv

    blk = pl.BlockSpec((tr, C), lambda i: (i, 0))
    sds = jax.ShapeDtypeStruct((R, C), F32)
    return pl.pallas_call(
        body, name=name, grid=(R // tr,), in_specs=[blk] * 4, out_specs=[blk] * 3, out_shape=[sds] * 3,
        compiler_params=_params("parallel"),
    )(w, g, m, v)


def sum_slots(x, name, out_dtype=F32):
    n, R, C = x.shape
    tr = _rows(R, 256)

    def body(x_ref, o_ref):
        acc = x_ref[0].astype(F32)
        for s in range(1, n):
            acc = acc + x_ref[s].astype(F32)
        o_ref[...] = acc.astype(out_dtype)

    return pl.pallas_call(
        body, name=name, grid=(R // tr,),
        in_specs=[pl.BlockSpec((n, tr, C), lambda i: (0, i, 0))],
        out_specs=pl.BlockSpec((tr, C), lambda i: (i, 0)),
        out_shape=jax.ShapeDtypeStruct((R, C), out_dtype),
        compiler_params=_params("parallel"),
    )(x)


_ANY = pl.BlockSpec(memory_space=pl.ANY)
GROUP_ALL = ([(0, 0, 1), (0, 1, 0), (0, 1, 1), (1, 0, 0), (1, 0, 1), (1, 1, 0), (1, 1, 1)],
             lambda d: 4 * d[0] + 2 * d[1] + d[2])
GROUP_CHIPS = ([(0, 1, 0), (1, 0, 0), (1, 1, 0)], lambda d: 2 * d[0] + d[1])
GROUP_SIBLING = ([(0, 0, 1)], lambda d: d[2])


def _me():
    return lax.axis_index("x"), lax.axis_index("y"), lax.axis_index("c")


def _peer(me, flip):
    return tuple(1 - a if f else a for a, f in zip(me, flip))


class Exchange:
    def __init__(self, x, group, scatter):
        self.flips, self.slot = group
        self.scatter = scatter
        self.n = len(self.flips) + 1
        self.out_shape = jax.ShapeDtypeStruct((self.n,) + x.shape[-2:], x.dtype)
        self.scratch = [pltpu.SemaphoreType.DMA((self.n - 1,)), pltpu.SemaphoreType.DMA((self.n - 1,)),
                        pltpu.SemaphoreType.DMA]

    def _copies(self, x_ref, o_ref, send_sems, recv_sems, local_sem, arrivals):
        me = _me()
        slot = self.slot
        mine = pltpu.make_async_copy(x_ref.at[slot(me)] if self.scatter else x_ref, o_ref.at[slot(me)], local_sem)
        sends, landed = [], []
        for k, flip in enumerate(self.flips):
            peer = _peer(me, flip)
            sends.append(pltpu.make_async_remote_copy(
                src_ref=x_ref.at[slot(peer)] if self.scatter else x_ref, dst_ref=o_ref.at[slot(me)],
                send_sem=send_sems.at[k], recv_sem=recv_sems.at[k], device_id=peer, device_id_type=MESH_ID))
            if arrivals:
                landed.append(pltpu.make_async_remote_copy(
                    src_ref=o_ref.at[slot(me)], dst_ref=o_ref.at[slot(peer)], send_sem=send_sems.at[k],
                    recv_sem=recv_sems.at[k], device_id=peer, device_id_type=MESH_ID))
        return mine, sends, landed

    def start(self, *refs):
        mine, sends, _ = self._copies(*refs, arrivals=False)
        mine.start()
        for cp in sends:
            cp.start()

    def wait(self, *refs):
        mine, sends, arrivals = self._copies(*refs, arrivals=True)
        for cp in arrivals:
            cp.wait_recv()
        for cp in sends:
            cp.wait_send()
        mine.wait()

    def __call__(self, x, name):
        def body(*refs):
            self.start(*refs)
            self.wait(*refs)

        return pl.pallas_call(body, name=name, in_specs=[_ANY], out_specs=_ANY, out_shape=self.out_shape,
                              scratch_shapes=self.scratch)(x)


def group_gather(x, name, group):
    return Exchange(x, group, scatter=False)(x, name)


def group_scatter(x, name, group):
    return Exchange(x, group, scatter=True)(x, name)


def _call(body, name, grid, in_specs, out_specs, out_shape, scratch, semantics, args, ride=None):
    if ride is None:
        return pl.pallas_call(body, name=name, grid=grid, in_specs=in_specs, out_specs=out_specs,
                              out_shape=out_shape, scratch_shapes=scratch,
                              compiler_params=_params(*semantics))(*args)
    x, exch = ride
    n_in, n_out, n_scr = len(in_specs), len(out_specs), len(scratch)

    def at_step(pick):
        hit = None
        for axis, size in enumerate(grid):
            here = pl.program_id(axis) == pick(size)
            hit = here if hit is None else jnp.logical_and(hit, here)
        return hit

    def riding(*refs):
        ins, x_ref = refs[:n_in], refs[n_in]
        outs, o_ref = refs[n_in + 1:n_in + 1 + n_out], refs[n_in + 1 + n_out]
        scr, sems = refs[n_in + 2 + n_out:n_in + 2 + n_out + n_scr], refs[n_in + 2 + n_out + n_scr:]

        @pl.when(at_step(lambda size: 0))
        def _():
            exch.start(x_ref, o_ref, *sems)

        body(*ins, *outs, *scr)

        @pl.when(at_step(lambda size: size - 1))
        def _():
            exch.wait(x_ref, o_ref, *sems)

    return pl.pallas_call(
        riding, name=name, grid=grid, in_specs=list(in_specs) + [_ANY], out_specs=list(out_specs) + [_ANY],
        out_shape=list(out_shape) + [exch.out_shape], scratch_shapes=list(scratch) + exch.scratch,
        compiler_params=_params(*(["arbitrary"] * len(grid))))(*args, x)


WEIGHTS = ['a_norm_g', 'a_w_in', 'a_b_if', 'a_hnorm_g', 'a_w_out', 'kv_norm_g', 'w_kv', 'b_norm_g', 'b_w_q',
           'b_w_out', 'rel_bias', 'f_norm_g', 'f_w_up', 'f_conv_w', 'f_conv_b', 'f_w_down', 'final_norm_g']
SHARD_AXIS = {'a_norm_g': 1, 'a_w_in': 2, 'a_b_if': None, 'a_hnorm_g': 2, 'a_w_out': 1, 'kv_norm_g': None,
              'w_kv': 1, 'b_norm_g': None, 'b_w_q': 2, 'b_w_out': 1, 'rel_bias': None, 'f_norm_g': None,
              'f_w_up': 2, 'f_conv_w': 2, 'f_conv_b': None, 'f_w_down': 1, 'final_norm_g': None}
BIG = ['a_w_in', 'a_w_out', 'w_kv', 'b_w_q', 'b_w_out', 'f_w_up', 'f_w_down']
SMALL = [n for n in WEIGHTS if n not in BIG]
LANES = 1024
PIECES = {'a_w_in': ('a_w_in', None, 2), 'a_w_out': ('a_w_out', None, 1), 'f_w_up0': ('f_w_up', 0, 1),
          'f_w_down0': ('f_w_down', 0, 0), 'w_kv': ('w_kv', None, 1), 'b_w_q': ('b_w_q', None, 2),
          'b_w_out': ('b_w_out', None, 1), 'f_w_up1': ('f_w_up', 1, 1), 'f_w_down1': ('f_w_down', 1, 0)}
LATE = ['w_kv', 'b_w_q', 'b_w_out', 'f_w_up1', 'f_w_down1']
WEIGHT_WAVES = {'first': ['a_w_in', 'a_w_out'], 'ffn0': ['f_w_up0', 'f_w_down0'], 'late': LATE}
GRAD_WAVES = {'late': LATE, 'layer0': ['f_w_up0', 'f_w_down0', 'a_w_out'], 'last': ['a_w_in']}


def _piece(arrays, p):
    leaf, layer, _ = PIECES[p]
    return arrays[leaf] if layer is None else arrays[leaf][layer]


class Packer:
    def __init__(self, pieces, shard):
        self.pieces = pieces
        self.shapes = [_piece(shard, p).shape for p in pieces]
        self.sizes = [math.prod(s) // (2 * LANES) for s in self.shapes]
        self.fill = -sum(self.sizes) % 16
        self.rows = sum(self.sizes) + self.fill

    def my_half(self, shard, half):
        both = jnp.concatenate([_piece(shard, p).astype(MM_DTYPE).reshape(2, -1, LANES) for p in self.pieces], axis=1)
        return jnp.pad(lax.dynamic_index_in_dim(both, half, axis=0, keepdims=False), ((0, self.fill), (0, 0)))

    def full_weights(self, gathered):
        g = gathered.reshape(4, 2, self.rows, LANES)
        out, off = {}, 0
        for p, shp, sz in zip(self.pieces, self.shapes, self.sizes):
            out[p] = _full_from_shards(g[:, :, off:off + sz].reshape((4,) + shp), PIECES[p][2])
            off += sz
        return out

    def grad_slots(self, grads):
        parts = [_shards_from_full(grads[p], PIECES[p][2]).reshape(4, 2, -1, LANES).astype(GRAD_WIRE_DTYPE)
                 for p in self.pieces]
        parts.append(jnp.zeros((4, 2, self.fill, LANES), GRAD_WIRE_DTYPE))
        return jnp.concatenate(parts, axis=2).reshape(8, self.rows, LANES)

    def shard_grads(self, both):
        out, off = {}, 0
        for p, shp, sz in zip(self.pieces, self.shapes, self.sizes):
            out[p] = both[:, off:off + sz].reshape(shp).astype(F32)
            off += sz
        return out


class Overlap:
    def __init__(self, shard, half):
        self.shard, self.half = shard, half
        self.weights = {w: Packer(p, shard) for w, p in WEIGHT_WAVES.items()}
        self.grads = {w: Packer(p, shard) for w, p in GRAD_WAVES.items()}
        self.shard_grads = {}

    def gather_ride(self, wave):
        mine = self.weights[wave].my_half(self.shard, self.half)
        return mine, Exchange(mine, GROUP_ALL, scatter=False)

    def gathered(self, wave, slots):
        return self.weights[wave].full_weights(slots)

    def scatter_ride(self, wave, grads):
        slots = self.grads[wave].grad_slots({p: grads.pop(p) for p in GRAD_WAVES[wave]})
        return slots, Exchange(slots, GROUP_ALL, scatter=True)

    def join_ride(self, wave, received):
        reduced = sum_slots(received, f"sum_grads_{wave}", GRAD_WIRE_DTYPE)
        return reduced, Exchange(reduced, GROUP_SIBLING, scatter=False)

    def joined(self, wave, both):
        self.shard_grads.update(self.grads[wave].shard_grads(both))


def _pad_rows(flat, mult):
    n = flat.shape[0]
    per = LANES * mult
    tot = -(-n // per) * per
    return jnp.pad(flat, (0, tot - n)).reshape(tot // LANES, LANES)


def _full_from_shards(sh, axis):
    return jnp.concatenate([sh[j] for j in range(4)], axis=axis)


def _shards_from_full(full, axis):
    return jnp.stack(jnp.split(full, 4, axis=axis))


def _local_step(x, target, W, overlap=None):
    T = x.shape[0]
    W = dict(W)
    row = lambda a: a.reshape(1, -1).astype(F32)
    w_in = jnp.pad(W['a_w_in'][0], ((0, 0), (0, A_IN_PAD - A_IN)))
    bias128 = jnp.pad(row(W['a_b_if'][0]), ((0, 0), (0, 120)))
    hng = row(W['a_hnorm_g'][0])
    w_up = lambda l: _interleave(W[f'f_w_up{l}'])
    cw = [_interleave(W['f_conv_w'][l].astype(F32)) for l in range(2)]
    cb = [_interleave(row(W['f_conv_b'][l])) for l in range(2)]
    onehots = [(jnp.asarray(_group_bucket(g).reshape(-1, 1)) == jnp.arange(128)[None, :]).astype(F32)
               for g in range(N_GROUPS)]
    rb_t = jnp.pad(W['rel_bias'].astype(F32).T, ((0, 0), (0, 128 - REL_BUCKETS)))
    biases = [mm_nn(rb_t[g * B_HEADS:(g + 1) * B_HEADS], onehots[g].T, f"rel_bias_table_g{g}", exact=True)
              .reshape(B_HEADS, B_BLOCK, 2 * B_BLOCK) for g in range(N_GROUPS)]
    G = {}

    def ffn_fwd(xin, l, ride=None):
        xn, = rms_fwd(xin, [row(W['f_norm_g'][l])], f"ffn{l}_norm")
        u, c, act, *rode = ffn_up_act(xn, w_up(l), cw[l], cb[l], f"ffn{l}_up_act", ride)
        return mm_nn(act, W[f'f_w_down{l}'], f"ffn{l}_down", res=xin), (xn, u, c, act), rode

    def ffn_bwd(xin, saved, dout, l, ride=None):
        xn, u, c, act = saved
        dact = mm_nn(dout, W[f'f_w_down{l}'].T, f"ffn{l}_ddown")
        G[f'f_w_down{l}'] = mm_tn(act, dout, f"ffn{l}_gdown")
        du, gcw, gcb, *rode = conv_act_bwd(u, c, dact, cw[l], f"ffn{l}_dact", ride)
        dxn = mm_nn(du, w_up(l).T, f"ffn{l}_dup")
        G[f'f_w_up{l}'] = _deinterleave(mm_tn(xn, du, f"ffn{l}_gup"))
        dxin, (gn,) = rms_bwd(xin, dout, [(dxn, row(W['f_norm_g'][l]))], f"ffn{l}_dnorm")
        return dxin, _deinterleave(gcw), _deinterleave(gcb), gn, rode

    xn_a, = rms_fwd(x, [row(W['a_norm_g'][0])], "a_norm")
    z = mm_nn(xn_a, w_in, "a_in")
    gcol, grow = gate_prep(z, bias128)
    hg, Cs, ns, ms, *rode = mlstm_fwd(z, gcol, grow, hng, overlap.gather_ride('ffn0') if overlap else None)
    if overlap:
        W.update(overlap.gathered('ffn0', rode[0]))
    x1 = mm_nn(hg, W['a_w_out'][0], "a_out", res=x)
    x2, ffn0, rode = ffn_fwd(x1, 0, overlap.gather_ride('late') if overlap else None)
    if overlap:
        W.update(overlap.gathered('late', rode[0]))
    xn_kv, xn_b = rms_fwd(x2, [row(W['kv_norm_g']), row(W['b_norm_g'][0])], "b_norms")
    gcols = lambda w, c: w[:, c * 1024:(c + 1) * 1024]
    qv = [mm_view(xn_b, gcols(W['b_w_q'][0], g), f"q_proj_g{g}", DILATIONS[g]) for g in range(N_GROUPS)]
    kvw = [mm_view(xn_kv, gcols(W['w_kv'], g), f"k_proj_g{g}", DILATIONS[g]) for g in range(N_GROUPS)]
    vvw = [mm_view(xn_kv, gcols(W['w_kv'], 3 + g), f"v_proj_g{g}", DILATIONS[g]) for g in range(N_GROUPS)]
    os_, lses = zip(*[attn_fwd(qv[g], kvw[g], vvw[g], biases[g], g) for g in range(N_GROUPS)])
    att, att_f, *lse_v = attn_merge(os_, lses)
    x3 = mm_nn(att, W['b_w_out'][0], "b_out", res=x2)
    x4, ffn1, _ = ffn_fwd(x3, 1)
    dx4, g_final, loss = loss_head(x4, target, row(W['final_norm_g']))
    G['final_norm_g'] = g_final.reshape(-1)

    dx3, gcw1, gcb1, gn1, _ = ffn_bwd(x3, ffn1, dx4, 1)
    datt = mm_nn(dx3, W['b_w_out'][0].T, "b_dout")
    G['b_w_out'] = mm_tn(att, dx3, "b_gout")[None]
    prep = attn_prep(datt, att_f)
    do_v, dl_v = prep[:3], prep[3:]
    parts = [attn_bwd(qv[g], kvw[g], vvw[g], biases[g], do_v[g], lse_v[g], dl_v[g], g) for g in range(N_GROUPS)]
    dq_all, dkv = attn_combine(parts)
    grb = []
    for g in range(N_GROUPS):
        gb = mm_nn(parts[g][5].reshape(B_HEADS, -1), onehots[g], f"rel_bias_g{g}", exact=True)
        grb.append(gb[:, :REL_BUCKETS].T)
    G['rel_bias'] = jnp.concatenate(grb, axis=1)
    dxn_b = mm_nn(dq_all, W['b_w_q'][0].T, "q_dproj")
    G['b_w_q'] = mm_tn(xn_b, dq_all, "q_gproj")[None]
    dxn_kv = mm_nn(dkv, W['w_kv'].T, "kv_dproj")
    G['w_kv'] = mm_tn(xn_kv, dkv, "kv_gproj")
    dx2, (g_kvn, g_bn) = rms_bwd(x2, dx3, [(dxn_kv, row(W['kv_norm_g'])), (dxn_b, row(W['b_norm_g'][0]))],
                                 "b_dnorms")
    G['kv_norm_g'] = g_kvn.reshape(-1)
    G['b_norm_g'] = g_bn
    dx1, gcw0, gcb0, gn0, late_slots = ffn_bwd(x1, ffn0, dx2, 0, overlap.scatter_ride('late', G) if overlap else None)
    G['f_conv_w'] = jnp.stack([gcw0, gcw1])
    G['f_conv_b'] = jnp.concatenate([gcb0, gcb1], axis=0)
    G['f_norm_g'] = jnp.concatenate([gn0, gn1], axis=0)
    dhg = mm_nn(dx1, W['a_w_out'][0].T, "a_dout")
    G['a_w_out'] = mm_tn(hg, dx1, "a_gout")[None]
    dz, g_hn, g_bif, *layer0_slots = mlstm_bwd(z, gcol, grow, hng, bias128, Cs, ns, ms, dhg,
                                               overlap.scatter_ride('layer0', G) if overlap else None)
    G['a_hnorm_g'] = g_hn.reshape(1, A_HEADS, A_V)
    G['a_b_if'] = g_bif[:, :2 * A_HEADS]
    if overlap:
        dxn_a, both = mm_nn(dz, w_in.T, "a_din", ride=overlap.join_ride('late', late_slots[0]))
        overlap.joined('late', both)
        g_in, both = mm_tn(xn_a, dz, "a_gin", ride=overlap.join_ride('layer0', layer0_slots[0]))
        overlap.joined('layer0', both)
    else:
        dxn_a = mm_nn(dz, w_in.T, "a_din")
        g_in = mm_tn(xn_a, dz, "a_gin")
    G['a_w_in'] = g_in[:, :A_IN][None]
    grad_x, (g_an,) = rms_bwd(x, dx1, [(dxn_a, row(W['a_norm_g'][0]))], "a_dnorm")
    G['a_norm_g'] = g_an
    return loss, grad_x, G


def kernel(x, a_norm_g, a_w_in, a_b_if, a_hnorm_g, a_w_out, kv_norm_g, w_kv, b_norm_g, b_w_q, b_w_out, rel_bias, f_norm_g, f_w_up, f_conv_w, f_conv_b, f_w_down, final_norm_g, loss_target, m_a_norm_g, m_a_w_in, m_a_b_if, m_a_hnorm_g, m_a_w_out, m_kv_norm_g, m_w_kv, m_b_norm_g, m_b_w_q, m_b_w_out, m_rel_bias, m_f_norm_g, m_f_w_up, m_f_conv_w, m_f_conv_b, m_f_w_down, m_final_norm_g, v_a_norm_g, v_a_w_in, v_a_b_if, v_a_hnorm_g, v_a_w_out, v_kv_norm_g, v_w_kv, v_b_norm_g, v_b_w_q, v_b_w_out, v_rel_bias, v_f_norm_g, v_f_w_up, v_f_conv_w, v_f_conv_b, v_f_w_down, v_final_norm_g):
    given = dict(locals())
    shard = {n: given[n] for n in WEIGHTS}
    mom = {n: given["m_" + n] for n in WEIGHTS}
    var = {n: given["v_" + n] for n in WEIGHTS}
    cx, cy, cc = _me()
    chip = 2 * cx + cy

    overlap = Overlap(shard, cc)
    mine, gather = overlap.gather_ride('first')
    W = overlap.gathered('first', gather(mine, "gather_weights"))
    sharded_small = [n for n in SMALL if SHARD_AXIS[n] is not None]
    ssz = [shard[n].size for n in sharded_small]
    sflat = jnp.concatenate([shard[n].reshape(-1) for n in sharded_small])
    sg = group_gather(_pad_rows(sflat, 8), "gather_small", GROUP_CHIPS).reshape(4, -1)
    off = 0
    for n, sz in zip(sharded_small, ssz):
        W[n] = _full_from_shards(sg[:, off:off + sz].reshape((4,) + shard[n].shape), SHARD_AXIS[n])
        off += sz
    for n in SMALL:
        if SHARD_AXIS[n] is None:
            W[n] = shard[n]

    loss_row, grad_x, G = _local_step(x[0], loss_target[0], W, overlap)

    slots, scatter = overlap.scatter_ride('last', G)
    reduced, join = overlap.join_ride('last', scatter(slots, "scatter_grads"))
    overlap.joined('last', join(reduced, "join_halves"))
    by_piece = overlap.shard_grads
    gsh = {}
    for n in BIG:
        layers = [p for p in PIECES if PIECES[p][0] == n]
        gsh[n] = by_piece[n] if layers == [n] else jnp.stack([by_piece[p] for p in layers])
    small_parts = [loss_row[0, 0:1]] + [G[n].reshape(-1) for n in SMALL]
    small_sz = [p.shape[0] for p in small_parts]
    small = sum_slots(group_gather(_pad_rows(jnp.concatenate(small_parts), 8), "gather_small_grads", GROUP_ALL),
                      "sum_small_grads").reshape(-1)
    loss = small[0]
    off = 1
    for n, sz in zip(SMALL, small_sz[1:]):
        full = small[off:off + sz].reshape(W[n].shape)
        off += sz
        if SHARD_AXIS[n] is None:
            gsh[n] = full
        else:
            gsh[n] = lax.dynamic_index_in_dim(_shards_from_full(full, SHARD_AXIS[n]), chip, 0, keepdims=False)

    delta, new_m, new_v = {}, {}, {}
    for n in WEIGHTS:
        shp = shard[n].shape
        two = lambda a: a.reshape(-1, shp[-1])
        d, nm, nv = adamw(two(shard[n]), two(gsh[n]), two(mom[n]), two(var[n]), f"adamw_{n}")
        delta[n], new_m[n], new_v[n] = d.reshape(shp), nm.reshape(shp), nv.reshape(shp)
    return (loss, grad_x[None], *[gsh[n] for n in WEIGHTS], *[delta[n] for n in WEIGHTS],
            *[new_m[n] for n in WEIGHTS], *[new_v[n] for n in WEIGHTS])
```

```python
import functools
import math

import numpy as np
import jax
import jax.numpy as jnp
from jax import lax
from jax.experimental import pallas as pl
from jax.experimental.pallas import tpu as pltpu

F32 = jnp.float32
BF16 = jnp.bfloat16
MM_DTYPE = jnp.bfloat16
GRAD_WIRE_DTYPE = jnp.bfloat16
HI = lax.Precision.HIGHEST

D_MODEL = 1024
A_HEADS = 4
A_QK = 128
A_V = 256
A_CHUNK = 64
A_IN = 3080
A_IN_PAD = 3200
GATE_COL = 3072
SOFTCAP = 15.0
N_GROUPS = 3
B_HEADS = 16
B_DH = 64
B_BLOCK = 128
DILATIONS = (1, 4, 16)
WINDOWS = (128, 512, 2048)
REL_BUCKETS = 32
REL_MAX_DIST = 2048
D_FF = 2816
FF_TC = 256
EPS = 1e-6
ADAM_LR, ADAM_B1, ADAM_B2, ADAM_EPS, ADAM_WD, ADAM_STEP = 0.001, 0.9, 0.999, 1e-08, 0.01, 10

VMEM_LIMIT = 56 * 1024 * 1024
NT_DIMS = (((1,), (1,)), ((), ()))
TN_DIMS = (((0,), (0,)), ((), ()))
MESH_ID = pl.DeviceIdType.MESH


def _params(*sem):
    return pltpu.CompilerParams(dimension_semantics=sem, vmem_limit_bytes=VMEM_LIMIT)


def _tile(n, cap):
    if n <= cap:
        return n
    best = None
    for t in range(128, cap + 1, 128):
        if n % t == 0:
            best = t
    assert best is not None, (n, cap)
    return best


def _rows(n, cap):
    if n <= cap:
        return n
    for t in range(cap // 8 * 8, 7, -8):
        if n % t == 0:
            return t
    raise ValueError((n, cap))


def _dot(a, b):
    return jnp.dot(a.astype(MM_DTYPE), b.astype(MM_DTYPE), preferred_element_type=F32)


def _dot_nt(a, b):
    return lax.dot_general(a.astype(MM_DTYPE), b.astype(MM_DTYPE), NT_DIMS, preferred_element_type=F32)


def _dot_tn(a, b):
    return lax.dot_general(a.astype(MM_DTYPE), b.astype(MM_DTYPE), TN_DIMS, preferred_element_type=F32)


def _sigmoid(x):
    return 1.0 / (1.0 + jnp.exp(-x))


def _sigmoid_tanh(x):
    return 0.5 * jnp.tanh(0.5 * x) + 0.5


def mm_nn(a, b, name, res=None, out_dtype=F32, exact=False, ride=None):
    M, K = a.shape
    N = b.shape[1]
    tm = _rows(M, 512)

    def footprint(tn):
        return 2 * (tm * K * a.dtype.itemsize + K * tn * b.dtype.itemsize) + 2 * tm * tn * 4 * (1 if res is None else 2)

    budget = 46 * 1024 * 1024
    tn = N if N <= 3328 and footprint(N) <= budget else _tile(N, 1536)
    tk = K if footprint(tn) <= budget else _tile(K, 1536)
    nk = K // tk

    def body(*refs):
        if res is None:
            a_ref, b_ref, o_ref, acc = refs
            r_ref = None
        else:
            a_ref, b_ref, r_ref, o_ref, acc = refs
        if exact:
            p = jnp.dot(a_ref[...], b_ref[...], precision=HI, preferred_element_type=F32)
        else:
            p = _dot(a_ref[...], b_ref[...])

        def finish(total):
            if r_ref is not None:
                total = total + r_ref[...]
            o_ref[...] = total.astype(out_dtype)

        if nk == 1:
            finish(p)
        else:
            k = pl.program_id(2)

            @pl.when(k == 0)
            def _():
                acc[...] = p

            @pl.when(jnp.logical_and(k > 0, k < nk - 1))
            def _():
                acc[...] += p

            @pl.when(k == nk - 1)
            def _():
                finish(acc[...] + p)

    in_specs = [pl.BlockSpec((tm, tk), lambda j, i, k: (i, k)),
                pl.BlockSpec((tk, tn), lambda j, i, k: (k, j))]
    args = [a, b]
    if res is not None:
        in_specs.append(pl.BlockSpec((tm, tn), lambda j, i, k: (i, j)))
        args.append(res)
    acc_shape = (tm, tn) if nk > 1 else (8, 128)
    outs = _call(body, name, (N // tn, M // tm, nk), in_specs, [pl.BlockSpec((tm, tn), lambda j, i, k: (i, j))],
                 [jax.ShapeDtypeStruct((M, N), out_dtype)], [pltpu.VMEM(acc_shape, F32)],
                 ("parallel", "parallel", "arbitrary"), args, ride)
    return outs[0] if ride is None else outs


def mm_view(a, b, name, dil):
    T, K = a.shape
    tm = 512

    def body(a_ref, b_ref, o_ref, sc):
        p = _dot(a_ref[...], b_ref[...])
        if dil == 1:
            o_ref[...] = p.astype(o_ref.dtype)
        else:
            _to_view(lambda c: p[:, c * 128:(c + 1) * 128], sc, o_ref, dil, 8, tm)

    return pl.pallas_call(
        body, name=name, grid=(T // tm,),
        in_specs=[pl.BlockSpec((tm, K), lambda i: (i, 0)), pl.BlockSpec((K, 1024), lambda i: (0, 0))],
        out_specs=pl.BlockSpec((tm // dil, dil * 1024), lambda i: (i, 0)),
        out_shape=jax.ShapeDtypeStruct((T // dil, dil * 1024), MM_DTYPE),
        scratch_shapes=[pltpu.VMEM((8, tm, 128), F32)],
        compiler_params=_params("parallel"),
    )(a, b)


def mm_tn(a, g, name, ride=None):
    T, Ka = a.shape
    N = g.shape[1]
    tka, tt = _tile(Ka, 1536), _rows(T, 1024)
    whole_n = 2 * (tt * tka * a.dtype.itemsize + tt * N * g.dtype.itemsize + tka * N * 4)
    tn = N if N <= 3328 and whole_n <= 46 * 1024 * 1024 else _tile(N, 1536)
    nt = T // tt

    def body(a_ref, g_ref, o_ref):
        t = pl.program_id(2)
        p = _dot_tn(a_ref[...], g_ref[...])

        @pl.when(t == 0)
        def _():
            o_ref[...] = p

        @pl.when(t > 0)
        def _():
            o_ref[...] += p

    outs = _call(body, name, (Ka // tka, N // tn, nt),
                 [pl.BlockSpec((tt, tka), lambda i, j, t: (t, i)), pl.BlockSpec((tt, tn), lambda i, j, t: (t, j))],
                 [pl.BlockSpec((tka, tn), lambda i, j, t: (i, j))], [jax.ShapeDtypeStruct((Ka, N), F32)], [],
                 ("parallel", "parallel", "arbitrary"), (a, g), ride)
    return outs[0] if ride is None else outs


def rms_fwd(x, gains, name):
    T, D = x.shape
    tt = _rows(T, 512)
    ng = len(gains)

    def body(*refs):
        x_ref = refs[0]
        g_refs = refs[1:1 + ng]
        o_refs = refs[1 + ng:]
        xf = x_ref[...]
        y = xf * lax.rsqrt(jnp.mean(xf * xf, axis=-1, keepdims=True) + EPS)
        for g_ref, o_ref in zip(g_refs, o_refs):
            o_ref[...] = (y * g_ref[...]).astype(o_ref.dtype)

    row = pl.BlockSpec((tt, D), lambda i: (i, 0))
    gsp = pl.BlockSpec((1, D), lambda i: (0, 0))
    return pl.pallas_call(
        body, name=name, grid=(T // tt,),
        in_specs=[row] + [gsp] * ng, out_specs=[row] * ng,
        out_shape=[jax.ShapeDtypeStruct((T, D), MM_DTYPE)] * ng,
        compiler_params=_params("parallel"),
    )(x, *gains)


def rms_bwd(x, dres, branches, name):
    T, D = x.shape
    tt = _rows(T, 256)
    nb = len(branches)

    def body(*refs):
        x_ref, r_ref = refs[0], refs[1]
        dy_refs = refs[2:2 + nb]
        g_refs = refs[2 + nb:2 + 2 * nb]
        dx_ref = refs[2 + 2 * nb]
        dg_refs = refs[3 + 2 * nb:]
        i = pl.program_id(0)
        xf = x_ref[...]
        r = lax.rsqrt(jnp.mean(xf * xf, axis=-1, keepdims=True) + EPS)
        xh = xf * r
        dx = r_ref[...]
        for dy_ref, g_ref, dg_ref in zip(dy_refs, g_refs, dg_refs):
            dy = dy_ref[...].astype(F32)
            dyg = dy * g_ref[...]
            dx = dx + r * (dyg - xh * jnp.mean(dyg * xh, axis=-1, keepdims=True))
            part = jnp.sum(dy * xh, axis=0, keepdims=True)

            @pl.when(i == 0)
            def _():
                dg_ref[...] = part

            @pl.when(i > 0)
            def _():
                dg_ref[...] += part
        dx_ref[...] = dx

    row = pl.BlockSpec((tt, D), lambda i: (i, 0))
    gsp = pl.BlockSpec((1, D), lambda i: (0, 0))
    outs = pl.pallas_call(
        body, name=name, grid=(T // tt,),
        in_specs=[row, row] + [row] * nb + [gsp] * nb,
        out_specs=[row] + [gsp] * nb,
        out_shape=[jax.ShapeDtypeStruct((T, D), F32)] + [jax.ShapeDtypeStruct((1, D), F32)] * nb,
        compiler_params=_params("arbitrary"),
    )(x, dres, *[b[0] for b in branches], *[b[1] for b in branches])
    return outs[0], outs[1:]


def loss_head(x, target, gain):
    T, D = x.shape
    tt = _rows(T, 256)

    def body(x_ref, t_ref, g_ref, dx_ref, dg_ref, loss_ref):
        i = pl.program_id(0)
        xf = x_ref[...]
        g = g_ref[...]
        r = lax.rsqrt(jnp.mean(xf * xf, axis=-1, keepdims=True) + EPS)
        xh = xf * r
        e = xh * g - t_ref[...]
        lpart = 0.5 * jnp.sum(jnp.sum(e * e, axis=1, keepdims=True), axis=0, keepdims=True) / D
        dy = e / D
        dyg = dy * g
        dx_ref[...] = r * (dyg - xh * jnp.mean(dyg * xh, axis=-1, keepdims=True))
        gpart = jnp.sum(dy * xh, axis=0, keepdims=True)
        lrow = jnp.broadcast_to(lpart, (1, 128))

        @pl.when(i == 0)
        def _():
            dg_ref[...] = gpart
            loss_ref[...] = lrow

        @pl.when(i > 0)
        def _():
            dg_ref[...] += gpart
            loss_ref[...] += lrow

    row = pl.BlockSpec((tt, D), lambda i: (i, 0))
    gsp = pl.BlockSpec((1, D), lambda i: (0, 0))
    return pl.pallas_call(
        body, name="loss_head", grid=(T // tt,),
        in_specs=[row, row, gsp],
        out_specs=[row, gsp, pl.BlockSpec((1, 128), lambda i: (0, 0))],
        out_shape=[jax.ShapeDtypeStruct((T, D), F32), jax.ShapeDtypeStruct((1, D), F32),
                   jax.ShapeDtypeStruct((1, 128), F32)],
        compiler_params=_params("arbitrary"),
    )(x, target, gain)


def _shift_down(u, prev8, first, k):
    rolled = pltpu.roll(u, k, 0)
    rid = lax.broadcasted_iota(jnp.int32, u.shape, 0)
    halo = jnp.where(first, 0.0, prev8)
    out = rolled
    for j in range(k):
        out = jnp.where(rid == j, halo[8 - k + j:8 - k + j + 1, :], out)
    return out


def _conv3(u, prev8, first, w, b):
    return (_shift_down(u, prev8, first, 2) * w[0:1, :] + _shift_down(u, prev8, first, 1) * w[1:2, :]
            + u * w[2:3, :] + b)


def ffn_up_act(xn, w_up, w, b, name, ride=None):
    T, K = xn.shape
    tt = _rows(T, 512)
    nj = D_FF // FF_TC

    def body(x_ref, wu_ref, w_ref, b_ref, u_ref, o_ref, tail):
        first = pl.program_id(1) == 0
        u = _dot(x_ref[...], wu_ref[...])
        u_ref[...] = u
        c = _conv3(u, tail[...], first, w_ref[...], b_ref[...])
        tail[...] = u[tt - 8:, :]
        cg, cv = c[:, :FF_TC], c[:, FF_TC:]
        o_ref[...] = (cg * _sigmoid_tanh(cg) * cv).astype(o_ref.dtype)

    return _call(
        body, name, (nj, T // tt),
        [pl.BlockSpec((tt, K), lambda j, i: (i, 0)),
         pl.BlockSpec((K, 2 * FF_TC), lambda j, i: (0, j)),
         pl.BlockSpec((3, 2 * FF_TC), lambda j, i: (0, j)),
         pl.BlockSpec((1, 2 * FF_TC), lambda j, i: (0, j))],
        [pl.BlockSpec((tt, 2 * FF_TC), lambda j, i: (i, j)), pl.BlockSpec((tt, FF_TC), lambda j, i: (i, j))],
        [jax.ShapeDtypeStruct((T, 2 * D_FF), F32), jax.ShapeDtypeStruct((T, D_FF), MM_DTYPE)],
        [pltpu.VMEM((8, 2 * FF_TC), F32)], ("parallel", "arbitrary"), (xn, w_up, w, b), ride)


def conv_act_bwd(u, da, w, b, name, ride=None):
    T = u.shape[0]
    tt = _rows(T, 512)
    nt = T // tt
    nj = D_FF // FF_TC
    te = tt + 8

    def body(u_ref, p_ref, n_ref, da_ref, dan_ref, w_ref, b_ref, du_ref, dw_ref, db_ref):
        i = pl.program_id(1)
        first = i == 0
        last = i == nt - 1
        w = w_ref[...]
        ue = jnp.concatenate([u_ref[...], n_ref[...]], axis=0)
        dae = jnp.concatenate([da_ref[...], jnp.where(last, 0.0, dan_ref[...])], axis=0)
        um2 = _shift_down(ue, p_ref[...], first, 2)
        um1 = _shift_down(ue, p_ref[...], first, 1)
        c = um2 * w[0:1, :] + um1 * w[1:2, :] + ue * w[2:3, :] + b_ref[...]
        cg, cv = c[:, :FF_TC], c[:, FF_TC:]
        s = _sigmoid_tanh(cg)
        dcg = dae * cv * (s * (1.0 + cg * (1.0 - s)))
        dcv = dae * (cg * s)
        dc = jnp.concatenate([dcg, dcv], axis=1)
        du = (dc * w[2:3, :] + pltpu.roll(dc, te - 1, 0) * w[1:2, :] + pltpu.roll(dc, te - 2, 0) * w[0:1, :])
        du_ref[...] = du[:tt, :].astype(du_ref.dtype)
        dcm = dc[:tt, :]
        dwp = jnp.concatenate([jnp.sum(dcm * um2[:tt, :], axis=0, keepdims=True),
                               jnp.sum(dcm * um1[:tt, :], axis=0, keepdims=True),
                               jnp.sum(dcm * ue[:tt, :], axis=0, keepdims=True)], axis=0)
        dbp = jnp.sum(dcm, axis=0, keepdims=True)

        @pl.when(first)
        def _():
            dw_ref[...] = dwp
            db_ref[...] = dbp

        @pl.when(i > 0)
        def _():
            dw_ref[...] += dwp
            db_ref[...] += dbp

    nb8 = T // 8
    return _call(
        body, name, (nj, nt),
        [pl.BlockSpec((tt, 2 * FF_TC), lambda j, i: (i, j)),
         pl.BlockSpec((8, 2 * FF_TC), lambda j, i: (jnp.maximum(i * (tt // 8) - 1, 0), j)),
         pl.BlockSpec((8, 2 * FF_TC), lambda j, i: (jnp.minimum((i + 1) * (tt // 8), nb8 - 1), j)),
         pl.BlockSpec((tt, FF_TC), lambda j, i: (i, j)),
         pl.BlockSpec((8, FF_TC), lambda j, i: (jnp.minimum((i + 1) * (tt // 8), nb8 - 1), j)),
         pl.BlockSpec((3, 2 * FF_TC), lambda j, i: (0, j)),
         pl.BlockSpec((1, 2 * FF_TC), lambda j, i: (0, j))],
        [pl.BlockSpec((tt, 2 * FF_TC), lambda j, i: (i, j)),
         pl.BlockSpec((3, 2 * FF_TC), lambda j, i: (0, j)),
         pl.BlockSpec((1, 2 * FF_TC), lambda j, i: (0, j))],
        [jax.ShapeDtypeStruct((T, 2 * D_FF), MM_DTYPE), jax.ShapeDtypeStruct((3, 2 * D_FF), F32),
         jax.ShapeDtypeStruct((1, 2 * D_FF), F32)],
        [], ("parallel", "arbitrary"), (u, u, u, da, da, w, b), ride)


def _interleave(a):
    lead = a.shape[:-1]
    nj = D_FF // FF_TC
    return jnp.swapaxes(a.reshape(*lead, 2, nj, FF_TC), -3, -2).reshape(*lead, 2 * D_FF)


def _deinterleave(a):
    lead = a.shape[:-1]
    nj = D_FF // FF_TC
    return jnp.swapaxes(a.reshape(*lead, nj, 2, FF_TC), -3, -2).reshape(*lead, 2 * D_FF)


A_GC = 2
A_TB = A_GC * A_CHUNK


def gate_prep(z, bias128):
    T = z.shape[0]
    tt = _rows(T, 512)

    def body(z_ref, b_ref, gc_ref, gr_ref):
        pre = z_ref[...] + b_ref[...]
        sc = SOFTCAP * jnp.tanh(pre / SOFTCAP)
        lf = jnp.minimum(sc, 0.0) - jnp.log(1.0 + jnp.exp(-jnp.abs(sc)))
        col = lax.broadcasted_iota(jnp.int32, pre.shape, 1)
        isf = jnp.logical_and(col >= A_HEADS, col < 2 * A_HEADS)
        r = lax.broadcasted_iota(jnp.int32, (tt, tt), 0)
        c = lax.broadcasted_iota(jnp.int32, (tt, tt), 1)
        tri = jnp.logical_and(jnp.right_shift(r, 6) == jnp.right_shift(c, 6), c <= r).astype(F32)
        bcum = jnp.dot(tri, jnp.where(isf, lf, 0.0), precision=HI, preferred_element_type=F32)
        g = jnp.where(col < A_HEADS, sc, jnp.where(isf, bcum, 0.0))
        gc_ref[...] = g
        for s in range(tt // 128):
            gr_ref[s] = g[s * 128:(s + 1) * 128, :].T[0:8, :]

    return pl.pallas_call(
        body, name="gate_prep", grid=(T // tt,),
        in_specs=[pl.BlockSpec((tt, 128), lambda i: (i, GATE_COL // 128)),
                  pl.BlockSpec((1, 128), lambda i: (0, 0))],
        out_specs=[pl.BlockSpec((tt, 128), lambda i: (i, 0)),
                   pl.BlockSpec((tt // 128, 8, 128), lambda i: (i, 0, 0))],
        out_shape=[jax.ShapeDtypeStruct((T, 128), F32), jax.ShapeDtypeStruct((T // 128, 8, 128), F32)],
        compiler_params=_params("parallel"),
    )(z, bias128)


def _chunk_decay(A, qh, bc, br, lir, n, m, causal):
    logD = jnp.where(causal, bc - br + lir, -jnp.inf)
    m_inter = bc + m
    m_t = jnp.maximum(m_inter, jnp.max(logD, axis=1, keepdims=True))
    E = jnp.exp(logD - m_t)
    Sm = A * E
    wi = jnp.exp(m_inter - m_t)
    qn = jnp.sum(qh.astype(F32) * n, axis=1, keepdims=True)
    den = jnp.sum(Sm, axis=1, keepdims=True) + wi * qn
    gs = jnp.maximum(jnp.abs(den), jnp.exp(-m_t))
    return E, Sm, wi, den, gs, m_t


def _state_weights(bc, lic, br, lir, m):
    bL = bc[A_CHUNK - 1:A_CHUNK, :]
    m_new = jnp.maximum(bL + m, jnp.max(bL - br + lir, axis=1, keepdims=True))
    wk = jnp.exp(bL - bc + lic - m_new)
    decay = jnp.exp(bL + m - m_new)
    return wk, decay, m_new


def _head_slices(h):
    return (slice(h * A_QK, (h + 1) * A_QK), slice(h * A_V, (h + 1) * A_V))


def mlstm_fwd(z, gcol, grow, hng, ride=None):
    T = z.shape[0]
    NC = T // A_CHUNK
    scale = A_QK ** -0.5

    def body(q_ref, k_ref, v_ref, o_ref, gc_ref, gr_ref, hng_ref, hg_ref, Cs_ref, ns_ref, ms_ref,
             C_sc, n_sc, m_sc):
        @pl.when(pl.program_id(0) == 0)
        def _():
            C_sc[...] = jnp.zeros_like(C_sc)
            n_sc[...] = jnp.zeros_like(n_sc)
            m_sc[...] = jnp.zeros_like(m_sc)

        ri = lax.broadcasted_iota(jnp.int32, (A_CHUNK, A_CHUNK), 0)
        ci = lax.broadcasted_iota(jnp.int32, (A_CHUNK, A_CHUNK), 1)
        causal = ri >= ci
        gr = gr_ref[0]
        for c in range(A_GC):
            rows = slice(c * A_CHUNK, (c + 1) * A_CHUNK)
            gc = gc_ref[rows, :]
            grc = gr[:, c * A_CHUNK:(c + 1) * A_CHUNK]
            for h in range(A_HEADS):
                sk, sv = _head_slices(h)
                qh = (q_ref[rows, sk] * scale).astype(MM_DTYPE)
                kh = k_ref[rows, sk].astype(MM_DTYPE)
                vh = v_ref[rows, sv].astype(MM_DTYPE)
                lic, bc = gc[:, h:h + 1], gc[:, A_HEADS + h:A_HEADS + h + 1]
                lir, br = grc[h:h + 1, :], grc[A_HEADS + h:A_HEADS + h + 1, :]
                C, n, m = C_sc[h], n_sc[h], m_sc[h][:, 0:1]
                Cs_ref[c, h] = C
                ns_ref[c, h] = n
                ms_ref[c, h] = m_sc[h]
                _, Sm, wi, _, gs, _ = _chunk_decay(_dot_nt(qh, kh), qh, bc, br, lir, n, m, causal)
                hh = (_dot(Sm, vh) + wi * _dot(qh, C)) / gs
                hn = hh * lax.rsqrt(jnp.mean(hh * hh, axis=1, keepdims=True) + EPS) * hng_ref[:, sv]
                hg_ref[rows, sv] = (hn * _sigmoid(o_ref[rows, sv])).astype(hg_ref.dtype)
                wk, decay, m_new = _state_weights(bc, lic, br, lir, m)
                kw = kh.astype(F32) * wk
                C_sc[h] = decay * C + _dot_tn(kw, vh)
                n_sc[h] = decay * n + jnp.sum(kw, axis=0, keepdims=True)
                m_sc[h] = jnp.broadcast_to(m_new, (1, 128))

    tok = lambda w, cb: pl.BlockSpec((A_TB, w), lambda i: (i, cb))
    return _call(
        body, "mlstm_fwd", (NC // A_GC,),
        [tok(512, 0), tok(512, 1), tok(1024, 1), tok(1024, 2),
         pl.BlockSpec((A_TB, 128), lambda i: (i, 0)),
         pl.BlockSpec((1, 8, 128), lambda i: (i, 0, 0)),
         pl.BlockSpec((1, 1024), lambda i: (0, 0))],
        [pl.BlockSpec((A_TB, 1024), lambda i: (i, 0)),
         pl.BlockSpec((A_GC, A_HEADS, A_QK, A_V), lambda i: (i, 0, 0, 0)),
         pl.BlockSpec((A_GC, A_HEADS, 1, 128), lambda i: (i, 0, 0, 0)),
         pl.BlockSpec((A_GC, A_HEADS, 1, 128), lambda i: (i, 0, 0, 0))],
        [jax.ShapeDtypeStruct((T, 1024), MM_DTYPE),
         jax.ShapeDtypeStruct((NC, A_HEADS, A_QK, A_V), F32),
         jax.ShapeDtypeStruct((NC, A_HEADS, 1, 128), F32),
         jax.ShapeDtypeStruct((NC, A_HEADS, 1, 128), F32)],
        [pltpu.VMEM((A_HEADS, A_QK, A_V), F32), pltpu.VMEM((A_HEADS, 1, 128), F32),
         pltpu.VMEM((A_HEADS, 1, 128), F32)],
        ("arbitrary",), (z, z, z, z, gcol, grow, hng), ride)


def mlstm_bwd(z, gcol, grow, hng, bias128, Cs, ns, ms, dhg, ride=None):
    T = z.shape[0]
    NC = T // A_CHUNK
    nsteps = NC // A_GC
    scale = A_QK ** -0.5

    def body(q_ref, k_ref, v_ref, o_ref, zg_ref, gc_ref, gr_ref, hng_ref, b_ref, Cs_ref, ns_ref, ms_ref,
             dhg_ref, dz_ref, dgn_ref, dbif_ref, dC_sc, dn_sc):
        @pl.when(pl.program_id(0) == 0)
        def _():
            dC_sc[...] = jnp.zeros_like(dC_sc)
            dn_sc[...] = jnp.zeros_like(dn_sc)
            dgn_ref[...] = jnp.zeros_like(dgn_ref)
            dbif_ref[...] = jnp.zeros_like(dbif_ref)

        ri = lax.broadcasted_iota(jnp.int32, (A_CHUNK, A_CHUNK), 0)
        ci = lax.broadcasted_iota(jnp.int32, (A_CHUNK, A_CHUNK), 1)
        causal = ri >= ci
        upper = (ci >= ri).astype(F32)
        rid = lax.broadcasted_iota(jnp.int32, (A_CHUNK, 1), 0)
        col = lax.broadcasted_iota(jnp.int32, (A_CHUNK, 128), 1)
        gr = gr_ref[0]
        for c in reversed(range(A_GC)):
            rows = slice(c * A_CHUNK, (c + 1) * A_CHUNK)
            gc = gc_ref[rows, :]
            grc = gr[:, c * A_CHUNK:(c + 1) * A_CHUNK]
            dG = jnp.zeros((A_CHUNK, 128), F32)
            hs = []
            for h in range(A_HEADS):
                sk, sv = _head_slices(h)
                s = dict(sk=sk, sv=sv, qh=(q_ref[rows, sk] * scale).astype(MM_DTYPE),
                         kh=k_ref[rows, sk].astype(MM_DTYPE), vh=v_ref[rows, sv].astype(MM_DTYPE),
                         lic=gc[:, h:h + 1], bc=gc[:, A_HEADS + h:A_HEADS + h + 1],
                         lir=grc[h:h + 1, :], br=grc[A_HEADS + h:A_HEADS + h + 1, :],
                         C=Cs_ref[c, h], n=ns_ref[c, h], m=ms_ref[c, h][:, 0:1], dC=dC_sc[h], dn=dn_sc[h])
                s['qf'], s['kf'] = s['qh'].astype(F32), s['kh'].astype(F32)
                s['wk'], s['decay'], _ = _state_weights(s['bc'], s['lic'], s['br'], s['lir'], s['m'])
                hs.append(s)
            for s in hs:
                s['A'] = _dot_nt(s['qh'], s['kh'])
                s['qC'] = _dot(s['qh'], s['C'])
                s['vdC'] = _dot_nt(s['vh'], s['dC'])
                s['kdC'] = _dot(s['kh'], s['dC'])
            for s in hs:
                s['E'], s['Sm'], s['wi'], s['den'], s['gs'], s['m_t'] = _chunk_decay(
                    s['A'], s['qh'], s['bc'], s['br'], s['lir'], s['n'], s['m'], causal)
            for s in hs:
                s['num'] = _dot(s['Sm'], s['vh']) + s['wi'] * s['qC']
            for h, s in enumerate(hs):
                sv, gs = s['sv'], s['gs']
                hh = s['num'] / gs
                r = lax.rsqrt(jnp.mean(hh * hh, axis=1, keepdims=True) + EPS)
                gn = hng_ref[:, sv]
                sg = _sigmoid(o_ref[rows, sv])
                dhg_h = dhg_ref[rows, sv]
                dhn = dhg_h * sg
                dz_ref[rows, 2048 + h * A_V:2048 + (h + 1) * A_V] = (
                    dhg_h * (hh * r * gn) * sg * (1.0 - sg)).astype(dz_ref.dtype)
                dgn_ref[:, sv] += jnp.sum(dhn * hh * r, axis=0, keepdims=True)
                dyg = dhn * gn
                dh = r * dyg - hh * (r * r * r) * jnp.mean(dyg * hh, axis=1, keepdims=True)
                s['dnum'] = dh / gs
                live = (jnp.abs(s['den']) > jnp.exp(-s['m_t'])).astype(F32)
                s['dden'] = -jnp.sum(dh * hh, axis=1, keepdims=True) / gs * jnp.sign(s['den']) * live
            for s in hs:
                s['dnv'] = _dot_nt(s['dnum'], s['vh'])
                s['dnC'] = _dot_nt(s['dnum'], s['C'])
            for s in hs:
                s['dSE'] = jnp.where(causal, s['dnv'] + s['dden'], 0.0) * s['E']
            for s in hs:
                s['dq'] = _dot(s['dSE'], s['kh']) + s['wi'] * (s['dnC'] + s['dden'] * s['n'])
                s['dk_inter'] = s['wk'] * (s['vdC'] + s['dn'])
                s['dk'] = _dot_tn(s['dSE'], s['qh']) + s['dk_inter']
                s['dv'] = _dot_tn(s['Sm'], s['dnum']) + s['wk'] * s['kdC']
                s['dCq'] = _dot_tn(s['qf'] * s['wi'], s['dnum'])
            for h, s in enumerate(hs):
                dq, dk, qf, kf, dC, dn = s['dq'], s['dk'], s['qf'], s['kf'], s['dC'], s['dn']
                dz_ref[rows, s['sk']] = (dq * scale).astype(dz_ref.dtype)
                dz_ref[rows, 512 + h * A_QK:512 + (h + 1) * A_QK] = dk.astype(dz_ref.dtype)
                dz_ref[rows, 1024 + h * A_V:1024 + (h + 1) * A_V] = s['dv'].astype(dz_ref.dtype)
                dli = jnp.sum(kf * dk, axis=1, keepdims=True)
                db = jnp.sum(qf * dq, axis=1, keepdims=True) - dli
                usum = jnp.sum(jnp.sum(kf * s['dk_inter'], axis=1, keepdims=True), axis=0, keepdims=True)
                ddecay = (jnp.sum(jnp.sum(dC * s['C'], axis=1, keepdims=True), axis=0, keepdims=True)
                          + jnp.sum(dn * s['n'], axis=1, keepdims=True))
                db = db + jnp.where(rid == A_CHUNK - 1, usum + ddecay * s['decay'], 0.0)
                dG = dG + jnp.where(col == h, dli, 0.0) + jnp.where(col == A_HEADS + h, db, 0.0)
                dC_sc[h] = s['decay'] * dC + s['dCq']
                dn_sc[h] = s['decay'] * dn + jnp.sum(qf * (s['wi'] * s['dden']), axis=0, keepdims=True)
            dlf = jnp.dot(upper, dG, precision=HI, preferred_element_type=F32)
            pre = zg_ref[rows, :] + b_ref[...]
            th = jnp.tanh(pre / SOFTCAP)
            dcap = 1.0 - th * th
            dpre = jnp.where(col < A_HEADS, dG * dcap,
                             jnp.where(col < 2 * A_HEADS, dlf * _sigmoid(-SOFTCAP * th) * dcap, 0.0))
            dz_ref[rows, GATE_COL:GATE_COL + 128] = dpre.astype(dz_ref.dtype)
            dbif_ref[...] += jnp.sum(dpre, axis=0, keepdims=True)

    rev = lambda i: nsteps - 1 - i
    tok = lambda w, cb: pl.BlockSpec((A_TB, w), lambda i: (rev(i), cb))
    st = lambda a, b: pl.BlockSpec((A_GC, A_HEADS, a, b), lambda i: (rev(i), 0, 0, 0))
    return _call(
        body, "mlstm_bwd", (nsteps,),
        [tok(512, 0), tok(512, 1), tok(1024, 1), tok(1024, 2), tok(128, GATE_COL // 128),
         pl.BlockSpec((A_TB, 128), lambda i: (rev(i), 0)),
         pl.BlockSpec((1, 8, 128), lambda i: (rev(i), 0, 0)),
         pl.BlockSpec((1, 1024), lambda i: (0, 0)),
         pl.BlockSpec((1, 128), lambda i: (0, 0)),
         st(A_QK, A_V), st(1, 128), st(1, 128),
         pl.BlockSpec((A_TB, 1024), lambda i: (rev(i), 0))],
        [pl.BlockSpec((A_TB, A_IN_PAD), lambda i: (rev(i), 0)),
         pl.BlockSpec((1, 1024), lambda i: (0, 0)),
         pl.BlockSpec((1, 128), lambda i: (0, 0))],
        [jax.ShapeDtypeStruct((T, A_IN_PAD), MM_DTYPE), jax.ShapeDtypeStruct((1, 1024), F32),
         jax.ShapeDtypeStruct((1, 128), F32)],
        [pltpu.VMEM((A_HEADS, A_QK, A_V), F32), pltpu.VMEM((A_HEADS, 1, 128), F32)],
        ("arbitrary",), (z, z, z, z, z, gcol, grow, hng, bias128, Cs, ns, ms, dhg), ride)


def _t5_bucket(dist):
    max_exact = REL_BUCKETS // 2
    d = np.maximum(dist, 0)
    log_ratio = np.log(np.maximum(d, 1) / max_exact) / math.log(REL_MAX_DIST / max_exact)
    large = np.minimum(max_exact + (log_ratio * (REL_BUCKETS - max_exact)).astype(np.int64), REL_BUCKETS - 1)
    return np.where(d < max_exact, d, large).astype(np.int32)


def _group_bucket(g):
    delta = B_BLOCK + np.arange(B_BLOCK)[:, None] - np.arange(2 * B_BLOCK)[None, :]
    return _t5_bucket(delta * DILATIONS[g])


def _band_mask(n):
    ri = lax.broadcasted_iota(jnp.int32, (B_BLOCK, 2 * B_BLOCK), 0)
    ci = lax.broadcasted_iota(jnp.int32, (B_BLOCK, 2 * B_BLOCK), 1)
    band = jnp.logical_and(ci >= ri, ci <= ri + B_BLOCK)
    return jnp.logical_and(band, jnp.logical_or(ci >= B_BLOCK, n > 0))


def _both(p_ref, c_ref, sl):
    return jnp.concatenate([p_ref[:, sl], c_ref[:, sl]], axis=0)


def _scores(qh, kh, bias_h, valid):
    return jnp.where(valid, _dot_nt(qh, kh) * (B_DH ** -0.5) + bias_h, -jnp.inf)


def _attn_specs():
    wide = pl.BlockSpec((B_BLOCK, 1024), lambda r, n: (n, r))
    prev = pl.BlockSpec((B_BLOCK, 1024), lambda r, n: (jnp.maximum(n - 1, 0), r))
    narrow = pl.BlockSpec((B_BLOCK, 128), lambda r, n: (n, r))
    bias = pl.BlockSpec((B_HEADS, B_BLOCK, 2 * B_BLOCK), lambda r, n: (0, 0, 0))
    return wide, prev, narrow, bias


def _to_view(read_chunk, sc, o_ref, dil, nc, tt):
    for c in range(nc):
        sc[c] = read_chunk(c)
    for r in range(dil):
        for c in range(nc):
            lo = (r * nc + c) * 128
            o_ref[:, lo:lo + 128] = sc[c, pl.ds(r, tt // dil, stride=dil), :].astype(o_ref.dtype)


def _from_view(read_view, sc, dil, nc, tt):
    for r in range(dil):
        for c in range(nc):
            sc[c, pl.ds(r, tt // dil, stride=dil), :] = read_view((r * nc + c) * 128).astype(F32)


def attn_fwd(qv, kvw, vvw, bias, g):
    dil = DILATIONS[g]
    Tv = qv.shape[0]
    nb = Tv // B_BLOCK
    wide, prev, narrow, bsp = _attn_specs()

    def body(q_ref, kp_ref, kc_ref, vp_ref, vc_ref, b_ref, o_ref, lse_ref):
        valid = _band_mask(pl.program_id(1))
        lse_ref[...] = jnp.zeros_like(lse_ref)
        heads = [slice(h * B_DH, (h + 1) * B_DH) for h in range(B_HEADS)]
        S = [_scores(q_ref[:, sl], _both(kp_ref, kc_ref, sl), b_ref[h], valid) for h, sl in enumerate(heads)]
        P, L = [], []
        for h in range(B_HEADS):
            m = jnp.max(S[h], axis=1, keepdims=True)
            p = jnp.exp(S[h] - m)
            l = jnp.sum(p, axis=1, keepdims=True)
            lse_ref[:, h:h + 1] = m + jnp.log(l)
            P.append(p.astype(MM_DTYPE))
            L.append(l)
        for h, sl in enumerate(heads):
            o_ref[:, sl] = _dot(P[h], _both(vp_ref, vc_ref, sl)) / L[h]

    return pl.pallas_call(
        body, name=f"attn_fwd_g{g}", grid=(dil, nb),
        in_specs=[wide, prev, wide, prev, wide, bsp], out_specs=[wide, narrow],
        out_shape=[jax.ShapeDtypeStruct((Tv, dil * 1024), F32), jax.ShapeDtypeStruct((Tv, dil * 128), F32)],
        compiler_params=_params("parallel", "parallel"),
    )(qv, kvw, kvw, vvw, vvw, bias)


def attn_bwd(qv, kvw, vvw, bias, do_v, lse_v, dl_v, g):
    dil = DILATIONS[g]
    Tv = qv.shape[0]
    nb = Tv // B_BLOCK
    wide, prev, narrow, bsp = _attn_specs()

    def body(q_ref, kp_ref, kc_ref, vp_ref, vc_ref, b_ref, bt_ref, do_ref, lse_ref, dl_ref,
             dq_ref, dkc_ref, dkp_ref, dvc_ref, dvp_ref, db_ref):
        @pl.when(jnp.logical_and(pl.program_id(0) == 0, pl.program_id(1) == 0))
        def _():
            db_ref[...] = jnp.zeros_like(db_ref)

        n = pl.program_id(1)
        valid = _band_mask(n)
        ki = lax.broadcasted_iota(jnp.int32, (2 * B_BLOCK, B_BLOCK), 0)
        qi = lax.broadcasted_iota(jnp.int32, (2 * B_BLOCK, B_BLOCK), 1)
        valid_t = jnp.logical_and(jnp.logical_and(ki >= qi, ki <= qi + B_BLOCK), jnp.logical_or(ki >= B_BLOCK, n > 0))
        lse_t, dl_t = lse_ref[...].T, dl_ref[...].T
        heads = [slice(h * B_DH, (h + 1) * B_DH) for h in range(B_HEADS)]
        scale = B_DH ** -0.5
        PT, DS, DST = [], [], []
        for h, sl in enumerate(heads):
            qh, doh = q_ref[:, sl], do_ref[:, sl].astype(MM_DTYPE)
            kh, vh = _both(kp_ref, kc_ref, sl), _both(vp_ref, vc_ref, sl)
            p = jnp.exp(_scores(qh, kh, b_ref[h], valid) - lse_ref[:, h:h + 1])
            ds = p * (_dot_nt(doh, vh) - dl_ref[:, h:h + 1])
            db_ref[h] += ds
            DS.append((ds * scale).astype(MM_DTYPE))
            pt = jnp.exp(_scores(kh, qh, bt_ref[h], valid_t) - lse_t[h:h + 1, :])
            PT.append(pt.astype(MM_DTYPE))
            DST.append((pt * (_dot_nt(vh, doh) - dl_t[h:h + 1, :]) * scale).astype(MM_DTYPE))
        for h, sl in enumerate(heads):
            qh, doh = q_ref[:, sl], do_ref[:, sl].astype(MM_DTYPE)
            dq_ref[:, sl] = _dot(DS[h], _both(kp_ref, kc_ref, sl)).astype(MM_DTYPE)
            dk = _dot(DST[h], qh).astype(MM_DTYPE)
            dv = _dot(PT[h], doh).astype(MM_DTYPE)
            dkp_ref[:, sl], dkc_ref[:, sl] = dk[:B_BLOCK], dk[B_BLOCK:]
            dvp_ref[:, sl], dvc_ref[:, sl] = dv[:B_BLOCK], dv[B_BLOCK:]

    big = jax.ShapeDtypeStruct((Tv, dil * 1024), MM_DTYPE)
    bsp_t = pl.BlockSpec((B_HEADS, 2 * B_BLOCK, B_BLOCK), lambda r, n: (0, 0, 0))
    return pl.pallas_call(
        body, name=f"attn_bwd_g{g}", grid=(dil, nb),
        in_specs=[wide, prev, wide, prev, wide, bsp, bsp_t, wide, narrow, narrow],
        out_specs=[wide] * 5 + [bsp],
        out_shape=[big] * 5 + [jax.ShapeDtypeStruct((B_HEADS, B_BLOCK, 2 * B_BLOCK), F32)],
        compiler_params=_params("arbitrary", "arbitrary"),
    )(qv, kvw, kvw, vvw, vvw, bias, jnp.swapaxes(bias, 1, 2), do_v, lse_v, dl_v)


def _head_expand():
    e = np.zeros((128, 1024), np.float32)
    for h in range(B_HEADS):
        e[h, h * B_DH:(h + 1) * B_DH] = 1.0
    return e


A_TT = 256


def _view_spec(dil, width):
    return pl.BlockSpec((A_TT // dil, dil * width), lambda i: (i, 0))


def attn_merge(os_v, lses_v):
    T = os_v[0].shape[0]
    tt = A_TT
    expand = jnp.asarray(_head_expand())

    def body(o0, o1, o2, l0, l1, l2, e_ref, ob_ref, of_ref, lse0_ref, lse1_ref, lse2_ref, sc_o, sc_l):
        for gi, (o_ref, l_ref) in enumerate(((o1, l1), (o2, l2))):
            dil = DILATIONS[gi + 1]
            _from_view(lambda lo: o_ref[:, lo:lo + 128], sc_o.at[gi], dil, 8, tt)
            _from_view(lambda lo: l_ref[:, lo:lo + 128], sc_l.at[gi], dil, 1, tt)
        ls = [l0[...], sc_l[0, 0], sc_l[1, 0]]
        m = jnp.maximum(jnp.maximum(ls[0], ls[1]), ls[2])
        ex = [jnp.exp(l - m) for l in ls]
        tot = ex[0] + ex[1] + ex[2]
        lse = m + jnp.log(tot)
        lse0_ref[...] = lse
        _to_view(lambda c: lse, sc_l.at[2], lse1_ref, DILATIONS[1], 1, tt)
        _to_view(lambda c: lse, sc_l.at[2], lse2_ref, DILATIONS[2], 1, tt)
        ws = [e / tot for e in ex]
        for c in range(8):
            cols = slice(c * 128, (c + 1) * 128)
            ecol = e_ref[:, cols]
            spread = [jnp.dot(w, ecol, precision=HI, preferred_element_type=F32) for w in ws]
            out = spread[0] * o0[:, cols] + spread[1] * sc_o[0, c] + spread[2] * sc_o[1, c]
            of_ref[:, cols] = out
            ob_ref[:, cols] = out.astype(ob_ref.dtype)

    wide = pl.BlockSpec((tt, 1024), lambda i: (i, 0))
    return pl.pallas_call(
        body, name="attn_merge", grid=(T // tt,),
        in_specs=[_view_spec(d, 1024) for d in DILATIONS] + [_view_spec(d, 128) for d in DILATIONS]
        + [pl.BlockSpec((128, 1024), lambda i: (0, 0))],
        out_specs=[wide, wide] + [_view_spec(d, 128) for d in DILATIONS],
        out_shape=[jax.ShapeDtypeStruct((T, 1024), MM_DTYPE), jax.ShapeDtypeStruct((T, 1024), F32)]
        + [jax.ShapeDtypeStruct((T // d, d * 128), F32) for d in DILATIONS],
        scratch_shapes=[pltpu.VMEM((2, 8, tt, 128), F32), pltpu.VMEM((3, 1, tt, 128), F32)],
        compiler_params=_params("parallel"),
    )(*os_v, *lses_v, expand)


def attn_prep(datt, out):
    T = datt.shape[0]
    tt = A_TT
    expand_t = jnp.asarray(_head_expand().T.copy())

    def body(d_ref, o_ref, e_ref, do0, do1, do2, dl0, dl1, dl2, sc_d, sc_l):
        delta = jnp.dot(d_ref[...] * o_ref[...], e_ref[...], precision=HI, preferred_element_type=F32)
        do0[...] = d_ref[...].astype(do0.dtype)
        dl0[...] = delta
        for do_ref, dl_ref, dil in ((do1, dl1, DILATIONS[1]), (do2, dl2, DILATIONS[2])):
            _to_view(lambda c: d_ref[:, c * 128:(c + 1) * 128], sc_d, do_ref, dil, 8, tt)
            _to_view(lambda c: delta, sc_l, dl_ref, dil, 1, tt)

    wide = pl.BlockSpec((tt, 1024), lambda i: (i, 0))
    return pl.pallas_call(
        body, name="attn_prep", grid=(T // tt,),
        in_specs=[wide, wide, pl.BlockSpec((1024, 128), lambda i: (0, 0))],
        out_specs=[_view_spec(d, 1024) for d in DILATIONS] + [_view_spec(d, 128) for d in DILATIONS],
        out_shape=[jax.ShapeDtypeStruct((T // d, d * 1024), MM_DTYPE) for d in DILATIONS]
        + [jax.ShapeDtypeStruct((T // d, d * 128), F32) for d in DILATIONS],
        scratch_shapes=[pltpu.VMEM((8, tt, 128), F32), pltpu.VMEM((1, tt, 128), F32)],
        compiler_params=_params("parallel"),
    )(datt, out, expand_t)


def attn_combine(parts):
    T = parts[0][0].shape[0]
    tt = A_TT
    nt = T // tt
    shift = [None] + [B_BLOCK * d // tt for d in DILATIONS[1:]]

    def body(dq0, kc0, vc0, kpa0, kpb0, vpa0, vpb0, dq1, kc1, kp1, vc1, vp1, dq2, kc2, kp2, vc2, vp2,
             dq_ref, dkv_ref, sc):
        i = pl.program_id(0)
        dq_ref[:, 0:1024] = dq0[...].astype(dq_ref.dtype)
        for col, c_ref, pa_ref, pb_ref in ((0, kc0, kpa0, kpb0), (3, vc0, vpa0, vpb0)):
            nxt = jnp.where(i + 1 < nt, pb_ref[:tt // 2, :].astype(F32), 0.0)
            later = jnp.concatenate([pa_ref[tt // 2:, :].astype(F32), nxt], axis=0)
            dkv_ref[:, col * 1024:(col + 1) * 1024] = (c_ref[...].astype(F32) + later).astype(dkv_ref.dtype)
        for g, (dq, kc, kp, vc, vp) in ((1, (dq1, kc1, kp1, vc1, vp1)), (2, (dq2, kc2, kp2, vc2, vp2))):
            dil = DILATIONS[g]
            live = i + shift[g] < nt
            _from_view(lambda lo: dq[:, lo:lo + 128], sc, dil, 8, tt)
            for c in range(8):
                dq_ref[:, g * 1024 + c * 128:g * 1024 + (c + 1) * 128] = sc[c].astype(dq_ref.dtype)
            for col, c_ref, p_ref in ((g, kc, kp), (3 + g, vc, vp)):
                _from_view(lambda lo: c_ref[:, lo:lo + 128].astype(F32)
                           + jnp.where(live, p_ref[:, lo:lo + 128].astype(F32), 0.0), sc, dil, 8, tt)
                for c in range(8):
                    dkv_ref[:, col * 1024 + c * 128:col * 1024 + (c + 1) * 128] = sc[c].astype(dkv_ref.dtype)

    def later_spec(dil, blocks):
        return pl.BlockSpec((tt // dil, dil * 1024), lambda i: (jnp.minimum(i + blocks, nt - 1), 0))

    cur = [_view_spec(d, 1024) for d in DILATIONS]
    in_specs = [cur[0], cur[0], cur[0], cur[0], later_spec(1, 1), cur[0], later_spec(1, 1)]
    args = [parts[0][0], parts[0][1], parts[0][3], parts[0][2], parts[0][2], parts[0][4], parts[0][4]]
    for g in (1, 2):
        in_specs += [cur[g], cur[g], later_spec(DILATIONS[g], shift[g]), cur[g], later_spec(DILATIONS[g], shift[g])]
        args += list(parts[g][:5])
    return pl.pallas_call(
        body, name="attn_combine", grid=(nt,), in_specs=in_specs,
        out_specs=[pl.BlockSpec((tt, 3072), lambda i: (i, 0)), pl.BlockSpec((tt, 6144), lambda i: (i, 0))],
        out_shape=[jax.ShapeDtypeStruct((T, 3072), MM_DTYPE), jax.ShapeDtypeStruct((T, 6144), MM_DTYPE)],
        scratch_shapes=[pltpu.VMEM((8, tt, 128), F32)],
        compiler_params=_params("parallel"),
    )(*args)


def adamw(w, g, m, v, name):
    R, C = w.shape
    tr = R if R * C * 4 <= (1 << 20) else _rows(R, max(8, ((1 << 20) // (C * 4)) // 8 * 8))

    def body(w_ref, g_ref, m_ref, v_ref, d_ref, nm_ref, nv_ref):
        gg = g_ref[...]
        nm = ADAM_B1 * m_ref[...] + (1.0 - ADAM_B1) * gg
        nv = ADAM_B2 * v_ref[...] + (1.0 - ADAM_B2) * (gg * gg)
        m_hat = nm / (1.0 - ADAM_B1 ** ADAM_STEP)
        v_hat = nv / (1.0 - ADAM_B2 ** ADAM_STEP)
        d_ref[...] = -ADAM_LR * (m_hat / (jnp.sqrt(v_hat) + ADAM_EPS) + ADAM_WD * w_ref[...])
        nm_ref[...] = nm
        nv_ref[...] = nv

    blk = pl.BlockSpec((tr, C), lambda i: (i, 0))
    sds = jax.ShapeDtypeStruct((R, C), F32)
    return pl.pallas_call(
        body, name=name, grid=(R // tr,), in_specs=[blk] * 4, out_specs=[blk] * 3, out_shape=[sds] * 3,
        compiler_params=_params("parallel"),
    )(w, g, m, v)


def sum_slots(x, name, out_dtype=F32):
    n, R, C = x.shape
    tr = _rows(R, 256)

    def body(x_ref, o_ref):
        acc = x_ref[0].astype(F32)
        for s in range(1, n):
            acc = acc + x_ref[s].astype(F32)
        o_ref[...] = acc.astype(out_dtype)

    return pl.pallas_call(
        body, name=name, grid=(R // tr,),
        in_specs=[pl.BlockSpec((n, tr, C), lambda i: (0, i, 0))],
        out_specs=pl.BlockSpec((tr, C), lambda i: (i, 0)),
        out_shape=jax.ShapeDtypeStruct((R, C), out_dtype),
        compiler_params=_params("parallel"),
    )(x)


_ANY = pl.BlockSpec(memory_space=pl.ANY)
GROUP_ALL = ([(0, 0, 1), (0, 1, 0), (0, 1, 1), (1, 0, 0), (1, 0, 1), (1, 1, 0), (1, 1, 1)],
             lambda d: 4 * d[0] + 2 * d[1] + d[2])
GROUP_CHIPS = ([(0, 1, 0), (1, 0, 0), (1, 1, 0)], lambda d: 2 * d[0] + d[1])
GROUP_SIBLING = ([(0, 0, 1)], lambda d: d[2])


def _me():
    return lax.axis_index("x"), lax.axis_index("y"), lax.axis_index("c")


def _peer(me, flip):
    return tuple(1 - a if f else a for a, f in zip(me, flip))


class Exchange:
    def __init__(self, x, group, scatter):
        self.flips, self.slot = group
        self.scatter = scatter
        self.n = len(self.flips) + 1
        self.out_shape = jax.ShapeDtypeStruct((self.n,) + x.shape[-2:], x.dtype)
        self.scratch = [pltpu.SemaphoreType.DMA((self.n - 1,)), pltpu.SemaphoreType.DMA((self.n - 1,)),
                        pltpu.SemaphoreType.DMA]

    def _copies(self, x_ref, o_ref, send_sems, recv_sems, local_sem, arrivals):
        me = _me()
        slot = self.slot
        mine = pltpu.make_async_copy(x_ref.at[slot(me)] if self.scatter else x_ref, o_ref.at[slot(me)], local_sem)
        sends, landed = [], []
        for k, flip in enumerate(self.flips):
            peer = _peer(me, flip)
            sends.append(pltpu.make_async_remote_copy(
                src_ref=x_ref.at[slot(peer)] if self.scatter else x_ref, dst_ref=o_ref.at[slot(me)],
                send_sem=send_sems.at[k], recv_sem=recv_sems.at[k], device_id=peer, device_id_type=MESH_ID))
            if arrivals:
                landed.append(pltpu.make_async_remote_copy(
                    src_ref=o_ref.at[slot(me)], dst_ref=o_ref.at[slot(peer)], send_sem=send_sems.at[k],
                    recv_sem=recv_sems.at[k], device_id=peer, device_id_type=MESH_ID))
        return mine, sends, landed

    def start(self, *refs):
        mine, sends, _ = self._copies(*refs, arrivals=False)
        mine.start()
        for cp in sends:
            cp.start()

    def wait(self, *refs):
        mine, sends, arrivals = self._copies(*refs, arrivals=True)
        for cp in arrivals:
            cp.wait_recv()
        for cp in sends:
            cp.wait_send()
        mine.wait()

    def __call__(self, x, name):
        def body(*refs):
            self.start(*refs)
            self.wait(*refs)

        return pl.pallas_call(body, name=name, in_specs=[_ANY], out_specs=_ANY, out_shape=self.out_shape,
                              scratch_shapes=self.scratch)(x)


def group_gather(x, name, group):
    return Exchange(x, group, scatter=False)(x, name)


def group_scatter(x, name, group):
    return Exchange(x, group, scatter=True)(x, name)


def _call(body, name, grid, in_specs, out_specs, out_shape, scratch, semantics, args, ride=None):
    if ride is None:
        return pl.pallas_call(body, name=name, grid=grid, in_specs=in_specs, out_specs=out_specs,
                              out_shape=out_shape, scratch_shapes=scratch,
                              compiler_params=_params(*semantics))(*args)
    x, exch = ride
    n_in, n_out, n_scr = len(in_specs), len(out_specs), len(scratch)

    def at_step(pick):
        hit = None
        for axis, size in enumerate(grid):
            here = pl.program_id(axis) == pick(size)
            hit = here if hit is None else jnp.logical_and(hit, here)
        return hit

    def riding(*refs):
        ins, x_ref = refs[:n_in], refs[n_in]
        outs, o_ref = refs[n_in + 1:n_in + 1 + n_out], refs[n_in + 1 + n_out]
        scr, sems = refs[n_in + 2 + n_out:n_in + 2 + n_out + n_scr], refs[n_in + 2 + n_out + n_scr:]

        @pl.when(at_step(lambda size: 0))
        def _():
            exch.start(x_ref, o_ref, *sems)

        body(*ins, *outs, *scr)

        @pl.when(at_step(lambda size: size - 1))
        def _():
            exch.wait(x_ref, o_ref, *sems)

    return pl.pallas_call(
        riding, name=name, grid=grid, in_specs=list(in_specs) + [_ANY], out_specs=list(out_specs) + [_ANY],
        out_shape=list(out_shape) + [exch.out_shape], scratch_shapes=list(scratch) + exch.scratch,
        compiler_params=_params(*(["arbitrary"] * len(grid))))(*args, x)


WEIGHTS = ['a_norm_g', 'a_w_in', 'a_b_if', 'a_hnorm_g', 'a_w_out', 'kv_norm_g', 'w_kv', 'b_norm_g', 'b_w_q',
           'b_w_out', 'rel_bias', 'f_norm_g', 'f_w_up', 'f_conv_w', 'f_conv_b', 'f_w_down', 'final_norm_g']
SHARD_AXIS = {'a_norm_g': 1, 'a_w_in': 2, 'a_b_if': None, 'a_hnorm_g': 2, 'a_w_out': 1, 'kv_norm_g': None,
              'w_kv': 1, 'b_norm_g': None, 'b_w_q': 2, 'b_w_out': 1, 'rel_bias': None, 'f_norm_g': None,
              'f_w_up': 2, 'f_conv_w': 2, 'f_conv_b': None, 'f_w_down': 1, 'final_norm_g': None}
BIG = ['a_w_in', 'a_w_out', 'w_kv', 'b_w_q', 'b_w_out', 'f_w_up', 'f_w_down']
SMALL = [n for n in WEIGHTS if n not in BIG]
LANES = 1024
PIECES = {'a_w_in': ('a_w_in', None, 2), 'a_w_out': ('a_w_out', None, 1), 'f_w_up0': ('f_w_up', 0, 1),
          'f_w_down0': ('f_w_down', 0, 0), 'w_kv': ('w_kv', None, 1), 'b_w_q': ('b_w_q', None, 2),
          'b_w_out': ('b_w_out', None, 1), 'f_w_up1': ('f_w_up', 1, 1), 'f_w_down1': ('f_w_down', 1, 0)}
LATE = ['w_kv', 'b_w_q', 'b_w_out', 'f_w_up1', 'f_w_down1']
WEIGHT_WAVES = {'first': ['a_w_in', 'a_w_out'], 'ffn0': ['f_w_up0', 'f_w_down0'], 'late': LATE}
GRAD_WAVES = {'late': LATE, 'layer0': ['f_w_up0', 'f_w_down0', 'a_w_out'], 'last': ['a_w_in']}


def _piece(arrays, p):
    leaf, layer, _ = PIECES[p]
    return arrays[leaf] if layer is None else arrays[leaf][layer]


class Packer:
    def __init__(self, pieces, shard):
        self.pieces = pieces
        self.shapes = [_piece(shard, p).shape for p in pieces]
        self.sizes = [math.prod(s) // (2 * LANES) for s in self.shapes]
        self.fill = -sum(self.sizes) % 16
        self.rows = sum(self.sizes) + self.fill

    def my_half(self, shard, half):
        both = jnp.concatenate([_piece(shard, p).astype(MM_DTYPE).reshape(2, -1, LANES) for p in self.pieces], axis=1)
        return jnp.pad(lax.dynamic_index_in_dim(both, half, axis=0, keepdims=False), ((0, self.fill), (0, 0)))

    def full_weights(self, gathered):
        g = gathered.reshape(4, 2, self.rows, LANES)
        out, off = {}, 0
        for p, shp, sz in zip(self.pieces, self.shapes, self.sizes):
            out[p] = _full_from_shards(g[:, :, off:off + sz].reshape((4,) + shp), PIECES[p][2])
            off += sz
        return out

    def grad_slots(self, grads):
        parts = [_shards_from_full(grads[p], PIECES[p][2]).reshape(4, 2, -1, LANES).astype(GRAD_WIRE_DTYPE)
                 for p in self.pieces]
        parts.append(jnp.zeros((4, 2, self.fill, LANES), GRAD_WIRE_DTYPE))
        return jnp.concatenate(parts, axis=2).reshape(8, self.rows, LANES)

    def shard_grads(self, both):
        out, off = {}, 0
        for p, shp, sz in zip(self.pieces, self.shapes, self.sizes):
            out[p] = both[:, off:off + sz].reshape(shp).astype(F32)
            off += sz
        return out


class Overlap:
    def __init__(self, shard, half):
        self.shard, self.half = shard, half
        self.weights = {w: Packer(p, shard) for w, p in WEIGHT_WAVES.items()}
        self.grads = {w: Packer(p, shard) for w, p in GRAD_WAVES.items()}
        self.shard_grads = {}

    def gather_ride(self, wave):
        mine = self.weights[wave].my_half(self.shard, self.half)
        return mine, Exchange(mine, GROUP_ALL, scatter=False)

    def gathered(self, wave, slots):
        return self.weights[wave].full_weights(slots)

    def scatter_ride(self, wave, grads):
        slots = self.grads[wave].grad_slots({p: grads.pop(p) for p in GRAD_WAVES[wave]})
        return slots, Exchange(slots, GROUP_ALL, scatter=True)

    def join_ride(self, wave, received):
        reduced = sum_slots(received, f"sum_grads_{wave}", GRAD_WIRE_DTYPE)
        return reduced, Exchange(reduced, GROUP_SIBLING, scatter=False)

    def joined(self, wave, both):
        self.shard_grads.update(self.grads[wave].shard_grads(both))


def _pad_rows(flat, mult):
    n = flat.shape[0]
    per = LANES * mult
    tot = -(-n // per) * per
    return jnp.pad(flat, (0, tot - n)).reshape(tot // LANES, LANES)


def _full_from_shards(sh, axis):
    shp = sh.shape[1:]
    return jnp.moveaxis(sh, 0, axis).reshape(shp[:axis] + (4 * shp[axis],) + shp[axis + 1:])


def _shards_from_full(full, axis):
    shp = full.shape
    return jnp.moveaxis(full.reshape(shp[:axis] + (4, shp[axis] // 4) + shp[axis + 1:]), axis, 0)


def _local_step(x, target, W, overlap=None):
    T = x.shape[0]
    W = dict(W)
    row = lambda a: a.reshape(1, -1).astype(F32)
    w_in = jnp.pad(W['a_w_in'][0], ((0, 0), (0, A_IN_PAD - A_IN)))
    bias128 = jnp.pad(row(W['a_b_if'][0]), ((0, 0), (0, 120)))
    hng = row(W['a_hnorm_g'][0])
    w_up = lambda l: _interleave(W[f'f_w_up{l}'])
    cw = [_interleave(W['f_conv_w'][l].astype(F32)) for l in range(2)]
    cb = [_interleave(row(W['f_conv_b'][l])) for l in range(2)]
    onehots = [(jnp.asarray(_group_bucket(g).reshape(-1, 1)) == jnp.arange(128)[None, :]).astype(F32)
               for g in range(N_GROUPS)]
    rb_t = jnp.pad(W['rel_bias'].astype(F32).T, ((0, 0), (0, 128 - REL_BUCKETS)))
    biases = [mm_nn(rb_t[g * B_HEADS:(g + 1) * B_HEADS], onehots[g].T, f"rel_bias_table_g{g}", exact=True)
              .reshape(B_HEADS, B_BLOCK, 2 * B_BLOCK) for g in range(N_GROUPS)]
    G = {}

    def ffn_fwd(xin, l, ride=None):
        xn, = rms_fwd(xin, [row(W['f_norm_g'][l])], f"ffn{l}_norm")
        u, act, *rode = ffn_up_act(xn, w_up(l), cw[l], cb[l], f"ffn{l}_up_act", ride)
        return mm_nn(act, W[f'f_w_down{l}'], f"ffn{l}_down", res=xin), (xn, u, act), rode

    def ffn_bwd(xin, saved, dout, l, ride=None):
        xn, u, act = saved
        dact = mm_nn(dout, W[f'f_w_down{l}'].T, f"ffn{l}_ddown")
        G[f'f_w_down{l}'] = mm_tn(act, dout, f"ffn{l}_gdown")
        du, gcw, gcb, *rode = conv_act_bwd(u, dact, cw[l], cb[l], f"ffn{l}_dact", ride)
        dxn = mm_nn(du, w_up(l).T, f"ffn{l}_dup")
        G[f'f_w_up{l}'] = _deinterleave(mm_tn(xn, du, f"ffn{l}_gup"))
        dxin, (gn,) = rms_bwd(xin, dout, [(dxn, row(W['f_norm_g'][l]))], f"ffn{l}_dnorm")
        return dxin, _deinterleave(gcw), _deinterleave(gcb), gn, rode

    xn_a, = rms_fwd(x, [row(W['a_norm_g'][0])], "a_norm")
    z = mm_nn(xn_a, w_in, "a_in")
    gcol, grow = gate_prep(z, bias128)
    hg, Cs, ns, ms, *rode = mlstm_fwd(z, gcol, grow, hng, overlap.gather_ride('ffn0') if overlap else None)
    if overlap:
        W.update(overlap.gathered('ffn0', rode[0]))
    x1 = mm_nn(hg, W['a_w_out'][0], "a_out", res=x)
    x2, ffn0, rode = ffn_fwd(x1, 0, overlap.gather_ride('late') if overlap else None)
    if overlap:
        W.update(overlap.gathered('late', rode[0]))
    xn_kv, xn_b = rms_fwd(x2, [row(W['kv_norm_g']), row(W['b_norm_g'][0])], "b_norms")
    gcols = lambda w, c: w[:, c * 1024:(c + 1) * 1024]
    qv = [mm_view(xn_b, gcols(W['b_w_q'][0], g), f"q_proj_g{g}", DILATIONS[g]) for g in range(N_GROUPS)]
    kvw = [mm_view(xn_kv, gcols(W['w_kv'], g), f"k_proj_g{g}", DILATIONS[g]) for g in range(N_GROUPS)]
    vvw = [mm_view(xn_kv, gcols(W['w_kv'], 3 + g), f"v_proj_g{g}", DILATIONS[g]) for g in range(N_GROUPS)]
    os_, lses = zip(*[attn_fwd(qv[g], kvw[g], vvw[g], biases[g], g) for g in range(N_GROUPS)])
    att, att_f, *lse_v = attn_merge(os_, lses)
    x3 = mm_nn(att, W['b_w_out'][0], "b_out", res=x2)
    x4, ffn1, _ = ffn_fwd(x3, 1)
    dx4, g_final, loss = loss_head(x4, target, row(W['final_norm_g']))
    G['final_norm_g'] = g_final.reshape(-1)

    dx3, gcw1, gcb1, gn1, _ = ffn_bwd(x3, ffn1, dx4, 1)
    datt = mm_nn(dx3, W['b_w_out'][0].T, "b_dout")
    G['b_w_out'] = mm_tn(att, dx3, "b_gout")[None]
    prep = attn_prep(datt, att_f)
    do_v, dl_v = prep[:3], prep[3:]
    parts = [attn_bwd(qv[g], kvw[g], vvw[g], biases[g], do_v[g], lse_v[g], dl_v[g], g) for g in range(N_GROUPS)]
    dq_all, dkv = attn_combine(parts)
    grb = []
    for g in range(N_GROUPS):
        gb = mm_nn(parts[g][5].reshape(B_HEADS, -1), onehots[g], f"rel_bias_g{g}", exact=True)
        grb.append(gb[:, :REL_BUCKETS].T)
    G['rel_bias'] = jnp.concatenate(grb, axis=1)
    dxn_b = mm_nn(dq_all, W['b_w_q'][0].T, "q_dproj")
    G['b_w_q'] = mm_tn(xn_b, dq_all, "q_gproj")[None]
    dxn_kv = mm_nn(dkv, W['w_kv'].T, "kv_dproj")
    G['w_kv'] = mm_tn(xn_kv, dkv, "kv_gproj")
    dx2, (g_kvn, g_bn) = rms_bwd(x2, dx3, [(dxn_kv, row(W['kv_norm_g'])), (dxn_b, row(W['b_norm_g'][0]))],
                                 "b_dnorms")
    G['kv_norm_g'] = g_kvn.reshape(-1)
    G['b_norm_g'] = g_bn
    dx1, gcw0, gcb0, gn0, late_slots = ffn_bwd(x1, ffn0, dx2, 0, overlap.scatter_ride('late', G) if overlap else None)
    G['f_conv_w'] = jnp.stack([gcw0, gcw1])
    G['f_conv_b'] = jnp.concatenate([gcb0, gcb1], axis=0)
    G['f_norm_g'] = jnp.concatenate([gn0, gn1], axis=0)
    dhg = mm_nn(dx1, W['a_w_out'][0].T, "a_dout")
    G['a_w_out'] = mm_tn(hg, dx1, "a_gout")[None]
    dz, g_hn, g_bif, *layer0_slots = mlstm_bwd(z, gcol, grow, hng, bias128, Cs, ns, ms, dhg,
                                               overlap.scatter_ride('layer0', G) if overlap else None)
    G['a_hnorm_g'] = g_hn.reshape(1, A_HEADS, A_V)
    G['a_b_if'] = g_bif[:, :2 * A_HEADS]
    if overlap:
        dxn_a, both = mm_nn(dz, w_in.T, "a_din", ride=overlap.join_ride('late', late_slots[0]))
        overlap.joined('late', both)
        g_in, both = mm_tn(xn_a, dz, "a_gin", ride=overlap.join_ride('layer0', layer0_slots[0]))
        overlap.joined('layer0', both)
    else:
        dxn_a = mm_nn(dz, w_in.T, "a_din")
        g_in = mm_tn(xn_a, dz, "a_gin")
    G['a_w_in'] = g_in[:, :A_IN][None]
    grad_x, (g_an,) = rms_bwd(x, dx1, [(dxn_a, row(W['a_norm_g'][0]))], "a_dnorm")
    G['a_norm_g'] = g_an
    return loss, grad_x, G


def kernel(x, a_norm_g, a_w_in, a_b_if, a_hnorm_g, a_w_out, kv_norm_g, w_kv, b_norm_g, b_w_q, b_w_out, rel_bias, f_norm_g, f_w_up, f_conv_w, f_conv_b, f_w_down, final_norm_g, loss_target, m_a_norm_g, m_a_w_in, m_a_b_if, m_a_hnorm_g, m_a_w_out, m_kv_norm_g, m_w_kv, m_b_norm_g, m_b_w_q, m_b_w_out, m_rel_bias, m_f_norm_g, m_f_w_up, m_f_conv_w, m_f_conv_b, m_f_w_down, m_final_norm_g, v_a_norm_g, v_a_w_in, v_a_b_if, v_a_hnorm_g, v_a_w_out, v_kv_norm_g, v_w_kv, v_b_norm_g, v_b_w_q, v_b_w_out, v_rel_bias, v_f_norm_g, v_f_w_up, v_f_conv_w, v_f_conv_b, v_f_w_down, v_final_norm_g):
    given = dict(locals())
    shard = {n: given[n] for n in WEIGHTS}
    mom = {n: given["m_" + n] for n in WEIGHTS}
    var = {n: given["v_" + n] for n in WEIGHTS}
    cx, cy, cc = _me()
    chip = 2 * cx + cy

    overlap = Overlap(shard, cc)
    mine, gather = overlap.gather_ride('first')
    W = overlap.gathered('first', gather(mine, "gather_weights"))
    sharded_small = [n for n in SMALL if SHARD_AXIS[n] is not None]
    ssz = [shard[n].size for n in sharded_small]
    sflat = jnp.concatenate([shard[n].reshape(-1) for n in sharded_small])
    sg = group_gather(_pad_rows(sflat, 8), "gather_small", GROUP_CHIPS).reshape(4, -1)
    off = 0
    for n, sz in zip(sharded_small, ssz):
        W[n] = _full_from_shards(sg[:, off:off + sz].reshape((4,) + shard[n].shape), SHARD_AXIS[n])
        off += sz
    for n in SMALL:
        if SHARD_AXIS[n] is None:
            W[n] = shard[n]

    loss_row, grad_x, G = _local_step(x[0], loss_target[0], W, overlap)

    slots, scatter = overlap.scatter_ride('last', G)
    reduced, join = overlap.join_ride('last', scatter(slots, "scatter_grads"))
    overlap.joined('last', join(reduced, "join_halves"))
    by_piece = overlap.shard_grads
    gsh = {}
    for n in BIG:
        layers = [p for p in PIECES if PIECES[p][0] == n]
        gsh[n] = by_piece[n] if layers == [n] else jnp.stack([by_piece[p] for p in layers])
    small_parts = [loss_row[0, 0:1]] + [G[n].reshape(-1) for n in SMALL]
    small_sz = [p.shape[0] for p in small_parts]
    small = sum_slots(group_gather(_pad_rows(jnp.concatenate(small_parts), 8), "gather_small_grads", GROUP_ALL),
                      "sum_small_grads").reshape(-1)
    loss = small[0]
    off = 1
    for n, sz in zip(SMALL, small_sz[1:]):
        full = small[off:off + sz].reshape(W[n].shape)
        off += sz
        if SHARD_AXIS[n] is None:
            gsh[n] = full
        else:
            gsh[n] = lax.dynamic_index_in_dim(_shards_from_full(full, SHARD_AXIS[n]), chip, 0, keepdims=False)

    delta, new_m, new_v = {}, {}, {}
    for n in WEIGHTS:
        shp = shard[n].shape
        two = lambda a: a.reshape(-1, shp[-1])
        d, nm, nv = adamw(two(shard[n]), two(gsh[n]), two(mom[n]), two(var[n]), f"adamw_{n}")
        delta[n], new_m[n], new_v[n] = d.reshape(shp), nm.reshape(shp), nv.reshape(shp)
    return (loss, grad_x[None], *[gsh[n] for n in WEIGHTS], *[delta[n] for n in WEIGHTS],
            *[new_m[n] for n in WEIGHTS], *[new_v[n] for n in WEIGHTS])
```

```python
import functools
import math

import numpy as np
import jax
import jax.numpy as jnp
from jax import lax
from jax.experimental import pallas as pl
from jax.experimental.pallas import tpu as pltpu

F32 = jnp.float32
BF16 = jnp.bfloat16
MM_DTYPE = jnp.bfloat16
GRAD_WIRE_DTYPE = jnp.bfloat16
HI = lax.Precision.HIGHEST

D_MODEL = 1024
A_HEADS = 4
A_QK = 128
A_V = 256
A_CHUNK = 128
A_IN = 3080
A_IN_PAD = 3200
GATE_COL = 3072
SOFTCAP = 15.0
N_GROUPS = 3
B_HEADS = 16
B_DH = 64
B_BLOCK = 128
DILATIONS = (1, 4, 16)
WINDOWS = (128, 512, 2048)
REL_BUCKETS = 32
REL_MAX_DIST = 2048
D_FF = 2816
FF_TC = 256
EPS = 1e-6
ADAM_LR, ADAM_B1, ADAM_B2, ADAM_EPS, ADAM_WD, ADAM_STEP = 0.001, 0.9, 0.999, 1e-08, 0.01, 10

VMEM_LIMIT = 56 * 1024 * 1024
NT_DIMS = (((1,), (1,)), ((), ()))
TN_DIMS = (((0,), (0,)), ((), ()))
MESH_ID = pl.DeviceIdType.MESH


def _params(*sem):
    return pltpu.CompilerParams(dimension_semantics=sem, vmem_limit_bytes=VMEM_LIMIT)


def _tile(n, cap):
    if n <= cap:
        return n
    best = None
    for t in range(128, cap + 1, 128):
        if n % t == 0:
            best = t
    assert best is not None, (n, cap)
    return best


def _rows(n, cap):
    if n <= cap:
        return n
    for t in range(cap // 8 * 8, 7, -8):
        if n % t == 0:
            return t
    raise ValueError((n, cap))


def _dot(a, b):
    return jnp.dot(a.astype(MM_DTYPE), b.astype(MM_DTYPE), preferred_element_type=F32)


def _dot_nt(a, b):
    return lax.dot_general(a.astype(MM_DTYPE), b.astype(MM_DTYPE), NT_DIMS, preferred_element_type=F32)


def _dot_tn(a, b):
    return lax.dot_general(a.astype(MM_DTYPE), b.astype(MM_DTYPE), TN_DIMS, preferred_element_type=F32)


def _sigmoid(x):
    return 1.0 / (1.0 + jnp.exp(-x))


def _sigmoid_tanh(x):
    return 0.5 * jnp.tanh(0.5 * x) + 0.5


def mm_nn(a, b, name, res=None, out_dtype=F32, exact=False, ride=None):
    M, K = a.shape
    N = b.shape[1]
    tm = _rows(M, 512)

    def footprint(tn):
        return 2 * (tm * K * a.dtype.itemsize + K * tn * b.dtype.itemsize) + 2 * tm * tn * 4 * (1 if res is None else 2)

    budget = 46 * 1024 * 1024
    tn = N if N <= 3328 and footprint(N) <= budget else _tile(N, 1536)
    tk = K if footprint(tn) <= budget else _tile(K, 1536)
    nk = K // tk

    def body(*refs):
        if res is None:
            a_ref, b_ref, o_ref, acc = refs
            r_ref = None
        else:
            a_ref, b_ref, r_ref, o_ref, acc = refs
        if exact:
            p = jnp.dot(a_ref[...], b_ref[...], precision=HI, preferred_element_type=F32)
        else:
            p = _dot(a_ref[...], b_ref[...])

        def finish(total):
            if r_ref is not None:
                total = total + r_ref[...]
            o_ref[...] = total.astype(out_dtype)

        if nk == 1:
            finish(p)
        else:
            k = pl.program_id(2)

            @pl.when(k == 0)
            def _():
                acc[...] = p

            @pl.when(jnp.logical_and(k > 0, k < nk - 1))
            def _():
                acc[...] += p

            @pl.when(k == nk - 1)
            def _():
                finish(acc[...] + p)

    in_specs = [pl.BlockSpec((tm, tk), lambda j, i, k: (i, k)),
                pl.BlockSpec((tk, tn), lambda j, i, k: (k, j))]
    args = [a, b]
    if res is not None:
        in_specs.append(pl.BlockSpec((tm, tn), lambda j, i, k: (i, j)))
        args.append(res)
    acc_shape = (tm, tn) if nk > 1 else (8, 128)
    outs = _call(body, name, (N // tn, M // tm, nk), in_specs, [pl.BlockSpec((tm, tn), lambda j, i, k: (i, j))],
                 [jax.ShapeDtypeStruct((M, N), out_dtype)], [pltpu.VMEM(acc_shape, F32)],
                 ("parallel", "parallel", "arbitrary"), args, ride)
    return outs[0] if ride is None else outs


def mm_view(a, b, name, dil):
    T, K = a.shape
    tm = 512

    def body(a_ref, b_ref, o_ref, sc):
        p = _dot(a_ref[...], b_ref[...])
        if dil == 1:
            o_ref[...] = p.astype(o_ref.dtype)
        else:
            _to_view(lambda c: p[:, c * 128:(c + 1) * 128], sc, o_ref, dil, 8, tm)

    return pl.pallas_call(
        body, name=name, grid=(T // tm,),
        in_specs=[pl.BlockSpec((tm, K), lambda i: (i, 0)), pl.BlockSpec((K, 1024), lambda i: (0, 0))],
        out_specs=pl.BlockSpec((tm // dil, dil * 1024), lambda i: (i, 0)),
        out_shape=jax.ShapeDtypeStruct((T // dil, dil * 1024), MM_DTYPE),
        scratch_shapes=[pltpu.VMEM((8, tm, 128), F32)],
        compiler_params=_params("parallel"),
    )(a, b)


def mm_tn(a, g, name, ride=None):
    T, Ka = a.shape
    N = g.shape[1]
    tka, tt = _tile(Ka, 1536), _rows(T, 1024)
    whole_n = 2 * (tt * tka * a.dtype.itemsize + tt * N * g.dtype.itemsize + tka * N * 4)
    tn = N if N <= 3328 and whole_n <= 46 * 1024 * 1024 else _tile(N, 1536)
    nt = T // tt

    def body(a_ref, g_ref, o_ref):
        t = pl.program_id(2)
        p = _dot_tn(a_ref[...], g_ref[...])

        @pl.when(t == 0)
        def _():
            o_ref[...] = p

        @pl.when(t > 0)
        def _():
            o_ref[...] += p

    outs = _call(body, name, (Ka // tka, N // tn, nt),
                 [pl.BlockSpec((tt, tka), lambda i, j, t: (t, i)), pl.BlockSpec((tt, tn), lambda i, j, t: (t, j))],
                 [pl.BlockSpec((tka, tn), lambda i, j, t: (i, j))], [jax.ShapeDtypeStruct((Ka, N), F32)], [],
                 ("parallel", "parallel", "arbitrary"), (a, g), ride)
    return outs[0] if ride is None else outs


def rms_fwd(x, gains, name):
    T, D = x.shape
    tt = _rows(T, 512)
    ng = len(gains)

    def body(*refs):
        x_ref = refs[0]
        g_refs = refs[1:1 + ng]
        o_refs = refs[1 + ng:]
        xf = x_ref[...]
        y = xf * lax.rsqrt(jnp.mean(xf * xf, axis=-1, keepdims=True) + EPS)
        for g_ref, o_ref in zip(g_refs, o_refs):
            o_ref[...] = (y * g_ref[...]).astype(o_ref.dtype)

    row = pl.BlockSpec((tt, D), lambda i: (i, 0))
    gsp = pl.BlockSpec((1, D), lambda i: (0, 0))
    return pl.pallas_call(
        body, name=name, grid=(T // tt,),
        in_specs=[row] + [gsp] * ng, out_specs=[row] * ng,
        out_shape=[jax.ShapeDtypeStruct((T, D), MM_DTYPE)] * ng,
        compiler_params=_params("parallel"),
    )(x, *gains)


def rms_bwd(x, dres, branches, name):
    T, D = x.shape
    tt = _rows(T, 256)
    nb = len(branches)

    def body(*refs):
        x_ref, r_ref = refs[0], refs[1]
        dy_refs = refs[2:2 + nb]
        g_refs = refs[2 + nb:2 + 2 * nb]
        dx_ref = refs[2 + 2 * nb]
        dg_refs = refs[3 + 2 * nb:]
        i = pl.program_id(0)
        xf = x_ref[...]
        r = lax.rsqrt(jnp.mean(xf * xf, axis=-1, keepdims=True) + EPS)
        xh = xf * r
        dx = r_ref[...]
        for dy_ref, g_ref, dg_ref in zip(dy_refs, g_refs, dg_refs):
            dy = dy_ref[...].astype(F32)
            dyg = dy * g_ref[...]
            dx = dx + r * (dyg - xh * jnp.mean(dyg * xh, axis=-1, keepdims=True))
            part = jnp.sum(dy * xh, axis=0, keepdims=True)

            @pl.when(i == 0)
            def _():
                dg_ref[...] = part

            @pl.when(i > 0)
            def _():
                dg_ref[...] += part
        dx_ref[...] = dx

    row = pl.BlockSpec((tt, D), lambda i: (i, 0))
    gsp = pl.BlockSpec((1, D), lambda i: (0, 0))
    outs = pl.pallas_call(
        body, name=name, grid=(T // tt,),
        in_specs=[row, row] + [row] * nb + [gsp] * nb,
        out_specs=[row] + [gsp] * nb,
        out_shape=[jax.ShapeDtypeStruct((T, D), F32)] + [jax.ShapeDtypeStruct((1, D), F32)] * nb,
        compiler_params=_params("arbitrary"),
    )(x, dres, *[b[0] for b in branches], *[b[1] for b in branches])
    return outs[0], outs[1:]


def loss_head(x, target, gain):
    T, D = x.shape
    tt = _rows(T, 256)

    def body(x_ref, t_ref, g_ref, dx_ref, dg_ref, loss_ref):
        i = pl.program_id(0)
        xf = x_ref[...]
        g = g_ref[...]
        r = lax.rsqrt(jnp.mean(xf * xf, axis=-1, keepdims=True) + EPS)
        xh = xf * r
        e = xh * g - t_ref[...]
        lpart = 0.5 * jnp.sum(jnp.sum(e * e, axis=1, keepdims=True), axis=0, keepdims=True) / D
        dy = e / D
        dyg = dy * g
        dx_ref[...] = r * (dyg - xh * jnp.mean(dyg * xh, axis=-1, keepdims=True))
        gpart = jnp.sum(dy * xh, axis=0, keepdims=True)
        lrow = jnp.broadcast_to(lpart, (1, 128))

        @pl.when(i == 0)
        def _():
            dg_ref[...] = gpart
            loss_ref[...] = lrow

        @pl.when(i > 0)
        def _():
            dg_ref[...] += gpart
            loss_ref[...] += lrow

    row = pl.BlockSpec((tt, D), lambda i: (i, 0))
    gsp = pl.BlockSpec((1, D), lambda i: (0, 0))
    return pl.pallas_call(
        body, name="loss_head", grid=(T // tt,),
        in_specs=[row, row, gsp],
        out_specs=[row, gsp, pl.BlockSpec((1, 128), lambda i: (0, 0))],
        out_shape=[jax.ShapeDtypeStruct((T, D), F32), jax.ShapeDtypeStruct((1, D), F32),
                   jax.ShapeDtypeStruct((1, 128), F32)],
        compiler_params=_params("arbitrary"),
    )(x, target, gain)


def _shift_down(u, prev8, first, k):
    rolled = pltpu.roll(u, k, 0)
    rid = lax.broadcasted_iota(jnp.int32, u.shape, 0)
    halo = jnp.where(first, 0.0, prev8)
    out = rolled
    for j in range(k):
        out = jnp.where(rid == j, halo[8 - k + j:8 - k + j + 1, :], out)
    return out


def _conv3(u, prev8, first, w, b):
    return (_shift_down(u, prev8, first, 2) * w[0:1, :] + _shift_down(u, prev8, first, 1) * w[1:2, :]
            + u * w[2:3, :] + b)


def ffn_up_act(xn, w_up, w, b, name, ride=None):
    T, K = xn.shape
    tt = _rows(T, 512)
    nj = D_FF // FF_TC

    def body(x_ref, wu_ref, w_ref, b_ref, u_ref, o_ref, tail):
        first = pl.program_id(1) == 0
        u = _dot(x_ref[...], wu_ref[...])
        u_ref[...] = u
        c = _conv3(u, tail[...], first, w_ref[...], b_ref[...])
        tail[...] = u[tt - 8:, :]
        cg, cv = c[:, :FF_TC], c[:, FF_TC:]
        o_ref[...] = (cg * _sigmoid_tanh(cg) * cv).astype(o_ref.dtype)

    return _call(
        body, name, (nj, T // tt),
        [pl.BlockSpec((tt, K), lambda j, i: (i, 0)),
         pl.BlockSpec((K, 2 * FF_TC), lambda j, i: (0, j)),
         pl.BlockSpec((3, 2 * FF_TC), lambda j, i: (0, j)),
         pl.BlockSpec((1, 2 * FF_TC), lambda j, i: (0, j))],
        [pl.BlockSpec((tt, 2 * FF_TC), lambda j, i: (i, j)), pl.BlockSpec((tt, FF_TC), lambda j, i: (i, j))],
        [jax.ShapeDtypeStruct((T, 2 * D_FF), F32), jax.ShapeDtypeStruct((T, D_FF), MM_DTYPE)],
        [pltpu.VMEM((8, 2 * FF_TC), F32)], ("parallel", "arbitrary"), (xn, w_up, w, b), ride)


def conv_act_bwd(u, da, w, b, name, ride=None):
    T = u.shape[0]
    tt = _rows(T, 512)
    nt = T // tt
    nj = D_FF // FF_TC
    te = tt + 8

    def body(u_ref, p_ref, n_ref, da_ref, dan_ref, w_ref, b_ref, du_ref, dw_ref, db_ref):
        i = pl.program_id(1)
        first = i == 0
        last = i == nt - 1
        w = w_ref[...]
        ue = jnp.concatenate([u_ref[...], n_ref[...]], axis=0)
        dae = jnp.concatenate([da_ref[...], jnp.where(last, 0.0, dan_ref[...])], axis=0)
        um2 = _shift_down(ue, p_ref[...], first, 2)
        um1 = _shift_down(ue, p_ref[...], first, 1)
        c = um2 * w[0:1, :] + um1 * w[1:2, :] + ue * w[2:3, :] + b_ref[...]
        cg, cv = c[:, :FF_TC], c[:, FF_TC:]
        s = _sigmoid_tanh(cg)
        dcg = dae * cv * (s * (1.0 + cg * (1.0 - s)))
        dcv = dae * (cg * s)
        dc = jnp.concatenate([dcg, dcv], axis=1)
        du = (dc * w[2:3, :] + pltpu.roll(dc, te - 1, 0) * w[1:2, :] + pltpu.roll(dc, te - 2, 0) * w[0:1, :])
        du_ref[...] = du[:tt, :].astype(du_ref.dtype)
        dcm = dc[:tt, :]
        dwp = jnp.concatenate([jnp.sum(dcm * um2[:tt, :], axis=0, keepdims=True),
                               jnp.sum(dcm * um1[:tt, :], axis=0, keepdims=True),
                               jnp.sum(dcm * ue[:tt, :], axis=0, keepdims=True)], axis=0)
        dbp = jnp.sum(dcm, axis=0, keepdims=True)

        @pl.when(first)
        def _():
            dw_ref[...] = dwp
            db_ref[...] = dbp

        @pl.when(i > 0)
        def _():
            dw_ref[...] += dwp
            db_ref[...] += dbp

    nb8 = T // 8
    return _call(
        body, name, (nj, nt),
        [pl.BlockSpec((tt, 2 * FF_TC), lambda j, i: (i, j)),
         pl.BlockSpec((8, 2 * FF_TC), lambda j, i: (jnp.maximum(i * (tt // 8) - 1, 0), j)),
         pl.BlockSpec((8, 2 * FF_TC), lambda j, i: (jnp.minimum((i + 1) * (tt // 8), nb8 - 1), j)),
         pl.BlockSpec((tt, FF_TC), lambda j, i: (i, j)),
         pl.BlockSpec((8, FF_TC), lambda j, i: (jnp.minimum((i + 1) * (tt // 8), nb8 - 1), j)),
         pl.BlockSpec((3, 2 * FF_TC), lambda j, i: (0, j)),
         pl.BlockSpec((1, 2 * FF_TC), lambda j, i: (0, j))],
        [pl.BlockSpec((tt, 2 * FF_TC), lambda j, i: (i, j)),
         pl.BlockSpec((3, 2 * FF_TC), lambda j, i: (0, j)),
         pl.BlockSpec((1, 2 * FF_TC), lambda j, i: (0, j))],
        [jax.ShapeDtypeStruct((T, 2 * D_FF), MM_DTYPE), jax.ShapeDtypeStruct((3, 2 * D_FF), F32),
         jax.ShapeDtypeStruct((1, 2 * D_FF), F32)],
        [], ("parallel", "arbitrary"), (u, u, u, da, da, w, b), ride)


def _interleave(a):
    lead = a.shape[:-1]
    nj = D_FF // FF_TC
    return jnp.swapaxes(a.reshape(*lead, 2, nj, FF_TC), -3, -2).reshape(*lead, 2 * D_FF)


def _deinterleave(a):
    lead = a.shape[:-1]
    nj = D_FF // FF_TC
    return jnp.swapaxes(a.reshape(*lead, nj, 2, FF_TC), -3, -2).reshape(*lead, 2 * D_FF)


A_GC = 1
A_TB = A_GC * A_CHUNK


def gate_prep(z, bias128):
    T = z.shape[0]
    tt = _rows(T, 512)

    def body(z_ref, b_ref, gc_ref, gr_ref):
        pre = z_ref[...] + b_ref[...]
        sc = SOFTCAP * jnp.tanh(pre / SOFTCAP)
        lf = jnp.minimum(sc, 0.0) - jnp.log(1.0 + jnp.exp(-jnp.abs(sc)))
        col = lax.broadcasted_iota(jnp.int32, pre.shape, 1)
        isf = jnp.logical_and(col >= A_HEADS, col < 2 * A_HEADS)
        r = lax.broadcasted_iota(jnp.int32, (tt, tt), 0)
        c = lax.broadcasted_iota(jnp.int32, (tt, tt), 1)
        bits = A_CHUNK.bit_length() - 1
        tri = jnp.logical_and(jnp.right_shift(r, bits) == jnp.right_shift(c, bits), c <= r).astype(F32)
        bcum = jnp.dot(tri, jnp.where(isf, lf, 0.0), precision=HI, preferred_element_type=F32)
        g = jnp.where(col < A_HEADS, sc, jnp.where(isf, bcum, 0.0))
        gc_ref[...] = g
        for s in range(tt // 128):
            gr_ref[s] = g[s * 128:(s + 1) * 128, :].T[0:8, :]

    return pl.pallas_call(
        body, name="gate_prep", grid=(T // tt,),
        in_specs=[pl.BlockSpec((tt, 128), lambda i: (i, GATE_COL // 128)),
                  pl.BlockSpec((1, 128), lambda i: (0, 0))],
        out_specs=[pl.BlockSpec((tt, 128), lambda i: (i, 0)),
                   pl.BlockSpec((tt // 128, 8, 128), lambda i: (i, 0, 0))],
        out_shape=[jax.ShapeDtypeStruct((T, 128), F32), jax.ShapeDtypeStruct((T // 128, 8, 128), F32)],
        compiler_params=_params("parallel"),
    )(z, bias128)


def _chunk_decay(A, qh, bc, br, lir, n, m, causal):
    logD = jnp.where(causal, bc - br + lir, -jnp.inf)
    m_inter = bc + m
    m_t = jnp.maximum(m_inter, jnp.max(logD, axis=1, keepdims=True))
    E = jnp.exp(logD - m_t)
    Sm = A * E
    wi = jnp.exp(m_inter - m_t)
    qn = jnp.sum(qh.astype(F32) * n, axis=1, keepdims=True)
    den = jnp.sum(Sm, axis=1, keepdims=True) + wi * qn
    gs = jnp.maximum(jnp.abs(den), jnp.exp(-m_t))
    return E, Sm, wi, den, gs, m_t


def _state_weights(bc, lic, br, lir, m):
    bL = bc[A_CHUNK - 1:A_CHUNK, :]
    m_new = jnp.maximum(bL + m, jnp.max(bL - br + lir, axis=1, keepdims=True))
    wk = jnp.exp(bL - bc + lic - m_new)
    decay = jnp.exp(bL + m - m_new)
    return wk, decay, m_new


def _head_slices(h):
    return (slice(h * A_QK, (h + 1) * A_QK), slice(h * A_V, (h + 1) * A_V))


def mlstm_fwd(z, gcol, grow, hng, ride=None):
    T = z.shape[0]
    NC = T // A_CHUNK
    scale = A_QK ** -0.5

    def body(q_ref, k_ref, v_ref, o_ref, gc_ref, gr_ref, hng_ref, hg_ref, Cs_ref, ns_ref, ms_ref,
             C_sc, n_sc, m_sc):
        @pl.when(pl.program_id(0) == 0)
        def _():
            C_sc[...] = jnp.zeros_like(C_sc)
            n_sc[...] = jnp.zeros_like(n_sc)
            m_sc[...] = jnp.zeros_like(m_sc)

        ri = lax.broadcasted_iota(jnp.int32, (A_CHUNK, A_CHUNK), 0)
        ci = lax.broadcasted_iota(jnp.int32, (A_CHUNK, A_CHUNK), 1)
        causal = ri >= ci
        gr = gr_ref[0]
        for c in range(A_GC):
            rows = slice(c * A_CHUNK, (c + 1) * A_CHUNK)
            gc = gc_ref[rows, :]
            grc = gr[:, c * A_CHUNK:(c + 1) * A_CHUNK]
            for h in range(A_HEADS):
                sk, sv = _head_slices(h)
                qh = (q_ref[rows, sk] * scale).astype(MM_DTYPE)
                kh = k_ref[rows, sk].astype(MM_DTYPE)
                vh = v_ref[rows, sv].astype(MM_DTYPE)
                lic, bc = gc[:, h:h + 1], gc[:, A_HEADS + h:A_HEADS + h + 1]
                lir, br = grc[h:h + 1, :], grc[A_HEADS + h:A_HEADS + h + 1, :]
                C, n, m = C_sc[h], n_sc[h], m_sc[h][:, 0:1]
                Cs_ref[c, h] = C
                ns_ref[c, h] = n
                ms_ref[c, h] = m_sc[h]
                _, Sm, wi, _, gs, _ = _chunk_decay(_dot_nt(qh, kh), qh, bc, br, lir, n, m, causal)
                hh = (_dot(Sm, vh) + wi * _dot(qh, C)) / gs
                hn = hh * lax.rsqrt(jnp.mean(hh * hh, axis=1, keepdims=True) + EPS) * hng_ref[:, sv]
                hg_ref[rows, sv] = (hn * _sigmoid(o_ref[rows, sv])).astype(hg_ref.dtype)
                wk, decay, m_new = _state_weights(bc, lic, br, lir, m)
                kw = kh.astype(F32) * wk
                C_sc[h] = decay * C + _dot_tn(kw, vh)
                n_sc[h] = decay * n + jnp.sum(kw, axis=0, keepdims=True)
                m_sc[h] = jnp.broadcast_to(m_new, (1, 128))

    tok = lambda w, cb: pl.BlockSpec((A_TB, w), lambda i: (i, cb))
    return _call(
        body, "mlstm_fwd", (NC // A_GC,),
        [tok(512, 0), tok(512, 1), tok(1024, 1), tok(1024, 2),
         pl.BlockSpec((A_TB, 128), lambda i: (i, 0)),
         pl.BlockSpec((1, 8, 128), lambda i: (i, 0, 0)),
         pl.BlockSpec((1, 1024), lambda i: (0, 0))],
        [pl.BlockSpec((A_TB, 1024), lambda i: (i, 0)),
         pl.BlockSpec((A_GC, A_HEADS, A_QK, A_V), lambda i: (i, 0, 0, 0)),
         pl.BlockSpec((A_GC, A_HEADS, 1, 128), lambda i: (i, 0, 0, 0)),
         pl.BlockSpec((A_GC, A_HEADS, 1, 128), lambda i: (i, 0, 0, 0))],
        [jax.ShapeDtypeStruct((T, 1024), MM_DTYPE),
         jax.ShapeDtypeStruct((NC, A_HEADS, A_QK, A_V), F32),
         jax.ShapeDtypeStruct((NC, A_HEADS, 1, 128), F32),
         jax.ShapeDtypeStruct((NC, A_HEADS, 1, 128), F32)],
        [pltpu.VMEM((A_HEADS, A_QK, A_V), F32), pltpu.VMEM((A_HEADS, 1, 128), F32),
         pltpu.VMEM((A_HEADS, 1, 128), F32)],
        ("arbitrary",), (z, z, z, z, gcol, grow, hng), ride)


def mlstm_bwd(z, gcol, grow, hng, bias128, Cs, ns, ms, dhg, ride=None):
    T = z.shape[0]
    NC = T // A_CHUNK
    nsteps = NC // A_GC
    scale = A_QK ** -0.5

    def body(q_ref, k_ref, v_ref, o_ref, zg_ref, gc_ref, gr_ref, hng_ref, b_ref, Cs_ref, ns_ref, ms_ref,
             dhg_ref, dz_ref, dgn_ref, dbif_ref, dC_sc, dn_sc):
        @pl.when(pl.program_id(0) == 0)
        def _():
            dC_sc[...] = jnp.zeros_like(dC_sc)
            dn_sc[...] = jnp.zeros_like(dn_sc)
            dgn_ref[...] = jnp.zeros_like(dgn_ref)
            dbif_ref[...] = jnp.zeros_like(dbif_ref)

        ri = lax.broadcasted_iota(jnp.int32, (A_CHUNK, A_CHUNK), 0)
        ci = lax.broadcasted_iota(jnp.int32, (A_CHUNK, A_CHUNK), 1)
        causal = ri >= ci
        upper = (ci >= ri).astype(F32)
        rid = lax.broadcasted_iota(jnp.int32, (A_CHUNK, 1), 0)
        col = lax.broadcasted_iota(jnp.int32, (A_CHUNK, 128), 1)
        gr = gr_ref[0]
        for c in reversed(range(A_GC)):
            rows = slice(c * A_CHUNK, (c + 1) * A_CHUNK)
            gc = gc_ref[rows, :]
            grc = gr[:, c * A_CHUNK:(c + 1) * A_CHUNK]
            dG = jnp.zeros((A_CHUNK, 128), F32)
            hs = []
            for h in range(A_HEADS):
                sk, sv = _head_slices(h)
                s = dict(sk=sk, sv=sv, qh=(q_ref[rows, sk] * scale).astype(MM_DTYPE),
                         kh=k_ref[rows, sk].astype(MM_DTYPE), vh=v_ref[rows, sv].astype(MM_DTYPE),
                         lic=gc[:, h:h + 1], bc=gc[:, A_HEADS + h:A_HEADS + h + 1],
                         lir=grc[h:h + 1, :], br=grc[A_HEADS + h:A_HEADS + h + 1, :],
                         C=Cs_ref[c, h], n=ns_ref[c, h], m=ms_ref[c, h][:, 0:1], dC=dC_sc[h], dn=dn_sc[h])
                s['qf'], s['kf'] = s['qh'].astype(F32), s['kh'].astype(F32)
                s['wk'], s['decay'], _ = _state_weights(s['bc'], s['lic'], s['br'], s['lir'], s['m'])
                hs.append(s)
            for s in hs:
                s['A'] = _dot_nt(s['qh'], s['kh'])
                s['qC'] = _dot(s['qh'], s['C'])
                s['vdC'] = _dot_nt(s['vh'], s['dC'])
                s['kdC'] = _dot(s['kh'], s['dC'])
            for s in hs:
                s['E'], s['Sm'], s['wi'], s['den'], s['gs'], s['m_t'] = _chunk_decay(
                    s['A'], s['qh'], s['bc'], s['br'], s['lir'], s['n'], s['m'], causal)
            for s in hs:
                s['num'] = _dot(s['Sm'], s['vh']) + s['wi'] * s['qC']
            for h, s in enumerate(hs):
                sv, gs = s['sv'], s['gs']
                hh = s['num'] / gs
                r = lax.rsqrt(jnp.mean(hh * hh, axis=1, keepdims=True) + EPS)
                gn = hng_ref[:, sv]
                sg = _sigmoid(o_ref[rows, sv])
                dhg_h = dhg_ref[rows, sv]
                dhn = dhg_h * sg
                dz_ref[rows, 2048 + h * A_V:2048 + (h + 1) * A_V] = (
                    dhg_h * (hh * r * gn) * sg * (1.0 - sg)).astype(dz_ref.dtype)
                dgn_ref[:, sv] += jnp.sum(dhn * hh * r, axis=0, keepdims=True)
                dyg = dhn * gn
                dh = r * dyg - hh * (r * r * r) * jnp.mean(dyg * hh, axis=1, keepdims=True)
                s['dnum'] = dh / gs
                live = (jnp.abs(s['den']) > jnp.exp(-s['m_t'])).astype(F32)
                s['dden'] = -jnp.sum(dh * hh, axis=1, keepdims=True) / gs * jnp.sign(s['den']) * live
            for s in hs:
                s['dnv'] = _dot_nt(s['dnum'], s['vh'])
                s['dnC'] = _dot_nt(s['dnum'], s['C'])
            for s in hs:
                s['dSE'] = jnp.where(causal, s['dnv'] + s['dden'], 0.0) * s['E']
            for s in hs:
                s['dq'] = _dot(s['dSE'], s['kh']) + s['wi'] * (s['dnC'] + s['dden'] * s['n'])
                s['dk_inter'] = s['wk'] * (s['vdC'] + s['dn'])
                s['dk'] = _dot_tn(s['dSE'], s['qh']) + s['dk_inter']
                s['dv'] = _dot_tn(s['Sm'], s['dnum']) + s['wk'] * s['kdC']
                s['dCq'] = _dot_tn(s['qf'] * s['wi'], s['dnum'])
            for h, s in enumerate(hs):
                dq, dk, qf, kf, dC, dn = s['dq'], s['dk'], s['qf'], s['kf'], s['dC'], s['dn']
                dz_ref[rows, s['sk']] = (dq * scale).astype(dz_ref.dtype)
                dz_ref[rows, 512 + h * A_QK:512 + (h + 1) * A_QK] = dk.astype(dz_ref.dtype)
                dz_ref[rows, 1024 + h * A_V:1024 + (h + 1) * A_V] = s['dv'].astype(dz_ref.dtype)
                dli = jnp.sum(kf * dk, axis=1, keepdims=True)
                db = jnp.sum(qf * dq, axis=1, keepdims=True) - dli
                usum = jnp.sum(jnp.sum(kf * s['dk_inter'], axis=1, keepdims=True), axis=0, keepdims=True)
                ddecay = (jnp.sum(jnp.sum(dC * s['C'], axis=1, keepdims=True), axis=0, keepdims=True)
                          + jnp.sum(dn * s['n'], axis=1, keepdims=True))
                db = db + jnp.where(rid == A_CHUNK - 1, usum + ddecay * s['decay'], 0.0)
                dG = dG + jnp.where(col == h, dli, 0.0) + jnp.where(col == A_HEADS + h, db, 0.0)
                dC_sc[h] = s['decay'] * dC + s['dCq']
                dn_sc[h] = s['decay'] * dn + jnp.sum(qf * (s['wi'] * s['dden']), axis=0, keepdims=True)
            dlf = jnp.dot(upper, dG, precision=HI, preferred_element_type=F32)
            pre = zg_ref[rows, :] + b_ref[...]
            th = jnp.tanh(pre / SOFTCAP)
            dcap = 1.0 - th * th
            dpre = jnp.where(col < A_HEADS, dG * dcap,
                             jnp.where(col < 2 * A_HEADS, dlf * _sigmoid(-SOFTCAP * th) * dcap, 0.0))
            dz_ref[rows, GATE_COL:GATE_COL + 128] = dpre.astype(dz_ref.dtype)
            dbif_ref[...] += jnp.sum(dpre, axis=0, keepdims=True)

    rev = lambda i: nsteps - 1 - i
    tok = lambda w, cb: pl.BlockSpec((A_TB, w), lambda i: (rev(i), cb))
    st = lambda a, b: pl.BlockSpec((A_GC, A_HEADS, a, b), lambda i: (rev(i), 0, 0, 0))
    return _call(
        body, "mlstm_bwd", (nsteps,),
        [tok(512, 0), tok(512, 1), tok(1024, 1), tok(1024, 2), tok(128, GATE_COL // 128),
         pl.BlockSpec((A_TB, 128), lambda i: (rev(i), 0)),
         pl.BlockSpec((1, 8, 128), lambda i: (rev(i), 0, 0)),
         pl.BlockSpec((1, 1024), lambda i: (0, 0)),
         pl.BlockSpec((1, 128), lambda i: (0, 0)),
         st(A_QK, A_V), st(1, 128), st(1, 128),
         pl.BlockSpec((A_TB, 1024), lambda i: (rev(i), 0))],
        [pl.BlockSpec((A_TB, A_IN_PAD), lambda i: (rev(i), 0)),
         pl.BlockSpec((1, 1024), lambda i: (0, 0)),
         pl.BlockSpec((1, 128), lambda i: (0, 0))],
        [jax.ShapeDtypeStruct((T, A_IN_PAD), MM_DTYPE), jax.ShapeDtypeStruct((1, 1024), F32),
         jax.ShapeDtypeStruct((1, 128), F32)],
        [pltpu.VMEM((A_HEADS, A_QK, A_V), F32), pltpu.VMEM((A_HEADS, 1, 128), F32)],
        ("arbitrary",), (z, z, z, z, z, gcol, grow, hng, bias128, Cs, ns, ms, dhg), ride)


def _t5_bucket(dist):
    max_exact = REL_BUCKETS // 2
    d = np.maximum(dist, 0)
    log_ratio = np.log(np.maximum(d, 1) / max_exact) / math.log(REL_MAX_DIST / max_exact)
    large = np.minimum(max_exact + (log_ratio * (REL_BUCKETS - max_exact)).astype(np.int64), REL_BUCKETS - 1)
    return np.where(d < max_exact, d, large).astype(np.int32)


def _group_bucket(g):
    delta = B_BLOCK + np.arange(B_BLOCK)[:, None] - np.arange(2 * B_BLOCK)[None, :]
    return _t5_bucket(delta * DILATIONS[g])


def _band_mask(n):
    ri = lax.broadcasted_iota(jnp.int32, (B_BLOCK, 2 * B_BLOCK), 0)
    ci = lax.broadcasted_iota(jnp.int32, (B_BLOCK, 2 * B_BLOCK), 1)
    band = jnp.logical_and(ci >= ri, ci <= ri + B_BLOCK)
    return jnp.logical_and(band, jnp.logical_or(ci >= B_BLOCK, n > 0))


def _both(p_ref, c_ref, sl):
    return jnp.concatenate([p_ref[:, sl], c_ref[:, sl]], axis=0)


def _scores(qh, kh, bias_h, valid):
    return jnp.where(valid, _dot_nt(qh, kh) * (B_DH ** -0.5) + bias_h, -jnp.inf)


def _attn_specs():
    wide = pl.BlockSpec((B_BLOCK, 1024), lambda r, n: (n, r))
    prev = pl.BlockSpec((B_BLOCK, 1024), lambda r, n: (jnp.maximum(n - 1, 0), r))
    narrow = pl.BlockSpec((B_BLOCK, 128), lambda r, n: (n, r))
    bias = pl.BlockSpec((B_HEADS, B_BLOCK, 2 * B_BLOCK), lambda r, n: (0, 0, 0))
    return wide, prev, narrow, bias


def _to_view(read_chunk, sc, o_ref, dil, nc, tt):
    for c in range(nc):
        sc[c] = read_chunk(c)
    for r in range(dil):
        for c in range(nc):
            lo = (r * nc + c) * 128
            o_ref[:, lo:lo + 128] = sc[c, pl.ds(r, tt // dil, stride=dil), :].astype(o_ref.dtype)


def _from_view(read_view, sc, dil, nc, tt):
    for r in range(dil):
        for c in range(nc):
            sc[c, pl.ds(r, tt // dil, stride=dil), :] = read_view((r * nc + c) * 128).astype(F32)


def attn_fwd(qv, kvw, vvw, bias, g):
    dil = DILATIONS[g]
    Tv = qv.shape[0]
    nb = Tv // B_BLOCK
    wide, prev, narrow, bsp = _attn_specs()

    def body(q_ref, kp_ref, kc_ref, vp_ref, vc_ref, b_ref, o_ref, lse_ref):
        valid = _band_mask(pl.program_id(1))
        lse_ref[...] = jnp.zeros_like(lse_ref)
        heads = [slice(h * B_DH, (h + 1) * B_DH) for h in range(B_HEADS)]
        S = [_scores(q_ref[:, sl], _both(kp_ref, kc_ref, sl), b_ref[h], valid) for h, sl in enumerate(heads)]
        P, L = [], []
        for h in range(B_HEADS):
            m = jnp.max(S[h], axis=1, keepdims=True)
            p = jnp.exp(S[h] - m)
            l = jnp.sum(p, axis=1, keepdims=True)
            lse_ref[:, h:h + 1] = m + jnp.log(l)
            P.append(p.astype(MM_DTYPE))
            L.append(l)
        for h, sl in enumerate(heads):
            o_ref[:, sl] = _dot(P[h], _both(vp_ref, vc_ref, sl)) / L[h]

    return pl.pallas_call(
        body, name=f"attn_fwd_g{g}", grid=(dil, nb),
        in_specs=[wide, prev, wide, prev, wide, bsp], out_specs=[wide, narrow],
        out_shape=[jax.ShapeDtypeStruct((Tv, dil * 1024), F32), jax.ShapeDtypeStruct((Tv, dil * 128), F32)],
        compiler_params=_params("parallel", "parallel"),
    )(qv, kvw, kvw, vvw, vvw, bias)


def attn_bwd(qv, kvw, vvw, bias, do_v, lse_v, dl_v, g):
    dil = DILATIONS[g]
    Tv = qv.shape[0]
    nb = Tv // B_BLOCK
    wide, prev, narrow, bsp = _attn_specs()

    def body(q_ref, kp_ref, kc_ref, vp_ref, vc_ref, b_ref, bt_ref, do_ref, lse_ref, dl_ref,
             dq_ref, dkc_ref, dkp_ref, dvc_ref, dvp_ref, db_ref):
        @pl.when(jnp.logical_and(pl.program_id(0) == 0, pl.program_id(1) == 0))
        def _():
            db_ref[...] = jnp.zeros_like(db_ref)

        n = pl.program_id(1)
        valid = _band_mask(n)
        ki = lax.broadcasted_iota(jnp.int32, (2 * B_BLOCK, B_BLOCK), 0)
        qi = lax.broadcasted_iota(jnp.int32, (2 * B_BLOCK, B_BLOCK), 1)
        valid_t = jnp.logical_and(jnp.logical_and(ki >= qi, ki <= qi + B_BLOCK), jnp.logical_or(ki >= B_BLOCK, n > 0))
        lse_t, dl_t = lse_ref[...].T, dl_ref[...].T
        heads = [slice(h * B_DH, (h + 1) * B_DH) for h in range(B_HEADS)]
        scale = B_DH ** -0.5
        PT, DS, DST = [], [], []
        for h, sl in enumerate(heads):
            qh, doh = q_ref[:, sl], do_ref[:, sl].astype(MM_DTYPE)
            kh, vh = _both(kp_ref, kc_ref, sl), _both(vp_ref, vc_ref, sl)
            p = jnp.exp(_scores(qh, kh, b_ref[h], valid) - lse_ref[:, h:h + 1])
            ds = p * (_dot_nt(doh, vh) - dl_ref[:, h:h + 1])
            db_ref[h] += ds
            DS.append((ds * scale).astype(MM_DTYPE))
            pt = jnp.exp(_scores(kh, qh, bt_ref[h], valid_t) - lse_t[h:h + 1, :])
            PT.append(pt.astype(MM_DTYPE))
            DST.append((pt * (_dot_nt(vh, doh) - dl_t[h:h + 1, :]) * scale).astype(MM_DTYPE))
        for h, sl in enumerate(heads):
            qh, doh = q_ref[:, sl], do_ref[:, sl].astype(MM_DTYPE)
            dq_ref[:, sl] = _dot(DS[h], _both(kp_ref, kc_ref, sl)).astype(MM_DTYPE)
            dk = _dot(DST[h], qh).astype(MM_DTYPE)
            dv = _dot(PT[h], doh).astype(MM_DTYPE)
            dkp_ref[:, sl], dkc_ref[:, sl] = dk[:B_BLOCK], dk[B_BLOCK:]
            dvp_ref[:, sl], dvc_ref[:, sl] = dv[:B_BLOCK], dv[B_BLOCK:]

    big = jax.ShapeDtypeStruct((Tv, dil * 1024), MM_DTYPE)
    bsp_t = pl.BlockSpec((B_HEADS, 2 * B_BLOCK, B_BLOCK), lambda r, n: (0, 0, 0))
    return pl.pallas_call(
        body, name=f"attn_bwd_g{g}", grid=(dil, nb),
        in_specs=[wide, prev, wide, prev, wide, bsp, bsp_t, wide, narrow, narrow],
        out_specs=[wide] * 5 + [bsp],
        out_shape=[big] * 5 + [jax.ShapeDtypeStruct((B_HEADS, B_BLOCK, 2 * B_BLOCK), F32)],
        compiler_params=_params("arbitrary", "arbitrary"),
    )(qv, kvw, kvw, vvw, vvw, bias, jnp.swapaxes(bias, 1, 2), do_v, lse_v, dl_v)


def _head_expand():
    e = np.zeros((128, 1024), np.float32)
    for h in range(B_HEADS):
        e[h, h * B_DH:(h + 1) * B_DH] = 1.0
    return e


A_TT = 256


def _view_spec(dil, width):
    return pl.BlockSpec((A_TT // dil, dil * width), lambda i: (i, 0))


def attn_merge(os_v, lses_v):
    T = os_v[0].shape[0]
    tt = A_TT
    expand = jnp.asarray(_head_expand())

    def body(o0, o1, o2, l0, l1, l2, e_ref, ob_ref, of_ref, lse0_ref, lse1_ref, lse2_ref, sc_o, sc_l):
        for gi, (o_ref, l_ref) in enumerate(((o1, l1), (o2, l2))):
            dil = DILATIONS[gi + 1]
            _from_view(lambda lo: o_ref[:, lo:lo + 128], sc_o.at[gi], dil, 8, tt)
            _from_view(lambda lo: l_ref[:, lo:lo + 128], sc_l.at[gi], dil, 1, tt)
        ls = [l0[...], sc_l[0, 0], sc_l[1, 0]]
        m = jnp.maximum(jnp.maximum(ls[0], ls[1]), ls[2])
        ex = [jnp.exp(l - m) for l in ls]
        tot = ex[0] + ex[1] + ex[2]
        lse = m + jnp.log(tot)
        lse0_ref[...] = lse
        _to_view(lambda c: lse, sc_l.at[2], lse1_ref, DILATIONS[1], 1, tt)
        _to_view(lambda c: lse, sc_l.at[2], lse2_ref, DILATIONS[2], 1, tt)
        ws = [e / tot for e in ex]
        for c in range(8):
            cols = slice(c * 128, (c + 1) * 128)
            ecol = e_ref[:, cols]
            spread = [jnp.dot(w, ecol, precision=HI, preferred_element_type=F32) for w in ws]
            out = spread[0] * o0[:, cols] + spread[1] * sc_o[0, c] + spread[2] * sc_o[1, c]
            of_ref[:, cols] = out
            ob_ref[:, cols] = out.astype(ob_ref.dtype)

    wide = pl.BlockSpec((tt, 1024), lambda i: (i, 0))
    return pl.pallas_call(
        body, name="attn_merge", grid=(T // tt,),
        in_specs=[_view_spec(d, 1024) for d in DILATIONS] + [_view_spec(d, 128) for d in DILATIONS]
        + [pl.BlockSpec((128, 1024), lambda i: (0, 0))],
        out_specs=[wide, wide] + [_view_spec(d, 128) for d in DILATIONS],
        out_shape=[jax.ShapeDtypeStruct((T, 1024), MM_DTYPE), jax.ShapeDtypeStruct((T, 1024), F32)]
        + [jax.ShapeDtypeStruct((T // d, d * 128), F32) for d in DILATIONS],
        scratch_shapes=[pltpu.VMEM((2, 8, tt, 128), F32), pltpu.VMEM((3, 1, tt, 128), F32)],
        compiler_params=_params("parallel"),
    )(*os_v, *lses_v, expand)


def attn_prep(datt, out):
    T = datt.shape[0]
    tt = A_TT
    expand_t = jnp.asarray(_head_expand().T.copy())

    def body(d_ref, o_ref, e_ref, do0, do1, do2, dl0, dl1, dl2, sc_d, sc_l):
        delta = jnp.dot(d_ref[...] * o_ref[...], e_ref[...], precision=HI, preferred_element_type=F32)
        do0[...] = d_ref[...].astype(do0.dtype)
        dl0[...] = delta
        for do_ref, dl_ref, dil in ((do1, dl1, DILATIONS[1]), (do2, dl2, DILATIONS[2])):
            _to_view(lambda c: d_ref[:, c * 128:(c + 1) * 128], sc_d, do_ref, dil, 8, tt)
            _to_view(lambda c: delta, sc_l, dl_ref, dil, 1, tt)

    wide = pl.BlockSpec((tt, 1024), lambda i: (i, 0))
    return pl.pallas_call(
        body, name="attn_prep", grid=(T // tt,),
        in_specs=[wide, wide, pl.BlockSpec((1024, 128), lambda i: (0, 0))],
        out_specs=[_view_spec(d, 1024) for d in DILATIONS] + [_view_spec(d, 128) for d in DILATIONS],
        out_shape=[jax.ShapeDtypeStruct((T // d, d * 1024), MM_DTYPE) for d in DILATIONS]
        + [jax.ShapeDtypeStruct((T // d, d * 128), F32) for d in DILATIONS],
        scratch_shapes=[pltpu.VMEM((8, tt, 128), F32), pltpu.VMEM((1, tt, 128), F32)],
        compiler_params=_params("parallel"),
    )(datt, out, expand_t)


def attn_combine(parts):
    T = parts[0][0].shape[0]
    tt = A_TT
    nt = T // tt
    shift = [None] + [B_BLOCK * d // tt for d in DILATIONS[1:]]

    def body(dq0, kc0, vc0, kpa0, kpb0, vpa0, vpb0, dq1, kc1, kp1, vc1, vp1, dq2, kc2, kp2, vc2, vp2,
             dq_ref, dkv_ref, sc):
        i = pl.program_id(0)
        dq_ref[:, 0:1024] = dq0[...].astype(dq_ref.dtype)
        for col, c_ref, pa_ref, pb_ref in ((0, kc0, kpa0, kpb0), (3, vc0, vpa0, vpb0)):
            nxt = jnp.where(i + 1 < nt, pb_ref[:tt // 2, :].astype(F32), 0.0)
            later = jnp.concatenate([pa_ref[tt // 2:, :].astype(F32), nxt], axis=0)
            dkv_ref[:, col * 1024:(col + 1) * 1024] = (c_ref[...].astype(F32) + later).astype(dkv_ref.dtype)
        for g, (dq, kc, kp, vc, vp) in ((1, (dq1, kc1, kp1, vc1, vp1)), (2, (dq2, kc2, kp2, vc2, vp2))):
            dil = DILATIONS[g]
            live = i + shift[g] < nt
            _from_view(lambda lo: dq[:, lo:lo + 128], sc, dil, 8, tt)
            for c in range(8):
                dq_ref[:, g * 1024 + c * 128:g * 1024 + (c + 1) * 128] = sc[c].astype(dq_ref.dtype)
            for col, c_ref, p_ref in ((g, kc, kp), (3 + g, vc, vp)):
                _from_view(lambda lo: c_ref[:, lo:lo + 128].astype(F32)
                           + jnp.where(live, p_ref[:, lo:lo + 128].astype(F32), 0.0), sc, dil, 8, tt)
                for c in range(8):
                    dkv_ref[:, col * 1024 + c * 128:col * 1024 + (c + 1) * 128] = sc[c].astype(dkv_ref.dtype)

    def later_spec(dil, blocks):
        return pl.BlockSpec((tt // dil, dil * 1024), lambda i: (jnp.minimum(i + blocks, nt - 1), 0))

    cur = [_view_spec(d, 1024) for d in DILATIONS]
    in_specs = [cur[0], cur[0], cur[0], cur[0], later_spec(1, 1), cur[0], later_spec(1, 1)]
    args = [parts[0][0], parts[0][1], parts[0][3], parts[0][2], parts[0][2], parts[0][4], parts[0][4]]
    for g in (1, 2):
        in_specs += [cur[g], cur[g], later_spec(DILATIONS[g], shift[g]), cur[g], later_spec(DILATIONS[g], shift[g])]
        args += list(parts[g][:5])
    return pl.pallas_call(
        body, name="attn_combine", grid=(nt,), in_specs=in_specs,
        out_specs=[pl.BlockSpec((tt, 3072), lambda i: (i, 0)), pl.BlockSpec((tt, 6144), lambda i: (i, 0))],
        out_shape=[jax.ShapeDtypeStruct((T, 3072), MM_DTYPE), jax.ShapeDtypeStruct((T, 6144), MM_DTYPE)],
        scratch_shapes=[pltpu.VMEM((8, tt, 128), F32)],
        compiler_params=_params("parallel"),
    )(*args)


def adamw(w, g, m, v, name):
    R, C = w.shape
    tr = R if R * C * 4 <= (1 << 20) else _rows(R, max(8, ((1 << 20) // (C * 4)) // 8 * 8))

    def body(w_ref, g_ref, m_ref, v_ref, d_ref, nm_ref, nv_ref):
        gg = g_ref[...]
        nm = ADAM_B1 * m_ref[...] + (1.0 - ADAM_B1) * gg
        nv = ADAM_B2 * v_ref[...] + (1.0 - ADAM_B2) * (gg * gg)
        m_hat = nm / (1.0 - ADAM_B1 ** ADAM_STEP)
        v_hat = nv / (1.0 - ADAM_B2 ** ADAM_STEP)
        d_ref[...] = -ADAM_LR * (m_hat / (jnp.sqrt(v_hat) + ADAM_EPS) + ADAM_WD * w_ref[...])
        nm_ref[...] = nm
        nv_ref[...] = nv

    blk = pl.BlockSpec((tr, C), lambda i: (i, 0))
    sds = jax.ShapeDtypeStruct((R, C), F32)
    return pl.pallas_call(
        body, name=name, grid=(R // tr,), in_specs=[blk] * 4, out_specs=[blk] * 3, out_shape=[sds] * 3,
        compiler_params=_params("parallel"),
    )(w, g, m, v)


def sum_slots(x, name, out_dtype=F32):
    n, R, C = x.shape
    tr = _rows(R, 256)

    def body(x_ref, o_ref):
        acc = x_ref[0].astype(F32)
        for s in range(1, n):
            acc = acc + x_ref[s].astype(F32)
        o_ref[...] = acc.astype(out_dtype)

    return pl.pallas_call(
        body, name=name, grid=(R // tr,),
        in_specs=[pl.BlockSpec((n, tr, C), lambda i: (0, i, 0))],
        out_specs=pl.BlockSpec((tr, C), lambda i: (i, 0)),
        out_shape=jax.ShapeDtypeStruct((R, C), out_dtype),
        compiler_params=_params("parallel"),
    )(x)


_ANY = pl.BlockSpec(memory_space=pl.ANY)
GROUP_ALL = ([(0, 0, 1), (0, 1, 0), (0, 1, 1), (1, 0, 0), (1, 0, 1), (1, 1, 0), (1, 1, 1)],
             lambda d: 4 * d[0] + 2 * d[1] + d[2])
GROUP_CHIPS = ([(0, 1, 0), (1, 0, 0), (1, 1, 0)], lambda d: 2 * d[0] + d[1])
GROUP_SIBLING = ([(0, 0, 1)], lambda d: d[2])


def _me():
    return lax.axis_index("x"), lax.axis_index("y"), lax.axis_index("c")


def _peer(me, flip):
    return tuple(1 - a if f else a for a, f in zip(me, flip))


class Exchange:
    def __init__(self, x, group, scatter):
        self.flips, self.slot = group
        self.scatter = scatter
        self.n = len(self.flips) + 1
        self.out_shape = jax.ShapeDtypeStruct((self.n,) + x.shape[-2:], x.dtype)
        self.scratch = [pltpu.SemaphoreType.DMA((self.n - 1,)), pltpu.SemaphoreType.DMA((self.n - 1,)),
                        pltpu.SemaphoreType.DMA]

    def _copies(self, x_ref, o_ref, send_sems, recv_sems, local_sem, arrivals):
        me = _me()
        slot = self.slot
        mine = pltpu.make_async_copy(x_ref.at[slot(me)] if self.scatter else x_ref, o_ref.at[slot(me)], local_sem)
        sends, landed = [], []
        for k, flip in enumerate(self.flips):
            peer = _peer(me, flip)
            sends.append(pltpu.make_async_remote_copy(
                src_ref=x_ref.at[slot(peer)] if self.scatter else x_ref, dst_ref=o_ref.at[slot(me)],
                send_sem=send_sems.at[k], recv_sem=recv_sems.at[k], device_id=peer, device_id_type=MESH_ID))
            if arrivals:
                landed.append(pltpu.make_async_remote_copy(
                    src_ref=o_ref.at[slot(me)], dst_ref=o_ref.at[slot(peer)], send_sem=send_sems.at[k],
                    recv_sem=recv_sems.at[k], device_id=peer, device_id_type=MESH_ID))
        return mine, sends, landed

    def start(self, *refs):
        mine, sends, _ = self._copies(*refs, arrivals=False)
        mine.start()
        for cp in sends:
            cp.start()

    def wait(self, *refs):
        mine, sends, arrivals = self._copies(*refs, arrivals=True)
        for cp in arrivals:
            cp.wait_recv()
        for cp in sends:
            cp.wait_send()
        mine.wait()

    def __call__(self, x, name):
        def body(*refs):
            self.start(*refs)
            self.wait(*refs)

        return pl.pallas_call(body, name=name, in_specs=[_ANY], out_specs=_ANY, out_shape=self.out_shape,
                              scratch_shapes=self.scratch)(x)


def group_gather(x, name, group):
    return Exchange(x, group, scatter=False)(x, name)


def group_scatter(x, name, group):
    return Exchange(x, group, scatter=True)(x, name)


def _call(body, name, grid, in_specs, out_specs, out_shape, scratch, semantics, args, ride=None):
    if ride is None:
        return pl.pallas_call(body, name=name, grid=grid, in_specs=in_specs, out_specs=out_specs,
                              out_shape=out_shape, scratch_shapes=scratch,
                              compiler_params=_params(*semantics))(*args)
    x, exch = ride
    n_in, n_out, n_scr = len(in_specs), len(out_specs), len(scratch)

    def at_step(pick):
        hit = None
        for axis, size in enumerate(grid):
            here = pl.program_id(axis) == pick(size)
            hit = here if hit is None else jnp.logical_and(hit, here)
        return hit

    def riding(*refs):
        ins, x_ref = refs[:n_in], refs[n_in]
        outs, o_ref = refs[n_in + 1:n_in + 1 + n_out], refs[n_in + 1 + n_out]
        scr, sems = refs[n_in + 2 + n_out:n_in + 2 + n_out + n_scr], refs[n_in + 2 + n_out + n_scr:]

        @pl.when(at_step(lambda size: 0))
        def _():
            exch.start(x_ref, o_ref, *sems)

        body(*ins, *outs, *scr)

        @pl.when(at_step(lambda size: size - 1))
        def _():
            exch.wait(x_ref, o_ref, *sems)

    return pl.pallas_call(
        riding, name=name, grid=grid, in_specs=list(in_specs) + [_ANY], out_specs=list(out_specs) + [_ANY],
        out_shape=list(out_shape) + [exch.out_shape], scratch_shapes=list(scratch) + exch.scratch,
        compiler_params=_params(*(["arbitrary"] * len(grid))))(*args, x)


WEIGHTS = ['a_norm_g', 'a_w_in', 'a_b_if', 'a_hnorm_g', 'a_w_out', 'kv_norm_g', 'w_kv', 'b_norm_g', 'b_w_q',
           'b_w_out', 'rel_bias', 'f_norm_g', 'f_w_up', 'f_conv_w', 'f_conv_b', 'f_w_down', 'final_norm_g']
SHARD_AXIS = {'a_norm_g': 1, 'a_w_in': 2, 'a_b_if': None, 'a_hnorm_g': 2, 'a_w_out': 1, 'kv_norm_g': None,
              'w_kv': 1, 'b_norm_g': None, 'b_w_q': 2, 'b_w_out': 1, 'rel_bias': None, 'f_norm_g': None,
              'f_w_up': 2, 'f_conv_w': 2, 'f_conv_b': None, 'f_w_down': 1, 'final_norm_g': None}
BIG = ['a_w_in', 'a_w_out', 'w_kv', 'b_w_q', 'b_w_out', 'f_w_up', 'f_w_down']
SMALL = [n for n in WEIGHTS if n not in BIG]
LANES = 1024
PIECES = {'a_w_in': ('a_w_in', None, 2), 'a_w_out': ('a_w_out', None, 1), 'f_w_up0': ('f_w_up', 0, 1),
          'f_w_down0': ('f_w_down', 0, 0), 'w_kv': ('w_kv', None, 1), 'b_w_q': ('b_w_q', None, 2),
          'b_w_out': ('b_w_out', None, 1), 'f_w_up1': ('f_w_up', 1, 1), 'f_w_down1': ('f_w_down', 1, 0)}
LATE = ['w_kv', 'b_w_q', 'b_w_out', 'f_w_up1', 'f_w_down1']
WEIGHT_WAVES = {'first': ['a_w_in', 'a_w_out'], 'ffn0': ['f_w_up0', 'f_w_down0'], 'late': LATE}
GRAD_WAVES = {'late': LATE, 'layer0': ['f_w_up0', 'f_w_down0', 'a_w_out'], 'last': ['a_w_in']}


def _piece(arrays, p):
    leaf, layer, _ = PIECES[p]
    return arrays[leaf] if layer is None else arrays[leaf][layer]


class Packer:
    def __init__(self, pieces, shard):
        self.pieces = pieces
        self.shapes = [_piece(shard, p).shape for p in pieces]
        self.sizes = [math.prod(s) // (2 * LANES) for s in self.shapes]
        self.fill = -sum(self.sizes) % 16
        self.rows = sum(self.sizes) + self.fill

    def my_half(self, shard, half):
        both = jnp.concatenate([_piece(shard, p).astype(MM_DTYPE).reshape(2, -1, LANES) for p in self.pieces], axis=1)
        return jnp.pad(lax.dynamic_index_in_dim(both, half, axis=0, keepdims=False), ((0, self.fill), (0, 0)))

    def full_weights(self, gathered):
        g = gathered.reshape(4, 2, self.rows, LANES)
        out, off = {}, 0
        for p, shp, sz in zip(self.pieces, self.shapes, self.sizes):
            out[p] = _full_from_shards(g[:, :, off:off + sz].reshape((4,) + shp), PIECES[p][2])
            off += sz
        return out

    def grad_slots(self, grads):
        parts = [_shards_from_full(grads[p], PIECES[p][2]).reshape(4, 2, -1, LANES).astype(GRAD_WIRE_DTYPE)
                 for p in self.pieces]
        parts.append(jnp.zeros((4, 2, self.fill, LANES), GRAD_WIRE_DTYPE))
        return jnp.concatenate(parts, axis=2).reshape(8, self.rows, LANES)

    def shard_grads(self, both):
        out, off = {}, 0
        for p, shp, sz in zip(self.pieces, self.shapes, self.sizes):
            out[p] = both[:, off:off + sz].reshape(shp).astype(F32)
            off += sz
        return out


class Overlap:
    def __init__(self, shard, half):
        self.shard, self.half = shard, half
        self.weights = {w: Packer(p, shard) for w, p in WEIGHT_WAVES.items()}
        self.grads = {w: Packer(p, shard) for w, p in GRAD_WAVES.items()}
        self.shard_grads = {}

    def gather_ride(self, wave):
        mine = self.weights[wave].my_half(self.shard, self.half)
        return mine, Exchange(mine, GROUP_ALL, scatter=False)

    def gathered(self, wave, slots):
        return self.weights[wave].full_weights(slots)

    def scatter_ride(self, wave, grads):
        slots = self.grads[wave].grad_slots({p: grads.pop(p) for p in GRAD_WAVES[wave]})
        return slots, Exchange(slots, GROUP_ALL, scatter=True)

    def join_ride(self, wave, received):
        reduced = sum_slots(received, f"sum_grads_{wave}", GRAD_WIRE_DTYPE)
        return reduced, Exchange(reduced, GROUP_SIBLING, scatter=False)

    def joined(self, wave, both):
        self.shard_grads.update(self.grads[wave].shard_grads(both))


def _pad_rows(flat, mult):
    n = flat.shape[0]
    per = LANES * mult
    tot = -(-n // per) * per
    return jnp.pad(flat, (0, tot - n)).reshape(tot // LANES, LANES)


def _full_from_shards(sh, axis):
    shp = sh.shape[1:]
    return jnp.moveaxis(sh, 0, axis).reshape(shp[:axis] + (4 * shp[axis],) + shp[axis + 1:])


def _shards_from_full(full, axis):
    shp = full.shape
    return jnp.moveaxis(full.reshape(shp[:axis] + (4, shp[axis] // 4) + shp[axis + 1:]), axis, 0)


def _local_step(x, target, W, overlap=None):
    T = x.shape[0]
    W = dict(W)
    row = lambda a: a.reshape(1, -1).astype(F32)
    w_in = jnp.pad(W['a_w_in'][0], ((0, 0), (0, A_IN_PAD - A_IN)))
    bias128 = jnp.pad(row(W['a_b_if'][0]), ((0, 0), (0, 120)))
    hng = row(W['a_hnorm_g'][0])
    w_up = lambda l: _interleave(W[f'f_w_up{l}'])
    cw = [_interleave(W['f_conv_w'][l].astype(F32)) for l in range(2)]
    cb = [_interleave(row(W['f_conv_b'][l])) for l in range(2)]
    onehots = [(jnp.asarray(_group_bucket(g).reshape(-1, 1)) == jnp.arange(128)[None, :]).astype(F32)
               for g in range(N_GROUPS)]
    rb_t = jnp.pad(W['rel_bias'].astype(F32).T, ((0, 0), (0, 128 - REL_BUCKETS)))
    biases = [mm_nn(rb_t[g * B_HEADS:(g + 1) * B_HEADS], onehots[g].T, f"rel_bias_table_g{g}", exact=True)
              .reshape(B_HEADS, B_BLOCK, 2 * B_BLOCK) for g in range(N_GROUPS)]
    G = {}

    def ffn_fwd(xin, l, ride=None):
        xn, = rms_fwd(xin, [row(W['f_norm_g'][l])], f"ffn{l}_norm")
        u, act, *rode = ffn_up_act(xn, w_up(l), cw[l], cb[l], f"ffn{l}_up_act", ride)
        return mm_nn(act, W[f'f_w_down{l}'], f"ffn{l}_down", res=xin), (xn, u, act), rode

    def ffn_bwd(xin, saved, dout, l, ride=None):
        xn, u, act = saved
        dact = mm_nn(dout, W[f'f_w_down{l}'].T, f"ffn{l}_ddown")
        G[f'f_w_down{l}'] = mm_tn(act, dout, f"ffn{l}_gdown")
        du, gcw, gcb, *rode = conv_act_bwd(u, dact, cw[l], cb[l], f"ffn{l}_dact", ride)
        dxn = mm_nn(du, w_up(l).T, f"ffn{l}_dup")
        G[f'f_w_up{l}'] = _deinterleave(mm_tn(xn, du, f"ffn{l}_gup"))
        dxin, (gn,) = rms_bwd(xin, dout, [(dxn, row(W['f_norm_g'][l]))], f"ffn{l}_dnorm")
        return dxin, _deinterleave(gcw), _deinterleave(gcb), gn, rode

    xn_a, = rms_fwd(x, [row(W['a_norm_g'][0])], "a_norm")
    z = mm_nn(xn_a, w_in, "a_in")
    gcol, grow = gate_prep(z, bias128)
    hg, Cs, ns, ms, *rode = mlstm_fwd(z, gcol, grow, hng, overlap.gather_ride('ffn0') if overlap else None)
    if overlap:
        W.update(overlap.gathered('ffn0', rode[0]))
    x1 = mm_nn(hg, W['a_w_out'][0], "a_out", res=x)
    x2, ffn0, rode = ffn_fwd(x1, 0, overlap.gather_ride('late') if overlap else None)
    if overlap:
        W.update(overlap.gathered('late', rode[0]))
    xn_kv, xn_b = rms_fwd(x2, [row(W['kv_norm_g']), row(W['b_norm_g'][0])], "b_norms")
    gcols = lambda w, c: w[:, c * 1024:(c + 1) * 1024]
    qv = [mm_view(xn_b, gcols(W['b_w_q'][0], g), f"q_proj_g{g}", DILATIONS[g]) for g in range(N_GROUPS)]
    kvw = [mm_view(xn_kv, gcols(W['w_kv'], g), f"k_proj_g{g}", DILATIONS[g]) for g in range(N_GROUPS)]
    vvw = [mm_view(xn_kv, gcols(W['w_kv'], 3 + g), f"v_proj_g{g}", DILATIONS[g]) for g in range(N_GROUPS)]
    os_, lses = zip(*[attn_fwd(qv[g], kvw[g], vvw[g], biases[g], g) for g in range(N_GROUPS)])
    att, att_f, *lse_v = attn_merge(os_, lses)
    x3 = mm_nn(att, W['b_w_out'][0], "b_out", res=x2)
    x4, ffn1, _ = ffn_fwd(x3, 1)
    dx4, g_final, loss = loss_head(x4, target, row(W['final_norm_g']))
    G['final_norm_g'] = g_final.reshape(-1)

    dx3, gcw1, gcb1, gn1, _ = ffn_bwd(x3, ffn1, dx4, 1)
    datt = mm_nn(dx3, W['b_w_out'][0].T, "b_dout")
    G['b_w_out'] = mm_tn(att, dx3, "b_gout")[None]
    prep = attn_prep(datt, att_f)
    do_v, dl_v = prep[:3], prep[3:]
    parts = [attn_bwd(qv[g], kvw[g], vvw[g], biases[g], do_v[g], lse_v[g], dl_v[g], g) for g in range(N_GROUPS)]
    dq_all, dkv = attn_combine(parts)
    grb = []
    for g in range(N_GROUPS):
        gb = mm_nn(parts[g][5].reshape(B_HEADS, -1), onehots[g], f"rel_bias_g{g}", exact=True)
        grb.append(gb[:, :REL_BUCKETS].T)
    G['rel_bias'] = jnp.concatenate(grb, axis=1)
    dxn_b = mm_nn(dq_all, W['b_w_q'][0].T, "q_dproj")
    G['b_w_q'] = mm_tn(xn_b, dq_all, "q_gproj")[None]
    dxn_kv = mm_nn(dkv, W['w_kv'].T, "kv_dproj")
    G['w_kv'] = mm_tn(xn_kv, dkv, "kv_gproj")
    dx2, (g_kvn, g_bn) = rms_bwd(x2, dx3, [(dxn_kv, row(W['kv_norm_g'])), (dxn_b, row(W['b_norm_g'][0]))],
                                 "b_dnorms")
    G['kv_norm_g'] = g_kvn.reshape(-1)
    G['b_norm_g'] = g_bn
    dx1, gcw0, gcb0, gn0, late_slots = ffn_bwd(x1, ffn0, dx2, 0, overlap.scatter_ride('late', G) if overlap else None)
    G['f_conv_w'] = jnp.stack([gcw0, gcw1])
    G['f_conv_b'] = jnp.concatenate([gcb0, gcb1], axis=0)
    G['f_norm_g'] = jnp.concatenate([gn0, gn1], axis=0)
    dhg = mm_nn(dx1, W['a_w_out'][0].T, "a_dout")
    G['a_w_out'] = mm_tn(hg, dx1, "a_gout")[None]
    dz, g_hn, g_bif, *layer0_slots = mlstm_bwd(z, gcol, grow, hng, bias128, Cs, ns, ms, dhg,
                                               overlap.scatter_ride('layer0', G) if overlap else None)
    G['a_hnorm_g'] = g_hn.reshape(1, A_HEADS, A_V)
    G['a_b_if'] = g_bif[:, :2 * A_HEADS]
    if overlap:
        dxn_a, both = mm_nn(dz, w_in.T, "a_din", ride=overlap.join_ride('late', late_slots[0]))
        overlap.joined('late', both)
        g_in, both = mm_tn(xn_a, dz, "a_gin", ride=overlap.join_ride('layer0', layer0_slots[0]))
        overlap.joined('layer0', both)
    else:
        dxn_a = mm_nn(dz, w_in.T, "a_din")
        g_in = mm_tn(xn_a, dz, "a_gin")
    G['a_w_in'] = g_in[:, :A_IN][None]
    grad_x, (g_an,) = rms_bwd(x, dx1, [(dxn_a, row(W['a_norm_g'][0]))], "a_dnorm")
    G['a_norm_g'] = g_an
    return loss, grad_x, G


def kernel(x, a_norm_g, a_w_in, a_b_if, a_hnorm_g, a_w_out, kv_norm_g, w_kv, b_norm_g, b_w_q, b_w_out, rel_bias, f_norm_g, f_w_up, f_conv_w, f_conv_b, f_w_down, final_norm_g, loss_target, m_a_norm_g, m_a_w_in, m_a_b_if, m_a_hnorm_g, m_a_w_out, m_kv_norm_g, m_w_kv, m_b_norm_g, m_b_w_q, m_b_w_out, m_rel_bias, m_f_norm_g, m_f_w_up, m_f_conv_w, m_f_conv_b, m_f_w_down, m_final_norm_g, v_a_norm_g, v_a_w_in, v_a_b_if, v_a_hnorm_g, v_a_w_out, v_kv_norm_g, v_w_kv, v_b_norm_g, v_b_w_q, v_b_w_out, v_rel_bias, v_f_norm_g, v_f_w_up, v_f_conv_w, v_f_conv_b, v_f_w_down, v_final_norm_g):
    given = dict(locals())
    shard = {n: given[n] for n in WEIGHTS}
    mom = {n: given["m_" + n] for n in WEIGHTS}
    var = {n: given["v_" + n] for n in WEIGHTS}
    cx, cy, cc = _me()
    chip = 2 * cx + cy

    overlap = Overlap(shard, cc)
    mine, gather = overlap.gather_ride('first')
    W = overlap.gathered('first', gather(mine, "gather_weights"))
    sharded_small = [n for n in SMALL if SHARD_AXIS[n] is not None]
    ssz = [shard[n].size for n in sharded_small]
    sflat = jnp.concatenate([shard[n].reshape(-1) for n in sharded_small])
    sg = group_gather(_pad_rows(sflat, 8), "gather_small", GROUP_CHIPS).reshape(4, -1)
    off = 0
    for n, sz in zip(sharded_small, ssz):
        W[n] = _full_from_shards(sg[:, off:off + sz].reshape((4,) + shard[n].shape), SHARD_AXIS[n])
        off += sz
    for n in SMALL:
        if SHARD_AXIS[n] is None:
            W[n] = shard[n]

    loss_row, grad_x, G = _local_step(x[0], loss_target[0], W, overlap)

    slots, scatter = overlap.scatter_ride('last', G)
    reduced, join = overlap.join_ride('last', scatter(slots, "scatter_grads"))
    overlap.joined('last', join(reduced, "join_halves"))
    by_piece = overlap.shard_grads
    gsh = {}
    for n in BIG:
        layers = [p for p in PIECES if PIECES[p][0] == n]
        gsh[n] = by_piece[n] if layers == [n] else jnp.stack([by_piece[p] for p in layers])
    small_parts = [loss_row[0, 0:1]] + [G[n].reshape(-1) for n in SMALL]
    small_sz = [p.shape[0] for p in small_parts]
    small = sum_slots(group_gather(_pad_rows(jnp.concatenate(small_parts), 8), "gather_small_grads", GROUP_ALL),
                      "sum_small_grads").reshape(-1)
    loss = small[0]
    off = 1
    for n, sz in zip(SMALL, small_sz[1:]):
        full = small[off:off + sz].reshape(W[n].shape)
        off += sz
        if SHARD_AXIS[n] is None:
            gsh[n] = full
        else:
            gsh[n] = lax.dynamic_index_in_dim(_shards_from_full(full, SHARD_AXIS[n]), chip, 0, keepdims=False)

    delta, new_m, new_v = {}, {}, {}
    for n in WEIGHTS:
        shp = shard[n].shape
        two = lambda a: a.reshape(-1, shp[-1])
        d, nm, nv = adamw(two(shard[n]), two(gsh[n]), two(mom[n]), two(var[n]), f"adamw_{n}")
        delta[n], new_m[n], new_v[n] = d.reshape(shp), nm.reshape(shp), nv.reshape(shp)
    return (loss, grad_x[None], *[gsh[n] for n in WEIGHTS], *[delta[n] for n in WEIGHTS],
            *[new_m[n] for n in WEIGHTS], *[new_v[n] for n in WEIGHTS])
```

```python
import functools
import math

import numpy as np
import jax
import jax.numpy as jnp
from jax import lax
from jax.experimental import pallas as pl
from jax.experimental.pallas import tpu as pltpu

F32 = jnp.float32
BF16 = jnp.bfloat16
MM_DTYPE = jnp.bfloat16
GRAD_WIRE_DTYPE = jnp.bfloat16
HI = lax.Precision.HIGHEST

D_MODEL = 1024
A_HEADS = 4
A_QK = 128
A_V = 256
A_CHUNK = 256
A_IN = 3080
A_IN_PAD = 3200
GATE_COL = 3072
SOFTCAP = 15.0
N_GROUPS = 3
B_HEADS = 16
B_DH = 64
B_BLOCK = 128
DILATIONS = (1, 4, 16)
WINDOWS = (128, 512, 2048)
REL_BUCKETS = 32
REL_MAX_DIST = 2048
D_FF = 2816
FF_TC = 256
EPS = 1e-6
ADAM_LR, ADAM_B1, ADAM_B2, ADAM_EPS, ADAM_WD, ADAM_STEP = 0.001, 0.9, 0.999, 1e-08, 0.01, 10

VMEM_LIMIT = 56 * 1024 * 1024
NT_DIMS = (((1,), (1,)), ((), ()))
TN_DIMS = (((0,), (0,)), ((), ()))
MESH_ID = pl.DeviceIdType.MESH


def _params(*sem):
    return pltpu.CompilerParams(dimension_semantics=sem, vmem_limit_bytes=VMEM_LIMIT)


def _tile(n, cap):
    if n <= cap:
        return n
    best = None
    for t in range(128, cap + 1, 128):
        if n % t == 0:
            best = t
    assert best is not None, (n, cap)
    return best


def _rows(n, cap):
    if n <= cap:
        return n
    for t in range(cap // 8 * 8, 7, -8):
        if n % t == 0:
            return t
    raise ValueError((n, cap))


def _dot(a, b):
    return jnp.dot(a.astype(MM_DTYPE), b.astype(MM_DTYPE), preferred_element_type=F32)


def _dot_nt(a, b):
    return lax.dot_general(a.astype(MM_DTYPE), b.astype(MM_DTYPE), NT_DIMS, preferred_element_type=F32)


def _dot_tn(a, b):
    return lax.dot_general(a.astype(MM_DTYPE), b.astype(MM_DTYPE), TN_DIMS, preferred_element_type=F32)


def _sigmoid(x):
    return 1.0 / (1.0 + jnp.exp(-x))


def _sigmoid_tanh(x):
    return 0.5 * jnp.tanh(0.5 * x) + 0.5


def mm_nn(a, b, name, res=None, out_dtype=F32, exact=False, ride=None):
    M, K = a.shape
    N = b.shape[1]
    tm = _rows(M, 512)

    def footprint(tn):
        return 2 * (tm * K * a.dtype.itemsize + K * tn * b.dtype.itemsize) + 2 * tm * tn * 4 * (1 if res is None else 2)

    budget = 46 * 1024 * 1024
    tn = N if N <= 3328 and footprint(N) <= budget else _tile(N, 1536)
    tk = K if footprint(tn) <= budget else _tile(K, 1536)
    nk = K // tk

    def body(*refs):
        if res is None:
            a_ref, b_ref, o_ref, acc = refs
            r_ref = None
        else:
            a_ref, b_ref, r_ref, o_ref, acc = refs
        if exact:
            p = jnp.dot(a_ref[...], b_ref[...], precision=HI, preferred_element_type=F32)
        else:
            p = _dot(a_ref[...], b_ref[...])

        def finish(total):
            if r_ref is not None:
                total = total + r_ref[...]
            o_ref[...] = total.astype(out_dtype)

        if nk == 1:
            finish(p)
        else:
            k = pl.program_id(2)

            @pl.when(k == 0)
            def _():
                acc[...] = p

            @pl.when(jnp.logical_and(k > 0, k < nk - 1))
            def _():
                acc[...] += p

            @pl.when(k == nk - 1)
            def _():
                finish(acc[...] + p)

    in_specs = [pl.BlockSpec((tm, tk), lambda j, i, k: (i, k)),
                pl.BlockSpec((tk, tn), lambda j, i, k: (k, j))]
    args = [a, b]
    if res is not None:
        in_specs.append(pl.BlockSpec((tm, tn), lambda j, i, k: (i, j)))
        args.append(res)
    acc_shape = (tm, tn) if nk > 1 else (8, 128)
    outs = _call(body, name, (N // tn, M // tm, nk), in_specs, [pl.BlockSpec((tm, tn), lambda j, i, k: (i, j))],
                 [jax.ShapeDtypeStruct((M, N), out_dtype)], [pltpu.VMEM(acc_shape, F32)],
                 ("parallel", "parallel", "arbitrary"), args, ride)
    return outs[0] if ride is None else outs


def mm_view(a, b, name, dil):
    T, K = a.shape
    tm = 512

    def body(a_ref, b_ref, o_ref, sc):
        p = _dot(a_ref[...], b_ref[...])
        if dil == 1:
            o_ref[...] = p.astype(o_ref.dtype)
        else:
            _to_view(lambda c: p[:, c * 128:(c + 1) * 128], sc, o_ref, dil, 8, tm)

    return pl.pallas_call(
        body, name=name, grid=(T // tm,),
        in_specs=[pl.BlockSpec((tm, K), lambda i: (i, 0)), pl.BlockSpec((K, 1024), lambda i: (0, 0))],
        out_specs=pl.BlockSpec((tm // dil, dil * 1024), lambda i: (i, 0)),
        out_shape=jax.ShapeDtypeStruct((T // dil, dil * 1024), MM_DTYPE),
        scratch_shapes=[pltpu.VMEM((8, tm, 128), F32)],
        compiler_params=_params("parallel"),
    )(a, b)


def mm_tn(a, g, name, ride=None):
    T, Ka = a.shape
    N = g.shape[1]
    tka, tt = _tile(Ka, 1536), _rows(T, 1024)
    whole_n = 2 * (tt * tka * a.dtype.itemsize + tt * N * g.dtype.itemsize + tka * N * 4)
    tn = N if N <= 3328 and whole_n <= 46 * 1024 * 1024 else _tile(N, 1536)
    nt = T // tt

    def body(a_ref, g_ref, o_ref):
        t = pl.program_id(2)
        p = _dot_tn(a_ref[...], g_ref[...])

        @pl.when(t == 0)
        def _():
            o_ref[...] = p

        @pl.when(t > 0)
        def _():
            o_ref[...] += p

    outs = _call(body, name, (Ka // tka, N // tn, nt),
                 [pl.BlockSpec((tt, tka), lambda i, j, t: (t, i)), pl.BlockSpec((tt, tn), lambda i, j, t: (t, j))],
                 [pl.BlockSpec((tka, tn), lambda i, j, t: (i, j))], [jax.ShapeDtypeStruct((Ka, N), F32)], [],
                 ("parallel", "parallel", "arbitrary"), (a, g), ride)
    return outs[0] if ride is None else outs


def rms_fwd(x, gains, name):
    T, D = x.shape
    tt = _rows(T, 512)
    ng = len(gains)

    def body(*refs):
        x_ref = refs[0]
        g_refs = refs[1:1 + ng]
        o_refs = refs[1 + ng:]
        xf = x_ref[...]
        y = xf * lax.rsqrt(jnp.mean(xf * xf, axis=-1, keepdims=True) + EPS)
        for g_ref, o_ref in zip(g_refs, o_refs):
            o_ref[...] = (y * g_ref[...]).astype(o_ref.dtype)

    row = pl.BlockSpec((tt, D), lambda i: (i, 0))
    gsp = pl.BlockSpec((1, D), lambda i: (0, 0))
    return pl.pallas_call(
        body, name=name, grid=(T // tt,),
        in_specs=[row] + [gsp] * ng, out_specs=[row] * ng,
        out_shape=[jax.ShapeDtypeStruct((T, D), MM_DTYPE)] * ng,
        compiler_params=_params("parallel"),
    )(x, *gains)


def rms_bwd(x, dres, branches, name):
    T, D = x.shape
    tt = _rows(T, 256)
    nb = len(branches)

    def body(*refs):
        x_ref, r_ref = refs[0], refs[1]
        dy_refs = refs[2:2 + nb]
        g_refs = refs[2 + nb:2 + 2 * nb]
        dx_ref = refs[2 + 2 * nb]
        dg_refs = refs[3 + 2 * nb:]
        i = pl.program_id(0)
        xf = x_ref[...]
        r = lax.rsqrt(jnp.mean(xf * xf, axis=-1, keepdims=True) + EPS)
        xh = xf * r
        dx = r_ref[...]
        for dy_ref, g_ref, dg_ref in zip(dy_refs, g_refs, dg_refs):
            dy = dy_ref[...].astype(F32)
            dyg = dy * g_ref[...]
            dx = dx + r * (dyg - xh * jnp.mean(dyg * xh, axis=-1, keepdims=True))
            part = jnp.sum(dy * xh, axis=0, keepdims=True)

            @pl.when(i == 0)
            def _():
                dg_ref[...] = part

            @pl.when(i > 0)
            def _():
                dg_ref[...] += part
        dx_ref[...] = dx

    row = pl.BlockSpec((tt, D), lambda i: (i, 0))
    gsp = pl.BlockSpec((1, D), lambda i: (0, 0))
    outs = pl.pallas_call(
        body, name=name, grid=(T // tt,),
        in_specs=[row, row] + [row] * nb + [gsp] * nb,
        out_specs=[row] + [gsp] * nb,
        out_shape=[jax.ShapeDtypeStruct((T, D), F32)] + [jax.ShapeDtypeStruct((1, D), F32)] * nb,
        compiler_params=_params("arbitrary"),
    )(x, dres, *[b[0] for b in branches], *[b[1] for b in branches])
    return outs[0], outs[1:]


def loss_head(x, target, gain):
    T, D = x.shape
    tt = _rows(T, 256)

    def body(x_ref, t_ref, g_ref, dx_ref, dg_ref, loss_ref):
        i = pl.program_id(0)
        xf = x_ref[...]
        g = g_ref[...]
        r = lax.rsqrt(jnp.mean(xf * xf, axis=-1, keepdims=True) + EPS)
        xh = xf * r
        e = xh * g - t_ref[...]
        lpart = 0.5 * jnp.sum(jnp.sum(e * e, axis=1, keepdims=True), axis=0, keepdims=True) / D
        dy = e / D
        dyg = dy * g
        dx_ref[...] = r * (dyg - xh * jnp.mean(dyg * xh, axis=-1, keepdims=True))
        gpart = jnp.sum(dy * xh, axis=0, keepdims=True)
        lrow = jnp.broadcast_to(lpart, (1, 128))

        @pl.when(i == 0)
        def _():
            dg_ref[...] = gpart
            loss_ref[...] = lrow

        @pl.when(i > 0)
        def _():
            dg_ref[...] += gpart
            loss_ref[...] += lrow

    row = pl.BlockSpec((tt, D), lambda i: (i, 0))
    gsp = pl.BlockSpec((1, D), lambda i: (0, 0))
    return pl.pallas_call(
        body, name="loss_head", grid=(T // tt,),
        in_specs=[row, row, gsp],
        out_specs=[row, gsp, pl.BlockSpec((1, 128), lambda i: (0, 0))],
        out_shape=[jax.ShapeDtypeStruct((T, D), F32), jax.ShapeDtypeStruct((1, D), F32),
                   jax.ShapeDtypeStruct((1, 128), F32)],
        compiler_params=_params("arbitrary"),
    )(x, target, gain)


def _shift_down(u, prev8, first, k):
    rolled = pltpu.roll(u, k, 0)
    rid = lax.broadcasted_iota(jnp.int32, u.shape, 0)
    halo = jnp.where(first, 0.0, prev8)
    out = rolled
    for j in range(k):
        out = jnp.where(rid == j, halo[8 - k + j:8 - k + j + 1, :], out)
    return out


def _conv3(u, prev8, first, w, b):
    return (_shift_down(u, prev8, first, 2) * w[0:1, :] + _shift_down(u, prev8, first, 1) * w[1:2, :]
            + u * w[2:3, :] + b)


def ffn_up_act(xn, w_up, w, b, name, ride=None):
    T, K = xn.shape
    tt = _rows(T, 512)
    nj = D_FF // FF_TC

    def body(x_ref, wu_ref, w_ref, b_ref, u_ref, o_ref, tail):
        first = pl.program_id(1) == 0
        u = _dot(x_ref[...], wu_ref[...])
        u_ref[...] = u
        c = _conv3(u, tail[...], first, w_ref[...], b_ref[...])
        tail[...] = u[tt - 8:, :]
        cg, cv = c[:, :FF_TC], c[:, FF_TC:]
        o_ref[...] = (cg * _sigmoid_tanh(cg) * cv).astype(o_ref.dtype)

    return _call(
        body, name, (nj, T // tt),
        [pl.BlockSpec((tt, K), lambda j, i: (i, 0)),
         pl.BlockSpec((K, 2 * FF_TC), lambda j, i: (0, j)),
         pl.BlockSpec((3, 2 * FF_TC), lambda j, i: (0, j)),
         pl.BlockSpec((1, 2 * FF_TC), lambda j, i: (0, j))],
        [pl.BlockSpec((tt, 2 * FF_TC), lambda j, i: (i, j)), pl.BlockSpec((tt, FF_TC), lambda j, i: (i, j))],
        [jax.ShapeDtypeStruct((T, 2 * D_FF), F32), jax.ShapeDtypeStruct((T, D_FF), MM_DTYPE)],
        [pltpu.VMEM((8, 2 * FF_TC), F32)], ("parallel", "arbitrary"), (xn, w_up, w, b), ride)


def conv_act_bwd(u, da, w, b, name, ride=None):
    T = u.shape[0]
    tt = _rows(T, 512)
    nt = T // tt
    nj = D_FF // FF_TC
    te = tt + 8

    def body(u_ref, p_ref, n_ref, da_ref, dan_ref, w_ref, b_ref, du_ref, dw_ref, db_ref):
        i = pl.program_id(1)
        first = i == 0
        last = i == nt - 1
        w = w_ref[...]
        ue = jnp.concatenate([u_ref[...], n_ref[...]], axis=0)
        dae = jnp.concatenate([da_ref[...], jnp.where(last, 0.0, dan_ref[...])], axis=0)
        um2 = _shift_down(ue, p_ref[...], first, 2)
        um1 = _shift_down(ue, p_ref[...], first, 1)
        c = um2 * w[0:1, :] + um1 * w[1:2, :] + ue * w[2:3, :] + b_ref[...]
        cg, cv = c[:, :FF_TC], c[:, FF_TC:]
        s = _sigmoid_tanh(cg)
        dcg = dae * cv * (s * (1.0 + cg * (1.0 - s)))
        dcv = dae * (cg * s)
        dc = jnp.concatenate([dcg, dcv], axis=1)
        du = (dc * w[2:3, :] + pltpu.roll(dc, te - 1, 0) * w[1:2, :] + pltpu.roll(dc, te - 2, 0) * w[0:1, :])
        du_ref[...] = du[:tt, :].astype(du_ref.dtype)
        dcm = dc[:tt, :]
        dwp = jnp.concatenate([jnp.sum(dcm * um2[:tt, :], axis=0, keepdims=True),
                               jnp.sum(dcm * um1[:tt, :], axis=0, keepdims=True),
                               jnp.sum(dcm * ue[:tt, :], axis=0, keepdims=True)], axis=0)
        dbp = jnp.sum(dcm, axis=0, keepdims=True)

        @pl.when(first)
        def _():
            dw_ref[...] = dwp
            db_ref[...] = dbp

        @pl.when(i > 0)
        def _():
            dw_ref[...] += dwp
            db_ref[...] += dbp

    nb8 = T // 8
    return _call(
        body, name, (nj, nt),
        [pl.BlockSpec((tt, 2 * FF_TC), lambda j, i: (i, j)),
         pl.BlockSpec((8, 2 * FF_TC), lambda j, i: (jnp.maximum(i * (tt // 8) - 1, 0), j)),
         pl.BlockSpec((8, 2 * FF_TC), lambda j, i: (jnp.minimum((i + 1) * (tt // 8), nb8 - 1), j)),
         pl.BlockSpec((tt, FF_TC), lambda j, i: (i, j)),
         pl.BlockSpec((8, FF_TC), lambda j, i: (jnp.minimum((i + 1) * (tt // 8), nb8 - 1), j)),
         pl.BlockSpec((3, 2 * FF_TC), lambda j, i: (0, j)),
         pl.BlockSpec((1, 2 * FF_TC), lambda j, i: (0, j))],
        [pl.BlockSpec((tt, 2 * FF_TC), lambda j, i: (i, j)),
         pl.BlockSpec((3, 2 * FF_TC), lambda j, i: (0, j)),
         pl.BlockSpec((1, 2 * FF_TC), lambda j, i: (0, j))],
        [jax.ShapeDtypeStruct((T, 2 * D_FF), MM_DTYPE), jax.ShapeDtypeStruct((3, 2 * D_FF), F32),
         jax.ShapeDtypeStruct((1, 2 * D_FF), F32)],
        [], ("parallel", "arbitrary"), (u, u, u, da, da, w, b), ride)


def _interleave(a):
    lead = a.shape[:-1]
    nj = D_FF // FF_TC
    return jnp.swapaxes(a.reshape(*lead, 2, nj, FF_TC), -3, -2).reshape(*lead, 2 * D_FF)


def _deinterleave(a):
    lead = a.shape[:-1]
    nj = D_FF // FF_TC
    return jnp.swapaxes(a.reshape(*lead, nj, 2, FF_TC), -3, -2).reshape(*lead, 2 * D_FF)


A_GC = 1
A_TB = A_GC * A_CHUNK


def gate_prep(z, bias128):
    T = z.shape[0]
    tt = _rows(T, 512)

    def body(z_ref, b_ref, gc_ref, gr_ref):
        pre = z_ref[...] + b_ref[...]
        sc = SOFTCAP * jnp.tanh(pre / SOFTCAP)
        lf = jnp.minimum(sc, 0.0) - jnp.log(1.0 + jnp.exp(-jnp.abs(sc)))
        col = lax.broadcasted_iota(jnp.int32, pre.shape, 1)
        isf = jnp.logical_and(col >= A_HEADS, col < 2 * A_HEADS)
        r = lax.broadcasted_iota(jnp.int32, (tt, tt), 0)
        c = lax.broadcasted_iota(jnp.int32, (tt, tt), 1)
        bits = A_CHUNK.bit_length() - 1
        tri = jnp.logical_and(jnp.right_shift(r, bits) == jnp.right_shift(c, bits), c <= r).astype(F32)
        bcum = jnp.dot(tri, jnp.where(isf, lf, 0.0), precision=HI, preferred_element_type=F32)
        g = jnp.where(col < A_HEADS, sc, jnp.where(isf, bcum, 0.0))
        gc_ref[...] = g
        for s in range(tt // 128):
            gr_ref[s] = g[s * 128:(s + 1) * 128, :].T[0:8, :]

    return pl.pallas_call(
        body, name="gate_prep", grid=(T // tt,),
        in_specs=[pl.BlockSpec((tt, 128), lambda i: (i, GATE_COL // 128)),
                  pl.BlockSpec((1, 128), lambda i: (0, 0))],
        out_specs=[pl.BlockSpec((tt, 128), lambda i: (i, 0)),
                   pl.BlockSpec((tt // 128, 8, 128), lambda i: (i, 0, 0))],
        out_shape=[jax.ShapeDtypeStruct((T, 128), F32), jax.ShapeDtypeStruct((T // 128, 8, 128), F32)],
        compiler_params=_params("parallel"),
    )(z, bias128)


def _chunk_decay(A, qh, bc, br, lir, n, m, causal):
    logD = jnp.where(causal, bc - br + lir, -jnp.inf)
    m_inter = bc + m
    m_t = jnp.maximum(m_inter, jnp.max(logD, axis=1, keepdims=True))
    E = jnp.exp(logD - m_t)
    Sm = A * E
    wi = jnp.exp(m_inter - m_t)
    qn = jnp.sum(qh.astype(F32) * n, axis=1, keepdims=True)
    den = jnp.sum(Sm, axis=1, keepdims=True) + wi * qn
    gs = jnp.maximum(jnp.abs(den), jnp.exp(-m_t))
    return E, Sm, wi, den, gs, m_t


def _state_weights(bc, lic, br, lir, m):
    bL = bc[A_CHUNK - 1:A_CHUNK, :]
    m_new = jnp.maximum(bL + m, jnp.max(bL - br + lir, axis=1, keepdims=True))
    wk = jnp.exp(bL - bc + lic - m_new)
    decay = jnp.exp(bL + m - m_new)
    return wk, decay, m_new


def _head_slices(h):
    return (slice(h * A_QK, (h + 1) * A_QK), slice(h * A_V, (h + 1) * A_V))


def mlstm_fwd(z, gcol, grow, hng, ride=None):
    T = z.shape[0]
    NC = T // A_CHUNK
    scale = A_QK ** -0.5

    def body(q_ref, k_ref, v_ref, o_ref, gc_ref, gr_ref, hng_ref, hg_ref, Cs_ref, ns_ref, ms_ref,
             C_sc, n_sc, m_sc):
        @pl.when(pl.program_id(0) == 0)
        def _():
            C_sc[...] = jnp.zeros_like(C_sc)
            n_sc[...] = jnp.zeros_like(n_sc)
            m_sc[...] = jnp.zeros_like(m_sc)

        ri = lax.broadcasted_iota(jnp.int32, (A_CHUNK, A_CHUNK), 0)
        ci = lax.broadcasted_iota(jnp.int32, (A_CHUNK, A_CHUNK), 1)
        causal = ri >= ci
        gr = jnp.concatenate([gr_ref[s] for s in range(A_TB // 128)], axis=1)
        for c in range(A_GC):
            rows = slice(c * A_CHUNK, (c + 1) * A_CHUNK)
            gc = gc_ref[rows, :]
            grc = gr[:, c * A_CHUNK:(c + 1) * A_CHUNK]
            for h in range(A_HEADS):
                sk, sv = _head_slices(h)
                qh = (q_ref[rows, sk] * scale).astype(MM_DTYPE)
                kh = k_ref[rows, sk].astype(MM_DTYPE)
                vh = v_ref[rows, sv].astype(MM_DTYPE)
                lic, bc = gc[:, h:h + 1], gc[:, A_HEADS + h:A_HEADS + h + 1]
                lir, br = grc[h:h + 1, :], grc[A_HEADS + h:A_HEADS + h + 1, :]
                C, n, m = C_sc[h], n_sc[h], m_sc[h][:, 0:1]
                Cs_ref[c, h] = C
                ns_ref[c, h] = n
                ms_ref[c, h] = m_sc[h]
                _, Sm, wi, _, gs, _ = _chunk_decay(_dot_nt(qh, kh), qh, bc, br, lir, n, m, causal)
                hh = (_dot(Sm, vh) + wi * _dot(qh, C)) / gs
                hn = hh * lax.rsqrt(jnp.mean(hh * hh, axis=1, keepdims=True) + EPS) * hng_ref[:, sv]
                hg_ref[rows, sv] = (hn * _sigmoid(o_ref[rows, sv])).astype(hg_ref.dtype)
                wk, decay, m_new = _state_weights(bc, lic, br, lir, m)
                kw = kh.astype(F32) * wk
                C_sc[h] = decay * C + _dot_tn(kw, vh)
                n_sc[h] = decay * n + jnp.sum(kw, axis=0, keepdims=True)
                m_sc[h] = jnp.broadcast_to(m_new, (1, 128))

    tok = lambda w, cb: pl.BlockSpec((A_TB, w), lambda i: (i, cb))
    return _call(
        body, "mlstm_fwd", (NC // A_GC,),
        [tok(512, 0), tok(512, 1), tok(1024, 1), tok(1024, 2),
         pl.BlockSpec((A_TB, 128), lambda i: (i, 0)),
         pl.BlockSpec((A_TB // 128, 8, 128), lambda i: (i, 0, 0)),
         pl.BlockSpec((1, 1024), lambda i: (0, 0))],
        [pl.BlockSpec((A_TB, 1024), lambda i: (i, 0)),
         pl.BlockSpec((A_GC, A_HEADS, A_QK, A_V), lambda i: (i, 0, 0, 0)),
         pl.BlockSpec((A_GC, A_HEADS, 1, 128), lambda i: (i, 0, 0, 0)),
         pl.BlockSpec((A_GC, A_HEADS, 1, 128), lambda i: (i, 0, 0, 0))],
        [jax.ShapeDtypeStruct((T, 1024), MM_DTYPE),
         jax.ShapeDtypeStruct((NC, A_HEADS, A_QK, A_V), F32),
         jax.ShapeDtypeStruct((NC, A_HEADS, 1, 128), F32),
         jax.ShapeDtypeStruct((NC, A_HEADS, 1, 128), F32)],
        [pltpu.VMEM((A_HEADS, A_QK, A_V), F32), pltpu.VMEM((A_HEADS, 1, 128), F32),
         pltpu.VMEM((A_HEADS, 1, 128), F32)],
        ("arbitrary",), (z, z, z, z, gcol, grow, hng), ride)


def mlstm_bwd(z, gcol, grow, hng, bias128, Cs, ns, ms, dhg, ride=None):
    T = z.shape[0]
    NC = T // A_CHUNK
    nsteps = NC // A_GC
    scale = A_QK ** -0.5

    def body(q_ref, k_ref, v_ref, o_ref, zg_ref, gc_ref, gr_ref, hng_ref, b_ref, Cs_ref, ns_ref, ms_ref,
             dhg_ref, dz_ref, dgn_ref, dbif_ref, dC_sc, dn_sc):
        @pl.when(pl.program_id(0) == 0)
        def _():
            dC_sc[...] = jnp.zeros_like(dC_sc)
            dn_sc[...] = jnp.zeros_like(dn_sc)
            dgn_ref[...] = jnp.zeros_like(dgn_ref)
            dbif_ref[...] = jnp.zeros_like(dbif_ref)

        ri = lax.broadcasted_iota(jnp.int32, (A_CHUNK, A_CHUNK), 0)
        ci = lax.broadcasted_iota(jnp.int32, (A_CHUNK, A_CHUNK), 1)
        causal = ri >= ci
        upper = (ci >= ri).astype(F32)
        rid = lax.broadcasted_iota(jnp.int32, (A_CHUNK, 1), 0)
        col = lax.broadcasted_iota(jnp.int32, (A_CHUNK, 128), 1)
        gr = jnp.concatenate([gr_ref[s] for s in range(A_TB // 128)], axis=1)
        for c in reversed(range(A_GC)):
            rows = slice(c * A_CHUNK, (c + 1) * A_CHUNK)
            gc = gc_ref[rows, :]
            grc = gr[:, c * A_CHUNK:(c + 1) * A_CHUNK]
            dG = jnp.zeros((A_CHUNK, 128), F32)
            hs = []
            for h in range(A_HEADS):
                sk, sv = _head_slices(h)
                s = dict(sk=sk, sv=sv, qh=(q_ref[rows, sk] * scale).astype(MM_DTYPE),
                         kh=k_ref[rows, sk].astype(MM_DTYPE), vh=v_ref[rows, sv].astype(MM_DTYPE),
                         lic=gc[:, h:h + 1], bc=gc[:, A_HEADS + h:A_HEADS + h + 1],
                         lir=grc[h:h + 1, :], br=grc[A_HEADS + h:A_HEADS + h + 1, :],
                         C=Cs_ref[c, h], n=ns_ref[c, h], m=ms_ref[c, h][:, 0:1], dC=dC_sc[h], dn=dn_sc[h])
                s['qf'], s['kf'] = s['qh'].astype(F32), s['kh'].astype(F32)
                s['wk'], s['decay'], _ = _state_weights(s['bc'], s['lic'], s['br'], s['lir'], s['m'])
                hs.append(s)
            for s in hs:
                s['A'] = _dot_nt(s['qh'], s['kh'])
                s['qC'] = _dot(s['qh'], s['C'])
                s['vdC'] = _dot_nt(s['vh'], s['dC'])
                s['kdC'] = _dot(s['kh'], s['dC'])
            for s in hs:
                s['E'], s['Sm'], s['wi'], s['den'], s['gs'], s['m_t'] = _chunk_decay(
                    s['A'], s['qh'], s['bc'], s['br'], s['lir'], s['n'], s['m'], causal)
            for s in hs:
                s['num'] = _dot(s['Sm'], s['vh']) + s['wi'] * s['qC']
            for h, s in enumerate(hs):
                sv, gs = s['sv'], s['gs']
                hh = s['num'] / gs
                r = lax.rsqrt(jnp.mean(hh * hh, axis=1, keepdims=True) + EPS)
                gn = hng_ref[:, sv]
                sg = _sigmoid(o_ref[rows, sv])
                dhg_h = dhg_ref[rows, sv]
                dhn = dhg_h * sg
                dz_ref[rows, 2048 + h * A_V:2048 + (h + 1) * A_V] = (
                    dhg_h * (hh * r * gn) * sg * (1.0 - sg)).astype(dz_ref.dtype)
                dgn_ref[:, sv] += jnp.sum(dhn * hh * r, axis=0, keepdims=True)
                dyg = dhn * gn
                dh = r * dyg - hh * (r * r * r) * jnp.mean(dyg * hh, axis=1, keepdims=True)
                s['dnum'] = dh / gs
                live = (jnp.abs(s['den']) > jnp.exp(-s['m_t'])).astype(F32)
                s['dden'] = -jnp.sum(dh * hh, axis=1, keepdims=True) / gs * jnp.sign(s['den']) * live
            for s in hs:
                s['dnv'] = _dot_nt(s['dnum'], s['vh'])
                s['dnC'] = _dot_nt(s['dnum'], s['C'])
            for s in hs:
                s['dSE'] = jnp.where(causal, s['dnv'] + s['dden'], 0.0) * s['E']
            for s in hs:
                s['dq'] = _dot(s['dSE'], s['kh']) + s['wi'] * (s['dnC'] + s['dden'] * s['n'])
                s['dk_inter'] = s['wk'] * (s['vdC'] + s['dn'])
                s['dk'] = _dot_tn(s['dSE'], s['qh']) + s['dk_inter']
                s['dv'] = _dot_tn(s['Sm'], s['dnum']) + s['wk'] * s['kdC']
                s['dCq'] = _dot_tn(s['qf'] * s['wi'], s['dnum'])
            for h, s in enumerate(hs):
                dq, dk, qf, kf, dC, dn = s['dq'], s['dk'], s['qf'], s['kf'], s['dC'], s['dn']
                dz_ref[rows, s['sk']] = (dq * scale).astype(dz_ref.dtype)
                dz_ref[rows, 512 + h * A_QK:512 + (h + 1) * A_QK] = dk.astype(dz_ref.dtype)
                dz_ref[rows, 1024 + h * A_V:1024 + (h + 1) * A_V] = s['dv'].astype(dz_ref.dtype)
                dli = jnp.sum(kf * dk, axis=1, keepdims=True)
                db = jnp.sum(qf * dq, axis=1, keepdims=True) - dli
                usum = jnp.sum(jnp.sum(kf * s['dk_inter'], axis=1, keepdims=True), axis=0, keepdims=True)
                ddecay = (jnp.sum(jnp.sum(dC * s['C'], axis=1, keepdims=True), axis=0, keepdims=True)
                          + jnp.sum(dn * s['n'], axis=1, keepdims=True))
                db = db + jnp.where(rid == A_CHUNK - 1, usum + ddecay * s['decay'], 0.0)
                dG = dG + jnp.where(col == h, dli, 0.0) + jnp.where(col == A_HEADS + h, db, 0.0)
                dC_sc[h] = s['decay'] * dC + s['dCq']
                dn_sc[h] = s['decay'] * dn + jnp.sum(qf * (s['wi'] * s['dden']), axis=0, keepdims=True)
            dlf = jnp.dot(upper, dG, precision=HI, preferred_element_type=F32)
            pre = zg_ref[rows, :] + b_ref[...]
            th = jnp.tanh(pre / SOFTCAP)
            dcap = 1.0 - th * th
            dpre = jnp.where(col < A_HEADS, dG * dcap,
                             jnp.where(col < 2 * A_HEADS, dlf * _sigmoid(-SOFTCAP * th) * dcap, 0.0))
            dz_ref[rows, GATE_COL:GATE_COL + 128] = dpre.astype(dz_ref.dtype)
            dbif_ref[...] += jnp.sum(dpre, axis=0, keepdims=True)

    rev = lambda i: nsteps - 1 - i
    tok = lambda w, cb: pl.BlockSpec((A_TB, w), lambda i: (rev(i), cb))
    st = lambda a, b: pl.BlockSpec((A_GC, A_HEADS, a, b), lambda i: (rev(i), 0, 0, 0))
    return _call(
        body, "mlstm_bwd", (nsteps,),
        [tok(512, 0), tok(512, 1), tok(1024, 1), tok(1024, 2), tok(128, GATE_COL // 128),
         pl.BlockSpec((A_TB, 128), lambda i: (rev(i), 0)),
         pl.BlockSpec((A_TB // 128, 8, 128), lambda i: (rev(i), 0, 0)),
         pl.BlockSpec((1, 1024), lambda i: (0, 0)),
         pl.BlockSpec((1, 128), lambda i: (0, 0)),
         st(A_QK, A_V), st(1, 128), st(1, 128),
         pl.BlockSpec((A_TB, 1024), lambda i: (rev(i), 0))],
        [pl.BlockSpec((A_TB, A_IN_PAD), lambda i: (rev(i), 0)),
         pl.BlockSpec((1, 1024), lambda i: (0, 0)),
         pl.BlockSpec((1, 128), lambda i: (0, 0))],
        [jax.ShapeDtypeStruct((T, A_IN_PAD), MM_DTYPE), jax.ShapeDtypeStruct((1, 1024), F32),
         jax.ShapeDtypeStruct((1, 128), F32)],
        [pltpu.VMEM((A_HEADS, A_QK, A_V), F32), pltpu.VMEM((A_HEADS, 1, 128), F32)],
        ("arbitrary",), (z, z, z, z, z, gcol, grow, hng, bias128, Cs, ns, ms, dhg), ride)


def _t5_bucket(dist):
    max_exact = REL_BUCKETS // 2
    d = np.maximum(dist, 0)
    log_ratio = np.log(np.maximum(d, 1) / max_exact) / math.log(REL_MAX_DIST / max_exact)
    large = np.minimum(max_exact + (log_ratio * (REL_BUCKETS - max_exact)).astype(np.int64), REL_BUCKETS - 1)
    return np.where(d < max_exact, d, large).astype(np.int32)


def _group_bucket(g):
    delta = B_BLOCK + np.arange(B_BLOCK)[:, None] - np.arange(2 * B_BLOCK)[None, :]
    return _t5_bucket(delta * DILATIONS[g])


def _band_mask(n):
    ri = lax.broadcasted_iota(jnp.int32, (B_BLOCK, 2 * B_BLOCK), 0)
    ci = lax.broadcasted_iota(jnp.int32, (B_BLOCK, 2 * B_BLOCK), 1)
    band = jnp.logical_and(ci >= ri, ci <= ri + B_BLOCK)
    return jnp.logical_and(band, jnp.logical_or(ci >= B_BLOCK, n > 0))


def _both(p_ref, c_ref, sl):
    return jnp.concatenate([p_ref[:, sl], c_ref[:, sl]], axis=0)


def _scores(qh, kh, bias_h, valid):
    return jnp.where(valid, _dot_nt(qh, kh) * (B_DH ** -0.5) + bias_h, -jnp.inf)


def _attn_specs():
    wide = pl.BlockSpec((B_BLOCK, 1024), lambda r, n: (n, r))
    prev = pl.BlockSpec((B_BLOCK, 1024), lambda r, n: (jnp.maximum(n - 1, 0), r))
    narrow = pl.BlockSpec((B_BLOCK, 128), lambda r, n: (n, r))
    bias = pl.BlockSpec((B_HEADS, B_BLOCK, 2 * B_BLOCK), lambda r, n: (0, 0, 0))
    return wide, prev, narrow, bias


def _to_view(read_chunk, sc, o_ref, dil, nc, tt):
    for c in range(nc):
        sc[c] = read_chunk(c)
    for r in range(dil):
        for c in range(nc):
            lo = (r * nc + c) * 128
            o_ref[:, lo:lo + 128] = sc[c, pl.ds(r, tt // dil, stride=dil), :].astype(o_ref.dtype)


def _from_view(read_view, sc, dil, nc, tt):
    for r in range(dil):
        for c in range(nc):
            sc[c, pl.ds(r, tt // dil, stride=dil), :] = read_view((r * nc + c) * 128).astype(F32)


def attn_fwd(qv, kvw, vvw, bias, g):
    dil = DILATIONS[g]
    Tv = qv.shape[0]
    nb = Tv // B_BLOCK
    wide, prev, narrow, bsp = _attn_specs()

    def body(q_ref, kp_ref, kc_ref, vp_ref, vc_ref, b_ref, o_ref, lse_ref):
        valid = _band_mask(pl.program_id(1))
        lse_ref[...] = jnp.zeros_like(lse_ref)
        heads = [slice(h * B_DH, (h + 1) * B_DH) for h in range(B_HEADS)]
        S = [_scores(q_ref[:, sl], _both(kp_ref, kc_ref, sl), b_ref[h], valid) for h, sl in enumerate(heads)]
        P, L = [], []
        for h in range(B_HEADS):
            m = jnp.max(S[h], axis=1, keepdims=True)
            p = jnp.exp(S[h] - m)
            l = jnp.sum(p, axis=1, keepdims=True)
            lse_ref[:, h:h + 1] = m + jnp.log(l)
            P.append(p.astype(MM_DTYPE))
            L.append(l)
        for h, sl in enumerate(heads):
            o_ref[:, sl] = _dot(P[h], _both(vp_ref, vc_ref, sl)) / L[h]

    return pl.pallas_call(
        body, name=f"attn_fwd_g{g}", grid=(dil, nb),
        in_specs=[wide, prev, wide, prev, wide, bsp], out_specs=[wide, narrow],
        out_shape=[jax.ShapeDtypeStruct((Tv, dil * 1024), F32), jax.ShapeDtypeStruct((Tv, dil * 128), F32)],
        compiler_params=_params("parallel", "parallel"),
    )(qv, kvw, kvw, vvw, vvw, bias)


def attn_bwd(qv, kvw, vvw, bias, do_v, lse_v, dl_v, g):
    dil = DILATIONS[g]
    Tv = qv.shape[0]
    nb = Tv // B_BLOCK
    wide, prev, narrow, bsp = _attn_specs()

    def body(q_ref, kp_ref, kc_ref, vp_ref, vc_ref, b_ref, bt_ref, do_ref, lse_ref, dl_ref,
             dq_ref, dkc_ref, dkp_ref, dvc_ref, dvp_ref, db_ref):
        @pl.when(jnp.logical_and(pl.program_id(0) == 0, pl.program_id(1) == 0))
        def _():
            db_ref[...] = jnp.zeros_like(db_ref)

        n = pl.program_id(1)
        valid = _band_mask(n)
        ki = lax.broadcasted_iota(jnp.int32, (2 * B_BLOCK, B_BLOCK), 0)
        qi = lax.broadcasted_iota(jnp.int32, (2 * B_BLOCK, B_BLOCK), 1)
        valid_t = jnp.logical_and(jnp.logical_and(ki >= qi, ki <= qi + B_BLOCK), jnp.logical_or(ki >= B_BLOCK, n > 0))
        lse_t, dl_t = lse_ref[...].T, dl_ref[...].T
        heads = [slice(h * B_DH, (h + 1) * B_DH) for h in range(B_HEADS)]
        scale = B_DH ** -0.5
        PT, DS, DST = [], [], []
        for h, sl in enumerate(heads):
            qh, doh = q_ref[:, sl], do_ref[:, sl].astype(MM_DTYPE)
            kh, vh = _both(kp_ref, kc_ref, sl), _both(vp_ref, vc_ref, sl)
            p = jnp.exp(_scores(qh, kh, b_ref[h], valid) - lse_ref[:, h:h + 1])
            ds = p * (_dot_nt(doh, vh) - dl_ref[:, h:h + 1])
            db_ref[h] += ds
            DS.append((ds * scale).astype(MM_DTYPE))
            pt = jnp.exp(_scores(kh, qh, bt_ref[h], valid_t) - lse_t[h:h + 1, :])
            PT.append(pt.astype(MM_DTYPE))
            DST.append((pt * (_dot_nt(vh, doh) - dl_t[h:h + 1, :]) * scale).astype(MM_DTYPE))
        for h, sl in enumerate(heads):
            qh, doh = q_ref[:, sl], do_ref[:, sl].astype(MM_DTYPE)
            dq_ref[:, sl] = _dot(DS[h], _both(kp_ref, kc_ref, sl)).astype(MM_DTYPE)
            dk = _dot(DST[h], qh).astype(MM_DTYPE)
            dv = _dot(PT[h], doh).astype(MM_DTYPE)
            dkp_ref[:, sl], dkc_ref[:, sl] = dk[:B_BLOCK], dk[B_BLOCK:]
            dvp_ref[:, sl], dvc_ref[:, sl] = dv[:B_BLOCK], dv[B_BLOCK:]

    big = jax.ShapeDtypeStruct((Tv, dil * 1024), MM_DTYPE)
    bsp_t = pl.BlockSpec((B_HEADS, 2 * B_BLOCK, B_BLOCK), lambda r, n: (0, 0, 0))
    return pl.pallas_call(
        body, name=f"attn_bwd_g{g}", grid=(dil, nb),
        in_specs=[wide, prev, wide, prev, wide, bsp, bsp_t, wide, narrow, narrow],
        out_specs=[wide] * 5 + [bsp],
        out_shape=[big] * 5 + [jax.ShapeDtypeStruct((B_HEADS, B_BLOCK, 2 * B_BLOCK), F32)],
        compiler_params=_params("arbitrary", "arbitrary"),
    )(qv, kvw, kvw, vvw, vvw, bias, jnp.swapaxes(bias, 1, 2), do_v, lse_v, dl_v)


def _head_expand():
    e = np.zeros((128, 1024), np.float32)
    for h in range(B_HEADS):
        e[h, h * B_DH:(h + 1) * B_DH] = 1.0
    return e


A_TT = 256


def _view_spec(dil, width):
    return pl.BlockSpec((A_TT // dil, dil * width), lambda i: (i, 0))


def attn_merge(os_v, lses_v):
    T = os_v[0].shape[0]
    tt = A_TT
    expand = jnp.asarray(_head_expand())

    def body(o0, o1, o2, l0, l1, l2, e_ref, ob_ref, of_ref, lse0_ref, lse1_ref, lse2_ref, sc_o, sc_l):
        for gi, (o_ref, l_ref) in enumerate(((o1, l1), (o2, l2))):
            dil = DILATIONS[gi + 1]
            _from_view(lambda lo: o_ref[:, lo:lo + 128], sc_o.at[gi], dil, 8, tt)
            _from_view(lambda lo: l_ref[:, lo:lo + 128], sc_l.at[gi], dil, 1, tt)
        ls = [l0[...], sc_l[0, 0], sc_l[1, 0]]
        m = jnp.maximum(jnp.maximum(ls[0], ls[1]), ls[2])
        ex = [jnp.exp(l - m) for l in ls]
        tot = ex[0] + ex[1] + ex[2]
        lse = m + jnp.log(tot)
        lse0_ref[...] = lse
        _to_view(lambda c: lse, sc_l.at[2], lse1_ref, DILATIONS[1], 1, tt)
        _to_view(lambda c: lse, sc_l.at[2], lse2_ref, DILATIONS[2], 1, tt)
        ws = [e / tot for e in ex]
        for c in range(8):
            cols = slice(c * 128, (c + 1) * 128)
            ecol = e_ref[:, cols]
            spread = [jnp.dot(w, ecol, precision=HI, preferred_element_type=F32) for w in ws]
            out = spread[0] * o0[:, cols] + spread[1] * sc_o[0, c] + spread[2] * sc_o[1, c]
            of_ref[:, cols] = out
            ob_ref[:, cols] = out.astype(ob_ref.dtype)

    wide = pl.BlockSpec((tt, 1024), lambda i: (i, 0))
    return pl.pallas_call(
        body, name="attn_merge", grid=(T // tt,),
        in_specs=[_view_spec(d, 1024) for d in DILATIONS] + [_view_spec(d, 128) for d in DILATIONS]
        + [pl.BlockSpec((128, 1024), lambda i: (0, 0))],
        out_specs=[wide, wide] + [_view_spec(d, 128) for d in DILATIONS],
        out_shape=[jax.ShapeDtypeStruct((T, 1024), MM_DTYPE), jax.ShapeDtypeStruct((T, 1024), F32)]
        + [jax.ShapeDtypeStruct((T // d, d * 128), F32) for d in DILATIONS],
        scratch_shapes=[pltpu.VMEM((2, 8, tt, 128), F32), pltpu.VMEM((3, 1, tt, 128), F32)],
        compiler_params=_params("parallel"),
    )(*os_v, *lses_v, expand)


def attn_prep(datt, out):
    T = datt.shape[0]
    tt = A_TT
    expand_t = jnp.asarray(_head_expand().T.copy())

    def body(d_ref, o_ref, e_ref, do0, do1, do2, dl0, dl1, dl2, sc_d, sc_l):
        delta = jnp.dot(d_ref[...] * o_ref[...], e_ref[...], precision=HI, preferred_element_type=F32)
        do0[...] = d_ref[...].astype(do0.dtype)
        dl0[...] = delta
        for do_ref, dl_ref, dil in ((do1, dl1, DILATIONS[1]), (do2, dl2, DILATIONS[2])):
            _to_view(lambda c: d_ref[:, c * 128:(c + 1) * 128], sc_d, do_ref, dil, 8, tt)
            _to_view(lambda c: delta, sc_l, dl_ref, dil, 1, tt)

    wide = pl.BlockSpec((tt, 1024), lambda i: (i, 0))
    return pl.pallas_call(
        body, name="attn_prep", grid=(T // tt,),
        in_specs=[wide, wide, pl.BlockSpec((1024, 128), lambda i: (0, 0))],
        out_specs=[_view_spec(d, 1024) for d in DILATIONS] + [_view_spec(d, 128) for d in DILATIONS],
        out_shape=[jax.ShapeDtypeStruct((T // d, d * 1024), MM_DTYPE) for d in DILATIONS]
        + [jax.ShapeDtypeStruct((T // d, d * 128), F32) for d in DILATIONS],
        scratch_shapes=[pltpu.VMEM((8, tt, 128), F32), pltpu.VMEM((1, tt, 128), F32)],
        compiler_params=_params("parallel"),
    )(datt, out, expand_t)


def attn_combine(parts):
    T = parts[0][0].shape[0]
    tt = A_TT
    nt = T // tt
    shift = [None] + [B_BLOCK * d // tt for d in DILATIONS[1:]]

    def body(dq0, kc0, vc0, kpa0, kpb0, vpa0, vpb0, dq1, kc1, kp1, vc1, vp1, dq2, kc2, kp2, vc2, vp2,
             dq_ref, dkv_ref, sc):
        i = pl.program_id(0)
        dq_ref[:, 0:1024] = dq0[...].astype(dq_ref.dtype)
        for col, c_ref, pa_ref, pb_ref in ((0, kc0, kpa0, kpb0), (3, vc0, vpa0, vpb0)):
            nxt = jnp.where(i + 1 < nt, pb_ref[:tt // 2, :].astype(F32), 0.0)
            later = jnp.concatenate([pa_ref[tt // 2:, :].astype(F32), nxt], axis=0)
            dkv_ref[:, col * 1024:(col + 1) * 1024] = (c_ref[...].astype(F32) + later).astype(dkv_ref.dtype)
        for g, (dq, kc, kp, vc, vp) in ((1, (dq1, kc1, kp1, vc1, vp1)), (2, (dq2, kc2, kp2, vc2, vp2))):
            dil = DILATIONS[g]
            live = i + shift[g] < nt
            _from_view(lambda lo: dq[:, lo:lo + 128], sc, dil, 8, tt)
            for c in range(8):
                dq_ref[:, g * 1024 + c * 128:g * 1024 + (c + 1) * 128] = sc[c].astype(dq_ref.dtype)
            for col, c_ref, p_ref in ((g, kc, kp), (3 + g, vc, vp)):
                _from_view(lambda lo: c_ref[:, lo:lo + 128].astype(F32)
                           + jnp.where(live, p_ref[:, lo:lo + 128].astype(F32), 0.0), sc, dil, 8, tt)
                for c in range(8):
                    dkv_ref[:, col * 1024 + c * 128:col * 1024 + (c + 1) * 128] = sc[c].astype(dkv_ref.dtype)

    def later_spec(dil, blocks):
        return pl.BlockSpec((tt // dil, dil * 1024), lambda i: (jnp.minimum(i + blocks, nt - 1), 0))

    cur = [_view_spec(d, 1024) for d in DILATIONS]
    in_specs = [cur[0], cur[0], cur[0], cur[0], later_spec(1, 1), cur[0], later_spec(1, 1)]
    args = [parts[0][0], parts[0][1], parts[0][3], parts[0][2], parts[0][2], parts[0][4], parts[0][4]]
    for g in (1, 2):
        in_specs += [cur[g], cur[g], later_spec(DILATIONS[g], shift[g]), cur[g], later_spec(DILATIONS[g], shift[g])]
        args += list(parts[g][:5])
    return pl.pallas_call(
        body, name="attn_combine", grid=(nt,), in_specs=in_specs,
        out_specs=[pl.BlockSpec((tt, 3072), lambda i: (i, 0)), pl.BlockSpec((tt, 6144), lambda i: (i, 0))],
        out_shape=[jax.ShapeDtypeStruct((T, 3072), MM_DTYPE), jax.ShapeDtypeStruct((T, 6144), MM_DTYPE)],
        scratch_shapes=[pltpu.VMEM((8, tt, 128), F32)],
        compiler_params=_params("parallel"),
    )(*args)


def adamw(w, g, m, v, name):
    R, C = w.shape
    tr = R if R * C * 4 <= (1 << 20) else _rows(R, max(8, ((1 << 20) // (C * 4)) // 8 * 8))

    def body(w_ref, g_ref, m_ref, v_ref, d_ref, nm_ref, nv_ref):
        gg = g_ref[...]
        nm = ADAM_B1 * m_ref[...] + (1.0 - ADAM_B1) * gg
        nv = ADAM_B2 * v_ref[...] + (1.0 - ADAM_B2) * (gg * gg)
        m_hat = nm / (1.0 - ADAM_B1 ** ADAM_STEP)
        v_hat = nv / (1.0 - ADAM_B2 ** ADAM_STEP)
        d_ref[...] = -ADAM_LR * (m_hat / (jnp.sqrt(v_hat) + ADAM_EPS) + ADAM_WD * w_ref[...])
        nm_ref[...] = nm
        nv_ref[...] = nv

    blk = pl.BlockSpec((tr, C), lambda i: (i, 0))
    sds = jax.ShapeDtypeStruct((R, C), F32)
    return pl.pallas_call(
        body, name=name, grid=(R // tr,), in_specs=[blk] * 4, out_specs=[blk] * 3, out_shape=[sds] * 3,
        compiler_params=_params("parallel"),
    )(w, g, m, v)


def sum_slots(x, name, out_dtype=F32):
    n, R, C = x.shape
    tr = _rows(R, 256)

    def body(x_ref, o_ref):
        acc = x_ref[0].astype(F32)
        for s in range(1, n):
            acc = acc + x_ref[s].astype(F32)
        o_ref[...] = acc.astype(out_dtype)

    return pl.pallas_call(
        body, name=name, grid=(R // tr,),
        in_specs=[pl.BlockSpec((n, tr, C), lambda i: (0, i, 0))],
        out_specs=pl.BlockSpec((tr, C), lambda i: (i, 0)),
        out_shape=jax.ShapeDtypeStruct((R, C), out_dtype),
        compiler_params=_params("parallel"),
    )(x)


_ANY = pl.BlockSpec(memory_space=pl.ANY)
GROUP_ALL = ([(0, 0, 1), (0, 1, 0), (0, 1, 1), (1, 0, 0), (1, 0, 1), (1, 1, 0), (1, 1, 1)],
             lambda d: 4 * d[0] + 2 * d[1] + d[2])
GROUP_CHIPS = ([(0, 1, 0), (1, 0, 0), (1, 1, 0)], lambda d: 2 * d[0] + d[1])
GROUP_SIBLING = ([(0, 0, 1)], lambda d: d[2])


def _me():
    return lax.axis_index("x"), lax.axis_index("y"), lax.axis_index("c")


def _peer(me, flip):
    return tuple(1 - a if f else a for a, f in zip(me, flip))


class Exchange:
    def __init__(self, x, group, scatter):
        self.flips, self.slot = group
        self.scatter = scatter
        self.n = len(self.flips) + 1
        self.out_shape = jax.ShapeDtypeStruct((self.n,) + x.shape[-2:], x.dtype)
        self.scratch = [pltpu.SemaphoreType.DMA((self.n - 1,)), pltpu.SemaphoreType.DMA((self.n - 1,)),
                        pltpu.SemaphoreType.DMA]

    def _copies(self, x_ref, o_ref, send_sems, recv_sems, local_sem, arrivals):
        me = _me()
        slot = self.slot
        mine = pltpu.make_async_copy(x_ref.at[slot(me)] if self.scatter else x_ref, o_ref.at[slot(me)], local_sem)
        sends, landed = [], []
        for k, flip in enumerate(self.flips):
            peer = _peer(me, flip)
            sends.append(pltpu.make_async_remote_copy(
                src_ref=x_ref.at[slot(peer)] if self.scatter else x_ref, dst_ref=o_ref.at[slot(me)],
                send_sem=send_sems.at[k], recv_sem=recv_sems.at[k], device_id=peer, device_id_type=MESH_ID))
            if arrivals:
                landed.append(pltpu.make_async_remote_copy(
                    src_ref=o_ref.at[slot(me)], dst_ref=o_ref.at[slot(peer)], send_sem=send_sems.at[k],
                    recv_sem=recv_sems.at[k], device_id=peer, device_id_type=MESH_ID))
        return mine, sends, landed

    def start(self, *refs):
        mine, sends, _ = self._copies(*refs, arrivals=False)
        mine.start()
        for cp in sends:
            cp.start()

    def wait(self, *refs):
        mine, sends, arrivals = self._copies(*refs, arrivals=True)
        for cp in arrivals:
            cp.wait_recv()
        for cp in sends:
            cp.wait_send()
        mine.wait()

    def __call__(self, x, name):
        def body(*refs):
            self.start(*refs)
            self.wait(*refs)

        return pl.pallas_call(body, name=name, in_specs=[_ANY], out_specs=_ANY, out_shape=self.out_shape,
                              scratch_shapes=self.scratch)(x)


def group_gather(x, name, group):
    return Exchange(x, group, scatter=False)(x, name)


def group_scatter(x, name, group):
    return Exchange(x, group, scatter=True)(x, name)


def _call(body, name, grid, in_specs, out_specs, out_shape, scratch, semantics, args, ride=None):
    if ride is None:
        return pl.pallas_call(body, name=name, grid=grid, in_specs=in_specs, out_specs=out_specs,
                              out_shape=out_shape, scratch_shapes=scratch,
                              compiler_params=_params(*semantics))(*args)
    x, exch = ride
    n_in, n_out, n_scr = len(in_specs), len(out_specs), len(scratch)

    def at_step(pick):
        hit = None
        for axis, size in enumerate(grid):
            here = pl.program_id(axis) == pick(size)
            hit = here if hit is None else jnp.logical_and(hit, here)
        return hit

    def riding(*refs):
        ins, x_ref = refs[:n_in], refs[n_in]
        outs, o_ref = refs[n_in + 1:n_in + 1 + n_out], refs[n_in + 1 + n_out]
        scr, sems = refs[n_in + 2 + n_out:n_in + 2 + n_out + n_scr], refs[n_in + 2 + n_out + n_scr:]

        @pl.when(at_step(lambda size: 0))
        def _():
            exch.start(x_ref, o_ref, *sems)

        body(*ins, *outs, *scr)

        @pl.when(at_step(lambda size: size - 1))
        def _():
            exch.wait(x_ref, o_ref, *sems)

    return pl.pallas_call(
        riding, name=name, grid=grid, in_specs=list(in_specs) + [_ANY], out_specs=list(out_specs) + [_ANY],
        out_shape=list(out_shape) + [exch.out_shape], scratch_shapes=list(scratch) + exch.scratch,
        compiler_params=_params(*(["arbitrary"] * len(grid))))(*args, x)


WEIGHTS = ['a_norm_g', 'a_w_in', 'a_b_if', 'a_hnorm_g', 'a_w_out', 'kv_norm_g', 'w_kv', 'b_norm_g', 'b_w_q',
           'b_w_out', 'rel_bias', 'f_norm_g', 'f_w_up', 'f_conv_w', 'f_conv_b', 'f_w_down', 'final_norm_g']
SHARD_AXIS = {'a_norm_g': 1, 'a_w_in': 2, 'a_b_if': None, 'a_hnorm_g': 2, 'a_w_out': 1, 'kv_norm_g': None,
              'w_kv': 1, 'b_norm_g': None, 'b_w_q': 2, 'b_w_out': 1, 'rel_bias': None, 'f_norm_g': None,
              'f_w_up': 2, 'f_conv_w': 2, 'f_conv_b': None, 'f_w_down': 1, 'final_norm_g': None}
BIG = ['a_w_in', 'a_w_out', 'w_kv', 'b_w_q', 'b_w_out', 'f_w_up', 'f_w_down']
SMALL = [n for n in WEIGHTS if n not in BIG]
LANES = 1024
PIECES = {'a_w_in': ('a_w_in', None, 2), 'a_w_out': ('a_w_out', None, 1), 'f_w_up0': ('f_w_up', 0, 1),
          'f_w_down0': ('f_w_down', 0, 0), 'w_kv': ('w_kv', None, 1), 'b_w_q': ('b_w_q', None, 2),
          'b_w_out': ('b_w_out', None, 1), 'f_w_up1': ('f_w_up', 1, 1), 'f_w_down1': ('f_w_down', 1, 0)}
LATE = ['w_kv', 'b_w_q', 'b_w_out', 'f_w_up1', 'f_w_down1']
WEIGHT_WAVES = {'first': ['a_w_in', 'a_w_out'], 'ffn0': ['f_w_up0', 'f_w_down0'], 'late': LATE}
GRAD_WAVES = {'late': LATE, 'layer0': ['f_w_up0', 'f_w_down0', 'a_w_out'], 'last': ['a_w_in']}


def _piece(arrays, p):
    leaf, layer, _ = PIECES[p]
    return arrays[leaf] if layer is None else arrays[leaf][layer]


class Packer:
    def __init__(self, pieces, shard):
        self.pieces = pieces
        self.shapes = [_piece(shard, p).shape for p in pieces]
        self.sizes = [math.prod(s) // (2 * LANES) for s in self.shapes]
        self.fill = -sum(self.sizes) % 16
        self.rows = sum(self.sizes) + self.fill

    def my_half(self, shard, half):
        both = jnp.concatenate([_piece(shard, p).astype(MM_DTYPE).reshape(2, -1, LANES) for p in self.pieces], axis=1)
        return jnp.pad(lax.dynamic_index_in_dim(both, half, axis=0, keepdims=False), ((0, self.fill), (0, 0)))

    def full_weights(self, gathered):
        g = gathered.reshape(4, 2, self.rows, LANES)
        out, off = {}, 0
        for p, shp, sz in zip(self.pieces, self.shapes, self.sizes):
            out[p] = _full_from_shards(g[:, :, off:off + sz].reshape((4,) + shp), PIECES[p][2])
            off += sz
        return out

    def grad_slots(self, grads):
        parts = [_shards_from_full(grads[p], PIECES[p][2]).reshape(4, 2, -1, LANES).astype(GRAD_WIRE_DTYPE)
                 for p in self.pieces]
        parts.append(jnp.zeros((4, 2, self.fill, LANES), GRAD_WIRE_DTYPE))
        return jnp.concatenate(parts, axis=2).reshape(8, self.rows, LANES)

    def shard_grads(self, both):
        out, off = {}, 0
        for p, shp, sz in zip(self.pieces, self.shapes, self.sizes):
            out[p] = both[:, off:off + sz].reshape(shp).astype(F32)
            off += sz
        return out


class Overlap:
    def __init__(self, shard, half):
        self.shard, self.half = shard, half
        self.weights = {w: Packer(p, shard) for w, p in WEIGHT_WAVES.items()}
        self.grads = {w: Packer(p, shard) for w, p in GRAD_WAVES.items()}
        self.shard_grads = {}

    def gather_ride(self, wave):
        mine = self.weights[wave].my_half(self.shard, self.half)
        return mine, Exchange(mine, GROUP_ALL, scatter=False)

    def gathered(self, wave, slots):
        return self.weights[wave].full_weights(slots)

    def scatter_ride(self, wave, grads):
        slots = self.grads[wave].grad_slots({p: grads.pop(p) for p in GRAD_WAVES[wave]})
        return slots, Exchange(slots, GROUP_ALL, scatter=True)

    def join_ride(self, wave, received):
        reduced = sum_slots(received, f"sum_grads_{wave}", GRAD_WIRE_DTYPE)
        return reduced, Exchange(reduced, GROUP_SIBLING, scatter=False)

    def joined(self, wave, both):
        self.shard_grads.update(self.grads[wave].shard_grads(both))


def _pad_rows(flat, mult):
    n = flat.shape[0]
    per = LANES * mult
    tot = -(-n // per) * per
    return jnp.pad(flat, (0, tot - n)).reshape(tot // LANES, LANES)


def _full_from_shards(sh, axis):
    shp = sh.shape[1:]
    return jnp.moveaxis(sh, 0, axis).reshape(shp[:axis] + (4 * shp[axis],) + shp[axis + 1:])


def _shards_from_full(full, axis):
    shp = full.shape
    return jnp.moveaxis(full.reshape(shp[:axis] + (4, shp[axis] // 4) + shp[axis + 1:]), axis, 0)


def _local_step(x, target, W, overlap=None):
    T = x.shape[0]
    W = dict(W)
    row = lambda a: a.reshape(1, -1).astype(F32)
    w_in = jnp.pad(W['a_w_in'][0], ((0, 0), (0, A_IN_PAD - A_IN)))
    bias128 = jnp.pad(row(W['a_b_if'][0]), ((0, 0), (0, 120)))
    hng = row(W['a_hnorm_g'][0])
    w_up = lambda l: _interleave(W[f'f_w_up{l}'])
    cw = [_interleave(W['f_conv_w'][l].astype(F32)) for l in range(2)]
    cb = [_interleave(row(W['f_conv_b'][l])) for l in range(2)]
    onehots = [(jnp.asarray(_group_bucket(g).reshape(-1, 1)) == jnp.arange(128)[None, :]).astype(F32)
               for g in range(N_GROUPS)]
    rb_t = jnp.pad(W['rel_bias'].astype(F32).T, ((0, 0), (0, 128 - REL_BUCKETS)))
    biases = [mm_nn(rb_t[g * B_HEADS:(g + 1) * B_HEADS], onehots[g].T, f"rel_bias_table_g{g}", exact=True)
              .reshape(B_HEADS, B_BLOCK, 2 * B_BLOCK) for g in range(N_GROUPS)]
    G = {}

    def ffn_fwd(xin, l, ride=None):
        xn, = rms_fwd(xin, [row(W['f_norm_g'][l])], f"ffn{l}_norm")
        u, act, *rode = ffn_up_act(xn, w_up(l), cw[l], cb[l], f"ffn{l}_up_act", ride)
        return mm_nn(act, W[f'f_w_down{l}'], f"ffn{l}_down", res=xin), (xn, u, act), rode

    def ffn_bwd(xin, saved, dout, l, ride=None):
        xn, u, act = saved
        dact = mm_nn(dout, W[f'f_w_down{l}'].T, f"ffn{l}_ddown")
        G[f'f_w_down{l}'] = mm_tn(act, dout, f"ffn{l}_gdown")
        du, gcw, gcb, *rode = conv_act_bwd(u, dact, cw[l], cb[l], f"ffn{l}_dact", ride)
        dxn = mm_nn(du, w_up(l).T, f"ffn{l}_dup")
        G[f'f_w_up{l}'] = _deinterleave(mm_tn(xn, du, f"ffn{l}_gup"))
        dxin, (gn,) = rms_bwd(xin, dout, [(dxn, row(W['f_norm_g'][l]))], f"ffn{l}_dnorm")
        return dxin, _deinterleave(gcw), _deinterleave(gcb), gn, rode

    xn_a, = rms_fwd(x, [row(W['a_norm_g'][0])], "a_norm")
    z = mm_nn(xn_a, w_in, "a_in")
    gcol, grow = gate_prep(z, bias128)
    hg, Cs, ns, ms, *rode = mlstm_fwd(z, gcol, grow, hng, overlap.gather_ride('ffn0') if overlap else None)
    if overlap:
        W.update(overlap.gathered('ffn0', rode[0]))
    x1 = mm_nn(hg, W['a_w_out'][0], "a_out", res=x)
    x2, ffn0, rode = ffn_fwd(x1, 0, overlap.gather_ride('late') if overlap else None)
    if overlap:
        W.update(overlap.gathered('late', rode[0]))
    xn_kv, xn_b = rms_fwd(x2, [row(W['kv_norm_g']), row(W['b_norm_g'][0])], "b_norms")
    gcols = lambda w, c: w[:, c * 1024:(c + 1) * 1024]
    qv = [mm_view(xn_b, gcols(W['b_w_q'][0], g), f"q_proj_g{g}", DILATIONS[g]) for g in range(N_GROUPS)]
    kvw = [mm_view(xn_kv, gcols(W['w_kv'], g), f"k_proj_g{g}", DILATIONS[g]) for g in range(N_GROUPS)]
    vvw = [mm_view(xn_kv, gcols(W['w_kv'], 3 + g), f"v_proj_g{g}", DILATIONS[g]) for g in range(N_GROUPS)]
    os_, lses = zip(*[attn_fwd(qv[g], kvw[g], vvw[g], biases[g], g) for g in range(N_GROUPS)])
    att, att_f, *lse_v = attn_merge(os_, lses)
    x3 = mm_nn(att, W['b_w_out'][0], "b_out", res=x2)
    x4, ffn1, _ = ffn_fwd(x3, 1)
    dx4, g_final, loss = loss_head(x4, target, row(W['final_norm_g']))
    G['final_norm_g'] = g_final.reshape(-1)

    dx3, gcw1, gcb1, gn1, _ = ffn_bwd(x3, ffn1, dx4, 1)
    datt = mm_nn(dx3, W['b_w_out'][0].T, "b_dout")
    G['b_w_out'] = mm_tn(att, dx3, "b_gout")[None]
    prep = attn_prep(datt, att_f)
    do_v, dl_v = prep[:3], prep[3:]
    parts = [attn_bwd(qv[g], kvw[g], vvw[g], biases[g], do_v[g], lse_v[g], dl_v[g], g) for g in range(N_GROUPS)]
    dq_all, dkv = attn_combine(parts)
    grb = []
    for g in range(N_GROUPS):
        gb = mm_nn(parts[g][5].reshape(B_HEADS, -1), onehots[g], f"rel_bias_g{g}", exact=True)
        grb.append(gb[:, :REL_BUCKETS].T)
    G['rel_bias'] = jnp.concatenate(grb, axis=1)
    dxn_b = mm_nn(dq_all, W['b_w_q'][0].T, "q_dproj")
    G['b_w_q'] = mm_tn(xn_b, dq_all, "q_gproj")[None]
    dxn_kv = mm_nn(dkv, W['w_kv'].T, "kv_dproj")
    G['w_kv'] = mm_tn(xn_kv, dkv, "kv_gproj")
    dx2, (g_kvn, g_bn) = rms_bwd(x2, dx3, [(dxn_kv, row(W['kv_norm_g'])), (dxn_b, row(W['b_norm_g'][0]))],
                                 "b_dnorms")
    G['kv_norm_g'] = g_kvn.reshape(-1)
    G['b_norm_g'] = g_bn
    dx1, gcw0, gcb0, gn0, late_slots = ffn_bwd(x1, ffn0, dx2, 0, overlap.scatter_ride('late', G) if overlap else None)
    G['f_conv_w'] = jnp.stack([gcw0, gcw1])
    G['f_conv_b'] = jnp.concatenate([gcb0, gcb1], axis=0)
    G['f_norm_g'] = jnp.concatenate([gn0, gn1], axis=0)
    dhg = mm_nn(dx1, W['a_w_out'][0].T, "a_dout")
    G['a_w_out'] = mm_tn(hg, dx1, "a_gout")[None]
    dz, g_hn, g_bif, *layer0_slots = mlstm_bwd(z, gcol, grow, hng, bias128, Cs, ns, ms, dhg,
                                               overlap.scatter_ride('layer0', G) if overlap else None)
    G['a_hnorm_g'] = g_hn.reshape(1, A_HEADS, A_V)
    G['a_b_if'] = g_bif[:, :2 * A_HEADS]
    if overlap:
        dxn_a, both = mm_nn(dz, w_in.T, "a_din", ride=overlap.join_ride('late', late_slots[0]))
        overlap.joined('late', both)
        g_in, both = mm_tn(xn_a, dz, "a_gin", ride=overlap.join_ride('layer0', layer0_slots[0]))
        overlap.joined('layer0', both)
    else:
        dxn_a = mm_nn(dz, w_in.T, "a_din")
        g_in = mm_tn(xn_a, dz, "a_gin")
    G['a_w_in'] = g_in[:, :A_IN][None]
    grad_x, (g_an,) = rms_bwd(x, dx1, [(dxn_a, row(W['a_norm_g'][0]))], "a_dnorm")
    G['a_norm_g'] = g_an
    return loss, grad_x, G


def kernel(x, a_norm_g, a_w_in, a_b_if, a_hnorm_g, a_w_out, kv_norm_g, w_kv, b_norm_g, b_w_q, b_w_out, rel_bias, f_norm_g, f_w_up, f_conv_w, f_conv_b, f_w_down, final_norm_g, loss_target, m_a_norm_g, m_a_w_in, m_a_b_if, m_a_hnorm_g, m_a_w_out, m_kv_norm_g, m_w_kv, m_b_norm_g, m_b_w_q, m_b_w_out, m_rel_bias, m_f_norm_g, m_f_w_up, m_f_conv_w, m_f_conv_b, m_f_w_down, m_final_norm_g, v_a_norm_g, v_a_w_in, v_a_b_if, v_a_hnorm_g, v_a_w_out, v_kv_norm_g, v_w_kv, v_b_norm_g, v_b_w_q, v_b_w_out, v_rel_bias, v_f_norm_g, v_f_w_up, v_f_conv_w, v_f_conv_b, v_f_w_down, v_final_norm_g):
    given = dict(locals())
    shard = {n: given[n] for n in WEIGHTS}
    mom = {n: given["m_" + n] for n in WEIGHTS}
    var = {n: given["v_" + n] for n in WEIGHTS}
    cx, cy, cc = _me()
    chip = 2 * cx + cy

    overlap = Overlap(shard, cc)
    mine, gather = overlap.gather_ride('first')
    W = overlap.gathered('first', gather(mine, "gather_weights"))
    sharded_small = [n for n in SMALL if SHARD_AXIS[n] is not None]
    ssz = [shard[n].size for n in sharded_small]
    sflat = jnp.concatenate([shard[n].reshape(-1) for n in sharded_small])
    sg = group_gather(_pad_rows(sflat, 8), "gather_small", GROUP_CHIPS).reshape(4, -1)
    off = 0
    for n, sz in zip(sharded_small, ssz):
        W[n] = _full_from_shards(sg[:, off:off + sz].reshape((4,) + shard[n].shape), SHARD_AXIS[n])
        off += sz
    for n in SMALL:
        if SHARD_AXIS[n] is None:
            W[n] = shard[n]

    loss_row, grad_x, G = _local_step(x[0], loss_target[0], W, overlap)

    slots, scatter = overlap.scatter_ride('last', G)
    reduced, join = overlap.join_ride('last', scatter(slots, "scatter_grads"))
    overlap.joined('last', join(reduced, "join_halves"))
    by_piece = overlap.shard_grads
    gsh = {}
    for n in BIG:
        layers = [p for p in PIECES if PIECES[p][0] == n]
        gsh[n] = by_piece[n] if layers == [n] else jnp.stack([by_piece[p] for p in layers])
    small_parts = [loss_row[0, 0:1]] + [G[n].reshape(-1) for n in SMALL]
    small_sz = [p.shape[0] for p in small_parts]
    small = sum_slots(group_gather(_pad_rows(jnp.concatenate(small_parts), 8), "gather_small_grads", GROUP_ALL),
                      "sum_small_grads").reshape(-1)
    loss = small[0]
    off = 1
    for n, sz in zip(SMALL, small_sz[1:]):
        full = small[off:off + sz].reshape(W[n].shape)
        off += sz
        if SHARD_AXIS[n] is None:
            gsh[n] = full
        else:
            gsh[n] = lax.dynamic_index_in_dim(_shards_from_full(full, SHARD_AXIS[n]), chip, 0, keepdims=False)

    delta, new_m, new_v = {}, {}, {}
    for n in WEIGHTS:
        shp = shard[n].shape
        two = lambda a: a.reshape(-1, shp[-1])
        d, nm, nv = adamw(two(shard[n]), two(gsh[n]), two(mom[n]), two(var[n]), f"adamw_{n}")
        delta[n], new_m[n], new_v[n] = d.reshape(shp), nm.reshape(shp), nv.reshape(shp)
    return (loss, grad_x[None], *[gsh[n] for n in WEIGHTS], *[delta[n] for n in WEIGHTS],
            *[new_m[n] for n in WEIGHTS], *[new_v[n] for n in WEIGHTS])
```

```python
import functools
import math

import numpy as np
import jax
import jax.numpy as jnp
from jax import lax
from jax.experimental import pallas as pl
from jax.experimental.pallas import tpu as pltpu

F32 = jnp.float32
BF16 = jnp.bfloat16
MM_DTYPE = jnp.bfloat16
GRAD_WIRE_DTYPE = jnp.bfloat16
HI = lax.Precision.HIGHEST

D_MODEL = 1024
A_HEADS = 4
A_QK = 128
A_V = 256
A_CHUNK = 256
A_IN = 3080
A_IN_PAD = 3200
GATE_COL = 3072
SOFTCAP = 15.0
N_GROUPS = 3
B_HEADS = 16
B_DH = 64
B_BLOCK = 128
DILATIONS = (1, 4, 16)
WINDOWS = (128, 512, 2048)
REL_BUCKETS = 32
REL_MAX_DIST = 2048
D_FF = 2816
FF_TC = 256
EPS = 1e-6
ADAM_LR, ADAM_B1, ADAM_B2, ADAM_EPS, ADAM_WD, ADAM_STEP = 0.001, 0.9, 0.999, 1e-08, 0.01, 10

VMEM_LIMIT = 56 * 1024 * 1024
NT_DIMS = (((1,), (1,)), ((), ()))
TN_DIMS = (((0,), (0,)), ((), ()))
MESH_ID = pl.DeviceIdType.MESH


def _params(*sem):
    return pltpu.CompilerParams(dimension_semantics=sem, vmem_limit_bytes=VMEM_LIMIT)


def _tile(n, cap):
    if n <= cap:
        return n
    best = None
    for t in range(128, cap + 1, 128):
        if n % t == 0:
            best = t
    assert best is not None, (n, cap)
    return best


def _rows(n, cap):
    if n <= cap:
        return n
    for t in range(cap // 8 * 8, 7, -8):
        if n % t == 0:
            return t
    raise ValueError((n, cap))


def _dot(a, b):
    return jnp.dot(a.astype(MM_DTYPE), b.astype(MM_DTYPE), preferred_element_type=F32)


def _dot_nt(a, b):
    return lax.dot_general(a.astype(MM_DTYPE), b.astype(MM_DTYPE), NT_DIMS, preferred_element_type=F32)


def _dot_tn(a, b):
    return lax.dot_general(a.astype(MM_DTYPE), b.astype(MM_DTYPE), TN_DIMS, preferred_element_type=F32)


def _sigmoid(x):
    return 1.0 / (1.0 + jnp.exp(-x))


def _sigmoid_tanh(x):
    return 0.5 * jnp.tanh(0.5 * x) + 0.5


def mm_nn(a, b, name, res=None, out_dtype=F32, exact=False, ride=None):
    M, K = a.shape
    N = b.shape[1]
    tm = _rows(M, 512)

    def footprint(tn):
        return 2 * (tm * K * a.dtype.itemsize + K * tn * b.dtype.itemsize) + 2 * tm * tn * 4 * (1 if res is None else 2)

    budget = 46 * 1024 * 1024
    tn = N if N <= 3328 and footprint(N) <= budget else _tile(N, 1536)
    tk = K if footprint(tn) <= budget else _tile(K, 1536)
    nk = K // tk

    def body(*refs):
        if res is None:
            a_ref, b_ref, o_ref, acc = refs
            r_ref = None
        else:
            a_ref, b_ref, r_ref, o_ref, acc = refs
        if exact:
            p = jnp.dot(a_ref[...], b_ref[...], precision=HI, preferred_element_type=F32)
        else:
            p = _dot(a_ref[...], b_ref[...])

        def finish(total):
            if r_ref is not None:
                total = total + r_ref[...]
            o_ref[...] = total.astype(out_dtype)

        if nk == 1:
            finish(p)
        else:
            k = pl.program_id(2)

            @pl.when(k == 0)
            def _():
                acc[...] = p

            @pl.when(jnp.logical_and(k > 0, k < nk - 1))
            def _():
                acc[...] += p

            @pl.when(k == nk - 1)
            def _():
                finish(acc[...] + p)

    in_specs = [pl.BlockSpec((tm, tk), lambda j, i, k: (i, k)),
                pl.BlockSpec((tk, tn), lambda j, i, k: (k, j))]
    args = [a, b]
    if res is not None:
        in_specs.append(pl.BlockSpec((tm, tn), lambda j, i, k: (i, j)))
        args.append(res)
    acc_shape = (tm, tn) if nk > 1 else (8, 128)
    outs = _call(body, name, (N // tn, M // tm, nk), in_specs, [pl.BlockSpec((tm, tn), lambda j, i, k: (i, j))],
                 [jax.ShapeDtypeStruct((M, N), out_dtype)], [pltpu.VMEM(acc_shape, F32)],
                 ("parallel", "parallel", "arbitrary"), args, ride)
    return outs[0] if ride is None else outs


def mm_view(a, b, name, dil):
    T, K = a.shape
    tm = 512

    def body(a_ref, b_ref, o_ref, sc):
        p = _dot(a_ref[...], b_ref[...])
        if dil == 1:
            o_ref[...] = p.astype(o_ref.dtype)
        else:
            _to_view(lambda c: p[:, c * 128:(c + 1) * 128], sc, o_ref, dil, 8, tm)

    return pl.pallas_call(
        body, name=name, grid=(T // tm,),
        in_specs=[pl.BlockSpec((tm, K), lambda i: (i, 0)), pl.BlockSpec((K, 1024), lambda i: (0, 0))],
        out_specs=pl.BlockSpec((tm // dil, dil * 1024), lambda i: (i, 0)),
        out_shape=jax.ShapeDtypeStruct((T // dil, dil * 1024), MM_DTYPE),
        scratch_shapes=[pltpu.VMEM((8, tm, 128), F32)],
        compiler_params=_params("parallel"),
    )(a, b)


def mm_tn(a, g, name, ride=None):
    T, Ka = a.shape
    N = g.shape[1]
    tka, tt = _tile(Ka, 1536), _rows(T, 1024)
    whole_n = 2 * (tt * tka * a.dtype.itemsize + tt * N * g.dtype.itemsize + tka * N * 4)
    tn = N if N <= 3328 and whole_n <= 46 * 1024 * 1024 else _tile(N, 1536)
    nt = T // tt

    def body(a_ref, g_ref, o_ref):
        t = pl.program_id(2)
        p = _dot_tn(a_ref[...], g_ref[...])

        @pl.when(t == 0)
        def _():
            o_ref[...] = p

        @pl.when(t > 0)
        def _():
            o_ref[...] += p

    outs = _call(body, name, (Ka // tka, N // tn, nt),
                 [pl.BlockSpec((tt, tka), lambda i, j, t: (t, i)), pl.BlockSpec((tt, tn), lambda i, j, t: (t, j))],
                 [pl.BlockSpec((tka, tn), lambda i, j, t: (i, j))], [jax.ShapeDtypeStruct((Ka, N), F32)], [],
                 ("parallel", "parallel", "arbitrary"), (a, g), ride)
    return outs[0] if ride is None else outs


def rms_fwd(x, gains, name):
    T, D = x.shape
    tt = _rows(T, 512)
    ng = len(gains)

    def body(*refs):
        x_ref = refs[0]
        g_refs = refs[1:1 + ng]
        o_refs = refs[1 + ng:]
        xf = x_ref[...]
        y = xf * lax.rsqrt(jnp.mean(xf * xf, axis=-1, keepdims=True) + EPS)
        for g_ref, o_ref in zip(g_refs, o_refs):
            o_ref[...] = (y * g_ref[...]).astype(o_ref.dtype)

    row = pl.BlockSpec((tt, D), lambda i: (i, 0))
    gsp = pl.BlockSpec((1, D), lambda i: (0, 0))
    return pl.pallas_call(
        body, name=name, grid=(T // tt,),
        in_specs=[row] + [gsp] * ng, out_specs=[row] * ng,
        out_shape=[jax.ShapeDtypeStruct((T, D), MM_DTYPE)] * ng,
        compiler_params=_params("parallel"),
    )(x, *gains)


def rms_bwd(x, dres, branches, name):
    T, D = x.shape
    tt = _rows(T, 512)
    nb = len(branches)

    def body(*refs):
        x_ref, r_ref = refs[0], refs[1]
        dy_refs = refs[2:2 + nb]
        g_refs = refs[2 + nb:2 + 2 * nb]
        dx_ref = refs[2 + 2 * nb]
        dg_refs = refs[3 + 2 * nb:]
        i = pl.program_id(0)
        xf = x_ref[...]
        r = lax.rsqrt(jnp.mean(xf * xf, axis=-1, keepdims=True) + EPS)
        xh = xf * r
        dx = r_ref[...]
        for dy_ref, g_ref, dg_ref in zip(dy_refs, g_refs, dg_refs):
            dy = dy_ref[...].astype(F32)
            dyg = dy * g_ref[...]
            dx = dx + r * (dyg - xh * jnp.mean(dyg * xh, axis=-1, keepdims=True))
            part = jnp.sum(dy * xh, axis=0, keepdims=True)

            @pl.when(i == 0)
            def _():
                dg_ref[...] = part

            @pl.when(i > 0)
            def _():
                dg_ref[...] += part
        dx_ref[...] = dx

    row = pl.BlockSpec((tt, D), lambda i: (i, 0))
    gsp = pl.BlockSpec((1, D), lambda i: (0, 0))
    outs = pl.pallas_call(
        body, name=name, grid=(T // tt,),
        in_specs=[row, row] + [row] * nb + [gsp] * nb,
        out_specs=[row] + [gsp] * nb,
        out_shape=[jax.ShapeDtypeStruct((T, D), F32)] + [jax.ShapeDtypeStruct((1, D), F32)] * nb,
        compiler_params=_params("arbitrary"),
    )(x, dres, *[b[0] for b in branches], *[b[1] for b in branches])
    return outs[0], outs[1:]


def loss_head(x, target, gain):
    T, D = x.shape
    tt = _rows(T, 512)

    def body(x_ref, t_ref, g_ref, dx_ref, dg_ref, loss_ref):
        i = pl.program_id(0)
        xf = x_ref[...]
        g = g_ref[...]
        r = lax.rsqrt(jnp.mean(xf * xf, axis=-1, keepdims=True) + EPS)
        xh = xf * r
        e = xh * g - t_ref[...]
        lpart = 0.5 * jnp.sum(jnp.sum(e * e, axis=1, keepdims=True), axis=0, keepdims=True) / D
        dy = e / D
        dyg = dy * g
        dx_ref[...] = r * (dyg - xh * jnp.mean(dyg * xh, axis=-1, keepdims=True))
        gpart = jnp.sum(dy * xh, axis=0, keepdims=True)
        lrow = jnp.broadcast_to(lpart, (1, 128))

        @pl.when(i == 0)
        def _():
            dg_ref[...] = gpart
            loss_ref[...] = lrow

        @pl.when(i > 0)
        def _():
            dg_ref[...] += gpart
            loss_ref[...] += lrow

    row = pl.BlockSpec((tt, D), lambda i: (i, 0))
    gsp = pl.BlockSpec((1, D), lambda i: (0, 0))
    return pl.pallas_call(
        body, name="loss_head", grid=(T // tt,),
        in_specs=[row, row, gsp],
        out_specs=[row, gsp, pl.BlockSpec((1, 128), lambda i: (0, 0))],
        out_shape=[jax.ShapeDtypeStruct((T, D), F32), jax.ShapeDtypeStruct((1, D), F32),
                   jax.ShapeDtypeStruct((1, 128), F32)],
        compiler_params=_params("arbitrary"),
    )(x, target, gain)


def _shift_down(u, prev8, first, k):
    rolled = pltpu.roll(u, k, 0)
    rid = lax.broadcasted_iota(jnp.int32, u.shape, 0)
    halo = jnp.where(first, 0.0, prev8)
    out = rolled
    for j in range(k):
        out = jnp.where(rid == j, halo[8 - k + j:8 - k + j + 1, :], out)
    return out


def _conv3(u, prev8, first, w, b):
    return (_shift_down(u, prev8, first, 2) * w[0:1, :] + _shift_down(u, prev8, first, 1) * w[1:2, :]
            + u * w[2:3, :] + b)


def ffn_up_act(xn, w_up, w, b, name, ride=None):
    T, K = xn.shape
    tt = _rows(T, 1024)
    nj = D_FF // FF_TC

    def body(x_ref, wu_ref, w_ref, b_ref, u_ref, o_ref, tail):
        first = pl.program_id(1) == 0
        u = _dot(x_ref[...], wu_ref[...])
        u_ref[...] = u
        c = _conv3(u, tail[...], first, w_ref[...], b_ref[...])
        tail[...] = u[tt - 8:, :]
        cg, cv = c[:, :FF_TC], c[:, FF_TC:]
        o_ref[...] = (cg * _sigmoid_tanh(cg) * cv).astype(o_ref.dtype)

    return _call(
        body, name, (nj, T // tt),
        [pl.BlockSpec((tt, K), lambda j, i: (i, 0)),
         pl.BlockSpec((K, 2 * FF_TC), lambda j, i: (0, j)),
         pl.BlockSpec((3, 2 * FF_TC), lambda j, i: (0, j)),
         pl.BlockSpec((1, 2 * FF_TC), lambda j, i: (0, j))],
        [pl.BlockSpec((tt, 2 * FF_TC), lambda j, i: (i, j)), pl.BlockSpec((tt, FF_TC), lambda j, i: (i, j))],
        [jax.ShapeDtypeStruct((T, 2 * D_FF), F32), jax.ShapeDtypeStruct((T, D_FF), MM_DTYPE)],
        [pltpu.VMEM((8, 2 * FF_TC), F32)], ("parallel", "arbitrary"), (xn, w_up, w, b), ride)


def conv_act_bwd(u, da, w, b, name, ride=None):
    T = u.shape[0]
    tt = _rows(T, 1024)
    nt = T // tt
    nj = D_FF // FF_TC
    te = tt + 8

    def body(u_ref, p_ref, n_ref, da_ref, dan_ref, w_ref, b_ref, du_ref, dw_ref, db_ref):
        i = pl.program_id(1)
        first = i == 0
        last = i == nt - 1
        w = w_ref[...]
        ue = jnp.concatenate([u_ref[...], n_ref[...]], axis=0)
        dae = jnp.concatenate([da_ref[...], jnp.where(last, 0.0, dan_ref[...])], axis=0)
        um2 = _shift_down(ue, p_ref[...], first, 2)
        um1 = _shift_down(ue, p_ref[...], first, 1)
        c = um2 * w[0:1, :] + um1 * w[1:2, :] + ue * w[2:3, :] + b_ref[...]
        cg, cv = c[:, :FF_TC], c[:, FF_TC:]
        s = _sigmoid_tanh(cg)
        dcg = dae * cv * (s * (1.0 + cg * (1.0 - s)))
        dcv = dae * (cg * s)
        dc = jnp.concatenate([dcg, dcv], axis=1)
        du = (dc * w[2:3, :] + pltpu.roll(dc, te - 1, 0) * w[1:2, :] + pltpu.roll(dc, te - 2, 0) * w[0:1, :])
        du_ref[...] = du[:tt, :].astype(du_ref.dtype)
        dcm = dc[:tt, :]
        dwp = jnp.concatenate([jnp.sum(dcm * um2[:tt, :], axis=0, keepdims=True),
                               jnp.sum(dcm * um1[:tt, :], axis=0, keepdims=True),
                               jnp.sum(dcm * ue[:tt, :], axis=0, keepdims=True)], axis=0)
        dbp = jnp.sum(dcm, axis=0, keepdims=True)

        @pl.when(first)
        def _():
            dw_ref[...] = dwp
            db_ref[...] = dbp

        @pl.when(i > 0)
        def _():
            dw_ref[...] += dwp
            db_ref[...] += dbp

    nb8 = T // 8
    return _call(
        body, name, (nj, nt),
        [pl.BlockSpec((tt, 2 * FF_TC), lambda j, i: (i, j)),
         pl.BlockSpec((8, 2 * FF_TC), lambda j, i: (jnp.maximum(i * (tt // 8) - 1, 0), j)),
         pl.BlockSpec((8, 2 * FF_TC), lambda j, i: (jnp.minimum((i + 1) * (tt // 8), nb8 - 1), j)),
         pl.BlockSpec((tt, FF_TC), lambda j, i: (i, j)),
         pl.BlockSpec((8, FF_TC), lambda j, i: (jnp.minimum((i + 1) * (tt // 8), nb8 - 1), j)),
         pl.BlockSpec((3, 2 * FF_TC), lambda j, i: (0, j)),
         pl.BlockSpec((1, 2 * FF_TC), lambda j, i: (0, j))],
        [pl.BlockSpec((tt, 2 * FF_TC), lambda j, i: (i, j)),
         pl.BlockSpec((3, 2 * FF_TC), lambda j, i: (0, j)),
         pl.BlockSpec((1, 2 * FF_TC), lambda j, i: (0, j))],
        [jax.ShapeDtypeStruct((T, 2 * D_FF), MM_DTYPE), jax.ShapeDtypeStruct((3, 2 * D_FF), F32),
         jax.ShapeDtypeStruct((1, 2 * D_FF), F32)],
        [], ("parallel", "arbitrary"), (u, u, u, da, da, w, b), ride)


def _interleave(a):
    lead = a.shape[:-1]
    nj = D_FF // FF_TC
    return jnp.swapaxes(a.reshape(*lead, 2, nj, FF_TC), -3, -2).reshape(*lead, 2 * D_FF)


def _deinterleave(a):
    lead = a.shape[:-1]
    nj = D_FF // FF_TC
    return jnp.swapaxes(a.reshape(*lead, nj, 2, FF_TC), -3, -2).reshape(*lead, 2 * D_FF)


A_GC = 1
A_TB = A_GC * A_CHUNK


def gate_prep(z, bias128):
    T = z.shape[0]
    tt = _rows(T, 512)

    def body(z_ref, b_ref, gc_ref, gr_ref):
        pre = z_ref[...] + b_ref[...]
        sc = SOFTCAP * jnp.tanh(pre / SOFTCAP)
        lf = jnp.minimum(sc, 0.0) - jnp.log(1.0 + jnp.exp(-jnp.abs(sc)))
        col = lax.broadcasted_iota(jnp.int32, pre.shape, 1)
        isf = jnp.logical_and(col >= A_HEADS, col < 2 * A_HEADS)
        r = lax.broadcasted_iota(jnp.int32, (tt, tt), 0)
        c = lax.broadcasted_iota(jnp.int32, (tt, tt), 1)
        bits = A_CHUNK.bit_length() - 1
        tri = jnp.logical_and(jnp.right_shift(r, bits) == jnp.right_shift(c, bits), c <= r).astype(F32)
        bcum = jnp.dot(tri, jnp.where(isf, lf, 0.0), precision=HI, preferred_element_type=F32)
        g = jnp.where(col < A_HEADS, sc, jnp.where(isf, bcum, 0.0))
        gc_ref[...] = g
        for s in range(tt // 128):
            gr_ref[s] = g[s * 128:(s + 1) * 128, :].T[0:8, :]

    return pl.pallas_call(
        body, name="gate_prep", grid=(T // tt,),
        in_specs=[pl.BlockSpec((tt, 128), lambda i: (i, GATE_COL // 128)),
                  pl.BlockSpec((1, 128), lambda i: (0, 0))],
        out_specs=[pl.BlockSpec((tt, 128), lambda i: (i, 0)),
                   pl.BlockSpec((tt // 128, 8, 128), lambda i: (i, 0, 0))],
        out_shape=[jax.ShapeDtypeStruct((T, 128), F32), jax.ShapeDtypeStruct((T // 128, 8, 128), F32)],
        compiler_params=_params("parallel"),
    )(z, bias128)


def _chunk_decay(A, qh, bc, br, lir, n, m, causal):
    logD = jnp.where(causal, bc - br + lir, -jnp.inf)
    m_inter = bc + m
    m_t = jnp.maximum(m_inter, jnp.max(logD, axis=1, keepdims=True))
    E = jnp.exp(logD - m_t)
    Sm = A * E
    wi = jnp.exp(m_inter - m_t)
    qn = jnp.sum(qh.astype(F32) * n, axis=1, keepdims=True)
    den = jnp.sum(Sm, axis=1, keepdims=True) + wi * qn
    gs = jnp.maximum(jnp.abs(den), jnp.exp(-m_t))
    return E, Sm, wi, den, gs, m_t


def _state_weights(bc, lic, br, lir, m):
    bL = bc[A_CHUNK - 1:A_CHUNK, :]
    m_new = jnp.maximum(bL + m, jnp.max(bL - br + lir, axis=1, keepdims=True))
    wk = jnp.exp(bL - bc + lic - m_new)
    decay = jnp.exp(bL + m - m_new)
    return wk, decay, m_new


def _head_slices(h):
    return (slice(h * A_QK, (h + 1) * A_QK), slice(h * A_V, (h + 1) * A_V))


def mlstm_fwd(z, gcol, grow, hng, ride=None):
    T = z.shape[0]
    NC = T // A_CHUNK
    scale = A_QK ** -0.5

    def body(q_ref, k_ref, v_ref, o_ref, gc_ref, gr_ref, hng_ref, hg_ref, Cs_ref, ns_ref, ms_ref,
             C_sc, n_sc, m_sc):
        @pl.when(pl.program_id(0) == 0)
        def _():
            C_sc[...] = jnp.zeros_like(C_sc)
            n_sc[...] = jnp.zeros_like(n_sc)
            m_sc[...] = jnp.zeros_like(m_sc)

        ri = lax.broadcasted_iota(jnp.int32, (A_CHUNK, A_CHUNK), 0)
        ci = lax.broadcasted_iota(jnp.int32, (A_CHUNK, A_CHUNK), 1)
        causal = ri >= ci
        gr = jnp.concatenate([gr_ref[s] for s in range(A_TB // 128)], axis=1)
        for c in range(A_GC):
            rows = slice(c * A_CHUNK, (c + 1) * A_CHUNK)
            gc = gc_ref[rows, :]
            grc = gr[:, c * A_CHUNK:(c + 1) * A_CHUNK]
            for h in range(A_HEADS):
                sk, sv = _head_slices(h)
                qh = (q_ref[rows, sk] * scale).astype(MM_DTYPE)
                kh = k_ref[rows, sk].astype(MM_DTYPE)
                vh = v_ref[rows, sv].astype(MM_DTYPE)
                lic, bc = gc[:, h:h + 1], gc[:, A_HEADS + h:A_HEADS + h + 1]
                lir, br = grc[h:h + 1, :], grc[A_HEADS + h:A_HEADS + h + 1, :]
                C, n, m = C_sc[h], n_sc[h], m_sc[h][:, 0:1]
                Cs_ref[c, h] = C
                ns_ref[c, h] = n
                ms_ref[c, h] = m_sc[h]
                _, Sm, wi, _, gs, _ = _chunk_decay(_dot_nt(qh, kh), qh, bc, br, lir, n, m, causal)
                hh = (_dot(Sm, vh) + wi * _dot(qh, C)) / gs
                hn = hh * lax.rsqrt(jnp.mean(hh * hh, axis=1, keepdims=True) + EPS) * hng_ref[:, sv]
                hg_ref[rows, sv] = (hn * _sigmoid(o_ref[rows, sv])).astype(hg_ref.dtype)
                wk, decay, m_new = _state_weights(bc, lic, br, lir, m)
                kw = kh.astype(F32) * wk
                C_sc[h] = decay * C + _dot_tn(kw, vh)
                n_sc[h] = decay * n + jnp.sum(kw, axis=0, keepdims=True)
                m_sc[h] = jnp.broadcast_to(m_new, (1, 128))

    tok = lambda w, cb: pl.BlockSpec((A_TB, w), lambda i: (i, cb))
    return _call(
        body, "mlstm_fwd", (NC // A_GC,),
        [tok(512, 0), tok(512, 1), tok(1024, 1), tok(1024, 2),
         pl.BlockSpec((A_TB, 128), lambda i: (i, 0)),
         pl.BlockSpec((A_TB // 128, 8, 128), lambda i: (i, 0, 0)),
         pl.BlockSpec((1, 1024), lambda i: (0, 0))],
        [pl.BlockSpec((A_TB, 1024), lambda i: (i, 0)),
         pl.BlockSpec((A_GC, A_HEADS, A_QK, A_V), lambda i: (i, 0, 0, 0)),
         pl.BlockSpec((A_GC, A_HEADS, 1, 128), lambda i: (i, 0, 0, 0)),
         pl.BlockSpec((A_GC, A_HEADS, 1, 128), lambda i: (i, 0, 0, 0))],
        [jax.ShapeDtypeStruct((T, 1024), MM_DTYPE),
         jax.ShapeDtypeStruct((NC, A_HEADS, A_QK, A_V), F32),
         jax.ShapeDtypeStruct((NC, A_HEADS, 1, 128), F32),
         jax.ShapeDtypeStruct((NC, A_HEADS, 1, 128), F32)],
        [pltpu.VMEM((A_HEADS, A_QK, A_V), F32), pltpu.VMEM((A_HEADS, 1, 128), F32),
         pltpu.VMEM((A_HEADS, 1, 128), F32)],
        ("arbitrary",), (z, z, z, z, gcol, grow, hng), ride)


def mlstm_bwd(z, gcol, grow, hng, bias128, Cs, ns, ms, dhg, ride=None):
    T = z.shape[0]
    NC = T // A_CHUNK
    nsteps = NC // A_GC
    scale = A_QK ** -0.5

    def body(q_ref, k_ref, v_ref, o_ref, zg_ref, gc_ref, gr_ref, hng_ref, b_ref, Cs_ref, ns_ref, ms_ref,
             dhg_ref, dz_ref, dgn_ref, dbif_ref, dC_sc, dn_sc):
        @pl.when(pl.program_id(0) == 0)
        def _():
            dC_sc[...] = jnp.zeros_like(dC_sc)
            dn_sc[...] = jnp.zeros_like(dn_sc)
            dgn_ref[...] = jnp.zeros_like(dgn_ref)
            dbif_ref[...] = jnp.zeros_like(dbif_ref)

        ri = lax.broadcasted_iota(jnp.int32, (A_CHUNK, A_CHUNK), 0)
        ci = lax.broadcasted_iota(jnp.int32, (A_CHUNK, A_CHUNK), 1)
        causal = ri >= ci
        upper = (ci >= ri).astype(F32)
        rid = lax.broadcasted_iota(jnp.int32, (A_CHUNK, 1), 0)
        col = lax.broadcasted_iota(jnp.int32, (A_CHUNK, 128), 1)
        gr = jnp.concatenate([gr_ref[s] for s in range(A_TB // 128)], axis=1)
        for c in reversed(range(A_GC)):
            rows = slice(c * A_CHUNK, (c + 1) * A_CHUNK)
            gc = gc_ref[rows, :]
            grc = gr[:, c * A_CHUNK:(c + 1) * A_CHUNK]
            dG = jnp.zeros((A_CHUNK, 128), F32)
            hs = []
            for h in range(A_HEADS):
                sk, sv = _head_slices(h)
                s = dict(sk=sk, sv=sv, qh=(q_ref[rows, sk] * scale).astype(MM_DTYPE),
                         kh=k_ref[rows, sk].astype(MM_DTYPE), vh=v_ref[rows, sv].astype(MM_DTYPE),
                         lic=gc[:, h:h + 1], bc=gc[:, A_HEADS + h:A_HEADS + h + 1],
                         lir=grc[h:h + 1, :], br=grc[A_HEADS + h:A_HEADS + h + 1, :],
                         C=Cs_ref[c, h], n=ns_ref[c, h], m=ms_ref[c, h][:, 0:1], dC=dC_sc[h], dn=dn_sc[h])
                s['qf'], s['kf'] = s['qh'].astype(F32), s['kh'].astype(F32)
                s['wk'], s['decay'], _ = _state_weights(s['bc'], s['lic'], s['br'], s['lir'], s['m'])
                hs.append(s)
            for s in hs:
                s['A'] = _dot_nt(s['qh'], s['kh'])
                s['qC'] = _dot(s['qh'], s['C'])
                s['vdC'] = _dot_nt(s['vh'], s['dC'])
                s['kdC'] = _dot(s['kh'], s['dC'])
            for s in hs:
                s['E'], s['Sm'], s['wi'], s['den'], s['gs'], s['m_t'] = _chunk_decay(
                    s['A'], s['qh'], s['bc'], s['br'], s['lir'], s['n'], s['m'], causal)
            for s in hs:
                s['num'] = _dot(s['Sm'], s['vh']) + s['wi'] * s['qC']
            for h, s in enumerate(hs):
                sv, gs = s['sv'], s['gs']
                hh = s['num'] / gs
                r = lax.rsqrt(jnp.mean(hh * hh, axis=1, keepdims=True) + EPS)
                gn = hng_ref[:, sv]
                sg = _sigmoid(o_ref[rows, sv])
                dhg_h = dhg_ref[rows, sv]
                dhn = dhg_h * sg
                dz_ref[rows, 2048 + h * A_V:2048 + (h + 1) * A_V] = (
                    dhg_h * (hh * r * gn) * sg * (1.0 - sg)).astype(dz_ref.dtype)
                dgn_ref[:, sv] += jnp.sum(dhn * hh * r, axis=0, keepdims=True)
                dyg = dhn * gn
                dh = r * dyg - hh * (r * r * r) * jnp.mean(dyg * hh, axis=1, keepdims=True)
                s['dnum'] = dh / gs
                live = (jnp.abs(s['den']) > jnp.exp(-s['m_t'])).astype(F32)
                s['dden'] = -jnp.sum(dh * hh, axis=1, keepdims=True) / gs * jnp.sign(s['den']) * live
            for s in hs:
                s['dnv'] = _dot_nt(s['dnum'], s['vh'])
                s['dnC'] = _dot_nt(s['dnum'], s['C'])
            for s in hs:
                s['dSE'] = jnp.where(causal, s['dnv'] + s['dden'], 0.0) * s['E']
            for s in hs:
                s['dq'] = _dot(s['dSE'], s['kh']) + s['wi'] * (s['dnC'] + s['dden'] * s['n'])
                s['dk_inter'] = s['wk'] * (s['vdC'] + s['dn'])
                s['dk'] = _dot_tn(s['dSE'], s['qh']) + s['dk_inter']
                s['dv'] = _dot_tn(s['Sm'], s['dnum']) + s['wk'] * s['kdC']
                s['dCq'] = _dot_tn(s['qf'] * s['wi'], s['dnum'])
            for h, s in enumerate(hs):
                dq, dk, qf, kf, dC, dn = s['dq'], s['dk'], s['qf'], s['kf'], s['dC'], s['dn']
                dz_ref[rows, s['sk']] = (dq * scale).astype(dz_ref.dtype)
                dz_ref[rows, 512 + h * A_QK:512 + (h + 1) * A_QK] = dk.astype(dz_ref.dtype)
                dz_ref[rows, 1024 + h * A_V:1024 + (h + 1) * A_V] = s['dv'].astype(dz_ref.dtype)
                dli = jnp.sum(kf * dk, axis=1, keepdims=True)
                db = jnp.sum(qf * dq, axis=1, keepdims=True) - dli
                usum = jnp.sum(jnp.sum(kf * s['dk_inter'], axis=1, keepdims=True), axis=0, keepdims=True)
                ddecay = (jnp.sum(jnp.sum(dC * s['C'], axis=1, keepdims=True), axis=0, keepdims=True)
                          + jnp.sum(dn * s['n'], axis=1, keepdims=True))
                db = db + jnp.where(rid == A_CHUNK - 1, usum + ddecay * s['decay'], 0.0)
                dG = dG + jnp.where(col == h, dli, 0.0) + jnp.where(col == A_HEADS + h, db, 0.0)
                dC_sc[h] = s['decay'] * dC + s['dCq']
                dn_sc[h] = s['decay'] * dn + jnp.sum(qf * (s['wi'] * s['dden']), axis=0, keepdims=True)
            dlf = jnp.dot(upper, dG, precision=HI, preferred_element_type=F32)
            pre = zg_ref[rows, :] + b_ref[...]
            th = jnp.tanh(pre / SOFTCAP)
            dcap = 1.0 - th * th
            dpre = jnp.where(col < A_HEADS, dG * dcap,
                             jnp.where(col < 2 * A_HEADS, dlf * _sigmoid(-SOFTCAP * th) * dcap, 0.0))
            dz_ref[rows, GATE_COL:GATE_COL + 128] = dpre.astype(dz_ref.dtype)
            dbif_ref[...] += jnp.sum(dpre, axis=0, keepdims=True)

    rev = lambda i: nsteps - 1 - i
    tok = lambda w, cb: pl.BlockSpec((A_TB, w), lambda i: (rev(i), cb))
    st = lambda a, b: pl.BlockSpec((A_GC, A_HEADS, a, b), lambda i: (rev(i), 0, 0, 0))
    return _call(
        body, "mlstm_bwd", (nsteps,),
        [tok(512, 0), tok(512, 1), tok(1024, 1), tok(1024, 2), tok(128, GATE_COL // 128),
         pl.BlockSpec((A_TB, 128), lambda i: (rev(i), 0)),
         pl.BlockSpec((A_TB // 128, 8, 128), lambda i: (rev(i), 0, 0)),
         pl.BlockSpec((1, 1024), lambda i: (0, 0)),
         pl.BlockSpec((1, 128), lambda i: (0, 0)),
         st(A_QK, A_V), st(1, 128), st(1, 128),
         pl.BlockSpec((A_TB, 1024), lambda i: (rev(i), 0))],
        [pl.BlockSpec((A_TB, A_IN_PAD), lambda i: (rev(i), 0)),
         pl.BlockSpec((1, 1024), lambda i: (0, 0)),
         pl.BlockSpec((1, 128), lambda i: (0, 0))],
        [jax.ShapeDtypeStruct((T, A_IN_PAD), MM_DTYPE), jax.ShapeDtypeStruct((1, 1024), F32),
         jax.ShapeDtypeStruct((1, 128), F32)],
        [pltpu.VMEM((A_HEADS, A_QK, A_V), F32), pltpu.VMEM((A_HEADS, 1, 128), F32)],
        ("arbitrary",), (z, z, z, z, z, gcol, grow, hng, bias128, Cs, ns, ms, dhg), ride)


def _t5_bucket(dist):
    max_exact = REL_BUCKETS // 2
    d = np.maximum(dist, 0)
    log_ratio = np.log(np.maximum(d, 1) / max_exact) / math.log(REL_MAX_DIST / max_exact)
    large = np.minimum(max_exact + (log_ratio * (REL_BUCKETS - max_exact)).astype(np.int64), REL_BUCKETS - 1)
    return np.where(d < max_exact, d, large).astype(np.int32)


def _group_bucket(g):
    delta = B_BLOCK + np.arange(B_BLOCK)[:, None] - np.arange(2 * B_BLOCK)[None, :]
    return _t5_bucket(delta * DILATIONS[g])


def _band_mask(n):
    ri = lax.broadcasted_iota(jnp.int32, (B_BLOCK, 2 * B_BLOCK), 0)
    ci = lax.broadcasted_iota(jnp.int32, (B_BLOCK, 2 * B_BLOCK), 1)
    band = jnp.logical_and(ci >= ri, ci <= ri + B_BLOCK)
    return jnp.logical_and(band, jnp.logical_or(ci >= B_BLOCK, n > 0))


def _both(p_ref, c_ref, sl):
    return jnp.concatenate([p_ref[:, sl], c_ref[:, sl]], axis=0)


def _scores(qh, kh, bias_h, valid):
    return jnp.where(valid, _dot_nt(qh, kh) * (B_DH ** -0.5) + bias_h, -jnp.inf)


def _attn_specs():
    wide = pl.BlockSpec((B_BLOCK, 1024), lambda r, n: (n, r))
    prev = pl.BlockSpec((B_BLOCK, 1024), lambda r, n: (jnp.maximum(n - 1, 0), r))
    narrow = pl.BlockSpec((B_BLOCK, 128), lambda r, n: (n, r))
    bias = pl.BlockSpec((B_HEADS, B_BLOCK, 2 * B_BLOCK), lambda r, n: (0, 0, 0))
    return wide, prev, narrow, bias


def _to_view(read_chunk, sc, o_ref, dil, nc, tt):
    for c in range(nc):
        sc[c] = read_chunk(c)
    for r in range(dil):
        for c in range(nc):
            lo = (r * nc + c) * 128
            o_ref[:, lo:lo + 128] = sc[c, pl.ds(r, tt // dil, stride=dil), :].astype(o_ref.dtype)


def _from_view(read_view, sc, dil, nc, tt):
    for r in range(dil):
        for c in range(nc):
            sc[c, pl.ds(r, tt // dil, stride=dil), :] = read_view((r * nc + c) * 128).astype(F32)


def attn_fwd(qv, kvw, vvw, bias, g):
    dil = DILATIONS[g]
    Tv = qv.shape[0]
    nb = Tv // B_BLOCK
    wide, prev, narrow, bsp = _attn_specs()

    def body(q_ref, kp_ref, kc_ref, vp_ref, vc_ref, b_ref, o_ref, lse_ref):
        valid = _band_mask(pl.program_id(1))
        lse_ref[...] = jnp.zeros_like(lse_ref)
        heads = [slice(h * B_DH, (h + 1) * B_DH) for h in range(B_HEADS)]
        S = [_scores(q_ref[:, sl], _both(kp_ref, kc_ref, sl), b_ref[h], valid) for h, sl in enumerate(heads)]
        P, L = [], []
        for h in range(B_HEADS):
            m = jnp.max(S[h], axis=1, keepdims=True)
            p = jnp.exp(S[h] - m)
            l = jnp.sum(p, axis=1, keepdims=True)
            lse_ref[:, h:h + 1] = m + jnp.log(l)
            P.append(p.astype(MM_DTYPE))
            L.append(l)
        for h, sl in enumerate(heads):
            o_ref[:, sl] = _dot(P[h], _both(vp_ref, vc_ref, sl)) / L[h]

    return pl.pallas_call(
        body, name=f"attn_fwd_g{g}", grid=(dil, nb),
        in_specs=[wide, prev, wide, prev, wide, bsp], out_specs=[wide, narrow],
        out_shape=[jax.ShapeDtypeStruct((Tv, dil * 1024), F32), jax.ShapeDtypeStruct((Tv, dil * 128), F32)],
        compiler_params=_params("parallel", "parallel"),
    )(qv, kvw, kvw, vvw, vvw, bias)


def attn_bwd(qv, kvw, vvw, bias, do_v, lse_v, dl_v, g):
    dil = DILATIONS[g]
    Tv = qv.shape[0]
    nb = Tv // B_BLOCK
    wide, prev, narrow, bsp = _attn_specs()

    def body(q_ref, kp_ref, kc_ref, vp_ref, vc_ref, b_ref, bt_ref, do_ref, lse_ref, dl_ref,
             dq_ref, dkc_ref, dkp_ref, dvc_ref, dvp_ref, db_ref):
        @pl.when(jnp.logical_and(pl.program_id(0) == 0, pl.program_id(1) == 0))
        def _():
            db_ref[...] = jnp.zeros_like(db_ref)

        n = pl.program_id(1)
        valid = _band_mask(n)
        ki = lax.broadcasted_iota(jnp.int32, (2 * B_BLOCK, B_BLOCK), 0)
        qi = lax.broadcasted_iota(jnp.int32, (2 * B_BLOCK, B_BLOCK), 1)
        valid_t = jnp.logical_and(jnp.logical_and(ki >= qi, ki <= qi + B_BLOCK), jnp.logical_or(ki >= B_BLOCK, n > 0))
        lse_t, dl_t = lse_ref[...].T, dl_ref[...].T
        heads = [slice(h * B_DH, (h + 1) * B_DH) for h in range(B_HEADS)]
        scale = B_DH ** -0.5
        PT, DS, DST = [], [], []
        for h, sl in enumerate(heads):
            qh, doh = q_ref[:, sl], do_ref[:, sl].astype(MM_DTYPE)
            kh, vh = _both(kp_ref, kc_ref, sl), _both(vp_ref, vc_ref, sl)
            p = jnp.exp(_scores(qh, kh, b_ref[h], valid) - lse_ref[:, h:h + 1])
            ds = p * (_dot_nt(doh, vh) - dl_ref[:, h:h + 1])
            db_ref[h] += ds
            DS.append((ds * scale).astype(MM_DTYPE))
            pt = jnp.exp(_scores(kh, qh, bt_ref[h], valid_t) - lse_t[h:h + 1, :])
            PT.append(pt.astype(MM_DTYPE))
            DST.append((pt * (_dot_nt(vh, doh) - dl_t[h:h + 1, :]) * scale).astype(MM_DTYPE))
        for h, sl in enumerate(heads):
            qh, doh = q_ref[:, sl], do_ref[:, sl].astype(MM_DTYPE)
            dq_ref[:, sl] = _dot(DS[h], _both(kp_ref, kc_ref, sl)).astype(MM_DTYPE)
            dk = _dot(DST[h], qh).astype(MM_DTYPE)
            dv = _dot(PT[h], doh).astype(MM_DTYPE)
            dkp_ref[:, sl], dkc_ref[:, sl] = dk[:B_BLOCK], dk[B_BLOCK:]
            dvp_ref[:, sl], dvc_ref[:, sl] = dv[:B_BLOCK], dv[B_BLOCK:]

    big = jax.ShapeDtypeStruct((Tv, dil * 1024), MM_DTYPE)
    bsp_t = pl.BlockSpec((B_HEADS, 2 * B_BLOCK, B_BLOCK), lambda r, n: (0, 0, 0))
    return pl.pallas_call(
        body, name=f"attn_bwd_g{g}", grid=(dil, nb),
        in_specs=[wide, prev, wide, prev, wide, bsp, bsp_t, wide, narrow, narrow],
        out_specs=[wide] * 5 + [bsp],
        out_shape=[big] * 5 + [jax.ShapeDtypeStruct((B_HEADS, B_BLOCK, 2 * B_BLOCK), F32)],
        compiler_params=_params("arbitrary", "arbitrary"),
    )(qv, kvw, kvw, vvw, vvw, bias, jnp.swapaxes(bias, 1, 2), do_v, lse_v, dl_v)


def _head_expand():
    e = np.zeros((128, 1024), np.float32)
    for h in range(B_HEADS):
        e[h, h * B_DH:(h + 1) * B_DH] = 1.0
    return e


A_TT = 256


def _view_spec(dil, width):
    return pl.BlockSpec((A_TT // dil, dil * width), lambda i: (i, 0))


def attn_merge(os_v, lses_v):
    T = os_v[0].shape[0]
    tt = A_TT
    expand = jnp.asarray(_head_expand())

    def body(o0, o1, o2, l0, l1, l2, e_ref, ob_ref, of_ref, lse0_ref, lse1_ref, lse2_ref, sc_o, sc_l):
        for gi, (o_ref, l_ref) in enumerate(((o1, l1), (o2, l2))):
            dil = DILATIONS[gi + 1]
            _from_view(lambda lo: o_ref[:, lo:lo + 128], sc_o.at[gi], dil, 8, tt)
            _from_view(lambda lo: l_ref[:, lo:lo + 128], sc_l.at[gi], dil, 1, tt)
        ls = [l0[...], sc_l[0, 0], sc_l[1, 0]]
        m = jnp.maximum(jnp.maximum(ls[0], ls[1]), ls[2])
        ex = [jnp.exp(l - m) for l in ls]
        tot = ex[0] + ex[1] + ex[2]
        lse = m + jnp.log(tot)
        lse0_ref[...] = lse
        _to_view(lambda c: lse, sc_l.at[2], lse1_ref, DILATIONS[1], 1, tt)
        _to_view(lambda c: lse, sc_l.at[2], lse2_ref, DILATIONS[2], 1, tt)
        ws = [e / tot for e in ex]
        for c in range(8):
            cols = slice(c * 128, (c + 1) * 128)
            ecol = e_ref[:, cols]
            spread = [jnp.dot(w, ecol, precision=HI, preferred_element_type=F32) for w in ws]
            out = spread[0] * o0[:, cols] + spread[1] * sc_o[0, c] + spread[2] * sc_o[1, c]
            of_ref[:, cols] = out
            ob_ref[:, cols] = out.astype(ob_ref.dtype)

    wide = pl.BlockSpec((tt, 1024), lambda i: (i, 0))
    return pl.pallas_call(
        body, name="attn_merge", grid=(T // tt,),
        in_specs=[_view_spec(d, 1024) for d in DILATIONS] + [_view_spec(d, 128) for d in DILATIONS]
        + [pl.BlockSpec((128, 1024), lambda i: (0, 0))],
        out_specs=[wide, wide] + [_view_spec(d, 128) for d in DILATIONS],
        out_shape=[jax.ShapeDtypeStruct((T, 1024), MM_DTYPE), jax.ShapeDtypeStruct((T, 1024), F32)]
        + [jax.ShapeDtypeStruct((T // d, d * 128), F32) for d in DILATIONS],
        scratch_shapes=[pltpu.VMEM((2, 8, tt, 128), F32), pltpu.VMEM((3, 1, tt, 128), F32)],
        compiler_params=_params("parallel"),
    )(*os_v, *lses_v, expand)


def attn_prep(datt, out):
    T = datt.shape[0]
    tt = A_TT
    expand_t = jnp.asarray(_head_expand().T.copy())

    def body(d_ref, o_ref, e_ref, do0, do1, do2, dl0, dl1, dl2, sc_d, sc_l):
        delta = jnp.dot(d_ref[...] * o_ref[...], e_ref[...], precision=HI, preferred_element_type=F32)
        do0[...] = d_ref[...].astype(do0.dtype)
        dl0[...] = delta
        for do_ref, dl_ref, dil in ((do1, dl1, DILATIONS[1]), (do2, dl2, DILATIONS[2])):
            _to_view(lambda c: d_ref[:, c * 128:(c + 1) * 128], sc_d, do_ref, dil, 8, tt)
            _to_view(lambda c: delta, sc_l, dl_ref, dil, 1, tt)

    wide = pl.BlockSpec((tt, 1024), lambda i: (i, 0))
    return pl.pallas_call(
        body, name="attn_prep", grid=(T // tt,),
        in_specs=[wide, wide, pl.BlockSpec((1024, 128), lambda i: (0, 0))],
        out_specs=[_view_spec(d, 1024) for d in DILATIONS] + [_view_spec(d, 128) for d in DILATIONS],
        out_shape=[jax.ShapeDtypeStruct((T // d, d * 1024), MM_DTYPE) for d in DILATIONS]
        + [jax.ShapeDtypeStruct((T // d, d * 128), F32) for d in DILATIONS],
        scratch_shapes=[pltpu.VMEM((8, tt, 128), F32), pltpu.VMEM((1, tt, 128), F32)],
        compiler_params=_params("parallel"),
    )(datt, out, expand_t)


def attn_combine(parts):
    T = parts[0][0].shape[0]
    tt = A_TT
    nt = T // tt
    shift = [None] + [B_BLOCK * d // tt for d in DILATIONS[1:]]

    def body(dq0, kc0, vc0, kpa0, kpb0, vpa0, vpb0, dq1, kc1, kp1, vc1, vp1, dq2, kc2, kp2, vc2, vp2,
             dq_ref, dkv_ref, sc):
        i = pl.program_id(0)
        dq_ref[:, 0:1024] = dq0[...].astype(dq_ref.dtype)
        for col, c_ref, pa_ref, pb_ref in ((0, kc0, kpa0, kpb0), (3, vc0, vpa0, vpb0)):
            nxt = jnp.where(i + 1 < nt, pb_ref[:tt // 2, :].astype(F32), 0.0)
            later = jnp.concatenate([pa_ref[tt // 2:, :].astype(F32), nxt], axis=0)
            dkv_ref[:, col * 1024:(col + 1) * 1024] = (c_ref[...].astype(F32) + later).astype(dkv_ref.dtype)
        for g, (dq, kc, kp, vc, vp) in ((1, (dq1, kc1, kp1, vc1, vp1)), (2, (dq2, kc2, kp2, vc2, vp2))):
            dil = DILATIONS[g]
            live = i + shift[g] < nt
            _from_view(lambda lo: dq[:, lo:lo + 128], sc, dil, 8, tt)
            for c in range(8):
                dq_ref[:, g * 1024 + c * 128:g * 1024 + (c + 1) * 128] = sc[c].astype(dq_ref.dtype)
            for col, c_ref, p_ref in ((g, kc, kp), (3 + g, vc, vp)):
                _from_view(lambda lo: c_ref[:, lo:lo + 128].astype(F32)
                           + jnp.where(live, p_ref[:, lo:lo + 128].astype(F32), 0.0), sc, dil, 8, tt)
                for c in range(8):
                    dkv_ref[:, col * 1024 + c * 128:col * 1024 + (c + 1) * 128] = sc[c].astype(dkv_ref.dtype)

    def later_spec(dil, blocks):
        return pl.BlockSpec((tt // dil, dil * 1024), lambda i: (jnp.minimum(i + blocks, nt - 1), 0))

    cur = [_view_spec(d, 1024) for d in DILATIONS]
    in_specs = [cur[0], cur[0], cur[0], cur[0], later_spec(1, 1), cur[0], later_spec(1, 1)]
    args = [parts[0][0], parts[0][1], parts[0][3], parts[0][2], parts[0][2], parts[0][4], parts[0][4]]
    for g in (1, 2):
        in_specs += [cur[g], cur[g], later_spec(DILATIONS[g], shift[g]), cur[g], later_spec(DILATIONS[g], shift[g])]
        args += list(parts[g][:5])
    return pl.pallas_call(
        body, name="attn_combine", grid=(nt,), in_specs=in_specs,
        out_specs=[pl.BlockSpec((tt, 3072), lambda i: (i, 0)), pl.BlockSpec((tt, 6144), lambda i: (i, 0))],
        out_shape=[jax.ShapeDtypeStruct((T, 3072), MM_DTYPE), jax.ShapeDtypeStruct((T, 6144), MM_DTYPE)],
        scratch_shapes=[pltpu.VMEM((8, tt, 128), F32)],
        compiler_params=_params("parallel"),
    )(*args)


def adamw(w, g, m, v, name):
    R, C = w.shape
    tr = R if R * C * 4 <= (1 << 20) else _rows(R, max(8, ((1 << 20) // (C * 4)) // 8 * 8))

    def body(w_ref, g_ref, m_ref, v_ref, d_ref, nm_ref, nv_ref):
        gg = g_ref[...]
        nm = ADAM_B1 * m_ref[...] + (1.0 - ADAM_B1) * gg
        nv = ADAM_B2 * v_ref[...] + (1.0 - ADAM_B2) * (gg * gg)
        m_hat = nm / (1.0 - ADAM_B1 ** ADAM_STEP)
        v_hat = nv / (1.0 - ADAM_B2 ** ADAM_STEP)
        d_ref[...] = -ADAM_LR * (m_hat / (jnp.sqrt(v_hat) + ADAM_EPS) + ADAM_WD * w_ref[...])
        nm_ref[...] = nm
        nv_ref[...] = nv

    blk = pl.BlockSpec((tr, C), lambda i: (i, 0))
    sds = jax.ShapeDtypeStruct((R, C), F32)
    return pl.pallas_call(
        body, name=name, grid=(R // tr,), in_specs=[blk] * 4, out_specs=[blk] * 3, out_shape=[sds] * 3,
        compiler_params=_params("parallel"),
    )(w, g, m, v)


def sum_slots(x, name, out_dtype=F32):
    n, R, C = x.shape
    tr = _rows(R, 256)

    def body(x_ref, o_ref):
        acc = x_ref[0].astype(F32)
        for s in range(1, n):
            acc = acc + x_ref[s].astype(F32)
        o_ref[...] = acc.astype(out_dtype)

    return pl.pallas_call(
        body, name=name, grid=(R // tr,),
        in_specs=[pl.BlockSpec((n, tr, C), lambda i: (0, i, 0))],
        out_specs=pl.BlockSpec((tr, C), lambda i: (i, 0)),
        out_shape=jax.ShapeDtypeStruct((R, C), out_dtype),
        compiler_params=_params("parallel"),
    )(x)


_ANY = pl.BlockSpec(memory_space=pl.ANY)
GROUP_ALL = ([(0, 0, 1), (0, 1, 0), (0, 1, 1), (1, 0, 0), (1, 0, 1), (1, 1, 0), (1, 1, 1)],
             lambda d: 4 * d[0] + 2 * d[1] + d[2])
GROUP_CHIPS = ([(0, 1, 0), (1, 0, 0), (1, 1, 0)], lambda d: 2 * d[0] + d[1])
GROUP_SIBLING = ([(0, 0, 1)], lambda d: d[2])


def _me():
    return lax.axis_index("x"), lax.axis_index("y"), lax.axis_index("c")


def _peer(me, flip):
    return tuple(1 - a if f else a for a, f in zip(me, flip))


class Exchange:
    def __init__(self, x, group, scatter):
        self.flips, self.slot = group
        self.scatter = scatter
        self.n = len(self.flips) + 1
        self.out_shape = jax.ShapeDtypeStruct((self.n,) + x.shape[-2:], x.dtype)
        self.scratch = [pltpu.SemaphoreType.DMA((self.n - 1,)), pltpu.SemaphoreType.DMA((self.n - 1,)),
                        pltpu.SemaphoreType.DMA]

    def _copies(self, x_ref, o_ref, send_sems, recv_sems, local_sem, arrivals):
        me = _me()
        slot = self.slot
        mine = pltpu.make_async_copy(x_ref.at[slot(me)] if self.scatter else x_ref, o_ref.at[slot(me)], local_sem)
        sends, landed = [], []
        for k, flip in enumerate(self.flips):
            peer = _peer(me, flip)
            sends.append(pltpu.make_async_remote_copy(
                src_ref=x_ref.at[slot(peer)] if self.scatter else x_ref, dst_ref=o_ref.at[slot(me)],
                send_sem=send_sems.at[k], recv_sem=recv_sems.at[k], device_id=peer, device_id_type=MESH_ID))
            if arrivals:
                landed.append(pltpu.make_async_remote_copy(
                    src_ref=o_ref.at[slot(me)], dst_ref=o_ref.at[slot(peer)], send_sem=send_sems.at[k],
                    recv_sem=recv_sems.at[k], device_id=peer, device_id_type=MESH_ID))
        return mine, sends, landed

    def start(self, *refs):
        mine, sends, _ = self._copies(*refs, arrivals=False)
        mine.start()
        for cp in sends:
            cp.start()

    def wait(self, *refs):
        mine, sends, arrivals = self._copies(*refs, arrivals=True)
        for cp in arrivals:
            cp.wait_recv()
        for cp in sends:
            cp.wait_send()
        mine.wait()

    def __call__(self, x, name):
        def body(*refs):
            self.start(*refs)
            self.wait(*refs)

        return pl.pallas_call(body, name=name, in_specs=[_ANY], out_specs=_ANY, out_shape=self.out_shape,
                              scratch_shapes=self.scratch)(x)


def group_gather(x, name, group):
    return Exchange(x, group, scatter=False)(x, name)


def group_scatter(x, name, group):
    return Exchange(x, group, scatter=True)(x, name)


def _call(body, name, grid, in_specs, out_specs, out_shape, scratch, semantics, args, ride=None):
    if ride is None:
        return pl.pallas_call(body, name=name, grid=grid, in_specs=in_specs, out_specs=out_specs,
                              out_shape=out_shape, scratch_shapes=scratch,
                              compiler_params=_params(*semantics))(*args)
    x, exch = ride
    n_in, n_out, n_scr = len(in_specs), len(out_specs), len(scratch)

    def at_step(pick):
        hit = None
        for axis, size in enumerate(grid):
            here = pl.program_id(axis) == pick(size)
            hit = here if hit is None else jnp.logical_and(hit, here)
        return hit

    def riding(*refs):
        ins, x_ref = refs[:n_in], refs[n_in]
        outs, o_ref = refs[n_in + 1:n_in + 1 + n_out], refs[n_in + 1 + n_out]
        scr, sems = refs[n_in + 2 + n_out:n_in + 2 + n_out + n_scr], refs[n_in + 2 + n_out + n_scr:]

        @pl.when(at_step(lambda size: 0))
        def _():
            exch.start(x_ref, o_ref, *sems)

        body(*ins, *outs, *scr)

        @pl.when(at_step(lambda size: size - 1))
        def _():
            exch.wait(x_ref, o_ref, *sems)

    return pl.pallas_call(
        riding, name=name, grid=grid, in_specs=list(in_specs) + [_ANY], out_specs=list(out_specs) + [_ANY],
        out_shape=list(out_shape) + [exch.out_shape], scratch_shapes=list(scratch) + exch.scratch,
        compiler_params=_params(*(["arbitrary"] * len(grid))))(*args, x)


WEIGHTS = ['a_norm_g', 'a_w_in', 'a_b_if', 'a_hnorm_g', 'a_w_out', 'kv_norm_g', 'w_kv', 'b_norm_g', 'b_w_q',
           'b_w_out', 'rel_bias', 'f_norm_g', 'f_w_up', 'f_conv_w', 'f_conv_b', 'f_w_down', 'final_norm_g']
SHARD_AXIS = {'a_norm_g': 1, 'a_w_in': 2, 'a_b_if': None, 'a_hnorm_g': 2, 'a_w_out': 1, 'kv_norm_g': None,
              'w_kv': 1, 'b_norm_g': None, 'b_w_q': 2, 'b_w_out': 1, 'rel_bias': None, 'f_norm_g': None,
              'f_w_up': 2, 'f_conv_w': 2, 'f_conv_b': None, 'f_w_down': 1, 'final_norm_g': None}
BIG = ['a_w_in', 'a_w_out', 'w_kv', 'b_w_q', 'b_w_out', 'f_w_up', 'f_w_down']
SMALL = [n for n in WEIGHTS if n not in BIG]
LANES = 1024
PIECES = {'a_w_in': ('a_w_in', None, 2), 'a_w_out': ('a_w_out', None, 1), 'f_w_up0': ('f_w_up', 0, 1),
          'f_w_down0': ('f_w_down', 0, 0), 'w_kv': ('w_kv', None, 1), 'b_w_q': ('b_w_q', None, 2),
          'b_w_out': ('b_w_out', None, 1), 'f_w_up1': ('f_w_up', 1, 1), 'f_w_down1': ('f_w_down', 1, 0)}
LATE = ['w_kv', 'b_w_q', 'b_w_out', 'f_w_up1', 'f_w_down1']
WEIGHT_WAVES = {'first': ['a_w_in', 'a_w_out'], 'ffn0': ['f_w_up0', 'f_w_down0'], 'late': LATE}
GRAD_WAVES = {'late': LATE, 'layer0': ['f_w_up0', 'f_w_down0', 'a_w_out'], 'last': ['a_w_in']}


def _piece(arrays, p):
    leaf, layer, _ = PIECES[p]
    return arrays[leaf] if layer is None else arrays[leaf][layer]


class Packer:
    def __init__(self, pieces, shard):
        self.pieces = pieces
        self.shapes = [_piece(shard, p).shape for p in pieces]
        self.sizes = [math.prod(s) // (2 * LANES) for s in self.shapes]
        self.fill = -sum(self.sizes) % 16
        self.rows = sum(self.sizes) + self.fill

    def my_half(self, shard, half):
        both = jnp.concatenate([_piece(shard, p).astype(MM_DTYPE).reshape(2, -1, LANES) for p in self.pieces], axis=1)
        return jnp.pad(lax.dynamic_index_in_dim(both, half, axis=0, keepdims=False), ((0, self.fill), (0, 0)))

    def full_weights(self, gathered):
        g = gathered.reshape(4, 2, self.rows, LANES)
        out, off = {}, 0
        for p, shp, sz in zip(self.pieces, self.shapes, self.sizes):
            out[p] = _full_from_shards(g[:, :, off:off + sz].reshape((4,) + shp), PIECES[p][2])
            off += sz
        return out

    def grad_slots(self, grads):
        parts = [_shards_from_full(grads[p], PIECES[p][2]).reshape(4, 2, -1, LANES).astype(GRAD_WIRE_DTYPE)
                 for p in self.pieces]
        parts.append(jnp.zeros((4, 2, self.fill, LANES), GRAD_WIRE_DTYPE))
        return jnp.concatenate(parts, axis=2).reshape(8, self.rows, LANES)

    def shard_grads(self, both):
        out, off = {}, 0
        for p, shp, sz in zip(self.pieces, self.shapes, self.sizes):
            out[p] = both[:, off:off + sz].reshape(shp).astype(F32)
            off += sz
        return out


class Overlap:
    def __init__(self, shard, half):
        self.shard, self.half = shard, half
        self.weights = {w: Packer(p, shard) for w, p in WEIGHT_WAVES.items()}
        self.grads = {w: Packer(p, shard) for w, p in GRAD_WAVES.items()}
        self.shard_grads = {}

    def gather_ride(self, wave):
        mine = self.weights[wave].my_half(self.shard, self.half)
        return mine, Exchange(mine, GROUP_ALL, scatter=False)

    def gathered(self, wave, slots):
        return self.weights[wave].full_weights(slots)

    def scatter_ride(self, wave, grads):
        slots = self.grads[wave].grad_slots({p: grads.pop(p) for p in GRAD_WAVES[wave]})
        return slots, Exchange(slots, GROUP_ALL, scatter=True)

    def join_ride(self, wave, received):
        reduced = sum_slots(received, f"sum_grads_{wave}", GRAD_WIRE_DTYPE)
        return reduced, Exchange(reduced, GROUP_SIBLING, scatter=False)

    def joined(self, wave, both):
        self.shard_grads.update(self.grads[wave].shard_grads(both))


def _pad_rows(flat, mult):
    n = flat.shape[0]
    per = LANES * mult
    tot = -(-n // per) * per
    return jnp.pad(flat, (0, tot - n)).reshape(tot // LANES, LANES)


def _full_from_shards(sh, axis):
    shp = sh.shape[1:]
    return jnp.moveaxis(sh, 0, axis).reshape(shp[:axis] + (4 * shp[axis],) + shp[axis + 1:])


def _shards_from_full(full, axis):
    shp = full.shape
    return jnp.moveaxis(full.reshape(shp[:axis] + (4, shp[axis] // 4) + shp[axis + 1:]), axis, 0)


def _local_step(x, target, W, overlap=None):
    T = x.shape[0]
    W = dict(W)
    row = lambda a: a.reshape(1, -1).astype(F32)
    w_in = jnp.pad(W['a_w_in'][0], ((0, 0), (0, A_IN_PAD - A_IN)))
    bias128 = jnp.pad(row(W['a_b_if'][0]), ((0, 0), (0, 120)))
    hng = row(W['a_hnorm_g'][0])
    w_up = lambda l: _interleave(W[f'f_w_up{l}'])
    cw = [_interleave(W['f_conv_w'][l].astype(F32)) for l in range(2)]
    cb = [_interleave(row(W['f_conv_b'][l])) for l in range(2)]
    onehots = [(jnp.asarray(_group_bucket(g).reshape(-1, 1)) == jnp.arange(128)[None, :]).astype(F32)
               for g in range(N_GROUPS)]
    rb_t = jnp.pad(W['rel_bias'].astype(F32).T, ((0, 0), (0, 128 - REL_BUCKETS)))
    biases = [mm_nn(rb_t[g * B_HEADS:(g + 1) * B_HEADS], onehots[g].T, f"rel_bias_table_g{g}", exact=True)
              .reshape(B_HEADS, B_BLOCK, 2 * B_BLOCK) for g in range(N_GROUPS)]
    G = {}

    def ffn_fwd(xin, l, ride=None):
        xn, = rms_fwd(xin, [row(W['f_norm_g'][l])], f"ffn{l}_norm")
        u, act, *rode = ffn_up_act(xn, w_up(l), cw[l], cb[l], f"ffn{l}_up_act", ride)
        return mm_nn(act, W[f'f_w_down{l}'], f"ffn{l}_down", res=xin), (xn, u, act), rode

    def ffn_bwd(xin, saved, dout, l, ride=None):
        xn, u, act = saved
        dact = mm_nn(dout, W[f'f_w_down{l}'].T, f"ffn{l}_ddown")
        G[f'f_w_down{l}'] = mm_tn(act, dout, f"ffn{l}_gdown")
        du, gcw, gcb, *rode = conv_act_bwd(u, dact, cw[l], cb[l], f"ffn{l}_dact", ride)
        dxn = mm_nn(du, w_up(l).T, f"ffn{l}_dup")
        G[f'f_w_up{l}'] = _deinterleave(mm_tn(xn, du, f"ffn{l}_gup"))
        dxin, (gn,) = rms_bwd(xin, dout, [(dxn, row(W['f_norm_g'][l]))], f"ffn{l}_dnorm")
        return dxin, _deinterleave(gcw), _deinterleave(gcb), gn, rode

    xn_a, = rms_fwd(x, [row(W['a_norm_g'][0])], "a_norm")
    z = mm_nn(xn_a, w_in, "a_in")
    gcol, grow = gate_prep(z, bias128)
    hg, Cs, ns, ms, *rode = mlstm_fwd(z, gcol, grow, hng, overlap.gather_ride('ffn0') if overlap else None)
    if overlap:
        W.update(overlap.gathered('ffn0', rode[0]))
    x1 = mm_nn(hg, W['a_w_out'][0], "a_out", res=x)
    x2, ffn0, rode = ffn_fwd(x1, 0, overlap.gather_ride('late') if overlap else None)
    if overlap:
        W.update(overlap.gathered('late', rode[0]))
    xn_kv, xn_b = rms_fwd(x2, [row(W['kv_norm_g']), row(W['b_norm_g'][0])], "b_norms")
    gcols = lambda w, c: w[:, c * 1024:(c + 1) * 1024]
    qv = [mm_view(xn_b, gcols(W['b_w_q'][0], g), f"q_proj_g{g}", DILATIONS[g]) for g in range(N_GROUPS)]
    kvw = [mm_view(xn_kv, gcols(W['w_kv'], g), f"k_proj_g{g}", DILATIONS[g]) for g in range(N_GROUPS)]
    vvw = [mm_view(xn_kv, gcols(W['w_kv'], 3 + g), f"v_proj_g{g}", DILATIONS[g]) for g in range(N_GROUPS)]
    os_, lses = zip(*[attn_fwd(qv[g], kvw[g], vvw[g], biases[g], g) for g in range(N_GROUPS)])
    att, att_f, *lse_v = attn_merge(os_, lses)
    x3 = mm_nn(att, W['b_w_out'][0], "b_out", res=x2)
    x4, ffn1, _ = ffn_fwd(x3, 1)
    dx4, g_final, loss = loss_head(x4, target, row(W['final_norm_g']))
    G['final_norm_g'] = g_final.reshape(-1)

    dx3, gcw1, gcb1, gn1, _ = ffn_bwd(x3, ffn1, dx4, 1)
    datt = mm_nn(dx3, W['b_w_out'][0].T, "b_dout")
    G['b_w_out'] = mm_tn(att, dx3, "b_gout")[None]
    prep = attn_prep(datt, att_f)
    do_v, dl_v = prep[:3], prep[3:]
    parts = [attn_bwd(qv[g], kvw[g], vvw[g], biases[g], do_v[g], lse_v[g], dl_v[g], g) for g in range(N_GROUPS)]
    dq_all, dkv = attn_combine(parts)
    grb = []
    for g in range(N_GROUPS):
        gb = mm_nn(parts[g][5].reshape(B_HEADS, -1), onehots[g], f"rel_bias_g{g}", exact=True)
        grb.append(gb[:, :REL_BUCKETS].T)
    G['rel_bias'] = jnp.concatenate(grb, axis=1)
    dxn_b = mm_nn(dq_all, W['b_w_q'][0].T, "q_dproj")
    G['b_w_q'] = mm_tn(xn_b, dq_all, "q_gproj")[None]
    dxn_kv = mm_nn(dkv, W['w_kv'].T, "kv_dproj")
    G['w_kv'] = mm_tn(xn_kv, dkv, "kv_gproj")
    dx2, (g_kvn, g_bn) = rms_bwd(x2, dx3, [(dxn_kv, row(W['kv_norm_g'])), (dxn_b, row(W['b_norm_g'][0]))],
                                 "b_dnorms")
    G['kv_norm_g'] = g_kvn.reshape(-1)
    G['b_norm_g'] = g_bn
    dx1, gcw0, gcb0, gn0, late_slots = ffn_bwd(x1, ffn0, dx2, 0, overlap.scatter_ride('late', G) if overlap else None)
    G['f_conv_w'] = jnp.stack([gcw0, gcw1])
    G['f_conv_b'] = jnp.concatenate([gcb0, gcb1], axis=0)
    G['f_norm_g'] = jnp.concatenate([gn0, gn1], axis=0)
    dhg = mm_nn(dx1, W['a_w_out'][0].T, "a_dout")
    G['a_w_out'] = mm_tn(hg, dx1, "a_gout")[None]
    dz, g_hn, g_bif, *layer0_slots = mlstm_bwd(z, gcol, grow, hng, bias128, Cs, ns, ms, dhg,
                                               overlap.scatter_ride('layer0', G) if overlap else None)
    G['a_hnorm_g'] = g_hn.reshape(1, A_HEADS, A_V)
    G['a_b_if'] = g_bif[:, :2 * A_HEADS]
    if overlap:
        dxn_a, both = mm_nn(dz, w_in.T, "a_din", ride=overlap.join_ride('late', late_slots[0]))
        overlap.joined('late', both)
        g_in, both = mm_tn(xn_a, dz, "a_gin", ride=overlap.join_ride('layer0', layer0_slots[0]))
        overlap.joined('layer0', both)
    else:
        dxn_a = mm_nn(dz, w_in.T, "a_din")
        g_in = mm_tn(xn_a, dz, "a_gin")
    G['a_w_in'] = g_in[:, :A_IN][None]
    grad_x, (g_an,) = rms_bwd(x, dx1, [(dxn_a, row(W['a_norm_g'][0]))], "a_dnorm")
    G['a_norm_g'] = g_an
    return loss, grad_x, G


def kernel(x, a_norm_g, a_w_in, a_b_if, a_hnorm_g, a_w_out, kv_norm_g, w_kv, b_norm_g, b_w_q, b_w_out, rel_bias, f_norm_g, f_w_up, f_conv_w, f_conv_b, f_w_down, final_norm_g, loss_target, m_a_norm_g, m_a_w_in, m_a_b_if, m_a_hnorm_g, m_a_w_out, m_kv_norm_g, m_w_kv, m_b_norm_g, m_b_w_q, m_b_w_out, m_rel_bias, m_f_norm_g, m_f_w_up, m_f_conv_w, m_f_conv_b, m_f_w_down, m_final_norm_g, v_a_norm_g, v_a_w_in, v_a_b_if, v_a_hnorm_g, v_a_w_out, v_kv_norm_g, v_w_kv, v_b_norm_g, v_b_w_q, v_b_w_out, v_rel_bias, v_f_norm_g, v_f_w_up, v_f_conv_w, v_f_conv_b, v_f_w_down, v_final_norm_g):
    given = dict(locals())
    shard = {n: given[n] for n in WEIGHTS}
    mom = {n: given["m_" + n] for n in WEIGHTS}
    var = {n: given["v_" + n] for n in WEIGHTS}
    cx, cy, cc = _me()
    chip = 2 * cx + cy

    overlap = Overlap(shard, cc)
    mine, gather = overlap.gather_ride('first')
    W = overlap.gathered('first', gather(mine, "gather_weights"))
    sharded_small = [n for n in SMALL if SHARD_AXIS[n] is not None]
    ssz = [shard[n].size for n in sharded_small]
    sflat = jnp.concatenate([shard[n].reshape(-1) for n in sharded_small])
    sg = group_gather(_pad_rows(sflat, 8), "gather_small", GROUP_CHIPS).reshape(4, -1)
    off = 0
    for n, sz in zip(sharded_small, ssz):
        W[n] = _full_from_shards(sg[:, off:off + sz].reshape((4,) + shard[n].shape), SHARD_AXIS[n])
        off += sz
    for n in SMALL:
        if SHARD_AXIS[n] is None:
            W[n] = shard[n]

    loss_row, grad_x, G = _local_step(x[0], loss_target[0], W, overlap)

    slots, scatter = overlap.scatter_ride('last', G)
    reduced, join = overlap.join_ride('last', scatter(slots, "scatter_grads"))
    overlap.joined('last', join(reduced, "join_halves"))
    by_piece = overlap.shard_grads
    gsh = {}
    for n in BIG:
        layers = [p for p in PIECES if PIECES[p][0] == n]
        gsh[n] = by_piece[n] if layers == [n] else jnp.stack([by_piece[p] for p in layers])
    small_parts = [loss_row[0, 0:1]] + [G[n].reshape(-1) for n in SMALL]
    small_sz = [p.shape[0] for p in small_parts]
    small = sum_slots(group_gather(_pad_rows(jnp.concatenate(small_parts), 8), "gather_small_grads", GROUP_ALL),
                      "sum_small_grads").reshape(-1)
    loss = small[0]
    off = 1
    for n, sz in zip(SMALL, small_sz[1:]):
        full = small[off:off + sz].reshape(W[n].shape)
        off += sz
        if SHARD_AXIS[n] is None:
            gsh[n] = full
        else:
            gsh[n] = lax.dynamic_index_in_dim(_shards_from_full(full, SHARD_AXIS[n]), chip, 0, keepdims=False)

    delta, new_m, new_v = {}, {}, {}
    for n in WEIGHTS:
        shp = shard[n].shape
        two = lambda a: a.reshape(-1, shp[-1])
        d, nm, nv = adamw(two(shard[n]), two(gsh[n]), two(mom[n]), two(var[n]), f"adamw_{n}")
        delta[n], new_m[n], new_v[n] = d.reshape(shp), nm.reshape(shp), nv.reshape(shp)
    return (loss, grad_x[None], *[gsh[n] for n in WEIGHTS], *[delta[n] for n in WEIGHTS],
            *[new_m[n] for n in WEIGHTS], *[new_v[n] for n in WEIGHTS])
```

```python
import functools
import math

import numpy as np
import jax
import jax.numpy as jnp
from jax import lax
from jax.experimental import pallas as pl
from jax.experimental.pallas import tpu as pltpu

F32 = jnp.float32
BF16 = jnp.bfloat16
MM_DTYPE = jnp.bfloat16
GRAD_WIRE_DTYPE = jnp.bfloat16
HI = lax.Precision.HIGHEST

D_MODEL = 1024
A_HEADS = 4
A_QK = 128
A_V = 256
A_CHUNK = 256
A_IN = 3080
A_IN_PAD = 3200
GATE_COL = 3072
SOFTCAP = 15.0
N_GROUPS = 3
B_HEADS = 16
B_DH = 64
B_BLOCK = 128
DILATIONS = (1, 4, 16)
WINDOWS = (128, 512, 2048)
REL_BUCKETS = 32
REL_MAX_DIST = 2048
D_FF = 2816
FF_TC = 256
EPS = 1e-6
ADAM_LR, ADAM_B1, ADAM_B2, ADAM_EPS, ADAM_WD, ADAM_STEP = 0.001, 0.9, 0.999, 1e-08, 0.01, 10

VMEM_LIMIT = 56 * 1024 * 1024
NT_DIMS = (((1,), (1,)), ((), ()))
TN_DIMS = (((0,), (0,)), ((), ()))
MESH_ID = pl.DeviceIdType.MESH


def _params(*sem):
    return pltpu.CompilerParams(dimension_semantics=sem, vmem_limit_bytes=VMEM_LIMIT)


def _tile(n, cap):
    if n <= cap:
        return n
    best = None
    for t in range(128, cap + 1, 128):
        if n % t == 0:
            best = t
    assert best is not None, (n, cap)
    return best


def _rows(n, cap):
    if n <= cap:
        return n
    for t in range(cap // 8 * 8, 7, -8):
        if n % t == 0:
            return t
    raise ValueError((n, cap))


def _dot(a, b):
    return jnp.dot(a.astype(MM_DTYPE), b.astype(MM_DTYPE), preferred_element_type=F32)


def _dot_nt(a, b):
    return lax.dot_general(a.astype(MM_DTYPE), b.astype(MM_DTYPE), NT_DIMS, preferred_element_type=F32)


def _dot_tn(a, b):
    return lax.dot_general(a.astype(MM_DTYPE), b.astype(MM_DTYPE), TN_DIMS, preferred_element_type=F32)


def _sigmoid(x):
    return 1.0 / (1.0 + jnp.exp(-x))


def _sigmoid_tanh(x):
    return 0.5 * jnp.tanh(0.5 * x) + 0.5


def mm_nn(a, b, name, res=None, out_dtype=F32, exact=False, ride=None):
    M, K = a.shape
    N = b.shape[1]
    def footprint(tm, tn):
        return 2 * (tm * K * a.dtype.itemsize + K * tn * b.dtype.itemsize) + 2 * tm * tn * 4 * (1 if res is None else 2)

    budget = 46 * 1024 * 1024
    tm = _rows(M, 512)
    tn = N if N <= 3328 and footprint(tm, N) <= budget else _tile(N, 1536)
    tk = K if footprint(tm, tn) <= budget else _tile(K, 1536)
    if tk == K and footprint(_rows(M, 1024), tn) <= budget:
        tm = _rows(M, 1024)
    nk = K // tk

    def body(*refs):
        if res is None:
            a_ref, b_ref, o_ref, acc = refs
            r_ref = None
        else:
            a_ref, b_ref, r_ref, o_ref, acc = refs
        if exact:
            p = jnp.dot(a_ref[...], b_ref[...], precision=HI, preferred_element_type=F32)
        else:
            p = _dot(a_ref[...], b_ref[...])

        def finish(total):
            if r_ref is not None:
                total = total + r_ref[...]
            o_ref[...] = total.astype(out_dtype)

        if nk == 1:
            finish(p)
        else:
            k = pl.program_id(2)

            @pl.when(k == 0)
            def _():
                acc[...] = p

            @pl.when(jnp.logical_and(k > 0, k < nk - 1))
            def _():
                acc[...] += p

            @pl.when(k == nk - 1)
            def _():
                finish(acc[...] + p)

    in_specs = [pl.BlockSpec((tm, tk), lambda j, i, k: (i, k)),
                pl.BlockSpec((tk, tn), lambda j, i, k: (k, j))]
    args = [a, b]
    if res is not None:
        in_specs.append(pl.BlockSpec((tm, tn), lambda j, i, k: (i, j)))
        args.append(res)
    acc_shape = (tm, tn) if nk > 1 else (8, 128)
    outs = _call(body, name, (N // tn, M // tm, nk), in_specs, [pl.BlockSpec((tm, tn), lambda j, i, k: (i, j))],
                 [jax.ShapeDtypeStruct((M, N), out_dtype)], [pltpu.VMEM(acc_shape, F32)],
                 ("parallel", "parallel", "arbitrary"), args, ride)
    return outs[0] if ride is None else outs


def mm_view(a, b, name, dil):
    T, K = a.shape
    tm = 1024

    def body(a_ref, b_ref, o_ref, sc):
        p = _dot(a_ref[...], b_ref[...])
        if dil == 1:
            o_ref[...] = p.astype(o_ref.dtype)
        else:
            _to_view(lambda c: p[:, c * 128:(c + 1) * 128], sc, o_ref, dil, 8, tm)

    return pl.pallas_call(
        body, name=name, grid=(T // tm,),
        in_specs=[pl.BlockSpec((tm, K), lambda i: (i, 0)), pl.BlockSpec((K, 1024), lambda i: (0, 0))],
        out_specs=pl.BlockSpec((tm // dil, dil * 1024), lambda i: (i, 0)),
        out_shape=jax.ShapeDtypeStruct((T // dil, dil * 1024), MM_DTYPE),
        scratch_shapes=[pltpu.VMEM((8, tm, 128), F32)],
        compiler_params=_params("parallel"),
    )(a, b)


def mm_tn(a, g, name, ride=None):
    T, Ka = a.shape
    N = g.shape[1]
    tka, tt = _tile(Ka, 1536), _rows(T, 1024)
    whole_n = 2 * (tt * tka * a.dtype.itemsize + tt * N * g.dtype.itemsize + tka * N * 4)
    tn = N if N <= 3328 and whole_n <= 46 * 1024 * 1024 else _tile(N, 1536)
    nt = T // tt

    def body(a_ref, g_ref, o_ref):
        t = pl.program_id(2)
        p = _dot_tn(a_ref[...], g_ref[...])

        @pl.when(t == 0)
        def _():
            o_ref[...] = p

        @pl.when(t > 0)
        def _():
            o_ref[...] += p

    outs = _call(body, name, (Ka // tka, N // tn, nt),
                 [pl.BlockSpec((tt, tka), lambda i, j, t: (t, i)), pl.BlockSpec((tt, tn), lambda i, j, t: (t, j))],
                 [pl.BlockSpec((tka, tn), lambda i, j, t: (i, j))], [jax.ShapeDtypeStruct((Ka, N), F32)], [],
                 ("parallel", "parallel", "arbitrary"), (a, g), ride)
    return outs[0] if ride is None else outs


def rms_fwd(x, gains, name):
    T, D = x.shape
    tt = _rows(T, 1024)
    ng = len(gains)

    def body(*refs):
        x_ref = refs[0]
        g_refs = refs[1:1 + ng]
        o_refs = refs[1 + ng:]
        xf = x_ref[...]
        y = xf * lax.rsqrt(jnp.mean(xf * xf, axis=-1, keepdims=True) + EPS)
        for g_ref, o_ref in zip(g_refs, o_refs):
            o_ref[...] = (y * g_ref[...]).astype(o_ref.dtype)

    row = pl.BlockSpec((tt, D), lambda i: (i, 0))
    gsp = pl.BlockSpec((1, D), lambda i: (0, 0))
    return pl.pallas_call(
        body, name=name, grid=(T // tt,),
        in_specs=[row] + [gsp] * ng, out_specs=[row] * ng,
        out_shape=[jax.ShapeDtypeStruct((T, D), MM_DTYPE)] * ng,
        compiler_params=_params("parallel"),
    )(x, *gains)


def rms_bwd(x, dres, branches, name):
    T, D = x.shape
    tt = _rows(T, 512)
    nb = len(branches)

    def body(*refs):
        x_ref, r_ref = refs[0], refs[1]
        dy_refs = refs[2:2 + nb]
        g_refs = refs[2 + nb:2 + 2 * nb]
        dx_ref = refs[2 + 2 * nb]
        dg_refs = refs[3 + 2 * nb:]
        i = pl.program_id(0)
        xf = x_ref[...]
        r = lax.rsqrt(jnp.mean(xf * xf, axis=-1, keepdims=True) + EPS)
        xh = xf * r
        dx = r_ref[...]
        for dy_ref, g_ref, dg_ref in zip(dy_refs, g_refs, dg_refs):
            dy = dy_ref[...].astype(F32)
            dyg = dy * g_ref[...]
            dx = dx + r * (dyg - xh * jnp.mean(dyg * xh, axis=-1, keepdims=True))
            part = jnp.sum(dy * xh, axis=0, keepdims=True)

            @pl.when(i == 0)
            def _():
                dg_ref[...] = part

            @pl.when(i > 0)
            def _():
                dg_ref[...] += part
        dx_ref[...] = dx

    row = pl.BlockSpec((tt, D), lambda i: (i, 0))
    gsp = pl.BlockSpec((1, D), lambda i: (0, 0))
    outs = pl.pallas_call(
        body, name=name, grid=(T // tt,),
        in_specs=[row, row] + [row] * nb + [gsp] * nb,
        out_specs=[row] + [gsp] * nb,
        out_shape=[jax.ShapeDtypeStruct((T, D), F32)] + [jax.ShapeDtypeStruct((1, D), F32)] * nb,
        compiler_params=_params("arbitrary"),
    )(x, dres, *[b[0] for b in branches], *[b[1] for b in branches])
    return outs[0], outs[1:]


def loss_head(x, target, gain):
    T, D = x.shape
    tt = _rows(T, 512)

    def body(x_ref, t_ref, g_ref, dx_ref, dg_ref, loss_ref):
        i = pl.program_id(0)
        xf = x_ref[...]
        g = g_ref[...]
        r = lax.rsqrt(jnp.mean(xf * xf, axis=-1, keepdims=True) + EPS)
        xh = xf * r
        e = xh * g - t_ref[...]
        lpart = 0.5 * jnp.sum(jnp.sum(e * e, axis=1, keepdims=True), axis=0, keepdims=True) / D
        dy = e / D
        dyg = dy * g
        dx_ref[...] = r * (dyg - xh * jnp.mean(dyg * xh, axis=-1, keepdims=True))
        gpart = jnp.sum(dy * xh, axis=0, keepdims=True)
        lrow = jnp.broadcast_to(lpart, (1, 128))

        @pl.when(i == 0)
        def _():
            dg_ref[...] = gpart
            loss_ref[...] = lrow

        @pl.when(i > 0)
        def _():
            dg_ref[...] += gpart
            loss_ref[...] += lrow

    row = pl.BlockSpec((tt, D), lambda i: (i, 0))
    gsp = pl.BlockSpec((1, D), lambda i: (0, 0))
    return pl.pallas_call(
        body, name="loss_head", grid=(T // tt,),
        in_specs=[row, row, gsp],
        out_specs=[row, gsp, pl.BlockSpec((1, 128), lambda i: (0, 0))],
        out_shape=[jax.ShapeDtypeStruct((T, D), F32), jax.ShapeDtypeStruct((1, D), F32),
                   jax.ShapeDtypeStruct((1, 128), F32)],
        compiler_params=_params("arbitrary"),
    )(x, target, gain)


def _shift_down(u, prev8, first, k):
    rolled = pltpu.roll(u, k, 0)
    rid = lax.broadcasted_iota(jnp.int32, u.shape, 0)
    halo = jnp.where(first, 0.0, prev8)
    out = rolled
    for j in range(k):
        out = jnp.where(rid == j, halo[8 - k + j:8 - k + j + 1, :], out)
    return out


def _conv3(u, prev8, first, w, b):
    return (_shift_down(u, prev8, first, 2) * w[0:1, :] + _shift_down(u, prev8, first, 1) * w[1:2, :]
            + u * w[2:3, :] + b)


def ffn_up_act(xn, w_up, w, b, name, ride=None):
    T, K = xn.shape
    tt = _rows(T, 2048)
    nj = D_FF // FF_TC

    def body(x_ref, wu_ref, w_ref, b_ref, u_ref, o_ref, tail):
        first = pl.program_id(1) == 0
        u = _dot(x_ref[...], wu_ref[...])
        u_ref[...] = u
        c = _conv3(u, tail[...], first, w_ref[...], b_ref[...])
        tail[...] = u[tt - 8:, :]
        cg, cv = c[:, :FF_TC], c[:, FF_TC:]
        o_ref[...] = (cg * _sigmoid_tanh(cg) * cv).astype(o_ref.dtype)

    return _call(
        body, name, (nj, T // tt),
        [pl.BlockSpec((tt, K), lambda j, i: (i, 0)),
         pl.BlockSpec((K, 2 * FF_TC), lambda j, i: (0, j)),
         pl.BlockSpec((3, 2 * FF_TC), lambda j, i: (0, j)),
         pl.BlockSpec((1, 2 * FF_TC), lambda j, i: (0, j))],
        [pl.BlockSpec((tt, 2 * FF_TC), lambda j, i: (i, j)), pl.BlockSpec((tt, FF_TC), lambda j, i: (i, j))],
        [jax.ShapeDtypeStruct((T, 2 * D_FF), F32), jax.ShapeDtypeStruct((T, D_FF), MM_DTYPE)],
        [pltpu.VMEM((8, 2 * FF_TC), F32)], ("parallel", "arbitrary"), (xn, w_up, w, b), ride)


def conv_act_bwd(u, da, w, b, name, ride=None):
    T = u.shape[0]
    tt = _rows(T, 2048)
    nt = T // tt
    nj = D_FF // FF_TC
    te = tt + 8

    def body(u_ref, p_ref, n_ref, da_ref, dan_ref, w_ref, b_ref, du_ref, dw_ref, db_ref):
        i = pl.program_id(1)
        first = i == 0
        last = i == nt - 1
        w = w_ref[...]
        ue = jnp.concatenate([u_ref[...], n_ref[...]], axis=0)
        dae = jnp.concatenate([da_ref[...], jnp.where(last, 0.0, dan_ref[...])], axis=0)
        um2 = _shift_down(ue, p_ref[...], first, 2)
        um1 = _shift_down(ue, p_ref[...], first, 1)
        c = um2 * w[0:1, :] + um1 * w[1:2, :] + ue * w[2:3, :] + b_ref[...]
        cg, cv = c[:, :FF_TC], c[:, FF_TC:]
        s = _sigmoid_tanh(cg)
        dcg = dae * cv * (s * (1.0 + cg * (1.0 - s)))
        dcv = dae * (cg * s)
        dc = jnp.concatenate([dcg, dcv], axis=1)
        du = (dc * w[2:3, :] + pltpu.roll(dc, te - 1, 0) * w[1:2, :] + pltpu.roll(dc, te - 2, 0) * w[0:1, :])
        du_ref[...] = du[:tt, :].astype(du_ref.dtype)
        dcm = dc[:tt, :]
        dwp = jnp.concatenate([jnp.sum(dcm * um2[:tt, :], axis=0, keepdims=True),
                               jnp.sum(dcm * um1[:tt, :], axis=0, keepdims=True),
                               jnp.sum(dcm * ue[:tt, :], axis=0, keepdims=True)], axis=0)
        dbp = jnp.sum(dcm, axis=0, keepdims=True)

        @pl.when(first)
        def _():
            dw_ref[...] = dwp
            db_ref[...] = dbp

        @pl.when(i > 0)
        def _():
            dw_ref[...] += dwp
            db_ref[...] += dbp

    nb8 = T // 8
    return _call(
        body, name, (nj, nt),
        [pl.BlockSpec((tt, 2 * FF_TC), lambda j, i: (i, j)),
         pl.BlockSpec((8, 2 * FF_TC), lambda j, i: (jnp.maximum(i * (tt // 8) - 1, 0), j)),
         pl.BlockSpec((8, 2 * FF_TC), lambda j, i: (jnp.minimum((i + 1) * (tt // 8), nb8 - 1), j)),
         pl.BlockSpec((tt, FF_TC), lambda j, i: (i, j)),
         pl.BlockSpec((8, FF_TC), lambda j, i: (jnp.minimum((i + 1) * (tt // 8), nb8 - 1), j)),
         pl.BlockSpec((3, 2 * FF_TC), lambda j, i: (0, j)),
         pl.BlockSpec((1, 2 * FF_TC), lambda j, i: (0, j))],
        [pl.BlockSpec((tt, 2 * FF_TC), lambda j, i: (i, j)),
         pl.BlockSpec((3, 2 * FF_TC), lambda j, i: (0, j)),
         pl.BlockSpec((1, 2 * FF_TC), lambda j, i: (0, j))],
        [jax.ShapeDtypeStruct((T, 2 * D_FF), MM_DTYPE), jax.ShapeDtypeStruct((3, 2 * D_FF), F32),
         jax.ShapeDtypeStruct((1, 2 * D_FF), F32)],
        [], ("parallel", "arbitrary"), (u, u, u, da, da, w, b), ride)


def _interleave(a):
    lead = a.shape[:-1]
    nj = D_FF // FF_TC
    return jnp.swapaxes(a.reshape(*lead, 2, nj, FF_TC), -3, -2).reshape(*lead, 2 * D_FF)


def _deinterleave(a):
    lead = a.shape[:-1]
    nj = D_FF // FF_TC
    return jnp.swapaxes(a.reshape(*lead, nj, 2, FF_TC), -3, -2).reshape(*lead, 2 * D_FF)


A_GC = 1
A_TB = A_GC * A_CHUNK


def gate_prep(z, bias128):
    T = z.shape[0]
    tt = _rows(T, 512)

    def body(z_ref, b_ref, gc_ref, gr_ref):
        pre = z_ref[...] + b_ref[...]
        sc = SOFTCAP * jnp.tanh(pre / SOFTCAP)
        lf = jnp.minimum(sc, 0.0) - jnp.log(1.0 + jnp.exp(-jnp.abs(sc)))
        col = lax.broadcasted_iota(jnp.int32, pre.shape, 1)
        isf = jnp.logical_and(col >= A_HEADS, col < 2 * A_HEADS)
        r = lax.broadcasted_iota(jnp.int32, (tt, tt), 0)
        c = lax.broadcasted_iota(jnp.int32, (tt, tt), 1)
        bits = A_CHUNK.bit_length() - 1
        tri = jnp.logical_and(jnp.right_shift(r, bits) == jnp.right_shift(c, bits), c <= r).astype(F32)
        bcum = jnp.dot(tri, jnp.where(isf, lf, 0.0), precision=HI, preferred_element_type=F32)
        g = jnp.where(col < A_HEADS, sc, jnp.where(isf, bcum, 0.0))
        gc_ref[...] = g
        for s in range(tt // 128):
            gr_ref[s] = g[s * 128:(s + 1) * 128, :].T[0:8, :]

    return pl.pallas_call(
        body, name="gate_prep", grid=(T // tt,),
        in_specs=[pl.BlockSpec((tt, 128), lambda i: (i, GATE_COL // 128)),
                  pl.BlockSpec((1, 128), lambda i: (0, 0))],
        out_specs=[pl.BlockSpec((tt, 128), lambda i: (i, 0)),
                   pl.BlockSpec((tt // 128, 8, 128), lambda i: (i, 0, 0))],
        out_shape=[jax.ShapeDtypeStruct((T, 128), F32), jax.ShapeDtypeStruct((T // 128, 8, 128), F32)],
        compiler_params=_params("parallel"),
    )(z, bias128)


def _chunk_decay(A, qh, bc, br, lir, n, m, causal):
    logD = jnp.where(causal, bc - br + lir, -jnp.inf)
    m_inter = bc + m
    m_t = jnp.maximum(m_inter, jnp.max(logD, axis=1, keepdims=True))
    E = jnp.exp(logD - m_t)
    Sm = A * E
    wi = jnp.exp(m_inter - m_t)
    qn = jnp.sum(qh.astype(F32) * n, axis=1, keepdims=True)
    den = jnp.sum(Sm, axis=1, keepdims=True) + wi * qn
    gs = jnp.maximum(jnp.abs(den), jnp.exp(-m_t))
    return E, Sm, wi, den, gs, m_t


def _state_weights(bc, lic, br, lir, m):
    bL = bc[A_CHUNK - 1:A_CHUNK, :]
    m_new = jnp.maximum(bL + m, jnp.max(bL - br + lir, axis=1, keepdims=True))
    wk = jnp.exp(bL - bc + lic - m_new)
    decay = jnp.exp(bL + m - m_new)
    return wk, decay, m_new


def _head_slices(h):
    return (slice(h * A_QK, (h + 1) * A_QK), slice(h * A_V, (h + 1) * A_V))


def mlstm_fwd(z, gcol, grow, hng, ride=None):
    T = z.shape[0]
    NC = T // A_CHUNK
    scale = A_QK ** -0.5

    def body(q_ref, k_ref, v_ref, o_ref, gc_ref, gr_ref, hng_ref, hg_ref, Cs_ref, ns_ref, ms_ref,
             C_sc, n_sc, m_sc):
        @pl.when(pl.program_id(0) == 0)
        def _():
            C_sc[...] = jnp.zeros_like(C_sc)
            n_sc[...] = jnp.zeros_like(n_sc)
            m_sc[...] = jnp.zeros_like(m_sc)

        ri = lax.broadcasted_iota(jnp.int32, (A_CHUNK, A_CHUNK), 0)
        ci = lax.broadcasted_iota(jnp.int32, (A_CHUNK, A_CHUNK), 1)
        causal = ri >= ci
        gr = jnp.concatenate([gr_ref[s] for s in range(A_TB // 128)], axis=1)
        for c in range(A_GC):
            rows = slice(c * A_CHUNK, (c + 1) * A_CHUNK)
            gc = gc_ref[rows, :]
            grc = gr[:, c * A_CHUNK:(c + 1) * A_CHUNK]
            for h in range(A_HEADS):
                sk, sv = _head_slices(h)
                qh = (q_ref[rows, sk] * scale).astype(MM_DTYPE)
                kh = k_ref[rows, sk].astype(MM_DTYPE)
                vh = v_ref[rows, sv].astype(MM_DTYPE)
                lic, bc = gc[:, h:h + 1], gc[:, A_HEADS + h:A_HEADS + h + 1]
                lir, br = grc[h:h + 1, :], grc[A_HEADS + h:A_HEADS + h + 1, :]
                C, n, m = C_sc[h], n_sc[h], m_sc[h][:, 0:1]
                Cs_ref[c, h] = C
                ns_ref[c, h] = n
                ms_ref[c, h] = m_sc[h]
                _, Sm, wi, _, gs, _ = _chunk_decay(_dot_nt(qh, kh), qh, bc, br, lir, n, m, causal)
                hh = (_dot(Sm, vh) + wi * _dot(qh, C)) / gs
                hn = hh * lax.rsqrt(jnp.mean(hh * hh, axis=1, keepdims=True) + EPS) * hng_ref[:, sv]
                hg_ref[rows, sv] = (hn * _sigmoid(o_ref[rows, sv])).astype(hg_ref.dtype)
                wk, decay, m_new = _state_weights(bc, lic, br, lir, m)
                kw = kh.astype(F32) * wk
                C_sc[h] = decay * C + _dot_tn(kw, vh)
                n_sc[h] = decay * n + jnp.sum(kw, axis=0, keepdims=True)
                m_sc[h] = jnp.broadcast_to(m_new, (1, 128))

    tok = lambda w, cb: pl.BlockSpec((A_TB, w), lambda i: (i, cb))
    return _call(
        body, "mlstm_fwd", (NC // A_GC,),
        [tok(512, 0), tok(512, 1), tok(1024, 1), tok(1024, 2),
         pl.BlockSpec((A_TB, 128), lambda i: (i, 0)),
         pl.BlockSpec((A_TB // 128, 8, 128), lambda i: (i, 0, 0)),
         pl.BlockSpec((1, 1024), lambda i: (0, 0))],
        [pl.BlockSpec((A_TB, 1024), lambda i: (i, 0)),
         pl.BlockSpec((A_GC, A_HEADS, A_QK, A_V), lambda i: (i, 0, 0, 0)),
         pl.BlockSpec((A_GC, A_HEADS, 1, 128), lambda i: (i, 0, 0, 0)),
         pl.BlockSpec((A_GC, A_HEADS, 1, 128), lambda i: (i, 0, 0, 0))],
        [jax.ShapeDtypeStruct((T, 1024), MM_DTYPE),
         jax.ShapeDtypeStruct((NC, A_HEADS, A_QK, A_V), F32),
         jax.ShapeDtypeStruct((NC, A_HEADS, 1, 128), F32),
         jax.ShapeDtypeStruct((NC, A_HEADS, 1, 128), F32)],
        [pltpu.VMEM((A_HEADS, A_QK, A_V), F32), pltpu.VMEM((A_HEADS, 1, 128), F32),
         pltpu.VMEM((A_HEADS, 1, 128), F32)],
        ("arbitrary",), (z, z, z, z, gcol, grow, hng), ride)


def mlstm_bwd(z, gcol, grow, hng, bias128, Cs, ns, ms, dhg, ride=None):
    T = z.shape[0]
    NC = T // A_CHUNK
    nsteps = NC // A_GC
    scale = A_QK ** -0.5

    def body(q_ref, k_ref, v_ref, o_ref, zg_ref, gc_ref, gr_ref, hng_ref, b_ref, Cs_ref, ns_ref, ms_ref,
             dhg_ref, dz_ref, dgn_ref, dbif_ref, dC_sc, dn_sc):
        @pl.when(pl.program_id(0) == 0)
        def _():
            dC_sc[...] = jnp.zeros_like(dC_sc)
            dn_sc[...] = jnp.zeros_like(dn_sc)
            dgn_ref[...] = jnp.zeros_like(dgn_ref)
            dbif_ref[...] = jnp.zeros_like(dbif_ref)

        ri = lax.broadcasted_iota(jnp.int32, (A_CHUNK, A_CHUNK), 0)
        ci = lax.broadcasted_iota(jnp.int32, (A_CHUNK, A_CHUNK), 1)
        causal = ri >= ci
        upper = (ci >= ri).astype(F32)
        rid = lax.broadcasted_iota(jnp.int32, (A_CHUNK, 1), 0)
        col = lax.broadcasted_iota(jnp.int32, (A_CHUNK, 128), 1)
        gr = jnp.concatenate([gr_ref[s] for s in range(A_TB // 128)], axis=1)
        for c in reversed(range(A_GC)):
            rows = slice(c * A_CHUNK, (c + 1) * A_CHUNK)
            gc = gc_ref[rows, :]
            grc = gr[:, c * A_CHUNK:(c + 1) * A_CHUNK]
            dG = jnp.zeros((A_CHUNK, 128), F32)
            hs = []
            for h in range(A_HEADS):
                sk, sv = _head_slices(h)
                s = dict(sk=sk, sv=sv, qh=(q_ref[rows, sk] * scale).astype(MM_DTYPE),
                         kh=k_ref[rows, sk].astype(MM_DTYPE), vh=v_ref[rows, sv].astype(MM_DTYPE),
                         lic=gc[:, h:h + 1], bc=gc[:, A_HEADS + h:A_HEADS + h + 1],
                         lir=grc[h:h + 1, :], br=grc[A_HEADS + h:A_HEADS + h + 1, :],
                         C=Cs_ref[c, h], n=ns_ref[c, h], m=ms_ref[c, h][:, 0:1], dC=dC_sc[h], dn=dn_sc[h])
                s['qf'], s['kf'] = s['qh'].astype(F32), s['kh'].astype(F32)
                s['wk'], s['decay'], _ = _state_weights(s['bc'], s['lic'], s['br'], s['lir'], s['m'])
                hs.append(s)
            for s in hs:
                s['A'] = _dot_nt(s['qh'], s['kh'])
                s['qC'] = _dot(s['qh'], s['C'])
                s['vdC'] = _dot_nt(s['vh'], s['dC'])
                s['kdC'] = _dot(s['kh'], s['dC'])
            for s in hs:
                s['E'], s['Sm'], s['wi'], s['den'], s['gs'], s['m_t'] = _chunk_decay(
                    s['A'], s['qh'], s['bc'], s['br'], s['lir'], s['n'], s['m'], causal)
            for s in hs:
                s['num'] = _dot(s['Sm'], s['vh']) + s['wi'] * s['qC']
            for h, s in enumerate(hs):
                sv, gs = s['sv'], s['gs']
                hh = s['num'] / gs
                r = lax.rsqrt(jnp.mean(hh * hh, axis=1, keepdims=True) + EPS)
                gn = hng_ref[:, sv]
                sg = _sigmoid(o_ref[rows, sv])
                dhg_h = dhg_ref[rows, sv]
                dhn = dhg_h * sg
                dz_ref[rows, 2048 + h * A_V:2048 + (h + 1) * A_V] = (
                    dhg_h * (hh * r * gn) * sg * (1.0 - sg)).astype(dz_ref.dtype)
                dgn_ref[:, sv] += jnp.sum(dhn * hh * r, axis=0, keepdims=True)
                dyg = dhn * gn
                dh = r * dyg - hh * (r * r * r) * jnp.mean(dyg * hh, axis=1, keepdims=True)
                s['dnum'] = dh / gs
                live = (jnp.abs(s['den']) > jnp.exp(-s['m_t'])).astype(F32)
                s['dden'] = -jnp.sum(dh * hh, axis=1, keepdims=True) / gs * jnp.sign(s['den']) * live
            for s in hs:
                s['dnv'] = _dot_nt(s['dnum'], s['vh'])
                s['dnC'] = _dot_nt(s['dnum'], s['C'])
            for s in hs:
                s['dSE'] = jnp.where(causal, s['dnv'] + s['dden'], 0.0) * s['E']
            for s in hs:
                s['dq'] = _dot(s['dSE'], s['kh']) + s['wi'] * (s['dnC'] + s['dden'] * s['n'])
                s['dk_inter'] = s['wk'] * (s['vdC'] + s['dn'])
                s['dk'] = _dot_tn(s['dSE'], s['qh']) + s['dk_inter']
                s['dv'] = _dot_tn(s['Sm'], s['dnum']) + s['wk'] * s['kdC']
                s['dCq'] = _dot_tn(s['qf'] * s['wi'], s['dnum'])
            for h, s in enumerate(hs):
                dq, dk, qf, kf, dC, dn = s['dq'], s['dk'], s['qf'], s['kf'], s['dC'], s['dn']
                dz_ref[rows, s['sk']] = (dq * scale).astype(dz_ref.dtype)
                dz_ref[rows, 512 + h * A_QK:512 + (h + 1) * A_QK] = dk.astype(dz_ref.dtype)
                dz_ref[rows, 1024 + h * A_V:1024 + (h + 1) * A_V] = s['dv'].astype(dz_ref.dtype)
                dli = jnp.sum(kf * dk, axis=1, keepdims=True)
                db = jnp.sum(qf * dq, axis=1, keepdims=True) - dli
                usum = jnp.sum(jnp.sum(kf * s['dk_inter'], axis=1, keepdims=True), axis=0, keepdims=True)
                ddecay = (jnp.sum(jnp.sum(dC * s['C'], axis=1, keepdims=True), axis=0, keepdims=True)
                          + jnp.sum(dn * s['n'], axis=1, keepdims=True))
                db = db + jnp.where(rid == A_CHUNK - 1, usum + ddecay * s['decay'], 0.0)
                dG = dG + jnp.where(col == h, dli, 0.0) + jnp.where(col == A_HEADS + h, db, 0.0)
                dC_sc[h] = s['decay'] * dC + s['dCq']
                dn_sc[h] = s['decay'] * dn + jnp.sum(qf * (s['wi'] * s['dden']), axis=0, keepdims=True)
            dlf = jnp.dot(upper, dG, precision=HI, preferred_element_type=F32)
            pre = zg_ref[rows, :] + b_ref[...]
            th = jnp.tanh(pre / SOFTCAP)
            dcap = 1.0 - th * th
            dpre = jnp.where(col < A_HEADS, dG * dcap,
                             jnp.where(col < 2 * A_HEADS, dlf * _sigmoid(-SOFTCAP * th) * dcap, 0.0))
            dz_ref[rows, GATE_COL:GATE_COL + 128] = dpre.astype(dz_ref.dtype)
            dbif_ref[...] += jnp.sum(dpre, axis=0, keepdims=True)

    rev = lambda i: nsteps - 1 - i
    tok = lambda w, cb: pl.BlockSpec((A_TB, w), lambda i: (rev(i), cb))
    st = lambda a, b: pl.BlockSpec((A_GC, A_HEADS, a, b), lambda i: (rev(i), 0, 0, 0))
    return _call(
        body, "mlstm_bwd", (nsteps,),
        [tok(512, 0), tok(512, 1), tok(1024, 1), tok(1024, 2), tok(128, GATE_COL // 128),
         pl.BlockSpec((A_TB, 128), lambda i: (rev(i), 0)),
         pl.BlockSpec((A_TB // 128, 8, 128), lambda i: (rev(i), 0, 0)),
         pl.BlockSpec((1, 1024), lambda i: (0, 0)),
         pl.BlockSpec((1, 128), lambda i: (0, 0)),
         st(A_QK, A_V), st(1, 128), st(1, 128),
         pl.BlockSpec((A_TB, 1024), lambda i: (rev(i), 0))],
        [pl.BlockSpec((A_TB, A_IN_PAD), lambda i: (rev(i), 0)),
         pl.BlockSpec((1, 1024), lambda i: (0, 0)),
         pl.BlockSpec((1, 128), lambda i: (0, 0))],
        [jax.ShapeDtypeStruct((T, A_IN_PAD), MM_DTYPE), jax.ShapeDtypeStruct((1, 1024), F32),
         jax.ShapeDtypeStruct((1, 128), F32)],
        [pltpu.VMEM((A_HEADS, A_QK, A_V), F32), pltpu.VMEM((A_HEADS, 1, 128), F32)],
        ("arbitrary",), (z, z, z, z, z, gcol, grow, hng, bias128, Cs, ns, ms, dhg), ride)


def _t5_bucket(dist):
    max_exact = REL_BUCKETS // 2
    d = np.maximum(dist, 0)
    log_ratio = np.log(np.maximum(d, 1) / max_exact) / math.log(REL_MAX_DIST / max_exact)
    large = np.minimum(max_exact + (log_ratio * (REL_BUCKETS - max_exact)).astype(np.int64), REL_BUCKETS - 1)
    return np.where(d < max_exact, d, large).astype(np.int32)


def _group_bucket(g):
    delta = B_BLOCK + np.arange(B_BLOCK)[:, None] - np.arange(2 * B_BLOCK)[None, :]
    return _t5_bucket(delta * DILATIONS[g])


def _band_mask(n):
    ri = lax.broadcasted_iota(jnp.int32, (B_BLOCK, 2 * B_BLOCK), 0)
    ci = lax.broadcasted_iota(jnp.int32, (B_BLOCK, 2 * B_BLOCK), 1)
    band = jnp.logical_and(ci >= ri, ci <= ri + B_BLOCK)
    return jnp.logical_and(band, jnp.logical_or(ci >= B_BLOCK, n > 0))


def _both(p_ref, c_ref, sl):
    return jnp.concatenate([p_ref[:, sl], c_ref[:, sl]], axis=0)


def _scores(qh, kh, bias_h, valid):
    return jnp.where(valid, _dot_nt(qh, kh) * (B_DH ** -0.5) + bias_h, -jnp.inf)


def _attn_specs():
    wide = pl.BlockSpec((B_BLOCK, 1024), lambda r, n: (n, r))
    prev = pl.BlockSpec((B_BLOCK, 1024), lambda r, n: (jnp.maximum(n - 1, 0), r))
    narrow = pl.BlockSpec((B_BLOCK, 128), lambda r, n: (n, r))
    bias = pl.BlockSpec((B_HEADS, B_BLOCK, 2 * B_BLOCK), lambda r, n: (0, 0, 0))
    return wide, prev, narrow, bias


def _to_view(read_chunk, sc, o_ref, dil, nc, tt):
    for c in range(nc):
        sc[c] = read_chunk(c)
    for r in range(dil):
        for c in range(nc):
            lo = (r * nc + c) * 128
            o_ref[:, lo:lo + 128] = sc[c, pl.ds(r, tt // dil, stride=dil), :].astype(o_ref.dtype)


def _from_view(read_view, sc, dil, nc, tt):
    for r in range(dil):
        for c in range(nc):
            sc[c, pl.ds(r, tt // dil, stride=dil), :] = read_view((r * nc + c) * 128).astype(F32)


def attn_fwd(qv, kvw, vvw, bias, g):
    dil = DILATIONS[g]
    Tv = qv.shape[0]
    nb = Tv // B_BLOCK
    wide, prev, narrow, bsp = _attn_specs()

    def body(q_ref, kp_ref, kc_ref, vp_ref, vc_ref, b_ref, o_ref, lse_ref):
        valid = _band_mask(pl.program_id(1))
        lse_ref[...] = jnp.zeros_like(lse_ref)
        heads = [slice(h * B_DH, (h + 1) * B_DH) for h in range(B_HEADS)]
        S = [_scores(q_ref[:, sl], _both(kp_ref, kc_ref, sl), b_ref[h], valid) for h, sl in enumerate(heads)]
        P, L = [], []
        for h in range(B_HEADS):
            m = jnp.max(S[h], axis=1, keepdims=True)
            p = jnp.exp(S[h] - m)
            l = jnp.sum(p, axis=1, keepdims=True)
            lse_ref[:, h:h + 1] = m + jnp.log(l)
            P.append(p.astype(MM_DTYPE))
            L.append(l)
        for h, sl in enumerate(heads):
            o_ref[:, sl] = _dot(P[h], _both(vp_ref, vc_ref, sl)) / L[h]

    return pl.pallas_call(
        body, name=f"attn_fwd_g{g}", grid=(dil, nb),
        in_specs=[wide, prev, wide, prev, wide, bsp], out_specs=[wide, narrow],
        out_shape=[jax.ShapeDtypeStruct((Tv, dil * 1024), F32), jax.ShapeDtypeStruct((Tv, dil * 128), F32)],
        compiler_params=_params("parallel", "parallel"),
    )(qv, kvw, kvw, vvw, vvw, bias)


def attn_bwd(qv, kvw, vvw, bias, do_v, lse_v, dl_v, g):
    dil = DILATIONS[g]
    Tv = qv.shape[0]
    nb = Tv // B_BLOCK
    wide, prev, narrow, bsp = _attn_specs()

    def body(q_ref, kp_ref, kc_ref, vp_ref, vc_ref, b_ref, bt_ref, do_ref, lse_ref, dl_ref,
             dq_ref, dkc_ref, dkp_ref, dvc_ref, dvp_ref, db_ref):
        @pl.when(jnp.logical_and(pl.program_id(0) == 0, pl.program_id(1) == 0))
        def _():
            db_ref[...] = jnp.zeros_like(db_ref)

        n = pl.program_id(1)
        valid = _band_mask(n)
        ki = lax.broadcasted_iota(jnp.int32, (2 * B_BLOCK, B_BLOCK), 0)
        qi = lax.broadcasted_iota(jnp.int32, (2 * B_BLOCK, B_BLOCK), 1)
        valid_t = jnp.logical_and(jnp.logical_and(ki >= qi, ki <= qi + B_BLOCK), jnp.logical_or(ki >= B_BLOCK, n > 0))
        lse_t, dl_t = lse_ref[...].T, dl_ref[...].T
        heads = [slice(h * B_DH, (h + 1) * B_DH) for h in range(B_HEADS)]
        scale = B_DH ** -0.5
        PT, DS, DST = [], [], []
        for h, sl in enumerate(heads):
            qh, doh = q_ref[:, sl], do_ref[:, sl].astype(MM_DTYPE)
            kh, vh = _both(kp_ref, kc_ref, sl), _both(vp_ref, vc_ref, sl)
            p = jnp.exp(_scores(qh, kh, b_ref[h], valid) - lse_ref[:, h:h + 1])
            ds = p * (_dot_nt(doh, vh) - dl_ref[:, h:h + 1])
            db_ref[h] += ds
            DS.append((ds * scale).astype(MM_DTYPE))
            pt = jnp.exp(_scores(kh, qh, bt_ref[h], valid_t) - lse_t[h:h + 1, :])
            PT.append(pt.astype(MM_DTYPE))
            DST.append((pt * (_dot_nt(vh, doh) - dl_t[h:h + 1, :]) * scale).astype(MM_DTYPE))
        for h, sl in enumerate(heads):
            qh, doh = q_ref[:, sl], do_ref[:, sl].astype(MM_DTYPE)
            dq_ref[:, sl] = _dot(DS[h], _both(kp_ref, kc_ref, sl)).astype(MM_DTYPE)
            dk = _dot(DST[h], qh).astype(MM_DTYPE)
            dv = _dot(PT[h], doh).astype(MM_DTYPE)
            dkp_ref[:, sl], dkc_ref[:, sl] = dk[:B_BLOCK], dk[B_BLOCK:]
            dvp_ref[:, sl], dvc_ref[:, sl] = dv[:B_BLOCK], dv[B_BLOCK:]

    big = jax.ShapeDtypeStruct((Tv, dil * 1024), MM_DTYPE)
    bsp_t = pl.BlockSpec((B_HEADS, 2 * B_BLOCK, B_BLOCK), lambda r, n: (0, 0, 0))
    return pl.pallas_call(
        body, name=f"attn_bwd_g{g}", grid=(dil, nb),
        in_specs=[wide, prev, wide, prev, wide, bsp, bsp_t, wide, narrow, narrow],
        out_specs=[wide] * 5 + [bsp],
        out_shape=[big] * 5 + [jax.ShapeDtypeStruct((B_HEADS, B_BLOCK, 2 * B_BLOCK), F32)],
        compiler_params=_params("arbitrary", "arbitrary"),
    )(qv, kvw, kvw, vvw, vvw, bias, jnp.swapaxes(bias, 1, 2), do_v, lse_v, dl_v)


def _head_expand():
    e = np.zeros((128, 1024), np.float32)
    for h in range(B_HEADS):
        e[h, h * B_DH:(h + 1) * B_DH] = 1.0
    return e


A_TT = 256
A_TT_WIDE = 512


def _view_spec(dil, width, tt=A_TT):
    return pl.BlockSpec((tt // dil, dil * width), lambda i: (i, 0))


def attn_merge(os_v, lses_v):
    T = os_v[0].shape[0]
    tt = A_TT_WIDE
    expand = jnp.asarray(_head_expand())

    def body(o0, o1, o2, l0, l1, l2, e_ref, ob_ref, of_ref, lse0_ref, lse1_ref, lse2_ref, sc_o, sc_l):
        for gi, (o_ref, l_ref) in enumerate(((o1, l1), (o2, l2))):
            dil = DILATIONS[gi + 1]
            _from_view(lambda lo: o_ref[:, lo:lo + 128], sc_o.at[gi], dil, 8, tt)
            _from_view(lambda lo: l_ref[:, lo:lo + 128], sc_l.at[gi], dil, 1, tt)
        ls = [l0[...], sc_l[0, 0], sc_l[1, 0]]
        m = jnp.maximum(jnp.maximum(ls[0], ls[1]), ls[2])
        ex = [jnp.exp(l - m) for l in ls]
        tot = ex[0] + ex[1] + ex[2]
        lse = m + jnp.log(tot)
        lse0_ref[...] = lse
        _to_view(lambda c: lse, sc_l.at[2], lse1_ref, DILATIONS[1], 1, tt)
        _to_view(lambda c: lse, sc_l.at[2], lse2_ref, DILATIONS[2], 1, tt)
        ws = [e / tot for e in ex]
        for c in range(8):
            cols = slice(c * 128, (c + 1) * 128)
            ecol = e_ref[:, cols]
            spread = [jnp.dot(w, ecol, precision=HI, preferred_element_type=F32) for w in ws]
            out = spread[0] * o0[:, cols] + spread[1] * sc_o[0, c] + spread[2] * sc_o[1, c]
            of_ref[:, cols] = out
            ob_ref[:, cols] = out.astype(ob_ref.dtype)

    wide = pl.BlockSpec((tt, 1024), lambda i: (i, 0))
    return pl.pallas_call(
        body, name="attn_merge", grid=(T // tt,),
        in_specs=[_view_spec(d, 1024, tt) for d in DILATIONS] + [_view_spec(d, 128, tt) for d in DILATIONS]
        + [pl.BlockSpec((128, 1024), lambda i: (0, 0))],
        out_specs=[wide, wide] + [_view_spec(d, 128, tt) for d in DILATIONS],
        out_shape=[jax.ShapeDtypeStruct((T, 1024), MM_DTYPE), jax.ShapeDtypeStruct((T, 1024), F32)]
        + [jax.ShapeDtypeStruct((T // d, d * 128), F32) for d in DILATIONS],
        scratch_shapes=[pltpu.VMEM((2, 8, tt, 128), F32), pltpu.VMEM((3, 1, tt, 128), F32)],
        compiler_params=_params("parallel"),
    )(*os_v, *lses_v, expand)


def attn_prep(datt, out):
    T = datt.shape[0]
    tt = A_TT_WIDE
    expand_t = jnp.asarray(_head_expand().T.copy())

    def body(d_ref, o_ref, e_ref, do0, do1, do2, dl0, dl1, dl2, sc_d, sc_l):
        delta = jnp.dot(d_ref[...] * o_ref[...], e_ref[...], precision=HI, preferred_element_type=F32)
        do0[...] = d_ref[...].astype(do0.dtype)
        dl0[...] = delta
        for do_ref, dl_ref, dil in ((do1, dl1, DILATIONS[1]), (do2, dl2, DILATIONS[2])):
            _to_view(lambda c: d_ref[:, c * 128:(c + 1) * 128], sc_d, do_ref, dil, 8, tt)
            _to_view(lambda c: delta, sc_l, dl_ref, dil, 1, tt)

    wide = pl.BlockSpec((tt, 1024), lambda i: (i, 0))
    return pl.pallas_call(
        body, name="attn_prep", grid=(T // tt,),
        in_specs=[wide, wide, pl.BlockSpec((1024, 128), lambda i: (0, 0))],
        out_specs=[_view_spec(d, 1024, tt) for d in DILATIONS] + [_view_spec(d, 128, tt) for d in DILATIONS],
        out_shape=[jax.ShapeDtypeStruct((T // d, d * 1024), MM_DTYPE) for d in DILATIONS]
        + [jax.ShapeDtypeStruct((T // d, d * 128), F32) for d in DILATIONS],
        scratch_shapes=[pltpu.VMEM((8, tt, 128), F32), pltpu.VMEM((1, tt, 128), F32)],
        compiler_params=_params("parallel"),
    )(datt, out, expand_t)


def attn_combine(parts):
    T = parts[0][0].shape[0]
    tt = A_TT
    nt = T // tt
    shift = [None] + [B_BLOCK * d // tt for d in DILATIONS[1:]]

    def body(dq0, kc0, vc0, kpa0, kpb0, vpa0, vpb0, dq1, kc1, kp1, vc1, vp1, dq2, kc2, kp2, vc2, vp2,
             dq_ref, dkv_ref, sc):
        i = pl.program_id(0)
        dq_ref[:, 0:1024] = dq0[...].astype(dq_ref.dtype)
        for col, c_ref, pa_ref, pb_ref in ((0, kc0, kpa0, kpb0), (3, vc0, vpa0, vpb0)):
            nxt = jnp.where(i + 1 < nt, pb_ref[:tt // 2, :].astype(F32), 0.0)
            later = jnp.concatenate([pa_ref[tt // 2:, :].astype(F32), nxt], axis=0)
            dkv_ref[:, col * 1024:(col + 1) * 1024] = (c_ref[...].astype(F32) + later).astype(dkv_ref.dtype)
        for g, (dq, kc, kp, vc, vp) in ((1, (dq1, kc1, kp1, vc1, vp1)), (2, (dq2, kc2, kp2, vc2, vp2))):
            dil = DILATIONS[g]
            live = i + shift[g] < nt
            _from_view(lambda lo: dq[:, lo:lo + 128], sc, dil, 8, tt)
            for c in range(8):
                dq_ref[:, g * 1024 + c * 128:g * 1024 + (c + 1) * 128] = sc[c].astype(dq_ref.dtype)
            for col, c_ref, p_ref in ((g, kc, kp), (3 + g, vc, vp)):
                _from_view(lambda lo: c_ref[:, lo:lo + 128].astype(F32)
                           + jnp.where(live, p_ref[:, lo:lo + 128].astype(F32), 0.0), sc, dil, 8, tt)
                for c in range(8):
                    dkv_ref[:, col * 1024 + c * 128:col * 1024 + (c + 1) * 128] = sc[c].astype(dkv_ref.dtype)

    def later_spec(dil, blocks):
        return pl.BlockSpec((tt // dil, dil * 1024), lambda i: (jnp.minimum(i + blocks, nt - 1), 0))

    cur = [_view_spec(d, 1024) for d in DILATIONS]
    in_specs = [cur[0], cur[0], cur[0], cur[0], later_spec(1, 1), cur[0], later_spec(1, 1)]
    args = [parts[0][0], parts[0][1], parts[0][3], parts[0][2], parts[0][2], parts[0][4], parts[0][4]]
    for g in (1, 2):
        in_specs += [cur[g], cur[g], later_spec(DILATIONS[g], shift[g]), cur[g], later_spec(DILATIONS[g], shift[g])]
        args += list(parts[g][:5])
    return pl.pallas_call(
        body, name="attn_combine", grid=(nt,), in_specs=in_specs,
        out_specs=[pl.BlockSpec((tt, 3072), lambda i: (i, 0)), pl.BlockSpec((tt, 6144), lambda i: (i, 0))],
        out_shape=[jax.ShapeDtypeStruct((T, 3072), MM_DTYPE), jax.ShapeDtypeStruct((T, 6144), MM_DTYPE)],
        scratch_shapes=[pltpu.VMEM((8, tt, 128), F32)],
        compiler_params=_params("parallel"),
    )(*args)


def adamw(w, g, m, v, name):
    R, C = w.shape
    tr = R if R * C * 4 <= (1 << 20) else _rows(R, max(8, ((1 << 20) // (C * 4)) // 8 * 8))

    def body(w_ref, g_ref, m_ref, v_ref, d_ref, nm_ref, nv_ref):
        gg = g_ref[...]
        nm = ADAM_B1 * m_ref[...] + (1.0 - ADAM_B1) * gg
        nv = ADAM_B2 * v_ref[...] + (1.0 - ADAM_B2) * (gg * gg)
        m_hat = nm / (1.0 - ADAM_B1 ** ADAM_STEP)
        v_hat = nv / (1.0 - ADAM_B2 ** ADAM_STEP)
        d_ref[...] = -ADAM_LR * (m_hat / (jnp.sqrt(v_hat) + ADAM_EPS) + ADAM_WD * w_ref[...])
        nm_ref[...] = nm
        nv_ref[...] = nv

    blk = pl.BlockSpec((tr, C), lambda i: (i, 0))
    sds = jax.ShapeDtypeStruct((R, C), F32)
    return pl.pallas_call(
        body, name=name, grid=(R // tr,), in_specs=[blk] * 4, out_specs=[blk] * 3, out_shape=[sds] * 3,
        compiler_params=_params("parallel"),
    )(w, g, m, v)


def sum_slots(x, name, out_dtype=F32):
    n, R, C = x.shape
    tr = _rows(R, 256)

    def body(x_ref, o_ref):
        acc = x_ref[0].astype(F32)
        for s in range(1, n):
            acc = acc + x_ref[s].astype(F32)
        o_ref[...] = acc.astype(out_dtype)

    return pl.pallas_call(
        body, name=name, grid=(R // tr,),
        in_specs=[pl.BlockSpec((n, tr, C), lambda i: (0, i, 0))],
        out_specs=pl.BlockSpec((tr, C), lambda i: (i, 0)),
        out_shape=jax.ShapeDtypeStruct((R, C), out_dtype),
        compiler_params=_params("parallel"),
    )(x)


_ANY = pl.BlockSpec(memory_space=pl.ANY)
GROUP_ALL = ([(0, 0, 1), (0, 1, 0), (0, 1, 1), (1, 0, 0), (1, 0, 1), (1, 1, 0), (1, 1, 1)],
             lambda d: 4 * d[0] + 2 * d[1] + d[2])
GROUP_CHIPS = ([(0, 1, 0), (1, 0, 0), (1, 1, 0)], lambda d: 2 * d[0] + d[1])
GROUP_SIBLING = ([(0, 0, 1)], lambda d: d[2])


def _me():
    return lax.axis_index("x"), lax.axis_index("y"), lax.axis_index("c")


def _peer(me, flip):
    return tuple(1 - a if f else a for a, f in zip(me, flip))


class Exchange:
    def __init__(self, x, group, scatter):
        self.flips, self.slot = group
        self.scatter = scatter
        self.n = len(self.flips) + 1
        self.out_shape = jax.ShapeDtypeStruct((self.n,) + x.shape[-2:], x.dtype)
        self.scratch = [pltpu.SemaphoreType.DMA((self.n - 1,)), pltpu.SemaphoreType.DMA((self.n - 1,)),
                        pltpu.SemaphoreType.DMA]

    def _copies(self, x_ref, o_ref, send_sems, recv_sems, local_sem, arrivals):
        me = _me()
        slot = self.slot
        mine = pltpu.make_async_copy(x_ref.at[slot(me)] if self.scatter else x_ref, o_ref.at[slot(me)], local_sem)
        sends, landed = [], []
        for k, flip in enumerate(self.flips):
            peer = _peer(me, flip)
            sends.append(pltpu.make_async_remote_copy(
                src_ref=x_ref.at[slot(peer)] if self.scatter else x_ref, dst_ref=o_ref.at[slot(me)],
                send_sem=send_sems.at[k], recv_sem=recv_sems.at[k], device_id=peer, device_id_type=MESH_ID))
            if arrivals:
                landed.append(pltpu.make_async_remote_copy(
                    src_ref=o_ref.at[slot(me)], dst_ref=o_ref.at[slot(peer)], send_sem=send_sems.at[k],
                    recv_sem=recv_sems.at[k], device_id=peer, device_id_type=MESH_ID))
        return mine, sends, landed

    def start(self, *refs):
        mine, sends, _ = self._copies(*refs, arrivals=False)
        mine.start()
        for cp in sends:
            cp.start()

    def wait(self, *refs):
        mine, sends, arrivals = self._copies(*refs, arrivals=True)
        for cp in arrivals:
            cp.wait_recv()
        for cp in sends:
            cp.wait_send()
        mine.wait()

    def __call__(self, x, name):
        def body(*refs):
            self.start(*refs)
            self.wait(*refs)

        return pl.pallas_call(body, name=name, in_specs=[_ANY], out_specs=_ANY, out_shape=self.out_shape,
                              scratch_shapes=self.scratch)(x)


def group_gather(x, name, group):
    return Exchange(x, group, scatter=False)(x, name)


def group_scatter(x, name, group):
    return Exchange(x, group, scatter=True)(x, name)


def _call(body, name, grid, in_specs, out_specs, out_shape, scratch, semantics, args, ride=None):
    if ride is None:
        return pl.pallas_call(body, name=name, grid=grid, in_specs=in_specs, out_specs=out_specs,
                              out_shape=out_shape, scratch_shapes=scratch,
                              compiler_params=_params(*semantics))(*args)
    x, exch = ride
    n_in, n_out, n_scr = len(in_specs), len(out_specs), len(scratch)

    def at_step(pick):
        hit = None
        for axis, size in enumerate(grid):
            here = pl.program_id(axis) == pick(size)
            hit = here if hit is None else jnp.logical_and(hit, here)
        return hit

    def riding(*refs):
        ins, x_ref = refs[:n_in], refs[n_in]
        outs, o_ref = refs[n_in + 1:n_in + 1 + n_out], refs[n_in + 1 + n_out]
        scr, sems = refs[n_in + 2 + n_out:n_in + 2 + n_out + n_scr], refs[n_in + 2 + n_out + n_scr:]

        @pl.when(at_step(lambda size: 0))
        def _():
            exch.start(x_ref, o_ref, *sems)

        body(*ins, *outs, *scr)

        @pl.when(at_step(lambda size: size - 1))
        def _():
            exch.wait(x_ref, o_ref, *sems)

    return pl.pallas_call(
        riding, name=name, grid=grid, in_specs=list(in_specs) + [_ANY], out_specs=list(out_specs) + [_ANY],
        out_shape=list(out_shape) + [exch.out_shape], scratch_shapes=list(scratch) + exch.scratch,
        compiler_params=_params(*(["arbitrary"] * len(grid))))(*args, x)


WEIGHTS = ['a_norm_g', 'a_w_in', 'a_b_if', 'a_hnorm_g', 'a_w_out', 'kv_norm_g', 'w_kv', 'b_norm_g', 'b_w_q',
           'b_w_out', 'rel_bias', 'f_norm_g', 'f_w_up', 'f_conv_w', 'f_conv_b', 'f_w_down', 'final_norm_g']
SHARD_AXIS = {'a_norm_g': 1, 'a_w_in': 2, 'a_b_if': None, 'a_hnorm_g': 2, 'a_w_out': 1, 'kv_norm_g': None,
              'w_kv': 1, 'b_norm_g': None, 'b_w_q': 2, 'b_w_out': 1, 'rel_bias': None, 'f_norm_g': None,
              'f_w_up': 2, 'f_conv_w': 2, 'f_conv_b': None, 'f_w_down': 1, 'final_norm_g': None}
BIG = ['a_w_in', 'a_w_out', 'w_kv', 'b_w_q', 'b_w_out', 'f_w_up', 'f_w_down']
SMALL = [n for n in WEIGHTS if n not in BIG]
LANES = 1024
PIECES = {'a_w_in': ('a_w_in', None, 2), 'a_w_out': ('a_w_out', None, 1), 'f_w_up0': ('f_w_up', 0, 1),
          'f_w_down0': ('f_w_down', 0, 0), 'w_kv': ('w_kv', None, 1), 'b_w_q': ('b_w_q', None, 2),
          'b_w_out': ('b_w_out', None, 1), 'f_w_up1': ('f_w_up', 1, 1), 'f_w_down1': ('f_w_down', 1, 0)}
LATE = ['w_kv', 'b_w_q', 'b_w_out', 'f_w_up1', 'f_w_down1']
WEIGHT_WAVES = {'first': ['a_w_in', 'a_w_out'], 'ffn0': ['f_w_up0', 'f_w_down0'], 'late': LATE}
GRAD_WAVES = {'late': LATE, 'layer0': ['f_w_up0', 'f_w_down0', 'a_w_out'], 'last': ['a_w_in']}


def _piece(arrays, p):
    leaf, layer, _ = PIECES[p]
    return arrays[leaf] if layer is None else arrays[leaf][layer]


class Packer:
    def __init__(self, pieces, shard):
        self.pieces = pieces
        self.shapes = [_piece(shard, p).shape for p in pieces]
        self.sizes = [math.prod(s) // (2 * LANES) for s in self.shapes]
        self.fill = -sum(self.sizes) % 16
        self.rows = sum(self.sizes) + self.fill

    def my_half(self, shard, half):
        both = jnp.concatenate([_piece(shard, p).astype(MM_DTYPE).reshape(2, -1, LANES) for p in self.pieces], axis=1)
        return jnp.pad(lax.dynamic_index_in_dim(both, half, axis=0, keepdims=False), ((0, self.fill), (0, 0)))

    def full_weights(self, gathered):
        g = gathered.reshape(4, 2, self.rows, LANES)
        out, off = {}, 0
        for p, shp, sz in zip(self.pieces, self.shapes, self.sizes):
            out[p] = _full_from_shards(g[:, :, off:off + sz].reshape((4,) + shp), PIECES[p][2])
            off += sz
        return out

    def grad_slots(self, grads):
        parts = [_shards_from_full(grads[p], PIECES[p][2]).reshape(4, 2, -1, LANES).astype(GRAD_WIRE_DTYPE)
                 for p in self.pieces]
        parts.append(jnp.zeros((4, 2, self.fill, LANES), GRAD_WIRE_DTYPE))
        return jnp.concatenate(parts, axis=2).reshape(8, self.rows, LANES)

    def shard_grads(self, both):
        out, off = {}, 0
        for p, shp, sz in zip(self.pieces, self.shapes, self.sizes):
            out[p] = both[:, off:off + sz].reshape(shp).astype(F32)
            off += sz
        return out


class Overlap:
    def __init__(self, shard, half):
        self.shard, self.half = shard, half
        self.weights = {w: Packer(p, shard) for w, p in WEIGHT_WAVES.items()}
        self.grads = {w: Packer(p, shard) for w, p in GRAD_WAVES.items()}
        self.shard_grads = {}

    def gather_ride(self, wave):
        mine = self.weights[wave].my_half(self.shard, self.half)
        return mine, Exchange(mine, GROUP_ALL, scatter=False)

    def gathered(self, wave, slots):
        return self.weights[wave].full_weights(slots)

    def scatter_ride(self, wave, grads):
        slots = self.grads[wave].grad_slots({p: grads.pop(p) for p in GRAD_WAVES[wave]})
        return slots, Exchange(slots, GROUP_ALL, scatter=True)

    def join_ride(self, wave, received):
        reduced = sum_slots(received, f"sum_grads_{wave}", GRAD_WIRE_DTYPE)
        return reduced, Exchange(reduced, GROUP_SIBLING, scatter=False)

    def joined(self, wave, both):
        self.shard_grads.update(self.grads[wave].shard_grads(both))


def _pad_rows(flat, mult):
    n = flat.shape[0]
    per = LANES * mult
    tot = -(-n // per) * per
    return jnp.pad(flat, (0, tot - n)).reshape(tot // LANES, LANES)


def _full_from_shards(sh, axis):
    shp = sh.shape[1:]
    return jnp.moveaxis(sh, 0, axis).reshape(shp[:axis] + (4 * shp[axis],) + shp[axis + 1:])


def _shards_from_full(full, axis):
    shp = full.shape
    return jnp.moveaxis(full.reshape(shp[:axis] + (4, shp[axis] // 4) + shp[axis + 1:]), axis, 0)


def _local_step(x, target, W, overlap=None):
    T = x.shape[0]
    W = dict(W)
    row = lambda a: a.reshape(1, -1).astype(F32)
    w_in = jnp.pad(W['a_w_in'][0], ((0, 0), (0, A_IN_PAD - A_IN)))
    bias128 = jnp.pad(row(W['a_b_if'][0]), ((0, 0), (0, 120)))
    hng = row(W['a_hnorm_g'][0])
    w_up = lambda l: _interleave(W[f'f_w_up{l}'])
    cw = [_interleave(W['f_conv_w'][l].astype(F32)) for l in range(2)]
    cb = [_interleave(row(W['f_conv_b'][l])) for l in range(2)]
    onehots = [(jnp.asarray(_group_bucket(g).reshape(-1, 1)) == jnp.arange(128)[None, :]).astype(F32)
               for g in range(N_GROUPS)]
    rb_t = jnp.pad(W['rel_bias'].astype(F32).T, ((0, 0), (0, 128 - REL_BUCKETS)))
    biases = [mm_nn(rb_t[g * B_HEADS:(g + 1) * B_HEADS], onehots[g].T, f"rel_bias_table_g{g}", exact=True)
              .reshape(B_HEADS, B_BLOCK, 2 * B_BLOCK) for g in range(N_GROUPS)]
    G = {}

    def ffn_fwd(xin, l, ride=None):
        xn, = rms_fwd(xin, [row(W['f_norm_g'][l])], f"ffn{l}_norm")
        u, act, *rode = ffn_up_act(xn, w_up(l), cw[l], cb[l], f"ffn{l}_up_act", ride)
        return mm_nn(act, W[f'f_w_down{l}'], f"ffn{l}_down", res=xin), (xn, u, act), rode

    def ffn_bwd(xin, saved, dout, l, ride=None):
        xn, u, act = saved
        dact = mm_nn(dout, W[f'f_w_down{l}'].T, f"ffn{l}_ddown")
        G[f'f_w_down{l}'] = mm_tn(act, dout, f"ffn{l}_gdown")
        du, gcw, gcb, *rode = conv_act_bwd(u, dact, cw[l], cb[l], f"ffn{l}_dact", ride)
        dxn = mm_nn(du, w_up(l).T, f"ffn{l}_dup")
        G[f'f_w_up{l}'] = _deinterleave(mm_tn(xn, du, f"ffn{l}_gup"))
        dxin, (gn,) = rms_bwd(xin, dout, [(dxn, row(W['f_norm_g'][l]))], f"ffn{l}_dnorm")
        return dxin, _deinterleave(gcw), _deinterleave(gcb), gn, rode

    xn_a, = rms_fwd(x, [row(W['a_norm_g'][0])], "a_norm")
    z = mm_nn(xn_a, w_in, "a_in")
    gcol, grow = gate_prep(z, bias128)
    hg, Cs, ns, ms, *rode = mlstm_fwd(z, gcol, grow, hng, overlap.gather_ride('ffn0') if overlap else None)
    if overlap:
        W.update(overlap.gathered('ffn0', rode[0]))
    x1 = mm_nn(hg, W['a_w_out'][0], "a_out", res=x)
    x2, ffn0, rode = ffn_fwd(x1, 0, overlap.gather_ride('late') if overlap else None)
    if overlap:
        W.update(overlap.gathered('late', rode[0]))
    xn_kv, xn_b = rms_fwd(x2, [row(W['kv_norm_g']), row(W['b_norm_g'][0])], "b_norms")
    gcols = lambda w, c: w[:, c * 1024:(c + 1) * 1024]
    qv = [mm_view(xn_b, gcols(W['b_w_q'][0], g), f"q_proj_g{g}", DILATIONS[g]) for g in range(N_GROUPS)]
    kvw = [mm_view(xn_kv, gcols(W['w_kv'], g), f"k_proj_g{g}", DILATIONS[g]) for g in range(N_GROUPS)]
    vvw = [mm_view(xn_kv, gcols(W['w_kv'], 3 + g), f"v_proj_g{g}", DILATIONS[g]) for g in range(N_GROUPS)]
    os_, lses = zip(*[attn_fwd(qv[g], kvw[g], vvw[g], biases[g], g) for g in range(N_GROUPS)])
    att, att_f, *lse_v = attn_merge(os_, lses)
    x3 = mm_nn(att, W['b_w_out'][0], "b_out", res=x2)
    x4, ffn1, _ = ffn_fwd(x3, 1)
    dx4, g_final, loss = loss_head(x4, target, row(W['final_norm_g']))
    G['final_norm_g'] = g_final.reshape(-1)

    dx3, gcw1, gcb1, gn1, _ = ffn_bwd(x3, ffn1, dx4, 1)
    datt = mm_nn(dx3, W['b_w_out'][0].T, "b_dout")
    G['b_w_out'] = mm_tn(att, dx3, "b_gout")[None]
    prep = attn_prep(datt, att_f)
    do_v, dl_v = prep[:3], prep[3:]
    parts = [attn_bwd(qv[g], kvw[g], vvw[g], biases[g], do_v[g], lse_v[g], dl_v[g], g) for g in range(N_GROUPS)]
    dq_all, dkv = attn_combine(parts)
    grb = []
    for g in range(N_GROUPS):
        gb = mm_nn(parts[g][5].reshape(B_HEADS, -1), onehots[g], f"rel_bias_g{g}", exact=True)
        grb.append(gb[:, :REL_BUCKETS].T)
    G['rel_bias'] = jnp.concatenate(grb, axis=1)
    dxn_b = mm_nn(dq_all, W['b_w_q'][0].T, "q_dproj")
    G['b_w_q'] = mm_tn(xn_b, dq_all, "q_gproj")[None]
    dxn_kv = mm_nn(dkv, W['w_kv'].T, "kv_dproj")
    G['w_kv'] = mm_tn(xn_kv, dkv, "kv_gproj")
    dx2, (g_kvn, g_bn) = rms_bwd(x2, dx3, [(dxn_kv, row(W['kv_norm_g'])), (dxn_b, row(W['b_norm_g'][0]))],
                                 "b_dnorms")
    G['kv_norm_g'] = g_kvn.reshape(-1)
    G['b_norm_g'] = g_bn
    dx1, gcw0, gcb0, gn0, late_slots = ffn_bwd(x1, ffn0, dx2, 0, overlap.scatter_ride('late', G) if overlap else None)
    G['f_conv_w'] = jnp.stack([gcw0, gcw1])
    G['f_conv_b'] = jnp.concatenate([gcb0, gcb1], axis=0)
    G['f_norm_g'] = jnp.concatenate([gn0, gn1], axis=0)
    dhg = mm_nn(dx1, W['a_w_out'][0].T, "a_dout")
    G['a_w_out'] = mm_tn(hg, dx1, "a_gout")[None]
    dz, g_hn, g_bif, *layer0_slots = mlstm_bwd(z, gcol, grow, hng, bias128, Cs, ns, ms, dhg,
                                               overlap.scatter_ride('layer0', G) if overlap else None)
    G['a_hnorm_g'] = g_hn.reshape(1, A_HEADS, A_V)
    G['a_b_if'] = g_bif[:, :2 * A_HEADS]
    if overlap:
        dxn_a, both = mm_nn(dz, w_in.T, "a_din", ride=overlap.join_ride('late', late_slots[0]))
        overlap.joined('late', both)
        g_in, both = mm_tn(xn_a, dz, "a_gin", ride=overlap.join_ride('layer0', layer0_slots[0]))
        overlap.joined('layer0', both)
    else:
        dxn_a = mm_nn(dz, w_in.T, "a_din")
        g_in = mm_tn(xn_a, dz, "a_gin")
    G['a_w_in'] = g_in[:, :A_IN][None]
    grad_x, (g_an,) = rms_bwd(x, dx1, [(dxn_a, row(W['a_norm_g'][0]))], "a_dnorm")
    G['a_norm_g'] = g_an
    return loss, grad_x, G


def kernel(x, a_norm_g, a_w_in, a_b_if, a_hnorm_g, a_w_out, kv_norm_g, w_kv, b_norm_g, b_w_q, b_w_out, rel_bias, f_norm_g, f_w_up, f_conv_w, f_conv_b, f_w_down, final_norm_g, loss_target, m_a_norm_g, m_a_w_in, m_a_b_if, m_a_hnorm_g, m_a_w_out, m_kv_norm_g, m_w_kv, m_b_norm_g, m_b_w_q, m_b_w_out, m_rel_bias, m_f_norm_g, m_f_w_up, m_f_conv_w, m_f_conv_b, m_f_w_down, m_final_norm_g, v_a_norm_g, v_a_w_in, v_a_b_if, v_a_hnorm_g, v_a_w_out, v_kv_norm_g, v_w_kv, v_b_norm_g, v_b_w_q, v_b_w_out, v_rel_bias, v_f_norm_g, v_f_w_up, v_f_conv_w, v_f_conv_b, v_f_w_down, v_final_norm_g):
    given = dict(locals())
    shard = {n: given[n] for n in WEIGHTS}
    mom = {n: given["m_" + n] for n in WEIGHTS}
    var = {n: given["v_" + n] for n in WEIGHTS}
    cx, cy, cc = _me()
    chip = 2 * cx + cy

    overlap = Overlap(shard, cc)
    mine, gather = overlap.gather_ride('first')
    W = overlap.gathered('first', gather(mine, "gather_weights"))
    sharded_small = [n for n in SMALL if SHARD_AXIS[n] is not None]
    ssz = [shard[n].size for n in sharded_small]
    sflat = jnp.concatenate([shard[n].reshape(-1) for n in sharded_small])
    sg = group_gather(_pad_rows(sflat, 8), "gather_small", GROUP_CHIPS).reshape(4, -1)
    off = 0
    for n, sz in zip(sharded_small, ssz):
        W[n] = _full_from_shards(sg[:, off:off + sz].reshape((4,) + shard[n].shape), SHARD_AXIS[n])
        off += sz
    for n in SMALL:
        if SHARD_AXIS[n] is None:
            W[n] = shard[n]

    loss_row, grad_x, G = _local_step(x[0], loss_target[0], W, overlap)

    slots, scatter = overlap.scatter_ride('last', G)
    reduced, join = overlap.join_ride('last', scatter(slots, "scatter_grads"))
    overlap.joined('last', join(reduced, "join_halves"))
    by_piece = overlap.shard_grads
    gsh = {}
    for n in BIG:
        layers = [p for p in PIECES if PIECES[p][0] == n]
        gsh[n] = by_piece[n] if layers == [n] else jnp.stack([by_piece[p] for p in layers])
    small_parts = [loss_row[0, 0:1]] + [G[n].reshape(-1) for n in SMALL]
    small_sz = [p.shape[0] for p in small_parts]
    small = sum_slots(group_gather(_pad_rows(jnp.concatenate(small_parts), 8), "gather_small_grads", GROUP_ALL),
                      "sum_small_grads").reshape(-1)
    loss = small[0]
    off = 1
    for n, sz in zip(SMALL, small_sz[1:]):
        full = small[off:off + sz].reshape(W[n].shape)
        off += sz
        if SHARD_AXIS[n] is None:
            gsh[n] = full
        else:
            gsh[n] = lax.dynamic_index_in_dim(_shards_from_full(full, SHARD_AXIS[n]), chip, 0, keepdims=False)

    delta, new_m, new_v = {}, {}, {}
    for n in WEIGHTS:
        shp = shard[n].shape
        two = lambda a: a.reshape(-1, shp[-1])
        d, nm, nv = adamw(two(shard[n]), two(gsh[n]), two(mom[n]), two(var[n]), f"adamw_{n}")
        delta[n], new_m[n], new_v[n] = d.reshape(shp), nm.reshape(shp), nv.reshape(shp)
    return (loss, grad_x[None], *[gsh[n] for n in WEIGHTS], *[delta[n] for n in WEIGHTS],
            *[new_m[n] for n in WEIGHTS], *[new_v[n] for n in WEIGHTS])
```

```python
import functools
import math

import numpy as np
import jax
import jax.numpy as jnp
from jax import lax
from jax.experimental import pallas as pl
from jax.experimental.pallas import tpu as pltpu

F32 = jnp.float32
BF16 = jnp.bfloat16
MM_DTYPE = jnp.bfloat16
GRAD_WIRE_DTYPE = jnp.bfloat16
HI = lax.Precision.HIGHEST

D_MODEL = 1024
A_HEADS = 4
A_QK = 128
A_V = 256
A_CHUNK = 256
A_IN = 3080
A_IN_PAD = 3200
GATE_COL = 3072
SOFTCAP = 15.0
N_GROUPS = 3
B_HEADS = 16
B_DH = 64
B_BLOCK = 128
DILATIONS = (1, 4, 16)
WINDOWS = (128, 512, 2048)
REL_BUCKETS = 32
REL_MAX_DIST = 2048
D_FF = 2816
FF_TC = 256
EPS = 1e-6
ADAM_LR, ADAM_B1, ADAM_B2, ADAM_EPS, ADAM_WD, ADAM_STEP = 0.001, 0.9, 0.999, 1e-08, 0.01, 10

VMEM_LIMIT = 56 * 1024 * 1024
NT_DIMS = (((1,), (1,)), ((), ()))
TN_DIMS = (((0,), (0,)), ((), ()))
MESH_ID = pl.DeviceIdType.MESH


def _params(*sem):
    return pltpu.CompilerParams(dimension_semantics=sem, vmem_limit_bytes=VMEM_LIMIT)


def _tile(n, cap):
    if n <= cap:
        return n
    best = None
    for t in range(128, cap + 1, 128):
        if n % t == 0:
            best = t
    assert best is not None, (n, cap)
    return best


def _rows(n, cap):
    if n <= cap:
        return n
    for t in range(cap // 8 * 8, 7, -8):
        if n % t == 0:
            return t
    raise ValueError((n, cap))


def _dot(a, b):
    return jnp.dot(a.astype(MM_DTYPE), b.astype(MM_DTYPE), preferred_element_type=F32)


def _dot_nt(a, b):
    return lax.dot_general(a.astype(MM_DTYPE), b.astype(MM_DTYPE), NT_DIMS, preferred_element_type=F32)


def _dot_tn(a, b):
    return lax.dot_general(a.astype(MM_DTYPE), b.astype(MM_DTYPE), TN_DIMS, preferred_element_type=F32)


def _sigmoid(x):
    return 1.0 / (1.0 + jnp.exp(-x))


def _sigmoid_tanh(x):
    return 0.5 * jnp.tanh(0.5 * x) + 0.5


def mm_nn(a, b, name, res=None, out_dtype=F32, exact=False, ride=None):
    M, K = a.shape
    N = b.shape[1]
    def footprint(tm, tn):
        return 2 * (tm * K * a.dtype.itemsize + K * tn * b.dtype.itemsize) + 2 * tm * tn * 4 * (1 if res is None else 2)

    budget = 46 * 1024 * 1024
    tm = _rows(M, 512)
    tn = N if N <= 3328 and footprint(tm, N) <= budget else _tile(N, 1536)
    tk = K if footprint(tm, tn) <= budget else _tile(K, 1536)
    if tk == K and footprint(_rows(M, 1024), tn) <= budget:
        tm = _rows(M, 1024)
    nk = K // tk

    def body(*refs):
        if res is None:
            a_ref, b_ref, o_ref, acc = refs
            r_ref = None
        else:
            a_ref, b_ref, r_ref, o_ref, acc = refs
        if exact:
            p = jnp.dot(a_ref[...], b_ref[...], precision=HI, preferred_element_type=F32)
        else:
            p = _dot(a_ref[...], b_ref[...])

        def finish(total):
            if r_ref is not None:
                total = total + r_ref[...]
            o_ref[...] = total.astype(out_dtype)

        if nk == 1:
            finish(p)
        else:
            k = pl.program_id(2)

            @pl.when(k == 0)
            def _():
                acc[...] = p

            @pl.when(jnp.logical_and(k > 0, k < nk - 1))
            def _():
                acc[...] += p

            @pl.when(k == nk - 1)
            def _():
                finish(acc[...] + p)

    in_specs = [pl.BlockSpec((tm, tk), lambda j, i, k: (i, k)),
                pl.BlockSpec((tk, tn), lambda j, i, k: (k, j))]
    args = [a, b]
    if res is not None:
        in_specs.append(pl.BlockSpec((tm, tn), lambda j, i, k: (i, j)))
        args.append(res)
    acc_shape = (tm, tn) if nk > 1 else (8, 128)
    outs = _call(body, name, (N // tn, M // tm, nk), in_specs, [pl.BlockSpec((tm, tn), lambda j, i, k: (i, j))],
                 [jax.ShapeDtypeStruct((M, N), out_dtype)], [pltpu.VMEM(acc_shape, F32)],
                 ("parallel", "parallel", "arbitrary"), args, ride)
    return outs[0] if ride is None else outs


def mm_view(a, b, name, dil):
    T, K = a.shape
    tm = 1024

    def body(a_ref, b_ref, o_ref, sc):
        p = _dot(a_ref[...], b_ref[...])
        if dil == 1:
            o_ref[...] = p.astype(o_ref.dtype)
        else:
            _to_view(lambda c: p[:, c * 128:(c + 1) * 128], sc, o_ref, dil, 8, tm)

    return pl.pallas_call(
        body, name=name, grid=(T // tm,),
        in_specs=[pl.BlockSpec((tm, K), lambda i: (i, 0)), pl.BlockSpec((K, 1024), lambda i: (0, 0))],
        out_specs=pl.BlockSpec((tm // dil, dil * 1024), lambda i: (i, 0)),
        out_shape=jax.ShapeDtypeStruct((T // dil, dil * 1024), MM_DTYPE),
        scratch_shapes=[pltpu.VMEM((8, tm, 128), F32)],
        compiler_params=_params("parallel"),
    )(a, b)


def mm_tn(a, g, name, ride=None):
    T, Ka = a.shape
    N = g.shape[1]
    tka, tt = _tile(Ka, 1536), _rows(T, 1024)

    def footprint(tt, tn):
        return 2 * (tt * tka * a.dtype.itemsize + tt * tn * g.dtype.itemsize + tka * tn * 4)

    budget = 46 * 1024 * 1024
    tn = N if N <= 3328 and footprint(tt, N) <= budget else _tile(N, 1536)
    if footprint(_rows(T, 2048), tn) <= budget:
        tt = _rows(T, 2048)
    nt = T // tt

    def body(a_ref, g_ref, o_ref):
        t = pl.program_id(2)
        p = _dot_tn(a_ref[...], g_ref[...])

        @pl.when(t == 0)
        def _():
            o_ref[...] = p

        @pl.when(t > 0)
        def _():
            o_ref[...] += p

    outs = _call(body, name, (Ka // tka, N // tn, nt),
                 [pl.BlockSpec((tt, tka), lambda i, j, t: (t, i)), pl.BlockSpec((tt, tn), lambda i, j, t: (t, j))],
                 [pl.BlockSpec((tka, tn), lambda i, j, t: (i, j))], [jax.ShapeDtypeStruct((Ka, N), F32)], [],
                 ("parallel", "parallel", "arbitrary"), (a, g), ride)
    return outs[0] if ride is None else outs


def rms_fwd(x, gains, name):
    T, D = x.shape
    tt = _rows(T, 1024)
    ng = len(gains)

    def body(*refs):
        x_ref = refs[0]
        g_refs = refs[1:1 + ng]
        o_refs = refs[1 + ng:]
        xf = x_ref[...]
        y = xf * lax.rsqrt(jnp.mean(xf * xf, axis=-1, keepdims=True) + EPS)
        for g_ref, o_ref in zip(g_refs, o_refs):
            o_ref[...] = (y * g_ref[...]).astype(o_ref.dtype)

    row = pl.BlockSpec((tt, D), lambda i: (i, 0))
    gsp = pl.BlockSpec((1, D), lambda i: (0, 0))
    return pl.pallas_call(
        body, name=name, grid=(T // tt,),
        in_specs=[row] + [gsp] * ng, out_specs=[row] * ng,
        out_shape=[jax.ShapeDtypeStruct((T, D), MM_DTYPE)] * ng,
        compiler_params=_params("parallel"),
    )(x, *gains)


def rms_bwd(x, dres, branches, name):
    T, D = x.shape
    tt = _rows(T, 512)
    nb = len(branches)

    def body(*refs):
        x_ref, r_ref = refs[0], refs[1]
        dy_refs = refs[2:2 + nb]
        g_refs = refs[2 + nb:2 + 2 * nb]
        dx_ref = refs[2 + 2 * nb]
        dg_refs = refs[3 + 2 * nb:]
        i = pl.program_id(0)
        xf = x_ref[...]
        r = lax.rsqrt(jnp.mean(xf * xf, axis=-1, keepdims=True) + EPS)
        xh = xf * r
        dx = r_ref[...]
        for dy_ref, g_ref, dg_ref in zip(dy_refs, g_refs, dg_refs):
            dy = dy_ref[...].astype(F32)
            dyg = dy * g_ref[...]
            dx = dx + r * (dyg - xh * jnp.mean(dyg * xh, axis=-1, keepdims=True))
            part = jnp.sum(dy * xh, axis=0, keepdims=True)

            @pl.when(i == 0)
            def _():
                dg_ref[...] = part

            @pl.when(i > 0)
            def _():
                dg_ref[...] += part
        dx_ref[...] = dx

    row = pl.BlockSpec((tt, D), lambda i: (i, 0))
    gsp = pl.BlockSpec((1, D), lambda i: (0, 0))
    outs = pl.pallas_call(
        body, name=name, grid=(T // tt,),
        in_specs=[row, row] + [row] * nb + [gsp] * nb,
        out_specs=[row] + [gsp] * nb,
        out_shape=[jax.ShapeDtypeStruct((T, D), F32)] + [jax.ShapeDtypeStruct((1, D), F32)] * nb,
        compiler_params=_params("arbitrary"),
    )(x, dres, *[b[0] for b in branches], *[b[1] for b in branches])
    return outs[0], outs[1:]


def loss_head(x, target, gain):
    T, D = x.shape
    tt = _rows(T, 512)

    def body(x_ref, t_ref, g_ref, dx_ref, dg_ref, loss_ref):
        i = pl.program_id(0)
        xf = x_ref[...]
        g = g_ref[...]
        r = lax.rsqrt(jnp.mean(xf * xf, axis=-1, keepdims=True) + EPS)
        xh = xf * r
        e = xh * g - t_ref[...]
        lpart = 0.5 * jnp.sum(jnp.sum(e * e, axis=1, keepdims=True), axis=0, keepdims=True) / D
        dy = e / D
        dyg = dy * g
        dx_ref[...] = r * (dyg - xh * jnp.mean(dyg * xh, axis=-1, keepdims=True))
        gpart = jnp.sum(dy * xh, axis=0, keepdims=True)
        lrow = jnp.broadcast_to(lpart, (1, 128))

        @pl.when(i == 0)
        def _():
            dg_ref[...] = gpart
            loss_ref[...] = lrow

        @pl.when(i > 0)
        def _():
            dg_ref[...] += gpart
            loss_ref[...] += lrow

    row = pl.BlockSpec((tt, D), lambda i: (i, 0))
    gsp = pl.BlockSpec((1, D), lambda i: (0, 0))
    return pl.pallas_call(
        body, name="loss_head", grid=(T // tt,),
        in_specs=[row, row, gsp],
        out_specs=[row, gsp, pl.BlockSpec((1, 128), lambda i: (0, 0))],
        out_shape=[jax.ShapeDtypeStruct((T, D), F32), jax.ShapeDtypeStruct((1, D), F32),
                   jax.ShapeDtypeStruct((1, 128), F32)],
        compiler_params=_params("arbitrary"),
    )(x, target, gain)


def _shift_down(u, prev8, first, k):
    rolled = pltpu.roll(u, k, 0)
    rid = lax.broadcasted_iota(jnp.int32, u.shape, 0)
    halo = jnp.where(first, 0.0, prev8)
    out = rolled
    for j in range(k):
        out = jnp.where(rid == j, halo[8 - k + j:8 - k + j + 1, :], out)
    return out


def _conv3(u, prev8, first, w, b):
    return (_shift_down(u, prev8, first, 2) * w[0:1, :] + _shift_down(u, prev8, first, 1) * w[1:2, :]
            + u * w[2:3, :] + b)


def ffn_up_act(xn, w_up, w, b, name, ride=None):
    T, K = xn.shape
    tt = _rows(T, 2048)
    nj = D_FF // FF_TC

    def body(x_ref, wu_ref, w_ref, b_ref, u_ref, o_ref, tail):
        first = pl.program_id(1) == 0
        u = _dot(x_ref[...], wu_ref[...])
        u_ref[...] = u
        c = _conv3(u, tail[...], first, w_ref[...], b_ref[...])
        tail[...] = u[tt - 8:, :]
        cg, cv = c[:, :FF_TC], c[:, FF_TC:]
        o_ref[...] = (cg * _sigmoid_tanh(cg) * cv).astype(o_ref.dtype)

    return _call(
        body, name, (nj, T // tt),
        [pl.BlockSpec((tt, K), lambda j, i: (i, 0)),
         pl.BlockSpec((K, 2 * FF_TC), lambda j, i: (0, j)),
         pl.BlockSpec((3, 2 * FF_TC), lambda j, i: (0, j)),
         pl.BlockSpec((1, 2 * FF_TC), lambda j, i: (0, j))],
        [pl.BlockSpec((tt, 2 * FF_TC), lambda j, i: (i, j)), pl.BlockSpec((tt, FF_TC), lambda j, i: (i, j))],
        [jax.ShapeDtypeStruct((T, 2 * D_FF), F32), jax.ShapeDtypeStruct((T, D_FF), MM_DTYPE)],
        [pltpu.VMEM((8, 2 * FF_TC), F32)], ("parallel", "arbitrary"), (xn, w_up, w, b), ride)


def conv_act_bwd(u, da, w, b, name, ride=None):
    T = u.shape[0]
    tt = _rows(T, 2048)
    nt = T // tt
    nj = D_FF // FF_TC
    te = tt + 8

    def body(u_ref, p_ref, n_ref, da_ref, dan_ref, w_ref, b_ref, du_ref, dw_ref, db_ref):
        i = pl.program_id(1)
        first = i == 0
        last = i == nt - 1
        w = w_ref[...]
        ue = jnp.concatenate([u_ref[...], n_ref[...]], axis=0)
        dae = jnp.concatenate([da_ref[...], jnp.where(last, 0.0, dan_ref[...])], axis=0)
        um2 = _shift_down(ue, p_ref[...], first, 2)
        um1 = _shift_down(ue, p_ref[...], first, 1)
        c = um2 * w[0:1, :] + um1 * w[1:2, :] + ue * w[2:3, :] + b_ref[...]
        cg, cv = c[:, :FF_TC], c[:, FF_TC:]
        s = _sigmoid_tanh(cg)
        dcg = dae * cv * (s * (1.0 + cg * (1.0 - s)))
        dcv = dae * (cg * s)
        dc = jnp.concatenate([dcg, dcv], axis=1)
        du = (dc * w[2:3, :] + pltpu.roll(dc, te - 1, 0) * w[1:2, :] + pltpu.roll(dc, te - 2, 0) * w[0:1, :])
        du_ref[...] = du[:tt, :].astype(du_ref.dtype)
        dcm = dc[:tt, :]
        dwp = jnp.concatenate([jnp.sum(dcm * um2[:tt, :], axis=0, keepdims=True),
                               jnp.sum(dcm * um1[:tt, :], axis=0, keepdims=True),
                               jnp.sum(dcm * ue[:tt, :], axis=0, keepdims=True)], axis=0)
        dbp = jnp.sum(dcm, axis=0, keepdims=True)

        @pl.when(first)
        def _():
            dw_ref[...] = dwp
            db_ref[...] = dbp

        @pl.when(i > 0)
        def _():
            dw_ref[...] += dwp
            db_ref[...] += dbp

    nb8 = T // 8
    return _call(
        body, name, (nj, nt),
        [pl.BlockSpec((tt, 2 * FF_TC), lambda j, i: (i, j)),
         pl.BlockSpec((8, 2 * FF_TC), lambda j, i: (jnp.maximum(i * (tt // 8) - 1, 0), j)),
         pl.BlockSpec((8, 2 * FF_TC), lambda j, i: (jnp.minimum((i + 1) * (tt // 8), nb8 - 1), j)),
         pl.BlockSpec((tt, FF_TC), lambda j, i: (i, j)),
         pl.BlockSpec((8, FF_TC), lambda j, i: (jnp.minimum((i + 1) * (tt // 8), nb8 - 1), j)),
         pl.BlockSpec((3, 2 * FF_TC), lambda j, i: (0, j)),
         pl.BlockSpec((1, 2 * FF_TC), lambda j, i: (0, j))],
        [pl.BlockSpec((tt, 2 * FF_TC), lambda j, i: (i, j)),
         pl.BlockSpec((3, 2 * FF_TC), lambda j, i: (0, j)),
         pl.BlockSpec((1, 2 * FF_TC), lambda j, i: (0, j))],
        [jax.ShapeDtypeStruct((T, 2 * D_FF), MM_DTYPE), jax.ShapeDtypeStruct((3, 2 * D_FF), F32),
         jax.ShapeDtypeStruct((1, 2 * D_FF), F32)],
        [], ("parallel", "arbitrary"), (u, u, u, da, da, w, b), ride)


def _interleave(a):
    lead = a.shape[:-1]
    nj = D_FF // FF_TC
    return jnp.swapaxes(a.reshape(*lead, 2, nj, FF_TC), -3, -2).reshape(*lead, 2 * D_FF)


def _deinterleave(a):
    lead = a.shape[:-1]
    nj = D_FF // FF_TC
    return jnp.swapaxes(a.reshape(*lead, nj, 2, FF_TC), -3, -2).reshape(*lead, 2 * D_FF)


A_GC = 1
A_TB = A_GC * A_CHUNK


def gate_prep(z, bias128):
    T = z.shape[0]
    tt = _rows(T, 512)

    def body(z_ref, b_ref, gc_ref, gr_ref):
        pre = z_ref[...] + b_ref[...]
        sc = SOFTCAP * jnp.tanh(pre / SOFTCAP)
        lf = jnp.minimum(sc, 0.0) - jnp.log(1.0 + jnp.exp(-jnp.abs(sc)))
        col = lax.broadcasted_iota(jnp.int32, pre.shape, 1)
        isf = jnp.logical_and(col >= A_HEADS, col < 2 * A_HEADS)
        r = lax.broadcasted_iota(jnp.int32, (tt, tt), 0)
        c = lax.broadcasted_iota(jnp.int32, (tt, tt), 1)
        bits = A_CHUNK.bit_length() - 1
        tri = jnp.logical_and(jnp.right_shift(r, bits) == jnp.right_shift(c, bits), c <= r).astype(F32)
        bcum = jnp.dot(tri, jnp.where(isf, lf, 0.0), precision=HI, preferred_element_type=F32)
        g = jnp.where(col < A_HEADS, sc, jnp.where(isf, bcum, 0.0))
        gc_ref[...] = g
        for s in range(tt // 128):
            gr_ref[s] = g[s * 128:(s + 1) * 128, :].T[0:8, :]

    return pl.pallas_call(
        body, name="gate_prep", grid=(T // tt,),
        in_specs=[pl.BlockSpec((tt, 128), lambda i: (i, GATE_COL // 128)),
                  pl.BlockSpec((1, 128), lambda i: (0, 0))],
        out_specs=[pl.BlockSpec((tt, 128), lambda i: (i, 0)),
                   pl.BlockSpec((tt // 128, 8, 128), lambda i: (i, 0, 0))],
        out_shape=[jax.ShapeDtypeStruct((T, 128), F32), jax.ShapeDtypeStruct((T // 128, 8, 128), F32)],
        compiler_params=_params("parallel"),
    )(z, bias128)


def _chunk_decay(A, qh, bc, br, lir, n, m, causal):
    logD = jnp.where(causal, bc - br + lir, -jnp.inf)
    m_inter = bc + m
    m_t = jnp.maximum(m_inter, jnp.max(logD, axis=1, keepdims=True))
    E = jnp.exp(logD - m_t)
    Sm = A * E
    wi = jnp.exp(m_inter - m_t)
    qn = jnp.sum(qh.astype(F32) * n, axis=1, keepdims=True)
    den = jnp.sum(Sm, axis=1, keepdims=True) + wi * qn
    gs = jnp.maximum(jnp.abs(den), jnp.exp(-m_t))
    return E, Sm, wi, den, gs, m_t


def _state_weights(bc, lic, br, lir, m):
    bL = bc[A_CHUNK - 1:A_CHUNK, :]
    m_new = jnp.maximum(bL + m, jnp.max(bL - br + lir, axis=1, keepdims=True))
    wk = jnp.exp(bL - bc + lic - m_new)
    decay = jnp.exp(bL + m - m_new)
    return wk, decay, m_new


def _head_slices(h):
    return (slice(h * A_QK, (h + 1) * A_QK), slice(h * A_V, (h + 1) * A_V))


def mlstm_fwd(z, gcol, grow, hng, ride=None):
    T = z.shape[0]
    NC = T // A_CHUNK
    scale = A_QK ** -0.5

    def body(q_ref, k_ref, v_ref, o_ref, gc_ref, gr_ref, hng_ref, hg_ref, Cs_ref, ns_ref, ms_ref,
             C_sc, n_sc, m_sc):
        @pl.when(pl.program_id(0) == 0)
        def _():
            C_sc[...] = jnp.zeros_like(C_sc)
            n_sc[...] = jnp.zeros_like(n_sc)
            m_sc[...] = jnp.zeros_like(m_sc)

        ri = lax.broadcasted_iota(jnp.int32, (A_CHUNK, A_CHUNK), 0)
        ci = lax.broadcasted_iota(jnp.int32, (A_CHUNK, A_CHUNK), 1)
        causal = ri >= ci
        gr = jnp.concatenate([gr_ref[s] for s in range(A_TB // 128)], axis=1)
        for c in range(A_GC):
            rows = slice(c * A_CHUNK, (c + 1) * A_CHUNK)
            gc = gc_ref[rows, :]
            grc = gr[:, c * A_CHUNK:(c + 1) * A_CHUNK]
            for h in range(A_HEADS):
                sk, sv = _head_slices(h)
                qh = (q_ref[rows, sk] * scale).astype(MM_DTYPE)
                kh = k_ref[rows, sk].astype(MM_DTYPE)
                vh = v_ref[rows, sv].astype(MM_DTYPE)
                lic, bc = gc[:, h:h + 1], gc[:, A_HEADS + h:A_HEADS + h + 1]
                lir, br = grc[h:h + 1, :], grc[A_HEADS + h:A_HEADS + h + 1, :]
                C, n, m = C_sc[h], n_sc[h], m_sc[h][:, 0:1]
                Cs_ref[c, h] = C
                ns_ref[c, h] = n
                ms_ref[c, h] = m_sc[h]
                _, Sm, wi, _, gs, _ = _chunk_decay(_dot_nt(qh, kh), qh, bc, br, lir, n, m, causal)
                hh = (_dot(Sm, vh) + wi * _dot(qh, C)) / gs
                hn = hh * lax.rsqrt(jnp.mean(hh * hh, axis=1, keepdims=True) + EPS) * hng_ref[:, sv]
                hg_ref[rows, sv] = (hn * _sigmoid(o_ref[rows, sv])).astype(hg_ref.dtype)
                wk, decay, m_new = _state_weights(bc, lic, br, lir, m)
                kw = kh.astype(F32) * wk
                C_sc[h] = decay * C + _dot_tn(kw, vh)
                n_sc[h] = decay * n + jnp.sum(kw, axis=0, keepdims=True)
                m_sc[h] = jnp.broadcast_to(m_new, (1, 128))

    tok = lambda w, cb: pl.BlockSpec((A_TB, w), lambda i: (i, cb))
    return _call(
        body, "mlstm_fwd", (NC // A_GC,),
        [tok(512, 0), tok(512, 1), tok(1024, 1), tok(1024, 2),
         pl.BlockSpec((A_TB, 128), lambda i: (i, 0)),
         pl.BlockSpec((A_TB // 128, 8, 128), lambda i: (i, 0, 0)),
         pl.BlockSpec((1, 1024), lambda i: (0, 0))],
        [pl.BlockSpec((A_TB, 1024), lambda i: (i, 0)),
         pl.BlockSpec((A_GC, A_HEADS, A_QK, A_V), lambda i: (i, 0, 0, 0)),
         pl.BlockSpec((A_GC, A_HEADS, 1, 128), lambda i: (i, 0, 0, 0)),
         pl.BlockSpec((A_GC, A_HEADS, 1, 128), lambda i: (i, 0, 0, 0))],
        [jax.ShapeDtypeStruct((T, 1024), MM_DTYPE),
         jax.ShapeDtypeStruct((NC, A_HEADS, A_QK, A_V), F32),
         jax.ShapeDtypeStruct((NC, A_HEADS, 1, 128), F32),
         jax.ShapeDtypeStruct((NC, A_HEADS, 1, 128), F32)],
        [pltpu.VMEM((A_HEADS, A_QK, A_V), F32), pltpu.VMEM((A_HEADS, 1, 128), F32),
         pltpu.VMEM((A_HEADS, 1, 128), F32)],
        ("arbitrary",), (z, z, z, z, gcol, grow, hng), ride)


def mlstm_bwd(z, gcol, grow, hng, bias128, Cs, ns, ms, dhg, ride=None):
    T = z.shape[0]
    NC = T // A_CHUNK
    nsteps = NC // A_GC
    scale = A_QK ** -0.5

    def body(q_ref, k_ref, v_ref, o_ref, zg_ref, gc_ref, gr_ref, hng_ref, b_ref, Cs_ref, ns_ref, ms_ref,
             dhg_ref, dz_ref, dgn_ref, dbif_ref, dC_sc, dn_sc):
        @pl.when(pl.program_id(0) == 0)
        def _():
            dC_sc[...] = jnp.zeros_like(dC_sc)
            dn_sc[...] = jnp.zeros_like(dn_sc)
            dgn_ref[...] = jnp.zeros_like(dgn_ref)
            dbif_ref[...] = jnp.zeros_like(dbif_ref)

        ri = lax.broadcasted_iota(jnp.int32, (A_CHUNK, A_CHUNK), 0)
        ci = lax.broadcasted_iota(jnp.int32, (A_CHUNK, A_CHUNK), 1)
        causal = ri >= ci
        upper = (ci >= ri).astype(F32)
        rid = lax.broadcasted_iota(jnp.int32, (A_CHUNK, 1), 0)
        col = lax.broadcasted_iota(jnp.int32, (A_CHUNK, 128), 1)
        gr = jnp.concatenate([gr_ref[s] for s in range(A_TB // 128)], axis=1)
        for c in reversed(range(A_GC)):
            rows = slice(c * A_CHUNK, (c + 1) * A_CHUNK)
            gc = gc_ref[rows, :]
            grc = gr[:, c * A_CHUNK:(c + 1) * A_CHUNK]
            dG = jnp.zeros((A_CHUNK, 128), F32)
            hs = []
            for h in range(A_HEADS):
                sk, sv = _head_slices(h)
                s = dict(sk=sk, sv=sv, qh=(q_ref[rows, sk] * scale).astype(MM_DTYPE),
                         kh=k_ref[rows, sk].astype(MM_DTYPE), vh=v_ref[rows, sv].astype(MM_DTYPE),
                         lic=gc[:, h:h + 1], bc=gc[:, A_HEADS + h:A_HEADS + h + 1],
                         lir=grc[h:h + 1, :], br=grc[A_HEADS + h:A_HEADS + h + 1, :],
                         C=Cs_ref[c, h], n=ns_ref[c, h], m=ms_ref[c, h][:, 0:1], dC=dC_sc[h], dn=dn_sc[h])
                s['qf'], s['kf'] = s['qh'].astype(F32), s['kh'].astype(F32)
                s['wk'], s['decay'], _ = _state_weights(s['bc'], s['lic'], s['br'], s['lir'], s['m'])
                hs.append(s)
            for s in hs:
                s['A'] = _dot_nt(s['qh'], s['kh'])
                s['qC'] = _dot(s['qh'], s['C'])
                s['vdC'] = _dot_nt(s['vh'], s['dC'])
                s['kdC'] = _dot(s['kh'], s['dC'])
            for s in hs:
                s['E'], s['Sm'], s['wi'], s['den'], s['gs'], s['m_t'] = _chunk_decay(
                    s['A'], s['qh'], s['bc'], s['br'], s['lir'], s['n'], s['m'], causal)
            for s in hs:
                s['num'] = _dot(s['Sm'], s['vh']) + s['wi'] * s['qC']
            for h, s in enumerate(hs):
                sv, gs = s['sv'], s['gs']
                hh = s['num'] / gs
                r = lax.rsqrt(jnp.mean(hh * hh, axis=1, keepdims=True) + EPS)
                gn = hng_ref[:, sv]
                sg = _sigmoid(o_ref[rows, sv])
                dhg_h = dhg_ref[rows, sv]
                dhn = dhg_h * sg
                dz_ref[rows, 2048 + h * A_V:2048 + (h + 1) * A_V] = (
                    dhg_h * (hh * r * gn) * sg * (1.0 - sg)).astype(dz_ref.dtype)
                dgn_ref[:, sv] += jnp.sum(dhn * hh * r, axis=0, keepdims=True)
                dyg = dhn * gn
                dh = r * dyg - hh * (r * r * r) * jnp.mean(dyg * hh, axis=1, keepdims=True)
                s['dnum'] = dh / gs
                live = (jnp.abs(s['den']) > jnp.exp(-s['m_t'])).astype(F32)
                s['dden'] = -jnp.sum(dh * hh, axis=1, keepdims=True) / gs * jnp.sign(s['den']) * live
            for s in hs:
                s['dnv'] = _dot_nt(s['dnum'], s['vh'])
                s['dnC'] = _dot_nt(s['dnum'], s['C'])
            for s in hs:
                s['dSE'] = jnp.where(causal, s['dnv'] + s['dden'], 0.0) * s['E']
            for s in hs:
                s['dq'] = _dot(s['dSE'], s['kh']) + s['wi'] * (s['dnC'] + s['dden'] * s['n'])
                s['dk_inter'] = s['wk'] * (s['vdC'] + s['dn'])
                s['dk'] = _dot_tn(s['dSE'], s['qh']) + s['dk_inter']
                s['dv'] = _dot_tn(s['Sm'], s['dnum']) + s['wk'] * s['kdC']
                s['dCq'] = _dot_tn(s['qf'] * s['wi'], s['dnum'])
            for h, s in enumerate(hs):
                dq, dk, qf, kf, dC, dn = s['dq'], s['dk'], s['qf'], s['kf'], s['dC'], s['dn']
                dz_ref[rows, s['sk']] = (dq * scale).astype(dz_ref.dtype)
                dz_ref[rows, 512 + h * A_QK:512 + (h + 1) * A_QK] = dk.astype(dz_ref.dtype)
                dz_ref[rows, 1024 + h * A_V:1024 + (h + 1) * A_V] = s['dv'].astype(dz_ref.dtype)
                dli = jnp.sum(kf * dk, axis=1, keepdims=True)
                db = jnp.sum(qf * dq, axis=1, keepdims=True) - dli
                usum = jnp.sum(jnp.sum(kf * s['dk_inter'], axis=1, keepdims=True), axis=0, keepdims=True)
                ddecay = (jnp.sum(jnp.sum(dC * s['C'], axis=1, keepdims=True), axis=0, keepdims=True)
                          + jnp.sum(dn * s['n'], axis=1, keepdims=True))
                db = db + jnp.where(rid == A_CHUNK - 1, usum + ddecay * s['decay'], 0.0)
                dG = dG + jnp.where(col == h, dli, 0.0) + jnp.where(col == A_HEADS + h, db, 0.0)
                dC_sc[h] = s['decay'] * dC + s['dCq']
                dn_sc[h] = s['decay'] * dn + jnp.sum(qf * (s['wi'] * s['dden']), axis=0, keepdims=True)
            dlf = jnp.dot(upper, dG, precision=HI, preferred_element_type=F32)
            pre = zg_ref[rows, :] + b_ref[...]
            th = jnp.tanh(pre / SOFTCAP)
            dcap = 1.0 - th * th
            dpre = jnp.where(col < A_HEADS, dG * dcap,
                             jnp.where(col < 2 * A_HEADS, dlf * _sigmoid(-SOFTCAP * th) * dcap, 0.0))
            dz_ref[rows, GATE_COL:GATE_COL + 128] = dpre.astype(dz_ref.dtype)
            dbif_ref[...] += jnp.sum(dpre, axis=0, keepdims=True)

    rev = lambda i: nsteps - 1 - i
    tok = lambda w, cb: pl.BlockSpec((A_TB, w), lambda i: (rev(i), cb))
    st = lambda a, b: pl.BlockSpec((A_GC, A_HEADS, a, b), lambda i: (rev(i), 0, 0, 0))
    return _call(
        body, "mlstm_bwd", (nsteps,),
        [tok(512, 0), tok(512, 1), tok(1024, 1), tok(1024, 2), tok(128, GATE_COL // 128),
         pl.BlockSpec((A_TB, 128), lambda i: (rev(i), 0)),
         pl.BlockSpec((A_TB // 128, 8, 128), lambda i: (rev(i), 0, 0)),
         pl.BlockSpec((1, 1024), lambda i: (0, 0)),
         pl.BlockSpec((1, 128), lambda i: (0, 0)),
         st(A_QK, A_V), st(1, 128), st(1, 128),
         pl.BlockSpec((A_TB, 1024), lambda i: (rev(i), 0))],
        [pl.BlockSpec((A_TB, A_IN_PAD), lambda i: (rev(i), 0)),
         pl.BlockSpec((1, 1024), lambda i: (0, 0)),
         pl.BlockSpec((1, 128), lambda i: (0, 0))],
        [jax.ShapeDtypeStruct((T, A_IN_PAD), MM_DTYPE), jax.ShapeDtypeStruct((1, 1024), F32),
         jax.ShapeDtypeStruct((1, 128), F32)],
        [pltpu.VMEM((A_HEADS, A_QK, A_V), F32), pltpu.VMEM((A_HEADS, 1, 128), F32)],
        ("arbitrary",), (z, z, z, z, z, gcol, grow, hng, bias128, Cs, ns, ms, dhg), ride)


def _t5_bucket(dist):
    max_exact = REL_BUCKETS // 2
    d = np.maximum(dist, 0)
    log_ratio = np.log(np.maximum(d, 1) / max_exact) / math.log(REL_MAX_DIST / max_exact)
    large = np.minimum(max_exact + (log_ratio * (REL_BUCKETS - max_exact)).astype(np.int64), REL_BUCKETS - 1)
    return np.where(d < max_exact, d, large).astype(np.int32)


def _group_bucket(g):
    delta = B_BLOCK + np.arange(B_BLOCK)[:, None] - np.arange(2 * B_BLOCK)[None, :]
    return _t5_bucket(delta * DILATIONS[g])


def _band_mask(n):
    ri = lax.broadcasted_iota(jnp.int32, (B_BLOCK, 2 * B_BLOCK), 0)
    ci = lax.broadcasted_iota(jnp.int32, (B_BLOCK, 2 * B_BLOCK), 1)
    band = jnp.logical_and(ci >= ri, ci <= ri + B_BLOCK)
    return jnp.logical_and(band, jnp.logical_or(ci >= B_BLOCK, n > 0))


def _both(p_ref, c_ref, sl):
    return jnp.concatenate([p_ref[:, sl], c_ref[:, sl]], axis=0)


def _scores(qh, kh, bias_h, valid):
    return jnp.where(valid, _dot_nt(qh, kh) * (B_DH ** -0.5) + bias_h, -jnp.inf)


def _attn_specs():
    wide = pl.BlockSpec((B_BLOCK, 1024), lambda r, n: (n, r))
    prev = pl.BlockSpec((B_BLOCK, 1024), lambda r, n: (jnp.maximum(n - 1, 0), r))
    narrow = pl.BlockSpec((B_BLOCK, 128), lambda r, n: (n, r))
    bias = pl.BlockSpec((B_HEADS, B_BLOCK, 2 * B_BLOCK), lambda r, n: (0, 0, 0))
    return wide, prev, narrow, bias


def _to_view(read_chunk, sc, o_ref, dil, nc, tt):
    for c in range(nc):
        sc[c] = read_chunk(c)
    for r in range(dil):
        for c in range(nc):
            lo = (r * nc + c) * 128
            o_ref[:, lo:lo + 128] = sc[c, pl.ds(r, tt // dil, stride=dil), :].astype(o_ref.dtype)


def _from_view(read_view, sc, dil, nc, tt):
    for r in range(dil):
        for c in range(nc):
            sc[c, pl.ds(r, tt // dil, stride=dil), :] = read_view((r * nc + c) * 128).astype(F32)


def attn_fwd(qv, kvw, vvw, bias, g):
    dil = DILATIONS[g]
    Tv = qv.shape[0]
    nb = Tv // B_BLOCK
    wide, prev, narrow, bsp = _attn_specs()

    def body(q_ref, kp_ref, kc_ref, vp_ref, vc_ref, b_ref, o_ref, lse_ref):
        valid = _band_mask(pl.program_id(1))
        lse_ref[...] = jnp.zeros_like(lse_ref)
        heads = [slice(h * B_DH, (h + 1) * B_DH) for h in range(B_HEADS)]
        S = [_scores(q_ref[:, sl], _both(kp_ref, kc_ref, sl), b_ref[h], valid) for h, sl in enumerate(heads)]
        P, L = [], []
        for h in range(B_HEADS):
            m = jnp.max(S[h], axis=1, keepdims=True)
            p = jnp.exp(S[h] - m)
            l = jnp.sum(p, axis=1, keepdims=True)
            lse_ref[:, h:h + 1] = m + jnp.log(l)
            P.append(p.astype(MM_DTYPE))
            L.append(l)
        for h, sl in enumerate(heads):
            o_ref[:, sl] = _dot(P[h], _both(vp_ref, vc_ref, sl)) / L[h]

    return pl.pallas_call(
        body, name=f"attn_fwd_g{g}", grid=(dil, nb),
        in_specs=[wide, prev, wide, prev, wide, bsp], out_specs=[wide, narrow],
        out_shape=[jax.ShapeDtypeStruct((Tv, dil * 1024), F32), jax.ShapeDtypeStruct((Tv, dil * 128), F32)],
        compiler_params=_params("parallel", "parallel"),
    )(qv, kvw, kvw, vvw, vvw, bias)


def attn_bwd(qv, kvw, vvw, bias, do_v, lse_v, dl_v, g):
    dil = DILATIONS[g]
    Tv = qv.shape[0]
    nb = Tv // B_BLOCK
    wide, prev, narrow, bsp = _attn_specs()

    def body(q_ref, kp_ref, kc_ref, vp_ref, vc_ref, b_ref, bt_ref, do_ref, lse_ref, dl_ref,
             dq_ref, dkc_ref, dkp_ref, dvc_ref, dvp_ref, db_ref):
        @pl.when(jnp.logical_and(pl.program_id(0) == 0, pl.program_id(1) == 0))
        def _():
            db_ref[...] = jnp.zeros_like(db_ref)

        n = pl.program_id(1)
        valid = _band_mask(n)
        ki = lax.broadcasted_iota(jnp.int32, (2 * B_BLOCK, B_BLOCK), 0)
        qi = lax.broadcasted_iota(jnp.int32, (2 * B_BLOCK, B_BLOCK), 1)
        valid_t = jnp.logical_and(jnp.logical_and(ki >= qi, ki <= qi + B_BLOCK), jnp.logical_or(ki >= B_BLOCK, n > 0))
        lse_t, dl_t = lse_ref[...].T, dl_ref[...].T
        heads = [slice(h * B_DH, (h + 1) * B_DH) for h in range(B_HEADS)]
        scale = B_DH ** -0.5
        PT, DS, DST = [], [], []
        for h, sl in enumerate(heads):
            qh, doh = q_ref[:, sl], do_ref[:, sl].astype(MM_DTYPE)
            kh, vh = _both(kp_ref, kc_ref, sl), _both(vp_ref, vc_ref, sl)
            p = jnp.exp(_scores(qh, kh, b_ref[h], valid) - lse_ref[:, h:h + 1])
            ds = p * (_dot_nt(doh, vh) - dl_ref[:, h:h + 1])
            db_ref[h] += ds
            DS.append((ds * scale).astype(MM_DTYPE))
            pt = jnp.exp(_scores(kh, qh, bt_ref[h], valid_t) - lse_t[h:h + 1, :])
            PT.append(pt.astype(MM_DTYPE))
            DST.append((pt * (_dot_nt(vh, doh) - dl_t[h:h + 1, :]) * scale).astype(MM_DTYPE))
        for h, sl in enumerate(heads):
            qh, doh = q_ref[:, sl], do_ref[:, sl].astype(MM_DTYPE)
            dq_ref[:, sl] = _dot(DS[h], _both(kp_ref, kc_ref, sl)).astype(MM_DTYPE)
            dk = _dot(DST[h], qh).astype(MM_DTYPE)
            dv = _dot(PT[h], doh).astype(MM_DTYPE)
            dkp_ref[:, sl], dkc_ref[:, sl] = dk[:B_BLOCK], dk[B_BLOCK:]
            dvp_ref[:, sl], dvc_ref[:, sl] = dv[:B_BLOCK], dv[B_BLOCK:]

    big = jax.ShapeDtypeStruct((Tv, dil * 1024), MM_DTYPE)
    bsp_t = pl.BlockSpec((B_HEADS, 2 * B_BLOCK, B_BLOCK), lambda r, n: (0, 0, 0))
    return pl.pallas_call(
        body, name=f"attn_bwd_g{g}", grid=(dil, nb),
        in_specs=[wide, prev, wide, prev, wide, bsp, bsp_t, wide, narrow, narrow],
        out_specs=[wide] * 5 + [bsp],
        out_shape=[big] * 5 + [jax.ShapeDtypeStruct((B_HEADS, B_BLOCK, 2 * B_BLOCK), F32)],
        compiler_params=_params("arbitrary", "arbitrary"),
    )(qv, kvw, kvw, vvw, vvw, bias, jnp.swapaxes(bias, 1, 2), do_v, lse_v, dl_v)


def _head_expand():
    e = np.zeros((128, 1024), np.float32)
    for h in range(B_HEADS):
        e[h, h * B_DH:(h + 1) * B_DH] = 1.0
    return e


A_TT = 256
A_TT_WIDE = 512


def _view_spec(dil, width, tt=A_TT):
    return pl.BlockSpec((tt // dil, dil * width), lambda i: (i, 0))


def attn_merge(os_v, lses_v):
    T = os_v[0].shape[0]
    tt = A_TT_WIDE
    expand = jnp.asarray(_head_expand())

    def body(o0, o1, o2, l0, l1, l2, e_ref, ob_ref, of_ref, lse0_ref, lse1_ref, lse2_ref, sc_o, sc_l):
        for gi, (o_ref, l_ref) in enumerate(((o1, l1), (o2, l2))):
            dil = DILATIONS[gi + 1]
            _from_view(lambda lo: o_ref[:, lo:lo + 128], sc_o.at[gi], dil, 8, tt)
            _from_view(lambda lo: l_ref[:, lo:lo + 128], sc_l.at[gi], dil, 1, tt)
        ls = [l0[...], sc_l[0, 0], sc_l[1, 0]]
        m = jnp.maximum(jnp.maximum(ls[0], ls[1]), ls[2])
        ex = [jnp.exp(l - m) for l in ls]
        tot = ex[0] + ex[1] + ex[2]
        lse = m + jnp.log(tot)
        lse0_ref[...] = lse
        _to_view(lambda c: lse, sc_l.at[2], lse1_ref, DILATIONS[1], 1, tt)
        _to_view(lambda c: lse, sc_l.at[2], lse2_ref, DILATIONS[2], 1, tt)
        ws = [e / tot for e in ex]
        for c in range(8):
            cols = slice(c * 128, (c + 1) * 128)
            ecol = e_ref[:, cols]
            spread = [jnp.dot(w, ecol, precision=HI, preferred_element_type=F32) for w in ws]
            out = spread[0] * o0[:, cols] + spread[1] * sc_o[0, c] + spread[2] * sc_o[1, c]
            of_ref[:, cols] = out
            ob_ref[:, cols] = out.astype(ob_ref.dtype)

    wide = pl.BlockSpec((tt, 1024), lambda i: (i, 0))
    return pl.pallas_call(
        body, name="attn_merge", grid=(T // tt,),
        in_specs=[_view_spec(d, 1024, tt) for d in DILATIONS] + [_view_spec(d, 128, tt) for d in DILATIONS]
        + [pl.BlockSpec((128, 1024), lambda i: (0, 0))],
        out_specs=[wide, wide] + [_view_spec(d, 128, tt) for d in DILATIONS],
        out_shape=[jax.ShapeDtypeStruct((T, 1024), MM_DTYPE), jax.ShapeDtypeStruct((T, 1024), F32)]
        + [jax.ShapeDtypeStruct((T // d, d * 128), F32) for d in DILATIONS],
        scratch_shapes=[pltpu.VMEM((2, 8, tt, 128), F32), pltpu.VMEM((3, 1, tt, 128), F32)],
        compiler_params=_params("parallel"),
    )(*os_v, *lses_v, expand)


def attn_prep(datt, out):
    T = datt.shape[0]
    tt = A_TT_WIDE
    expand_t = jnp.asarray(_head_expand().T.copy())

    def body(d_ref, o_ref, e_ref, do0, do1, do2, dl0, dl1, dl2, sc_d, sc_l):
        delta = jnp.dot(d_ref[...] * o_ref[...], e_ref[...], precision=HI, preferred_element_type=F32)
        do0[...] = d_ref[...].astype(do0.dtype)
        dl0[...] = delta
        for do_ref, dl_ref, dil in ((do1, dl1, DILATIONS[1]), (do2, dl2, DILATIONS[2])):
            _to_view(lambda c: d_ref[:, c * 128:(c + 1) * 128], sc_d, do_ref, dil, 8, tt)
            _to_view(lambda c: delta, sc_l, dl_ref, dil, 1, tt)

    wide = pl.BlockSpec((tt, 1024), lambda i: (i, 0))
    return pl.pallas_call(
        body, name="attn_prep", grid=(T // tt,),
        in_specs=[wide, wide, pl.BlockSpec((1024, 128), lambda i: (0, 0))],
        out_specs=[_view_spec(d, 1024, tt) for d in DILATIONS] + [_view_spec(d, 128, tt) for d in DILATIONS],
        out_shape=[jax.ShapeDtypeStruct((T // d, d * 1024), MM_DTYPE) for d in DILATIONS]
        + [jax.ShapeDtypeStruct((T // d, d * 128), F32) for d in DILATIONS],
        scratch_shapes=[pltpu.VMEM((8, tt, 128), F32), pltpu.VMEM((1, tt, 128), F32)],
        compiler_params=_params("parallel"),
    )(datt, out, expand_t)


def attn_combine(parts):
    T = parts[0][0].shape[0]
    tt = A_TT
    nt = T // tt
    shift = [None] + [B_BLOCK * d // tt for d in DILATIONS[1:]]

    def body(dq0, kc0, vc0, kpa0, kpb0, vpa0, vpb0, dq1, kc1, kp1, vc1, vp1, dq2, kc2, kp2, vc2, vp2,
             dq_ref, dkv_ref, sc):
        i = pl.program_id(0)
        dq_ref[:, 0:1024] = dq0[...].astype(dq_ref.dtype)
        for col, c_ref, pa_ref, pb_ref in ((0, kc0, kpa0, kpb0), (3, vc0, vpa0, vpb0)):
            nxt = jnp.where(i + 1 < nt, pb_ref[:tt // 2, :].astype(F32), 0.0)
            later = jnp.concatenate([pa_ref[tt // 2:, :].astype(F32), nxt], axis=0)
            dkv_ref[:, col * 1024:(col + 1) * 1024] = (c_ref[...].astype(F32) + later).astype(dkv_ref.dtype)
        for g, (dq, kc, kp, vc, vp) in ((1, (dq1, kc1, kp1, vc1, vp1)), (2, (dq2, kc2, kp2, vc2, vp2))):
            dil = DILATIONS[g]
            live = i + shift[g] < nt
            _from_view(lambda lo: dq[:, lo:lo + 128], sc, dil, 8, tt)
            for c in range(8):
                dq_ref[:, g * 1024 + c * 128:g * 1024 + (c + 1) * 128] = sc[c].astype(dq_ref.dtype)
            for col, c_ref, p_ref in ((g, kc, kp), (3 + g, vc, vp)):
                _from_view(lambda lo: c_ref[:, lo:lo + 128].astype(F32)
                           + jnp.where(live, p_ref[:, lo:lo + 128].astype(F32), 0.0), sc, dil, 8, tt)
                for c in range(8):
                    dkv_ref[:, col * 1024 + c * 128:col * 1024 + (c + 1) * 128] = sc[c].astype(dkv_ref.dtype)

    def later_spec(dil, blocks):
        return pl.BlockSpec((tt // dil, dil * 1024), lambda i: (jnp.minimum(i + blocks, nt - 1), 0))

    cur = [_view_spec(d, 1024) for d in DILATIONS]
    in_specs = [cur[0], cur[0], cur[0], cur[0], later_spec(1, 1), cur[0], later_spec(1, 1)]
    args = [parts[0][0], parts[0][1], parts[0][3], parts[0][2], parts[0][2], parts[0][4], parts[0][4]]
    for g in (1, 2):
        in_specs += [cur[g], cur[g], later_spec(DILATIONS[g], shift[g]), cur[g], later_spec(DILATIONS[g], shift[g])]
        args += list(parts[g][:5])
    return pl.pallas_call(
        body, name="attn_combine", grid=(nt,), in_specs=in_specs,
        out_specs=[pl.BlockSpec((tt, 3072), lambda i: (i, 0)), pl.BlockSpec((tt, 6144), lambda i: (i, 0))],
        out_shape=[jax.ShapeDtypeStruct((T, 3072), MM_DTYPE), jax.ShapeDtypeStruct((T, 6144), MM_DTYPE)],
        scratch_shapes=[pltpu.VMEM((8, tt, 128), F32)],
        compiler_params=_params("parallel"),
    )(*args)


def adamw(w, g, m, v, name):
    R, C = w.shape
    tr = R if R * C * 4 <= (1 << 20) else _rows(R, max(8, ((1 << 20) // (C * 4)) // 8 * 8))

    def body(w_ref, g_ref, m_ref, v_ref, d_ref, nm_ref, nv_ref):
        gg = g_ref[...]
        nm = ADAM_B1 * m_ref[...] + (1.0 - ADAM_B1) * gg
        nv = ADAM_B2 * v_ref[...] + (1.0 - ADAM_B2) * (gg * gg)
        m_hat = nm / (1.0 - ADAM_B1 ** ADAM_STEP)
        v_hat = nv / (1.0 - ADAM_B2 ** ADAM_STEP)
        d_ref[...] = -ADAM_LR * (m_hat / (jnp.sqrt(v_hat) + ADAM_EPS) + ADAM_WD * w_ref[...])
        nm_ref[...] = nm
        nv_ref[...] = nv

    blk = pl.BlockSpec((tr, C), lambda i: (i, 0))
    sds = jax.ShapeDtypeStruct((R, C), F32)
    return pl.pallas_call(
        body, name=name, grid=(R // tr,), in_specs=[blk] * 4, out_specs=[blk] * 3, out_shape=[sds] * 3,
        compiler_params=_params("parallel"),
    )(w, g, m, v)


def sum_slots(x, name, out_dtype=F32):
    n, R, C = x.shape
    tr = _rows(R, 256)

    def body(x_ref, o_ref):
        acc = x_ref[0].astype(F32)
        for s in range(1, n):
            acc = acc + x_ref[s].astype(F32)
        o_ref[...] = acc.astype(out_dtype)

    return pl.pallas_call(
        body, name=name, grid=(R // tr,),
        in_specs=[pl.BlockSpec((n, tr, C), lambda i: (0, i, 0))],
        out_specs=pl.BlockSpec((tr, C), lambda i: (i, 0)),
        out_shape=jax.ShapeDtypeStruct((R, C), out_dtype),
        compiler_params=_params("parallel"),
    )(x)


_ANY = pl.BlockSpec(memory_space=pl.ANY)
GROUP_ALL = ([(0, 0, 1), (0, 1, 0), (0, 1, 1), (1, 0, 0), (1, 0, 1), (1, 1, 0), (1, 1, 1)],
             lambda d: 4 * d[0] + 2 * d[1] + d[2])
GROUP_CHIPS = ([(0, 1, 0), (1, 0, 0), (1, 1, 0)], lambda d: 2 * d[0] + d[1])
GROUP_SIBLING = ([(0, 0, 1)], lambda d: d[2])


def _me():
    return lax.axis_index("x"), lax.axis_index("y"), lax.axis_index("c")


def _peer(me, flip):
    return tuple(1 - a if f else a for a, f in zip(me, flip))


class Exchange:
    def __init__(self, x, group, scatter):
        self.flips, self.slot = group
        self.scatter = scatter
        self.n = len(self.flips) + 1
        self.out_shape = jax.ShapeDtypeStruct((self.n,) + x.shape[-2:], x.dtype)
        self.scratch = [pltpu.SemaphoreType.DMA((self.n - 1,)), pltpu.SemaphoreType.DMA((self.n - 1,)),
                        pltpu.SemaphoreType.DMA]

    def _copies(self, x_ref, o_ref, send_sems, recv_sems, local_sem, arrivals):
        me = _me()
        slot = self.slot
        mine = pltpu.make_async_copy(x_ref.at[slot(me)] if self.scatter else x_ref, o_ref.at[slot(me)], local_sem)
        sends, landed = [], []
        for k, flip in enumerate(self.flips):
            peer = _peer(me, flip)
            sends.append(pltpu.make_async_remote_copy(
                src_ref=x_ref.at[slot(peer)] if self.scatter else x_ref, dst_ref=o_ref.at[slot(me)],
                send_sem=send_sems.at[k], recv_sem=recv_sems.at[k], device_id=peer, device_id_type=MESH_ID))
            if arrivals:
                landed.append(pltpu.make_async_remote_copy(
                    src_ref=o_ref.at[slot(me)], dst_ref=o_ref.at[slot(peer)], send_sem=send_sems.at[k],
                    recv_sem=recv_sems.at[k], device_id=peer, device_id_type=MESH_ID))
        return mine, sends, landed

    def start(self, *refs):
        mine, sends, _ = self._copies(*refs, arrivals=False)
        mine.start()
        for cp in sends:
            cp.start()

    def wait(self, *refs):
        mine, sends, arrivals = self._copies(*refs, arrivals=True)
        for cp in arrivals:
            cp.wait_recv()
        for cp in sends:
            cp.wait_send()
        mine.wait()

    def __call__(self, x, name):
        def body(*refs):
            self.start(*refs)
            self.wait(*refs)

        return pl.pallas_call(body, name=name, in_specs=[_ANY], out_specs=_ANY, out_shape=self.out_shape,
                              scratch_shapes=self.scratch)(x)


def group_gather(x, name, group):
    return Exchange(x, group, scatter=False)(x, name)


def group_scatter(x, name, group):
    return Exchange(x, group, scatter=True)(x, name)


def _call(body, name, grid, in_specs, out_specs, out_shape, scratch, semantics, args, ride=None):
    if ride is None:
        return pl.pallas_call(body, name=name, grid=grid, in_specs=in_specs, out_specs=out_specs,
                              out_shape=out_shape, scratch_shapes=scratch,
                              compiler_params=_params(*semantics))(*args)
    x, exch = ride
    n_in, n_out, n_scr = len(in_specs), len(out_specs), len(scratch)

    def at_step(pick):
        hit = None
        for axis, size in enumerate(grid):
            here = pl.program_id(axis) == pick(size)
            hit = here if hit is None else jnp.logical_and(hit, here)
        return hit

    def riding(*refs):
        ins, x_ref = refs[:n_in], refs[n_in]
        outs, o_ref = refs[n_in + 1:n_in + 1 + n_out], refs[n_in + 1 + n_out]
        scr, sems = refs[n_in + 2 + n_out:n_in + 2 + n_out + n_scr], refs[n_in + 2 + n_out + n_scr:]

        @pl.when(at_step(lambda size: 0))
        def _():
            exch.start(x_ref, o_ref, *sems)

        body(*ins, *outs, *scr)

        @pl.when(at_step(lambda size: size - 1))
        def _():
            exch.wait(x_ref, o_ref, *sems)

    return pl.pallas_call(
        riding, name=name, grid=grid, in_specs=list(in_specs) + [_ANY], out_specs=list(out_specs) + [_ANY],
        out_shape=list(out_shape) + [exch.out_shape], scratch_shapes=list(scratch) + exch.scratch,
        compiler_params=_params(*(["arbitrary"] * len(grid))))(*args, x)


WEIGHTS = ['a_norm_g', 'a_w_in', 'a_b_if', 'a_hnorm_g', 'a_w_out', 'kv_norm_g', 'w_kv', 'b_norm_g', 'b_w_q',
           'b_w_out', 'rel_bias', 'f_norm_g', 'f_w_up', 'f_conv_w', 'f_conv_b', 'f_w_down', 'final_norm_g']
SHARD_AXIS = {'a_norm_g': 1, 'a_w_in': 2, 'a_b_if': None, 'a_hnorm_g': 2, 'a_w_out': 1, 'kv_norm_g': None,
              'w_kv': 1, 'b_norm_g': None, 'b_w_q': 2, 'b_w_out': 1, 'rel_bias': None, 'f_norm_g': None,
              'f_w_up': 2, 'f_conv_w': 2, 'f_conv_b': None, 'f_w_down': 1, 'final_norm_g': None}
BIG = ['a_w_in', 'a_w_out', 'w_kv', 'b_w_q', 'b_w_out', 'f_w_up', 'f_w_down']
SMALL = [n for n in WEIGHTS if n not in BIG]
LANES = 1024
PIECES = {'a_w_in': ('a_w_in', None, 2), 'a_w_out': ('a_w_out', None, 1), 'f_w_up0': ('f_w_up', 0, 1),
          'f_w_down0': ('f_w_down', 0, 0), 'w_kv': ('w_kv', None, 1), 'b_w_q': ('b_w_q', None, 2),
          'b_w_out': ('b_w_out', None, 1), 'f_w_up1': ('f_w_up', 1, 1), 'f_w_down1': ('f_w_down', 1, 0)}
LATE = ['w_kv', 'b_w_q', 'b_w_out', 'f_w_up1', 'f_w_down1']
WEIGHT_WAVES = {'first': ['a_w_in', 'a_w_out'], 'ffn0': ['f_w_up0', 'f_w_down0'], 'late': LATE}
GRAD_WAVES = {'late': LATE, 'layer0': ['f_w_up0', 'f_w_down0', 'a_w_out'], 'last': ['a_w_in']}


def _piece(arrays, p):
    leaf, layer, _ = PIECES[p]
    return arrays[leaf] if layer is None else arrays[leaf][layer]


class Packer:
    def __init__(self, pieces, shard):
        self.pieces = pieces
        self.shapes = [_piece(shard, p).shape for p in pieces]
        self.sizes = [math.prod(s) // (2 * LANES) for s in self.shapes]
        self.fill = -sum(self.sizes) % 16
        self.rows = sum(self.sizes) + self.fill

    def my_half(self, shard, half):
        both = jnp.concatenate([_piece(shard, p).astype(MM_DTYPE).reshape(2, -1, LANES) for p in self.pieces], axis=1)
        return jnp.pad(lax.dynamic_index_in_dim(both, half, axis=0, keepdims=False), ((0, self.fill), (0, 0)))

    def full_weights(self, gathered):
        g = gathered.reshape(4, 2, self.rows, LANES)
        out, off = {}, 0
        for p, shp, sz in zip(self.pieces, self.shapes, self.sizes):
            out[p] = _full_from_shards(g[:, :, off:off + sz].reshape((4,) + shp), PIECES[p][2])
            off += sz
        return out

    def grad_slots(self, grads):
        parts = [_shards_from_full(grads[p], PIECES[p][2]).reshape(4, 2, -1, LANES).astype(GRAD_WIRE_DTYPE)
                 for p in self.pieces]
        parts.append(jnp.zeros((4, 2, self.fill, LANES), GRAD_WIRE_DTYPE))
        return jnp.concatenate(parts, axis=2).reshape(8, self.rows, LANES)

    def shard_grads(self, both):
        out, off = {}, 0
        for p, shp, sz in zip(self.pieces, self.shapes, self.sizes):
            out[p] = both[:, off:off + sz].reshape(shp).astype(F32)
            off += sz
        return out


class Overlap:
    def __init__(self, shard, half):
        self.shard, self.half = shard, half
        self.weights = {w: Packer(p, shard) for w, p in WEIGHT_WAVES.items()}
        self.grads = {w: Packer(p, shard) for w, p in GRAD_WAVES.items()}
        self.shard_grads = {}

    def gather_ride(self, wave):
        mine = self.weights[wave].my_half(self.shard, self.half)
        return mine, Exchange(mine, GROUP_ALL, scatter=False)

    def gathered(self, wave, slots):
        return self.weights[wave].full_weights(slots)

    def scatter_ride(self, wave, grads):
        slots = self.grads[wave].grad_slots({p: grads.pop(p) for p in GRAD_WAVES[wave]})
        return slots, Exchange(slots, GROUP_ALL, scatter=True)

    def join_ride(self, wave, received):
        reduced = sum_slots(received, f"sum_grads_{wave}", GRAD_WIRE_DTYPE)
        return reduced, Exchange(reduced, GROUP_SIBLING, scatter=False)

    def joined(self, wave, both):
        self.shard_grads.update(self.grads[wave].shard_grads(both))


def _pad_rows(flat, mult):
    n = flat.shape[0]
    per = LANES * mult
    tot = -(-n // per) * per
    return jnp.pad(flat, (0, tot - n)).reshape(tot // LANES, LANES)


def _full_from_shards(sh, axis):
    shp = sh.shape[1:]
    return jnp.moveaxis(sh, 0, axis).reshape(shp[:axis] + (4 * shp[axis],) + shp[axis + 1:])


def _shards_from_full(full, axis):
    shp = full.shape
    return jnp.moveaxis(full.reshape(shp[:axis] + (4, shp[axis] // 4) + shp[axis + 1:]), axis, 0)


def _local_step(x, target, W, overlap=None):
    T = x.shape[0]
    W = dict(W)
    row = lambda a: a.reshape(1, -1).astype(F32)
    w_in = jnp.pad(W['a_w_in'][0], ((0, 0), (0, A_IN_PAD - A_IN)))
    bias128 = jnp.pad(row(W['a_b_if'][0]), ((0, 0), (0, 120)))
    hng = row(W['a_hnorm_g'][0])
    w_up = lambda l: _interleave(W[f'f_w_up{l}'])
    cw = [_interleave(W['f_conv_w'][l].astype(F32)) for l in range(2)]
    cb = [_interleave(row(W['f_conv_b'][l])) for l in range(2)]
    onehots = [(jnp.asarray(_group_bucket(g).reshape(-1, 1)) == jnp.arange(128)[None, :]).astype(F32)
               for g in range(N_GROUPS)]
    rb_t = jnp.pad(W['rel_bias'].astype(F32).T, ((0, 0), (0, 128 - REL_BUCKETS)))
    biases = [mm_nn(rb_t[g * B_HEADS:(g + 1) * B_HEADS], onehots[g].T, f"rel_bias_table_g{g}", exact=True)
              .reshape(B_HEADS, B_BLOCK, 2 * B_BLOCK) for g in range(N_GROUPS)]
    G = {}

    def ffn_fwd(xin, l, ride=None):
        xn, = rms_fwd(xin, [row(W['f_norm_g'][l])], f"ffn{l}_norm")
        u, act, *rode = ffn_up_act(xn, w_up(l), cw[l], cb[l], f"ffn{l}_up_act", ride)
        return mm_nn(act, W[f'f_w_down{l}'], f"ffn{l}_down", res=xin), (xn, u, act), rode

    def ffn_bwd(xin, saved, dout, l, ride=None):
        xn, u, act = saved
        dact = mm_nn(dout, W[f'f_w_down{l}'].T, f"ffn{l}_ddown")
        G[f'f_w_down{l}'] = mm_tn(act, dout, f"ffn{l}_gdown")
        du, gcw, gcb, *rode = conv_act_bwd(u, dact, cw[l], cb[l], f"ffn{l}_dact", ride)
        dxn = mm_nn(du, w_up(l).T, f"ffn{l}_dup")
        G[f'f_w_up{l}'] = _deinterleave(mm_tn(xn, du, f"ffn{l}_gup"))
        dxin, (gn,) = rms_bwd(xin, dout, [(dxn, row(W['f_norm_g'][l]))], f"ffn{l}_dnorm")
        return dxin, _deinterleave(gcw), _deinterleave(gcb), gn, rode

    xn_a, = rms_fwd(x, [row(W['a_norm_g'][0])], "a_norm")
    z = mm_nn(xn_a, w_in, "a_in")
    gcol, grow = gate_prep(z, bias128)
    hg, Cs, ns, ms, *rode = mlstm_fwd(z, gcol, grow, hng, overlap.gather_ride('ffn0') if overlap else None)
    if overlap:
        W.update(overlap.gathered('ffn0', rode[0]))
    x1 = mm_nn(hg, W['a_w_out'][0], "a_out", res=x)
    x2, ffn0, rode = ffn_fwd(x1, 0, overlap.gather_ride('late') if overlap else None)
    if overlap:
        W.update(overlap.gathered('late', rode[0]))
    xn_kv, xn_b = rms_fwd(x2, [row(W['kv_norm_g']), row(W['b_norm_g'][0])], "b_norms")
    gcols = lambda w, c: w[:, c * 1024:(c + 1) * 1024]
    qv = [mm_view(xn_b, gcols(W['b_w_q'][0], g), f"q_proj_g{g}", DILATIONS[g]) for g in range(N_GROUPS)]
    kvw = [mm_view(xn_kv, gcols(W['w_kv'], g), f"k_proj_g{g}", DILATIONS[g]) for g in range(N_GROUPS)]
    vvw = [mm_view(xn_kv, gcols(W['w_kv'], 3 + g), f"v_proj_g{g}", DILATIONS[g]) for g in range(N_GROUPS)]
    os_, lses = zip(*[attn_fwd(qv[g], kvw[g], vvw[g], biases[g], g) for g in range(N_GROUPS)])
    att, att_f, *lse_v = attn_merge(os_, lses)
    x3 = mm_nn(att, W['b_w_out'][0], "b_out", res=x2)
    x4, ffn1, _ = ffn_fwd(x3, 1)
    dx4, g_final, loss = loss_head(x4, target, row(W['final_norm_g']))
    G['final_norm_g'] = g_final.reshape(-1)

    dx3, gcw1, gcb1, gn1, _ = ffn_bwd(x3, ffn1, dx4, 1)
    datt = mm_nn(dx3, W['b_w_out'][0].T, "b_dout")
    G['b_w_out'] = mm_tn(att, dx3, "b_gout")[None]
    prep = attn_prep(datt, att_f)
    do_v, dl_v = prep[:3], prep[3:]
    parts = [attn_bwd(qv[g], kvw[g], vvw[g], biases[g], do_v[g], lse_v[g], dl_v[g], g) for g in range(N_GROUPS)]
    dq_all, dkv = attn_combine(parts)
    grb = []
    for g in range(N_GROUPS):
        gb = mm_nn(parts[g][5].reshape(B_HEADS, -1), onehots[g], f"rel_bias_g{g}", exact=True)
        grb.append(gb[:, :REL_BUCKETS].T)
    G['rel_bias'] = jnp.concatenate(grb, axis=1)
    dxn_b = mm_nn(dq_all, W['b_w_q'][0].T, "q_dproj")
    G['b_w_q'] = mm_tn(xn_b, dq_all, "q_gproj")[None]
    dxn_kv = mm_nn(dkv, W['w_kv'].T, "kv_dproj")
    G['w_kv'] = mm_tn(xn_kv, dkv, "kv_gproj")
    dx2, (g_kvn, g_bn) = rms_bwd(x2, dx3, [(dxn_kv, row(W['kv_norm_g'])), (dxn_b, row(W['b_norm_g'][0]))],
                                 "b_dnorms")
    G['kv_norm_g'] = g_kvn.reshape(-1)
    G['b_norm_g'] = g_bn
    dx1, gcw0, gcb0, gn0, late_slots = ffn_bwd(x1, ffn0, dx2, 0, overlap.scatter_ride('late', G) if overlap else None)
    G['f_conv_w'] = jnp.stack([gcw0, gcw1])
    G['f_conv_b'] = jnp.concatenate([gcb0, gcb1], axis=0)
    G['f_norm_g'] = jnp.concatenate([gn0, gn1], axis=0)
    dhg = mm_nn(dx1, W['a_w_out'][0].T, "a_dout")
    G['a_w_out'] = mm_tn(hg, dx1, "a_gout")[None]
    dz, g_hn, g_bif, *layer0_slots = mlstm_bwd(z, gcol, grow, hng, bias128, Cs, ns, ms, dhg,
                                               overlap.scatter_ride('layer0', G) if overlap else None)
    G['a_hnorm_g'] = g_hn.reshape(1, A_HEADS, A_V)
    G['a_b_if'] = g_bif[:, :2 * A_HEADS]
    if overlap:
        dxn_a, both = mm_nn(dz, w_in.T, "a_din", ride=overlap.join_ride('late', late_slots[0]))
        overlap.joined('late', both)
        g_in, both = mm_tn(xn_a, dz, "a_gin", ride=overlap.join_ride('layer0', layer0_slots[0]))
        overlap.joined('layer0', both)
    else:
        dxn_a = mm_nn(dz, w_in.T, "a_din")
        g_in = mm_tn(xn_a, dz, "a_gin")
    G['a_w_in'] = g_in[:, :A_IN][None]
    grad_x, (g_an,) = rms_bwd(x, dx1, [(dxn_a, row(W['a_norm_g'][0]))], "a_dnorm")
    G['a_norm_g'] = g_an
    return loss, grad_x, G


def kernel(x, a_norm_g, a_w_in, a_b_if, a_hnorm_g, a_w_out, kv_norm_g, w_kv, b_norm_g, b_w_q, b_w_out, rel_bias, f_norm_g, f_w_up, f_conv_w, f_conv_b, f_w_down, final_norm_g, loss_target, m_a_norm_g, m_a_w_in, m_a_b_if, m_a_hnorm_g, m_a_w_out, m_kv_norm_g, m_w_kv, m_b_norm_g, m_b_w_q, m_b_w_out, m_rel_bias, m_f_norm_g, m_f_w_up, m_f_conv_w, m_f_conv_b, m_f_w_down, m_final_norm_g, v_a_norm_g, v_a_w_in, v_a_b_if, v_a_hnorm_g, v_a_w_out, v_kv_norm_g, v_w_kv, v_b_norm_g, v_b_w_q, v_b_w_out, v_rel_bias, v_f_norm_g, v_f_w_up, v_f_conv_w, v_f_conv_b, v_f_w_down, v_final_norm_g):
    given = dict(locals())
    shard = {n: given[n] for n in WEIGHTS}
    mom = {n: given["m_" + n] for n in WEIGHTS}
    var = {n: given["v_" + n] for n in WEIGHTS}
    cx, cy, cc = _me()
    chip = 2 * cx + cy

    overlap = Overlap(shard, cc)
    mine, gather = overlap.gather_ride('first')
    W = overlap.gathered('first', gather(mine, "gather_weights"))
    sharded_small = [n for n in SMALL if SHARD_AXIS[n] is not None]
    ssz = [shard[n].size for n in sharded_small]
    sflat = jnp.concatenate([shard[n].reshape(-1) for n in sharded_small])
    sg = group_gather(_pad_rows(sflat, 8), "gather_small", GROUP_CHIPS).reshape(4, -1)
    off = 0
    for n, sz in zip(sharded_small, ssz):
        W[n] = _full_from_shards(sg[:, off:off + sz].reshape((4,) + shard[n].shape), SHARD_AXIS[n])
        off += sz
    for n in SMALL:
        if SHARD_AXIS[n] is None:
            W[n] = shard[n]

    loss_row, grad_x, G = _local_step(x[0], loss_target[0], W, overlap)

    slots, scatter = overlap.scatter_ride('last', G)
    reduced, join = overlap.join_ride('last', scatter(slots, "scatter_grads"))
    overlap.joined('last', join(reduced, "join_halves"))
    by_piece = overlap.shard_grads
    gsh = {}
    for n in BIG:
        layers = [p for p in PIECES if PIECES[p][0] == n]
        gsh[n] = by_piece[n] if layers == [n] else jnp.stack([by_piece[p] for p in layers])
    small_parts = [loss_row[0, 0:1]] + [G[n].reshape(-1) for n in SMALL]
    small_sz = [p.shape[0] for p in small_parts]
    small = sum_slots(group_gather(_pad_rows(jnp.concatenate(small_parts), 8), "gather_small_grads", GROUP_ALL),
                      "sum_small_grads").reshape(-1)
    loss = small[0]
    off = 1
    for n, sz in zip(SMALL, small_sz[1:]):
        full = small[off:off + sz].reshape(W[n].shape)
        off += sz
        if SHARD_AXIS[n] is None:
            gsh[n] = full
        else:
            gsh[n] = lax.dynamic_index_in_dim(_shards_from_full(full, SHARD_AXIS[n]), chip, 0, keepdims=False)

    delta, new_m, new_v = {}, {}, {}
    for n in WEIGHTS:
        shp = shard[n].shape
        two = lambda a: a.reshape(-1, shp[-1])
        d, nm, nv = adamw(two(shard[n]), two(gsh[n]), two(mom[n]), two(var[n]), f"adamw_{n}")
        delta[n], new_m[n], new_v[n] = d.reshape(shp), nm.reshape(shp), nv.reshape(shp)
    return (loss, grad_x[None], *[gsh[n] for n in WEIGHTS], *[delta[n] for n in WEIGHTS],
            *[new_m[n] for n in WEIGHTS], *[new_v[n] for n in WEIGHTS])
```

```python
import functools
import math

import numpy as np
import jax
import jax.numpy as jnp
from jax import lax
from jax.experimental import pallas as pl
from jax.experimental.pallas import tpu as pltpu

F32 = jnp.float32
BF16 = jnp.bfloat16
MM_DTYPE = jnp.bfloat16
GRAD_WIRE_DTYPE = jnp.bfloat16
HI = lax.Precision.HIGHEST

D_MODEL = 1024
A_HEADS = 4
A_QK = 128
A_V = 256
A_CHUNK = 256
A_IN = 3080
A_IN_PAD = 3200
GATE_COL = 3072
SOFTCAP = 15.0
N_GROUPS = 3
B_HEADS = 16
B_DH = 64
B_BLOCK = 128
DILATIONS = (1, 4, 16)
WINDOWS = (128, 512, 2048)
REL_BUCKETS = 32
REL_MAX_DIST = 2048
D_FF = 2816
FF_TC = 256
EPS = 1e-6
ADAM_LR, ADAM_B1, ADAM_B2, ADAM_EPS, ADAM_WD, ADAM_STEP = 0.001, 0.9, 0.999, 1e-08, 0.01, 10

VMEM_LIMIT = 56 * 1024 * 1024
NT_DIMS = (((1,), (1,)), ((), ()))
TN_DIMS = (((0,), (0,)), ((), ()))
MESH_ID = pl.DeviceIdType.MESH


def _params(*sem):
    return pltpu.CompilerParams(dimension_semantics=sem, vmem_limit_bytes=VMEM_LIMIT)


def _tile(n, cap):
    if n <= cap:
        return n
    best = None
    for t in range(128, cap + 1, 128):
        if n % t == 0:
            best = t
    assert best is not None, (n, cap)
    return best


def _rows(n, cap):
    if n <= cap:
        return n
    for t in range(cap // 8 * 8, 7, -8):
        if n % t == 0:
            return t
    raise ValueError((n, cap))


def _dot(a, b):
    return jnp.dot(a.astype(MM_DTYPE), b.astype(MM_DTYPE), preferred_element_type=F32)


def _dot_nt(a, b):
    return lax.dot_general(a.astype(MM_DTYPE), b.astype(MM_DTYPE), NT_DIMS, preferred_element_type=F32)


def _dot_tn(a, b):
    return lax.dot_general(a.astype(MM_DTYPE), b.astype(MM_DTYPE), TN_DIMS, preferred_element_type=F32)


def _sigmoid(x):
    return 1.0 / (1.0 + jnp.exp(-x))


def _sigmoid_tanh(x):
    return 0.5 * jnp.tanh(0.5 * x) + 0.5


def mm_nn(a, b, name, res=None, out_dtype=F32, exact=False, ride=None):
    M, K = a.shape
    N = b.shape[1]
    def footprint(tm, tn):
        return 2 * (tm * K * a.dtype.itemsize + K * tn * b.dtype.itemsize) + 2 * tm * tn * 4 * (1 if res is None else 2)

    budget = 46 * 1024 * 1024
    tm = _rows(M, 512)
    tn = N if N <= 3328 and footprint(tm, N) <= budget else _tile(N, 1536)
    tk = K if footprint(tm, tn) <= budget else _tile(K, 1536)
    if tk == K and footprint(_rows(M, 1024), tn) <= budget:
        tm = _rows(M, 1024)
    nk = K // tk

    def body(*refs):
        if res is None:
            a_ref, b_ref, o_ref, acc = refs
            r_ref = None
        else:
            a_ref, b_ref, r_ref, o_ref, acc = refs
        if exact:
            p = jnp.dot(a_ref[...], b_ref[...], precision=HI, preferred_element_type=F32)
        else:
            p = _dot(a_ref[...], b_ref[...])

        def finish(total):
            if r_ref is not None:
                total = total + r_ref[...]
            o_ref[...] = total.astype(out_dtype)

        if nk == 1:
            finish(p)
        else:
            k = pl.program_id(2)

            @pl.when(k == 0)
            def _():
                acc[...] = p

            @pl.when(jnp.logical_and(k > 0, k < nk - 1))
            def _():
                acc[...] += p

            @pl.when(k == nk - 1)
            def _():
                finish(acc[...] + p)

    in_specs = [pl.BlockSpec((tm, tk), lambda j, i, k: (i, k)),
                pl.BlockSpec((tk, tn), lambda j, i, k: (k, j))]
    args = [a, b]
    if res is not None:
        in_specs.append(pl.BlockSpec((tm, tn), lambda j, i, k: (i, j)))
        args.append(res)
    acc_shape = (tm, tn) if nk > 1 else (8, 128)
    outs = _call(body, name, (N // tn, M // tm, nk), in_specs, [pl.BlockSpec((tm, tn), lambda j, i, k: (i, j))],
                 [jax.ShapeDtypeStruct((M, N), out_dtype)], [pltpu.VMEM(acc_shape, F32)],
                 ("parallel", "parallel", "arbitrary"), args, ride)
    return outs[0] if ride is None else outs


def mm_view(a, b, name, dil):
    T, K = a.shape
    tm = 1024

    def body(a_ref, b_ref, o_ref, sc):
        p = _dot(a_ref[...], b_ref[...])
        if dil == 1:
            o_ref[...] = p.astype(o_ref.dtype)
        else:
            _to_view(lambda c: p[:, c * 128:(c + 1) * 128], sc, o_ref, dil, 8, tm)

    return pl.pallas_call(
        body, name=name, grid=(T // tm,),
        in_specs=[pl.BlockSpec((tm, K), lambda i: (i, 0)), pl.BlockSpec((K, 1024), lambda i: (0, 0))],
        out_specs=pl.BlockSpec((tm // dil, dil * 1024), lambda i: (i, 0)),
        out_shape=jax.ShapeDtypeStruct((T // dil, dil * 1024), MM_DTYPE),
        scratch_shapes=[pltpu.VMEM((8, tm, 128), F32)],
        compiler_params=_params("parallel"),
    )(a, b)


def mm_tn(a, g, name, ride=None):
    T, Ka = a.shape
    N = g.shape[1]
    tka, tt = _tile(Ka, 1536), _rows(T, 1024)

    def footprint(tt, tn):
        return 2 * (tt * tka * a.dtype.itemsize + tt * tn * g.dtype.itemsize + tka * tn * 4)

    budget = 46 * 1024 * 1024
    tn = N if N <= 3328 and footprint(tt, N) <= budget else _tile(N, 1536)
    if footprint(_rows(T, 2048), tn) <= budget:
        tt = _rows(T, 2048)
    nt = T // tt

    def body(a_ref, g_ref, o_ref):
        t = pl.program_id(2)
        p = _dot_tn(a_ref[...], g_ref[...])

        @pl.when(t == 0)
        def _():
            o_ref[...] = p

        @pl.when(t > 0)
        def _():
            o_ref[...] += p

    outs = _call(body, name, (Ka // tka, N // tn, nt),
                 [pl.BlockSpec((tt, tka), lambda i, j, t: (t, i)), pl.BlockSpec((tt, tn), lambda i, j, t: (t, j))],
                 [pl.BlockSpec((tka, tn), lambda i, j, t: (i, j))], [jax.ShapeDtypeStruct((Ka, N), F32)], [],
                 ("parallel", "parallel", "arbitrary"), (a, g), ride)
    return outs[0] if ride is None else outs


def rms_fwd(x, gains, name):
    T, D = x.shape
    tt = _rows(T, 1024)
    ng = len(gains)

    def body(*refs):
        x_ref = refs[0]
        g_refs = refs[1:1 + ng]
        o_refs = refs[1 + ng:]
        xf = x_ref[...]
        y = xf * lax.rsqrt(jnp.mean(xf * xf, axis=-1, keepdims=True) + EPS)
        for g_ref, o_ref in zip(g_refs, o_refs):
            o_ref[...] = (y * g_ref[...]).astype(o_ref.dtype)

    row = pl.BlockSpec((tt, D), lambda i: (i, 0))
    gsp = pl.BlockSpec((1, D), lambda i: (0, 0))
    return pl.pallas_call(
        body, name=name, grid=(T // tt,),
        in_specs=[row] + [gsp] * ng, out_specs=[row] * ng,
        out_shape=[jax.ShapeDtypeStruct((T, D), MM_DTYPE)] * ng,
        compiler_params=_params("parallel"),
    )(x, *gains)


def rms_bwd(x, dres, branches, name):
    T, D = x.shape
    nb = len(branches)
    tt = _rows(T, 1024 if nb == 1 else 512)

    def body(*refs):
        x_ref, r_ref = refs[0], refs[1]
        dy_refs = refs[2:2 + nb]
        g_refs = refs[2 + nb:2 + 2 * nb]
        dx_ref = refs[2 + 2 * nb]
        dg_refs = refs[3 + 2 * nb:]
        i = pl.program_id(0)
        xf = x_ref[...]
        r = lax.rsqrt(jnp.mean(xf * xf, axis=-1, keepdims=True) + EPS)
        xh = xf * r
        dx = r_ref[...]
        for dy_ref, g_ref, dg_ref in zip(dy_refs, g_refs, dg_refs):
            dy = dy_ref[...].astype(F32)
            dyg = dy * g_ref[...]
            dx = dx + r * (dyg - xh * jnp.mean(dyg * xh, axis=-1, keepdims=True))
            part = jnp.sum(dy * xh, axis=0, keepdims=True)

            @pl.when(i == 0)
            def _():
                dg_ref[...] = part

            @pl.when(i > 0)
            def _():
                dg_ref[...] += part
        dx_ref[...] = dx

    row = pl.BlockSpec((tt, D), lambda i: (i, 0))
    gsp = pl.BlockSpec((1, D), lambda i: (0, 0))
    outs = pl.pallas_call(
        body, name=name, grid=(T // tt,),
        in_specs=[row, row] + [row] * nb + [gsp] * nb,
        out_specs=[row] + [gsp] * nb,
        out_shape=[jax.ShapeDtypeStruct((T, D), F32)] + [jax.ShapeDtypeStruct((1, D), F32)] * nb,
        compiler_params=_params("arbitrary"),
    )(x, dres, *[b[0] for b in branches], *[b[1] for b in branches])
    return outs[0], outs[1:]


def loss_head(x, target, gain):
    T, D = x.shape
    tt = _rows(T, 512)

    def body(x_ref, t_ref, g_ref, dx_ref, dg_ref, loss_ref):
        i = pl.program_id(0)
        xf = x_ref[...]
        g = g_ref[...]
        r = lax.rsqrt(jnp.mean(xf * xf, axis=-1, keepdims=True) + EPS)
        xh = xf * r
        e = xh * g - t_ref[...]
        lpart = 0.5 * jnp.sum(jnp.sum(e * e, axis=1, keepdims=True), axis=0, keepdims=True) / D
        dy = e / D
        dyg = dy * g
        dx_ref[...] = r * (dyg - xh * jnp.mean(dyg * xh, axis=-1, keepdims=True))
        gpart = jnp.sum(dy * xh, axis=0, keepdims=True)
        lrow = jnp.broadcast_to(lpart, (1, 128))

        @pl.when(i == 0)
        def _():
            dg_ref[...] = gpart
            loss_ref[...] = lrow

        @pl.when(i > 0)
        def _():
            dg_ref[...] += gpart
            loss_ref[...] += lrow

    row = pl.BlockSpec((tt, D), lambda i: (i, 0))
    gsp = pl.BlockSpec((1, D), lambda i: (0, 0))
    return pl.pallas_call(
        body, name="loss_head", grid=(T // tt,),
        in_specs=[row, row, gsp],
        out_specs=[row, gsp, pl.BlockSpec((1, 128), lambda i: (0, 0))],
        out_shape=[jax.ShapeDtypeStruct((T, D), F32), jax.ShapeDtypeStruct((1, D), F32),
                   jax.ShapeDtypeStruct((1, 128), F32)],
        compiler_params=_params("arbitrary"),
    )(x, target, gain)


def _shift_down(u, prev8, first, k):
    rolled = pltpu.roll(u, k, 0)
    rid = lax.broadcasted_iota(jnp.int32, u.shape, 0)
    halo = jnp.where(first, 0.0, prev8)
    out = rolled
    for j in range(k):
        out = jnp.where(rid == j, halo[8 - k + j:8 - k + j + 1, :], out)
    return out


def _conv3(u, prev8, first, w, b):
    return (_shift_down(u, prev8, first, 2) * w[0:1, :] + _shift_down(u, prev8, first, 1) * w[1:2, :]
            + u * w[2:3, :] + b)


def ffn_up_act(xn, w_up, w, b, name, ride=None):
    T, K = xn.shape
    tt = _rows(T, 2048)
    nj = D_FF // FF_TC

    def body(x_ref, wu_ref, w_ref, b_ref, u_ref, o_ref, tail):
        first = pl.program_id(1) == 0
        u = _dot(x_ref[...], wu_ref[...])
        u_ref[...] = u
        c = _conv3(u, tail[...], first, w_ref[...], b_ref[...])
        tail[...] = u[tt - 8:, :]
        cg, cv = c[:, :FF_TC], c[:, FF_TC:]
        o_ref[...] = (cg * _sigmoid_tanh(cg) * cv).astype(o_ref.dtype)

    return _call(
        body, name, (nj, T // tt),
        [pl.BlockSpec((tt, K), lambda j, i: (i, 0)),
         pl.BlockSpec((K, 2 * FF_TC), lambda j, i: (0, j)),
         pl.BlockSpec((3, 2 * FF_TC), lambda j, i: (0, j)),
         pl.BlockSpec((1, 2 * FF_TC), lambda j, i: (0, j))],
        [pl.BlockSpec((tt, 2 * FF_TC), lambda j, i: (i, j)), pl.BlockSpec((tt, FF_TC), lambda j, i: (i, j))],
        [jax.ShapeDtypeStruct((T, 2 * D_FF), F32), jax.ShapeDtypeStruct((T, D_FF), MM_DTYPE)],
        [pltpu.VMEM((8, 2 * FF_TC), F32)], ("parallel", "arbitrary"), (xn, w_up, w, b), ride)


def conv_act_bwd(u, da, w, b, name, ride=None):
    T = u.shape[0]
    tt = _rows(T, 2048)
    nt = T // tt
    nj = D_FF // FF_TC
    te = tt + 8

    def body(u_ref, p_ref, n_ref, da_ref, dan_ref, w_ref, b_ref, du_ref, dw_ref, db_ref):
        i = pl.program_id(1)
        first = i == 0
        last = i == nt - 1
        w = w_ref[...]
        ue = jnp.concatenate([u_ref[...], n_ref[...]], axis=0)
        dae = jnp.concatenate([da_ref[...], jnp.where(last, 0.0, dan_ref[...])], axis=0)
        um2 = _shift_down(ue, p_ref[...], first, 2)
        um1 = _shift_down(ue, p_ref[...], first, 1)
        c = um2 * w[0:1, :] + um1 * w[1:2, :] + ue * w[2:3, :] + b_ref[...]
        cg, cv = c[:, :FF_TC], c[:, FF_TC:]
        s = _sigmoid_tanh(cg)
        dcg = dae * cv * (s * (1.0 + cg * (1.0 - s)))
        dcv = dae * (cg * s)
        dc = jnp.concatenate([dcg, dcv], axis=1)
        du = (dc * w[2:3, :] + pltpu.roll(dc, te - 1, 0) * w[1:2, :] + pltpu.roll(dc, te - 2, 0) * w[0:1, :])
        du_ref[...] = du[:tt, :].astype(du_ref.dtype)
        dcm = dc[:tt, :]
        dwp = jnp.concatenate([jnp.sum(dcm * um2[:tt, :], axis=0, keepdims=True),
                               jnp.sum(dcm * um1[:tt, :], axis=0, keepdims=True),
                               jnp.sum(dcm * ue[:tt, :], axis=0, keepdims=True)], axis=0)
        dbp = jnp.sum(dcm, axis=0, keepdims=True)

        @pl.when(first)
        def _():
            dw_ref[...] = dwp
            db_ref[...] = dbp

        @pl.when(i > 0)
        def _():
            dw_ref[...] += dwp
            db_ref[...] += dbp

    nb8 = T // 8
    return _call(
        body, name, (nj, nt),
        [pl.BlockSpec((tt, 2 * FF_TC), lambda j, i: (i, j)),
         pl.BlockSpec((8, 2 * FF_TC), lambda j, i: (jnp.maximum(i * (tt // 8) - 1, 0), j)),
         pl.BlockSpec((8, 2 * FF_TC), lambda j, i: (jnp.minimum((i + 1) * (tt // 8), nb8 - 1), j)),
         pl.BlockSpec((tt, FF_TC), lambda j, i: (i, j)),
         pl.BlockSpec((8, FF_TC), lambda j, i: (jnp.minimum((i + 1) * (tt // 8), nb8 - 1), j)),
         pl.BlockSpec((3, 2 * FF_TC), lambda j, i: (0, j)),
         pl.BlockSpec((1, 2 * FF_TC), lambda j, i: (0, j))],
        [pl.BlockSpec((tt, 2 * FF_TC), lambda j, i: (i, j)),
         pl.BlockSpec((3, 2 * FF_TC), lambda j, i: (0, j)),
         pl.BlockSpec((1, 2 * FF_TC), lambda j, i: (0, j))],
        [jax.ShapeDtypeStruct((T, 2 * D_FF), MM_DTYPE), jax.ShapeDtypeStruct((3, 2 * D_FF), F32),
         jax.ShapeDtypeStruct((1, 2 * D_FF), F32)],
        [], ("parallel", "arbitrary"), (u, u, u, da, da, w, b), ride)


def _interleave(a):
    lead = a.shape[:-1]
    nj = D_FF // FF_TC
    return jnp.swapaxes(a.reshape(*lead, 2, nj, FF_TC), -3, -2).reshape(*lead, 2 * D_FF)


def _deinterleave(a):
    lead = a.shape[:-1]
    nj = D_FF // FF_TC
    return jnp.swapaxes(a.reshape(*lead, nj, 2, FF_TC), -3, -2).reshape(*lead, 2 * D_FF)


A_GC = 1
A_TB = A_GC * A_CHUNK


def gate_prep(z, bias128):
    T = z.shape[0]
    tt = _rows(T, 512)

    def body(z_ref, b_ref, gc_ref, gr_ref):
        pre = z_ref[...] + b_ref[...]
        sc = SOFTCAP * jnp.tanh(pre / SOFTCAP)
        lf = jnp.minimum(sc, 0.0) - jnp.log(1.0 + jnp.exp(-jnp.abs(sc)))
        col = lax.broadcasted_iota(jnp.int32, pre.shape, 1)
        isf = jnp.logical_and(col >= A_HEADS, col < 2 * A_HEADS)
        r = lax.broadcasted_iota(jnp.int32, (tt, tt), 0)
        c = lax.broadcasted_iota(jnp.int32, (tt, tt), 1)
        bits = A_CHUNK.bit_length() - 1
        tri = jnp.logical_and(jnp.right_shift(r, bits) == jnp.right_shift(c, bits), c <= r).astype(F32)
        bcum = jnp.dot(tri, jnp.where(isf, lf, 0.0), precision=HI, preferred_element_type=F32)
        g = jnp.where(col < A_HEADS, sc, jnp.where(isf, bcum, 0.0))
        gc_ref[...] = g
        for s in range(tt // 128):
            gr_ref[s] = g[s * 128:(s + 1) * 128, :].T[0:8, :]

    return pl.pallas_call(
        body, name="gate_prep", grid=(T // tt,),
        in_specs=[pl.BlockSpec((tt, 128), lambda i: (i, GATE_COL // 128)),
                  pl.BlockSpec((1, 128), lambda i: (0, 0))],
        out_specs=[pl.BlockSpec((tt, 128), lambda i: (i, 0)),
                   pl.BlockSpec((tt // 128, 8, 128), lambda i: (i, 0, 0))],
        out_shape=[jax.ShapeDtypeStruct((T, 128), F32), jax.ShapeDtypeStruct((T // 128, 8, 128), F32)],
        compiler_params=_params("parallel"),
    )(z, bias128)


def _chunk_decay(A, qh, bc, br, lir, n, m, causal):
    logD = jnp.where(causal, bc - br + lir, -jnp.inf)
    m_inter = bc + m
    m_t = jnp.maximum(m_inter, jnp.max(logD, axis=1, keepdims=True))
    E = jnp.exp(logD - m_t)
    Sm = A * E
    wi = jnp.exp(m_inter - m_t)
    qn = jnp.sum(qh.astype(F32) * n, axis=1, keepdims=True)
    den = jnp.sum(Sm, axis=1, keepdims=True) + wi * qn
    gs = jnp.maximum(jnp.abs(den), jnp.exp(-m_t))
    return E, Sm, wi, den, gs, m_t


def _state_weights(bc, lic, br, lir, m):
    bL = bc[A_CHUNK - 1:A_CHUNK, :]
    m_new = jnp.maximum(bL + m, jnp.max(bL - br + lir, axis=1, keepdims=True))
    wk = jnp.exp(bL - bc + lic - m_new)
    decay = jnp.exp(bL + m - m_new)
    return wk, decay, m_new


def _head_slices(h):
    return (slice(h * A_QK, (h + 1) * A_QK), slice(h * A_V, (h + 1) * A_V))


def mlstm_fwd(z, gcol, grow, hng, ride=None):
    T = z.shape[0]
    NC = T // A_CHUNK
    scale = A_QK ** -0.5

    def body(q_ref, k_ref, v_ref, o_ref, gc_ref, gr_ref, hng_ref, hg_ref, Cs_ref, ns_ref, ms_ref,
             C_sc, n_sc, m_sc):
        @pl.when(pl.program_id(0) == 0)
        def _():
            C_sc[...] = jnp.zeros_like(C_sc)
            n_sc[...] = jnp.zeros_like(n_sc)
            m_sc[...] = jnp.zeros_like(m_sc)

        ri = lax.broadcasted_iota(jnp.int32, (A_CHUNK, A_CHUNK), 0)
        ci = lax.broadcasted_iota(jnp.int32, (A_CHUNK, A_CHUNK), 1)
        causal = ri >= ci
        gr = jnp.concatenate([gr_ref[s] for s in range(A_TB // 128)], axis=1)
        for c in range(A_GC):
            rows = slice(c * A_CHUNK, (c + 1) * A_CHUNK)
            gc = gc_ref[rows, :]
            grc = gr[:, c * A_CHUNK:(c + 1) * A_CHUNK]
            for h in range(A_HEADS):
                sk, sv = _head_slices(h)
                qh = (q_ref[rows, sk] * scale).astype(MM_DTYPE)
                kh = k_ref[rows, sk].astype(MM_DTYPE)
                vh = v_ref[rows, sv].astype(MM_DTYPE)
                lic, bc = gc[:, h:h + 1], gc[:, A_HEADS + h:A_HEADS + h + 1]
                lir, br = grc[h:h + 1, :], grc[A_HEADS + h:A_HEADS + h + 1, :]
                C, n, m = C_sc[h], n_sc[h], m_sc[h][:, 0:1]
                Cs_ref[c, h] = C
                ns_ref[c, h] = n
                ms_ref[c, h] = m_sc[h]
                _, Sm, wi, _, gs, _ = _chunk_decay(_dot_nt(qh, kh), qh, bc, br, lir, n, m, causal)
                hh = (_dot(Sm, vh) + wi * _dot(qh, C)) / gs
                hn = hh * lax.rsqrt(jnp.mean(hh * hh, axis=1, keepdims=True) + EPS) * hng_ref[:, sv]
                hg_ref[rows, sv] = (hn * _sigmoid(o_ref[rows, sv])).astype(hg_ref.dtype)
                wk, decay, m_new = _state_weights(bc, lic, br, lir, m)
                kw = kh.astype(F32) * wk
                C_sc[h] = decay * C + _dot_tn(kw, vh)
                n_sc[h] = decay * n + jnp.sum(kw, axis=0, keepdims=True)
                m_sc[h] = jnp.broadcast_to(m_new, (1, 128))

    tok = lambda w, cb: pl.BlockSpec((A_TB, w), lambda i: (i, cb))
    return _call(
        body, "mlstm_fwd", (NC // A_GC,),
        [tok(512, 0), tok(512, 1), tok(1024, 1), tok(1024, 2),
         pl.BlockSpec((A_TB, 128), lambda i: (i, 0)),
         pl.BlockSpec((A_TB // 128, 8, 128), lambda i: (i, 0, 0)),
         pl.BlockSpec((1, 1024), lambda i: (0, 0))],
        [pl.BlockSpec((A_TB, 1024), lambda i: (i, 0)),
         pl.BlockSpec((A_GC, A_HEADS, A_QK, A_V), lambda i: (i, 0, 0, 0)),
         pl.BlockSpec((A_GC, A_HEADS, 1, 128), lambda i: (i, 0, 0, 0)),
         pl.BlockSpec((A_GC, A_HEADS, 1, 128), lambda i: (i, 0, 0, 0))],
        [jax.ShapeDtypeStruct((T, 1024), MM_DTYPE),
         jax.ShapeDtypeStruct((NC, A_HEADS, A_QK, A_V), F32),
         jax.ShapeDtypeStruct((NC, A_HEADS, 1, 128), F32),
         jax.ShapeDtypeStruct((NC, A_HEADS, 1, 128), F32)],
        [pltpu.VMEM((A_HEADS, A_QK, A_V), F32), pltpu.VMEM((A_HEADS, 1, 128), F32),
         pltpu.VMEM((A_HEADS, 1, 128), F32)],
        ("arbitrary",), (z, z, z, z, gcol, grow, hng), ride)


def mlstm_bwd(z, gcol, grow, hng, bias128, Cs, ns, ms, dhg, ride=None):
    T = z.shape[0]
    NC = T // A_CHUNK
    nsteps = NC // A_GC
    scale = A_QK ** -0.5

    def body(q_ref, k_ref, v_ref, o_ref, zg_ref, gc_ref, gr_ref, hng_ref, b_ref, Cs_ref, ns_ref, ms_ref,
             dhg_ref, dz_ref, dgn_ref, dbif_ref, dC_sc, dn_sc):
        @pl.when(pl.program_id(0) == 0)
        def _():
            dC_sc[...] = jnp.zeros_like(dC_sc)
            dn_sc[...] = jnp.zeros_like(dn_sc)
            dgn_ref[...] = jnp.zeros_like(dgn_ref)
            dbif_ref[...] = jnp.zeros_like(dbif_ref)

        ri = lax.broadcasted_iota(jnp.int32, (A_CHUNK, A_CHUNK), 0)
        ci = lax.broadcasted_iota(jnp.int32, (A_CHUNK, A_CHUNK), 1)
        causal = ri >= ci
        upper = (ci >= ri).astype(F32)
        rid = lax.broadcasted_iota(jnp.int32, (A_CHUNK, 1), 0)
        col = lax.broadcasted_iota(jnp.int32, (A_CHUNK, 128), 1)
        gr = jnp.concatenate([gr_ref[s] for s in range(A_TB // 128)], axis=1)
        for c in reversed(range(A_GC)):
            rows = slice(c * A_CHUNK, (c + 1) * A_CHUNK)
            gc = gc_ref[rows, :]
            grc = gr[:, c * A_CHUNK:(c + 1) * A_CHUNK]
            dG = jnp.zeros((A_CHUNK, 128), F32)
            hs = []
            for h in range(A_HEADS):
                sk, sv = _head_slices(h)
                s = dict(sk=sk, sv=sv, qh=(q_ref[rows, sk] * scale).astype(MM_DTYPE),
                         kh=k_ref[rows, sk].astype(MM_DTYPE), vh=v_ref[rows, sv].astype(MM_DTYPE),
                         lic=gc[:, h:h + 1], bc=gc[:, A_HEADS + h:A_HEADS + h + 1],
                         lir=grc[h:h + 1, :], br=grc[A_HEADS + h:A_HEADS + h + 1, :],
                         C=Cs_ref[c, h], n=ns_ref[c, h], m=ms_ref[c, h][:, 0:1], dC=dC_sc[h], dn=dn_sc[h])
                s['qf'], s['kf'] = s['qh'].astype(F32), s['kh'].astype(F32)
                s['wk'], s['decay'], _ = _state_weights(s['bc'], s['lic'], s['br'], s['lir'], s['m'])
                hs.append(s)
            for s in hs:
                s['A'] = _dot_nt(s['qh'], s['kh'])
                s['qC'] = _dot(s['qh'], s['C'])
                s['vdC'] = _dot_nt(s['vh'], s['dC'])
                s['kdC'] = _dot(s['kh'], s['dC'])
            for s in hs:
                s['E'], s['Sm'], s['wi'], s['den'], s['gs'], s['m_t'] = _chunk_decay(
                    s['A'], s['qh'], s['bc'], s['br'], s['lir'], s['n'], s['m'], causal)
            for s in hs:
                s['num'] = _dot(s['Sm'], s['vh']) + s['wi'] * s['qC']
            for h, s in enumerate(hs):
                sv, gs = s['sv'], s['gs']
                hh = s['num'] / gs
                r = lax.rsqrt(jnp.mean(hh * hh, axis=1, keepdims=True) + EPS)
                gn = hng_ref[:, sv]
                sg = _sigmoid(o_ref[rows, sv])
                dhg_h = dhg_ref[rows, sv]
                dhn = dhg_h * sg
                dz_ref[rows, 2048 + h * A_V:2048 + (h + 1) * A_V] = (
                    dhg_h * (hh * r * gn) * sg * (1.0 - sg)).astype(dz_ref.dtype)
                dgn_ref[:, sv] += jnp.sum(dhn * hh * r, axis=0, keepdims=True)
                dyg = dhn * gn
                dh = r * dyg - hh * (r * r * r) * jnp.mean(dyg * hh, axis=1, keepdims=True)
                s['dnum'] = dh / gs
                live = (jnp.abs(s['den']) > jnp.exp(-s['m_t'])).astype(F32)
                s['dden'] = -jnp.sum(dh * hh, axis=1, keepdims=True) / gs * jnp.sign(s['den']) * live
            for s in hs:
                s['dnv'] = _dot_nt(s['dnum'], s['vh'])
                s['dnC'] = _dot_nt(s['dnum'], s['C'])
            for s in hs:
                s['dSE'] = jnp.where(causal, s['dnv'] + s['dden'], 0.0) * s['E']
            for s in hs:
                s['dq'] = _dot(s['dSE'], s['kh']) + s['wi'] * (s['dnC'] + s['dden'] * s['n'])
                s['dk_inter'] = s['wk'] * (s['vdC'] + s['dn'])
                s['dk'] = _dot_tn(s['dSE'], s['qh']) + s['dk_inter']
                s['dv'] = _dot_tn(s['Sm'], s['dnum']) + s['wk'] * s['kdC']
                s['dCq'] = _dot_tn(s['qf'] * s['wi'], s['dnum'])
            for h, s in enumerate(hs):
                dq, dk, qf, kf, dC, dn = s['dq'], s['dk'], s['qf'], s['kf'], s['dC'], s['dn']
                dz_ref[rows, s['sk']] = (dq * scale).astype(dz_ref.dtype)
                dz_ref[rows, 512 + h * A_QK:512 + (h + 1) * A_QK] = dk.astype(dz_ref.dtype)
                dz_ref[rows, 1024 + h * A_V:1024 + (h + 1) * A_V] = s['dv'].astype(dz_ref.dtype)
                dli = jnp.sum(kf * dk, axis=1, keepdims=True)
                db = jnp.sum(qf * dq, axis=1, keepdims=True) - dli
                usum = jnp.sum(jnp.sum(kf * s['dk_inter'], axis=1, keepdims=True), axis=0, keepdims=True)
                ddecay = (jnp.sum(jnp.sum(dC * s['C'], axis=1, keepdims=True), axis=0, keepdims=True)
                          + jnp.sum(dn * s['n'], axis=1, keepdims=True))
                db = db + jnp.where(rid == A_CHUNK - 1, usum + ddecay * s['decay'], 0.0)
                dG = dG + jnp.where(col == h, dli, 0.0) + jnp.where(col == A_HEADS + h, db, 0.0)
                dC_sc[h] = s['decay'] * dC + s['dCq']
                dn_sc[h] = s['decay'] * dn + jnp.sum(qf * (s['wi'] * s['dden']), axis=0, keepdims=True)
            dlf = jnp.dot(upper, dG, precision=HI, preferred_element_type=F32)
            pre = zg_ref[rows, :] + b_ref[...]
            th = jnp.tanh(pre / SOFTCAP)
            dcap = 1.0 - th * th
            dpre = jnp.where(col < A_HEADS, dG * dcap,
                             jnp.where(col < 2 * A_HEADS, dlf * _sigmoid(-SOFTCAP * th) * dcap, 0.0))
            dz_ref[rows, GATE_COL:GATE_COL + 128] = dpre.astype(dz_ref.dtype)
            dbif_ref[...] += jnp.sum(dpre, axis=0, keepdims=True)

    rev = lambda i: nsteps - 1 - i
    tok = lambda w, cb: pl.BlockSpec((A_TB, w), lambda i: (rev(i), cb))
    st = lambda a, b: pl.BlockSpec((A_GC, A_HEADS, a, b), lambda i: (rev(i), 0, 0, 0))
    return _call(
        body, "mlstm_bwd", (nsteps,),
        [tok(512, 0), tok(512, 1), tok(1024, 1), tok(1024, 2), tok(128, GATE_COL // 128),
         pl.BlockSpec((A_TB, 128), lambda i: (rev(i), 0)),
         pl.BlockSpec((A_TB // 128, 8, 128), lambda i: (rev(i), 0, 0)),
         pl.BlockSpec((1, 1024), lambda i: (0, 0)),
         pl.BlockSpec((1, 128), lambda i: (0, 0)),
         st(A_QK, A_V), st(1, 128), st(1, 128),
         pl.BlockSpec((A_TB, 1024), lambda i: (rev(i), 0))],
        [pl.BlockSpec((A_TB, A_IN_PAD), lambda i: (rev(i), 0)),
         pl.BlockSpec((1, 1024), lambda i: (0, 0)),
         pl.BlockSpec((1, 128), lambda i: (0, 0))],
        [jax.ShapeDtypeStruct((T, A_IN_PAD), MM_DTYPE), jax.ShapeDtypeStruct((1, 1024), F32),
         jax.ShapeDtypeStruct((1, 128), F32)],
        [pltpu.VMEM((A_HEADS, A_QK, A_V), F32), pltpu.VMEM((A_HEADS, 1, 128), F32)],
        ("arbitrary",), (z, z, z, z, z, gcol, grow, hng, bias128, Cs, ns, ms, dhg), ride)


def _t5_bucket(dist):
    max_exact = REL_BUCKETS // 2
    d = np.maximum(dist, 0)
    log_ratio = np.log(np.maximum(d, 1) / max_exact) / math.log(REL_MAX_DIST / max_exact)
    large = np.minimum(max_exact + (log_ratio * (REL_BUCKETS - max_exact)).astype(np.int64), REL_BUCKETS - 1)
    return np.where(d < max_exact, d, large).astype(np.int32)


def _group_bucket(g):
    delta = B_BLOCK + np.arange(B_BLOCK)[:, None] - np.arange(2 * B_BLOCK)[None, :]
    return _t5_bucket(delta * DILATIONS[g])


def _band_mask(n):
    ri = lax.broadcasted_iota(jnp.int32, (B_BLOCK, 2 * B_BLOCK), 0)
    ci = lax.broadcasted_iota(jnp.int32, (B_BLOCK, 2 * B_BLOCK), 1)
    band = jnp.logical_and(ci >= ri, ci <= ri + B_BLOCK)
    return jnp.logical_and(band, jnp.logical_or(ci >= B_BLOCK, n > 0))


def _both(p_ref, c_ref, sl):
    return jnp.concatenate([p_ref[:, sl], c_ref[:, sl]], axis=0)


def _scores(qh, kh, bias_h, valid):
    return jnp.where(valid, _dot_nt(qh, kh) * (B_DH ** -0.5) + bias_h, -jnp.inf)


def _attn_specs():
    wide = pl.BlockSpec((B_BLOCK, 1024), lambda r, n: (n, r))
    prev = pl.BlockSpec((B_BLOCK, 1024), lambda r, n: (jnp.maximum(n - 1, 0), r))
    narrow = pl.BlockSpec((B_BLOCK, 128), lambda r, n: (n, r))
    bias = pl.BlockSpec((B_HEADS, B_BLOCK, 2 * B_BLOCK), lambda r, n: (0, 0, 0))
    return wide, prev, narrow, bias


def _to_view(read_chunk, sc, o_ref, dil, nc, tt):
    for c in range(nc):
        sc[c] = read_chunk(c)
    for r in range(dil):
        for c in range(nc):
            lo = (r * nc + c) * 128
            o_ref[:, lo:lo + 128] = sc[c, pl.ds(r, tt // dil, stride=dil), :].astype(o_ref.dtype)


def _from_view(read_view, sc, dil, nc, tt):
    for r in range(dil):
        for c in range(nc):
            sc[c, pl.ds(r, tt // dil, stride=dil), :] = read_view((r * nc + c) * 128).astype(F32)


def attn_fwd(qv, kvw, vvw, bias, g):
    dil = DILATIONS[g]
    Tv = qv.shape[0]
    nb = Tv // B_BLOCK
    wide, prev, narrow, bsp = _attn_specs()

    def body(q_ref, kp_ref, kc_ref, vp_ref, vc_ref, b_ref, o_ref, lse_ref):
        valid = _band_mask(pl.program_id(1))
        lse_ref[...] = jnp.zeros_like(lse_ref)
        heads = [slice(h * B_DH, (h + 1) * B_DH) for h in range(B_HEADS)]
        S = [_scores(q_ref[:, sl], _both(kp_ref, kc_ref, sl), b_ref[h], valid) for h, sl in enumerate(heads)]
        P, L = [], []
        for h in range(B_HEADS):
            m = jnp.max(S[h], axis=1, keepdims=True)
            p = jnp.exp(S[h] - m)
            l = jnp.sum(p, axis=1, keepdims=True)
            lse_ref[:, h:h + 1] = m + jnp.log(l)
            P.append(p.astype(MM_DTYPE))
            L.append(l)
        for h, sl in enumerate(heads):
            o_ref[:, sl] = _dot(P[h], _both(vp_ref, vc_ref, sl)) / L[h]

    return pl.pallas_call(
        body, name=f"attn_fwd_g{g}", grid=(dil, nb),
        in_specs=[wide, prev, wide, prev, wide, bsp], out_specs=[wide, narrow],
        out_shape=[jax.ShapeDtypeStruct((Tv, dil * 1024), F32), jax.ShapeDtypeStruct((Tv, dil * 128), F32)],
        compiler_params=_params("parallel", "parallel"),
    )(qv, kvw, kvw, vvw, vvw, bias)


def attn_bwd(qv, kvw, vvw, bias, do_v, lse_v, dl_v, g):
    dil = DILATIONS[g]
    Tv = qv.shape[0]
    nb = Tv // B_BLOCK
    wide, prev, narrow, bsp = _attn_specs()

    def body(q_ref, kp_ref, kc_ref, vp_ref, vc_ref, b_ref, bt_ref, do_ref, lse_ref, dl_ref,
             dq_ref, dkc_ref, dkp_ref, dvc_ref, dvp_ref, db_ref):
        @pl.when(jnp.logical_and(pl.program_id(0) == 0, pl.program_id(1) == 0))
        def _():
            db_ref[...] = jnp.zeros_like(db_ref)

        n = pl.program_id(1)
        valid = _band_mask(n)
        ki = lax.broadcasted_iota(jnp.int32, (2 * B_BLOCK, B_BLOCK), 0)
        qi = lax.broadcasted_iota(jnp.int32, (2 * B_BLOCK, B_BLOCK), 1)
        valid_t = jnp.logical_and(jnp.logical_and(ki >= qi, ki <= qi + B_BLOCK), jnp.logical_or(ki >= B_BLOCK, n > 0))
        lse_t, dl_t = lse_ref[...].T, dl_ref[...].T
        heads = [slice(h * B_DH, (h + 1) * B_DH) for h in range(B_HEADS)]
        scale = B_DH ** -0.5
        PT, DS, DST = [], [], []
        for h, sl in enumerate(heads):
            qh, doh = q_ref[:, sl], do_ref[:, sl].astype(MM_DTYPE)
            kh, vh = _both(kp_ref, kc_ref, sl), _both(vp_ref, vc_ref, sl)
            p = jnp.exp(_scores(qh, kh, b_ref[h], valid) - lse_ref[:, h:h + 1])
            ds = p * (_dot_nt(doh, vh) - dl_ref[:, h:h + 1])
            db_ref[h] += ds
            DS.append((ds * scale).astype(MM_DTYPE))
            pt = jnp.exp(_scores(kh, qh, bt_ref[h], valid_t) - lse_t[h:h + 1, :])
            PT.append(pt.astype(MM_DTYPE))
            DST.append((pt * (_dot_nt(vh, doh) - dl_t[h:h + 1, :]) * scale).astype(MM_DTYPE))
        for h, sl in enumerate(heads):
            qh, doh = q_ref[:, sl], do_ref[:, sl].astype(MM_DTYPE)
            dq_ref[:, sl] = _dot(DS[h], _both(kp_ref, kc_ref, sl)).astype(MM_DTYPE)
            dk = _dot(DST[h], qh).astype(MM_DTYPE)
            dv = _dot(PT[h], doh).astype(MM_DTYPE)
            dkp_ref[:, sl], dkc_ref[:, sl] = dk[:B_BLOCK], dk[B_BLOCK:]
            dvp_ref[:, sl], dvc_ref[:, sl] = dv[:B_BLOCK], dv[B_BLOCK:]

    big = jax.ShapeDtypeStruct((Tv, dil * 1024), MM_DTYPE)
    bsp_t = pl.BlockSpec((B_HEADS, 2 * B_BLOCK, B_BLOCK), lambda r, n: (0, 0, 0))
    return pl.pallas_call(
        body, name=f"attn_bwd_g{g}", grid=(dil, nb),
        in_specs=[wide, prev, wide, prev, wide, bsp, bsp_t, wide, narrow, narrow],
        out_specs=[wide] * 5 + [bsp],
        out_shape=[big] * 5 + [jax.ShapeDtypeStruct((B_HEADS, B_BLOCK, 2 * B_BLOCK), F32)],
        compiler_params=_params("arbitrary", "arbitrary"),
    )(qv, kvw, kvw, vvw, vvw, bias, jnp.swapaxes(bias, 1, 2), do_v, lse_v, dl_v)


def _head_expand():
    e = np.zeros((128, 1024), np.float32)
    for h in range(B_HEADS):
        e[h, h * B_DH:(h + 1) * B_DH] = 1.0
    return e


A_TT = 256
A_TT_WIDE = 512


def _view_spec(dil, width, tt=A_TT):
    return pl.BlockSpec((tt // dil, dil * width), lambda i: (i, 0))


def attn_merge(os_v, lses_v):
    T = os_v[0].shape[0]
    tt = A_TT_WIDE
    expand = jnp.asarray(_head_expand())

    def body(o0, o1, o2, l0, l1, l2, e_ref, ob_ref, of_ref, lse0_ref, lse1_ref, lse2_ref, sc_o, sc_l):
        for gi, (o_ref, l_ref) in enumerate(((o1, l1), (o2, l2))):
            dil = DILATIONS[gi + 1]
            _from_view(lambda lo: o_ref[:, lo:lo + 128], sc_o.at[gi], dil, 8, tt)
            _from_view(lambda lo: l_ref[:, lo:lo + 128], sc_l.at[gi], dil, 1, tt)
        ls = [l0[...], sc_l[0, 0], sc_l[1, 0]]
        m = jnp.maximum(jnp.maximum(ls[0], ls[1]), ls[2])
        ex = [jnp.exp(l - m) for l in ls]
        tot = ex[0] + ex[1] + ex[2]
        lse = m + jnp.log(tot)
        lse0_ref[...] = lse
        _to_view(lambda c: lse, sc_l.at[2], lse1_ref, DILATIONS[1], 1, tt)
        _to_view(lambda c: lse, sc_l.at[2], lse2_ref, DILATIONS[2], 1, tt)
        ws = [e / tot for e in ex]
        for c in range(8):
            cols = slice(c * 128, (c + 1) * 128)
            ecol = e_ref[:, cols]
            spread = [jnp.dot(w, ecol, precision=HI, preferred_element_type=F32) for w in ws]
            out = spread[0] * o0[:, cols] + spread[1] * sc_o[0, c] + spread[2] * sc_o[1, c]
            of_ref[:, cols] = out
            ob_ref[:, cols] = out.astype(ob_ref.dtype)

    wide = pl.BlockSpec((tt, 1024), lambda i: (i, 0))
    return pl.pallas_call(
        body, name="attn_merge", grid=(T // tt,),
        in_specs=[_view_spec(d, 1024, tt) for d in DILATIONS] + [_view_spec(d, 128, tt) for d in DILATIONS]
        + [pl.BlockSpec((128, 1024), lambda i: (0, 0))],
        out_specs=[wide, wide] + [_view_spec(d, 128, tt) for d in DILATIONS],
        out_shape=[jax.ShapeDtypeStruct((T, 1024), MM_DTYPE), jax.ShapeDtypeStruct((T, 1024), F32)]
        + [jax.ShapeDtypeStruct((T // d, d * 128), F32) for d in DILATIONS],
        scratch_shapes=[pltpu.VMEM((2, 8, tt, 128), F32), pltpu.VMEM((3, 1, tt, 128), F32)],
        compiler_params=_params("parallel"),
    )(*os_v, *lses_v, expand)


def attn_prep(datt, out):
    T = datt.shape[0]
    tt = A_TT_WIDE
    expand_t = jnp.asarray(_head_expand().T.copy())

    def body(d_ref, o_ref, e_ref, do0, do1, do2, dl0, dl1, dl2, sc_d, sc_l):
        delta = jnp.dot(d_ref[...] * o_ref[...], e_ref[...], precision=HI, preferred_element_type=F32)
        do0[...] = d_ref[...].astype(do0.dtype)
        dl0[...] = delta
        for do_ref, dl_ref, dil in ((do1, dl1, DILATIONS[1]), (do2, dl2, DILATIONS[2])):
            _to_view(lambda c: d_ref[:, c * 128:(c + 1) * 128], sc_d, do_ref, dil, 8, tt)
            _to_view(lambda c: delta, sc_l, dl_ref, dil, 1, tt)

    wide = pl.BlockSpec((tt, 1024), lambda i: (i, 0))
    return pl.pallas_call(
        body, name="attn_prep", grid=(T // tt,),
        in_specs=[wide, wide, pl.BlockSpec((1024, 128), lambda i: (0, 0))],
        out_specs=[_view_spec(d, 1024, tt) for d in DILATIONS] + [_view_spec(d, 128, tt) for d in DILATIONS],
        out_shape=[jax.ShapeDtypeStruct((T // d, d * 1024), MM_DTYPE) for d in DILATIONS]
        + [jax.ShapeDtypeStruct((T // d, d * 128), F32) for d in DILATIONS],
        scratch_shapes=[pltpu.VMEM((8, tt, 128), F32), pltpu.VMEM((1, tt, 128), F32)],
        compiler_params=_params("parallel"),
    )(datt, out, expand_t)


def attn_combine(parts):
    T = parts[0][0].shape[0]
    tt = A_TT
    nt = T // tt
    shift = [None] + [B_BLOCK * d // tt for d in DILATIONS[1:]]

    def body(dq0, kc0, vc0, kpa0, kpb0, vpa0, vpb0, dq1, kc1, kp1, vc1, vp1, dq2, kc2, kp2, vc2, vp2,
             dq_ref, dkv_ref, sc):
        i = pl.program_id(0)
        dq_ref[:, 0:1024] = dq0[...].astype(dq_ref.dtype)
        for col, c_ref, pa_ref, pb_ref in ((0, kc0, kpa0, kpb0), (3, vc0, vpa0, vpb0)):
            nxt = jnp.where(i + 1 < nt, pb_ref[:tt // 2, :].astype(F32), 0.0)
            later = jnp.concatenate([pa_ref[tt // 2:, :].astype(F32), nxt], axis=0)
            dkv_ref[:, col * 1024:(col + 1) * 1024] = (c_ref[...].astype(F32) + later).astype(dkv_ref.dtype)
        for g, (dq, kc, kp, vc, vp) in ((1, (dq1, kc1, kp1, vc1, vp1)), (2, (dq2, kc2, kp2, vc2, vp2))):
            dil = DILATIONS[g]
            live = i + shift[g] < nt
            _from_view(lambda lo: dq[:, lo:lo + 128], sc, dil, 8, tt)
            for c in range(8):
                dq_ref[:, g * 1024 + c * 128:g * 1024 + (c + 1) * 128] = sc[c].astype(dq_ref.dtype)
            for col, c_ref, p_ref in ((g, kc, kp), (3 + g, vc, vp)):
                _from_view(lambda lo: c_ref[:, lo:lo + 128].astype(F32)
                           + jnp.where(live, p_ref[:, lo:lo + 128].astype(F32), 0.0), sc, dil, 8, tt)
                for c in range(8):
                    dkv_ref[:, col * 1024 + c * 128:col * 1024 + (c + 1) * 128] = sc[c].astype(dkv_ref.dtype)

    def later_spec(dil, blocks):
        return pl.BlockSpec((tt // dil, dil * 1024), lambda i: (jnp.minimum(i + blocks, nt - 1), 0))

    cur = [_view_spec(d, 1024) for d in DILATIONS]
    in_specs = [cur[0], cur[0], cur[0], cur[0], later_spec(1, 1), cur[0], later_spec(1, 1)]
    args = [parts[0][0], parts[0][1], parts[0][3], parts[0][2], parts[0][2], parts[0][4], parts[0][4]]
    for g in (1, 2):
        in_specs += [cur[g], cur[g], later_spec(DILATIONS[g], shift[g]), cur[g], later_spec(DILATIONS[g], shift[g])]
        args += list(parts[g][:5])
    return pl.pallas_call(
        body, name="attn_combine", grid=(nt,), in_specs=in_specs,
        out_specs=[pl.BlockSpec((tt, 3072), lambda i: (i, 0)), pl.BlockSpec((tt, 6144), lambda i: (i, 0))],
        out_shape=[jax.ShapeDtypeStruct((T, 3072), MM_DTYPE), jax.ShapeDtypeStruct((T, 6144), MM_DTYPE)],
        scratch_shapes=[pltpu.VMEM((8, tt, 128), F32)],
        compiler_params=_params("parallel"),
    )(*args)


def adamw(w, g, m, v, name):
    R, C = w.shape
    tr = R if R * C * 4 <= (1 << 20) else _rows(R, max(8, ((1 << 20) // (C * 4)) // 8 * 8))

    def body(w_ref, g_ref, m_ref, v_ref, d_ref, nm_ref, nv_ref):
        gg = g_ref[...]
        nm = ADAM_B1 * m_ref[...] + (1.0 - ADAM_B1) * gg
        nv = ADAM_B2 * v_ref[...] + (1.0 - ADAM_B2) * (gg * gg)
        m_hat = nm / (1.0 - ADAM_B1 ** ADAM_STEP)
        v_hat = nv / (1.0 - ADAM_B2 ** ADAM_STEP)
        d_ref[...] = -ADAM_LR * (m_hat / (jnp.sqrt(v_hat) + ADAM_EPS) + ADAM_WD * w_ref[...])
        nm_ref[...] = nm
        nv_ref[...] = nv

    blk = pl.BlockSpec((tr, C), lambda i: (i, 0))
    sds = jax.ShapeDtypeStruct((R, C), F32)
    return pl.pallas_call(
        body, name=name, grid=(R // tr,), in_specs=[blk] * 4, out_specs=[blk] * 3, out_shape=[sds] * 3,
        compiler_params=_params("parallel"),
    )(w, g, m, v)


def sum_slots(x, name, out_dtype=F32):
    n, R, C = x.shape
    tr = _rows(R, 256)

    def body(x_ref, o_ref):
        acc = x_ref[0].astype(F32)
        for s in range(1, n):
            acc = acc + x_ref[s].astype(F32)
        o_ref[...] = acc.astype(out_dtype)

    return pl.pallas_call(
        body, name=name, grid=(R // tr,),
        in_specs=[pl.BlockSpec((n, tr, C), lambda i: (0, i, 0))],
        out_specs=pl.BlockSpec((tr, C), lambda i: (i, 0)),
        out_shape=jax.ShapeDtypeStruct((R, C), out_dtype),
        compiler_params=_params("parallel"),
    )(x)


_ANY = pl.BlockSpec(memory_space=pl.ANY)
GROUP_ALL = ([(0, 0, 1), (0, 1, 0), (0, 1, 1), (1, 0, 0), (1, 0, 1), (1, 1, 0), (1, 1, 1)],
             lambda d: 4 * d[0] + 2 * d[1] + d[2])
GROUP_CHIPS = ([(0, 1, 0), (1, 0, 0), (1, 1, 0)], lambda d: 2 * d[0] + d[1])
GROUP_SIBLING = ([(0, 0, 1)], lambda d: d[2])


def _me():
    return lax.axis_index("x"), lax.axis_index("y"), lax.axis_index("c")


def _peer(me, flip):
    return tuple(1 - a if f else a for a, f in zip(me, flip))


class Exchange:
    def __init__(self, x, group, scatter):
        self.flips, self.slot = group
        self.scatter = scatter
        self.n = len(self.flips) + 1
        self.out_shape = jax.ShapeDtypeStruct((self.n,) + x.shape[-2:], x.dtype)
        self.scratch = [pltpu.SemaphoreType.DMA((self.n - 1,)), pltpu.SemaphoreType.DMA((self.n - 1,)),
                        pltpu.SemaphoreType.DMA]

    def _copies(self, x_ref, o_ref, send_sems, recv_sems, local_sem, arrivals):
        me = _me()
        slot = self.slot
        mine = pltpu.make_async_copy(x_ref.at[slot(me)] if self.scatter else x_ref, o_ref.at[slot(me)], local_sem)
        sends, landed = [], []
        for k, flip in enumerate(self.flips):
            peer = _peer(me, flip)
            sends.append(pltpu.make_async_remote_copy(
                src_ref=x_ref.at[slot(peer)] if self.scatter else x_ref, dst_ref=o_ref.at[slot(me)],
                send_sem=send_sems.at[k], recv_sem=recv_sems.at[k], device_id=peer, device_id_type=MESH_ID))
            if arrivals:
                landed.append(pltpu.make_async_remote_copy(
                    src_ref=o_ref.at[slot(me)], dst_ref=o_ref.at[slot(peer)], send_sem=send_sems.at[k],
                    recv_sem=recv_sems.at[k], device_id=peer, device_id_type=MESH_ID))
        return mine, sends, landed

    def start(self, *refs):
        mine, sends, _ = self._copies(*refs, arrivals=False)
        mine.start()
        for cp in sends:
            cp.start()

    def wait(self, *refs):
        mine, sends, arrivals = self._copies(*refs, arrivals=True)
        for cp in arrivals:
            cp.wait_recv()
        for cp in sends:
            cp.wait_send()
        mine.wait()

    def __call__(self, x, name):
        def body(*refs):
            self.start(*refs)
            self.wait(*refs)

        return pl.pallas_call(body, name=name, in_specs=[_ANY], out_specs=_ANY, out_shape=self.out_shape,
                              scratch_shapes=self.scratch)(x)


def group_gather(x, name, group):
    return Exchange(x, group, scatter=False)(x, name)


def group_scatter(x, name, group):
    return Exchange(x, group, scatter=True)(x, name)


def _call(body, name, grid, in_specs, out_specs, out_shape, scratch, semantics, args, ride=None):
    if ride is None:
        return pl.pallas_call(body, name=name, grid=grid, in_specs=in_specs, out_specs=out_specs,
                              out_shape=out_shape, scratch_shapes=scratch,
                              compiler_params=_params(*semantics))(*args)
    x, exch = ride
    n_in, n_out, n_scr = len(in_specs), len(out_specs), len(scratch)

    def at_step(pick):
        hit = None
        for axis, size in enumerate(grid):
            here = pl.program_id(axis) == pick(size)
            hit = here if hit is None else jnp.logical_and(hit, here)
        return hit

    def riding(*refs):
        ins, x_ref = refs[:n_in], refs[n_in]
        outs, o_ref = refs[n_in + 1:n_in + 1 + n_out], refs[n_in + 1 + n_out]
        scr, sems = refs[n_in + 2 + n_out:n_in + 2 + n_out + n_scr], refs[n_in + 2 + n_out + n_scr:]

        @pl.when(at_step(lambda size: 0))
        def _():
            exch.start(x_ref, o_ref, *sems)

        body(*ins, *outs, *scr)

        @pl.when(at_step(lambda size: size - 1))
        def _():
            exch.wait(x_ref, o_ref, *sems)

    return pl.pallas_call(
        riding, name=name, grid=grid, in_specs=list(in_specs) + [_ANY], out_specs=list(out_specs) + [_ANY],
        out_shape=list(out_shape) + [exch.out_shape], scratch_shapes=list(scratch) + exch.scratch,
        compiler_params=_params(*(["arbitrary"] * len(grid))))(*args, x)


WEIGHTS = ['a_norm_g', 'a_w_in', 'a_b_if', 'a_hnorm_g', 'a_w_out', 'kv_norm_g', 'w_kv', 'b_norm_g', 'b_w_q',
           'b_w_out', 'rel_bias', 'f_norm_g', 'f_w_up', 'f_conv_w', 'f_conv_b', 'f_w_down', 'final_norm_g']
SHARD_AXIS = {'a_norm_g': 1, 'a_w_in': 2, 'a_b_if': None, 'a_hnorm_g': 2, 'a_w_out': 1, 'kv_norm_g': None,
              'w_kv': 1, 'b_norm_g': None, 'b_w_q': 2, 'b_w_out': 1, 'rel_bias': None, 'f_norm_g': None,
              'f_w_up': 2, 'f_conv_w': 2, 'f_conv_b': None, 'f_w_down': 1, 'final_norm_g': None}
BIG = ['a_w_in', 'a_w_out', 'w_kv', 'b_w_q', 'b_w_out', 'f_w_up', 'f_w_down']
SMALL = [n for n in WEIGHTS if n not in BIG]
LANES = 1024
PIECES = {'a_w_in': ('a_w_in', None, 2), 'a_w_out': ('a_w_out', None, 1), 'f_w_up0': ('f_w_up', 0, 1),
          'f_w_down0': ('f_w_down', 0, 0), 'w_kv': ('w_kv', None, 1), 'b_w_q': ('b_w_q', None, 2),
          'b_w_out': ('b_w_out', None, 1), 'f_w_up1': ('f_w_up', 1, 1), 'f_w_down1': ('f_w_down', 1, 0)}
LATE = ['w_kv', 'b_w_q', 'b_w_out', 'f_w_up1', 'f_w_down1']
WEIGHT_WAVES = {'first': ['a_w_in'], 'ffn0': ['a_w_out', 'f_w_up0', 'f_w_down0'], 'late': LATE}
GRAD_WAVES = {'late': LATE, 'layer0': ['f_w_up0', 'f_w_down0', 'a_w_out'], 'last': ['a_w_in']}


def _piece(arrays, p):
    leaf, layer, _ = PIECES[p]
    return arrays[leaf] if layer is None else arrays[leaf][layer]


class Packer:
    def __init__(self, pieces, shard):
        self.pieces = pieces
        self.shapes = [_piece(shard, p).shape for p in pieces]
        self.sizes = [math.prod(s) // (2 * LANES) for s in self.shapes]
        self.fill = -sum(self.sizes) % 16
        self.rows = sum(self.sizes) + self.fill

    def my_half(self, shard, half):
        both = jnp.concatenate([_piece(shard, p).astype(MM_DTYPE).reshape(2, -1, LANES) for p in self.pieces], axis=1)
        return jnp.pad(lax.dynamic_index_in_dim(both, half, axis=0, keepdims=False), ((0, self.fill), (0, 0)))

    def full_weights(self, gathered):
        g = gathered.reshape(4, 2, self.rows, LANES)
        out, off = {}, 0
        for p, shp, sz in zip(self.pieces, self.shapes, self.sizes):
            out[p] = _full_from_shards(g[:, :, off:off + sz].reshape((4,) + shp), PIECES[p][2])
            off += sz
        return out

    def grad_slots(self, grads):
        parts = [_shards_from_full(grads[p], PIECES[p][2]).reshape(4, 2, -1, LANES).astype(GRAD_WIRE_DTYPE)
                 for p in self.pieces]
        parts.append(jnp.zeros((4, 2, self.fill, LANES), GRAD_WIRE_DTYPE))
        return jnp.concatenate(parts, axis=2).reshape(8, self.rows, LANES)

    def shard_grads(self, both):
        out, off = {}, 0
        for p, shp, sz in zip(self.pieces, self.shapes, self.sizes):
            out[p] = both[:, off:off + sz].reshape(shp).astype(F32)
            off += sz
        return out


class Overlap:
    def __init__(self, shard, half):
        self.shard, self.half = shard, half
        self.weights = {w: Packer(p, shard) for w, p in WEIGHT_WAVES.items()}
        self.grads = {w: Packer(p, shard) for w, p in GRAD_WAVES.items()}
        self.shard_grads = {}

    def gather_ride(self, wave):
        mine = self.weights[wave].my_half(self.shard, self.half)
        return mine, Exchange(mine, GROUP_ALL, scatter=False)

    def gathered(self, wave, slots):
        return self.weights[wave].full_weights(slots)

    def scatter_ride(self, wave, grads):
        slots = self.grads[wave].grad_slots({p: grads.pop(p) for p in GRAD_WAVES[wave]})
        return slots, Exchange(slots, GROUP_ALL, scatter=True)

    def join_ride(self, wave, received):
        reduced = sum_slots(received, f"sum_grads_{wave}", GRAD_WIRE_DTYPE)
        return reduced, Exchange(reduced, GROUP_SIBLING, scatter=False)

    def joined(self, wave, both):
        self.shard_grads.update(self.grads[wave].shard_grads(both))


def _pad_rows(flat, mult):
    n = flat.shape[0]
    per = LANES * mult
    tot = -(-n // per) * per
    return jnp.pad(flat, (0, tot - n)).reshape(tot // LANES, LANES)


def _full_from_shards(sh, axis):
    shp = sh.shape[1:]
    return jnp.moveaxis(sh, 0, axis).reshape(shp[:axis] + (4 * shp[axis],) + shp[axis + 1:])


def _shards_from_full(full, axis):
    shp = full.shape
    return jnp.moveaxis(full.reshape(shp[:axis] + (4, shp[axis] // 4) + shp[axis + 1:]), axis, 0)


def _local_step(x, target, W, overlap=None):
    T = x.shape[0]
    W = dict(W)
    row = lambda a: a.reshape(1, -1).astype(F32)
    w_in = jnp.pad(W['a_w_in'][0], ((0, 0), (0, A_IN_PAD - A_IN)))
    bias128 = jnp.pad(row(W['a_b_if'][0]), ((0, 0), (0, 120)))
    hng = row(W['a_hnorm_g'][0])
    w_up = lambda l: _interleave(W[f'f_w_up{l}'])
    cw = [_interleave(W['f_conv_w'][l].astype(F32)) for l in range(2)]
    cb = [_interleave(row(W['f_conv_b'][l])) for l in range(2)]
    onehots = [(jnp.asarray(_group_bucket(g).reshape(-1, 1)) == jnp.arange(128)[None, :]).astype(F32)
               for g in range(N_GROUPS)]
    rb_t = jnp.pad(W['rel_bias'].astype(F32).T, ((0, 0), (0, 128 - REL_BUCKETS)))
    biases = [mm_nn(rb_t[g * B_HEADS:(g + 1) * B_HEADS], onehots[g].T, f"rel_bias_table_g{g}", exact=True)
              .reshape(B_HEADS, B_BLOCK, 2 * B_BLOCK) for g in range(N_GROUPS)]
    G = {}

    def ffn_fwd(xin, l, ride=None):
        xn, = rms_fwd(xin, [row(W['f_norm_g'][l])], f"ffn{l}_norm")
        u, act, *rode = ffn_up_act(xn, w_up(l), cw[l], cb[l], f"ffn{l}_up_act", ride)
        return mm_nn(act, W[f'f_w_down{l}'], f"ffn{l}_down", res=xin), (xn, u, act), rode

    def ffn_bwd(xin, saved, dout, l, ride=None):
        xn, u, act = saved
        dact = mm_nn(dout, W[f'f_w_down{l}'].T, f"ffn{l}_ddown")
        G[f'f_w_down{l}'] = mm_tn(act, dout, f"ffn{l}_gdown")
        du, gcw, gcb, *rode = conv_act_bwd(u, dact, cw[l], cb[l], f"ffn{l}_dact", ride)
        dxn = mm_nn(du, w_up(l).T, f"ffn{l}_dup")
        G[f'f_w_up{l}'] = _deinterleave(mm_tn(xn, du, f"ffn{l}_gup"))
        dxin, (gn,) = rms_bwd(xin, dout, [(dxn, row(W['f_norm_g'][l]))], f"ffn{l}_dnorm")
        return dxin, _deinterleave(gcw), _deinterleave(gcb), gn, rode

    xn_a, = rms_fwd(x, [row(W['a_norm_g'][0])], "a_norm")
    z = mm_nn(xn_a, w_in, "a_in")
    gcol, grow = gate_prep(z, bias128)
    hg, Cs, ns, ms, *rode = mlstm_fwd(z, gcol, grow, hng, overlap.gather_ride('ffn0') if overlap else None)
    if overlap:
        W.update(overlap.gathered('ffn0', rode[0]))
    x1 = mm_nn(hg, W['a_w_out'][0], "a_out", res=x)
    x2, ffn0, rode = ffn_fwd(x1, 0, overlap.gather_ride('late') if overlap else None)
    if overlap:
        W.update(overlap.gathered('late', rode[0]))
    xn_kv, xn_b = rms_fwd(x2, [row(W['kv_norm_g']), row(W['b_norm_g'][0])], "b_norms")
    gcols = lambda w, c: w[:, c * 1024:(c + 1) * 1024]
    qv = [mm_view(xn_b, gcols(W['b_w_q'][0], g), f"q_proj_g{g}", DILATIONS[g]) for g in range(N_GROUPS)]
    kvw = [mm_view(xn_kv, gcols(W['w_kv'], g), f"k_proj_g{g}", DILATIONS[g]) for g in range(N_GROUPS)]
    vvw = [mm_view(xn_kv, gcols(W['w_kv'], 3 + g), f"v_proj_g{g}", DILATIONS[g]) for g in range(N_GROUPS)]
    os_, lses = zip(*[attn_fwd(qv[g], kvw[g], vvw[g], biases[g], g) for g in range(N_GROUPS)])
    att, att_f, *lse_v = attn_merge(os_, lses)
    x3 = mm_nn(att, W['b_w_out'][0], "b_out", res=x2)
    x4, ffn1, _ = ffn_fwd(x3, 1)
    dx4, g_final, loss = loss_head(x4, target, row(W['final_norm_g']))
    G['final_norm_g'] = g_final.reshape(-1)

    dx3, gcw1, gcb1, gn1, _ = ffn_bwd(x3, ffn1, dx4, 1)
    datt = mm_nn(dx3, W['b_w_out'][0].T, "b_dout")
    G['b_w_out'] = mm_tn(att, dx3, "b_gout")[None]
    prep = attn_prep(datt, att_f)
    do_v, dl_v = prep[:3], prep[3:]
    parts = [attn_bwd(qv[g], kvw[g], vvw[g], biases[g], do_v[g], lse_v[g], dl_v[g], g) for g in range(N_GROUPS)]
    dq_all, dkv = attn_combine(parts)
    grb = []
    for g in range(N_GROUPS):
        gb = mm_nn(parts[g][5].reshape(B_HEADS, -1), onehots[g], f"rel_bias_g{g}", exact=True)
        grb.append(gb[:, :REL_BUCKETS].T)
    G['rel_bias'] = jnp.concatenate(grb, axis=1)
    dxn_b = mm_nn(dq_all, W['b_w_q'][0].T, "q_dproj")
    G['b_w_q'] = mm_tn(xn_b, dq_all, "q_gproj")[None]
    dxn_kv = mm_nn(dkv, W['w_kv'].T, "kv_dproj")
    G['w_kv'] = mm_tn(xn_kv, dkv, "kv_gproj")
    dx2, (g_kvn, g_bn) = rms_bwd(x2, dx3, [(dxn_kv, row(W['kv_norm_g'])), (dxn_b, row(W['b_norm_g'][0]))],
                                 "b_dnorms")
    G['kv_norm_g'] = g_kvn.reshape(-1)
    G['b_norm_g'] = g_bn
    dx1, gcw0, gcb0, gn0, late_slots = ffn_bwd(x1, ffn0, dx2, 0, overlap.scatter_ride('late', G) if overlap else None)
    G['f_conv_w'] = jnp.stack([gcw0, gcw1])
    G['f_conv_b'] = jnp.concatenate([gcb0, gcb1], axis=0)
    G['f_norm_g'] = jnp.concatenate([gn0, gn1], axis=0)
    dhg = mm_nn(dx1, W['a_w_out'][0].T, "a_dout")
    G['a_w_out'] = mm_tn(hg, dx1, "a_gout")[None]
    dz, g_hn, g_bif, *layer0_slots = mlstm_bwd(z, gcol, grow, hng, bias128, Cs, ns, ms, dhg,
                                               overlap.scatter_ride('layer0', G) if overlap else None)
    G['a_hnorm_g'] = g_hn.reshape(1, A_HEADS, A_V)
    G['a_b_if'] = g_bif[:, :2 * A_HEADS]
    if overlap:
        dxn_a, both = mm_nn(dz, w_in.T, "a_din", ride=overlap.join_ride('late', late_slots[0]))
        overlap.joined('late', both)
        g_in, both = mm_tn(xn_a, dz, "a_gin", ride=overlap.join_ride('layer0', layer0_slots[0]))
        overlap.joined('layer0', both)
    else:
        dxn_a = mm_nn(dz, w_in.T, "a_din")
        g_in = mm_tn(xn_a, dz, "a_gin")
    G['a_w_in'] = g_in[:, :A_IN][None]
    grad_x, (g_an,) = rms_bwd(x, dx1, [(dxn_a, row(W['a_norm_g'][0]))], "a_dnorm")
    G['a_norm_g'] = g_an
    return loss, grad_x, G


def kernel(x, a_norm_g, a_w_in, a_b_if, a_hnorm_g, a_w_out, kv_norm_g, w_kv, b_norm_g, b_w_q, b_w_out, rel_bias, f_norm_g, f_w_up, f_conv_w, f_conv_b, f_w_down, final_norm_g, loss_target, m_a_norm_g, m_a_w_in, m_a_b_if, m_a_hnorm_g, m_a_w_out, m_kv_norm_g, m_w_kv, m_b_norm_g, m_b_w_q, m_b_w_out, m_rel_bias, m_f_norm_g, m_f_w_up, m_f_conv_w, m_f_conv_b, m_f_w_down, m_final_norm_g, v_a_norm_g, v_a_w_in, v_a_b_if, v_a_hnorm_g, v_a_w_out, v_kv_norm_g, v_w_kv, v_b_norm_g, v_b_w_q, v_b_w_out, v_rel_bias, v_f_norm_g, v_f_w_up, v_f_conv_w, v_f_conv_b, v_f_w_down, v_final_norm_g):
    given = dict(locals())
    shard = {n: given[n] for n in WEIGHTS}
    mom = {n: given["m_" + n] for n in WEIGHTS}
    var = {n: given["v_" + n] for n in WEIGHTS}
    cx, cy, cc = _me()
    chip = 2 * cx + cy

    overlap = Overlap(shard, cc)
    mine, gather = overlap.gather_ride('first')
    W = overlap.gathered('first', gather(mine, "gather_weights"))
    sharded_small = [n for n in SMALL if SHARD_AXIS[n] is not None]
    ssz = [shard[n].size for n in sharded_small]
    sflat = jnp.concatenate([shard[n].reshape(-1) for n in sharded_small])
    sg = group_gather(_pad_rows(sflat, 8), "gather_small", GROUP_CHIPS).reshape(4, -1)
    off = 0
    for n, sz in zip(sharded_small, ssz):
        W[n] = _full_from_shards(sg[:, off:off + sz].reshape((4,) + shard[n].shape), SHARD_AXIS[n])
        off += sz
    for n in SMALL:
        if SHARD_AXIS[n] is None:
            W[n] = shard[n]

    loss_row, grad_x, G = _local_step(x[0], loss_target[0], W, overlap)

    slots, scatter = overlap.scatter_ride('last', G)
    reduced, join = overlap.join_ride('last', scatter(slots, "scatter_grads"))
    overlap.joined('last', join(reduced, "join_halves"))
    by_piece = overlap.shard_grads
    gsh = {}
    for n in BIG:
        layers = [p for p in PIECES if PIECES[p][0] == n]
        gsh[n] = by_piece[n] if layers == [n] else jnp.stack([by_piece[p] for p in layers])
    small_parts = [loss_row[0, 0:1]] + [G[n].reshape(-1) for n in SMALL]
    small_sz = [p.shape[0] for p in small_parts]
    small = sum_slots(group_gather(_pad_rows(jnp.concatenate(small_parts), 8), "gather_small_grads", GROUP_ALL),
                      "sum_small_grads").reshape(-1)
    loss = small[0]
    off = 1
    for n, sz in zip(SMALL, small_sz[1:]):
        full = small[off:off + sz].reshape(W[n].shape)
        off += sz
        if SHARD_AXIS[n] is None:
            gsh[n] = full
        else:
            gsh[n] = lax.dynamic_index_in_dim(_shards_from_full(full, SHARD_AXIS[n]), chip, 0, keepdims=False)

    delta, new_m, new_v = {}, {}, {}
    for n in WEIGHTS:
        shp = shard[n].shape
        two = lambda a: a.reshape(-1, shp[-1])
        d, nm, nv = adamw(two(shard[n]), two(gsh[n]), two(mom[n]), two(var[n]), f"adamw_{n}")
        delta[n], new_m[n], new_v[n] = d.reshape(shp), nm.reshape(shp), nv.reshape(shp)
    return (loss, grad_x[None], *[gsh[n] for n in WEIGHTS], *[delta[n] for n in WEIGHTS],
            *[new_m[n] for n in WEIGHTS], *[new_v[n] for n in WEIGHTS])
```

```python
import functools
import math

import numpy as np
import jax
import jax.numpy as jnp
from jax import lax
from jax.experimental import pallas as pl
from jax.experimental.pallas import tpu as pltpu

F32 = jnp.float32
BF16 = jnp.bfloat16
MM_DTYPE = jnp.bfloat16
GRAD_WIRE_DTYPE = jnp.bfloat16
HI = lax.Precision.HIGHEST

D_MODEL = 1024
A_HEADS = 4
A_QK = 128
A_V = 256
A_CHUNK = 512
A_IN = 3080
A_IN_PAD = 3200
GATE_COL = 3072
SOFTCAP = 15.0
N_GROUPS = 3
B_HEADS = 16
B_DH = 64
B_BLOCK = 128
DILATIONS = (1, 4, 16)
WINDOWS = (128, 512, 2048)
REL_BUCKETS = 32
REL_MAX_DIST = 2048
D_FF = 2816
FF_TC = 256
EPS = 1e-6
ADAM_LR, ADAM_B1, ADAM_B2, ADAM_EPS, ADAM_WD, ADAM_STEP = 0.001, 0.9, 0.999, 1e-08, 0.01, 10

VMEM_LIMIT = 56 * 1024 * 1024
NT_DIMS = (((1,), (1,)), ((), ()))
TN_DIMS = (((0,), (0,)), ((), ()))
MESH_ID = pl.DeviceIdType.MESH


def _params(*sem):
    return pltpu.CompilerParams(dimension_semantics=sem, vmem_limit_bytes=VMEM_LIMIT)


def _tile(n, cap):
    if n <= cap:
        return n
    best = None
    for t in range(128, cap + 1, 128):
        if n % t == 0:
            best = t
    assert best is not None, (n, cap)
    return best


def _rows(n, cap):
    if n <= cap:
        return n
    for t in range(cap // 8 * 8, 7, -8):
        if n % t == 0:
            return t
    raise ValueError((n, cap))


def _dot(a, b):
    return jnp.dot(a.astype(MM_DTYPE), b.astype(MM_DTYPE), preferred_element_type=F32)


def _dot_nt(a, b):
    return lax.dot_general(a.astype(MM_DTYPE), b.astype(MM_DTYPE), NT_DIMS, preferred_element_type=F32)


def _dot_tn(a, b):
    return lax.dot_general(a.astype(MM_DTYPE), b.astype(MM_DTYPE), TN_DIMS, preferred_element_type=F32)


def _sigmoid(x):
    return 1.0 / (1.0 + jnp.exp(-x))


def _sigmoid_tanh(x):
    return 0.5 * jnp.tanh(0.5 * x) + 0.5


def mm_nn(a, b, name, res=None, out_dtype=F32, exact=False, ride=None):
    M, K = a.shape
    N = b.shape[1]
    def footprint(tm, tn):
        return 2 * (tm * K * a.dtype.itemsize + K * tn * b.dtype.itemsize) + 2 * tm * tn * 4 * (1 if res is None else 2)

    budget = 46 * 1024 * 1024
    tm = _rows(M, 512)
    tn = N if N <= 3328 and footprint(tm, N) <= budget else _tile(N, 1536)
    tk = K if footprint(tm, tn) <= budget else _tile(K, 1536)
    if tk == K and footprint(_rows(M, 1024), tn) <= budget:
        tm = _rows(M, 1024)
    nk = K // tk

    def body(*refs):
        if res is None:
            a_ref, b_ref, o_ref, acc = refs
            r_ref = None
        else:
            a_ref, b_ref, r_ref, o_ref, acc = refs
        if exact:
            p = jnp.dot(a_ref[...], b_ref[...], precision=HI, preferred_element_type=F32)
        else:
            p = _dot(a_ref[...], b_ref[...])

        def finish(total):
            if r_ref is not None:
                total = total + r_ref[...]
            o_ref[...] = total.astype(out_dtype)

        if nk == 1:
            finish(p)
        else:
            k = pl.program_id(2)

            @pl.when(k == 0)
            def _():
                acc[...] = p

            @pl.when(jnp.logical_and(k > 0, k < nk - 1))
            def _():
                acc[...] += p

            @pl.when(k == nk - 1)
            def _():
                finish(acc[...] + p)

    in_specs = [pl.BlockSpec((tm, tk), lambda j, i, k: (i, k)),
                pl.BlockSpec((tk, tn), lambda j, i, k: (k, j))]
    args = [a, b]
    if res is not None:
        in_specs.append(pl.BlockSpec((tm, tn), lambda j, i, k: (i, j)))
        args.append(res)
    acc_shape = (tm, tn) if nk > 1 else (8, 128)
    outs = _call(body, name, (N // tn, M // tm, nk), in_specs, [pl.BlockSpec((tm, tn), lambda j, i, k: (i, j))],
                 [jax.ShapeDtypeStruct((M, N), out_dtype)], [pltpu.VMEM(acc_shape, F32)],
                 ("parallel", "parallel", "arbitrary"), args, ride)
    return outs[0] if ride is None else outs


def mm_view(a, b, name, dil):
    T, K = a.shape
    tm = 1024

    def body(a_ref, b_ref, o_ref, sc):
        p = _dot(a_ref[...], b_ref[...])
        if dil == 1:
            o_ref[...] = p.astype(o_ref.dtype)
        else:
            _to_view(lambda c: p[:, c * 128:(c + 1) * 128], sc, o_ref, dil, 8, tm)

    return pl.pallas_call(
        body, name=name, grid=(T // tm,),
        in_specs=[pl.BlockSpec((tm, K), lambda i: (i, 0)), pl.BlockSpec((K, 1024), lambda i: (0, 0))],
        out_specs=pl.BlockSpec((tm // dil, dil * 1024), lambda i: (i, 0)),
        out_shape=jax.ShapeDtypeStruct((T // dil, dil * 1024), MM_DTYPE),
        scratch_shapes=[pltpu.VMEM((8, tm, 128), F32)],
        compiler_params=_params("parallel"),
    )(a, b)


def mm_tn(a, g, name, ride=None):
    T, Ka = a.shape
    N = g.shape[1]
    tka, tt = _tile(Ka, 1536), _rows(T, 1024)

    def footprint(tt, tn):
        return 2 * (tt * tka * a.dtype.itemsize + tt * tn * g.dtype.itemsize + tka * tn * 4)

    budget = 46 * 1024 * 1024
    tn = N if N <= 3328 and footprint(tt, N) <= budget else _tile(N, 1536)
    if footprint(_rows(T, 2048), tn) <= budget:
        tt = _rows(T, 2048)
    nt = T // tt

    def body(a_ref, g_ref, o_ref):
        t = pl.program_id(2)
        p = _dot_tn(a_ref[...], g_ref[...])

        @pl.when(t == 0)
        def _():
            o_ref[...] = p

        @pl.when(t > 0)
        def _():
            o_ref[...] += p

    outs = _call(body, name, (Ka // tka, N // tn, nt),
                 [pl.BlockSpec((tt, tka), lambda i, j, t: (t, i)), pl.BlockSpec((tt, tn), lambda i, j, t: (t, j))],
                 [pl.BlockSpec((tka, tn), lambda i, j, t: (i, j))], [jax.ShapeDtypeStruct((Ka, N), F32)], [],
                 ("parallel", "parallel", "arbitrary"), (a, g), ride)
    return outs[0] if ride is None else outs


def rms_fwd(x, gains, name):
    T, D = x.shape
    tt = _rows(T, 1024)
    ng = len(gains)

    def body(*refs):
        x_ref = refs[0]
        g_refs = refs[1:1 + ng]
        o_refs = refs[1 + ng:]
        xf = x_ref[...]
        y = xf * lax.rsqrt(jnp.mean(xf * xf, axis=-1, keepdims=True) + EPS)
        for g_ref, o_ref in zip(g_refs, o_refs):
            o_ref[...] = (y * g_ref[...]).astype(o_ref.dtype)

    row = pl.BlockSpec((tt, D), lambda i: (i, 0))
    gsp = pl.BlockSpec((1, D), lambda i: (0, 0))
    return pl.pallas_call(
        body, name=name, grid=(T // tt,),
        in_specs=[row] + [gsp] * ng, out_specs=[row] * ng,
        out_shape=[jax.ShapeDtypeStruct((T, D), MM_DTYPE)] * ng,
        compiler_params=_params("parallel"),
    )(x, *gains)


def rms_bwd(x, dres, branches, name):
    T, D = x.shape
    nb = len(branches)
    tt = _rows(T, 1024 if nb == 1 else 512)

    def body(*refs):
        x_ref, r_ref = refs[0], refs[1]
        dy_refs = refs[2:2 + nb]
        g_refs = refs[2 + nb:2 + 2 * nb]
        dx_ref = refs[2 + 2 * nb]
        dg_refs = refs[3 + 2 * nb:]
        i = pl.program_id(0)
        xf = x_ref[...]
        r = lax.rsqrt(jnp.mean(xf * xf, axis=-1, keepdims=True) + EPS)
        xh = xf * r
        dx = r_ref[...]
        for dy_ref, g_ref, dg_ref in zip(dy_refs, g_refs, dg_refs):
            dy = dy_ref[...].astype(F32)
            dyg = dy * g_ref[...]
            dx = dx + r * (dyg - xh * jnp.mean(dyg * xh, axis=-1, keepdims=True))
            part = jnp.sum(dy * xh, axis=0, keepdims=True)

            @pl.when(i == 0)
            def _():
                dg_ref[...] = part

            @pl.when(i > 0)
            def _():
                dg_ref[...] += part
        dx_ref[...] = dx

    row = pl.BlockSpec((tt, D), lambda i: (i, 0))
    gsp = pl.BlockSpec((1, D), lambda i: (0, 0))
    outs = pl.pallas_call(
        body, name=name, grid=(T // tt,),
        in_specs=[row, row] + [row] * nb + [gsp] * nb,
        out_specs=[row] + [gsp] * nb,
        out_shape=[jax.ShapeDtypeStruct((T, D), F32)] + [jax.ShapeDtypeStruct((1, D), F32)] * nb,
        compiler_params=_params("arbitrary"),
    )(x, dres, *[b[0] for b in branches], *[b[1] for b in branches])
    return outs[0], outs[1:]


def loss_head(x, target, gain):
    T, D = x.shape
    tt = _rows(T, 512)

    def body(x_ref, t_ref, g_ref, dx_ref, dg_ref, loss_ref):
        i = pl.program_id(0)
        xf = x_ref[...]
        g = g_ref[...]
        r = lax.rsqrt(jnp.mean(xf * xf, axis=-1, keepdims=True) + EPS)
        xh = xf * r
        e = xh * g - t_ref[...]
        lpart = 0.5 * jnp.sum(jnp.sum(e * e, axis=1, keepdims=True), axis=0, keepdims=True) / D
        dy = e / D
        dyg = dy * g
        dx_ref[...] = r * (dyg - xh * jnp.mean(dyg * xh, axis=-1, keepdims=True))
        gpart = jnp.sum(dy * xh, axis=0, keepdims=True)
        lrow = jnp.broadcast_to(lpart, (1, 128))

        @pl.when(i == 0)
        def _():
            dg_ref[...] = gpart
            loss_ref[...] = lrow

        @pl.when(i > 0)
        def _():
            dg_ref[...] += gpart
            loss_ref[...] += lrow

    row = pl.BlockSpec((tt, D), lambda i: (i, 0))
    gsp = pl.BlockSpec((1, D), lambda i: (0, 0))
    return pl.pallas_call(
        body, name="loss_head", grid=(T // tt,),
        in_specs=[row, row, gsp],
        out_specs=[row, gsp, pl.BlockSpec((1, 128), lambda i: (0, 0))],
        out_shape=[jax.ShapeDtypeStruct((T, D), F32), jax.ShapeDtypeStruct((1, D), F32),
                   jax.ShapeDtypeStruct((1, 128), F32)],
        compiler_params=_params("arbitrary"),
    )(x, target, gain)


def _shift_down(u, prev8, first, k):
    rolled = pltpu.roll(u, k, 0)
    rid = lax.broadcasted_iota(jnp.int32, u.shape, 0)
    halo = jnp.where(first, 0.0, prev8)
    out = rolled
    for j in range(k):
        out = jnp.where(rid == j, halo[8 - k + j:8 - k + j + 1, :], out)
    return out


def _conv3(u, prev8, first, w, b):
    return (_shift_down(u, prev8, first, 2) * w[0:1, :] + _shift_down(u, prev8, first, 1) * w[1:2, :]
            + u * w[2:3, :] + b)


def ffn_up_act(xn, w_up, w, b, name, ride=None):
    T, K = xn.shape
    tt = _rows(T, 2048)
    nj = D_FF // FF_TC

    def body(x_ref, wu_ref, w_ref, b_ref, u_ref, o_ref, tail):
        first = pl.program_id(1) == 0
        u = _dot(x_ref[...], wu_ref[...])
        u_ref[...] = u
        c = _conv3(u, tail[...], first, w_ref[...], b_ref[...])
        tail[...] = u[tt - 8:, :]
        cg, cv = c[:, :FF_TC], c[:, FF_TC:]
        o_ref[...] = (cg * _sigmoid_tanh(cg) * cv).astype(o_ref.dtype)

    return _call(
        body, name, (nj, T // tt),
        [pl.BlockSpec((tt, K), lambda j, i: (i, 0)),
         pl.BlockSpec((K, 2 * FF_TC), lambda j, i: (0, j)),
         pl.BlockSpec((3, 2 * FF_TC), lambda j, i: (0, j)),
         pl.BlockSpec((1, 2 * FF_TC), lambda j, i: (0, j))],
        [pl.BlockSpec((tt, 2 * FF_TC), lambda j, i: (i, j)), pl.BlockSpec((tt, FF_TC), lambda j, i: (i, j))],
        [jax.ShapeDtypeStruct((T, 2 * D_FF), F32), jax.ShapeDtypeStruct((T, D_FF), MM_DTYPE)],
        [pltpu.VMEM((8, 2 * FF_TC), F32)], ("parallel", "arbitrary"), (xn, w_up, w, b), ride)


def conv_act_bwd(u, da, w, b, name, ride=None):
    T = u.shape[0]
    tt = _rows(T, 2048)
    nt = T // tt
    nj = D_FF // FF_TC
    te = tt + 8

    def body(u_ref, p_ref, n_ref, da_ref, dan_ref, w_ref, b_ref, du_ref, dw_ref, db_ref):
        i = pl.program_id(1)
        first = i == 0
        last = i == nt - 1
        w = w_ref[...]
        ue = jnp.concatenate([u_ref[...], n_ref[...]], axis=0)
        dae = jnp.concatenate([da_ref[...], jnp.where(last, 0.0, dan_ref[...])], axis=0)
        um2 = _shift_down(ue, p_ref[...], first, 2)
        um1 = _shift_down(ue, p_ref[...], first, 1)
        c = um2 * w[0:1, :] + um1 * w[1:2, :] + ue * w[2:3, :] + b_ref[...]
        cg, cv = c[:, :FF_TC], c[:, FF_TC:]
        s = _sigmoid_tanh(cg)
        dcg = dae * cv * (s * (1.0 + cg * (1.0 - s)))
        dcv = dae * (cg * s)
        dc = jnp.concatenate([dcg, dcv], axis=1)
        du = (dc * w[2:3, :] + pltpu.roll(dc, te - 1, 0) * w[1:2, :] + pltpu.roll(dc, te - 2, 0) * w[0:1, :])
        du_ref[...] = du[:tt, :].astype(du_ref.dtype)
        dcm = dc[:tt, :]
        dwp = jnp.concatenate([jnp.sum(dcm * um2[:tt, :], axis=0, keepdims=True),
                               jnp.sum(dcm * um1[:tt, :], axis=0, keepdims=True),
                               jnp.sum(dcm * ue[:tt, :], axis=0, keepdims=True)], axis=0)
        dbp = jnp.sum(dcm, axis=0, keepdims=True)

        @pl.when(first)
        def _():
            dw_ref[...] = dwp
            db_ref[...] = dbp

        @pl.when(i > 0)
        def _():
            dw_ref[...] += dwp
            db_ref[...] += dbp

    nb8 = T // 8
    return _call(
        body, name, (nj, nt),
        [pl.BlockSpec((tt, 2 * FF_TC), lambda j, i: (i, j)),
         pl.BlockSpec((8, 2 * FF_TC), lambda j, i: (jnp.maximum(i * (tt // 8) - 1, 0), j)),
         pl.BlockSpec((8, 2 * FF_TC), lambda j, i: (jnp.minimum((i + 1) * (tt // 8), nb8 - 1), j)),
         pl.BlockSpec((tt, FF_TC), lambda j, i: (i, j)),
         pl.BlockSpec((8, FF_TC), lambda j, i: (jnp.minimum((i + 1) * (tt // 8), nb8 - 1), j)),
         pl.BlockSpec((3, 2 * FF_TC), lambda j, i: (0, j)),
         pl.BlockSpec((1, 2 * FF_TC), lambda j, i: (0, j))],
        [pl.BlockSpec((tt, 2 * FF_TC), lambda j, i: (i, j)),
         pl.BlockSpec((3, 2 * FF_TC), lambda j, i: (0, j)),
         pl.BlockSpec((1, 2 * FF_TC), lambda j, i: (0, j))],
        [jax.ShapeDtypeStruct((T, 2 * D_FF), MM_DTYPE), jax.ShapeDtypeStruct((3, 2 * D_FF), F32),
         jax.ShapeDtypeStruct((1, 2 * D_FF), F32)],
        [], ("parallel", "arbitrary"), (u, u, u, da, da, w, b), ride)


def _interleave(a):
    lead = a.shape[:-1]
    nj = D_FF // FF_TC
    return jnp.swapaxes(a.reshape(*lead, 2, nj, FF_TC), -3, -2).reshape(*lead, 2 * D_FF)


def _deinterleave(a):
    lead = a.shape[:-1]
    nj = D_FF // FF_TC
    return jnp.swapaxes(a.reshape(*lead, nj, 2, FF_TC), -3, -2).reshape(*lead, 2 * D_FF)


A_GC = 1
A_TB = A_GC * A_CHUNK


def gate_prep(z, bias128):
    T = z.shape[0]
    tt = _rows(T, 512)

    def body(z_ref, b_ref, gc_ref, gr_ref):
        pre = z_ref[...] + b_ref[...]
        sc = SOFTCAP * jnp.tanh(pre / SOFTCAP)
        lf = jnp.minimum(sc, 0.0) - jnp.log(1.0 + jnp.exp(-jnp.abs(sc)))
        col = lax.broadcasted_iota(jnp.int32, pre.shape, 1)
        isf = jnp.logical_and(col >= A_HEADS, col < 2 * A_HEADS)
        r = lax.broadcasted_iota(jnp.int32, (tt, tt), 0)
        c = lax.broadcasted_iota(jnp.int32, (tt, tt), 1)
        bits = A_CHUNK.bit_length() - 1
        tri = jnp.logical_and(jnp.right_shift(r, bits) == jnp.right_shift(c, bits), c <= r).astype(F32)
        bcum = jnp.dot(tri, jnp.where(isf, lf, 0.0), precision=HI, preferred_element_type=F32)
        g = jnp.where(col < A_HEADS, sc, jnp.where(isf, bcum, 0.0))
        gc_ref[...] = g
        for s in range(tt // 128):
            gr_ref[s] = g[s * 128:(s + 1) * 128, :].T[0:8, :]

    return pl.pallas_call(
        body, name="gate_prep", grid=(T // tt,),
        in_specs=[pl.BlockSpec((tt, 128), lambda i: (i, GATE_COL // 128)),
                  pl.BlockSpec((1, 128), lambda i: (0, 0))],
        out_specs=[pl.BlockSpec((tt, 128), lambda i: (i, 0)),
                   pl.BlockSpec((tt // 128, 8, 128), lambda i: (i, 0, 0))],
        out_shape=[jax.ShapeDtypeStruct((T, 128), F32), jax.ShapeDtypeStruct((T // 128, 8, 128), F32)],
        compiler_params=_params("parallel"),
    )(z, bias128)


def _chunk_decay(A, qh, bc, br, lir, n, m, causal):
    logD = jnp.where(causal, bc - br + lir, -jnp.inf)
    m_inter = bc + m
    m_t = jnp.maximum(m_inter, jnp.max(logD, axis=1, keepdims=True))
    E = jnp.exp(logD - m_t)
    Sm = A * E
    wi = jnp.exp(m_inter - m_t)
    qn = jnp.sum(qh.astype(F32) * n, axis=1, keepdims=True)
    den = jnp.sum(Sm, axis=1, keepdims=True) + wi * qn
    gs = jnp.maximum(jnp.abs(den), jnp.exp(-m_t))
    return E, Sm, wi, den, gs, m_t


def _state_weights(bc, lic, br, lir, m):
    bL = bc[A_CHUNK - 1:A_CHUNK, :]
    m_new = jnp.maximum(bL + m, jnp.max(bL - br + lir, axis=1, keepdims=True))
    wk = jnp.exp(bL - bc + lic - m_new)
    decay = jnp.exp(bL + m - m_new)
    return wk, decay, m_new


def _head_slices(h):
    return (slice(h * A_QK, (h + 1) * A_QK), slice(h * A_V, (h + 1) * A_V))


def mlstm_fwd(z, gcol, grow, hng, ride=None):
    T = z.shape[0]
    NC = T // A_CHUNK
    scale = A_QK ** -0.5

    def body(q_ref, k_ref, v_ref, o_ref, gc_ref, gr_ref, hng_ref, hg_ref, Cs_ref, ns_ref, ms_ref,
             C_sc, n_sc, m_sc):
        @pl.when(pl.program_id(0) == 0)
        def _():
            C_sc[...] = jnp.zeros_like(C_sc)
            n_sc[...] = jnp.zeros_like(n_sc)
            m_sc[...] = jnp.zeros_like(m_sc)

        ri = lax.broadcasted_iota(jnp.int32, (A_CHUNK, A_CHUNK), 0)
        ci = lax.broadcasted_iota(jnp.int32, (A_CHUNK, A_CHUNK), 1)
        causal = ri >= ci
        gr = jnp.concatenate([gr_ref[s] for s in range(A_TB // 128)], axis=1)
        for c in range(A_GC):
            rows = slice(c * A_CHUNK, (c + 1) * A_CHUNK)
            gc = gc_ref[rows, :]
            grc = gr[:, c * A_CHUNK:(c + 1) * A_CHUNK]
            for h in range(A_HEADS):
                sk, sv = _head_slices(h)
                qh = (q_ref[rows, sk] * scale).astype(MM_DTYPE)
                kh = k_ref[rows, sk].astype(MM_DTYPE)
                vh = v_ref[rows, sv].astype(MM_DTYPE)
                lic, bc = gc[:, h:h + 1], gc[:, A_HEADS + h:A_HEADS + h + 1]
                lir, br = grc[h:h + 1, :], grc[A_HEADS + h:A_HEADS + h + 1, :]
                C, n, m = C_sc[h], n_sc[h], m_sc[h][:, 0:1]
                Cs_ref[c, h] = C
                ns_ref[c, h] = n
                ms_ref[c, h] = m_sc[h]
                _, Sm, wi, _, gs, _ = _chunk_decay(_dot_nt(qh, kh), qh, bc, br, lir, n, m, causal)
                hh = (_dot(Sm, vh) + wi * _dot(qh, C)) / gs
                hn = hh * lax.rsqrt(jnp.mean(hh * hh, axis=1, keepdims=True) + EPS) * hng_ref[:, sv]
                hg_ref[rows, sv] = (hn * _sigmoid(o_ref[rows, sv])).astype(hg_ref.dtype)
                wk, decay, m_new = _state_weights(bc, lic, br, lir, m)
                kw = kh.astype(F32) * wk
                C_sc[h] = decay * C + _dot_tn(kw, vh)
                n_sc[h] = decay * n + jnp.sum(kw, axis=0, keepdims=True)
                m_sc[h] = jnp.broadcast_to(m_new, (1, 128))

    tok = lambda w, cb: pl.BlockSpec((A_TB, w), lambda i: (i, cb))
    return _call(
        body, "mlstm_fwd", (NC // A_GC,),
        [tok(512, 0), tok(512, 1), tok(1024, 1), tok(1024, 2),
         pl.BlockSpec((A_TB, 128), lambda i: (i, 0)),
         pl.BlockSpec((A_TB // 128, 8, 128), lambda i: (i, 0, 0)),
         pl.BlockSpec((1, 1024), lambda i: (0, 0))],
        [pl.BlockSpec((A_TB, 1024), lambda i: (i, 0)),
         pl.BlockSpec((A_GC, A_HEADS, A_QK, A_V), lambda i: (i, 0, 0, 0)),
         pl.BlockSpec((A_GC, A_HEADS, 1, 128), lambda i: (i, 0, 0, 0)),
         pl.BlockSpec((A_GC, A_HEADS, 1, 128), lambda i: (i, 0, 0, 0))],
        [jax.ShapeDtypeStruct((T, 1024), MM_DTYPE),
         jax.ShapeDtypeStruct((NC, A_HEADS, A_QK, A_V), F32),
         jax.ShapeDtypeStruct((NC, A_HEADS, 1, 128), F32),
         jax.ShapeDtypeStruct((NC, A_HEADS, 1, 128), F32)],
        [pltpu.VMEM((A_HEADS, A_QK, A_V), F32), pltpu.VMEM((A_HEADS, 1, 128), F32),
         pltpu.VMEM((A_HEADS, 1, 128), F32)],
        ("arbitrary",), (z, z, z, z, gcol, grow, hng), ride)


def mlstm_bwd(z, gcol, grow, hng, bias128, Cs, ns, ms, dhg, ride=None):
    T = z.shape[0]
    NC = T // A_CHUNK
    nsteps = NC // A_GC
    scale = A_QK ** -0.5

    def body(q_ref, k_ref, v_ref, o_ref, zg_ref, gc_ref, gr_ref, hng_ref, b_ref, Cs_ref, ns_ref, ms_ref,
             dhg_ref, dz_ref, dgn_ref, dbif_ref, dC_sc, dn_sc):
        @pl.when(pl.program_id(0) == 0)
        def _():
            dC_sc[...] = jnp.zeros_like(dC_sc)
            dn_sc[...] = jnp.zeros_like(dn_sc)
            dgn_ref[...] = jnp.zeros_like(dgn_ref)
            dbif_ref[...] = jnp.zeros_like(dbif_ref)

        ri = lax.broadcasted_iota(jnp.int32, (A_CHUNK, A_CHUNK), 0)
        ci = lax.broadcasted_iota(jnp.int32, (A_CHUNK, A_CHUNK), 1)
        causal = ri >= ci
        upper = (ci >= ri).astype(F32)
        rid = lax.broadcasted_iota(jnp.int32, (A_CHUNK, 1), 0)
        col = lax.broadcasted_iota(jnp.int32, (A_CHUNK, 128), 1)
        gr = jnp.concatenate([gr_ref[s] for s in range(A_TB // 128)], axis=1)
        for c in reversed(range(A_GC)):
            rows = slice(c * A_CHUNK, (c + 1) * A_CHUNK)
            gc = gc_ref[rows, :]
            grc = gr[:, c * A_CHUNK:(c + 1) * A_CHUNK]
            dG = jnp.zeros((A_CHUNK, 128), F32)
            hs = []
            for h in range(A_HEADS):
                sk, sv = _head_slices(h)
                s = dict(sk=sk, sv=sv, qh=(q_ref[rows, sk] * scale).astype(MM_DTYPE),
                         kh=k_ref[rows, sk].astype(MM_DTYPE), vh=v_ref[rows, sv].astype(MM_DTYPE),
                         lic=gc[:, h:h + 1], bc=gc[:, A_HEADS + h:A_HEADS + h + 1],
                         lir=grc[h:h + 1, :], br=grc[A_HEADS + h:A_HEADS + h + 1, :],
                         C=Cs_ref[c, h], n=ns_ref[c, h], m=ms_ref[c, h][:, 0:1], dC=dC_sc[h], dn=dn_sc[h])
                s['qf'], s['kf'] = s['qh'].astype(F32), s['kh'].astype(F32)
                s['wk'], s['decay'], _ = _state_weights(s['bc'], s['lic'], s['br'], s['lir'], s['m'])
                hs.append(s)
            for s in hs:
                s['A'] = _dot_nt(s['qh'], s['kh'])
                s['qC'] = _dot(s['qh'], s['C'])
                s['vdC'] = _dot_nt(s['vh'], s['dC'])
                s['kdC'] = _dot(s['kh'], s['dC'])
            for s in hs:
                s['E'], s['Sm'], s['wi'], s['den'], s['gs'], s['m_t'] = _chunk_decay(
                    s['A'], s['qh'], s['bc'], s['br'], s['lir'], s['n'], s['m'], causal)
            for s in hs:
                s['num'] = _dot(s['Sm'], s['vh']) + s['wi'] * s['qC']
            for h, s in enumerate(hs):
                sv, gs = s['sv'], s['gs']
                hh = s['num'] / gs
                r = lax.rsqrt(jnp.mean(hh * hh, axis=1, keepdims=True) + EPS)
                gn = hng_ref[:, sv]
                sg = _sigmoid(o_ref[rows, sv])
                dhg_h = dhg_ref[rows, sv]
                dhn = dhg_h * sg
                dz_ref[rows, 2048 + h * A_V:2048 + (h + 1) * A_V] = (
                    dhg_h * (hh * r * gn) * sg * (1.0 - sg)).astype(dz_ref.dtype)
                dgn_ref[:, sv] += jnp.sum(dhn * hh * r, axis=0, keepdims=True)
                dyg = dhn * gn
                dh = r * dyg - hh * (r * r * r) * jnp.mean(dyg * hh, axis=1, keepdims=True)
                s['dnum'] = dh / gs
                live = (jnp.abs(s['den']) > jnp.exp(-s['m_t'])).astype(F32)
                s['dden'] = -jnp.sum(dh * hh, axis=1, keepdims=True) / gs * jnp.sign(s['den']) * live
            for s in hs:
                s['dnv'] = _dot_nt(s['dnum'], s['vh'])
                s['dnC'] = _dot_nt(s['dnum'], s['C'])
            for s in hs:
                s['dSE'] = jnp.where(causal, s['dnv'] + s['dden'], 0.0) * s['E']
            for s in hs:
                s['dq'] = _dot(s['dSE'], s['kh']) + s['wi'] * (s['dnC'] + s['dden'] * s['n'])
                s['dk_inter'] = s['wk'] * (s['vdC'] + s['dn'])
                s['dk'] = _dot_tn(s['dSE'], s['qh']) + s['dk_inter']
                s['dv'] = _dot_tn(s['Sm'], s['dnum']) + s['wk'] * s['kdC']
                s['dCq'] = _dot_tn(s['qf'] * s['wi'], s['dnum'])
            for h, s in enumerate(hs):
                dq, dk, qf, kf, dC, dn = s['dq'], s['dk'], s['qf'], s['kf'], s['dC'], s['dn']
                dz_ref[rows, s['sk']] = (dq * scale).astype(dz_ref.dtype)
                dz_ref[rows, 512 + h * A_QK:512 + (h + 1) * A_QK] = dk.astype(dz_ref.dtype)
                dz_ref[rows, 1024 + h * A_V:1024 + (h + 1) * A_V] = s['dv'].astype(dz_ref.dtype)
                dli = jnp.sum(kf * dk, axis=1, keepdims=True)
                db = jnp.sum(qf * dq, axis=1, keepdims=True) - dli
                usum = jnp.sum(jnp.sum(kf * s['dk_inter'], axis=1, keepdims=True), axis=0, keepdims=True)
                ddecay = (jnp.sum(jnp.sum(dC * s['C'], axis=1, keepdims=True), axis=0, keepdims=True)
                          + jnp.sum(dn * s['n'], axis=1, keepdims=True))
                db = db + jnp.where(rid == A_CHUNK - 1, usum + ddecay * s['decay'], 0.0)
                dG = dG + jnp.where(col == h, dli, 0.0) + jnp.where(col == A_HEADS + h, db, 0.0)
                dC_sc[h] = s['decay'] * dC + s['dCq']
                dn_sc[h] = s['decay'] * dn + jnp.sum(qf * (s['wi'] * s['dden']), axis=0, keepdims=True)
            dlf = jnp.dot(upper, dG, precision=HI, preferred_element_type=F32)
            pre = zg_ref[rows, :] + b_ref[...]
            th = jnp.tanh(pre / SOFTCAP)
            dcap = 1.0 - th * th
            dpre = jnp.where(col < A_HEADS, dG * dcap,
                             jnp.where(col < 2 * A_HEADS, dlf * _sigmoid(-SOFTCAP * th) * dcap, 0.0))
            dz_ref[rows, GATE_COL:GATE_COL + 128] = dpre.astype(dz_ref.dtype)
            dbif_ref[...] += jnp.sum(dpre, axis=0, keepdims=True)

    rev = lambda i: nsteps - 1 - i
    tok = lambda w, cb: pl.BlockSpec((A_TB, w), lambda i: (rev(i), cb))
    st = lambda a, b: pl.BlockSpec((A_GC, A_HEADS, a, b), lambda i: (rev(i), 0, 0, 0))
    return _call(
        body, "mlstm_bwd", (nsteps,),
        [tok(512, 0), tok(512, 1), tok(1024, 1), tok(1024, 2), tok(128, GATE_COL // 128),
         pl.BlockSpec((A_TB, 128), lambda i: (rev(i), 0)),
         pl.BlockSpec((A_TB // 128, 8, 128), lambda i: (rev(i), 0, 0)),
         pl.BlockSpec((1, 1024), lambda i: (0, 0)),
         pl.BlockSpec((1, 128), lambda i: (0, 0)),
         st(A_QK, A_V), st(1, 128), st(1, 128),
         pl.BlockSpec((A_TB, 1024), lambda i: (rev(i), 0))],
        [pl.BlockSpec((A_TB, A_IN_PAD), lambda i: (rev(i), 0)),
         pl.BlockSpec((1, 1024), lambda i: (0, 0)),
         pl.BlockSpec((1, 128), lambda i: (0, 0))],
        [jax.ShapeDtypeStruct((T, A_IN_PAD), MM_DTYPE), jax.ShapeDtypeStruct((1, 1024), F32),
         jax.ShapeDtypeStruct((1, 128), F32)],
        [pltpu.VMEM((A_HEADS, A_QK, A_V), F32), pltpu.VMEM((A_HEADS, 1, 128), F32)],
        ("arbitrary",), (z, z, z, z, z, gcol, grow, hng, bias128, Cs, ns, ms, dhg), ride)


def _t5_bucket(dist):
    max_exact = REL_BUCKETS // 2
    d = np.maximum(dist, 0)
    log_ratio = np.log(np.maximum(d, 1) / max_exact) / math.log(REL_MAX_DIST / max_exact)
    large = np.minimum(max_exact + (log_ratio * (REL_BUCKETS - max_exact)).astype(np.int64), REL_BUCKETS - 1)
    return np.where(d < max_exact, d, large).astype(np.int32)


def _group_bucket(g):
    delta = B_BLOCK + np.arange(B_BLOCK)[:, None] - np.arange(2 * B_BLOCK)[None, :]
    return _t5_bucket(delta * DILATIONS[g])


def _band_mask(n):
    ri = lax.broadcasted_iota(jnp.int32, (B_BLOCK, 2 * B_BLOCK), 0)
    ci = lax.broadcasted_iota(jnp.int32, (B_BLOCK, 2 * B_BLOCK), 1)
    band = jnp.logical_and(ci >= ri, ci <= ri + B_BLOCK)
    return jnp.logical_and(band, jnp.logical_or(ci >= B_BLOCK, n > 0))


def _both(p_ref, c_ref, sl):
    return jnp.concatenate([p_ref[:, sl], c_ref[:, sl]], axis=0)


def _scores(qh, kh, bias_h, valid):
    return jnp.where(valid, _dot_nt(qh, kh) * (B_DH ** -0.5) + bias_h, -jnp.inf)


def _attn_specs():
    wide = pl.BlockSpec((B_BLOCK, 1024), lambda r, n: (n, r))
    prev = pl.BlockSpec((B_BLOCK, 1024), lambda r, n: (jnp.maximum(n - 1, 0), r))
    narrow = pl.BlockSpec((B_BLOCK, 128), lambda r, n: (n, r))
    bias = pl.BlockSpec((B_HEADS, B_BLOCK, 2 * B_BLOCK), lambda r, n: (0, 0, 0))
    return wide, prev, narrow, bias


def _to_view(read_chunk, sc, o_ref, dil, nc, tt):
    for c in range(nc):
        sc[c] = read_chunk(c)
    for r in range(dil):
        for c in range(nc):
            lo = (r * nc + c) * 128
            o_ref[:, lo:lo + 128] = sc[c, pl.ds(r, tt // dil, stride=dil), :].astype(o_ref.dtype)


def _from_view(read_view, sc, dil, nc, tt):
    for r in range(dil):
        for c in range(nc):
            sc[c, pl.ds(r, tt // dil, stride=dil), :] = read_view((r * nc + c) * 128).astype(F32)


def attn_fwd(qv, kvw, vvw, bias, g):
    dil = DILATIONS[g]
    Tv = qv.shape[0]
    nb = Tv // B_BLOCK
    wide, prev, narrow, bsp = _attn_specs()

    def body(q_ref, kp_ref, kc_ref, vp_ref, vc_ref, b_ref, o_ref, lse_ref):
        valid = _band_mask(pl.program_id(1))
        lse_ref[...] = jnp.zeros_like(lse_ref)
        heads = [slice(h * B_DH, (h + 1) * B_DH) for h in range(B_HEADS)]
        S = [_scores(q_ref[:, sl], _both(kp_ref, kc_ref, sl), b_ref[h], valid) for h, sl in enumerate(heads)]
        P, L = [], []
        for h in range(B_HEADS):
            m = jnp.max(S[h], axis=1, keepdims=True)
            p = jnp.exp(S[h] - m)
            l = jnp.sum(p, axis=1, keepdims=True)
            lse_ref[:, h:h + 1] = m + jnp.log(l)
            P.append(p.astype(MM_DTYPE))
            L.append(l)
        for h, sl in enumerate(heads):
            o_ref[:, sl] = _dot(P[h], _both(vp_ref, vc_ref, sl)) / L[h]

    return pl.pallas_call(
        body, name=f"attn_fwd_g{g}", grid=(dil, nb),
        in_specs=[wide, prev, wide, prev, wide, bsp], out_specs=[wide, narrow],
        out_shape=[jax.ShapeDtypeStruct((Tv, dil * 1024), F32), jax.ShapeDtypeStruct((Tv, dil * 128), F32)],
        compiler_params=_params("parallel", "parallel"),
    )(qv, kvw, kvw, vvw, vvw, bias)


def attn_bwd(qv, kvw, vvw, bias, do_v, lse_v, dl_v, g):
    dil = DILATIONS[g]
    Tv = qv.shape[0]
    nb = Tv // B_BLOCK
    wide, prev, narrow, bsp = _attn_specs()

    def body(q_ref, kp_ref, kc_ref, vp_ref, vc_ref, b_ref, bt_ref, do_ref, lse_ref, dl_ref,
             dq_ref, dkc_ref, dkp_ref, dvc_ref, dvp_ref, db_ref):
        @pl.when(jnp.logical_and(pl.program_id(0) == 0, pl.program_id(1) == 0))
        def _():
            db_ref[...] = jnp.zeros_like(db_ref)

        n = pl.program_id(1)
        valid = _band_mask(n)
        ki = lax.broadcasted_iota(jnp.int32, (2 * B_BLOCK, B_BLOCK), 0)
        qi = lax.broadcasted_iota(jnp.int32, (2 * B_BLOCK, B_BLOCK), 1)
        valid_t = jnp.logical_and(jnp.logical_and(ki >= qi, ki <= qi + B_BLOCK), jnp.logical_or(ki >= B_BLOCK, n > 0))
        lse_t, dl_t = lse_ref[...].T, dl_ref[...].T
        heads = [slice(h * B_DH, (h + 1) * B_DH) for h in range(B_HEADS)]
        scale = B_DH ** -0.5
        PT, DS, DST = [], [], []
        for h, sl in enumerate(heads):
            qh, doh = q_ref[:, sl], do_ref[:, sl].astype(MM_DTYPE)
            kh, vh = _both(kp_ref, kc_ref, sl), _both(vp_ref, vc_ref, sl)
            p = jnp.exp(_scores(qh, kh, b_ref[h], valid) - lse_ref[:, h:h + 1])
            ds = p * (_dot_nt(doh, vh) - dl_ref[:, h:h + 1])
            db_ref[h] += ds
            DS.append((ds * scale).astype(MM_DTYPE))
            pt = jnp.exp(_scores(kh, qh, bt_ref[h], valid_t) - lse_t[h:h + 1, :])
            PT.append(pt.astype(MM_DTYPE))
            DST.append((pt * (_dot_nt(vh, doh) - dl_t[h:h + 1, :]) * scale).astype(MM_DTYPE))
        for h, sl in enumerate(heads):
            qh, doh = q_ref[:, sl], do_ref[:, sl].astype(MM_DTYPE)
            dq_ref[:, sl] = _dot(DS[h], _both(kp_ref, kc_ref, sl)).astype(MM_DTYPE)
            dk = _dot(DST[h], qh).astype(MM_DTYPE)
            dv = _dot(PT[h], doh).astype(MM_DTYPE)
            dkp_ref[:, sl], dkc_ref[:, sl] = dk[:B_BLOCK], dk[B_BLOCK:]
            dvp_ref[:, sl], dvc_ref[:, sl] = dv[:B_BLOCK], dv[B_BLOCK:]

    big = jax.ShapeDtypeStruct((Tv, dil * 1024), MM_DTYPE)
    bsp_t = pl.BlockSpec((B_HEADS, 2 * B_BLOCK, B_BLOCK), lambda r, n: (0, 0, 0))
    return pl.pallas_call(
        body, name=f"attn_bwd_g{g}", grid=(dil, nb),
        in_specs=[wide, prev, wide, prev, wide, bsp, bsp_t, wide, narrow, narrow],
        out_specs=[wide] * 5 + [bsp],
        out_shape=[big] * 5 + [jax.ShapeDtypeStruct((B_HEADS, B_BLOCK, 2 * B_BLOCK), F32)],
        compiler_params=_params("arbitrary", "arbitrary"),
    )(qv, kvw, kvw, vvw, vvw, bias, jnp.swapaxes(bias, 1, 2), do_v, lse_v, dl_v)


def _head_expand():
    e = np.zeros((128, 1024), np.float32)
    for h in range(B_HEADS):
        e[h, h * B_DH:(h + 1) * B_DH] = 1.0
    return e


A_TT = 256
A_TT_WIDE = 512


def _view_spec(dil, width, tt=A_TT):
    return pl.BlockSpec((tt // dil, dil * width), lambda i: (i, 0))


def attn_merge(os_v, lses_v):
    T = os_v[0].shape[0]
    tt = A_TT_WIDE
    expand = jnp.asarray(_head_expand())

    def body(o0, o1, o2, l0, l1, l2, e_ref, ob_ref, of_ref, lse0_ref, lse1_ref, lse2_ref, sc_o, sc_l):
        for gi, (o_ref, l_ref) in enumerate(((o1, l1), (o2, l2))):
            dil = DILATIONS[gi + 1]
            _from_view(lambda lo: o_ref[:, lo:lo + 128], sc_o.at[gi], dil, 8, tt)
            _from_view(lambda lo: l_ref[:, lo:lo + 128], sc_l.at[gi], dil, 1, tt)
        ls = [l0[...], sc_l[0, 0], sc_l[1, 0]]
        m = jnp.maximum(jnp.maximum(ls[0], ls[1]), ls[2])
        ex = [jnp.exp(l - m) for l in ls]
        tot = ex[0] + ex[1] + ex[2]
        lse = m + jnp.log(tot)
        lse0_ref[...] = lse
        _to_view(lambda c: lse, sc_l.at[2], lse1_ref, DILATIONS[1], 1, tt)
        _to_view(lambda c: lse, sc_l.at[2], lse2_ref, DILATIONS[2], 1, tt)
        ws = [e / tot for e in ex]
        for c in range(8):
            cols = slice(c * 128, (c + 1) * 128)
            ecol = e_ref[:, cols]
            spread = [jnp.dot(w, ecol, precision=HI, preferred_element_type=F32) for w in ws]
            out = spread[0] * o0[:, cols] + spread[1] * sc_o[0, c] + spread[2] * sc_o[1, c]
            of_ref[:, cols] = out
            ob_ref[:, cols] = out.astype(ob_ref.dtype)

    wide = pl.BlockSpec((tt, 1024), lambda i: (i, 0))
    return pl.pallas_call(
        body, name="attn_merge", grid=(T // tt,),
        in_specs=[_view_spec(d, 1024, tt) for d in DILATIONS] + [_view_spec(d, 128, tt) for d in DILATIONS]
        + [pl.BlockSpec((128, 1024), lambda i: (0, 0))],
        out_specs=[wide, wide] + [_view_spec(d, 128, tt) for d in DILATIONS],
        out_shape=[jax.ShapeDtypeStruct((T, 1024), MM_DTYPE), jax.ShapeDtypeStruct((T, 1024), F32)]
        + [jax.ShapeDtypeStruct((T // d, d * 128), F32) for d in DILATIONS],
        scratch_shapes=[pltpu.VMEM((2, 8, tt, 128), F32), pltpu.VMEM((3, 1, tt, 128), F32)],
        compiler_params=_params("parallel"),
    )(*os_v, *lses_v, expand)


def attn_prep(datt, out):
    T = datt.shape[0]
    tt = A_TT_WIDE
    expand_t = jnp.asarray(_head_expand().T.copy())

    def body(d_ref, o_ref, e_ref, do0, do1, do2, dl0, dl1, dl2, sc_d, sc_l):
        delta = jnp.dot(d_ref[...] * o_ref[...], e_ref[...], precision=HI, preferred_element_type=F32)
        do0[...] = d_ref[...].astype(do0.dtype)
        dl0[...] = delta
        for do_ref, dl_ref, dil in ((do1, dl1, DILATIONS[1]), (do2, dl2, DILATIONS[2])):
            _to_view(lambda c: d_ref[:, c * 128:(c + 1) * 128], sc_d, do_ref, dil, 8, tt)
            _to_view(lambda c: delta, sc_l, dl_ref, dil, 1, tt)

    wide = pl.BlockSpec((tt, 1024), lambda i: (i, 0))
    return pl.pallas_call(
        body, name="attn_prep", grid=(T // tt,),
        in_specs=[wide, wide, pl.BlockSpec((1024, 128), lambda i: (0, 0))],
        out_specs=[_view_spec(d, 1024, tt) for d in DILATIONS] + [_view_spec(d, 128, tt) for d in DILATIONS],
        out_shape=[jax.ShapeDtypeStruct((T // d, d * 1024), MM_DTYPE) for d in DILATIONS]
        + [jax.ShapeDtypeStruct((T // d, d * 128), F32) for d in DILATIONS],
        scratch_shapes=[pltpu.VMEM((8, tt, 128), F32), pltpu.VMEM((1, tt, 128), F32)],
        compiler_params=_params("parallel"),
    )(datt, out, expand_t)


def attn_combine(parts):
    T = parts[0][0].shape[0]
    tt = A_TT
    nt = T // tt
    shift = [None] + [B_BLOCK * d // tt for d in DILATIONS[1:]]

    def body(dq0, kc0, vc0, kpa0, kpb0, vpa0, vpb0, dq1, kc1, kp1, vc1, vp1, dq2, kc2, kp2, vc2, vp2,
             dq_ref, dkv_ref, sc):
        i = pl.program_id(0)
        dq_ref[:, 0:1024] = dq0[...].astype(dq_ref.dtype)
        for col, c_ref, pa_ref, pb_ref in ((0, kc0, kpa0, kpb0), (3, vc0, vpa0, vpb0)):
            nxt = jnp.where(i + 1 < nt, pb_ref[:tt // 2, :].astype(F32), 0.0)
            later = jnp.concatenate([pa_ref[tt // 2:, :].astype(F32), nxt], axis=0)
            dkv_ref[:, col * 1024:(col + 1) * 1024] = (c_ref[...].astype(F32) + later).astype(dkv_ref.dtype)
        for g, (dq, kc, kp, vc, vp) in ((1, (dq1, kc1, kp1, vc1, vp1)), (2, (dq2, kc2, kp2, vc2, vp2))):
            dil = DILATIONS[g]
            live = i + shift[g] < nt
            _from_view(lambda lo: dq[:, lo:lo + 128], sc, dil, 8, tt)
            for c in range(8):
                dq_ref[:, g * 1024 + c * 128:g * 1024 + (c + 1) * 128] = sc[c].astype(dq_ref.dtype)
            for col, c_ref, p_ref in ((g, kc, kp), (3 + g, vc, vp)):
                _from_view(lambda lo: c_ref[:, lo:lo + 128].astype(F32)
                           + jnp.where(live, p_ref[:, lo:lo + 128].astype(F32), 0.0), sc, dil, 8, tt)
                for c in range(8):
                    dkv_ref[:, col * 1024 + c * 128:col * 1024 + (c + 1) * 128] = sc[c].astype(dkv_ref.dtype)

    def later_spec(dil, blocks):
        return pl.BlockSpec((tt // dil, dil * 1024), lambda i: (jnp.minimum(i + blocks, nt - 1), 0))

    cur = [_view_spec(d, 1024) for d in DILATIONS]
    in_specs = [cur[0], cur[0], cur[0], cur[0], later_spec(1, 1), cur[0], later_spec(1, 1)]
    args = [parts[0][0], parts[0][1], parts[0][3], parts[0][2], parts[0][2], parts[0][4], parts[0][4]]
    for g in (1, 2):
        in_specs += [cur[g], cur[g], later_spec(DILATIONS[g], shift[g]), cur[g], later_spec(DILATIONS[g], shift[g])]
        args += list(parts[g][:5])
    return pl.pallas_call(
        body, name="attn_combine", grid=(nt,), in_specs=in_specs,
        out_specs=[pl.BlockSpec((tt, 3072), lambda i: (i, 0)), pl.BlockSpec((tt, 6144), lambda i: (i, 0))],
        out_shape=[jax.ShapeDtypeStruct((T, 3072), MM_DTYPE), jax.ShapeDtypeStruct((T, 6144), MM_DTYPE)],
        scratch_shapes=[pltpu.VMEM((8, tt, 128), F32)],
        compiler_params=_params("parallel"),
    )(*args)


def adamw(w, g, m, v, name):
    R, C = w.shape
    tr = R if R * C * 4 <= (1 << 20) else _rows(R, max(8, ((1 << 20) // (C * 4)) // 8 * 8))

    def body(w_ref, g_ref, m_ref, v_ref, d_ref, nm_ref, nv_ref):
        gg = g_ref[...]
        nm = ADAM_B1 * m_ref[...] + (1.0 - ADAM_B1) * gg
        nv = ADAM_B2 * v_ref[...] + (1.0 - ADAM_B2) * (gg * gg)
        m_hat = nm / (1.0 - ADAM_B1 ** ADAM_STEP)
        v_hat = nv / (1.0 - ADAM_B2 ** ADAM_STEP)
        d_ref[...] = -ADAM_LR * (m_hat / (jnp.sqrt(v_hat) + ADAM_EPS) + ADAM_WD * w_ref[...])
        nm_ref[...] = nm
        nv_ref[...] = nv

    blk = pl.BlockSpec((tr, C), lambda i: (i, 0))
    sds = jax.ShapeDtypeStruct((R, C), F32)
    return pl.pallas_call(
        body, name=name, grid=(R // tr,), in_specs=[blk] * 4, out_specs=[blk] * 3, out_shape=[sds] * 3,
        compiler_params=_params("parallel"),
    )(w, g, m, v)


def sum_slots(x, name, out_dtype=F32):
    n, R, C = x.shape
    tr = _rows(R, 256)

    def body(x_ref, o_ref):
        acc = x_ref[0].astype(F32)
        for s in range(1, n):
            acc = acc + x_ref[s].astype(F32)
        o_ref[...] = acc.astype(out_dtype)

    return pl.pallas_call(
        body, name=name, grid=(R // tr,),
        in_specs=[pl.BlockSpec((n, tr, C), lambda i: (0, i, 0))],
        out_specs=pl.BlockSpec((tr, C), lambda i: (i, 0)),
        out_shape=jax.ShapeDtypeStruct((R, C), out_dtype),
        compiler_params=_params("parallel"),
    )(x)


_ANY = pl.BlockSpec(memory_space=pl.ANY)
GROUP_ALL = ([(0, 0, 1), (0, 1, 0), (0, 1, 1), (1, 0, 0), (1, 0, 1), (1, 1, 0), (1, 1, 1)],
             lambda d: 4 * d[0] + 2 * d[1] + d[2])
GROUP_CHIPS = ([(0, 1, 0), (1, 0, 0), (1, 1, 0)], lambda d: 2 * d[0] + d[1])
GROUP_SIBLING = ([(0, 0, 1)], lambda d: d[2])


def _me():
    return lax.axis_index("x"), lax.axis_index("y"), lax.axis_index("c")


def _peer(me, flip):
    return tuple(1 - a if f else a for a, f in zip(me, flip))


class Exchange:
    def __init__(self, x, group, scatter):
        self.flips, self.slot = group
        self.scatter = scatter
        self.n = len(self.flips) + 1
        self.out_shape = jax.ShapeDtypeStruct((self.n,) + x.shape[-2:], x.dtype)
        self.scratch = [pltpu.SemaphoreType.DMA((self.n - 1,)), pltpu.SemaphoreType.DMA((self.n - 1,)),
                        pltpu.SemaphoreType.DMA]

    def _copies(self, x_ref, o_ref, send_sems, recv_sems, local_sem, arrivals):
        me = _me()
        slot = self.slot
        mine = pltpu.make_async_copy(x_ref.at[slot(me)] if self.scatter else x_ref, o_ref.at[slot(me)], local_sem)
        sends, landed = [], []
        for k, flip in enumerate(self.flips):
            peer = _peer(me, flip)
            sends.append(pltpu.make_async_remote_copy(
                src_ref=x_ref.at[slot(peer)] if self.scatter else x_ref, dst_ref=o_ref.at[slot(me)],
                send_sem=send_sems.at[k], recv_sem=recv_sems.at[k], device_id=peer, device_id_type=MESH_ID))
            if arrivals:
                landed.append(pltpu.make_async_remote_copy(
                    src_ref=o_ref.at[slot(me)], dst_ref=o_ref.at[slot(peer)], send_sem=send_sems.at[k],
                    recv_sem=recv_sems.at[k], device_id=peer, device_id_type=MESH_ID))
        return mine, sends, landed

    def start(self, *refs):
        mine, sends, _ = self._copies(*refs, arrivals=False)
        mine.start()
        for cp in sends:
            cp.start()

    def wait(self, *refs):
        mine, sends, arrivals = self._copies(*refs, arrivals=True)
        for cp in arrivals:
            cp.wait_recv()
        for cp in sends:
            cp.wait_send()
        mine.wait()

    def __call__(self, x, name):
        def body(*refs):
            self.start(*refs)
            self.wait(*refs)

        return pl.pallas_call(body, name=name, in_specs=[_ANY], out_specs=_ANY, out_shape=self.out_shape,
                              scratch_shapes=self.scratch)(x)


def group_gather(x, name, group):
    return Exchange(x, group, scatter=False)(x, name)


def group_scatter(x, name, group):
    return Exchange(x, group, scatter=True)(x, name)


def _call(body, name, grid, in_specs, out_specs, out_shape, scratch, semantics, args, ride=None):
    if ride is None:
        return pl.pallas_call(body, name=name, grid=grid, in_specs=in_specs, out_specs=out_specs,
                              out_shape=out_shape, scratch_shapes=scratch,
                              compiler_params=_params(*semantics))(*args)
    x, exch = ride
    n_in, n_out, n_scr = len(in_specs), len(out_specs), len(scratch)

    def at_step(pick):
        hit = None
        for axis, size in enumerate(grid):
            here = pl.program_id(axis) == pick(size)
            hit = here if hit is None else jnp.logical_and(hit, here)
        return hit

    def riding(*refs):
        ins, x_ref = refs[:n_in], refs[n_in]
        outs, o_ref = refs[n_in + 1:n_in + 1 + n_out], refs[n_in + 1 + n_out]
        scr, sems = refs[n_in + 2 + n_out:n_in + 2 + n_out + n_scr], refs[n_in + 2 + n_out + n_scr:]

        @pl.when(at_step(lambda size: 0))
        def _():
            exch.start(x_ref, o_ref, *sems)

        body(*ins, *outs, *scr)

        @pl.when(at_step(lambda size: size - 1))
        def _():
            exch.wait(x_ref, o_ref, *sems)

    return pl.pallas_call(
        riding, name=name, grid=grid, in_specs=list(in_specs) + [_ANY], out_specs=list(out_specs) + [_ANY],
        out_shape=list(out_shape) + [exch.out_shape], scratch_shapes=list(scratch) + exch.scratch,
        compiler_params=_params(*(["arbitrary"] * len(grid))))(*args, x)


WEIGHTS = ['a_norm_g', 'a_w_in', 'a_b_if', 'a_hnorm_g', 'a_w_out', 'kv_norm_g', 'w_kv', 'b_norm_g', 'b_w_q',
           'b_w_out', 'rel_bias', 'f_norm_g', 'f_w_up', 'f_conv_w', 'f_conv_b', 'f_w_down', 'final_norm_g']
SHARD_AXIS = {'a_norm_g': 1, 'a_w_in': 2, 'a_b_if': None, 'a_hnorm_g': 2, 'a_w_out': 1, 'kv_norm_g': None,
              'w_kv': 1, 'b_norm_g': None, 'b_w_q': 2, 'b_w_out': 1, 'rel_bias': None, 'f_norm_g': None,
              'f_w_up': 2, 'f_conv_w': 2, 'f_conv_b': None, 'f_w_down': 1, 'final_norm_g': None}
BIG = ['a_w_in', 'a_w_out', 'w_kv', 'b_w_q', 'b_w_out', 'f_w_up', 'f_w_down']
SMALL = [n for n in WEIGHTS if n not in BIG]
LANES = 1024
PIECES = {'a_w_in': ('a_w_in', None, 2), 'a_w_out': ('a_w_out', None, 1), 'f_w_up0': ('f_w_up', 0, 1),
          'f_w_down0': ('f_w_down', 0, 0), 'w_kv': ('w_kv', None, 1), 'b_w_q': ('b_w_q', None, 2),
          'b_w_out': ('b_w_out', None, 1), 'f_w_up1': ('f_w_up', 1, 1), 'f_w_down1': ('f_w_down', 1, 0)}
LATE = ['w_kv', 'b_w_q', 'b_w_out', 'f_w_up1', 'f_w_down1']
WEIGHT_WAVES = {'first': ['a_w_in'], 'ffn0': ['a_w_out', 'f_w_up0', 'f_w_down0'], 'late': LATE}
GRAD_WAVES = {'late': LATE, 'layer0': ['f_w_up0', 'f_w_down0', 'a_w_out'], 'last': ['a_w_in']}


def _piece(arrays, p):
    leaf, layer, _ = PIECES[p]
    return arrays[leaf] if layer is None else arrays[leaf][layer]


class Packer:
    def __init__(self, pieces, shard):
        self.pieces = pieces
        self.shapes = [_piece(shard, p).shape for p in pieces]
        self.sizes = [math.prod(s) // (2 * LANES) for s in self.shapes]
        self.fill = -sum(self.sizes) % 16
        self.rows = sum(self.sizes) + self.fill

    def my_half(self, shard, half):
        both = jnp.concatenate([_piece(shard, p).astype(MM_DTYPE).reshape(2, -1, LANES) for p in self.pieces], axis=1)
        return jnp.pad(lax.dynamic_index_in_dim(both, half, axis=0, keepdims=False), ((0, self.fill), (0, 0)))

    def full_weights(self, gathered):
        g = gathered.reshape(4, 2, self.rows, LANES)
        out, off = {}, 0
        for p, shp, sz in zip(self.pieces, self.shapes, self.sizes):
            out[p] = _full_from_shards(g[:, :, off:off + sz].reshape((4,) + shp), PIECES[p][2])
            off += sz
        return out

    def grad_slots(self, grads):
        parts = [_shards_from_full(grads[p], PIECES[p][2]).reshape(4, 2, -1, LANES).astype(GRAD_WIRE_DTYPE)
                 for p in self.pieces]
        parts.append(jnp.zeros((4, 2, self.fill, LANES), GRAD_WIRE_DTYPE))
        return jnp.concatenate(parts, axis=2).reshape(8, self.rows, LANES)

    def shard_grads(self, both):
        out, off = {}, 0
        for p, shp, sz in zip(self.pieces, self.shapes, self.sizes):
            out[p] = both[:, off:off + sz].reshape(shp).astype(F32)
            off += sz
        return out


class Overlap:
    def __init__(self, shard, half):
        self.shard, self.half = shard, half
        self.weights = {w: Packer(p, shard) for w, p in WEIGHT_WAVES.items()}
        self.grads = {w: Packer(p, shard) for w, p in GRAD_WAVES.items()}
        self.shard_grads = {}

    def gather_ride(self, wave):
        mine = self.weights[wave].my_half(self.shard, self.half)
        return mine, Exchange(mine, GROUP_ALL, scatter=False)

    def gathered(self, wave, slots):
        return self.weights[wave].full_weights(slots)

    def scatter_ride(self, wave, grads):
        slots = self.grads[wave].grad_slots({p: grads.pop(p) for p in GRAD_WAVES[wave]})
        return slots, Exchange(slots, GROUP_ALL, scatter=True)

    def join_ride(self, wave, received):
        reduced = sum_slots(received, f"sum_grads_{wave}", GRAD_WIRE_DTYPE)
        return reduced, Exchange(reduced, GROUP_SIBLING, scatter=False)

    def joined(self, wave, both):
        self.shard_grads.update(self.grads[wave].shard_grads(both))


def _pad_rows(flat, mult):
    n = flat.shape[0]
    per = LANES * mult
    tot = -(-n // per) * per
    return jnp.pad(flat, (0, tot - n)).reshape(tot // LANES, LANES)


def _full_from_shards(sh, axis):
    shp = sh.shape[1:]
    return jnp.moveaxis(sh, 0, axis).reshape(shp[:axis] + (4 * shp[axis],) + shp[axis + 1:])


def _shards_from_full(full, axis):
    shp = full.shape
    return jnp.moveaxis(full.reshape(shp[:axis] + (4, shp[axis] // 4) + shp[axis + 1:]), axis, 0)


def _local_step(x, target, W, overlap=None):
    T = x.shape[0]
    W = dict(W)
    row = lambda a: a.reshape(1, -1).astype(F32)
    w_in = jnp.pad(W['a_w_in'][0], ((0, 0), (0, A_IN_PAD - A_IN)))
    bias128 = jnp.pad(row(W['a_b_if'][0]), ((0, 0), (0, 120)))
    hng = row(W['a_hnorm_g'][0])
    w_up = lambda l: _interleave(W[f'f_w_up{l}'])
    cw = [_interleave(W['f_conv_w'][l].astype(F32)) for l in range(2)]
    cb = [_interleave(row(W['f_conv_b'][l])) for l in range(2)]
    onehots = [(jnp.asarray(_group_bucket(g).reshape(-1, 1)) == jnp.arange(128)[None, :]).astype(F32)
               for g in range(N_GROUPS)]
    rb_t = jnp.pad(W['rel_bias'].astype(F32).T, ((0, 0), (0, 128 - REL_BUCKETS)))
    biases = [mm_nn(rb_t[g * B_HEADS:(g + 1) * B_HEADS], onehots[g].T, f"rel_bias_table_g{g}", exact=True)
              .reshape(B_HEADS, B_BLOCK, 2 * B_BLOCK) for g in range(N_GROUPS)]
    G = {}

    def ffn_fwd(xin, l, ride=None):
        xn, = rms_fwd(xin, [row(W['f_norm_g'][l])], f"ffn{l}_norm")
        u, act, *rode = ffn_up_act(xn, w_up(l), cw[l], cb[l], f"ffn{l}_up_act", ride)
        return mm_nn(act, W[f'f_w_down{l}'], f"ffn{l}_down", res=xin), (xn, u, act), rode

    def ffn_bwd(xin, saved, dout, l, ride=None):
        xn, u, act = saved
        dact = mm_nn(dout, W[f'f_w_down{l}'].T, f"ffn{l}_ddown")
        G[f'f_w_down{l}'] = mm_tn(act, dout, f"ffn{l}_gdown")
        du, gcw, gcb, *rode = conv_act_bwd(u, dact, cw[l], cb[l], f"ffn{l}_dact", ride)
        dxn = mm_nn(du, w_up(l).T, f"ffn{l}_dup")
        G[f'f_w_up{l}'] = _deinterleave(mm_tn(xn, du, f"ffn{l}_gup"))
        dxin, (gn,) = rms_bwd(xin, dout, [(dxn, row(W['f_norm_g'][l]))], f"ffn{l}_dnorm")
        return dxin, _deinterleave(gcw), _deinterleave(gcb), gn, rode

    xn_a, = rms_fwd(x, [row(W['a_norm_g'][0])], "a_norm")
    z = mm_nn(xn_a, w_in, "a_in")
    gcol, grow = gate_prep(z, bias128)
    hg, Cs, ns, ms, *rode = mlstm_fwd(z, gcol, grow, hng, overlap.gather_ride('ffn0') if overlap else None)
    if overlap:
        W.update(overlap.gathered('ffn0', rode[0]))
    x1 = mm_nn(hg, W['a_w_out'][0], "a_out", res=x)
    x2, ffn0, rode = ffn_fwd(x1, 0, overlap.gather_ride('late') if overlap else None)
    if overlap:
        W.update(overlap.gathered('late', rode[0]))
    xn_kv, xn_b = rms_fwd(x2, [row(W['kv_norm_g']), row(W['b_norm_g'][0])], "b_norms")
    gcols = lambda w, c: w[:, c * 1024:(c + 1) * 1024]
    qv = [mm_view(xn_b, gcols(W['b_w_q'][0], g), f"q_proj_g{g}", DILATIONS[g]) for g in range(N_GROUPS)]
    kvw = [mm_view(xn_kv, gcols(W['w_kv'], g), f"k_proj_g{g}", DILATIONS[g]) for g in range(N_GROUPS)]
    vvw = [mm_view(xn_kv, gcols(W['w_kv'], 3 + g), f"v_proj_g{g}", DILATIONS[g]) for g in range(N_GROUPS)]
    os_, lses = zip(*[attn_fwd(qv[g], kvw[g], vvw[g], biases[g], g) for g in range(N_GROUPS)])
    att, att_f, *lse_v = attn_merge(os_, lses)
    x3 = mm_nn(att, W['b_w_out'][0], "b_out", res=x2)
    x4, ffn1, _ = ffn_fwd(x3, 1)
    dx4, g_final, loss = loss_head(x4, target, row(W['final_norm_g']))
    G['final_norm_g'] = g_final.reshape(-1)

    dx3, gcw1, gcb1, gn1, _ = ffn_bwd(x3, ffn1, dx4, 1)
    datt = mm_nn(dx3, W['b_w_out'][0].T, "b_dout")
    G['b_w_out'] = mm_tn(att, dx3, "b_gout")[None]
    prep = attn_prep(datt, att_f)
    do_v, dl_v = prep[:3], prep[3:]
    parts = [attn_bwd(qv[g], kvw[g], vvw[g], biases[g], do_v[g], lse_v[g], dl_v[g], g) for g in range(N_GROUPS)]
    dq_all, dkv = attn_combine(parts)
    grb = []
    for g in range(N_GROUPS):
        gb = mm_nn(parts[g][5].reshape(B_HEADS, -1), onehots[g], f"rel_bias_g{g}", exact=True)
        grb.append(gb[:, :REL_BUCKETS].T)
    G['rel_bias'] = jnp.concatenate(grb, axis=1)
    dxn_b = mm_nn(dq_all, W['b_w_q'][0].T, "q_dproj")
    G['b_w_q'] = mm_tn(xn_b, dq_all, "q_gproj")[None]
    dxn_kv = mm_nn(dkv, W['w_kv'].T, "kv_dproj")
    G['w_kv'] = mm_tn(xn_kv, dkv, "kv_gproj")
    dx2, (g_kvn, g_bn) = rms_bwd(x2, dx3, [(dxn_kv, row(W['kv_norm_g'])), (dxn_b, row(W['b_norm_g'][0]))],
                                 "b_dnorms")
    G['kv_norm_g'] = g_kvn.reshape(-1)
    G['b_norm_g'] = g_bn
    dx1, gcw0, gcb0, gn0, late_slots = ffn_bwd(x1, ffn0, dx2, 0, overlap.scatter_ride('late', G) if overlap else None)
    G['f_conv_w'] = jnp.stack([gcw0, gcw1])
    G['f_conv_b'] = jnp.concatenate([gcb0, gcb1], axis=0)
    G['f_norm_g'] = jnp.concatenate([gn0, gn1], axis=0)
    dhg = mm_nn(dx1, W['a_w_out'][0].T, "a_dout")
    G['a_w_out'] = mm_tn(hg, dx1, "a_gout")[None]
    dz, g_hn, g_bif, *layer0_slots = mlstm_bwd(z, gcol, grow, hng, bias128, Cs, ns, ms, dhg,
                                               overlap.scatter_ride('layer0', G) if overlap else None)
    G['a_hnorm_g'] = g_hn.reshape(1, A_HEADS, A_V)
    G['a_b_if'] = g_bif[:, :2 * A_HEADS]
    if overlap:
        dxn_a, both = mm_nn(dz, w_in.T, "a_din", ride=overlap.join_ride('late', late_slots[0]))
        overlap.joined('late', both)
        g_in, both = mm_tn(xn_a, dz, "a_gin", ride=overlap.join_ride('layer0', layer0_slots[0]))
        overlap.joined('layer0', both)
    else:
        dxn_a = mm_nn(dz, w_in.T, "a_din")
        g_in = mm_tn(xn_a, dz, "a_gin")
    G['a_w_in'] = g_in[:, :A_IN][None]
    grad_x, (g_an,) = rms_bwd(x, dx1, [(dxn_a, row(W['a_norm_g'][0]))], "a_dnorm")
    G['a_norm_g'] = g_an
    return loss, grad_x, G


def kernel(x, a_norm_g, a_w_in, a_b_if, a_hnorm_g, a_w_out, kv_norm_g, w_kv, b_norm_g, b_w_q, b_w_out, rel_bias, f_norm_g, f_w_up, f_conv_w, f_conv_b, f_w_down, final_norm_g, loss_target, m_a_norm_g, m_a_w_in, m_a_b_if, m_a_hnorm_g, m_a_w_out, m_kv_norm_g, m_w_kv, m_b_norm_g, m_b_w_q, m_b_w_out, m_rel_bias, m_f_norm_g, m_f_w_up, m_f_conv_w, m_f_conv_b, m_f_w_down, m_final_norm_g, v_a_norm_g, v_a_w_in, v_a_b_if, v_a_hnorm_g, v_a_w_out, v_kv_norm_g, v_w_kv, v_b_norm_g, v_b_w_q, v_b_w_out, v_rel_bias, v_f_norm_g, v_f_w_up, v_f_conv_w, v_f_conv_b, v_f_w_down, v_final_norm_g):
    given = dict(locals())
    shard = {n: given[n] for n in WEIGHTS}
    mom = {n: given["m_" + n] for n in WEIGHTS}
    var = {n: given["v_" + n] for n in WEIGHTS}
    cx, cy, cc = _me()
    chip = 2 * cx + cy

    overlap = Overlap(shard, cc)
    mine, gather = overlap.gather_ride('first')
    W = overlap.gathered('first', gather(mine, "gather_weights"))
    sharded_small = [n for n in SMALL if SHARD_AXIS[n] is not None]
    ssz = [shard[n].size for n in sharded_small]
    sflat = jnp.concatenate([shard[n].reshape(-1) for n in sharded_small])
    sg = group_gather(_pad_rows(sflat, 8), "gather_small", GROUP_CHIPS).reshape(4, -1)
    off = 0
    for n, sz in zip(sharded_small, ssz):
        W[n] = _full_from_shards(sg[:, off:off + sz].reshape((4,) + shard[n].shape), SHARD_AXIS[n])
        off += sz
    for n in SMALL:
        if SHARD_AXIS[n] is None:
            W[n] = shard[n]

    loss_row, grad_x, G = _local_step(x[0], loss_target[0], W, overlap)

    slots, scatter = overlap.scatter_ride('last', G)
    reduced, join = overlap.join_ride('last', scatter(slots, "scatter_grads"))
    overlap.joined('last', join(reduced, "join_halves"))
    by_piece = overlap.shard_grads
    gsh = {}
    for n in BIG:
        layers = [p for p in PIECES if PIECES[p][0] == n]
        gsh[n] = by_piece[n] if layers == [n] else jnp.stack([by_piece[p] for p in layers])
    small_parts = [loss_row[0, 0:1]] + [G[n].reshape(-1) for n in SMALL]
    small_sz = [p.shape[0] for p in small_parts]
    small = sum_slots(group_gather(_pad_rows(jnp.concatenate(small_parts), 8), "gather_small_grads", GROUP_ALL),
                      "sum_small_grads").reshape(-1)
    loss = small[0]
    off = 1
    for n, sz in zip(SMALL, small_sz[1:]):
        full = small[off:off + sz].reshape(W[n].shape)
        off += sz
        if SHARD_AXIS[n] is None:
            gsh[n] = full
        else:
            gsh[n] = lax.dynamic_index_in_dim(_shards_from_full(full, SHARD_AXIS[n]), chip, 0, keepdims=False)

    delta, new_m, new_v = {}, {}, {}
    for n in WEIGHTS:
        shp = shard[n].shape
        two = lambda a: a.reshape(-1, shp[-1])
        d, nm, nv = adamw(two(shard[n]), two(gsh[n]), two(mom[n]), two(var[n]), f"adamw_{n}")
        delta[n], new_m[n], new_v[n] = d.reshape(shp), nm.reshape(shp), nv.reshape(shp)
    return (loss, grad_x[None], *[gsh[n] for n in WEIGHTS], *[delta[n] for n in WEIGHTS],
            *[new_m[n] for n in WEIGHTS], *[new_v[n] for n in WEIGHTS])
```

```python
import functools
import math

import numpy as np
import jax
import jax.numpy as jnp
from jax import lax
from jax.experimental import pallas as pl
from jax.experimental.pallas import tpu as pltpu

F32 = jnp.float32
BF16 = jnp.bfloat16
MM_DTYPE = jnp.bfloat16
GRAD_WIRE_DTYPE = jnp.bfloat16
HI = lax.Precision.HIGHEST

D_MODEL = 1024
A_HEADS = 4
A_QK = 128
A_V = 256
A_CHUNK = 256
A_IN = 3080
A_IN_PAD = 3200
GATE_COL = 3072
SOFTCAP = 15.0
N_GROUPS = 3
B_HEADS = 16
B_DH = 64
B_BLOCK = 128
DILATIONS = (1, 4, 16)
WINDOWS = (128, 512, 2048)
REL_BUCKETS = 32
REL_MAX_DIST = 2048
D_FF = 2816
FF_TC = 256
EPS = 1e-6
ADAM_LR, ADAM_B1, ADAM_B2, ADAM_EPS, ADAM_WD, ADAM_STEP = 0.001, 0.9, 0.999, 1e-08, 0.01, 10

VMEM_LIMIT = 56 * 1024 * 1024
NT_DIMS = (((1,), (1,)), ((), ()))
TN_DIMS = (((0,), (0,)), ((), ()))
MESH_ID = pl.DeviceIdType.MESH


def _params(*sem):
    return pltpu.CompilerParams(dimension_semantics=sem, vmem_limit_bytes=VMEM_LIMIT)


def _tile(n, cap):
    if n <= cap:
        return n
    best = None
    for t in range(128, cap + 1, 128):
        if n % t == 0:
            best = t
    assert best is not None, (n, cap)
    return best


def _rows(n, cap):
    if n <= cap:
        return n
    for t in range(cap // 8 * 8, 7, -8):
        if n % t == 0:
            return t
    raise ValueError((n, cap))


def _dot(a, b):
    return jnp.dot(a.astype(MM_DTYPE), b.astype(MM_DTYPE), preferred_element_type=F32)


def _dot_nt(a, b):
    return lax.dot_general(a.astype(MM_DTYPE), b.astype(MM_DTYPE), NT_DIMS, preferred_element_type=F32)


def _dot_tn(a, b):
    return lax.dot_general(a.astype(MM_DTYPE), b.astype(MM_DTYPE), TN_DIMS, preferred_element_type=F32)


def _sigmoid(x):
    return 1.0 / (1.0 + jnp.exp(-x))


def _sigmoid_tanh(x):
    return 0.5 * jnp.tanh(0.5 * x) + 0.5


def mm_nn(a, b, name, res=None, out_dtype=F32, exact=False, ride=None):
    M, K = a.shape
    N = b.shape[1]
    def footprint(tm, tn):
        return 2 * (tm * K * a.dtype.itemsize + K * tn * b.dtype.itemsize) + 2 * tm * tn * 4 * (1 if res is None else 2)

    budget = 46 * 1024 * 1024
    tm = _rows(M, 512)
    tn = N if N <= 3328 and footprint(tm, N) <= budget else _tile(N, 1536)
    tk = K if footprint(tm, tn) <= budget else _tile(K, 1536)
    if tk == K and footprint(_rows(M, 1024), tn) <= budget:
        tm = _rows(M, 1024)
    nk = K // tk

    def body(*refs):
        if res is None:
            a_ref, b_ref, o_ref, acc = refs
            r_ref = None
        else:
            a_ref, b_ref, r_ref, o_ref, acc = refs
        if exact:
            p = jnp.dot(a_ref[...], b_ref[...], precision=HI, preferred_element_type=F32)
        else:
            p = _dot(a_ref[...], b_ref[...])

        def finish(total):
            if r_ref is not None:
                total = total + r_ref[...]
            o_ref[...] = total.astype(out_dtype)

        if nk == 1:
            finish(p)
        else:
            k = pl.program_id(2)

            @pl.when(k == 0)
            def _():
                acc[...] = p

            @pl.when(jnp.logical_and(k > 0, k < nk - 1))
            def _():
                acc[...] += p

            @pl.when(k == nk - 1)
            def _():
                finish(acc[...] + p)

    in_specs = [pl.BlockSpec((tm, tk), lambda j, i, k: (i, k)),
                pl.BlockSpec((tk, tn), lambda j, i, k: (k, j))]
    args = [a, b]
    if res is not None:
        in_specs.append(pl.BlockSpec((tm, tn), lambda j, i, k: (i, j)))
        args.append(res)
    acc_shape = (tm, tn) if nk > 1 else (8, 128)
    outs = _call(body, name, (N // tn, M // tm, nk), in_specs, [pl.BlockSpec((tm, tn), lambda j, i, k: (i, j))],
                 [jax.ShapeDtypeStruct((M, N), out_dtype)], [pltpu.VMEM(acc_shape, F32)],
                 ("parallel", "parallel", "arbitrary"), args, ride)
    return outs[0] if ride is None else outs


def mm_view(a, b, name, dil):
    T, K = a.shape
    tm = 2048

    def body(a_ref, b_ref, o_ref, sc):
        p = _dot(a_ref[...], b_ref[...])
        if dil == 1:
            o_ref[...] = p.astype(o_ref.dtype)
        else:
            _to_view(lambda c: p[:, c * 128:(c + 1) * 128], sc, o_ref, dil, 8, tm)

    return pl.pallas_call(
        body, name=name, grid=(T // tm,),
        in_specs=[pl.BlockSpec((tm, K), lambda i: (i, 0)), pl.BlockSpec((K, 1024), lambda i: (0, 0))],
        out_specs=pl.BlockSpec((tm // dil, dil * 1024), lambda i: (i, 0)),
        out_shape=jax.ShapeDtypeStruct((T // dil, dil * 1024), MM_DTYPE),
        scratch_shapes=[pltpu.VMEM((8, tm, 128), F32)],
        compiler_params=_params("parallel"),
    )(a, b)


def mm_tn(a, g, name, ride=None):
    T, Ka = a.shape
    N = g.shape[1]
    tka, tt = _tile(Ka, 1536), _rows(T, 1024)

    def footprint(tt, tn):
        return 2 * (tt * tka * a.dtype.itemsize + tt * tn * g.dtype.itemsize + tka * tn * 4)

    budget = 46 * 1024 * 1024
    tn = N if N <= 3328 and footprint(tt, N) <= budget else _tile(N, 1536)
    if footprint(_rows(T, 2048), tn) <= budget:
        tt = _rows(T, 2048)
    nt = T // tt

    def body(a_ref, g_ref, o_ref):
        t = pl.program_id(2)
        p = _dot_tn(a_ref[...], g_ref[...])

        @pl.when(t == 0)
        def _():
            o_ref[...] = p

        @pl.when(t > 0)
        def _():
            o_ref[...] += p

    outs = _call(body, name, (Ka // tka, N // tn, nt),
                 [pl.BlockSpec((tt, tka), lambda i, j, t: (t, i)), pl.BlockSpec((tt, tn), lambda i, j, t: (t, j))],
                 [pl.BlockSpec((tka, tn), lambda i, j, t: (i, j))], [jax.ShapeDtypeStruct((Ka, N), F32)], [],
                 ("parallel", "parallel", "arbitrary"), (a, g), ride)
    return outs[0] if ride is None else outs


def rms_fwd(x, gains, name):
    T, D = x.shape
    tt = _rows(T, 1024)
    ng = len(gains)

    def body(*refs):
        x_ref = refs[0]
        g_refs = refs[1:1 + ng]
        o_refs = refs[1 + ng:]
        xf = x_ref[...]
        y = xf * lax.rsqrt(jnp.mean(xf * xf, axis=-1, keepdims=True) + EPS)
        for g_ref, o_ref in zip(g_refs, o_refs):
            o_ref[...] = (y * g_ref[...]).astype(o_ref.dtype)

    row = pl.BlockSpec((tt, D), lambda i: (i, 0))
    gsp = pl.BlockSpec((1, D), lambda i: (0, 0))
    return pl.pallas_call(
        body, name=name, grid=(T // tt,),
        in_specs=[row] + [gsp] * ng, out_specs=[row] * ng,
        out_shape=[jax.ShapeDtypeStruct((T, D), MM_DTYPE)] * ng,
        compiler_params=_params("parallel"),
    )(x, *gains)


def rms_bwd(x, dres, branches, name):
    T, D = x.shape
    nb = len(branches)
    tt = _rows(T, 1024 if nb == 1 else 512)

    def body(*refs):
        x_ref, r_ref = refs[0], refs[1]
        dy_refs = refs[2:2 + nb]
        g_refs = refs[2 + nb:2 + 2 * nb]
        dx_ref = refs[2 + 2 * nb]
        dg_refs = refs[3 + 2 * nb:]
        i = pl.program_id(0)
        xf = x_ref[...]
        r = lax.rsqrt(jnp.mean(xf * xf, axis=-1, keepdims=True) + EPS)
        xh = xf * r
        dx = r_ref[...]
        for dy_ref, g_ref, dg_ref in zip(dy_refs, g_refs, dg_refs):
            dy = dy_ref[...].astype(F32)
            dyg = dy * g_ref[...]
            dx = dx + r * (dyg - xh * jnp.mean(dyg * xh, axis=-1, keepdims=True))
            part = jnp.sum(dy * xh, axis=0, keepdims=True)

            @pl.when(i == 0)
            def _():
                dg_ref[...] = part

            @pl.when(i > 0)
            def _():
                dg_ref[...] += part
        dx_ref[...] = dx

    row = pl.BlockSpec((tt, D), lambda i: (i, 0))
    gsp = pl.BlockSpec((1, D), lambda i: (0, 0))
    outs = pl.pallas_call(
        body, name=name, grid=(T // tt,),
        in_specs=[row, row] + [row] * nb + [gsp] * nb,
        out_specs=[row] + [gsp] * nb,
        out_shape=[jax.ShapeDtypeStruct((T, D), F32)] + [jax.ShapeDtypeStruct((1, D), F32)] * nb,
        compiler_params=_params("arbitrary"),
    )(x, dres, *[b[0] for b in branches], *[b[1] for b in branches])
    return outs[0], outs[1:]


def loss_head(x, target, gain):
    T, D = x.shape
    tt = _rows(T, 512)

    def body(x_ref, t_ref, g_ref, dx_ref, dg_ref, loss_ref):
        i = pl.program_id(0)
        xf = x_ref[...]
        g = g_ref[...]
        r = lax.rsqrt(jnp.mean(xf * xf, axis=-1, keepdims=True) + EPS)
        xh = xf * r
        e = xh * g - t_ref[...]
        lpart = 0.5 * jnp.sum(jnp.sum(e * e, axis=1, keepdims=True), axis=0, keepdims=True) / D
        dy = e / D
        dyg = dy * g
        dx_ref[...] = r * (dyg - xh * jnp.mean(dyg * xh, axis=-1, keepdims=True))
        gpart = jnp.sum(dy * xh, axis=0, keepdims=True)
        lrow = jnp.broadcast_to(lpart, (1, 128))

        @pl.when(i == 0)
        def _():
            dg_ref[...] = gpart
            loss_ref[...] = lrow

        @pl.when(i > 0)
        def _():
            dg_ref[...] += gpart
            loss_ref[...] += lrow

    row = pl.BlockSpec((tt, D), lambda i: (i, 0))
    gsp = pl.BlockSpec((1, D), lambda i: (0, 0))
    return pl.pallas_call(
        body, name="loss_head", grid=(T // tt,),
        in_specs=[row, row, gsp],
        out_specs=[row, gsp, pl.BlockSpec((1, 128), lambda i: (0, 0))],
        out_shape=[jax.ShapeDtypeStruct((T, D), F32), jax.ShapeDtypeStruct((1, D), F32),
                   jax.ShapeDtypeStruct((1, 128), F32)],
        compiler_params=_params("arbitrary"),
    )(x, target, gain)


def _shift_down(u, prev8, first, k):
    rolled = pltpu.roll(u, k, 0)
    rid = lax.broadcasted_iota(jnp.int32, u.shape, 0)
    halo = jnp.where(first, 0.0, prev8)
    out = rolled
    for j in range(k):
        out = jnp.where(rid == j, halo[8 - k + j:8 - k + j + 1, :], out)
    return out


def _conv3(u, prev8, first, w, b):
    return (_shift_down(u, prev8, first, 2) * w[0:1, :] + _shift_down(u, prev8, first, 1) * w[1:2, :]
            + u * w[2:3, :] + b)


def ffn_up_act(xn, w_up, w, b, name, ride=None):
    T, K = xn.shape
    tt = _rows(T, 2048)
    nj = D_FF // FF_TC

    def body(x_ref, wu_ref, w_ref, b_ref, u_ref, o_ref, tail):
        first = pl.program_id(1) == 0
        u = _dot(x_ref[...], wu_ref[...])
        u_ref[...] = u
        c = _conv3(u, tail[...], first, w_ref[...], b_ref[...])
        tail[...] = u[tt - 8:, :]
        cg, cv = c[:, :FF_TC], c[:, FF_TC:]
        o_ref[...] = (cg * _sigmoid_tanh(cg) * cv).astype(o_ref.dtype)

    return _call(
        body, name, (nj, T // tt),
        [pl.BlockSpec((tt, K), lambda j, i: (i, 0)),
         pl.BlockSpec((K, 2 * FF_TC), lambda j, i: (0, j)),
         pl.BlockSpec((3, 2 * FF_TC), lambda j, i: (0, j)),
         pl.BlockSpec((1, 2 * FF_TC), lambda j, i: (0, j))],
        [pl.BlockSpec((tt, 2 * FF_TC), lambda j, i: (i, j)), pl.BlockSpec((tt, FF_TC), lambda j, i: (i, j))],
        [jax.ShapeDtypeStruct((T, 2 * D_FF), F32), jax.ShapeDtypeStruct((T, D_FF), MM_DTYPE)],
        [pltpu.VMEM((8, 2 * FF_TC), F32)], ("parallel", "arbitrary"), (xn, w_up, w, b), ride)


def conv_act_bwd(u, da, w, b, name, ride=None):
    T = u.shape[0]
    tt = _rows(T, 2048)
    nt = T // tt
    nj = D_FF // FF_TC
    te = tt + 8

    def body(u_ref, p_ref, n_ref, da_ref, dan_ref, w_ref, b_ref, du_ref, dw_ref, db_ref):
        i = pl.program_id(1)
        first = i == 0
        last = i == nt - 1
        w = w_ref[...]
        ue = jnp.concatenate([u_ref[...], n_ref[...]], axis=0)
        dae = jnp.concatenate([da_ref[...], jnp.where(last, 0.0, dan_ref[...])], axis=0)
        um2 = _shift_down(ue, p_ref[...], first, 2)
        um1 = _shift_down(ue, p_ref[...], first, 1)
        c = um2 * w[0:1, :] + um1 * w[1:2, :] + ue * w[2:3, :] + b_ref[...]
        cg, cv = c[:, :FF_TC], c[:, FF_TC:]
        s = _sigmoid_tanh(cg)
        dcg = dae * cv * (s * (1.0 + cg * (1.0 - s)))
        dcv = dae * (cg * s)
        dc = jnp.concatenate([dcg, dcv], axis=1)
        du = (dc * w[2:3, :] + pltpu.roll(dc, te - 1, 0) * w[1:2, :] + pltpu.roll(dc, te - 2, 0) * w[0:1, :])
        du_ref[...] = du[:tt, :].astype(du_ref.dtype)
        dcm = dc[:tt, :]
        dwp = jnp.concatenate([jnp.sum(dcm * um2[:tt, :], axis=0, keepdims=True),
                               jnp.sum(dcm * um1[:tt, :], axis=0, keepdims=True),
                               jnp.sum(dcm * ue[:tt, :], axis=0, keepdims=True)], axis=0)
        dbp = jnp.sum(dcm, axis=0, keepdims=True)

        @pl.when(first)
        def _():
            dw_ref[...] = dwp
            db_ref[...] = dbp

        @pl.when(i > 0)
        def _():
            dw_ref[...] += dwp
            db_ref[...] += dbp

    nb8 = T // 8
    return _call(
        body, name, (nj, nt),
        [pl.BlockSpec((tt, 2 * FF_TC), lambda j, i: (i, j)),
         pl.BlockSpec((8, 2 * FF_TC), lambda j, i: (jnp.maximum(i * (tt // 8) - 1, 0), j)),
         pl.BlockSpec((8, 2 * FF_TC), lambda j, i: (jnp.minimum((i + 1) * (tt // 8), nb8 - 1), j)),
         pl.BlockSpec((tt, FF_TC), lambda j, i: (i, j)),
         pl.BlockSpec((8, FF_TC), lambda j, i: (jnp.minimum((i + 1) * (tt // 8), nb8 - 1), j)),
         pl.BlockSpec((3, 2 * FF_TC), lambda j, i: (0, j)),
         pl.BlockSpec((1, 2 * FF_TC), lambda j, i: (0, j))],
        [pl.BlockSpec((tt, 2 * FF_TC), lambda j, i: (i, j)),
         pl.BlockSpec((3, 2 * FF_TC), lambda j, i: (0, j)),
         pl.BlockSpec((1, 2 * FF_TC), lambda j, i: (0, j))],
        [jax.ShapeDtypeStruct((T, 2 * D_FF), MM_DTYPE), jax.ShapeDtypeStruct((3, 2 * D_FF), F32),
         jax.ShapeDtypeStruct((1, 2 * D_FF), F32)],
        [], ("parallel", "arbitrary"), (u, u, u, da, da, w, b), ride)


def _interleave(a):
    lead = a.shape[:-1]
    nj = D_FF // FF_TC
    return jnp.swapaxes(a.reshape(*lead, 2, nj, FF_TC), -3, -2).reshape(*lead, 2 * D_FF)


def _deinterleave(a):
    lead = a.shape[:-1]
    nj = D_FF // FF_TC
    return jnp.swapaxes(a.reshape(*lead, nj, 2, FF_TC), -3, -2).reshape(*lead, 2 * D_FF)


A_GC = 1
A_TB = A_GC * A_CHUNK


def gate_prep(z, bias128):
    T = z.shape[0]
    tt = _rows(T, 512)

    def body(z_ref, b_ref, gc_ref, gr_ref):
        pre = z_ref[...] + b_ref[...]
        sc = SOFTCAP * jnp.tanh(pre / SOFTCAP)
        lf = jnp.minimum(sc, 0.0) - jnp.log(1.0 + jnp.exp(-jnp.abs(sc)))
        col = lax.broadcasted_iota(jnp.int32, pre.shape, 1)
        isf = jnp.logical_and(col >= A_HEADS, col < 2 * A_HEADS)
        r = lax.broadcasted_iota(jnp.int32, (tt, tt), 0)
        c = lax.broadcasted_iota(jnp.int32, (tt, tt), 1)
        bits = A_CHUNK.bit_length() - 1
        tri = jnp.logical_and(jnp.right_shift(r, bits) == jnp.right_shift(c, bits), c <= r).astype(F32)
        bcum = jnp.dot(tri, jnp.where(isf, lf, 0.0), precision=HI, preferred_element_type=F32)
        g = jnp.where(col < A_HEADS, sc, jnp.where(isf, bcum, 0.0))
        gc_ref[...] = g
        for s in range(tt // 128):
            gr_ref[s] = g[s * 128:(s + 1) * 128, :].T[0:8, :]

    return pl.pallas_call(
        body, name="gate_prep", grid=(T // tt,),
        in_specs=[pl.BlockSpec((tt, 128), lambda i: (i, GATE_COL // 128)),
                  pl.BlockSpec((1, 128), lambda i: (0, 0))],
        out_specs=[pl.BlockSpec((tt, 128), lambda i: (i, 0)),
                   pl.BlockSpec((tt // 128, 8, 128), lambda i: (i, 0, 0))],
        out_shape=[jax.ShapeDtypeStruct((T, 128), F32), jax.ShapeDtypeStruct((T // 128, 8, 128), F32)],
        compiler_params=_params("parallel"),
    )(z, bias128)


def _chunk_decay(A, qh, bc, br, lir, n, m, causal):
    logD = jnp.where(causal, bc - br + lir, -jnp.inf)
    m_inter = bc + m
    m_t = jnp.maximum(m_inter, jnp.max(logD, axis=1, keepdims=True))
    E = jnp.exp(logD - m_t)
    Sm = A * E
    wi = jnp.exp(m_inter - m_t)
    qn = jnp.sum(qh.astype(F32) * n, axis=1, keepdims=True)
    den = jnp.sum(Sm, axis=1, keepdims=True) + wi * qn
    gs = jnp.maximum(jnp.abs(den), jnp.exp(-m_t))
    return E, Sm, wi, den, gs, m_t


def _state_weights(bc, lic, br, lir, m):
    bL = bc[A_CHUNK - 1:A_CHUNK, :]
    m_new = jnp.maximum(bL + m, jnp.max(bL - br + lir, axis=1, keepdims=True))
    wk = jnp.exp(bL - bc + lic - m_new)
    decay = jnp.exp(bL + m - m_new)
    return wk, decay, m_new


def _head_slices(h):
    return (slice(h * A_QK, (h + 1) * A_QK), slice(h * A_V, (h + 1) * A_V))


def mlstm_fwd(z, gcol, grow, hng, ride=None):
    T = z.shape[0]
    NC = T // A_CHUNK
    scale = A_QK ** -0.5

    def body(q_ref, k_ref, v_ref, o_ref, gc_ref, gr_ref, hng_ref, hg_ref, Cs_ref, ns_ref, ms_ref,
             C_sc, n_sc, m_sc):
        @pl.when(pl.program_id(0) == 0)
        def _():
            C_sc[...] = jnp.zeros_like(C_sc)
            n_sc[...] = jnp.zeros_like(n_sc)
            m_sc[...] = jnp.zeros_like(m_sc)

        ri = lax.broadcasted_iota(jnp.int32, (A_CHUNK, A_CHUNK), 0)
        ci = lax.broadcasted_iota(jnp.int32, (A_CHUNK, A_CHUNK), 1)
        causal = ri >= ci
        gr = jnp.concatenate([gr_ref[s] for s in range(A_TB // 128)], axis=1)
        for c in range(A_GC):
            rows = slice(c * A_CHUNK, (c + 1) * A_CHUNK)
            gc = gc_ref[rows, :]
            grc = gr[:, c * A_CHUNK:(c + 1) * A_CHUNK]
            for h in range(A_HEADS):
                sk, sv = _head_slices(h)
                qh = (q_ref[rows, sk] * scale).astype(MM_DTYPE)
                kh = k_ref[rows, sk].astype(MM_DTYPE)
                vh = v_ref[rows, sv].astype(MM_DTYPE)
                lic, bc = gc[:, h:h + 1], gc[:, A_HEADS + h:A_HEADS + h + 1]
                lir, br = grc[h:h + 1, :], grc[A_HEADS + h:A_HEADS + h + 1, :]
                C, n, m = C_sc[h], n_sc[h], m_sc[h][:, 0:1]
                Cs_ref[c, h] = C
                ns_ref[c, h] = n
                ms_ref[c, h] = m_sc[h]
                _, Sm, wi, _, gs, _ = _chunk_decay(_dot_nt(qh, kh), qh, bc, br, lir, n, m, causal)
                hh = (_dot(Sm, vh) + wi * _dot(qh, C)) / gs
                hn = hh * lax.rsqrt(jnp.mean(hh * hh, axis=1, keepdims=True) + EPS) * hng_ref[:, sv]
                hg_ref[rows, sv] = (hn * _sigmoid(o_ref[rows, sv])).astype(hg_ref.dtype)
                wk, decay, m_new = _state_weights(bc, lic, br, lir, m)
                kw = kh.astype(F32) * wk
                C_sc[h] = decay * C + _dot_tn(kw, vh)
                n_sc[h] = decay * n + jnp.sum(kw, axis=0, keepdims=True)
                m_sc[h] = jnp.broadcast_to(m_new, (1, 128))

    tok = lambda w, cb: pl.BlockSpec((A_TB, w), lambda i: (i, cb))
    return _call(
        body, "mlstm_fwd", (NC // A_GC,),
        [tok(512, 0), tok(512, 1), tok(1024, 1), tok(1024, 2),
         pl.BlockSpec((A_TB, 128), lambda i: (i, 0)),
         pl.BlockSpec((A_TB // 128, 8, 128), lambda i: (i, 0, 0)),
         pl.BlockSpec((1, 1024), lambda i: (0, 0))],
        [pl.BlockSpec((A_TB, 1024), lambda i: (i, 0)),
         pl.BlockSpec((A_GC, A_HEADS, A_QK, A_V), lambda i: (i, 0, 0, 0)),
         pl.BlockSpec((A_GC, A_HEADS, 1, 128), lambda i: (i, 0, 0, 0)),
         pl.BlockSpec((A_GC, A_HEADS, 1, 128), lambda i: (i, 0, 0, 0))],
        [jax.ShapeDtypeStruct((T, 1024), MM_DTYPE),
         jax.ShapeDtypeStruct((NC, A_HEADS, A_QK, A_V), F32),
         jax.ShapeDtypeStruct((NC, A_HEADS, 1, 128), F32),
         jax.ShapeDtypeStruct((NC, A_HEADS, 1, 128), F32)],
        [pltpu.VMEM((A_HEADS, A_QK, A_V), F32), pltpu.VMEM((A_HEADS, 1, 128), F32),
         pltpu.VMEM((A_HEADS, 1, 128), F32)],
        ("arbitrary",), (z, z, z, z, gcol, grow, hng), ride)


def mlstm_bwd(z, gcol, grow, hng, bias128, Cs, ns, ms, dhg, ride=None):
    T = z.shape[0]
    NC = T // A_CHUNK
    nsteps = NC // A_GC
    scale = A_QK ** -0.5

    def body(q_ref, k_ref, v_ref, o_ref, zg_ref, gc_ref, gr_ref, hng_ref, b_ref, Cs_ref, ns_ref, ms_ref,
             dhg_ref, dz_ref, dgn_ref, dbif_ref, dC_sc, dn_sc):
        @pl.when(pl.program_id(0) == 0)
        def _():
            dC_sc[...] = jnp.zeros_like(dC_sc)
            dn_sc[...] = jnp.zeros_like(dn_sc)
            dgn_ref[...] = jnp.zeros_like(dgn_ref)
            dbif_ref[...] = jnp.zeros_like(dbif_ref)

        ri = lax.broadcasted_iota(jnp.int32, (A_CHUNK, A_CHUNK), 0)
        ci = lax.broadcasted_iota(jnp.int32, (A_CHUNK, A_CHUNK), 1)
        causal = ri >= ci
        upper = (ci >= ri).astype(F32)
        rid = lax.broadcasted_iota(jnp.int32, (A_CHUNK, 1), 0)
        col = lax.broadcasted_iota(jnp.int32, (A_CHUNK, 128), 1)
        gr = jnp.concatenate([gr_ref[s] for s in range(A_TB // 128)], axis=1)
        for c in reversed(range(A_GC)):
            rows = slice(c * A_CHUNK, (c + 1) * A_CHUNK)
            gc = gc_ref[rows, :]
            grc = gr[:, c * A_CHUNK:(c + 1) * A_CHUNK]
            dG = jnp.zeros((A_CHUNK, 128), F32)
            hs = []
            for h in range(A_HEADS):
                sk, sv = _head_slices(h)
                s = dict(sk=sk, sv=sv, qh=(q_ref[rows, sk] * scale).astype(MM_DTYPE),
                         kh=k_ref[rows, sk].astype(MM_DTYPE), vh=v_ref[rows, sv].astype(MM_DTYPE),
                         lic=gc[:, h:h + 1], bc=gc[:, A_HEADS + h:A_HEADS + h + 1],
                         lir=grc[h:h + 1, :], br=grc[A_HEADS + h:A_HEADS + h + 1, :],
                         C=Cs_ref[c, h], n=ns_ref[c, h], m=ms_ref[c, h][:, 0:1], dC=dC_sc[h], dn=dn_sc[h])
                s['qf'], s['kf'] = s['qh'].astype(F32), s['kh'].astype(F32)
                s['wk'], s['decay'], _ = _state_weights(s['bc'], s['lic'], s['br'], s['lir'], s['m'])
                hs.append(s)
            for s in hs:
                s['A'] = _dot_nt(s['qh'], s['kh'])
                s['qC'] = _dot(s['qh'], s['C'])
                s['vdC'] = _dot_nt(s['vh'], s['dC'])
                s['kdC'] = _dot(s['kh'], s['dC'])
            for s in hs:
                s['E'], s['Sm'], s['wi'], s['den'], s['gs'], s['m_t'] = _chunk_decay(
                    s['A'], s['qh'], s['bc'], s['br'], s['lir'], s['n'], s['m'], causal)
            for s in hs:
                s['num'] = _dot(s['Sm'], s['vh']) + s['wi'] * s['qC']
            for h, s in enumerate(hs):
                sv, gs = s['sv'], s['gs']
                hh = s['num'] / gs
                r = lax.rsqrt(jnp.mean(hh * hh, axis=1, keepdims=True) + EPS)
                gn = hng_ref[:, sv]
                sg = _sigmoid(o_ref[rows, sv])
                dhg_h = dhg_ref[rows, sv]
                dhn = dhg_h * sg
                dz_ref[rows, 2048 + h * A_V:2048 + (h + 1) * A_V] = (
                    dhg_h * (hh * r * gn) * sg * (1.0 - sg)).astype(dz_ref.dtype)
                dgn_ref[:, sv] += jnp.sum(dhn * hh * r, axis=0, keepdims=True)
                dyg = dhn * gn
                dh = r * dyg - hh * (r * r * r) * jnp.mean(dyg * hh, axis=1, keepdims=True)
                s['dnum'] = dh / gs
                live = (jnp.abs(s['den']) > jnp.exp(-s['m_t'])).astype(F32)
                s['dden'] = -jnp.sum(dh * hh, axis=1, keepdims=True) / gs * jnp.sign(s['den']) * live
            for s in hs:
                s['dnv'] = _dot_nt(s['dnum'], s['vh'])
                s['dnC'] = _dot_nt(s['dnum'], s['C'])
            for s in hs:
                s['dSE'] = jnp.where(causal, s['dnv'] + s['dden'], 0.0) * s['E']
            for s in hs:
                s['dq'] = _dot(s['dSE'], s['kh']) + s['wi'] * (s['dnC'] + s['dden'] * s['n'])
                s['dk_inter'] = s['wk'] * (s['vdC'] + s['dn'])
                s['dk'] = _dot_tn(s['dSE'], s['qh']) + s['dk_inter']
                s['dv'] = _dot_tn(s['Sm'], s['dnum']) + s['wk'] * s['kdC']
                s['dCq'] = _dot_tn(s['qf'] * s['wi'], s['dnum'])
            for h, s in enumerate(hs):
                dq, dk, qf, kf, dC, dn = s['dq'], s['dk'], s['qf'], s['kf'], s['dC'], s['dn']
                dz_ref[rows, s['sk']] = (dq * scale).astype(dz_ref.dtype)
                dz_ref[rows, 512 + h * A_QK:512 + (h + 1) * A_QK] = dk.astype(dz_ref.dtype)
                dz_ref[rows, 1024 + h * A_V:1024 + (h + 1) * A_V] = s['dv'].astype(dz_ref.dtype)
                dli = jnp.sum(kf * dk, axis=1, keepdims=True)
                db = jnp.sum(qf * dq, axis=1, keepdims=True) - dli
                usum = jnp.sum(jnp.sum(kf * s['dk_inter'], axis=1, keepdims=True), axis=0, keepdims=True)
                ddecay = (jnp.sum(jnp.sum(dC * s['C'], axis=1, keepdims=True), axis=0, keepdims=True)
                          + jnp.sum(dn * s['n'], axis=1, keepdims=True))
                db = db + jnp.where(rid == A_CHUNK - 1, usum + ddecay * s['decay'], 0.0)
                dG = dG + jnp.where(col == h, dli, 0.0) + jnp.where(col == A_HEADS + h, db, 0.0)
                dC_sc[h] = s['decay'] * dC + s['dCq']
                dn_sc[h] = s['decay'] * dn + jnp.sum(qf * (s['wi'] * s['dden']), axis=0, keepdims=True)
            dlf = jnp.dot(upper, dG, precision=HI, preferred_element_type=F32)
            pre = zg_ref[rows, :] + b_ref[...]
            th = jnp.tanh(pre / SOFTCAP)
            dcap = 1.0 - th * th
            dpre = jnp.where(col < A_HEADS, dG * dcap,
                             jnp.where(col < 2 * A_HEADS, dlf * _sigmoid(-SOFTCAP * th) * dcap, 0.0))
            dz_ref[rows, GATE_COL:GATE_COL + 128] = dpre.astype(dz_ref.dtype)
            dbif_ref[...] += jnp.sum(dpre, axis=0, keepdims=True)

    rev = lambda i: nsteps - 1 - i
    tok = lambda w, cb: pl.BlockSpec((A_TB, w), lambda i: (rev(i), cb))
    st = lambda a, b: pl.BlockSpec((A_GC, A_HEADS, a, b), lambda i: (rev(i), 0, 0, 0))
    return _call(
        body, "mlstm_bwd", (nsteps,),
        [tok(512, 0), tok(512, 1), tok(1024, 1), tok(1024, 2), tok(128, GATE_COL // 128),
         pl.BlockSpec((A_TB, 128), lambda i: (rev(i), 0)),
         pl.BlockSpec((A_TB // 128, 8, 128), lambda i: (rev(i), 0, 0)),
         pl.BlockSpec((1, 1024), lambda i: (0, 0)),
         pl.BlockSpec((1, 128), lambda i: (0, 0)),
         st(A_QK, A_V), st(1, 128), st(1, 128),
         pl.BlockSpec((A_TB, 1024), lambda i: (rev(i), 0))],
        [pl.BlockSpec((A_TB, A_IN_PAD), lambda i: (rev(i), 0)),
         pl.BlockSpec((1, 1024), lambda i: (0, 0)),
         pl.BlockSpec((1, 128), lambda i: (0, 0))],
        [jax.ShapeDtypeStruct((T, A_IN_PAD), MM_DTYPE), jax.ShapeDtypeStruct((1, 1024), F32),
         jax.ShapeDtypeStruct((1, 128), F32)],
        [pltpu.VMEM((A_HEADS, A_QK, A_V), F32), pltpu.VMEM((A_HEADS, 1, 128), F32)],
        ("arbitrary",), (z, z, z, z, z, gcol, grow, hng, bias128, Cs, ns, ms, dhg), ride)


def _t5_bucket(dist):
    max_exact = REL_BUCKETS // 2
    d = np.maximum(dist, 0)
    log_ratio = np.log(np.maximum(d, 1) / max_exact) / math.log(REL_MAX_DIST / max_exact)
    large = np.minimum(max_exact + (log_ratio * (REL_BUCKETS - max_exact)).astype(np.int64), REL_BUCKETS - 1)
    return np.where(d < max_exact, d, large).astype(np.int32)


def _group_bucket(g):
    delta = B_BLOCK + np.arange(B_BLOCK)[:, None] - np.arange(2 * B_BLOCK)[None, :]
    return _t5_bucket(delta * DILATIONS[g])


def _band_mask(n):
    ri = lax.broadcasted_iota(jnp.int32, (B_BLOCK, 2 * B_BLOCK), 0)
    ci = lax.broadcasted_iota(jnp.int32, (B_BLOCK, 2 * B_BLOCK), 1)
    band = jnp.logical_and(ci >= ri, ci <= ri + B_BLOCK)
    return jnp.logical_and(band, jnp.logical_or(ci >= B_BLOCK, n > 0))


def _both(p_ref, c_ref, sl):
    return jnp.concatenate([p_ref[:, sl], c_ref[:, sl]], axis=0)


def _scores(qh, kh, bias_h, valid):
    return jnp.where(valid, _dot_nt(qh, kh) * (B_DH ** -0.5) + bias_h, -jnp.inf)


def _attn_specs():
    wide = pl.BlockSpec((B_BLOCK, 1024), lambda r, n: (n, r))
    prev = pl.BlockSpec((B_BLOCK, 1024), lambda r, n: (jnp.maximum(n - 1, 0), r))
    narrow = pl.BlockSpec((B_BLOCK, 128), lambda r, n: (n, r))
    bias = pl.BlockSpec((B_HEADS, B_BLOCK, 2 * B_BLOCK), lambda r, n: (0, 0, 0))
    return wide, prev, narrow, bias


def _to_view(read_chunk, sc, o_ref, dil, nc, tt):
    for c in range(nc):
        sc[c] = read_chunk(c)
    for r in range(dil):
        for c in range(nc):
            lo = (r * nc + c) * 128
            o_ref[:, lo:lo + 128] = sc[c, pl.ds(r, tt // dil, stride=dil), :].astype(o_ref.dtype)


def _from_view(read_view, sc, dil, nc, tt):
    for r in range(dil):
        for c in range(nc):
            sc[c, pl.ds(r, tt // dil, stride=dil), :] = read_view((r * nc + c) * 128).astype(F32)


def attn_fwd(qv, kvw, vvw, bias, g):
    dil = DILATIONS[g]
    Tv = qv.shape[0]
    nb = Tv // B_BLOCK
    wide, prev, narrow, bsp = _attn_specs()

    def body(q_ref, kp_ref, kc_ref, vp_ref, vc_ref, b_ref, o_ref, lse_ref):
        valid = _band_mask(pl.program_id(1))
        lse_ref[...] = jnp.zeros_like(lse_ref)
        heads = [slice(h * B_DH, (h + 1) * B_DH) for h in range(B_HEADS)]
        S = [_scores(q_ref[:, sl], _both(kp_ref, kc_ref, sl), b_ref[h], valid) for h, sl in enumerate(heads)]
        P, L = [], []
        for h in range(B_HEADS):
            m = jnp.max(S[h], axis=1, keepdims=True)
            p = jnp.exp(S[h] - m)
            l = jnp.sum(p, axis=1, keepdims=True)
            lse_ref[:, h:h + 1] = m + jnp.log(l)
            P.append(p.astype(MM_DTYPE))
            L.append(l)
        for h, sl in enumerate(heads):
            o_ref[:, sl] = _dot(P[h], _both(vp_ref, vc_ref, sl)) / L[h]

    return pl.pallas_call(
        body, name=f"attn_fwd_g{g}", grid=(dil, nb),
        in_specs=[wide, prev, wide, prev, wide, bsp], out_specs=[wide, narrow],
        out_shape=[jax.ShapeDtypeStruct((Tv, dil * 1024), F32), jax.ShapeDtypeStruct((Tv, dil * 128), F32)],
        compiler_params=_params("parallel", "parallel"),
    )(qv, kvw, kvw, vvw, vvw, bias)


def attn_bwd(qv, kvw, vvw, bias, do_v, lse_v, dl_v, g):
    dil = DILATIONS[g]
    Tv = qv.shape[0]
    nb = Tv // B_BLOCK
    wide, prev, narrow, bsp = _attn_specs()

    def body(q_ref, kp_ref, kc_ref, vp_ref, vc_ref, b_ref, bt_ref, do_ref, lse_ref, dl_ref,
             dq_ref, dkc_ref, dkp_ref, dvc_ref, dvp_ref, db_ref):
        @pl.when(jnp.logical_and(pl.program_id(0) == 0, pl.program_id(1) == 0))
        def _():
            db_ref[...] = jnp.zeros_like(db_ref)

        n = pl.program_id(1)
        valid = _band_mask(n)
        ki = lax.broadcasted_iota(jnp.int32, (2 * B_BLOCK, B_BLOCK), 0)
        qi = lax.broadcasted_iota(jnp.int32, (2 * B_BLOCK, B_BLOCK), 1)
        valid_t = jnp.logical_and(jnp.logical_and(ki >= qi, ki <= qi + B_BLOCK), jnp.logical_or(ki >= B_BLOCK, n > 0))
        lse_t, dl_t = lse_ref[...].T, dl_ref[...].T
        heads = [slice(h * B_DH, (h + 1) * B_DH) for h in range(B_HEADS)]
        scale = B_DH ** -0.5
        PT, DS, DST = [], [], []
        for h, sl in enumerate(heads):
            qh, doh = q_ref[:, sl], do_ref[:, sl].astype(MM_DTYPE)
            kh, vh = _both(kp_ref, kc_ref, sl), _both(vp_ref, vc_ref, sl)
            p = jnp.exp(_scores(qh, kh, b_ref[h], valid) - lse_ref[:, h:h + 1])
            ds = p * (_dot_nt(doh, vh) - dl_ref[:, h:h + 1])
            db_ref[h] += ds
            DS.append((ds * scale).astype(MM_DTYPE))
            pt = jnp.exp(_scores(kh, qh, bt_ref[h], valid_t) - lse_t[h:h + 1, :])
            PT.append(pt.astype(MM_DTYPE))
            DST.append((pt * (_dot_nt(vh, doh) - dl_t[h:h + 1, :]) * scale).astype(MM_DTYPE))
        for h, sl in enumerate(heads):
            qh, doh = q_ref[:, sl], do_ref[:, sl].astype(MM_DTYPE)
            dq_ref[:, sl] = _dot(DS[h], _both(kp_ref, kc_ref, sl)).astype(MM_DTYPE)
            dk = _dot(DST[h], qh).astype(MM_DTYPE)
            dv = _dot(PT[h], doh).astype(MM_DTYPE)
            dkp_ref[:, sl], dkc_ref[:, sl] = dk[:B_BLOCK], dk[B_BLOCK:]
            dvp_ref[:, sl], dvc_ref[:, sl] = dv[:B_BLOCK], dv[B_BLOCK:]

    big = jax.ShapeDtypeStruct((Tv, dil * 1024), MM_DTYPE)
    bsp_t = pl.BlockSpec((B_HEADS, 2 * B_BLOCK, B_BLOCK), lambda r, n: (0, 0, 0))
    return pl.pallas_call(
        body, name=f"attn_bwd_g{g}", grid=(dil, nb),
        in_specs=[wide, prev, wide, prev, wide, bsp, bsp_t, wide, narrow, narrow],
        out_specs=[wide] * 5 + [bsp],
        out_shape=[big] * 5 + [jax.ShapeDtypeStruct((B_HEADS, B_BLOCK, 2 * B_BLOCK), F32)],
        compiler_params=_params("arbitrary", "arbitrary"),
    )(qv, kvw, kvw, vvw, vvw, bias, jnp.swapaxes(bias, 1, 2), do_v, lse_v, dl_v)


def _head_expand():
    e = np.zeros((128, 1024), np.float32)
    for h in range(B_HEADS):
        e[h, h * B_DH:(h + 1) * B_DH] = 1.0
    return e


A_TT = 256
A_TT_WIDE = 512


def _view_spec(dil, width, tt=A_TT):
    return pl.BlockSpec((tt // dil, dil * width), lambda i: (i, 0))


def attn_merge(os_v, lses_v):
    T = os_v[0].shape[0]
    tt = A_TT_WIDE
    expand = jnp.asarray(_head_expand())

    def body(o0, o1, o2, l0, l1, l2, e_ref, ob_ref, of_ref, lse0_ref, lse1_ref, lse2_ref, sc_o, sc_l):
        for gi, (o_ref, l_ref) in enumerate(((o1, l1), (o2, l2))):
            dil = DILATIONS[gi + 1]
            _from_view(lambda lo: o_ref[:, lo:lo + 128], sc_o.at[gi], dil, 8, tt)
            _from_view(lambda lo: l_ref[:, lo:lo + 128], sc_l.at[gi], dil, 1, tt)
        ls = [l0[...], sc_l[0, 0], sc_l[1, 0]]
        m = jnp.maximum(jnp.maximum(ls[0], ls[1]), ls[2])
        ex = [jnp.exp(l - m) for l in ls]
        tot = ex[0] + ex[1] + ex[2]
        lse = m + jnp.log(tot)
        lse0_ref[...] = lse
        _to_view(lambda c: lse, sc_l.at[2], lse1_ref, DILATIONS[1], 1, tt)
        _to_view(lambda c: lse, sc_l.at[2], lse2_ref, DILATIONS[2], 1, tt)
        ws = [e / tot for e in ex]
        for c in range(8):
            cols = slice(c * 128, (c + 1) * 128)
            ecol = e_ref[:, cols]
            spread = [jnp.dot(w, ecol, precision=HI, preferred_element_type=F32) for w in ws]
            out = spread[0] * o0[:, cols] + spread[1] * sc_o[0, c] + spread[2] * sc_o[1, c]
            of_ref[:, cols] = out
            ob_ref[:, cols] = out.astype(ob_ref.dtype)

    wide = pl.BlockSpec((tt, 1024), lambda i: (i, 0))
    return pl.pallas_call(
        body, name="attn_merge", grid=(T // tt,),
        in_specs=[_view_spec(d, 1024, tt) for d in DILATIONS] + [_view_spec(d, 128, tt) for d in DILATIONS]
        + [pl.BlockSpec((128, 1024), lambda i: (0, 0))],
        out_specs=[wide, wide] + [_view_spec(d, 128, tt) for d in DILATIONS],
        out_shape=[jax.ShapeDtypeStruct((T, 1024), MM_DTYPE), jax.ShapeDtypeStruct((T, 1024), F32)]
        + [jax.ShapeDtypeStruct((T // d, d * 128), F32) for d in DILATIONS],
        scratch_shapes=[pltpu.VMEM((2, 8, tt, 128), F32), pltpu.VMEM((3, 1, tt, 128), F32)],
        compiler_params=_params("parallel"),
    )(*os_v, *lses_v, expand)


def attn_prep(datt, out):
    T = datt.shape[0]
    tt = A_TT_WIDE
    expand_t = jnp.asarray(_head_expand().T.copy())

    def body(d_ref, o_ref, e_ref, do0, do1, do2, dl0, dl1, dl2, sc_d, sc_l):
        delta = jnp.dot(d_ref[...] * o_ref[...], e_ref[...], precision=HI, preferred_element_type=F32)
        do0[...] = d_ref[...].astype(do0.dtype)
        dl0[...] = delta
        for do_ref, dl_ref, dil in ((do1, dl1, DILATIONS[1]), (do2, dl2, DILATIONS[2])):
            _to_view(lambda c: d_ref[:, c * 128:(c + 1) * 128], sc_d, do_ref, dil, 8, tt)
            _to_view(lambda c: delta, sc_l, dl_ref, dil, 1, tt)

    wide = pl.BlockSpec((tt, 1024), lambda i: (i, 0))
    return pl.pallas_call(
        body, name="attn_prep", grid=(T // tt,),
        in_specs=[wide, wide, pl.BlockSpec((1024, 128), lambda i: (0, 0))],
        out_specs=[_view_spec(d, 1024, tt) for d in DILATIONS] + [_view_spec(d, 128, tt) for d in DILATIONS],
        out_shape=[jax.ShapeDtypeStruct((T // d, d * 1024), MM_DTYPE) for d in DILATIONS]
        + [jax.ShapeDtypeStruct((T // d, d * 128), F32) for d in DILATIONS],
        scratch_shapes=[pltpu.VMEM((8, tt, 128), F32), pltpu.VMEM((1, tt, 128), F32)],
        compiler_params=_params("parallel"),
    )(datt, out, expand_t)


def attn_combine(parts):
    T = parts[0][0].shape[0]
    tt = A_TT
    nt = T // tt
    shift = [None] + [B_BLOCK * d // tt for d in DILATIONS[1:]]

    def body(dq0, kc0, vc0, kpa0, kpb0, vpa0, vpb0, dq1, kc1, kp1, vc1, vp1, dq2, kc2, kp2, vc2, vp2,
             dq_ref, dkv_ref, sc):
        i = pl.program_id(0)
        dq_ref[:, 0:1024] = dq0[...].astype(dq_ref.dtype)
        for col, c_ref, pa_ref, pb_ref in ((0, kc0, kpa0, kpb0), (3, vc0, vpa0, vpb0)):
            nxt = jnp.where(i + 1 < nt, pb_ref[:tt // 2, :].astype(F32), 0.0)
            later = jnp.concatenate([pa_ref[tt // 2:, :].astype(F32), nxt], axis=0)
            dkv_ref[:, col * 1024:(col + 1) * 1024] = (c_ref[...].astype(F32) + later).astype(dkv_ref.dtype)
        for g, (dq, kc, kp, vc, vp) in ((1, (dq1, kc1, kp1, vc1, vp1)), (2, (dq2, kc2, kp2, vc2, vp2))):
            dil = DILATIONS[g]
            live = i + shift[g] < nt
            _from_view(lambda lo: dq[:, lo:lo + 128], sc, dil, 8, tt)
            for c in range(8):
                dq_ref[:, g * 1024 + c * 128:g * 1024 + (c + 1) * 128] = sc[c].astype(dq_ref.dtype)
            for col, c_ref, p_ref in ((g, kc, kp), (3 + g, vc, vp)):
                _from_view(lambda lo: c_ref[:, lo:lo + 128].astype(F32)
                           + jnp.where(live, p_ref[:, lo:lo + 128].astype(F32), 0.0), sc, dil, 8, tt)
                for c in range(8):
                    dkv_ref[:, col * 1024 + c * 128:col * 1024 + (c + 1) * 128] = sc[c].astype(dkv_ref.dtype)

    def later_spec(dil, blocks):
        return pl.BlockSpec((tt // dil, dil * 1024), lambda i: (jnp.minimum(i + blocks, nt - 1), 0))

    cur = [_view_spec(d, 1024) for d in DILATIONS]
    in_specs = [cur[0], cur[0], cur[0], cur[0], later_spec(1, 1), cur[0], later_spec(1, 1)]
    args = [parts[0][0], parts[0][1], parts[0][3], parts[0][2], parts[0][2], parts[0][4], parts[0][4]]
    for g in (1, 2):
        in_specs += [cur[g], cur[g], later_spec(DILATIONS[g], shift[g]), cur[g], later_spec(DILATIONS[g], shift[g])]
        args += list(parts[g][:5])
    return pl.pallas_call(
        body, name="attn_combine", grid=(nt,), in_specs=in_specs,
        out_specs=[pl.BlockSpec((tt, 3072), lambda i: (i, 0)), pl.BlockSpec((tt, 6144), lambda i: (i, 0))],
        out_shape=[jax.ShapeDtypeStruct((T, 3072), MM_DTYPE), jax.ShapeDtypeStruct((T, 6144), MM_DTYPE)],
        scratch_shapes=[pltpu.VMEM((8, tt, 128), F32)],
        compiler_params=_params("parallel"),
    )(*args)


def adamw(w, g, m, v, name):
    R, C = w.shape
    tr = R if R * C * 4 <= (1 << 20) else _rows(R, max(8, ((1 << 20) // (C * 4)) // 8 * 8))

    def body(w_ref, g_ref, m_ref, v_ref, d_ref, nm_ref, nv_ref):
        gg = g_ref[...]
        nm = ADAM_B1 * m_ref[...] + (1.0 - ADAM_B1) * gg
        nv = ADAM_B2 * v_ref[...] + (1.0 - ADAM_B2) * (gg * gg)
        m_hat = nm / (1.0 - ADAM_B1 ** ADAM_STEP)
        v_hat = nv / (1.0 - ADAM_B2 ** ADAM_STEP)
        d_ref[...] = -ADAM_LR * (m_hat / (jnp.sqrt(v_hat) + ADAM_EPS) + ADAM_WD * w_ref[...])
        nm_ref[...] = nm
        nv_ref[...] = nv

    blk = pl.BlockSpec((tr, C), lambda i: (i, 0))
    sds = jax.ShapeDtypeStruct((R, C), F32)
    return pl.pallas_call(
        body, name=name, grid=(R // tr,), in_specs=[blk] * 4, out_specs=[blk] * 3, out_shape=[sds] * 3,
        compiler_params=_params("parallel"),
    )(w, g, m, v)


def sum_slots(x, name, out_dtype=F32):
    n, R, C = x.shape
    tr = _rows(R, 256)

    def body(x_ref, o_ref):
        acc = x_ref[0].astype(F32)
        for s in range(1, n):
            acc = acc + x_ref[s].astype(F32)
        o_ref[...] = acc.astype(out_dtype)

    return pl.pallas_call(
        body, name=name, grid=(R // tr,),
        in_specs=[pl.BlockSpec((n, tr, C), lambda i: (0, i, 0))],
        out_specs=pl.BlockSpec((tr, C), lambda i: (i, 0)),
        out_shape=jax.ShapeDtypeStruct((R, C), out_dtype),
        compiler_params=_params("parallel"),
    )(x)


_ANY = pl.BlockSpec(memory_space=pl.ANY)
GROUP_ALL = ([(0, 0, 1), (0, 1, 0), (0, 1, 1), (1, 0, 0), (1, 0, 1), (1, 1, 0), (1, 1, 1)],
             lambda d: 4 * d[0] + 2 * d[1] + d[2])
GROUP_CHIPS = ([(0, 1, 0), (1, 0, 0), (1, 1, 0)], lambda d: 2 * d[0] + d[1])
GROUP_SIBLING = ([(0, 0, 1)], lambda d: d[2])


def _me():
    return lax.axis_index("x"), lax.axis_index("y"), lax.axis_index("c")


def _peer(me, flip):
    return tuple(1 - a if f else a for a, f in zip(me, flip))


class Exchange:
    def __init__(self, x, group, scatter):
        self.flips, self.slot = group
        self.scatter = scatter
        self.n = len(self.flips) + 1
        self.out_shape = jax.ShapeDtypeStruct((self.n,) + x.shape[-2:], x.dtype)
        self.scratch = [pltpu.SemaphoreType.DMA((self.n - 1,)), pltpu.SemaphoreType.DMA((self.n - 1,)),
                        pltpu.SemaphoreType.DMA]

    def _copies(self, x_ref, o_ref, send_sems, recv_sems, local_sem, arrivals):
        me = _me()
        slot = self.slot
        mine = pltpu.make_async_copy(x_ref.at[slot(me)] if self.scatter else x_ref, o_ref.at[slot(me)], local_sem)
        sends, landed = [], []
        for k, flip in enumerate(self.flips):
            peer = _peer(me, flip)
            sends.append(pltpu.make_async_remote_copy(
                src_ref=x_ref.at[slot(peer)] if self.scatter else x_ref, dst_ref=o_ref.at[slot(me)],
                send_sem=send_sems.at[k], recv_sem=recv_sems.at[k], device_id=peer, device_id_type=MESH_ID))
            if arrivals:
                landed.append(pltpu.make_async_remote_copy(
                    src_ref=o_ref.at[slot(me)], dst_ref=o_ref.at[slot(peer)], send_sem=send_sems.at[k],
                    recv_sem=recv_sems.at[k], device_id=peer, device_id_type=MESH_ID))
        return mine, sends, landed

    def start(self, *refs):
        mine, sends, _ = self._copies(*refs, arrivals=False)
        mine.start()
        for cp in sends:
            cp.start()

    def wait(self, *refs):
        mine, sends, arrivals = self._copies(*refs, arrivals=True)
        for cp in arrivals:
            cp.wait_recv()
        for cp in sends:
            cp.wait_send()
        mine.wait()

    def __call__(self, x, name):
        def body(*refs):
            self.start(*refs)
            self.wait(*refs)

        return pl.pallas_call(body, name=name, in_specs=[_ANY], out_specs=_ANY, out_shape=self.out_shape,
                              scratch_shapes=self.scratch)(x)


def group_gather(x, name, group):
    return Exchange(x, group, scatter=False)(x, name)


def group_scatter(x, name, group):
    return Exchange(x, group, scatter=True)(x, name)


def _call(body, name, grid, in_specs, out_specs, out_shape, scratch, semantics, args, ride=None):
    if ride is None:
        return pl.pallas_call(body, name=name, grid=grid, in_specs=in_specs, out_specs=out_specs,
                              out_shape=out_shape, scratch_shapes=scratch,
                              compiler_params=_params(*semantics))(*args)
    x, exch = ride
    n_in, n_out, n_scr = len(in_specs), len(out_specs), len(scratch)

    def at_step(pick):
        hit = None
        for axis, size in enumerate(grid):
            here = pl.program_id(axis) == pick(size)
            hit = here if hit is None else jnp.logical_and(hit, here)
        return hit

    def riding(*refs):
        ins, x_ref = refs[:n_in], refs[n_in]
        outs, o_ref = refs[n_in + 1:n_in + 1 + n_out], refs[n_in + 1 + n_out]
        scr, sems = refs[n_in + 2 + n_out:n_in + 2 + n_out + n_scr], refs[n_in + 2 + n_out + n_scr:]

        @pl.when(at_step(lambda size: 0))
        def _():
            exch.start(x_ref, o_ref, *sems)

        body(*ins, *outs, *scr)

        @pl.when(at_step(lambda size: size - 1))
        def _():
            exch.wait(x_ref, o_ref, *sems)

    return pl.pallas_call(
        riding, name=name, grid=grid, in_specs=list(in_specs) + [_ANY], out_specs=list(out_specs) + [_ANY],
        out_shape=list(out_shape) + [exch.out_shape], scratch_shapes=list(scratch) + exch.scratch,
        compiler_params=_params(*(["arbitrary"] * len(grid))))(*args, x)


WEIGHTS = ['a_norm_g', 'a_w_in', 'a_b_if', 'a_hnorm_g', 'a_w_out', 'kv_norm_g', 'w_kv', 'b_norm_g', 'b_w_q',
           'b_w_out', 'rel_bias', 'f_norm_g', 'f_w_up', 'f_conv_w', 'f_conv_b', 'f_w_down', 'final_norm_g']
SHARD_AXIS = {'a_norm_g': 1, 'a_w_in': 2, 'a_b_if': None, 'a_hnorm_g': 2, 'a_w_out': 1, 'kv_norm_g': None,
              'w_kv': 1, 'b_norm_g': None, 'b_w_q': 2, 'b_w_out': 1, 'rel_bias': None, 'f_norm_g': None,
              'f_w_up': 2, 'f_conv_w': 2, 'f_conv_b': None, 'f_w_down': 1, 'final_norm_g': None}
BIG = ['a_w_in', 'a_w_out', 'w_kv', 'b_w_q', 'b_w_out', 'f_w_up', 'f_w_down']
SMALL = [n for n in WEIGHTS if n not in BIG]
LANES = 1024
PIECES = {'a_w_in': ('a_w_in', None, 2), 'a_w_out': ('a_w_out', None, 1), 'f_w_up0': ('f_w_up', 0, 1),
          'f_w_down0': ('f_w_down', 0, 0), 'w_kv': ('w_kv', None, 1), 'b_w_q': ('b_w_q', None, 2),
          'b_w_out': ('b_w_out', None, 1), 'f_w_up1': ('f_w_up', 1, 1), 'f_w_down1': ('f_w_down', 1, 0)}
LATE = ['w_kv', 'b_w_q', 'b_w_out', 'f_w_up1', 'f_w_down1']
WEIGHT_WAVES = {'first': ['a_w_in'], 'ffn0': ['a_w_out', 'f_w_up0', 'f_w_down0'], 'late': LATE}
GRAD_WAVES = {'late': LATE, 'layer0': ['f_w_up0', 'f_w_down0', 'a_w_out'], 'last': ['a_w_in']}


def _piece(arrays, p):
    leaf, layer, _ = PIECES[p]
    return arrays[leaf] if layer is None else arrays[leaf][layer]


class Packer:
    def __init__(self, pieces, shard):
        self.pieces = pieces
        self.shapes = [_piece(shard, p).shape for p in pieces]
        self.sizes = [math.prod(s) // (2 * LANES) for s in self.shapes]
        self.fill = -sum(self.sizes) % 16
        self.rows = sum(self.sizes) + self.fill

    def my_half(self, shard, half):
        both = jnp.concatenate([_piece(shard, p).astype(MM_DTYPE).reshape(2, -1, LANES) for p in self.pieces], axis=1)
        return jnp.pad(lax.dynamic_index_in_dim(both, half, axis=0, keepdims=False), ((0, self.fill), (0, 0)))

    def full_weights(self, gathered):
        g = gathered.reshape(4, 2, self.rows, LANES)
        out, off = {}, 0
        for p, shp, sz in zip(self.pieces, self.shapes, self.sizes):
            out[p] = _full_from_shards(g[:, :, off:off + sz].reshape((4,) + shp), PIECES[p][2])
            off += sz
        return out

    def grad_slots(self, grads):
        parts = [_shards_from_full(grads[p], PIECES[p][2]).reshape(4, 2, -1, LANES).astype(GRAD_WIRE_DTYPE)
                 for p in self.pieces]
        parts.append(jnp.zeros((4, 2, self.fill, LANES), GRAD_WIRE_DTYPE))
        return jnp.concatenate(parts, axis=2).reshape(8, self.rows, LANES)

    def shard_grads(self, both):
        out, off = {}, 0
        for p, shp, sz in zip(self.pieces, self.shapes, self.sizes):
            out[p] = both[:, off:off + sz].reshape(shp).astype(F32)
            off += sz
        return out


class Overlap:
    def __init__(self, shard, half):
        self.shard, self.half = shard, half
        self.weights = {w: Packer(p, shard) for w, p in WEIGHT_WAVES.items()}
        self.grads = {w: Packer(p, shard) for w, p in GRAD_WAVES.items()}
        self.shard_grads = {}

    def gather_ride(self, wave):
        mine = self.weights[wave].my_half(self.shard, self.half)
        return mine, Exchange(mine, GROUP_ALL, scatter=False)

    def gathered(self, wave, slots):
        return self.weights[wave].full_weights(slots)

    def scatter_ride(self, wave, grads):
        slots = self.grads[wave].grad_slots({p: grads.pop(p) for p in GRAD_WAVES[wave]})
        return slots, Exchange(slots, GROUP_ALL, scatter=True)

    def join_ride(self, wave, received):
        reduced = sum_slots(received, f"sum_grads_{wave}", GRAD_WIRE_DTYPE)
        return reduced, Exchange(reduced, GROUP_SIBLING, scatter=False)

    def joined(self, wave, both):
        self.shard_grads.update(self.grads[wave].shard_grads(both))


def _pad_rows(flat, mult):
    n = flat.shape[0]
    per = LANES * mult
    tot = -(-n // per) * per
    return jnp.pad(flat, (0, tot - n)).reshape(tot // LANES, LANES)


def _full_from_shards(sh, axis):
    shp = sh.shape[1:]
    return jnp.moveaxis(sh, 0, axis).reshape(shp[:axis] + (4 * shp[axis],) + shp[axis + 1:])


def _shards_from_full(full, axis):
    shp = full.shape
    return jnp.moveaxis(full.reshape(shp[:axis] + (4, shp[axis] // 4) + shp[axis + 1:]), axis, 0)


def _local_step(x, target, W, overlap=None):
    T = x.shape[0]
    W = dict(W)
    row = lambda a: a.reshape(1, -1).astype(F32)
    w_in = jnp.pad(W['a_w_in'][0], ((0, 0), (0, A_IN_PAD - A_IN)))
    bias128 = jnp.pad(row(W['a_b_if'][0]), ((0, 0), (0, 120)))
    hng = row(W['a_hnorm_g'][0])
    w_up = lambda l: _interleave(W[f'f_w_up{l}'])
    cw = [_interleave(W['f_conv_w'][l].astype(F32)) for l in range(2)]
    cb = [_interleave(row(W['f_conv_b'][l])) for l in range(2)]
    onehots = [(jnp.asarray(_group_bucket(g).reshape(-1, 1)) == jnp.arange(128)[None, :]).astype(F32)
               for g in range(N_GROUPS)]
    rb_t = jnp.pad(W['rel_bias'].astype(F32).T, ((0, 0), (0, 128 - REL_BUCKETS)))
    biases = [mm_nn(rb_t[g * B_HEADS:(g + 1) * B_HEADS], onehots[g].T, f"rel_bias_table_g{g}", exact=True)
              .reshape(B_HEADS, B_BLOCK, 2 * B_BLOCK) for g in range(N_GROUPS)]
    G = {}

    def ffn_fwd(xin, l, ride=None):
        xn, = rms_fwd(xin, [row(W['f_norm_g'][l])], f"ffn{l}_norm")
        u, act, *rode = ffn_up_act(xn, w_up(l), cw[l], cb[l], f"ffn{l}_up_act", ride)
        return mm_nn(act, W[f'f_w_down{l}'], f"ffn{l}_down", res=xin), (xn, u, act), rode

    def ffn_bwd(xin, saved, dout, l, ride=None):
        xn, u, act = saved
        dact = mm_nn(dout, W[f'f_w_down{l}'].T, f"ffn{l}_ddown")
        G[f'f_w_down{l}'] = mm_tn(act, dout, f"ffn{l}_gdown")
        du, gcw, gcb, *rode = conv_act_bwd(u, dact, cw[l], cb[l], f"ffn{l}_dact", ride)
        dxn = mm_nn(du, w_up(l).T, f"ffn{l}_dup")
        G[f'f_w_up{l}'] = _deinterleave(mm_tn(xn, du, f"ffn{l}_gup"))
        dxin, (gn,) = rms_bwd(xin, dout, [(dxn, row(W['f_norm_g'][l]))], f"ffn{l}_dnorm")
        return dxin, _deinterleave(gcw), _deinterleave(gcb), gn, rode

    xn_a, = rms_fwd(x, [row(W['a_norm_g'][0])], "a_norm")
    z = mm_nn(xn_a, w_in, "a_in")
    gcol, grow = gate_prep(z, bias128)
    hg, Cs, ns, ms, *rode = mlstm_fwd(z, gcol, grow, hng, overlap.gather_ride('ffn0') if overlap else None)
    if overlap:
        W.update(overlap.gathered('ffn0', rode[0]))
    x1 = mm_nn(hg, W['a_w_out'][0], "a_out", res=x)
    x2, ffn0, rode = ffn_fwd(x1, 0, overlap.gather_ride('late') if overlap else None)
    if overlap:
        W.update(overlap.gathered('late', rode[0]))
    xn_kv, xn_b = rms_fwd(x2, [row(W['kv_norm_g']), row(W['b_norm_g'][0])], "b_norms")
    gcols = lambda w, c: w[:, c * 1024:(c + 1) * 1024]
    qv = [mm_view(xn_b, gcols(W['b_w_q'][0], g), f"q_proj_g{g}", DILATIONS[g]) for g in range(N_GROUPS)]
    kvw = [mm_view(xn_kv, gcols(W['w_kv'], g), f"k_proj_g{g}", DILATIONS[g]) for g in range(N_GROUPS)]
    vvw = [mm_view(xn_kv, gcols(W['w_kv'], 3 + g), f"v_proj_g{g}", DILATIONS[g]) for g in range(N_GROUPS)]
    os_, lses = zip(*[attn_fwd(qv[g], kvw[g], vvw[g], biases[g], g) for g in range(N_GROUPS)])
    att, att_f, *lse_v = attn_merge(os_, lses)
    x3 = mm_nn(att, W['b_w_out'][0], "b_out", res=x2)
    x4, ffn1, _ = ffn_fwd(x3, 1)
    dx4, g_final, loss = loss_head(x4, target, row(W['final_norm_g']))
    G['final_norm_g'] = g_final.reshape(-1)

    dx3, gcw1, gcb1, gn1, _ = ffn_bwd(x3, ffn1, dx4, 1)
    datt = mm_nn(dx3, W['b_w_out'][0].T, "b_dout")
    G['b_w_out'] = mm_tn(att, dx3, "b_gout")[None]
    prep = attn_prep(datt, att_f)
    do_v, dl_v = prep[:3], prep[3:]
    parts = [attn_bwd(qv[g], kvw[g], vvw[g], biases[g], do_v[g], lse_v[g], dl_v[g], g) for g in range(N_GROUPS)]
    dq_all, dkv = attn_combine(parts)
    grb = []
    for g in range(N_GROUPS):
        gb = mm_nn(parts[g][5].reshape(B_HEADS, -1), onehots[g], f"rel_bias_g{g}", exact=True)
        grb.append(gb[:, :REL_BUCKETS].T)
    G['rel_bias'] = jnp.concatenate(grb, axis=1)
    dxn_b = mm_nn(dq_all, W['b_w_q'][0].T, "q_dproj")
    G['b_w_q'] = mm_tn(xn_b, dq_all, "q_gproj")[None]
    dxn_kv = mm_nn(dkv, W['w_kv'].T, "kv_dproj")
    G['w_kv'] = mm_tn(xn_kv, dkv, "kv_gproj")
    dx2, (g_kvn, g_bn) = rms_bwd(x2, dx3, [(dxn_kv, row(W['kv_norm_g'])), (dxn_b, row(W['b_norm_g'][0]))],
                                 "b_dnorms")
    G['kv_norm_g'] = g_kvn.reshape(-1)
    G['b_norm_g'] = g_bn
    dx1, gcw0, gcb0, gn0, late_slots = ffn_bwd(x1, ffn0, dx2, 0, overlap.scatter_ride('late', G) if overlap else None)
    G['f_conv_w'] = jnp.stack([gcw0, gcw1])
    G['f_conv_b'] = jnp.concatenate([gcb0, gcb1], axis=0)
    G['f_norm_g'] = jnp.concatenate([gn0, gn1], axis=0)
    dhg = mm_nn(dx1, W['a_w_out'][0].T, "a_dout")
    G['a_w_out'] = mm_tn(hg, dx1, "a_gout")[None]
    dz, g_hn, g_bif, *layer0_slots = mlstm_bwd(z, gcol, grow, hng, bias128, Cs, ns, ms, dhg,
                                               overlap.scatter_ride('layer0', G) if overlap else None)
    G['a_hnorm_g'] = g_hn.reshape(1, A_HEADS, A_V)
    G['a_b_if'] = g_bif[:, :2 * A_HEADS]
    if overlap:
        dxn_a, both = mm_nn(dz, w_in.T, "a_din", ride=overlap.join_ride('late', late_slots[0]))
        overlap.joined('late', both)
        g_in, both = mm_tn(xn_a, dz, "a_gin", ride=overlap.join_ride('layer0', layer0_slots[0]))
        overlap.joined('layer0', both)
    else:
        dxn_a = mm_nn(dz, w_in.T, "a_din")
        g_in = mm_tn(xn_a, dz, "a_gin")
    G['a_w_in'] = g_in[:, :A_IN][None]
    grad_x, (g_an,) = rms_bwd(x, dx1, [(dxn_a, row(W['a_norm_g'][0]))], "a_dnorm")
    G['a_norm_g'] = g_an
    return loss, grad_x, G


def kernel(x, a_norm_g, a_w_in, a_b_if, a_hnorm_g, a_w_out, kv_norm_g, w_kv, b_norm_g, b_w_q, b_w_out, rel_bias, f_norm_g, f_w_up, f_conv_w, f_conv_b, f_w_down, final_norm_g, loss_target, m_a_norm_g, m_a_w_in, m_a_b_if, m_a_hnorm_g, m_a_w_out, m_kv_norm_g, m_w_kv, m_b_norm_g, m_b_w_q, m_b_w_out, m_rel_bias, m_f_norm_g, m_f_w_up, m_f_conv_w, m_f_conv_b, m_f_w_down, m_final_norm_g, v_a_norm_g, v_a_w_in, v_a_b_if, v_a_hnorm_g, v_a_w_out, v_kv_norm_g, v_w_kv, v_b_norm_g, v_b_w_q, v_b_w_out, v_rel_bias, v_f_norm_g, v_f_w_up, v_f_conv_w, v_f_conv_b, v_f_w_down, v_final_norm_g):
    given = dict(locals())
    shard = {n: given[n] for n in WEIGHTS}
    mom = {n: given["m_" + n] for n in WEIGHTS}
    var = {n: given["v_" + n] for n in WEIGHTS}
    cx, cy, cc = _me()
    chip = 2 * cx + cy

    overlap = Overlap(shard, cc)
    mine, gather = overlap.gather_ride('first')
    W = overlap.gathered('first', gather(mine, "gather_weights"))
    sharded_small = [n for n in SMALL if SHARD_AXIS[n] is not None]
    ssz = [shard[n].size for n in sharded_small]
    sflat = jnp.concatenate([shard[n].reshape(-1) for n in sharded_small])
    sg = group_gather(_pad_rows(sflat, 8), "gather_small", GROUP_CHIPS).reshape(4, -1)
    off = 0
    for n, sz in zip(sharded_small, ssz):
        W[n] = _full_from_shards(sg[:, off:off + sz].reshape((4,) + shard[n].shape), SHARD_AXIS[n])
        off += sz
    for n in SMALL:
        if SHARD_AXIS[n] is None:
            W[n] = shard[n]

    loss_row, grad_x, G = _local_step(x[0], loss_target[0], W, overlap)

    slots, scatter = overlap.scatter_ride('last', G)
    reduced, join = overlap.join_ride('last', scatter(slots, "scatter_grads"))
    overlap.joined('last', join(reduced, "join_halves"))
    by_piece = overlap.shard_grads
    gsh = {}
    for n in BIG:
        layers = [p for p in PIECES if PIECES[p][0] == n]
        gsh[n] = by_piece[n] if layers == [n] else jnp.stack([by_piece[p] for p in layers])
    small_parts = [loss_row[0, 0:1]] + [G[n].reshape(-1) for n in SMALL]
    small_sz = [p.shape[0] for p in small_parts]
    small = sum_slots(group_gather(_pad_rows(jnp.concatenate(small_parts), 8), "gather_small_grads", GROUP_ALL),
                      "sum_small_grads").reshape(-1)
    loss = small[0]
    off = 1
    for n, sz in zip(SMALL, small_sz[1:]):
        full = small[off:off + sz].reshape(W[n].shape)
        off += sz
        if SHARD_AXIS[n] is None:
            gsh[n] = full
        else:
            gsh[n] = lax.dynamic_index_in_dim(_shards_from_full(full, SHARD_AXIS[n]), chip, 0, keepdims=False)

    delta, new_m, new_v = {}, {}, {}
    for n in WEIGHTS:
        shp = shard[n].shape
        two = lambda a: a.reshape(-1, shp[-1])
        d, nm, nv = adamw(two(shard[n]), two(gsh[n]), two(mom[n]), two(var[n]), f"adamw_{n}")
        delta[n], new_m[n], new_v[n] = d.reshape(shp), nm.reshape(shp), nv.reshape(shp)
    return (loss, grad_x[None], *[gsh[n] for n in WEIGHTS], *[delta[n] for n in WEIGHTS],
            *[new_m[n] for n in WEIGHTS], *[new_v[n] for n in WEIGHTS])
```

```python
import functools
import math

import numpy as np
import jax
import jax.numpy as jnp
from jax import lax
from jax.experimental import pallas as pl
from jax.experimental.pallas import tpu as pltpu

F32 = jnp.float32
BF16 = jnp.bfloat16
MM_DTYPE = jnp.bfloat16
GRAD_WIRE_DTYPE = jnp.bfloat16
HI = lax.Precision.HIGHEST

D_MODEL = 1024
A_HEADS = 4
A_QK = 128
A_V = 256
A_CHUNK = 256
A_IN = 3080
A_IN_PAD = 3200
GATE_COL = 3072
SOFTCAP = 15.0
N_GROUPS = 3
B_HEADS = 16
B_DH = 64
B_BLOCK = 128
DILATIONS = (1, 4, 16)
WINDOWS = (128, 512, 2048)
REL_BUCKETS = 32
REL_MAX_DIST = 2048
D_FF = 2816
FF_TC = 256
EPS = 1e-6
ADAM_LR, ADAM_B1, ADAM_B2, ADAM_EPS, ADAM_WD, ADAM_STEP = 0.001, 0.9, 0.999, 1e-08, 0.01, 10

VMEM_LIMIT = 56 * 1024 * 1024
NT_DIMS = (((1,), (1,)), ((), ()))
TN_DIMS = (((0,), (0,)), ((), ()))
MESH_ID = pl.DeviceIdType.MESH


def _params(*sem):
    return pltpu.CompilerParams(dimension_semantics=sem, vmem_limit_bytes=VMEM_LIMIT)


def _tile(n, cap):
    if n <= cap:
        return n
    best = None
    for t in range(128, cap + 1, 128):
        if n % t == 0:
            best = t
    assert best is not None, (n, cap)
    return best


def _rows(n, cap):
    if n <= cap:
        return n
    for t in range(cap // 8 * 8, 7, -8):
        if n % t == 0:
            return t
    raise ValueError((n, cap))


def _dot(a, b):
    return jnp.dot(a.astype(MM_DTYPE), b.astype(MM_DTYPE), preferred_element_type=F32)


def _dot_nt(a, b):
    return lax.dot_general(a.astype(MM_DTYPE), b.astype(MM_DTYPE), NT_DIMS, preferred_element_type=F32)


def _dot_tn(a, b):
    return lax.dot_general(a.astype(MM_DTYPE), b.astype(MM_DTYPE), TN_DIMS, preferred_element_type=F32)


def _sigmoid(x):
    return 1.0 / (1.0 + jnp.exp(-x))


def _sigmoid_tanh(x):
    return 0.5 * jnp.tanh(0.5 * x) + 0.5


def mm_nn(a, b, name, res=None, out_dtype=F32, exact=False, ride=None):
    M, K = a.shape
    N = b.shape[1]
    def footprint(tm, tn):
        return 2 * (tm * K * a.dtype.itemsize + K * tn * b.dtype.itemsize) + 2 * tm * tn * 4 * (1 if res is None else 2)

    budget = 46 * 1024 * 1024
    tm = _rows(M, 512)
    tn = N if N <= 3328 and footprint(tm, N) <= budget else _tile(N, 1536)
    tk = K if footprint(tm, tn) <= budget else _tile(K, 1536)
    if tk == K and footprint(_rows(M, 1024), tn) <= budget:
        tm = _rows(M, 1024)
    nk = K // tk

    def body(*refs):
        if res is None:
            a_ref, b_ref, o_ref, acc = refs
            r_ref = None
        else:
            a_ref, b_ref, r_ref, o_ref, acc = refs
        if exact:
            p = jnp.dot(a_ref[...], b_ref[...], precision=HI, preferred_element_type=F32)
        else:
            p = _dot(a_ref[...], b_ref[...])

        def finish(total):
            if r_ref is not None:
                total = total + r_ref[...]
            o_ref[...] = total.astype(out_dtype)

        if nk == 1:
            finish(p)
        else:
            k = pl.program_id(2)

            @pl.when(k == 0)
            def _():
                acc[...] = p

            @pl.when(jnp.logical_and(k > 0, k < nk - 1))
            def _():
                acc[...] += p

            @pl.when(k == nk - 1)
            def _():
                finish(acc[...] + p)

    in_specs = [pl.BlockSpec((tm, tk), lambda j, i, k: (i, k)),
                pl.BlockSpec((tk, tn), lambda j, i, k: (k, j))]
    args = [a, b]
    if res is not None:
        in_specs.append(pl.BlockSpec((tm, tn), lambda j, i, k: (i, j)))
        args.append(res)
    acc_shape = (tm, tn) if nk > 1 else (8, 128)
    outs = _call(body, name, (N // tn, M // tm, nk), in_specs, [pl.BlockSpec((tm, tn), lambda j, i, k: (i, j))],
                 [jax.ShapeDtypeStruct((M, N), out_dtype)], [pltpu.VMEM(acc_shape, F32)],
                 ("parallel", "parallel", "arbitrary"), args, ride)
    return outs[0] if ride is None else outs


def mm_view(a, b, name, dil):
    T, K = a.shape
    tm = 1024

    def body(a_ref, b_ref, o_ref, sc):
        p = _dot(a_ref[...], b_ref[...])
        if dil == 1:
            o_ref[...] = p.astype(o_ref.dtype)
        else:
            _to_view(lambda c: p[:, c * 128:(c + 1) * 128], sc, o_ref, dil, 8, tm)

    return pl.pallas_call(
        body, name=name, grid=(T // tm,),
        in_specs=[pl.BlockSpec((tm, K), lambda i: (i, 0)), pl.BlockSpec((K, 1024), lambda i: (0, 0))],
        out_specs=pl.BlockSpec((tm // dil, dil * 1024), lambda i: (i, 0)),
        out_shape=jax.ShapeDtypeStruct((T // dil, dil * 1024), MM_DTYPE),
        scratch_shapes=[pltpu.VMEM((8, tm, 128), F32)],
        compiler_params=_params("parallel"),
    )(a, b)


def mm_tn(a, g, name, ride=None):
    T, Ka = a.shape
    N = g.shape[1]
    tka, tt = _tile(Ka, 1536), _rows(T, 1024)

    def footprint(tt, tn):
        return 2 * (tt * tka * a.dtype.itemsize + tt * tn * g.dtype.itemsize + tka * tn * 4)

    budget = 46 * 1024 * 1024
    tn = N if N <= 3328 and footprint(tt, N) <= budget else _tile(N, 1536)
    if footprint(_rows(T, 2048), tn) <= budget:
        tt = _rows(T, 2048)
    nt = T // tt

    def body(a_ref, g_ref, o_ref):
        t = pl.program_id(2)
        p = _dot_tn(a_ref[...], g_ref[...])

        @pl.when(t == 0)
        def _():
            o_ref[...] = p

        @pl.when(t > 0)
        def _():
            o_ref[...] += p

    outs = _call(body, name, (Ka // tka, N // tn, nt),
                 [pl.BlockSpec((tt, tka), lambda i, j, t: (t, i)), pl.BlockSpec((tt, tn), lambda i, j, t: (t, j))],
                 [pl.BlockSpec((tka, tn), lambda i, j, t: (i, j))], [jax.ShapeDtypeStruct((Ka, N), F32)], [],
                 ("parallel", "parallel", "arbitrary"), (a, g), ride)
    return outs[0] if ride is None else outs


def rms_fwd(x, gains, name):
    T, D = x.shape
    tt = _rows(T, 1024)
    ng = len(gains)

    def body(*refs):
        x_ref = refs[0]
        g_refs = refs[1:1 + ng]
        o_refs = refs[1 + ng:]
        xf = x_ref[...]
        y = xf * lax.rsqrt(jnp.mean(xf * xf, axis=-1, keepdims=True) + EPS)
        for g_ref, o_ref in zip(g_refs, o_refs):
            o_ref[...] = (y * g_ref[...]).astype(o_ref.dtype)

    row = pl.BlockSpec((tt, D), lambda i: (i, 0))
    gsp = pl.BlockSpec((1, D), lambda i: (0, 0))
    return pl.pallas_call(
        body, name=name, grid=(T // tt,),
        in_specs=[row] + [gsp] * ng, out_specs=[row] * ng,
        out_shape=[jax.ShapeDtypeStruct((T, D), MM_DTYPE)] * ng,
        compiler_params=_params("parallel"),
    )(x, *gains)


def rms_bwd(x, dres, branches, name):
    T, D = x.shape
    nb = len(branches)
    tt = _rows(T, 1024 if nb == 1 else 512)

    def body(*refs):
        x_ref, r_ref = refs[0], refs[1]
        dy_refs = refs[2:2 + nb]
        g_refs = refs[2 + nb:2 + 2 * nb]
        dx_ref = refs[2 + 2 * nb]
        dg_refs = refs[3 + 2 * nb:]
        i = pl.program_id(0)
        xf = x_ref[...]
        r = lax.rsqrt(jnp.mean(xf * xf, axis=-1, keepdims=True) + EPS)
        xh = xf * r
        dx = r_ref[...]
        for dy_ref, g_ref, dg_ref in zip(dy_refs, g_refs, dg_refs):
            dy = dy_ref[...].astype(F32)
            dyg = dy * g_ref[...]
            dx = dx + r * (dyg - xh * jnp.mean(dyg * xh, axis=-1, keepdims=True))
            part = jnp.sum(dy * xh, axis=0, keepdims=True)

            @pl.when(i == 0)
            def _():
                dg_ref[...] = part

            @pl.when(i > 0)
            def _():
                dg_ref[...] += part
        dx_ref[...] = dx

    row = pl.BlockSpec((tt, D), lambda i: (i, 0))
    gsp = pl.BlockSpec((1, D), lambda i: (0, 0))
    outs = pl.pallas_call(
        body, name=name, grid=(T // tt,),
        in_specs=[row, row] + [row] * nb + [gsp] * nb,
        out_specs=[row] + [gsp] * nb,
        out_shape=[jax.ShapeDtypeStruct((T, D), F32)] + [jax.ShapeDtypeStruct((1, D), F32)] * nb,
        compiler_params=_params("arbitrary"),
    )(x, dres, *[b[0] for b in branches], *[b[1] for b in branches])
    return outs[0], outs[1:]


def loss_head(x, target, gain):
    T, D = x.shape
    tt = _rows(T, 512)

    def body(x_ref, t_ref, g_ref, dx_ref, dg_ref, loss_ref):
        i = pl.program_id(0)
        xf = x_ref[...]
        g = g_ref[...]
        r = lax.rsqrt(jnp.mean(xf * xf, axis=-1, keepdims=True) + EPS)
        xh = xf * r
        e = xh * g - t_ref[...]
        lpart = 0.5 * jnp.sum(jnp.sum(e * e, axis=1, keepdims=True), axis=0, keepdims=True) / D
        dy = e / D
        dyg = dy * g
        dx_ref[...] = r * (dyg - xh * jnp.mean(dyg * xh, axis=-1, keepdims=True))
        gpart = jnp.sum(dy * xh, axis=0, keepdims=True)
        lrow = jnp.broadcast_to(lpart, (1, 128))

        @pl.when(i == 0)
        def _():
            dg_ref[...] = gpart
            loss_ref[...] = lrow

        @pl.when(i > 0)
        def _():
            dg_ref[...] += gpart
            loss_ref[...] += lrow

    row = pl.BlockSpec((tt, D), lambda i: (i, 0))
    gsp = pl.BlockSpec((1, D), lambda i: (0, 0))
    return pl.pallas_call(
        body, name="loss_head", grid=(T // tt,),
        in_specs=[row, row, gsp],
        out_specs=[row, gsp, pl.BlockSpec((1, 128), lambda i: (0, 0))],
        out_shape=[jax.ShapeDtypeStruct((T, D), F32), jax.ShapeDtypeStruct((1, D), F32),
                   jax.ShapeDtypeStruct((1, 128), F32)],
        compiler_params=_params("arbitrary"),
    )(x, target, gain)


def _shift_down(u, prev8, first, k):
    rolled = pltpu.roll(u, k, 0)
    rid = lax.broadcasted_iota(jnp.int32, u.shape, 0)
    halo = jnp.where(first, 0.0, prev8)
    out = rolled
    for j in range(k):
        out = jnp.where(rid == j, halo[8 - k + j:8 - k + j + 1, :], out)
    return out


def _conv3(u, prev8, first, w, b):
    return (_shift_down(u, prev8, first, 2) * w[0:1, :] + _shift_down(u, prev8, first, 1) * w[1:2, :]
            + u * w[2:3, :] + b)


def ffn_up_act(xn, w_up, w, b, name, ride=None):
    T, K = xn.shape
    tt = _rows(T, 1024)
    nt = T // tt
    nj = D_FF // FF_TC

    def body(x_ref, wu_ref, w_ref, b_ref, u_ref, o_ref, ubuf, tail):
        s = pl.program_id(1)

        @pl.when(s == 0)
        def _():
            ubuf[...] = jnp.zeros_like(ubuf)
            tail[...] = jnp.zeros_like(tail)

        prev = ubuf[(s + 1) % 2]
        c = _conv3(prev, tail[...], s <= 1, w_ref[...], b_ref[...])
        tail[...] = prev[tt - 8:, :]
        cg, cv = c[:, :FF_TC], c[:, FF_TC:]
        o_ref[...] = (cg * _sigmoid_tanh(cg) * cv).astype(o_ref.dtype)
        u = _dot(x_ref[...], wu_ref[...])
        u_ref[...] = u
        ubuf[s % 2] = u

    cur = lambda i: jnp.minimum(i, nt - 1)
    return _call(
        body, name, (nj, nt + 1),
        [pl.BlockSpec((tt, K), lambda j, i: (cur(i), 0)),
         pl.BlockSpec((K, 2 * FF_TC), lambda j, i: (0, j)),
         pl.BlockSpec((3, 2 * FF_TC), lambda j, i: (0, j)),
         pl.BlockSpec((1, 2 * FF_TC), lambda j, i: (0, j))],
        [pl.BlockSpec((tt, 2 * FF_TC), lambda j, i: (cur(i), j)),
         pl.BlockSpec((tt, FF_TC), lambda j, i: (jnp.maximum(i - 1, 0), j))],
        [jax.ShapeDtypeStruct((T, 2 * D_FF), F32), jax.ShapeDtypeStruct((T, D_FF), MM_DTYPE)],
        [pltpu.VMEM((2, tt, 2 * FF_TC), F32), pltpu.VMEM((8, 2 * FF_TC), F32)],
        ("arbitrary", "arbitrary"), (xn, w_up, w, b), ride)


def conv_act_bwd(u, da, w, b, name, ride=None):
    T = u.shape[0]
    tt = _rows(T, 2048)
    nt = T // tt
    nj = D_FF // FF_TC
    te = tt + 8

    def body(u_ref, p_ref, n_ref, da_ref, dan_ref, w_ref, b_ref, du_ref, dw_ref, db_ref):
        i = pl.program_id(1)
        first = i == 0
        last = i == nt - 1
        w = w_ref[...]
        ue = jnp.concatenate([u_ref[...], n_ref[...]], axis=0)
        dae = jnp.concatenate([da_ref[...], jnp.where(last, 0.0, dan_ref[...])], axis=0)
        um2 = _shift_down(ue, p_ref[...], first, 2)
        um1 = _shift_down(ue, p_ref[...], first, 1)
        c = um2 * w[0:1, :] + um1 * w[1:2, :] + ue * w[2:3, :] + b_ref[...]
        cg, cv = c[:, :FF_TC], c[:, FF_TC:]
        s = _sigmoid_tanh(cg)
        dcg = dae * cv * (s * (1.0 + cg * (1.0 - s)))
        dcv = dae * (cg * s)
        dc = jnp.concatenate([dcg, dcv], axis=1)
        du = (dc * w[2:3, :] + pltpu.roll(dc, te - 1, 0) * w[1:2, :] + pltpu.roll(dc, te - 2, 0) * w[0:1, :])
        du_ref[...] = du[:tt, :].astype(du_ref.dtype)
        dcm = dc[:tt, :]
        dwp = jnp.concatenate([jnp.sum(dcm * um2[:tt, :], axis=0, keepdims=True),
                               jnp.sum(dcm * um1[:tt, :], axis=0, keepdims=True),
                               jnp.sum(dcm * ue[:tt, :], axis=0, keepdims=True)], axis=0)
        dbp = jnp.sum(dcm, axis=0, keepdims=True)

        @pl.when(first)
        def _():
            dw_ref[...] = dwp
            db_ref[...] = dbp

        @pl.when(i > 0)
        def _():
            dw_ref[...] += dwp
            db_ref[...] += dbp

    nb8 = T // 8
    return _call(
        body, name, (nj, nt),
        [pl.BlockSpec((tt, 2 * FF_TC), lambda j, i: (i, j)),
         pl.BlockSpec((8, 2 * FF_TC), lambda j, i: (jnp.maximum(i * (tt // 8) - 1, 0), j)),
         pl.BlockSpec((8, 2 * FF_TC), lambda j, i: (jnp.minimum((i + 1) * (tt // 8), nb8 - 1), j)),
         pl.BlockSpec((tt, FF_TC), lambda j, i: (i, j)),
         pl.BlockSpec((8, FF_TC), lambda j, i: (jnp.minimum((i + 1) * (tt // 8), nb8 - 1), j)),
         pl.BlockSpec((3, 2 * FF_TC), lambda j, i: (0, j)),
         pl.BlockSpec((1, 2 * FF_TC), lambda j, i: (0, j))],
        [pl.BlockSpec((tt, 2 * FF_TC), lambda j, i: (i, j)),
         pl.BlockSpec((3, 2 * FF_TC), lambda j, i: (0, j)),
         pl.BlockSpec((1, 2 * FF_TC), lambda j, i: (0, j))],
        [jax.ShapeDtypeStruct((T, 2 * D_FF), MM_DTYPE), jax.ShapeDtypeStruct((3, 2 * D_FF), F32),
         jax.ShapeDtypeStruct((1, 2 * D_FF), F32)],
        [], ("parallel", "arbitrary"), (u, u, u, da, da, w, b), ride)


def _interleave(a):
    lead = a.shape[:-1]
    nj = D_FF // FF_TC
    return jnp.swapaxes(a.reshape(*lead, 2, nj, FF_TC), -3, -2).reshape(*lead, 2 * D_FF)


def _deinterleave(a):
    lead = a.shape[:-1]
    nj = D_FF // FF_TC
    return jnp.swapaxes(a.reshape(*lead, nj, 2, FF_TC), -3, -2).reshape(*lead, 2 * D_FF)


A_GC = 1
A_TB = A_GC * A_CHUNK


def gate_prep(z, bias128):
    T = z.shape[0]
    tt = _rows(T, 512)

    def body(z_ref, b_ref, gc_ref, gr_ref):
        pre = z_ref[...] + b_ref[...]
        sc = SOFTCAP * jnp.tanh(pre / SOFTCAP)
        lf = jnp.minimum(sc, 0.0) - jnp.log(1.0 + jnp.exp(-jnp.abs(sc)))
        col = lax.broadcasted_iota(jnp.int32, pre.shape, 1)
        isf = jnp.logical_and(col >= A_HEADS, col < 2 * A_HEADS)
        r = lax.broadcasted_iota(jnp.int32, (tt, tt), 0)
        c = lax.broadcasted_iota(jnp.int32, (tt, tt), 1)
        bits = A_CHUNK.bit_length() - 1
        tri = jnp.logical_and(jnp.right_shift(r, bits) == jnp.right_shift(c, bits), c <= r).astype(F32)
        bcum = jnp.dot(tri, jnp.where(isf, lf, 0.0), precision=HI, preferred_element_type=F32)
        g = jnp.where(col < A_HEADS, sc, jnp.where(isf, bcum, 0.0))
        gc_ref[...] = g
        for s in range(tt // 128):
            gr_ref[s] = g[s * 128:(s + 1) * 128, :].T[0:8, :]

    return pl.pallas_call(
        body, name="gate_prep", grid=(T // tt,),
        in_specs=[pl.BlockSpec((tt, 128), lambda i: (i, GATE_COL // 128)),
                  pl.BlockSpec((1, 128), lambda i: (0, 0))],
        out_specs=[pl.BlockSpec((tt, 128), lambda i: (i, 0)),
                   pl.BlockSpec((tt // 128, 8, 128), lambda i: (i, 0, 0))],
        out_shape=[jax.ShapeDtypeStruct((T, 128), F32), jax.ShapeDtypeStruct((T // 128, 8, 128), F32)],
        compiler_params=_params("parallel"),
    )(z, bias128)


def _chunk_decay(A, qh, bc, br, lir, n, m, causal):
    logD = jnp.where(causal, bc - br + lir, -jnp.inf)
    m_inter = bc + m
    m_t = jnp.maximum(m_inter, jnp.max(logD, axis=1, keepdims=True))
    E = jnp.exp(logD - m_t)
    Sm = A * E
    wi = jnp.exp(m_inter - m_t)
    qn = jnp.sum(qh.astype(F32) * n, axis=1, keepdims=True)
    den = jnp.sum(Sm, axis=1, keepdims=True) + wi * qn
    gs = jnp.maximum(jnp.abs(den), jnp.exp(-m_t))
    return E, Sm, wi, den, gs, m_t


def _state_weights(bc, lic, br, lir, m):
    bL = bc[A_CHUNK - 1:A_CHUNK, :]
    m_new = jnp.maximum(bL + m, jnp.max(bL - br + lir, axis=1, keepdims=True))
    wk = jnp.exp(bL - bc + lic - m_new)
    decay = jnp.exp(bL + m - m_new)
    return wk, decay, m_new


def _head_slices(h):
    return (slice(h * A_QK, (h + 1) * A_QK), slice(h * A_V, (h + 1) * A_V))


def mlstm_fwd(z, gcol, grow, hng, ride=None):
    T = z.shape[0]
    NC = T // A_CHUNK
    scale = A_QK ** -0.5

    def body(q_ref, k_ref, v_ref, o_ref, gc_ref, gr_ref, hng_ref, hg_ref, Cs_ref, ns_ref, ms_ref,
             C_sc, n_sc, m_sc):
        @pl.when(pl.program_id(0) == 0)
        def _():
            C_sc[...] = jnp.zeros_like(C_sc)
            n_sc[...] = jnp.zeros_like(n_sc)
            m_sc[...] = jnp.zeros_like(m_sc)

        ri = lax.broadcasted_iota(jnp.int32, (A_CHUNK, A_CHUNK), 0)
        ci = lax.broadcasted_iota(jnp.int32, (A_CHUNK, A_CHUNK), 1)
        causal = ri >= ci
        gr = jnp.concatenate([gr_ref[s] for s in range(A_TB // 128)], axis=1)
        for c in range(A_GC):
            rows = slice(c * A_CHUNK, (c + 1) * A_CHUNK)
            gc = gc_ref[rows, :]
            grc = gr[:, c * A_CHUNK:(c + 1) * A_CHUNK]
            for h in range(A_HEADS):
                sk, sv = _head_slices(h)
                qh = (q_ref[rows, sk] * scale).astype(MM_DTYPE)
                kh = k_ref[rows, sk].astype(MM_DTYPE)
                vh = v_ref[rows, sv].astype(MM_DTYPE)
                lic, bc = gc[:, h:h + 1], gc[:, A_HEADS + h:A_HEADS + h + 1]
                lir, br = grc[h:h + 1, :], grc[A_HEADS + h:A_HEADS + h + 1, :]
                C, n, m = C_sc[h], n_sc[h], m_sc[h][:, 0:1]
                Cs_ref[c, h] = C
                ns_ref[c, h] = n
                ms_ref[c, h] = m_sc[h]
                _, Sm, wi, _, gs, _ = _chunk_decay(_dot_nt(qh, kh), qh, bc, br, lir, n, m, causal)
                hh = (_dot(Sm, vh) + wi * _dot(qh, C)) / gs
                hn = hh * lax.rsqrt(jnp.mean(hh * hh, axis=1, keepdims=True) + EPS) * hng_ref[:, sv]
                hg_ref[rows, sv] = (hn * _sigmoid(o_ref[rows, sv])).astype(hg_ref.dtype)
                wk, decay, m_new = _state_weights(bc, lic, br, lir, m)
                kw = kh.astype(F32) * wk
                C_sc[h] = decay * C + _dot_tn(kw, vh)
                n_sc[h] = decay * n + jnp.sum(kw, axis=0, keepdims=True)
                m_sc[h] = jnp.broadcast_to(m_new, (1, 128))

    tok = lambda w, cb: pl.BlockSpec((A_TB, w), lambda i: (i, cb))
    return _call(
        body, "mlstm_fwd", (NC // A_GC,),
        [tok(512, 0), tok(512, 1), tok(1024, 1), tok(1024, 2),
         pl.BlockSpec((A_TB, 128), lambda i: (i, 0)),
         pl.BlockSpec((A_TB // 128, 8, 128), lambda i: (i, 0, 0)),
         pl.BlockSpec((1, 1024), lambda i: (0, 0))],
        [pl.BlockSpec((A_TB, 1024), lambda i: (i, 0)),
         pl.BlockSpec((A_GC, A_HEADS, A_QK, A_V), lambda i: (i, 0, 0, 0)),
         pl.BlockSpec((A_GC, A_HEADS, 1, 128), lambda i: (i, 0, 0, 0)),
         pl.BlockSpec((A_GC, A_HEADS, 1, 128), lambda i: (i, 0, 0, 0))],
        [jax.ShapeDtypeStruct((T, 1024), MM_DTYPE),
         jax.ShapeDtypeStruct((NC, A_HEADS, A_QK, A_V), F32),
         jax.ShapeDtypeStruct((NC, A_HEADS, 1, 128), F32),
         jax.ShapeDtypeStruct((NC, A_HEADS, 1, 128), F32)],
        [pltpu.VMEM((A_HEADS, A_QK, A_V), F32), pltpu.VMEM((A_HEADS, 1, 128), F32),
         pltpu.VMEM((A_HEADS, 1, 128), F32)],
        ("arbitrary",), (z, z, z, z, gcol, grow, hng), ride)


def mlstm_bwd(z, gcol, grow, hng, bias128, Cs, ns, ms, dhg, ride=None):
    T = z.shape[0]
    NC = T // A_CHUNK
    nsteps = NC // A_GC
    scale = A_QK ** -0.5

    def body(q_ref, k_ref, v_ref, o_ref, zg_ref, gc_ref, gr_ref, hng_ref, b_ref, Cs_ref, ns_ref, ms_ref,
             dhg_ref, dz_ref, dgn_ref, dbif_ref, dC_sc, dn_sc):
        @pl.when(pl.program_id(0) == 0)
        def _():
            dC_sc[...] = jnp.zeros_like(dC_sc)
            dn_sc[...] = jnp.zeros_like(dn_sc)
            dgn_ref[...] = jnp.zeros_like(dgn_ref)
            dbif_ref[...] = jnp.zeros_like(dbif_ref)

        ri = lax.broadcasted_iota(jnp.int32, (A_CHUNK, A_CHUNK), 0)
        ci = lax.broadcasted_iota(jnp.int32, (A_CHUNK, A_CHUNK), 1)
        causal = ri >= ci
        upper = (ci >= ri).astype(F32)
        rid = lax.broadcasted_iota(jnp.int32, (A_CHUNK, 1), 0)
        col = lax.broadcasted_iota(jnp.int32, (A_CHUNK, 128), 1)
        gr = jnp.concatenate([gr_ref[s] for s in range(A_TB // 128)], axis=1)
        for c in reversed(range(A_GC)):
            rows = slice(c * A_CHUNK, (c + 1) * A_CHUNK)
            gc = gc_ref[rows, :]
            grc = gr[:, c * A_CHUNK:(c + 1) * A_CHUNK]
            dG = jnp.zeros((A_CHUNK, 128), F32)
            hs = []
            for h in range(A_HEADS):
                sk, sv = _head_slices(h)
                s = dict(sk=sk, sv=sv, qh=(q_ref[rows, sk] * scale).astype(MM_DTYPE),
                         kh=k_ref[rows, sk].astype(MM_DTYPE), vh=v_ref[rows, sv].astype(MM_DTYPE),
                         lic=gc[:, h:h + 1], bc=gc[:, A_HEADS + h:A_HEADS + h + 1],
                         lir=grc[h:h + 1, :], br=grc[A_HEADS + h:A_HEADS + h + 1, :],
                         C=Cs_ref[c, h], n=ns_ref[c, h], m=ms_ref[c, h][:, 0:1], dC=dC_sc[h], dn=dn_sc[h])
                s['qf'], s['kf'] = s['qh'].astype(F32), s['kh'].astype(F32)
                s['wk'], s['decay'], _ = _state_weights(s['bc'], s['lic'], s['br'], s['lir'], s['m'])
                hs.append(s)
            for s in hs:
                s['A'] = _dot_nt(s['qh'], s['kh'])
                s['qC'] = _dot(s['qh'], s['C'])
                s['vdC'] = _dot_nt(s['vh'], s['dC'])
                s['kdC'] = _dot(s['kh'], s['dC'])
            for s in hs:
                s['E'], s['Sm'], s['wi'], s['den'], s['gs'], s['m_t'] = _chunk_decay(
                    s['A'], s['qh'], s['bc'], s['br'], s['lir'], s['n'], s['m'], causal)
            for s in hs:
                s['num'] = _dot(s['Sm'], s['vh']) + s['wi'] * s['qC']
            for h, s in enumerate(hs):
                sv, gs = s['sv'], s['gs']
                hh = s['num'] / gs
                r = lax.rsqrt(jnp.mean(hh * hh, axis=1, keepdims=True) + EPS)
                gn = hng_ref[:, sv]
                sg = _sigmoid(o_ref[rows, sv])
                dhg_h = dhg_ref[rows, sv]
                dhn = dhg_h * sg
                dz_ref[rows, 2048 + h * A_V:2048 + (h + 1) * A_V] = (
                    dhg_h * (hh * r * gn) * sg * (1.0 - sg)).astype(dz_ref.dtype)
                dgn_ref[:, sv] += jnp.sum(dhn * hh * r, axis=0, keepdims=True)
                dyg = dhn * gn
                dh = r * dyg - hh * (r * r * r) * jnp.mean(dyg * hh, axis=1, keepdims=True)
                s['dnum'] = dh / gs
                live = (jnp.abs(s['den']) > jnp.exp(-s['m_t'])).astype(F32)
                s['dden'] = -jnp.sum(dh * hh, axis=1, keepdims=True) / gs * jnp.sign(s['den']) * live
            for s in hs:
                s['dnv'] = _dot_nt(s['dnum'], s['vh'])
                s['dnC'] = _dot_nt(s['dnum'], s['C'])
            for s in hs:
                s['dSE'] = jnp.where(causal, s['dnv'] + s['dden'], 0.0) * s['E']
            for s in hs:
                s['dq'] = _dot(s['dSE'], s['kh']) + s['wi'] * (s['dnC'] + s['dden'] * s['n'])
                s['dk_inter'] = s['wk'] * (s['vdC'] + s['dn'])
                s['dk'] = _dot_tn(s['dSE'], s['qh']) + s['dk_inter']
                s['dv'] = _dot_tn(s['Sm'], s['dnum']) + s['wk'] * s['kdC']
                s['dCq'] = _dot_tn(s['qf'] * s['wi'], s['dnum'])
            for h, s in enumerate(hs):
                dq, dk, qf, kf, dC, dn = s['dq'], s['dk'], s['qf'], s['kf'], s['dC'], s['dn']
                dz_ref[rows, s['sk']] = (dq * scale).astype(dz_ref.dtype)
                dz_ref[rows, 512 + h * A_QK:512 + (h + 1) * A_QK] = dk.astype(dz_ref.dtype)
                dz_ref[rows, 1024 + h * A_V:1024 + (h + 1) * A_V] = s['dv'].astype(dz_ref.dtype)
                dli = jnp.sum(kf * dk, axis=1, keepdims=True)
                db = jnp.sum(qf * dq, axis=1, keepdims=True) - dli
                usum = jnp.sum(jnp.sum(kf * s['dk_inter'], axis=1, keepdims=True), axis=0, keepdims=True)
                ddecay = (jnp.sum(jnp.sum(dC * s['C'], axis=1, keepdims=True), axis=0, keepdims=True)
                          + jnp.sum(dn * s['n'], axis=1, keepdims=True))
                db = db + jnp.where(rid == A_CHUNK - 1, usum + ddecay * s['decay'], 0.0)
                dG = dG + jnp.where(col == h, dli, 0.0) + jnp.where(col == A_HEADS + h, db, 0.0)
                dC_sc[h] = s['decay'] * dC + s['dCq']
                dn_sc[h] = s['decay'] * dn + jnp.sum(qf * (s['wi'] * s['dden']), axis=0, keepdims=True)
            dlf = jnp.dot(upper, dG, precision=HI, preferred_element_type=F32)
            pre = zg_ref[rows, :] + b_ref[...]
            th = jnp.tanh(pre / SOFTCAP)
            dcap = 1.0 - th * th
            dpre = jnp.where(col < A_HEADS, dG * dcap,
                             jnp.where(col < 2 * A_HEADS, dlf * _sigmoid(-SOFTCAP * th) * dcap, 0.0))
            dz_ref[rows, GATE_COL:GATE_COL + 128] = dpre.astype(dz_ref.dtype)
            dbif_ref[...] += jnp.sum(dpre, axis=0, keepdims=True)

    rev = lambda i: nsteps - 1 - i
    tok = lambda w, cb: pl.BlockSpec((A_TB, w), lambda i: (rev(i), cb))
    st = lambda a, b: pl.BlockSpec((A_GC, A_HEADS, a, b), lambda i: (rev(i), 0, 0, 0))
    return _call(
        body, "mlstm_bwd", (nsteps,),
        [tok(512, 0), tok(512, 1), tok(1024, 1), tok(1024, 2), tok(128, GATE_COL // 128),
         pl.BlockSpec((A_TB, 128), lambda i: (rev(i), 0)),
         pl.BlockSpec((A_TB // 128, 8, 128), lambda i: (rev(i), 0, 0)),
         pl.BlockSpec((1, 1024), lambda i: (0, 0)),
         pl.BlockSpec((1, 128), lambda i: (0, 0)),
         st(A_QK, A_V), st(1, 128), st(1, 128),
         pl.BlockSpec((A_TB, 1024), lambda i: (rev(i), 0))],
        [pl.BlockSpec((A_TB, A_IN_PAD), lambda i: (rev(i), 0)),
         pl.BlockSpec((1, 1024), lambda i: (0, 0)),
         pl.BlockSpec((1, 128), lambda i: (0, 0))],
        [jax.ShapeDtypeStruct((T, A_IN_PAD), MM_DTYPE), jax.ShapeDtypeStruct((1, 1024), F32),
         jax.ShapeDtypeStruct((1, 128), F32)],
        [pltpu.VMEM((A_HEADS, A_QK, A_V), F32), pltpu.VMEM((A_HEADS, 1, 128), F32)],
        ("arbitrary",), (z, z, z, z, z, gcol, grow, hng, bias128, Cs, ns, ms, dhg), ride)


def _t5_bucket(dist):
    max_exact = REL_BUCKETS // 2
    d = np.maximum(dist, 0)
    log_ratio = np.log(np.maximum(d, 1) / max_exact) / math.log(REL_MAX_DIST / max_exact)
    large = np.minimum(max_exact + (log_ratio * (REL_BUCKETS - max_exact)).astype(np.int64), REL_BUCKETS - 1)
    return np.where(d < max_exact, d, large).astype(np.int32)


def _group_bucket(g):
    delta = B_BLOCK + np.arange(B_BLOCK)[:, None] - np.arange(2 * B_BLOCK)[None, :]
    return _t5_bucket(delta * DILATIONS[g])


def _band_mask(n):
    ri = lax.broadcasted_iota(jnp.int32, (B_BLOCK, 2 * B_BLOCK), 0)
    ci = lax.broadcasted_iota(jnp.int32, (B_BLOCK, 2 * B_BLOCK), 1)
    band = jnp.logical_and(ci >= ri, ci <= ri + B_BLOCK)
    return jnp.logical_and(band, jnp.logical_or(ci >= B_BLOCK, n > 0))


def _both(p_ref, c_ref, sl):
    return jnp.concatenate([p_ref[:, sl], c_ref[:, sl]], axis=0)


def _scores(qh, kh, bias_h, valid):
    return jnp.where(valid, _dot_nt(qh, kh) * (B_DH ** -0.5) + bias_h, -jnp.inf)


def _attn_specs():
    wide = pl.BlockSpec((B_BLOCK, 1024), lambda r, n: (n, r))
    prev = pl.BlockSpec((B_BLOCK, 1024), lambda r, n: (jnp.maximum(n - 1, 0), r))
    narrow = pl.BlockSpec((B_BLOCK, 128), lambda r, n: (n, r))
    bias = pl.BlockSpec((B_HEADS, B_BLOCK, 2 * B_BLOCK), lambda r, n: (0, 0, 0))
    return wide, prev, narrow, bias


def _to_view(read_chunk, sc, o_ref, dil, nc, tt):
    for c in range(nc):
        sc[c] = read_chunk(c)
    for r in range(dil):
        for c in range(nc):
            lo = (r * nc + c) * 128
            o_ref[:, lo:lo + 128] = sc[c, pl.ds(r, tt // dil, stride=dil), :].astype(o_ref.dtype)


def _from_view(read_view, sc, dil, nc, tt):
    for r in range(dil):
        for c in range(nc):
            sc[c, pl.ds(r, tt // dil, stride=dil), :] = read_view((r * nc + c) * 128).astype(F32)


def attn_fwd(qv, kvw, vvw, bias, g):
    dil = DILATIONS[g]
    Tv = qv.shape[0]
    nb = Tv // B_BLOCK
    wide, prev, narrow, bsp = _attn_specs()

    def body(q_ref, kp_ref, kc_ref, vp_ref, vc_ref, b_ref, o_ref, lse_ref):
        valid = _band_mask(pl.program_id(1))
        lse_ref[...] = jnp.zeros_like(lse_ref)
        heads = [slice(h * B_DH, (h + 1) * B_DH) for h in range(B_HEADS)]
        S = [_scores(q_ref[:, sl], _both(kp_ref, kc_ref, sl), b_ref[h], valid) for h, sl in enumerate(heads)]
        P, L = [], []
        for h in range(B_HEADS):
            m = jnp.max(S[h], axis=1, keepdims=True)
            p = jnp.exp(S[h] - m)
            l = jnp.sum(p, axis=1, keepdims=True)
            lse_ref[:, h:h + 1] = m + jnp.log(l)
            P.append(p.astype(MM_DTYPE))
            L.append(l)
        for h, sl in enumerate(heads):
            o_ref[:, sl] = _dot(P[h], _both(vp_ref, vc_ref, sl)) / L[h]

    return pl.pallas_call(
        body, name=f"attn_fwd_g{g}", grid=(dil, nb),
        in_specs=[wide, prev, wide, prev, wide, bsp], out_specs=[wide, narrow],
        out_shape=[jax.ShapeDtypeStruct((Tv, dil * 1024), F32), jax.ShapeDtypeStruct((Tv, dil * 128), F32)],
        compiler_params=_params("parallel", "parallel"),
    )(qv, kvw, kvw, vvw, vvw, bias)


def attn_bwd(qv, kvw, vvw, bias, do_v, lse_v, dl_v, g):
    dil = DILATIONS[g]
    Tv = qv.shape[0]
    nb = Tv // B_BLOCK
    wide, prev, narrow, bsp = _attn_specs()

    def body(q_ref, kp_ref, kc_ref, vp_ref, vc_ref, b_ref, bt_ref, do_ref, lse_ref, dl_ref,
             dq_ref, dkc_ref, dkp_ref, dvc_ref, dvp_ref, db_ref):
        @pl.when(jnp.logical_and(pl.program_id(0) == 0, pl.program_id(1) == 0))
        def _():
            db_ref[...] = jnp.zeros_like(db_ref)

        n = pl.program_id(1)
        valid = _band_mask(n)
        ki = lax.broadcasted_iota(jnp.int32, (2 * B_BLOCK, B_BLOCK), 0)
        qi = lax.broadcasted_iota(jnp.int32, (2 * B_BLOCK, B_BLOCK), 1)
        valid_t = jnp.logical_and(jnp.logical_and(ki >= qi, ki <= qi + B_BLOCK), jnp.logical_or(ki >= B_BLOCK, n > 0))
        lse_t, dl_t = lse_ref[...].T, dl_ref[...].T
        heads = [slice(h * B_DH, (h + 1) * B_DH) for h in range(B_HEADS)]
        scale = B_DH ** -0.5
        PT, DS, DST = [], [], []
        for h, sl in enumerate(heads):
            qh, doh = q_ref[:, sl], do_ref[:, sl].astype(MM_DTYPE)
            kh, vh = _both(kp_ref, kc_ref, sl), _both(vp_ref, vc_ref, sl)
            p = jnp.exp(_scores(qh, kh, b_ref[h], valid) - lse_ref[:, h:h + 1])
            ds = p * (_dot_nt(doh, vh) - dl_ref[:, h:h + 1])
            db_ref[h] += ds
            DS.append((ds * scale).astype(MM_DTYPE))
            pt = jnp.exp(_scores(kh, qh, bt_ref[h], valid_t) - lse_t[h:h + 1, :])
            PT.append(pt.astype(MM_DTYPE))
            DST.append((pt * (_dot_nt(vh, doh) - dl_t[h:h + 1, :]) * scale).astype(MM_DTYPE))
        for h, sl in enumerate(heads):
            qh, doh = q_ref[:, sl], do_ref[:, sl].astype(MM_DTYPE)
            dq_ref[:, sl] = _dot(DS[h], _both(kp_ref, kc_ref, sl)).astype(MM_DTYPE)
            dk = _dot(DST[h], qh).astype(MM_DTYPE)
            dv = _dot(PT[h], doh).astype(MM_DTYPE)
            dkp_ref[:, sl], dkc_ref[:, sl] = dk[:B_BLOCK], dk[B_BLOCK:]
            dvp_ref[:, sl], dvc_ref[:, sl] = dv[:B_BLOCK], dv[B_BLOCK:]

    big = jax.ShapeDtypeStruct((Tv, dil * 1024), MM_DTYPE)
    bsp_t = pl.BlockSpec((B_HEADS, 2 * B_BLOCK, B_BLOCK), lambda r, n: (0, 0, 0))
    return pl.pallas_call(
        body, name=f"attn_bwd_g{g}", grid=(dil, nb),
        in_specs=[wide, prev, wide, prev, wide, bsp, bsp_t, wide, narrow, narrow],
        out_specs=[wide] * 5 + [bsp],
        out_shape=[big] * 5 + [jax.ShapeDtypeStruct((B_HEADS, B_BLOCK, 2 * B_BLOCK), F32)],
        compiler_params=_params("arbitrary", "arbitrary"),
    )(qv, kvw, kvw, vvw, vvw, bias, jnp.swapaxes(bias, 1, 2), do_v, lse_v, dl_v)


def _head_expand():
    e = np.zeros((128, 1024), np.float32)
    for h in range(B_HEADS):
        e[h, h * B_DH:(h + 1) * B_DH] = 1.0
    return e


A_TT = 256
A_TT_WIDE = 512


def _view_spec(dil, width, tt=A_TT):
    return pl.BlockSpec((tt // dil, dil * width), lambda i: (i, 0))


def attn_merge(os_v, lses_v):
    T = os_v[0].shape[0]
    tt = A_TT_WIDE
    expand = jnp.asarray(_head_expand())

    def body(o0, o1, o2, l0, l1, l2, e_ref, ob_ref, of_ref, lse0_ref, lse1_ref, lse2_ref, sc_o, sc_l):
        for gi, (o_ref, l_ref) in enumerate(((o1, l1), (o2, l2))):
            dil = DILATIONS[gi + 1]
            _from_view(lambda lo: o_ref[:, lo:lo + 128], sc_o.at[gi], dil, 8, tt)
            _from_view(lambda lo: l_ref[:, lo:lo + 128], sc_l.at[gi], dil, 1, tt)
        ls = [l0[...], sc_l[0, 0], sc_l[1, 0]]
        m = jnp.maximum(jnp.maximum(ls[0], ls[1]), ls[2])
        ex = [jnp.exp(l - m) for l in ls]
        tot = ex[0] + ex[1] + ex[2]
        lse = m + jnp.log(tot)
        lse0_ref[...] = lse
        _to_view(lambda c: lse, sc_l.at[2], lse1_ref, DILATIONS[1], 1, tt)
        _to_view(lambda c: lse, sc_l.at[2], lse2_ref, DILATIONS[2], 1, tt)
        ws = [e / tot for e in ex]
        for c in range(8):
            cols = slice(c * 128, (c + 1) * 128)
            ecol = e_ref[:, cols]
            spread = [jnp.dot(w, ecol, precision=HI, preferred_element_type=F32) for w in ws]
            out = spread[0] * o0[:, cols] + spread[1] * sc_o[0, c] + spread[2] * sc_o[1, c]
            of_ref[:, cols] = out
            ob_ref[:, cols] = out.astype(ob_ref.dtype)

    wide = pl.BlockSpec((tt, 1024), lambda i: (i, 0))
    return pl.pallas_call(
        body, name="attn_merge", grid=(T // tt,),
        in_specs=[_view_spec(d, 1024, tt) for d in DILATIONS] + [_view_spec(d, 128, tt) for d in DILATIONS]
        + [pl.BlockSpec((128, 1024), lambda i: (0, 0))],
        out_specs=[wide, wide] + [_view_spec(d, 128, tt) for d in DILATIONS],
        out_shape=[jax.ShapeDtypeStruct((T, 1024), MM_DTYPE), jax.ShapeDtypeStruct((T, 1024), F32)]
        + [jax.ShapeDtypeStruct((T // d, d * 128), F32) for d in DILATIONS],
        scratch_shapes=[pltpu.VMEM((2, 8, tt, 128), F32), pltpu.VMEM((3, 1, tt, 128), F32)],
        compiler_params=_params("parallel"),
    )(*os_v, *lses_v, expand)


def attn_prep(datt, out):
    T = datt.shape[0]
    tt = A_TT_WIDE
    expand_t = jnp.asarray(_head_expand().T.copy())

    def body(d_ref, o_ref, e_ref, do0, do1, do2, dl0, dl1, dl2, sc_d, sc_l):
        delta = jnp.dot(d_ref[...] * o_ref[...], e_ref[...], precision=HI, preferred_element_type=F32)
        do0[...] = d_ref[...].astype(do0.dtype)
        dl0[...] = delta
        for do_ref, dl_ref, dil in ((do1, dl1, DILATIONS[1]), (do2, dl2, DILATIONS[2])):
            _to_view(lambda c: d_ref[:, c * 128:(c + 1) * 128], sc_d, do_ref, dil, 8, tt)
            _to_view(lambda c: delta, sc_l, dl_ref, dil, 1, tt)

    wide = pl.BlockSpec((tt, 1024), lambda i: (i, 0))
    return pl.pallas_call(
        body, name="attn_prep", grid=(T // tt,),
        in_specs=[wide, wide, pl.BlockSpec((1024, 128), lambda i: (0, 0))],
        out_specs=[_view_spec(d, 1024, tt) for d in DILATIONS] + [_view_spec(d, 128, tt) for d in DILATIONS],
        out_shape=[jax.ShapeDtypeStruct((T // d, d * 1024), MM_DTYPE) for d in DILATIONS]
        + [jax.ShapeDtypeStruct((T // d, d * 128), F32) for d in DILATIONS],
        scratch_shapes=[pltpu.VMEM((8, tt, 128), F32), pltpu.VMEM((1, tt, 128), F32)],
        compiler_params=_params("parallel"),
    )(datt, out, expand_t)


def attn_combine(parts):
    T = parts[0][0].shape[0]
    tt = A_TT
    nt = T // tt
    shift = [None] + [B_BLOCK * d // tt for d in DILATIONS[1:]]

    def body(dq0, kc0, vc0, kpa0, kpb0, vpa0, vpb0, dq1, kc1, kp1, vc1, vp1, dq2, kc2, kp2, vc2, vp2,
             dq_ref, dkv_ref, sc):
        i = pl.program_id(0)
        dq_ref[:, 0:1024] = dq0[...].astype(dq_ref.dtype)
        for col, c_ref, pa_ref, pb_ref in ((0, kc0, kpa0, kpb0), (3, vc0, vpa0, vpb0)):
            nxt = jnp.where(i + 1 < nt, pb_ref[:tt // 2, :].astype(F32), 0.0)
            later = jnp.concatenate([pa_ref[tt // 2:, :].astype(F32), nxt], axis=0)
            dkv_ref[:, col * 1024:(col + 1) * 1024] = (c_ref[...].astype(F32) + later).astype(dkv_ref.dtype)
        for g, (dq, kc, kp, vc, vp) in ((1, (dq1, kc1, kp1, vc1, vp1)), (2, (dq2, kc2, kp2, vc2, vp2))):
            dil = DILATIONS[g]
            live = i + shift[g] < nt
            _from_view(lambda lo: dq[:, lo:lo + 128], sc, dil, 8, tt)
            for c in range(8):
                dq_ref[:, g * 1024 + c * 128:g * 1024 + (c + 1) * 128] = sc[c].astype(dq_ref.dtype)
            for col, c_ref, p_ref in ((g, kc, kp), (3 + g, vc, vp)):
                _from_view(lambda lo: c_ref[:, lo:lo + 128].astype(F32)
                           + jnp.where(live, p_ref[:, lo:lo + 128].astype(F32), 0.0), sc, dil, 8, tt)
                for c in range(8):
                    dkv_ref[:, col * 1024 + c * 128:col * 1024 + (c + 1) * 128] = sc[c].astype(dkv_ref.dtype)

    def later_spec(dil, blocks):
        return pl.BlockSpec((tt // dil, dil * 1024), lambda i: (jnp.minimum(i + blocks, nt - 1), 0))

    cur = [_view_spec(d, 1024) for d in DILATIONS]
    in_specs = [cur[0], cur[0], cur[0], cur[0], later_spec(1, 1), cur[0], later_spec(1, 1)]
    args = [parts[0][0], parts[0][1], parts[0][3], parts[0][2], parts[0][2], parts[0][4], parts[0][4]]
    for g in (1, 2):
        in_specs += [cur[g], cur[g], later_spec(DILATIONS[g], shift[g]), cur[g], later_spec(DILATIONS[g], shift[g])]
        args += list(parts[g][:5])
    return pl.pallas_call(
        body, name="attn_combine", grid=(nt,), in_specs=in_specs,
        out_specs=[pl.BlockSpec((tt, 3072), lambda i: (i, 0)), pl.BlockSpec((tt, 6144), lambda i: (i, 0))],
        out_shape=[jax.ShapeDtypeStruct((T, 3072), MM_DTYPE), jax.ShapeDtypeStruct((T, 6144), MM_DTYPE)],
        scratch_shapes=[pltpu.VMEM((8, tt, 128), F32)],
        compiler_params=_params("parallel"),
    )(*args)


def adamw(w, g, m, v, name):
    R, C = w.shape
    tr = R if R * C * 4 <= (1 << 20) else _rows(R, max(8, ((1 << 20) // (C * 4)) // 8 * 8))

    def body(w_ref, g_ref, m_ref, v_ref, d_ref, nm_ref, nv_ref):
        gg = g_ref[...]
        nm = ADAM_B1 * m_ref[...] + (1.0 - ADAM_B1) * gg
        nv = ADAM_B2 * v_ref[...] + (1.0 - ADAM_B2) * (gg * gg)
        m_hat = nm / (1.0 - ADAM_B1 ** ADAM_STEP)
        v_hat = nv / (1.0 - ADAM_B2 ** ADAM_STEP)
        d_ref[...] = -ADAM_LR * (m_hat / (jnp.sqrt(v_hat) + ADAM_EPS) + ADAM_WD * w_ref[...])
        nm_ref[...] = nm
        nv_ref[...] = nv

    blk = pl.BlockSpec((tr, C), lambda i: (i, 0))
    sds = jax.ShapeDtypeStruct((R, C), F32)
    return pl.pallas_call(
        body, name=name, grid=(R // tr,), in_specs=[blk] * 4, out_specs=[blk] * 3, out_shape=[sds] * 3,
        compiler_params=_params("parallel"),
    )(w, g, m, v)


def sum_slots(x, name, out_dtype=F32):
    n, R, C = x.shape
    tr = _rows(R, 256)

    def body(x_ref, o_ref):
        acc = x_ref[0].astype(F32)
        for s in range(1, n):
            acc = acc + x_ref[s].astype(F32)
        o_ref[...] = acc.astype(out_dtype)

    return pl.pallas_call(
        body, name=name, grid=(R // tr,),
        in_specs=[pl.BlockSpec((n, tr, C), lambda i: (0, i, 0))],
        out_specs=pl.BlockSpec((tr, C), lambda i: (i, 0)),
        out_shape=jax.ShapeDtypeStruct((R, C), out_dtype),
        compiler_params=_params("parallel"),
    )(x)


_ANY = pl.BlockSpec(memory_space=pl.ANY)
GROUP_ALL = ([(0, 0, 1), (0, 1, 0), (0, 1, 1), (1, 0, 0), (1, 0, 1), (1, 1, 0), (1, 1, 1)],
             lambda d: 4 * d[0] + 2 * d[1] + d[2])
GROUP_CHIPS = ([(0, 1, 0), (1, 0, 0), (1, 1, 0)], lambda d: 2 * d[0] + d[1])
GROUP_SIBLING = ([(0, 0, 1)], lambda d: d[2])


def _me():
    return lax.axis_index("x"), lax.axis_index("y"), lax.axis_index("c")


def _peer(me, flip):
    return tuple(1 - a if f else a for a, f in zip(me, flip))


class Exchange:
    def __init__(self, x, group, scatter):
        self.flips, self.slot = group
        self.scatter = scatter
        self.n = len(self.flips) + 1
        self.out_shape = jax.ShapeDtypeStruct((self.n,) + x.shape[-2:], x.dtype)
        self.scratch = [pltpu.SemaphoreType.DMA((self.n - 1,)), pltpu.SemaphoreType.DMA((self.n - 1,)),
                        pltpu.SemaphoreType.DMA]

    def _copies(self, x_ref, o_ref, send_sems, recv_sems, local_sem, arrivals):
        me = _me()
        slot = self.slot
        mine = pltpu.make_async_copy(x_ref.at[slot(me)] if self.scatter else x_ref, o_ref.at[slot(me)], local_sem)
        sends, landed = [], []
        for k, flip in enumerate(self.flips):
            peer = _peer(me, flip)
            sends.append(pltpu.make_async_remote_copy(
                src_ref=x_ref.at[slot(peer)] if self.scatter else x_ref, dst_ref=o_ref.at[slot(me)],
                send_sem=send_sems.at[k], recv_sem=recv_sems.at[k], device_id=peer, device_id_type=MESH_ID))
            if arrivals:
                landed.append(pltpu.make_async_remote_copy(
                    src_ref=o_ref.at[slot(me)], dst_ref=o_ref.at[slot(peer)], send_sem=send_sems.at[k],
                    recv_sem=recv_sems.at[k], device_id=peer, device_id_type=MESH_ID))
        return mine, sends, landed

    def start(self, *refs):
        mine, sends, _ = self._copies(*refs, arrivals=False)
        mine.start()
        for cp in sends:
            cp.start()

    def wait(self, *refs):
        mine, sends, arrivals = self._copies(*refs, arrivals=True)
        for cp in arrivals:
            cp.wait_recv()
        for cp in sends:
            cp.wait_send()
        mine.wait()

    def __call__(self, x, name):
        def body(*refs):
            self.start(*refs)
            self.wait(*refs)

        return pl.pallas_call(body, name=name, in_specs=[_ANY], out_specs=_ANY, out_shape=self.out_shape,
                              scratch_shapes=self.scratch)(x)


def group_gather(x, name, group):
    return Exchange(x, group, scatter=False)(x, name)


def group_scatter(x, name, group):
    return Exchange(x, group, scatter=True)(x, name)


def _call(body, name, grid, in_specs, out_specs, out_shape, scratch, semantics, args, ride=None):
    if ride is None:
        return pl.pallas_call(body, name=name, grid=grid, in_specs=in_specs, out_specs=out_specs,
                              out_shape=out_shape, scratch_shapes=scratch,
                              compiler_params=_params(*semantics))(*args)
    x, exch = ride
    n_in, n_out, n_scr = len(in_specs), len(out_specs), len(scratch)

    def at_step(pick):
        hit = None
        for axis, size in enumerate(grid):
            here = pl.program_id(axis) == pick(size)
            hit = here if hit is None else jnp.logical_and(hit, here)
        return hit

    def riding(*refs):
        ins, x_ref = refs[:n_in], refs[n_in]
        outs, o_ref = refs[n_in + 1:n_in + 1 + n_out], refs[n_in + 1 + n_out]
        scr, sems = refs[n_in + 2 + n_out:n_in + 2 + n_out + n_scr], refs[n_in + 2 + n_out + n_scr:]

        @pl.when(at_step(lambda size: 0))
        def _():
            exch.start(x_ref, o_ref, *sems)

        body(*ins, *outs, *scr)

        @pl.when(at_step(lambda size: size - 1))
        def _():
            exch.wait(x_ref, o_ref, *sems)

    return pl.pallas_call(
        riding, name=name, grid=grid, in_specs=list(in_specs) + [_ANY], out_specs=list(out_specs) + [_ANY],
        out_shape=list(out_shape) + [exch.out_shape], scratch_shapes=list(scratch) + exch.scratch,
        compiler_params=_params(*(["arbitrary"] * len(grid))))(*args, x)


WEIGHTS = ['a_norm_g', 'a_w_in', 'a_b_if', 'a_hnorm_g', 'a_w_out', 'kv_norm_g', 'w_kv', 'b_norm_g', 'b_w_q',
           'b_w_out', 'rel_bias', 'f_norm_g', 'f_w_up', 'f_conv_w', 'f_conv_b', 'f_w_down', 'final_norm_g']
SHARD_AXIS = {'a_norm_g': 1, 'a_w_in': 2, 'a_b_if': None, 'a_hnorm_g': 2, 'a_w_out': 1, 'kv_norm_g': None,
              'w_kv': 1, 'b_norm_g': None, 'b_w_q': 2, 'b_w_out': 1, 'rel_bias': None, 'f_norm_g': None,
              'f_w_up': 2, 'f_conv_w': 2, 'f_conv_b': None, 'f_w_down': 1, 'final_norm_g': None}
BIG = ['a_w_in', 'a_w_out', 'w_kv', 'b_w_q', 'b_w_out', 'f_w_up', 'f_w_down']
SMALL = [n for n in WEIGHTS if n not in BIG]
LANES = 1024
PIECES = {'a_w_in': ('a_w_in', None, 2), 'a_w_out': ('a_w_out', None, 1), 'f_w_up0': ('f_w_up', 0, 1),
          'f_w_down0': ('f_w_down', 0, 0), 'w_kv': ('w_kv', None, 1), 'b_w_q': ('b_w_q', None, 2),
          'b_w_out': ('b_w_out', None, 1), 'f_w_up1': ('f_w_up', 1, 1), 'f_w_down1': ('f_w_down', 1, 0)}
LATE = ['w_kv', 'b_w_q', 'b_w_out', 'f_w_up1', 'f_w_down1']
WEIGHT_WAVES = {'first': ['a_w_in'], 'ffn0': ['a_w_out', 'f_w_up0', 'f_w_down0'], 'late': LATE}
GRAD_WAVES = {'late': LATE, 'layer0': ['f_w_up0', 'f_w_down0', 'a_w_out'], 'last': ['a_w_in']}


def _piece(arrays, p):
    leaf, layer, _ = PIECES[p]
    return arrays[leaf] if layer is None else arrays[leaf][layer]


class Packer:
    def __init__(self, pieces, shard):
        self.pieces = pieces
        self.shapes = [_piece(shard, p).shape for p in pieces]
        self.sizes = [math.prod(s) // (2 * LANES) for s in self.shapes]
        self.fill = -sum(self.sizes) % 16
        self.rows = sum(self.sizes) + self.fill

    def my_half(self, shard, half):
        both = jnp.concatenate([_piece(shard, p).astype(MM_DTYPE).reshape(2, -1, LANES) for p in self.pieces], axis=1)
        return jnp.pad(lax.dynamic_index_in_dim(both, half, axis=0, keepdims=False), ((0, self.fill), (0, 0)))

    def full_weights(self, gathered):
        g = gathered.reshape(4, 2, self.rows, LANES)
        out, off = {}, 0
        for p, shp, sz in zip(self.pieces, self.shapes, self.sizes):
            out[p] = _full_from_shards(g[:, :, off:off + sz].reshape((4,) + shp), PIECES[p][2])
            off += sz
        return out

    def grad_slots(self, grads):
        parts = [_shards_from_full(grads[p], PIECES[p][2]).reshape(4, 2, -1, LANES).astype(GRAD_WIRE_DTYPE)
                 for p in self.pieces]
        parts.append(jnp.zeros((4, 2, self.fill, LANES), GRAD_WIRE_DTYPE))
        return jnp.concatenate(parts, axis=2).reshape(8, self.rows, LANES)

    def shard_grads(self, both):
        out, off = {}, 0
        for p, shp, sz in zip(self.pieces, self.shapes, self.sizes):
            out[p] = both[:, off:off + sz].reshape(shp).astype(F32)
            off += sz
        return out


class Overlap:
    def __init__(self, shard, half):
        self.shard, self.half = shard, half
        self.weights = {w: Packer(p, shard) for w, p in WEIGHT_WAVES.items()}
        self.grads = {w: Packer(p, shard) for w, p in GRAD_WAVES.items()}
        self.shard_grads = {}

    def gather_ride(self, wave):
        mine = self.weights[wave].my_half(self.shard, self.half)
        return mine, Exchange(mine, GROUP_ALL, scatter=False)

    def gathered(self, wave, slots):
        return self.weights[wave].full_weights(slots)

    def scatter_ride(self, wave, grads):
        slots = self.grads[wave].grad_slots({p: grads.pop(p) for p in GRAD_WAVES[wave]})
        return slots, Exchange(slots, GROUP_ALL, scatter=True)

    def join_ride(self, wave, received):
        reduced = sum_slots(received, f"sum_grads_{wave}", GRAD_WIRE_DTYPE)
        return reduced, Exchange(reduced, GROUP_SIBLING, scatter=False)

    def joined(self, wave, both):
        self.shard_grads.update(self.grads[wave].shard_grads(both))


def _pad_rows(flat, mult):
    n = flat.shape[0]
    per = LANES * mult
    tot = -(-n // per) * per
    return jnp.pad(flat, (0, tot - n)).reshape(tot // LANES, LANES)


def _full_from_shards(sh, axis):
    shp = sh.shape[1:]
    return jnp.moveaxis(sh, 0, axis).reshape(shp[:axis] + (4 * shp[axis],) + shp[axis + 1:])


def _shards_from_full(full, axis):
    shp = full.shape
    return jnp.moveaxis(full.reshape(shp[:axis] + (4, shp[axis] // 4) + shp[axis + 1:]), axis, 0)


def _local_step(x, target, W, overlap=None):
    T = x.shape[0]
    W = dict(W)
    row = lambda a: a.reshape(1, -1).astype(F32)
    w_in = jnp.pad(W['a_w_in'][0], ((0, 0), (0, A_IN_PAD - A_IN)))
    bias128 = jnp.pad(row(W['a_b_if'][0]), ((0, 0), (0, 120)))
    hng = row(W['a_hnorm_g'][0])
    w_up = lambda l: _interleave(W[f'f_w_up{l}'])
    cw = [_interleave(W['f_conv_w'][l].astype(F32)) for l in range(2)]
    cb = [_interleave(row(W['f_conv_b'][l])) for l in range(2)]
    onehots = [(jnp.asarray(_group_bucket(g).reshape(-1, 1)) == jnp.arange(128)[None, :]).astype(F32)
               for g in range(N_GROUPS)]
    rb_t = jnp.pad(W['rel_bias'].astype(F32).T, ((0, 0), (0, 128 - REL_BUCKETS)))
    biases = [mm_nn(rb_t[g * B_HEADS:(g + 1) * B_HEADS], onehots[g].T, f"rel_bias_table_g{g}", exact=True)
              .reshape(B_HEADS, B_BLOCK, 2 * B_BLOCK) for g in range(N_GROUPS)]
    G = {}

    def ffn_fwd(xin, l, ride=None):
        xn, = rms_fwd(xin, [row(W['f_norm_g'][l])], f"ffn{l}_norm")
        u, act, *rode = ffn_up_act(xn, w_up(l), cw[l], cb[l], f"ffn{l}_up_act", ride)
        return mm_nn(act, W[f'f_w_down{l}'], f"ffn{l}_down", res=xin), (xn, u, act), rode

    def ffn_bwd(xin, saved, dout, l, ride=None):
        xn, u, act = saved
        dact = mm_nn(dout, W[f'f_w_down{l}'].T, f"ffn{l}_ddown")
        G[f'f_w_down{l}'] = mm_tn(act, dout, f"ffn{l}_gdown")
        du, gcw, gcb, *rode = conv_act_bwd(u, dact, cw[l], cb[l], f"ffn{l}_dact", ride)
        dxn = mm_nn(du, w_up(l).T, f"ffn{l}_dup")
        G[f'f_w_up{l}'] = _deinterleave(mm_tn(xn, du, f"ffn{l}_gup"))
        dxin, (gn,) = rms_bwd(xin, dout, [(dxn, row(W['f_norm_g'][l]))], f"ffn{l}_dnorm")
        return dxin, _deinterleave(gcw), _deinterleave(gcb), gn, rode

    xn_a, = rms_fwd(x, [row(W['a_norm_g'][0])], "a_norm")
    z = mm_nn(xn_a, w_in, "a_in")
    gcol, grow = gate_prep(z, bias128)
    hg, Cs, ns, ms, *rode = mlstm_fwd(z, gcol, grow, hng, overlap.gather_ride('ffn0') if overlap else None)
    if overlap:
        W.update(overlap.gathered('ffn0', rode[0]))
    x1 = mm_nn(hg, W['a_w_out'][0], "a_out", res=x)
    x2, ffn0, rode = ffn_fwd(x1, 0, overlap.gather_ride('late') if overlap else None)
    if overlap:
        W.update(overlap.gathered('late', rode[0]))
    xn_kv, xn_b = rms_fwd(x2, [row(W['kv_norm_g']), row(W['b_norm_g'][0])], "b_norms")
    gcols = lambda w, c: w[:, c * 1024:(c + 1) * 1024]
    qv = [mm_view(xn_b, gcols(W['b_w_q'][0], g), f"q_proj_g{g}", DILATIONS[g]) for g in range(N_GROUPS)]
    kvw = [mm_view(xn_kv, gcols(W['w_kv'], g), f"k_proj_g{g}", DILATIONS[g]) for g in range(N_GROUPS)]
    vvw = [mm_view(xn_kv, gcols(W['w_kv'], 3 + g), f"v_proj_g{g}", DILATIONS[g]) for g in range(N_GROUPS)]
    os_, lses = zip(*[attn_fwd(qv[g], kvw[g], vvw[g], biases[g], g) for g in range(N_GROUPS)])
    att, att_f, *lse_v = attn_merge(os_, lses)
    x3 = mm_nn(att, W['b_w_out'][0], "b_out", res=x2)
    x4, ffn1, _ = ffn_fwd(x3, 1)
    dx4, g_final, loss = loss_head(x4, target, row(W['final_norm_g']))
    G['final_norm_g'] = g_final.reshape(-1)

    dx3, gcw1, gcb1, gn1, _ = ffn_bwd(x3, ffn1, dx4, 1)
    datt = mm_nn(dx3, W['b_w_out'][0].T, "b_dout")
    G['b_w_out'] = mm_tn(att, dx3, "b_gout")[None]
    prep = attn_prep(datt, att_f)
    do_v, dl_v = prep[:3], prep[3:]
    parts = [attn_bwd(qv[g], kvw[g], vvw[g], biases[g], do_v[g], lse_v[g], dl_v[g], g) for g in range(N_GROUPS)]
    dq_all, dkv = attn_combine(parts)
    grb = []
    for g in range(N_GROUPS):
        gb = mm_nn(parts[g][5].reshape(B_HEADS, -1), onehots[g], f"rel_bias_g{g}", exact=True)
        grb.append(gb[:, :REL_BUCKETS].T)
    G['rel_bias'] = jnp.concatenate(grb, axis=1)
    dxn_b = mm_nn(dq_all, W['b_w_q'][0].T, "q_dproj")
    G['b_w_q'] = mm_tn(xn_b, dq_all, "q_gproj")[None]
    dxn_kv = mm_nn(dkv, W['w_kv'].T, "kv_dproj")
    G['w_kv'] = mm_tn(xn_kv, dkv, "kv_gproj")
    dx2, (g_kvn, g_bn) = rms_bwd(x2, dx3, [(dxn_kv, row(W['kv_norm_g'])), (dxn_b, row(W['b_norm_g'][0]))],
                                 "b_dnorms")
    G['kv_norm_g'] = g_kvn.reshape(-1)
    G['b_norm_g'] = g_bn
    dx1, gcw0, gcb0, gn0, late_slots = ffn_bwd(x1, ffn0, dx2, 0, overlap.scatter_ride('late', G) if overlap else None)
    G['f_conv_w'] = jnp.stack([gcw0, gcw1])
    G['f_conv_b'] = jnp.concatenate([gcb0, gcb1], axis=0)
    G['f_norm_g'] = jnp.concatenate([gn0, gn1], axis=0)
    dhg = mm_nn(dx1, W['a_w_out'][0].T, "a_dout")
    G['a_w_out'] = mm_tn(hg, dx1, "a_gout")[None]
    dz, g_hn, g_bif, *layer0_slots = mlstm_bwd(z, gcol, grow, hng, bias128, Cs, ns, ms, dhg,
                                               overlap.scatter_ride('layer0', G) if overlap else None)
    G['a_hnorm_g'] = g_hn.reshape(1, A_HEADS, A_V)
    G['a_b_if'] = g_bif[:, :2 * A_HEADS]
    if overlap:
        dxn_a, both = mm_nn(dz, w_in.T, "a_din", ride=overlap.join_ride('late', late_slots[0]))
        overlap.joined('late', both)
        g_in, both = mm_tn(xn_a, dz, "a_gin", ride=overlap.join_ride('layer0', layer0_slots[0]))
        overlap.joined('layer0', both)
    else:
        dxn_a = mm_nn(dz, w_in.T, "a_din")
        g_in = mm_tn(xn_a, dz, "a_gin")
    G['a_w_in'] = g_in[:, :A_IN][None]
    grad_x, (g_an,) = rms_bwd(x, dx1, [(dxn_a, row(W['a_norm_g'][0]))], "a_dnorm")
    G['a_norm_g'] = g_an
    return loss, grad_x, G


def kernel(x, a_norm_g, a_w_in, a_b_if, a_hnorm_g, a_w_out, kv_norm_g, w_kv, b_norm_g, b_w_q, b_w_out, rel_bias, f_norm_g, f_w_up, f_conv_w, f_conv_b, f_w_down, final_norm_g, loss_target, m_a_norm_g, m_a_w_in, m_a_b_if, m_a_hnorm_g, m_a_w_out, m_kv_norm_g, m_w_kv, m_b_norm_g, m_b_w_q, m_b_w_out, m_rel_bias, m_f_norm_g, m_f_w_up, m_f_conv_w, m_f_conv_b, m_f_w_down, m_final_norm_g, v_a_norm_g, v_a_w_in, v_a_b_if, v_a_hnorm_g, v_a_w_out, v_kv_norm_g, v_w_kv, v_b_norm_g, v_b_w_q, v_b_w_out, v_rel_bias, v_f_norm_g, v_f_w_up, v_f_conv_w, v_f_conv_b, v_f_w_down, v_final_norm_g):
    given = dict(locals())
    shard = {n: given[n] for n in WEIGHTS}
    mom = {n: given["m_" + n] for n in WEIGHTS}
    var = {n: given["v_" + n] for n in WEIGHTS}
    cx, cy, cc = _me()
    chip = 2 * cx + cy

    overlap = Overlap(shard, cc)
    mine, gather = overlap.gather_ride('first')
    W = overlap.gathered('first', gather(mine, "gather_weights"))
    sharded_small = [n for n in SMALL if SHARD_AXIS[n] is not None]
    ssz = [shard[n].size for n in sharded_small]
    sflat = jnp.concatenate([shard[n].reshape(-1) for n in sharded_small])
    sg = group_gather(_pad_rows(sflat, 8), "gather_small", GROUP_CHIPS).reshape(4, -1)
    off = 0
    for n, sz in zip(sharded_small, ssz):
        W[n] = _full_from_shards(sg[:, off:off + sz].reshape((4,) + shard[n].shape), SHARD_AXIS[n])
        off += sz
    for n in SMALL:
        if SHARD_AXIS[n] is None:
            W[n] = shard[n]

    loss_row, grad_x, G = _local_step(x[0], loss_target[0], W, overlap)

    slots, scatter = overlap.scatter_ride('last', G)
    reduced, join = overlap.join_ride('last', scatter(slots, "scatter_grads"))
    overlap.joined('last', join(reduced, "join_halves"))
    by_piece = overlap.shard_grads
    gsh = {}
    for n in BIG:
        layers = [p for p in PIECES if PIECES[p][0] == n]
        gsh[n] = by_piece[n] if layers == [n] else jnp.stack([by_piece[p] for p in layers])
    small_parts = [loss_row[0, 0:1]] + [G[n].reshape(-1) for n in SMALL]
    small_sz = [p.shape[0] for p in small_parts]
    small = sum_slots(group_gather(_pad_rows(jnp.concatenate(small_parts), 8), "gather_small_grads", GROUP_ALL),
                      "sum_small_grads").reshape(-1)
    loss = small[0]
    off = 1
    for n, sz in zip(SMALL, small_sz[1:]):
        full = small[off:off + sz].reshape(W[n].shape)
        off += sz
        if SHARD_AXIS[n] is None:
            gsh[n] = full
        else:
            gsh[n] = lax.dynamic_index_in_dim(_shards_from_full(full, SHARD_AXIS[n]), chip, 0, keepdims=False)

    delta, new_m, new_v = {}, {}, {}
    for n in WEIGHTS:
        shp = shard[n].shape
        two = lambda a: a.reshape(-1, shp[-1])
        d, nm, nv = adamw(two(shard[n]), two(gsh[n]), two(mom[n]), two(var[n]), f"adamw_{n}")
        delta[n], new_m[n], new_v[n] = d.reshape(shp), nm.reshape(shp), nv.reshape(shp)
    return (loss, grad_x[None], *[gsh[n] for n in WEIGHTS], *[delta[n] for n in WEIGHTS],
            *[new_m[n] for n in WEIGHTS], *[new_v[n] for n in WEIGHTS])
```
